```python
import jax, jax.numpy as jnp
from jax import lax
import numpy as np

D_MODEL = 2048
BATCH = 8
SEQ = 2048
DEPTH = 1

N_META = 16
CHUNK = 64
MIX_WIDTH = D_MODEL
GLA_WIDTH = MIX_WIDTH // 2
POOL_WIDTH = MIX_WIDTH - GLA_WIDTH
GLA_HEADS = 4
GLA_DV = GLA_WIDTH // GLA_HEADS
GLA_DK = GLA_DV // 2
GLA_KW = GLA_HEADS * GLA_DK
GATE_RANK = 16
GATE_TAU = 16.0
POOL_WINDOWS = (2, 4, 8, 16)
POOL_GROUPS = len(POOL_WINDOWS)
POOL_GC = POOL_WIDTH // POOL_GROUPS
D_FF = 4 * D_MODEL
EPS = 1e-6
SPLIT_POINTS = (
    GLA_KW,
    2 * GLA_KW,
    2 * GLA_KW + GLA_WIDTH,
    2 * GLA_KW + 2 * GLA_WIDTH,
    2 * GLA_KW + 2 * GLA_WIDTH + GATE_RANK,
)
D_IN = 2 * GLA_KW + 2 * GLA_WIDTH + GATE_RANK + POOL_WIDTH

kernel_name = "hybrid_gla_multiscale_pool_meta"


def rmsnorm(x, w):
    xf = x.astype(jnp.float32)
    y = xf * lax.rsqrt(jnp.mean(xf * xf, axis=-1, keepdims=True) + EPS)
    return (y * w.astype(jnp.float32)).astype(x.dtype)


def gla_chunked(q, k, v, logg):
    B, T, H, DK = q.shape
    DV = v.shape[-1]
    N = T // CHUNK

    def chunks(a):
        return a.reshape(B, N, CHUNK, H, a.shape[-1]).transpose(0, 3, 1, 2, 4)

    q, k, v, logg = chunks(q), chunks(k), chunks(v), chunks(logg)
    G = jnp.cumsum(logg, axis=3)
    G_last = G[:, :, :, -1:, :]
    q_dec = q * jnp.exp(G)
    k_inv = k * jnp.exp(-G)
    causal = jnp.tril(jnp.ones((CHUNK, CHUNK), dtype=bool))
    scores = jnp.einsum('bhncd,bhnsd->bhncs', q_dec, k_inv)
    scores = jnp.where(causal, scores, 0.0)
    o_intra = jnp.einsum('bhncs,bhnsv->bhncv', scores, v)
    k_to_end = k * jnp.exp(G_last - G)
    dS = jnp.einsum('bhncd,bhncv->bhndv', k_to_end, v)
    decay = jnp.exp(G_last[:, :, :, 0, :])

    def step(S, inp):
        dec, ds = inp
        return dec[..., None] * S + ds, S

    S0 = jnp.zeros((B, H, DK, DV), jnp.float32)
    _, S_prev = lax.scan(step, S0, (decay.transpose(2, 0, 1, 3), dS.transpose(2, 0, 1, 3, 4)))
    S_prev = S_prev.transpose(1, 2, 0, 3, 4)
    o_inter = jnp.einsum('bhncd,bhndv->bhncv', q_dec, S_prev)
    o = o_intra + o_inter
    return o.transpose(0, 2, 3, 1, 4).reshape(B, T, H, DV)


def multiscale_pool(pu, pool_w, pool_scale):
    B, L, _ = pu.shape
    xg = pu.astype(jnp.float32).reshape(B, L, POOL_GROUPS, POOL_GC)
    cs = jnp.pad(jnp.cumsum(xg, axis=1), ((0, 0), (1, 0), (0, 0), (0, 0)))
    t = jnp.arange(L)
    win = jnp.array(POOL_WINDOWS, dtype=jnp.int32)
    lo = jnp.maximum(t[:, None] + 1 - win[None, :], 0)
    g_idx = jnp.arange(POOL_GROUPS)[None, :]
    window_sum = cs[:, 1:] - cs[:, lo, g_idx]
    count = (t[:, None] + 1 - lo).astype(jnp.float32)[None, :, :, None]
    y = window_sum / count - xg
    y = jnp.einsum('blgc,gcd->blgd', y, pool_w.astype(jnp.float32))
    return y.reshape(B, L, POOL_WIDTH) * pool_scale.astype(jnp.float32)


def hybrid_layer(h, norm1_w, w_in, gate_w2, gate_b, gla_norm_w, pool_w, pool_scale,
                 w_out, norm2_w, mlp_w1, mlp_w2):
    B, L, _ = h.shape
    u = rmsnorm(h, norm1_w)
    proj = u @ w_in
    q, k, v, r, glr, pu = jnp.split(proj, SPLIT_POINTS, axis=-1)

    g_raw = (glr @ gate_w2 + gate_b).astype(jnp.float32)
    logg = jax.nn.log_sigmoid(g_raw) / GATE_TAU
    pad = (-N_META) % CHUNK

    def heads(a, d):
        a = a.astype(jnp.float32).reshape(B, L, GLA_HEADS, d)
        return jnp.pad(a, ((0, 0), (pad, 0), (0, 0), (0, 0)))

    o = gla_chunked(heads(q, GLA_DK) * (GLA_DK ** -0.5), heads(k, GLA_DK),
                    heads(v, GLA_DV), heads(logg, GLA_DK))[:, pad:]
    o = rmsnorm(o, gla_norm_w)
    gate_out = jax.nn.silu(r.astype(jnp.float32)).reshape(B, L, GLA_HEADS, GLA_DV)
    o_gla = (o * gate_out).reshape(B, L, GLA_WIDTH)

    o_pool = multiscale_pool(pu, pool_w, pool_scale)

    mixed = jnp.concatenate([o_gla, o_pool], axis=-1).astype(h.dtype)
    h = h + mixed @ w_out

    z = rmsnorm(h, norm2_w) @ mlp_w1
    h = h + jnp.square(jax.nn.relu(z)) @ mlp_w2
    return h


def _fwd_setup_inputs(seed: int = 0) -> dict:
    key = jax.random.key(seed)
    ks = jax.random.split(key, 16)
    f32 = jnp.float32
    nrm = lambda k, shape, s: jax.random.normal(k, shape, f32) * s
    return {
        "x": nrm(ks[0], (BATCH, SEQ, D_MODEL), 1.0),
        "meta_tokens": nrm(ks[1], (N_META, D_MODEL), 1.0),
        "norm1_w": 1.0 + nrm(ks[2], (DEPTH, D_MODEL), 0.02),
        "w_in": nrm(ks[3], (DEPTH, D_MODEL, D_IN), D_MODEL ** -0.5),
        "gate_w2": nrm(ks[4], (DEPTH, GATE_RANK, GLA_KW), GATE_RANK ** -0.5),
        "gate_b": nrm(ks[5], (DEPTH, GLA_KW), 0.1),
        "gla_norm_w": 1.0 + nrm(ks[6], (DEPTH, GLA_DV), 0.02),
        "pool_w": nrm(ks[7], (DEPTH, POOL_GROUPS, POOL_GC, POOL_GC), POOL_GC ** -0.5),
        "pool_scale": 1.0 + nrm(ks[8], (DEPTH, POOL_WIDTH), 0.1),
        "w_out": nrm(ks[9], (DEPTH, MIX_WIDTH, D_MODEL), MIX_WIDTH ** -0.5),
        "norm2_w": 1.0 + nrm(ks[10], (DEPTH, D_MODEL), 0.02),
        "mlp_w1": nrm(ks[11], (DEPTH, D_MODEL, D_FF), D_MODEL ** -0.5),
        "mlp_w2": nrm(ks[12], (DEPTH, D_FF, D_MODEL), D_FF ** -0.5),
        "final_norm_w": 1.0 + nrm(ks[13], (D_MODEL,), 0.02),
    }


def _fwd_reference(x, meta_tokens, norm1_w, w_in, gate_w2, gate_b, gla_norm_w, pool_w,
              pool_scale, w_out, norm2_w, mlp_w1, mlp_w2, final_norm_w):
    B = x.shape[0]
    meta = jnp.broadcast_to(meta_tokens[None].astype(x.dtype), (B, N_META, D_MODEL))
    h = jnp.concatenate([meta, x], axis=1)
    for i in range(DEPTH):
        h = hybrid_layer(h, norm1_w[i], w_in[i], gate_w2[i], gate_b[i], gla_norm_w[i],
                         pool_w[i], pool_scale[i], w_out[i], norm2_w[i], mlp_w1[i], mlp_w2[i])
    h = rmsnorm(h, final_norm_w)
    return h[:, N_META:]


import jax as _jax
import jax.numpy as _jnp

TWIN_FORMAT = 'train_step'
FWD_PARAMS = ['x', 'meta_tokens', 'norm1_w', 'w_in', 'gate_w2', 'gate_b', 'gla_norm_w', 'pool_w', 'pool_scale', 'w_out', 'norm2_w', 'mlp_w1', 'mlp_w2', 'final_norm_w']
TWIN_WEIGHTS = ['meta_tokens', 'norm1_w', 'w_in', 'gate_w2', 'gate_b', 'gla_norm_w', 'pool_w', 'pool_scale', 'w_out', 'norm2_w', 'mlp_w1', 'mlp_w2', 'final_norm_w']
TWIN_DIFF_INPUT = 'x'
TWIN_INPUTS = ['x', 'meta_tokens', 'norm1_w', 'w_in', 'gate_w2', 'gate_b', 'gla_norm_w', 'pool_w', 'pool_scale', 'w_out', 'norm2_w', 'mlp_w1', 'mlp_w2', 'final_norm_w', 'loss_target', 'm_meta_tokens', 'm_norm1_w', 'm_w_in', 'm_gate_w2', 'm_gate_b', 'm_gla_norm_w', 'm_pool_w', 'm_pool_scale', 'm_w_out', 'm_norm2_w', 'm_mlp_w1', 'm_mlp_w2', 'm_final_norm_w', 'v_meta_tokens', 'v_norm1_w', 'v_w_in', 'v_gate_w2', 'v_gate_b', 'v_gla_norm_w', 'v_pool_w', 'v_pool_scale', 'v_w_out', 'v_norm2_w', 'v_mlp_w1', 'v_mlp_w2', 'v_final_norm_w']
TWIN_OUTPUTS = ['loss', 'grad_x', 'grad_meta_tokens', 'grad_norm1_w', 'grad_w_in', 'grad_gate_w2', 'grad_gate_b', 'grad_gla_norm_w', 'grad_pool_w', 'grad_pool_scale', 'grad_w_out', 'grad_norm2_w', 'grad_mlp_w1', 'grad_mlp_w2', 'grad_final_norm_w', 'delta_meta_tokens', 'delta_norm1_w', 'delta_w_in', 'delta_gate_w2', 'delta_gate_b', 'delta_gla_norm_w', 'delta_pool_w', 'delta_pool_scale', 'delta_w_out', 'delta_norm2_w', 'delta_mlp_w1', 'delta_mlp_w2', 'delta_final_norm_w', 'new_m_meta_tokens', 'new_m_norm1_w', 'new_m_w_in', 'new_m_gate_w2', 'new_m_gate_b', 'new_m_gla_norm_w', 'new_m_pool_w', 'new_m_pool_scale', 'new_m_w_out', 'new_m_norm2_w', 'new_m_mlp_w1', 'new_m_mlp_w2', 'new_m_final_norm_w', 'new_v_meta_tokens', 'new_v_norm1_w', 'new_v_w_in', 'new_v_gate_w2', 'new_v_gate_b', 'new_v_gla_norm_w', 'new_v_pool_w', 'new_v_pool_scale', 'new_v_w_out', 'new_v_norm2_w', 'new_v_mlp_w1', 'new_v_mlp_w2', 'new_v_final_norm_w']
TWIN_LEAF_KINDS = {'loss': 'loss', 'grad_x': 'grad_x', 'grad_meta_tokens': 'grad_w', 'grad_norm1_w': 'grad_w', 'grad_w_in': 'grad_w', 'grad_gate_w2': 'grad_w', 'grad_gate_b': 'grad_w', 'grad_gla_norm_w': 'grad_w', 'grad_pool_w': 'grad_w', 'grad_pool_scale': 'grad_w', 'grad_w_out': 'grad_w', 'grad_norm2_w': 'grad_w', 'grad_mlp_w1': 'grad_w', 'grad_mlp_w2': 'grad_w', 'grad_final_norm_w': 'grad_w', 'delta_meta_tokens': 'delta_w', 'delta_norm1_w': 'delta_w', 'delta_w_in': 'delta_w', 'delta_gate_w2': 'delta_w', 'delta_gate_b': 'delta_w', 'delta_gla_norm_w': 'delta_w', 'delta_pool_w': 'delta_w', 'delta_pool_scale': 'delta_w', 'delta_w_out': 'delta_w', 'delta_norm2_w': 'delta_w', 'delta_mlp_w1': 'delta_w', 'delta_mlp_w2': 'delta_w', 'delta_final_norm_w': 'delta_w', 'new_m_meta_tokens': 'new_m', 'new_m_norm1_w': 'new_m', 'new_m_w_in': 'new_m', 'new_m_gate_w2': 'new_m', 'new_m_gate_b': 'new_m', 'new_m_gla_norm_w': 'new_m', 'new_m_pool_w': 'new_m', 'new_m_pool_scale': 'new_m', 'new_m_w_out': 'new_m', 'new_m_norm2_w': 'new_m', 'new_m_mlp_w1': 'new_m', 'new_m_mlp_w2': 'new_m', 'new_m_final_norm_w': 'new_m', 'new_v_meta_tokens': 'new_v', 'new_v_norm1_w': 'new_v', 'new_v_w_in': 'new_v', 'new_v_gate_w2': 'new_v', 'new_v_gate_b': 'new_v', 'new_v_gla_norm_w': 'new_v', 'new_v_pool_w': 'new_v', 'new_v_pool_scale': 'new_v', 'new_v_w_out': 'new_v', 'new_v_norm2_w': 'new_v', 'new_v_mlp_w1': 'new_v', 'new_v_mlp_w2': 'new_v', 'new_v_final_norm_w': 'new_v'}


def _forward(args):
    return _fwd_reference(*[args[k] for k in FWD_PARAMS])


def _output_shape():
    out = _jax.eval_shape(lambda: _forward(_fwd_setup_inputs(0)))
    return out.shape, out.dtype

N_MICROBATCH = 1
ADAM_LR = 0.001
ADAM_B1 = 0.9
ADAM_B2 = 0.999
ADAM_EPS = 1e-08
ADAM_WD = 0.01
ADAM_STEP = 10
PER_EXAMPLE_BATCH_AXIS = {'x': 0, 'loss_target': 0}
SHARED_INPUTS = []
_WEIGHT_DTYPES = {'meta_tokens': _jnp.float32, 'norm1_w': _jnp.float32, 'w_in': _jnp.float32, 'gate_w2': _jnp.float32, 'gate_b': _jnp.float32, 'gla_norm_w': _jnp.float32, 'pool_w': _jnp.float32, 'pool_scale': _jnp.float32, 'w_out': _jnp.float32, 'norm2_w': _jnp.float32, 'mlp_w1': _jnp.float32, 'mlp_w2': _jnp.float32, 'final_norm_w': _jnp.float32}
MOMENT_SCALE = {'meta_tokens': 1.563896e-03, 'norm1_w': 6.016591e-02, 'w_in': 4.076847e-02, 'gate_w2': 5.144766e-03, 'gate_b': 2.044415e-02, 'gla_norm_w': 7.120531e-02, 'pool_w': 4.741558e-02, 'pool_scale': 4.834787e-02, 'w_out': 4.070999e-02, 'norm2_w': 5.164442e-02, 'mlp_w1': 2.534259e-02, 'mlp_w2': 4.800771e-02, 'final_norm_w': 8.058278e+00}


def _to_microbatches(a, axis):
    t = _jnp.moveaxis(a, axis, 0)
    t = t.reshape((N_MICROBATCH, t.shape[0] // N_MICROBATCH) + t.shape[1:])
    return _jnp.moveaxis(t, 1, axis + 1)


def setup_inputs(seed: int = 0) -> dict:
    inp = _fwd_setup_inputs(seed)
    key = _jax.random.fold_in(_jax.random.key(seed), 7919)
    shape, _ = _output_shape()
    out = dict(inp)
    out["loss_target"] = _jax.random.normal(_jax.random.fold_in(key, 0), shape, _jnp.float32)
    for i, name in enumerate(TWIN_WEIGHTS):
        w = inp[name].astype(_jnp.float32)
        if MOMENT_SCALE is None:
            s = _jnp.sqrt(_jnp.mean(_jnp.square(w)) + 1e-30)
        else:
            s = MOMENT_SCALE[name]
        km, kv = _jax.random.split(_jax.random.fold_in(key, i + 1))
        out[name] = w
        out["m_" + name] = s * _jax.random.normal(km, w.shape, _jnp.float32)
        out["v_" + name] = (s * s) * _jax.random.uniform(kv, w.shape, _jnp.float32, 0.5, 1.5)
    if N_MICROBATCH > 1:
        for name, axis in PER_EXAMPLE_BATCH_AXIS.items():
            out[name] = _to_microbatches(out[name], axis)
    return {'x': out['x'], 'meta_tokens': out['meta_tokens'], 'norm1_w': out['norm1_w'], 'w_in': out['w_in'], 'gate_w2': out['gate_w2'], 'gate_b': out['gate_b'], 'gla_norm_w': out['gla_norm_w'], 'pool_w': out['pool_w'], 'pool_scale': out['pool_scale'], 'w_out': out['w_out'], 'norm2_w': out['norm2_w'], 'mlp_w1': out['mlp_w1'], 'mlp_w2': out['mlp_w2'], 'final_norm_w': out['final_norm_w'], 'loss_target': out['loss_target'], 'm_meta_tokens': out['m_meta_tokens'], 'm_norm1_w': out['m_norm1_w'], 'm_w_in': out['m_w_in'], 'm_gate_w2': out['m_gate_w2'], 'm_gate_b': out['m_gate_b'], 'm_gla_norm_w': out['m_gla_norm_w'], 'm_pool_w': out['m_pool_w'], 'm_pool_scale': out['m_pool_scale'], 'm_w_out': out['m_w_out'], 'm_norm2_w': out['m_norm2_w'], 'm_mlp_w1': out['m_mlp_w1'], 'm_mlp_w2': out['m_mlp_w2'], 'm_final_norm_w': out['m_final_norm_w'], 'v_meta_tokens': out['v_meta_tokens'], 'v_norm1_w': out['v_norm1_w'], 'v_w_in': out['v_w_in'], 'v_gate_w2': out['v_gate_w2'], 'v_gate_b': out['v_gate_b'], 'v_gla_norm_w': out['v_gla_norm_w'], 'v_pool_w': out['v_pool_w'], 'v_pool_scale': out['v_pool_scale'], 'v_w_out': out['v_w_out'], 'v_norm2_w': out['v_norm2_w'], 'v_mlp_w1': out['v_mlp_w1'], 'v_mlp_w2': out['v_mlp_w2'], 'v_final_norm_w': out['v_final_norm_w']}


def _loss(weights, diff, rest, loss_target):
    with _jax.named_scope("forward"):
        args = {**rest, TWIN_DIFF_INPUT: diff, **{k: w.astype(_WEIGHT_DTYPES[k]) for k, w in weights.items()}}
        y = _forward(args)
    with _jax.named_scope("loss_head"):
        err = _jnp.square(y.astype(_jnp.float32) - loss_target)
        return 0.5 * _jnp.sum(_jnp.mean(err, axis=-1)) if err.ndim else 0.5 * err


def _adamw(w, g, m, v):
    m = ADAM_B1 * m + (1.0 - ADAM_B1) * g
    v = ADAM_B2 * v + (1.0 - ADAM_B2) * _jnp.square(g)
    m_hat = m / (1.0 - ADAM_B1 ** ADAM_STEP)
    v_hat = v / (1.0 - ADAM_B2 ** ADAM_STEP)
    delta = -ADAM_LR * (m_hat / (_jnp.sqrt(v_hat) + ADAM_EPS) + ADAM_WD * w)
    return delta, m, v


def reference(x, meta_tokens, norm1_w, w_in, gate_w2, gate_b, gla_norm_w, pool_w, pool_scale, w_out, norm2_w, mlp_w1, mlp_w2, final_norm_w, loss_target, m_meta_tokens, m_norm1_w, m_w_in, m_gate_w2, m_gate_b, m_gla_norm_w, m_pool_w, m_pool_scale, m_w_out, m_norm2_w, m_mlp_w1, m_mlp_w2, m_final_norm_w, v_meta_tokens, v_norm1_w, v_w_in, v_gate_w2, v_gate_b, v_gla_norm_w, v_pool_w, v_pool_scale, v_w_out, v_norm2_w, v_mlp_w1, v_mlp_w2, v_final_norm_w):
    given = dict(x=x, meta_tokens=meta_tokens, norm1_w=norm1_w, w_in=w_in, gate_w2=gate_w2, gate_b=gate_b, gla_norm_w=gla_norm_w, pool_w=pool_w, pool_scale=pool_scale, w_out=w_out, norm2_w=norm2_w, mlp_w1=mlp_w1, mlp_w2=mlp_w2, final_norm_w=final_norm_w, loss_target=loss_target, m_meta_tokens=m_meta_tokens, m_norm1_w=m_norm1_w, m_w_in=m_w_in, m_gate_w2=m_gate_w2, m_gate_b=m_gate_b, m_gla_norm_w=m_gla_norm_w, m_pool_w=m_pool_w, m_pool_scale=m_pool_scale, m_w_out=m_w_out, m_norm2_w=m_norm2_w, m_mlp_w1=m_mlp_w1, m_mlp_w2=m_mlp_w2, m_final_norm_w=m_final_norm_w, v_meta_tokens=v_meta_tokens, v_norm1_w=v_norm1_w, v_w_in=v_w_in, v_gate_w2=v_gate_w2, v_gate_b=v_gate_b, v_gla_norm_w=v_gla_norm_w, v_pool_w=v_pool_w, v_pool_scale=v_pool_scale, v_w_out=v_w_out, v_norm2_w=v_norm2_w, v_mlp_w1=v_mlp_w1, v_mlp_w2=v_mlp_w2, v_final_norm_w=v_final_norm_w)
    weights = {n: given[n] for n in TWIN_WEIGHTS}
    shared = {n: given[n] for n in SHARED_INPUTS}
    per_example = {n: given[n] for n in ['x']}
    grad_fn = _jax.value_and_grad(_loss, argnums=(0, 1))

    def one_microbatch(ex, loss_target):
        ex = dict(ex)
        diff = ex.pop(TWIN_DIFF_INPUT)
        return grad_fn(weights, diff, {**shared, **ex}, loss_target)

    if N_MICROBATCH == 1:
        loss, (grad_w, grad_x) = one_microbatch(per_example, given["loss_target"])
    else:
        def body(carry, xs):
            loss_sum, grad_sum = carry
            l_k, (gw_k, gx_k) = one_microbatch(xs[0], xs[1])
            with _jax.named_scope("update"):
                return (loss_sum + l_k, _jax.tree.map(_jnp.add, grad_sum, gw_k)), gx_k

        init = (_jnp.zeros((), _jnp.float32), _jax.tree.map(_jnp.zeros_like, weights))
        (loss, grad_w), grad_x = _jax.lax.scan(body, init, (per_example, given["loss_target"]))
    with _jax.named_scope("update"):
        delta_w, new_m, new_v = {}, {}, {}
        for n in TWIN_WEIGHTS:
            delta_w[n], new_m[n], new_v[n] = _adamw(weights[n], grad_w[n], given["m_" + n], given["v_" + n])
    return (loss, grad_x, *[grad_w[n] for n in TWIN_WEIGHTS], *[delta_w[n] for n in TWIN_WEIGHTS],
            *[new_m[n] for n in TWIN_WEIGHTS], *[new_v[n] for n in TWIN_WEIGHTS])
```

```python
import functools

import jax
import jax.numpy as jnp
from jax import lax
from jax.experimental import pallas as pl
from jax.experimental.pallas import tpu as pltpu

F32, BF16 = jnp.float32, jnp.bfloat16
MESH = pl.DeviceIdType.MESH

NDEV = 8
D = 2048
SEQ = 2048
N_META = 16
CHUNK = 64
PAD = (-N_META) % CHUNK
ROW_X = PAD + N_META
LP = ROW_X + SEQ
NCH = LP // CHUNK
H = 4
DK = 128
DV = 256
KW = H * DK
GW = H * DV
PW = 1024
RANK = 16
TAU = 16.0
WINDOWS = (2, 4, 8, 16)
GC = 256
DFF = 4 * D
EPS = 1e-6
D_IN = 2 * KW + 2 * GW + RANK + PW
D_INP = 4224
GLR_BLK = (2 * KW + 2 * GW + PW) // 128
POOL_BLK = (2 * KW + 2 * GW) // GC
LR, B1, B2, AEPS, WD, STEP = 0.001, 0.9, 0.999, 1e-08, 0.01, 10
VMEM_LIMIT = 48 * 1024 * 1024
CPS = 3


def _params(sem=None):
    return pltpu.CompilerParams(dimension_semantics=sem, vmem_limit_bytes=VMEM_LIMIT)


def _sds(shape, dtype):
    return jax.ShapeDtypeStruct(shape, dtype)


def _me():
    return lax.axis_index("x"), lax.axis_index("y"), lax.axis_index("c")


def _peer(j):
    x, y, c = _me()
    return (x ^ ((j >> 2) & 1), y ^ ((j >> 1) & 1), c ^ (j & 1))


def _slot(dev):
    return 4 * dev[0] + 2 * dev[1] + dev[2]


def _all_gather_hbm(shards, name):
    n = len(shards)

    def body(*refs):
        src, dst = refs[:n], refs[n:2 * n]
        send_sems, recv_sems, local_sems = refs[2 * n:]
        me = _slot(_me())
        copies = []
        for i in range(n):
            loc = pltpu.make_async_copy(src[i], dst[i].at[me], local_sems.at[i])
            loc.start()
            copies.append(loc)
            for j in range(1, NDEV):
                cp = pltpu.make_async_remote_copy(
                    src_ref=src[i], dst_ref=dst[i].at[me], send_sem=send_sems.at[i, j - 1],
                    recv_sem=recv_sems.at[i, j - 1], device_id=_peer(j), device_id_type=MESH)
                cp.start()
                copies.append(cp)
        for cp in copies:
            cp.wait()

    any_spec = pl.BlockSpec(memory_space=pl.ANY)
    return pl.pallas_call(
        body, name=name,
        out_shape=[_sds((NDEV,) + s.shape, s.dtype) for s in shards],
        in_specs=[any_spec] * n, out_specs=[any_spec] * n,
        scratch_shapes=[pltpu.SemaphoreType.DMA((n, NDEV - 1)), pltpu.SemaphoreType.DMA((n, NDEV - 1)),
                        pltpu.SemaphoreType.DMA((n,))],
    )(*shards)


def _scatter_hbm(parts, name):
    n = len(parts)

    def body(*refs):
        src, dst = refs[:n], refs[n:2 * n]
        send_sems, recv_sems, local_sems = refs[2 * n:]
        me = _slot(_me())
        copies = []
        for i in range(n):
            loc = pltpu.make_async_copy(src[i].at[me], dst[i].at[me], local_sems.at[i])
            loc.start()
            copies.append(loc)
            for j in range(1, NDEV):
                to = _peer(j)
                cp = pltpu.make_async_remote_copy(
                    src_ref=src[i].at[_slot(to)], dst_ref=dst[i].at[me], send_sem=send_sems.at[i, j - 1],
                    recv_sem=recv_sems.at[i, j - 1], device_id=to, device_id_type=MESH)
                cp.start()
                copies.append(cp)
        for cp in copies:
            cp.wait()

    any_spec = pl.BlockSpec(memory_space=pl.ANY)
    return pl.pallas_call(
        body, name=name,
        out_shape=[_sds(p.shape, p.dtype) for p in parts],
        in_specs=[any_spec] * n, out_specs=[any_spec] * n,
        scratch_shapes=[pltpu.SemaphoreType.DMA((n, NDEV - 1)), pltpu.SemaphoreType.DMA((n, NDEV - 1)),
                        pltpu.SemaphoreType.DMA((n,))],
    )(*parts)


def _exchange_small(v, reduce, name):
    r, c = v.shape

    def body(v_ref, o_ref, land, send_sems, recv_sems):
        me = _slot(_me())
        copies = []
        for j in range(1, NDEV):
            cp = pltpu.make_async_remote_copy(
                src_ref=v_ref, dst_ref=land.at[me], send_sem=send_sems.at[j - 1],
                recv_sem=recv_sems.at[j - 1], device_id=_peer(j), device_id_type=MESH)
            cp.start()
            copies.append(cp)
        land[me] = v_ref[...]
        for cp in copies:
            cp.wait()
        if reduce:
            acc = land[0]
            for k in range(1, NDEV):
                acc = acc + land[k]
            o_ref[...] = acc
        else:
            o_ref[...] = land[...]

    vm = pl.BlockSpec(memory_space=pltpu.VMEM)
    return pl.pallas_call(
        body, name=name,
        out_shape=_sds((r, c) if reduce else (NDEV, r, c), F32),
        in_specs=[vm], out_specs=vm,
        scratch_shapes=[pltpu.VMEM((NDEV, r, c), F32), pltpu.SemaphoreType.DMA((NDEV - 1,)),
                        pltpu.SemaphoreType.DMA((NDEV - 1,))],
        compiler_params=_params(),
    )(v)


def _matmul(a, b, *, mode, tm, tn, tk, name, out_dtype=F32, epi=None, extra=None, b_slots=False, out_slots=False):
    slot_w = b.shape[-1] if b_slots else None
    if mode == "nn":
        M, K = a.shape
        N = NDEV * slot_w if b_slots else b.shape[1]
    elif mode == "tn":
        K, M = a.shape
        N = b.shape[1]
    else:
        M, K = a.shape
        N = b.shape[-2]
        if b_slots:
            assert K == NDEV * slot_w and tk == slot_w
    if mode == "nn" and b_slots:
        assert tn == slot_w
    if out_slots:
        assert tn * NDEV == N
    assert M % tm == 0 and N % tn == 0 and K % tk == 0, (name, M, N, K, tm, tn, tk)
    nk = K // tk
    dims = {"nn": ((1,), (0,)), "tn": ((0,), (0,)), "nt": ((1,), (1,))}[mode]

    if mode == "tn":
        a_spec = pl.BlockSpec((tk, tm), lambda i, j, k: (k, i))
    else:
        a_spec = pl.BlockSpec((tm, tk), lambda i, j, k: (i, k))
    if mode == "nt":
        b_spec = (pl.BlockSpec((None, tn, tk), lambda i, j, k: (k, j, 0)) if b_slots
                  else pl.BlockSpec((tn, tk), lambda i, j, k: (j, k)))
    else:
        b_spec = (pl.BlockSpec((None, tk, tn), lambda i, j, k: (j, k, 0)) if b_slots
                  else pl.BlockSpec((tk, tn), lambda i, j, k: (k, j)))
    tile = pl.BlockSpec((tm, tn), lambda i, j, k: (i, j))
    if out_slots:
        out_spec, out_shape = pl.BlockSpec((None, tm, tn), lambda i, j, k: (j, i, 0)), _sds((NDEV, M, tn), out_dtype)
    else:
        out_spec, out_shape = tile, _sds((M, N), out_dtype)
    ins, in_specs = [a, b], [a_spec, b_spec]
    if epi in ("add", "dz"):
        ins.append(extra)
        in_specs.append(tile)
    if epi == "relu2":
        out_specs, out_shapes = [tile, tile], [_sds((M, N), F32), _sds((M, N), BF16)]
    else:
        out_specs, out_shapes = out_spec, out_shape
    n_in = len(ins)

    def body(*refs):
        outs = refs[n_in:-1] if nk > 1 else refs[n_in:]

        def finish(p):
            if epi is None:
                outs[0][...] = p.astype(out_dtype)
            elif epi == "add":
                outs[0][...] = (p + refs[2][...]).astype(out_dtype)
            elif epi == "relu2":
                outs[0][...] = p
                rz = jnp.maximum(p, 0.0)
                outs[1][...] = (rz * rz).astype(BF16)
            else:
                outs[0][...] = (p * (2.0 * jnp.maximum(refs[2][...], 0.0))).astype(out_dtype)

        p = lax.dot_general(refs[0][...].astype(BF16), refs[1][...].astype(BF16), (dims, ((), ())),
                            preferred_element_type=F32)
        if nk == 1:
            finish(p)
            return
        acc = refs[-1]
        k = pl.program_id(2)

        @pl.when(k == 0)
        def _():
            acc[...] = p

        @pl.when(k > 0)
        def _():
            acc[...] += p

        @pl.when(k == nk - 1)
        def _():
            finish(acc[...])

    return pl.pallas_call(
        body, name=name, grid=(M // tm, N // tn, nk),
        in_specs=in_specs, out_specs=out_specs, out_shape=out_shapes,
        scratch_shapes=[pltpu.VMEM((tm, tn), F32)] if nk > 1 else [],
        compiler_params=_params(("parallel", "parallel", "arbitrary")),
    )(*ins)


ROWS = 352


def _rmsnorm_fwd(h, w, name):
    def body(h_ref, w_ref, u_ref):
        x = h_ref[...]
        rstd = lax.rsqrt(jnp.mean(x * x, axis=-1, keepdims=True) + EPS)
        u_ref[...] = (x * rstd * w_ref[...]).astype(BF16)

    row = pl.BlockSpec((ROWS, D), lambda i: (i, 0))
    return pl.pallas_call(
        body, name=name, grid=(LP // ROWS,), in_specs=[row, pl.BlockSpec((1, D), lambda i: (0, 0))],
        out_specs=row, out_shape=_sds((LP, D), BF16), compiler_params=_params(("parallel",)),
    )(h, w)


def _rmsnorm_bwd(h, w, du, dres, name):
    def body(h_ref, w_ref, du_ref, dres_ref, dh_ref, dhb_ref, gw_ref):
        x = h_ref[...]
        rstd = lax.rsqrt(jnp.mean(x * x, axis=-1, keepdims=True) + EPS)
        xhat = x * rstd
        dy = du_ref[...]
        dxh = dy * w_ref[...]
        dh = dres_ref[...] + rstd * (dxh - xhat * jnp.mean(dxh * xhat, axis=-1, keepdims=True))
        dh_ref[...] = dh
        dhb_ref[...] = dh.astype(BF16)
        part = jnp.sum(dy * xhat, axis=0, keepdims=True)

        @pl.when(pl.program_id(0) == 0)
        def _():
            gw_ref[...] = part

        @pl.when(pl.program_id(0) > 0)
        def _():
            gw_ref[...] += part

    row = pl.BlockSpec((ROWS, D), lambda i: (i, 0))
    vec = pl.BlockSpec((1, D), lambda i: (0, 0))
    return pl.pallas_call(
        body, name=name, grid=(LP // ROWS,), in_specs=[row, vec, row, row], out_specs=[row, row, vec],
        out_shape=[_sds((LP, D), F32), _sds((LP, D), BF16), _sds((1, D), F32)],
        compiler_params=_params(("arbitrary",)),
    )(h, w, du, dres)


def _loss_head(h2, wf, target):
    def body(h_ref, w_ref, t_ref, dh_ref, dhb_ref, sq_ref, gw_ref):
        i = pl.program_id(0)

        @pl.when(i == 0)
        def _():
            dh_ref[...] = jnp.zeros_like(dh_ref)
            dhb_ref[...] = jnp.zeros_like(dhb_ref)
            sq_ref[...] = jnp.zeros_like(sq_ref)
            gw_ref[...] = jnp.zeros_like(gw_ref)

        @pl.when(i > 0)
        def _():
            x = h_ref[...]
            rstd = lax.rsqrt(jnp.mean(x * x, axis=-1, keepdims=True) + EPS)
            xhat = x * rstd
            w = w_ref[...]
            err = xhat * w - t_ref[...]
            sq_ref[...] += jnp.sum(err * err, axis=0, keepdims=True)
            dy = err * (1.0 / D)
            gw_ref[...] += jnp.sum(dy * xhat, axis=0, keepdims=True)
            dxh = dy * w
            dh = rstd * (dxh - xhat * jnp.mean(dxh * xhat, axis=-1, keepdims=True))
            dh_ref[...] = dh
            dhb_ref[...] = dh.astype(BF16)

    row = pl.BlockSpec((CHUNK, D), lambda i: (i, 0))
    vec = pl.BlockSpec((1, D), lambda i: (0, 0))
    return pl.pallas_call(
        body, name="loss_head", grid=(NCH,),
        in_specs=[row, vec, pl.BlockSpec((CHUNK, D), lambda i: (jnp.maximum(i - 1, 0), 0))],
        out_specs=[row, row, vec, vec],
        out_shape=[_sds((LP, D), F32), _sds((LP, D), BF16), _sds((1, D), F32), _sds((1, D), F32)],
        compiler_params=_params(("arbitrary",)),
    )(h2, wf, target)


def _dot(a, b, dims):
    return lax.dot_general(a, b, (dims, ((), ())), preferred_element_type=F32)


NN, TN, NT = ((1,), (0,)), ((0,), (0,)), ((1,), (1,))


def _tri_sum(t, x):
    hi = x.astype(BF16)
    r1 = x - hi.astype(F32)
    mid = r1.astype(BF16)
    lo = (r1 - mid.astype(F32)).astype(BF16)
    return _dot(t, hi, NN) + _dot(t, mid, NN) + _dot(t, lo, NN)


def _gla_gates(glr_ref, gw2_ref, gb_ref, rows, row0):
    g_raw = _dot(glr_ref[rows, :].astype(BF16), gw2_ref[...], NN) + gb_ref[...]
    logsig = jnp.minimum(g_raw, 0.0) - jnp.log(1.0 + jnp.exp(-jnp.abs(g_raw)))
    rid = row0 + lax.broadcasted_iota(jnp.int32, g_raw.shape, 0)
    live = rid >= PAD
    return g_raw, jnp.where(live, logsig / TAU, 0.0), live


def _tri_masks():
    r = lax.broadcasted_iota(jnp.int32, (CHUNK, CHUNK), 0)
    c = lax.broadcasted_iota(jnp.int32, (CHUNK, CHUNK), 1)
    return r >= c


def _gla_specs(rev):
    n = NCH // CPS
    R = CPS * CHUNK
    st = (lambda s: n - 1 - s) if rev else (lambda s: s)
    return R, n, st, [
        pl.BlockSpec((R, KW), lambda s: (st(s), 0)),
        pl.BlockSpec((R, KW), lambda s: (st(s), 1)),
        pl.BlockSpec((R, GW), lambda s: (st(s), 1)),
        pl.BlockSpec((R, GW), lambda s: (st(s), 2)),
        pl.BlockSpec((R, 128), lambda s: (st(s), GLR_BLK)),
    ]


def _gla_fwd(proj, gw2p, gate_b, gnw):
    R, n, st, pspecs = _gla_specs(False)

    def body(q_ref, k_ref, v_ref, r_ref, glr_ref, gw2_ref, gb_ref, gnw_ref, og_ref, o_ref, st_ref, state):
        s = pl.program_id(0)

        @pl.when(s == 0)
        def _():
            state[...] = jnp.zeros_like(state)

        causal = _tri_masks()
        tri = causal.astype(BF16)
        for c in range(CPS):
            rows = slice(c * CHUNK, (c + 1) * CHUNK)
            _, logg, _ = _gla_gates(glr_ref, gw2_ref, gb_ref, rows, s * R + c * CHUNK)
            G = _tri_sum(tri, logg)
            g_last = G[CHUNK - 1:CHUNK, :]
            q_dec = (q_ref[rows, :] * (DK ** -0.5) * jnp.exp(G)).astype(BF16)
            kk = k_ref[rows, :]
            k_inv = (kk * jnp.exp(-G)).astype(BF16)
            k_end = (kk * jnp.exp(g_last - G)).astype(BF16)
            decay = jnp.exp(g_last)
            for h in range(H):
                lk = slice(h * DK, (h + 1) * DK)
                lv = slice(h * DV, (h + 1) * DV)
                v = v_ref[rows, lv].astype(BF16)
                S = state[h]
                st_ref[c, h] = S
                A = jnp.where(causal, _dot(q_dec[:, lk], k_inv[:, lk], NT), 0.0).astype(BF16)
                o = _dot(A, v, NN) + _dot(q_dec[:, lk], S.astype(BF16), NT)
                state[h] = decay[:, lk] * S + _dot(v, k_end[:, lk], TN)
                o_ref[rows, lv] = o
                on = o * lax.rsqrt(jnp.mean(o * o, axis=-1, keepdims=True) + EPS) * gnw_ref[...]
                rr = r_ref[rows, lv]
                og_ref[rows, lv] = (on * (rr * jax.nn.sigmoid(rr))).astype(BF16)

    full = lambda shape: pl.BlockSpec(shape, lambda s: (0,) * len(shape))
    return pl.pallas_call(
        body, name="gla_fwd", grid=(n,),
        in_specs=pspecs + [full((128, KW)), full((1, KW)), full((1, DV))],
        out_specs=[pl.BlockSpec((R, GW), lambda s: (s, 0)), pl.BlockSpec((R, GW), lambda s: (s, 0)),
                   pl.BlockSpec((CPS, H, DV, DK), lambda s: (s, 0, 0, 0))],
        out_shape=[_sds((LP, GW), BF16), _sds((LP, GW), F32), _sds((NCH, H, DV, DK), F32)],
        scratch_shapes=[pltpu.VMEM((H, DV, DK), F32)],
        compiler_params=_params(("arbitrary",)),
    )(proj, proj, proj, proj, proj, gw2p, gate_b, gnw)


def _gla_bwd(proj, dmixed, o_saved, st_saved, gw2p, gate_b, gnw):
    R, n, st, pspecs = _gla_specs(True)

    def body(q_ref, k_ref, v_ref, r_ref, glr_ref, dog_ref, o_ref, st_ref, gw2_ref, gb_ref, gnw_ref,
             dqkvr_ref, dglr_ref, ggn_ref, ggb_ref, ggw_ref, gstate):
        s = pl.program_id(0)

        @pl.when(s == 0)
        def _():
            gstate[...] = jnp.zeros_like(gstate)
            ggn_ref[...] = jnp.zeros_like(ggn_ref)
            ggb_ref[...] = jnp.zeros_like(ggb_ref)
            ggw_ref[...] = jnp.zeros_like(ggw_ref)

        causal = _tri_masks()
        tri = causal.astype(BF16)
        tri_up = (lax.broadcasted_iota(jnp.int32, (CHUNK, CHUNK), 0)
                  <= lax.broadcasted_iota(jnp.int32, (CHUNK, CHUNK), 1)).astype(BF16)
        gnw = gnw_ref[...]
        for c in reversed(range(CPS)):
            rows = slice(c * CHUNK, (c + 1) * CHUNK)
            g_raw, logg, live = _gla_gates(glr_ref, gw2_ref, gb_ref, rows, (n - 1 - s) * R + c * CHUNK)
            G = _tri_sum(tri, logg)
            g_last = G[CHUNK - 1:CHUNK, :]
            e_g, e_gi, e_end = jnp.exp(G), jnp.exp(-G), jnp.exp(g_last - G)
            q_dec = q_ref[rows, :] * (DK ** -0.5) * e_g
            kk = k_ref[rows, :]
            k_inv, k_end = kk * e_gi, kk * e_end
            q_dec_b, k_inv_b, k_end_b = q_dec.astype(BF16), k_inv.astype(BF16), k_end.astype(BF16)
            decay = jnp.exp(g_last)
            d_g, d_gl = [], []
            for h in range(H):
                lk = slice(h * DK, (h + 1) * DK)
                lv = slice(h * DV, (h + 1) * DV)
                o = o_ref[rows, lv]
                rr = r_ref[rows, lv]
                dog = dog_ref[rows, lv]
                rstd = lax.rsqrt(jnp.mean(o * o, axis=-1, keepdims=True) + EPS)
                ohat = o * rstd
                sr = jax.nn.sigmoid(rr)
                don = dog * (rr * sr)
                dqkvr_ref[rows, 2 * KW + GW + h * DV:2 * KW + GW + (h + 1) * DV] = (
                    dog * (ohat * gnw) * (sr * (1.0 + rr * (1.0 - sr)))).astype(BF16)
                ggn_ref[...] += jnp.sum(don * ohat, axis=0, keepdims=True)
                dohat = don * gnw
                do = (rstd * (dohat - ohat * jnp.mean(dohat * ohat, axis=-1, keepdims=True))).astype(BF16)
                v = v_ref[rows, lv].astype(BF16)
                S = st_ref[c, h]
                gS = gstate[h]
                S_b, gS_b = S.astype(BF16), gS.astype(BF16)
                qd, ki, ke = q_dec_b[:, lk], k_inv_b[:, lk], k_end_b[:, lk]
                A = jnp.where(causal, _dot(qd, ki, NT), 0.0).astype(BF16)
                dA = jnp.where(causal, _dot(do, v, NT), 0.0).astype(BF16)
                dv = _dot(A, do, TN) + _dot(ke, gS_b, NT)
                dq_dec = _dot(dA, ki, NN) + _dot(do, S_b, NN)
                dk_inv = _dot(dA, qd, TN)
                dk_end = _dot(v, gS_b, NN)
                d_decay = jnp.sum(gS * S, axis=0, keepdims=True)
                gstate[h] = decay[:, lk] * gS + _dot(do, qd, TN)
                dqkvr_ref[rows, lk] = (dq_dec * e_g[:, lk] * (DK ** -0.5)).astype(BF16)
                dqkvr_ref[rows, KW + h * DK:KW + (h + 1) * DK] = (
                    dk_inv * e_gi[:, lk] + dk_end * e_end[:, lk]).astype(BF16)
                dqkvr_ref[rows, 2 * KW + h * DV:2 * KW + (h + 1) * DV] = dv.astype(BF16)
                ke_prod = dk_end * k_end[:, lk]
                d_g.append(dq_dec * q_dec[:, lk] - dk_inv * k_inv[:, lk] - ke_prod)
                d_gl.append(jnp.sum(ke_prod, axis=0, keepdims=True) + d_decay * decay[:, lk])
            dlogg = _tri_sum(tri_up, jnp.concatenate(d_g, axis=1)) + jnp.concatenate(d_gl, axis=1)
            dg_raw = jnp.where(live, dlogg * (1.0 / TAU) * jax.nn.sigmoid(-g_raw), 0.0)
            ggb_ref[...] += jnp.sum(dg_raw, axis=0, keepdims=True)
            dg_b = dg_raw.astype(BF16)
            ggw_ref[...] += _dot(glr_ref[rows, :].astype(BF16), dg_b, TN)
            dglr_ref[rows, :] = _dot(dg_b, gw2_ref[...], NT).astype(BF16)

    full = lambda shape: pl.BlockSpec(shape, lambda s: (0,) * len(shape))
    return pl.pallas_call(
        body, name="gla_bwd", grid=(n,),
        in_specs=pspecs + [pl.BlockSpec((R, GW), lambda s: (st(s), 0)), pl.BlockSpec((R, GW), lambda s: (st(s), 0)),
                           pl.BlockSpec((CPS, H, DV, DK), lambda s: (st(s), 0, 0, 0)),
                           full((128, KW)), full((1, KW)), full((1, DV))],
        out_specs=[pl.BlockSpec((R, 2 * KW + 2 * GW), lambda s: (st(s), 0)), pl.BlockSpec((R, 128), lambda s: (st(s), 0)),
                   full((1, DV)), full((1, KW)), full((128, KW))],
        out_shape=[_sds((LP, 2 * KW + 2 * GW), BF16), _sds((LP, 128), BF16),
                   _sds((1, DV), F32), _sds((1, KW), F32), _sds((128, KW), F32)],
        scratch_shapes=[pltpu.VMEM((H, DV, DK), F32)],
        compiler_params=_params(("arbitrary",)),
    )(proj, proj, proj, proj, proj, dmixed, o_saved, st_saved, gw2p, gate_b, gnw)


def _pool_pre(x, win, rid):
    s, step = x, 1
    while step < win:
        s = s + pltpu.roll(s, step, 0)
        step *= 2
    cnt = jnp.clip(rid - (PAD - 1), 1, win).astype(F32)
    live = rid >= PAD
    return jnp.where(live, s / cnt - x, 0.0), cnt, live


def _pool_fwd(proj, pool_w, pool_scale):
    def body(pu_ref, w_ref, sc_ref, o_ref):
        rid = lax.broadcasted_iota(jnp.int32, (LP, GC), 0)
        for g, win in enumerate(WINDOWS):
            @pl.when(pl.program_id(0) == g)
            def _():
                y, _, _ = _pool_pre(pu_ref[...], win, rid)
                o_ref[...] = (_dot(y.astype(BF16), w_ref[...], NN) * sc_ref[...]).astype(BF16)

    col = lambda base: pl.BlockSpec((LP, GC), lambda g: (0, base + g))
    return pl.pallas_call(
        body, name="pool_fwd", grid=(len(WINDOWS),),
        in_specs=[col(POOL_BLK), pl.BlockSpec((None, GC, GC), lambda g: (g, 0, 0)),
                  pl.BlockSpec((1, GC), lambda g: (0, g))],
        out_specs=col(0), out_shape=_sds((LP, PW), BF16), compiler_params=_params(("parallel",)),
    )(proj, pool_w, pool_scale)


def _pool_bwd(proj, dmixed, pool_w, pool_scale):
    def body(pu_ref, do_ref, w_ref, sc_ref, dpu_ref, dw_ref, dsc_ref):
        rid = lax.broadcasted_iota(jnp.int32, (LP, GC), 0)
        for g, win in enumerate(WINDOWS):
            @pl.when(pl.program_id(0) == g)
            def _():
                y, cnt, live = _pool_pre(pu_ref[...], win, rid)
                y_b = y.astype(BF16)
                w = w_ref[...]
                do = do_ref[...]
                dsc_ref[...] = jnp.sum(do * _dot(y_b, w, NN), axis=0, keepdims=True)
                dyw = (do * sc_ref[...]).astype(BF16)
                dw_ref[...] = _dot(y_b, dyw, TN)
                dy = jnp.where(live, _dot(dyw, w, NT), 0.0)
                s, step = dy / cnt, 1
                while step < win:
                    s = s + pltpu.roll(s, LP - step, 0)
                    step *= 2
                dpu_ref[...] = (s - dy).astype(BF16)

    col = lambda base: pl.BlockSpec((LP, GC), lambda g: (0, base + g))
    mat = pl.BlockSpec((None, GC, GC), lambda g: (g, 0, 0))
    vec = pl.BlockSpec((1, GC), lambda g: (0, g))
    return pl.pallas_call(
        body, name="pool_bwd", grid=(len(WINDOWS),),
        in_specs=[col(POOL_BLK), col(GW // GC), mat, vec], out_specs=[col(0), mat, vec],
        out_shape=[_sds((LP, PW), BF16), _sds((4, GC, GC), F32), _sds((1, PW), F32)],
        compiler_params=_params(("parallel",)),
    )(proj, dmixed, pool_w, pool_scale)


def _adamw_math(w, g, m, v):
    m = B1 * m + (1.0 - B1) * g
    v = B2 * v + (1.0 - B2) * (g * g)
    m_hat = m / (1.0 - B1 ** STEP)
    v_hat = v / (1.0 - B2 ** STEP)
    return -LR * (m_hat / (jnp.sqrt(v_hat) + AEPS) + WD * w), m, v


def _adamw_landed(landed, w, m, v, rows, name):
    r, c = w.shape

    def body(l_ref, w_ref, m_ref, v_ref, g_ref, d_ref, mo_ref, vo_ref):
        g = l_ref[0].astype(F32)
        for k in range(1, NDEV):
            g = g + l_ref[k].astype(F32)
        g_ref[...] = g
        d_ref[...], mo_ref[...], vo_ref[...] = _adamw_math(w_ref[...], g, m_ref[...], v_ref[...])

    blk = pl.BlockSpec((rows, c), lambda i: (i, 0))
    return pl.pallas_call(
        body, name=name, grid=(r // rows,),
        in_specs=[pl.BlockSpec((NDEV, rows, c), lambda i: (0, i, 0)), blk, blk, blk], out_specs=[blk] * 4,
        out_shape=[_sds((r, c), F32)] * 4, compiler_params=_params(("parallel",)),
    )(landed, w, m, v)


def _adamw_small(g, w, m, v):
    def body(g_ref, w_ref, m_ref, v_ref, d_ref, mo_ref, vo_ref):
        d_ref[...], mo_ref[...], vo_ref[...] = _adamw_math(w_ref[...], g_ref[...], m_ref[...], v_ref[...])

    return pl.pallas_call(body, name="adamw_small", out_shape=[_sds(w.shape, F32)] * 3)(g, w, m, v)


SMALL_REPL = (("norm1_w", D), ("norm2_w", D), ("final_norm_w", D), ("pool_scale", PW), ("gate_b", KW),
              ("gla_norm_w", DV))


def _pack_rows(vecs, rows):
    flat = jnp.concatenate([jnp.ravel(v) for v in vecs])
    return jnp.pad(flat, (0, rows * 1024 - flat.shape[0])).reshape(rows, 1024)


def kernel(x, meta_tokens, norm1_w, w_in, gate_w2, gate_b, gla_norm_w, pool_w, pool_scale, w_out, norm2_w, mlp_w1, mlp_w2, final_norm_w, loss_target, m_meta_tokens, m_norm1_w, m_w_in, m_gate_w2, m_gate_b, m_gla_norm_w, m_pool_w, m_pool_scale, m_w_out, m_norm2_w, m_mlp_w1, m_mlp_w2, m_final_norm_w, v_meta_tokens, v_norm1_w, v_w_in, v_gate_w2, v_gate_b, v_gla_norm_w, v_pool_w, v_pool_scale, v_w_out, v_norm2_w, v_mlp_w1, v_mlp_w2, v_final_norm_w):
    me = 4 * lax.axis_index("x") + 2 * lax.axis_index("y") + lax.axis_index("c")
    W = dict(meta_tokens=meta_tokens, norm1_w=norm1_w, w_in=w_in, gate_w2=gate_w2, gate_b=gate_b,
             gla_norm_w=gla_norm_w, pool_w=pool_w, pool_scale=pool_scale, w_out=w_out, norm2_w=norm2_w,
             mlp_w1=mlp_w1, mlp_w2=mlp_w2, final_norm_w=final_norm_w)
    Mo = dict(meta_tokens=m_meta_tokens, norm1_w=m_norm1_w, w_in=m_w_in, gate_w2=m_gate_w2, gate_b=m_gate_b,
              gla_norm_w=m_gla_norm_w, pool_w=m_pool_w, pool_scale=m_pool_scale, w_out=m_w_out, norm2_w=m_norm2_w,
              mlp_w1=m_mlp_w1, mlp_w2=m_mlp_w2, final_norm_w=m_final_norm_w)
    Vo = dict(meta_tokens=v_meta_tokens, norm1_w=v_norm1_w, w_in=v_w_in, gate_w2=v_gate_w2, gate_b=v_gate_b,
              gla_norm_w=v_gla_norm_w, pool_w=v_pool_w, pool_scale=v_pool_scale, w_out=v_w_out, norm2_w=v_norm2_w,
              mlp_w1=v_mlp_w1, mlp_w2=v_mlp_w2, final_norm_w=v_final_norm_w)

    big = dict(w_in=w_in[0], w_out=w_out[0], mlp_w1=mlp_w1[0], mlp_w2=mlp_w2[0], pool_w=pool_w[0].reshape(4 * 32, GC))
    names = list(big)
    gathered = dict(zip(names, _all_gather_hbm([big[k].astype(BF16) for k in names], "gather_weights")))
    small = _exchange_small(_pack_rows([meta_tokens, gate_w2[0]], 8), False, "gather_small")
    meta_full = small[:, 0:4].reshape(NDEV, N_META, D // NDEV).transpose(1, 0, 2).reshape(N_META, D)
    gw2_full = small[:, 4].reshape(NDEV, RANK, KW // NDEV).transpose(1, 0, 2).reshape(RANK, KW)
    gw2p = jnp.pad(gw2_full, ((0, 128 - RANK), (0, 0))).astype(BF16)

    win_nat = gathered["w_in"].transpose(1, 0, 2).reshape(D, D_IN)
    c_glr = 2 * KW + 2 * GW
    win_p = jnp.concatenate([win_nat[:, :c_glr], win_nat[:, c_glr + RANK:], win_nat[:, c_glr:c_glr + RANK],
                             jnp.zeros((D, D_INP - D_IN), BF16)], axis=1)
    wout_f = gathered["w_out"].reshape(D, D)
    w1_g = gathered["mlp_w1"]
    w2_f = gathered["mlp_w2"].reshape(DFF, D)
    poolw_f = gathered["pool_w"].reshape(NDEV, 4, 32, GC).transpose(1, 0, 2, 3).reshape(4, GC, GC)

    step = _layer_step(x[0], loss_target[0], meta_full, gw2p, win_p, wout_f, w1_g, w2_f, poolw_f, norm1_w, gate_b,
                       gla_norm_w, pool_scale, norm2_w, final_norm_w.reshape(1, D))
    grad_x = step["dh0"][ROW_X:][None]
    g_win_p = step["w_in"]

    g_win = jnp.concatenate([g_win_p[:, :c_glr], g_win_p[:, c_glr + PW:c_glr + PW + RANK],
                             g_win_p[:, c_glr:c_glr + PW]], axis=1)
    parts = dict(
        w_in=g_win.reshape(D, NDEV, D_IN // NDEV).transpose(1, 0, 2),
        w_out=step["w_out"].reshape(NDEV, D // NDEV, D),
        mlp_w1=step["mlp_w1"],
        mlp_w2=step["mlp_w2"].reshape(NDEV, DFF // NDEV, D),
        pool_w=step["pool_w"].astype(BF16).reshape(4, NDEV, 32, GC).transpose(1, 0, 2, 3).reshape(NDEV, 4 * 32, GC),
    )
    landed = dict(zip(names, _scatter_hbm([parts[k] for k in names], "scatter_grads")))

    loss_part = 0.5 * jnp.sum(step["sq"]) / D
    packed = jnp.concatenate([
        _pack_rows([step[k] for k, _ in SMALL_REPL] + [loss_part], 8),
        step["gate_w2"][:RANK].reshape(8, 1024), step["dh0"][PAD:ROW_X].reshape(32, 1024)], axis=0)
    red = _exchange_small(packed, True, "reduce_small")
    loss = red[7, 768]
    g_gw2_mine = lax.dynamic_slice(red[8:16].reshape(RANK, KW), (0, me * (KW // NDEV)), (RANK, KW // NDEV))
    g_meta_mine = lax.dynamic_slice(red[16:48].reshape(N_META, D), (0, me * (D // NDEV)), (N_META, D // NDEV))

    shard2d = dict(w_in=(D, D_IN // NDEV), w_out=(D // NDEV, D), mlp_w1=(D, DFF // NDEV), mlp_w2=(DFF // NDEV, D),
                   pool_w=(4 * 32, GC))
    rows = dict(w_in=256, w_out=64, mlp_w1=128, mlp_w2=64, pool_w=128)
    out = {}
    for k in names:
        res = _adamw_landed(landed[k], W[k].reshape(shard2d[k]), Mo[k].reshape(shard2d[k]),
                            Vo[k].reshape(shard2d[k]), rows[k], "adamw_" + k)
        out[k] = [a.reshape(W[k].shape) for a in res]

    def small_pack(P):
        return jnp.concatenate([_pack_rows([P[k] for k, _ in SMALL_REPL], 8),
                                _pack_rows([P["meta_tokens"], P["gate_w2"]], 8)], axis=0)

    g_small = jnp.concatenate([red[0:8], _pack_rows([g_meta_mine, g_gw2_mine], 8)], axis=0)
    g_small = g_small.at[7, 768].set(0.0)
    res_small = _adamw_small(g_small, small_pack(W), small_pack(Mo), small_pack(Vo))
    res_small = [g_small] + list(res_small)
    off = 0
    for k, nel in SMALL_REPL:
        out[k] = [a[0:8].reshape(-1)[off:off + nel].reshape(W[k].shape) for a in res_small]
        off += nel
    out["meta_tokens"] = [a[8:12].reshape(N_META, D // NDEV) for a in res_small]
    out["gate_w2"] = [a[12].reshape(1, RANK, KW // NDEV) for a in res_small]

    order = ["meta_tokens", "norm1_w", "w_in", "gate_w2", "gate_b", "gla_norm_w", "pool_w", "pool_scale", "w_out",
             "norm2_w", "mlp_w1", "mlp_w2", "final_norm_w"]
    return (loss, grad_x, *[out[k][0] for k in order], *[out[k][1] for k in order],
            *[out[k][2] for k in order], *[out[k][3] for k in order])


def _layer_step(x, target, meta_full, gw2p, win_p, wout_f, w1_g, w2_f, poolw_f, norm1_w, gate_b, gla_norm_w,
                pool_scale, norm2_w, final_norm_w):
    h0 = jnp.concatenate([jnp.zeros((PAD, D), F32), meta_full, x], axis=0)
    u1 = _rmsnorm_fwd(h0, norm1_w, "rmsnorm1")
    proj = _matmul(u1, win_p, mode="nn", tm=1056, tn=1408, tk=512, name="proj")
    og, o_saved, st_saved = _gla_fwd(proj, gw2p, gate_b, gla_norm_w)
    op = _pool_fwd(proj, poolw_f, pool_scale)
    mixed = jnp.concatenate([og, op], axis=1)
    h1 = _matmul(mixed, wout_f, mode="nn", tm=1056, tn=1024, tk=512, name="mix_out", epi="add", extra=h0)
    u2 = _rmsnorm_fwd(h1, norm2_w, "rmsnorm2")
    z, act = _matmul(u2, w1_g, mode="nn", tm=1056, tn=1024, tk=512, name="mlp_up", epi="relu2", b_slots=True)
    h2 = _matmul(act, w2_f, mode="nn", tm=1056, tn=1024, tk=512, name="mlp_down", epi="add", extra=h1)
    dh2, dh2b, sq, g_fnw = _loss_head(h2, final_norm_w, target)

    g_w2 = _matmul(act, dh2b, mode="tn", tm=512, tn=1024, tk=LP, name="d_mlp_w2", out_dtype=BF16)
    dz = _matmul(dh2b, w2_f, mode="nt", tm=1056, tn=1024, tk=512, name="d_act", out_dtype=BF16, epi="dz", extra=z)
    g_w1 = _matmul(u2, dz, mode="tn", tm=512, tn=1024, tk=LP, name="d_mlp_w1", out_dtype=BF16, out_slots=True)
    du2 = _matmul(dz, w1_g, mode="nt", tm=1056, tn=1024, tk=1024, name="d_u2", b_slots=True)
    dh1, dh1b, g_n2 = _rmsnorm_bwd(h1, norm2_w, du2, dh2, "rmsnorm2_bwd")
    g_wout = _matmul(mixed, dh1b, mode="tn", tm=512, tn=1024, tk=LP, name="d_w_out", out_dtype=BF16)
    dmixed = _matmul(dh1b, wout_f, mode="nt", tm=1056, tn=1024, tk=512, name="d_mixed")
    dqkvr, dglr, g_gnw, g_gb, g_gw2 = _gla_bwd(proj, dmixed, o_saved, st_saved, gw2p, gate_b, gla_norm_w)
    dpu, g_poolw, g_psc = _pool_bwd(proj, dmixed, poolw_f, pool_scale)
    dproj = jnp.concatenate([dqkvr, dpu, dglr], axis=1)
    g_win_p = _matmul(u1, dproj, mode="tn", tm=512, tn=1408, tk=LP, name="d_w_in", out_dtype=BF16)
    du1 = _matmul(dproj, win_p, mode="nt", tm=1056, tn=1024, tk=1408, name="d_u1")
    dh0, _, g_n1 = _rmsnorm_bwd(h0, norm1_w, du1, dh1, "rmsnorm1_bwd")
    return dict(dh0=dh0, sq=sq, w_in=g_win_p, w_out=g_wout, mlp_w1=g_w1, mlp_w2=g_w2, pool_w=g_poolw, gate_w2=g_gw2,
                norm1_w=g_n1, norm2_w=g_n2, final_norm_w=g_fnw, pool_scale=g_psc, gate_b=g_gb, gla_norm_w=g_gnw)
```

```python
import functools

import jax
import jax.numpy as jnp
from jax import lax
from jax.experimental import pallas as pl
from jax.experimental.pallas import tpu as pltpu

F32, BF16 = jnp.float32, jnp.bfloat16
MESH = pl.DeviceIdType.MESH

NDEV = 8
D = 2048
SEQ = 2048
N_META = 16
CHUNK = 64
PAD = (-N_META) % CHUNK
ROW_X = PAD + N_META
LP = ROW_X + SEQ
NCH = LP // CHUNK
H = 4
DK = 128
DV = 256
KW = H * DK
GW = H * DV
PW = 1024
RANK = 16
TAU = 16.0
WINDOWS = (2, 4, 8, 16)
GC = 256
DFF = 4 * D
EPS = 1e-6
D_IN = 2 * KW + 2 * GW + RANK + PW
D_INP = 4224
GLR_BLK = (2 * KW + 2 * GW + PW) // 128
POOL_BLK = (2 * KW + 2 * GW) // GC
LR, B1, B2, AEPS, WD, STEP = 0.001, 0.9, 0.999, 1e-08, 0.01, 10
VMEM_LIMIT = 48 * 1024 * 1024
CPS = 3


def _params(sem=None):
    return pltpu.CompilerParams(dimension_semantics=sem, vmem_limit_bytes=VMEM_LIMIT)


def _sds(shape, dtype):
    return jax.ShapeDtypeStruct(shape, dtype)


def _me():
    return lax.axis_index("x"), lax.axis_index("y"), lax.axis_index("c")


def _peer(j):
    x, y, c = _me()
    return (x ^ ((j >> 2) & 1), y ^ ((j >> 1) & 1), c ^ (j & 1))


def _slot(dev):
    return 4 * dev[0] + 2 * dev[1] + dev[2]


HBM_SPEC = pl.BlockSpec(memory_space=pltpu.HBM)
SEM_SPEC = pl.BlockSpec(memory_space=pltpu.SEMAPHORE)
ANY_SPEC = pl.BlockSpec(memory_space=pl.ANY)
EFFECT = pltpu.SideEffectType.DATAFLOW_SIDE_EFFECTING
SIBLING = 1
OTHER_CHIPS = (2, 4, 6)


def _in_hbm(a):
    return pltpu.with_memory_space_constraint(a, pltpu.HBM)


def _chip(dev):
    return 2 * dev[0] + dev[1]


def _rcopy(src, dst, send_sem, recv_sem, to):
    return pltpu.make_async_remote_copy(src_ref=src, dst_ref=dst, send_sem=send_sem, recv_sem=recv_sem,
                                        device_id=to, device_id_type=MESH)


def _split_call(body, name, ins, in_specs, out_shape, out_specs, aliases, scratch=()):
    n = len(ins) + len(out_shape)

    def with_token(*refs):
        body(*refs[:n], *refs[n + 1:])
        refs[n][...] = jnp.zeros_like(refs[n])

    return pl.pallas_call(
        with_token, name=name, in_specs=in_specs, out_shape=list(out_shape) + [_sds((8, 128), F32)],
        out_specs=list(out_specs) + [pl.BlockSpec(memory_space=pltpu.VMEM)],
        input_output_aliases=aliases, scratch_shapes=list(scratch),
        compiler_params=pltpu.CompilerParams(has_side_effects=EFFECT),
    )(*ins)


def _after(body, n_in, deps):
    deps = [d for d in deps if d is not None]
    if not deps:
        return body, [], []
    return (lambda *refs: body(*refs[:n_in], *refs[n_in + len(deps):])), deps, [ANY_SPEC] * len(deps)


def _gather_start(shards, name):
    n = len(shards)
    lands = [lax.empty((NDEV,) + s.shape, s.dtype) for s in shards]

    def body(*refs):
        src, land = refs[:n], refs[n:2 * n]
        outs = refs[2 * n:2 * n + 4 * n]
        local_sems = refs[-1]
        me = _slot(_me())
        own = [pltpu.make_async_copy(src[i], land[i].at[me], local_sems.at[i]) for i in range(n)]
        for cp in own:
            cp.start()
        for i in range(n):
            send_sems, recv_sems = outs[4 * i], outs[4 * i + 1]
            for k, rel in enumerate((SIBLING,) + OTHER_CHIPS):
                _rcopy(src[i], land[i].at[me], send_sems.at[k], recv_sems.at[k], _peer(rel)).start()
        for cp in own:
            cp.wait()

    out_shape, out_specs, aliases = [], [], {}
    for i, s in enumerate(shards):
        out_shape += [pltpu.SemaphoreType.DMA((4,)), pltpu.SemaphoreType.DMA((4,)), pltpu.HBM(s.shape, s.dtype),
                      pltpu.HBM((NDEV,) + s.shape, s.dtype)]
        out_specs += [SEM_SPEC, SEM_SPEC, HBM_SPEC, HBM_SPEC]
        aliases[i] = 4 * i + 2
        aliases[n + i] = 4 * i + 3
    res = _split_call(body, name, [_in_hbm(s) for s in shards] + [_in_hbm(l) for l in lands], [HBM_SPEC] * (2 * n),
                      out_shape, out_specs, aliases, scratch=[pltpu.SemaphoreType.DMA((n,))])
    return [tuple(res[4 * i:4 * i + 4]) for i in range(n)], res[-1]


def _gather_forward(started, after, name):
    n = len(started)

    def body(*refs):
        land, recv1 = refs[:n], refs[n:2 * n]
        outs = refs[2 * n + 1:]
        for i in range(n):
            send2, recv2 = outs[3 * i + 1], outs[3 * i + 2]
            for k, rel in enumerate(OTHER_CHIPS):
                blk = land[i].at[_slot(_peer(rel))]
                _rcopy(blk, blk, send2.at[k], recv1[i].at[1 + k], _peer(rel)).wait_recv()
                _rcopy(blk, blk, send2.at[k], recv2.at[k], _peer(SIBLING)).start()

    ins = [_in_hbm(st[3]) for st in started] + [st[1] for st in started] + [after]
    out_shape, out_specs, aliases = [], [], {}
    for i, st in enumerate(started):
        out_shape += [pltpu.HBM(st[3].shape, st[3].dtype), pltpu.SemaphoreType.DMA((3,)), pltpu.SemaphoreType.DMA((3,))]
        out_specs += [HBM_SPEC, SEM_SPEC, SEM_SPEC]
        aliases[i] = 3 * i
    res = _split_call(body, name, ins, [HBM_SPEC] * n + [SEM_SPEC] * n + [ANY_SPEC], out_shape, out_specs, aliases)
    return [(st[0], st[1], st[2], res[3 * i], res[3 * i + 1], res[3 * i + 2]) for i, st in enumerate(started)], res[-1]


def _gather_finish(forwarded, after, name):
    n = len(forwarded)

    def body(*refs):
        for i in range(n):
            send1, recv1, src, land, send2, recv2 = refs[6 * i:6 * i + 6]
            me = _slot(_me())
            sib = _slot(_peer(SIBLING))
            for k, rel in enumerate((SIBLING,) + OTHER_CHIPS):
                _rcopy(src, land.at[me], send1.at[k], recv1.at[k], _peer(rel)).wait_send()
            _rcopy(src, land.at[sib], send1.at[0], recv1.at[0], _peer(SIBLING)).wait_recv()
            for k, rel in enumerate(OTHER_CHIPS):
                mine, theirs = land.at[_slot(_peer(rel))], land.at[_slot(_peer(rel ^ SIBLING))]
                _rcopy(mine, mine, send2.at[k], recv2.at[k], _peer(SIBLING)).wait_send()
                _rcopy(theirs, theirs, send2.at[k], recv2.at[k], _peer(SIBLING)).wait_recv()

    ins, in_specs, out_shape, aliases = [], [], [], {}
    for i, f in enumerate(forwarded):
        ins += [f[0], f[1], _in_hbm(f[2]), _in_hbm(f[3]), f[4], f[5]]
        in_specs += [SEM_SPEC, SEM_SPEC, HBM_SPEC, HBM_SPEC, SEM_SPEC, SEM_SPEC]
        out_shape.append(pltpu.HBM(f[3].shape, f[3].dtype))
        aliases[6 * i + 3] = i
    res = _split_call(body, name, ins + [after], in_specs + [ANY_SPEC], out_shape, [HBM_SPEC] * n, aliases)
    return list(res[:-1])


def _to_sibling_start(parts, name):
    n = len(parts)
    lands = [lax.empty(p.shape[1:], p.dtype) for p in parts]

    def body(*refs):
        src, land = refs[:n], refs[n:2 * n]
        outs = refs[2 * n:]
        other = 1 - lax.axis_index("c")
        for i in range(n):
            _rcopy(src[i].at[other], land[i], outs[4 * i], outs[4 * i + 1], _peer(SIBLING)).start()

    out_shape, out_specs, aliases = [], [], {}
    for i, p in enumerate(parts):
        out_shape += [pltpu.SemaphoreType.DMA(()), pltpu.SemaphoreType.DMA(()), pltpu.HBM(p.shape, p.dtype),
                      pltpu.HBM(p.shape[1:], p.dtype)]
        out_specs += [SEM_SPEC, SEM_SPEC, HBM_SPEC, HBM_SPEC]
        aliases[i] = 4 * i + 2
        aliases[n + i] = 4 * i + 3
    res = _split_call(body, name, [_in_hbm(p) for p in parts] + [_in_hbm(l) for l in lands], [HBM_SPEC] * (2 * n),
                      out_shape, out_specs, aliases)
    return [tuple(res[4 * i:4 * i + 4]) for i in range(n)], res[-1]


def _to_sibling_finish(started, after, name):
    n = len(started)

    def body(*refs):
        for i in range(n):
            send, recv, src, land = refs[4 * i:4 * i + 4]
            cp = _rcopy(src.at[0], land, send, recv, _peer(SIBLING))
            cp.wait_send()
            cp.wait_recv()

    ins, in_specs, out_shape, aliases = [], [], [], {}
    for i, st in enumerate(started):
        ins += [st[0], st[1], _in_hbm(st[2]), _in_hbm(st[3])]
        in_specs += [SEM_SPEC, SEM_SPEC, HBM_SPEC, HBM_SPEC]
        out_shape += [pltpu.HBM(st[2].shape, st[2].dtype), pltpu.HBM(st[3].shape, st[3].dtype)]
        aliases[4 * i + 2] = 2 * i
        aliases[4 * i + 3] = 2 * i + 1
    res = _split_call(body, name, ins + [after], in_specs + [ANY_SPEC], out_shape, [HBM_SPEC] * (2 * n), aliases)
    return [(res[2 * i], res[2 * i + 1]) for i in range(n)]


def _chip_sum(parts, from_sibling, my_c, rows, name):
    _, _, r, c = parts.shape

    def body(c_ref, p_ref, s_ref, o_ref):
        o_ref[...] = (p_ref[...].astype(F32) + s_ref[...].astype(F32)).astype(o_ref.dtype)

    blk = pl.BlockSpec((4, rows, c), lambda i, c_ref: (0, i, 0))
    return pl.pallas_call(
        body, name=name, out_shape=_sds((4, r, c), parts.dtype),
        grid_spec=pltpu.PrefetchScalarGridSpec(
            num_scalar_prefetch=1, grid=(r // rows,),
            in_specs=[pl.BlockSpec((None, 4, rows, c), lambda i, c_ref: (c_ref[0], 0, i, 0)), blk], out_specs=blk),
        compiler_params=_params(("parallel",)),
    )(my_c, parts, from_sibling)


def _to_chips_start(sums, name):
    n = len(sums)
    lands = [lax.empty((3,) + s.shape[1:], s.dtype) for s in sums]

    def body(*refs):
        src, land = refs[:n], refs[n:2 * n]
        outs = refs[2 * n:]
        for i in range(n):
            for k, rel in enumerate(OTHER_CHIPS):
                to = _peer(rel)
                _rcopy(src[i].at[_chip(to)], land[i].at[k], outs[4 * i].at[k], outs[4 * i + 1].at[k], to).start()

    out_shape, out_specs, aliases = [], [], {}
    for i, s in enumerate(sums):
        out_shape += [pltpu.SemaphoreType.DMA((3,)), pltpu.SemaphoreType.DMA((3,)), pltpu.HBM(s.shape, s.dtype),
                      pltpu.HBM((3,) + s.shape[1:], s.dtype)]
        out_specs += [SEM_SPEC, SEM_SPEC, HBM_SPEC, HBM_SPEC]
        aliases[i] = 4 * i + 2
        aliases[n + i] = 4 * i + 3
    res = _split_call(body, name, [_in_hbm(s) for s in sums] + [_in_hbm(l) for l in lands], [HBM_SPEC] * (2 * n),
                      out_shape, out_specs, aliases)
    return [tuple(res[4 * i:4 * i + 4]) for i in range(n)], res[-1]


def _to_chips_finish(started, after, name):
    n = len(started)

    def body(*refs):
        for i in range(n):
            send, recv, src, land = refs[4 * i:4 * i + 4]
            for k, rel in enumerate(OTHER_CHIPS):
                cp = _rcopy(src.at[0], land.at[k], send.at[k], recv.at[k], _peer(rel))
                cp.wait_send()
                cp.wait_recv()

    ins, in_specs, out_shape, aliases = [], [], [], {}
    for i, st in enumerate(started):
        ins += [st[0], st[1], _in_hbm(st[2]), _in_hbm(st[3])]
        in_specs += [SEM_SPEC, SEM_SPEC, HBM_SPEC, HBM_SPEC]
        out_shape += [pltpu.HBM(st[2].shape, st[2].dtype), pltpu.HBM(st[3].shape, st[3].dtype)]
        aliases[4 * i + 2] = 2 * i
        aliases[4 * i + 3] = 2 * i + 1
    res = _split_call(body, name, ins + [after], in_specs + [ANY_SPEC], out_shape, [HBM_SPEC] * (2 * n), aliases)
    return [(res[2 * i], res[2 * i + 1]) for i in range(n)]


def _exchange_small(v, reduce, name):
    r, c = v.shape

    def body(v_ref, o_ref, land, send_sems, recv_sems):
        me = _slot(_me())
        copies = []
        for j in range(1, NDEV):
            cp = pltpu.make_async_remote_copy(
                src_ref=v_ref, dst_ref=land.at[me], send_sem=send_sems.at[j - 1],
                recv_sem=recv_sems.at[j - 1], device_id=_peer(j), device_id_type=MESH)
            cp.start()
            copies.append(cp)
        land[me] = v_ref[...]
        for cp in copies:
            cp.wait()
        if reduce:
            acc = land[0]
            for k in range(1, NDEV):
                acc = acc + land[k]
            o_ref[...] = acc
        else:
            o_ref[...] = land[...]

    vm = pl.BlockSpec(memory_space=pltpu.VMEM)
    return pl.pallas_call(
        body, name=name,
        out_shape=_sds((r, c) if reduce else (NDEV, r, c), F32),
        in_specs=[vm], out_specs=vm,
        scratch_shapes=[pltpu.VMEM((NDEV, r, c), F32), pltpu.SemaphoreType.DMA((NDEV - 1,)),
                        pltpu.SemaphoreType.DMA((NDEV - 1,))],
        compiler_params=_params(),
    )(v)


def _matmul(a, b, *, mode, tm, tn, tk, name, out_dtype=F32, epi=None, extra=None, b_slots=False, out_slots=False,
            deps=()):
    slot_w = b.shape[-1] if b_slots else None
    if mode == "nn":
        M, K = a.shape
        N = NDEV * slot_w if b_slots else b.shape[1]
    elif mode == "tn":
        K, M = a.shape
        N = b.shape[1]
    else:
        M, K = a.shape
        N = b.shape[-2]
        if b_slots:
            assert K == NDEV * slot_w and tk == slot_w
    if mode == "nn" and b_slots:
        assert tn == slot_w
    if out_slots == "cols":
        assert tn * NDEV == N
    if out_slots == "rows":
        assert (M // NDEV) % tm == 0
    assert M % tm == 0 and N % tn == 0 and K % tk == 0, (name, M, N, K, tm, tn, tk)
    nk = K // tk
    dims = {"nn": ((1,), (0,)), "tn": ((0,), (0,)), "nt": ((1,), (1,))}[mode]

    if mode == "tn":
        a_spec = pl.BlockSpec((tk, tm), lambda i, j, k: (k, i))
    else:
        a_spec = pl.BlockSpec((tm, tk), lambda i, j, k: (i, k))
    if mode == "nt":
        b_spec = (pl.BlockSpec((None, tn, tk), lambda i, j, k: (k, j, 0)) if b_slots
                  else pl.BlockSpec((tn, tk), lambda i, j, k: (j, k)))
    else:
        b_spec = (pl.BlockSpec((None, tk, tn), lambda i, j, k: (j, k, 0)) if b_slots
                  else pl.BlockSpec((tk, tn), lambda i, j, k: (k, j)))
    tile = pl.BlockSpec((tm, tn), lambda i, j, k: (i, j))
    if out_slots == "cols":
        out_spec = pl.BlockSpec((None, None, tm, tn), lambda i, j, k: (j % 2, j // 2, i, 0))
        out_shape = _sds((2, 4, M, tn), out_dtype)
    elif out_slots == "rows":
        per = M // NDEV // tm
        out_spec = pl.BlockSpec((None, None, tm, tn), lambda i, j, k: ((i // per) % 2, (i // per) // 2, i % per, j))
        out_shape = _sds((2, 4, M // NDEV, N), out_dtype)
    else:
        out_spec, out_shape = tile, _sds((M, N), out_dtype)
    ins, in_specs = [a, b], [a_spec, b_spec]
    if epi in ("add", "dz"):
        ins.append(extra)
        in_specs.append(tile)
    if epi == "relu2":
        out_specs, out_shapes = [tile, tile], [_sds((M, N), F32), _sds((M, N), BF16)]
    else:
        out_specs, out_shapes = out_spec, out_shape
    n_in = len(ins)

    def body(*refs):
        outs = refs[n_in:-1] if nk > 1 else refs[n_in:]

        def finish(p):
            if epi is None:
                outs[0][...] = p.astype(out_dtype)
            elif epi == "add":
                outs[0][...] = (p + refs[2][...]).astype(out_dtype)
            elif epi == "relu2":
                outs[0][...] = p
                rz = jnp.maximum(p, 0.0)
                outs[1][...] = (rz * rz).astype(BF16)
            else:
                outs[0][...] = (p * (2.0 * jnp.maximum(refs[2][...], 0.0))).astype(out_dtype)

        p = lax.dot_general(refs[0][...].astype(BF16), refs[1][...].astype(BF16), (dims, ((), ())),
                            preferred_element_type=F32)
        if nk == 1:
            finish(p)
            return
        acc = refs[-1]
        k = pl.program_id(2)

        @pl.when(k == 0)
        def _():
            acc[...] = p

        @pl.when(k > 0)
        def _():
            acc[...] += p

        @pl.when(k == nk - 1)
        def _():
            finish(acc[...])

    body, dep_ins, dep_specs = _after(body, n_in, deps)
    return pl.pallas_call(
        body, name=name, grid=(M // tm, N // tn, nk),
        in_specs=in_specs + dep_specs, out_specs=out_specs, out_shape=out_shapes,
        scratch_shapes=[pltpu.VMEM((tm, tn), F32)] if nk > 1 else [],
        compiler_params=_params(("parallel", "parallel", "arbitrary")),
    )(*ins, *dep_ins)


ROWS = 352


def _rmsnorm_fwd(h, w, name, deps=()):
    def body(h_ref, w_ref, u_ref):
        x = h_ref[...]
        rstd = lax.rsqrt(jnp.mean(x * x, axis=-1, keepdims=True) + EPS)
        u_ref[...] = (x * rstd * w_ref[...]).astype(BF16)

    row = pl.BlockSpec((ROWS, D), lambda i: (i, 0))
    body, dep_ins, dep_specs = _after(body, 2, deps)
    return pl.pallas_call(
        body, name=name, grid=(LP // ROWS,), in_specs=[row, pl.BlockSpec((1, D), lambda i: (0, 0))] + dep_specs,
        out_specs=row, out_shape=_sds((LP, D), BF16), compiler_params=_params(("parallel",)),
    )(h, w, *dep_ins)


def _rmsnorm_bwd(h, w, du, dres, name, deps=()):
    def body(h_ref, w_ref, du_ref, dres_ref, dh_ref, dhb_ref, gw_ref):
        x = h_ref[...]
        rstd = lax.rsqrt(jnp.mean(x * x, axis=-1, keepdims=True) + EPS)
        xhat = x * rstd
        dy = du_ref[...]
        dxh = dy * w_ref[...]
        dh = dres_ref[...] + rstd * (dxh - xhat * jnp.mean(dxh * xhat, axis=-1, keepdims=True))
        dh_ref[...] = dh
        dhb_ref[...] = dh.astype(BF16)
        part = jnp.sum(dy * xhat, axis=0, keepdims=True)

        @pl.when(pl.program_id(0) == 0)
        def _():
            gw_ref[...] = part

        @pl.when(pl.program_id(0) > 0)
        def _():
            gw_ref[...] += part

    row = pl.BlockSpec((ROWS, D), lambda i: (i, 0))
    vec = pl.BlockSpec((1, D), lambda i: (0, 0))
    body, dep_ins, dep_specs = _after(body, 4, deps)
    return pl.pallas_call(
        body, name=name, grid=(LP // ROWS,), in_specs=[row, vec, row, row] + dep_specs, out_specs=[row, row, vec],
        out_shape=[_sds((LP, D), F32), _sds((LP, D), BF16), _sds((1, D), F32)],
        compiler_params=_params(("arbitrary",)),
    )(h, w, du, dres, *dep_ins)


def _loss_head(h2, wf, target):
    def body(h_ref, w_ref, t_ref, dh_ref, dhb_ref, sq_ref, gw_ref):
        i = pl.program_id(0)

        @pl.when(i == 0)
        def _():
            dh_ref[...] = jnp.zeros_like(dh_ref)
            dhb_ref[...] = jnp.zeros_like(dhb_ref)
            sq_ref[...] = jnp.zeros_like(sq_ref)
            gw_ref[...] = jnp.zeros_like(gw_ref)

        @pl.when(i > 0)
        def _():
            x = h_ref[...]
            rstd = lax.rsqrt(jnp.mean(x * x, axis=-1, keepdims=True) + EPS)
            xhat = x * rstd
            w = w_ref[...]
            err = xhat * w - t_ref[...]
            sq_ref[...] += jnp.sum(err * err, axis=0, keepdims=True)
            dy = err * (1.0 / D)
            gw_ref[...] += jnp.sum(dy * xhat, axis=0, keepdims=True)
            dxh = dy * w
            dh = rstd * (dxh - xhat * jnp.mean(dxh * xhat, axis=-1, keepdims=True))
            dh_ref[...] = dh
            dhb_ref[...] = dh.astype(BF16)

    row = pl.BlockSpec((CHUNK, D), lambda i: (i, 0))
    vec = pl.BlockSpec((1, D), lambda i: (0, 0))
    return pl.pallas_call(
        body, name="loss_head", grid=(NCH,),
        in_specs=[row, vec, pl.BlockSpec((CHUNK, D), lambda i: (jnp.maximum(i - 1, 0), 0))],
        out_specs=[row, row, vec, vec],
        out_shape=[_sds((LP, D), F32), _sds((LP, D), BF16), _sds((1, D), F32), _sds((1, D), F32)],
        compiler_params=_params(("arbitrary",)),
    )(h2, wf, target)


def _dot(a, b, dims):
    return lax.dot_general(a, b, (dims, ((), ())), preferred_element_type=F32)


NN, TN, NT = ((1,), (0,)), ((0,), (0,)), ((1,), (1,))


def _tri_sum(t, x):
    hi = x.astype(BF16)
    r1 = x - hi.astype(F32)
    mid = r1.astype(BF16)
    lo = (r1 - mid.astype(F32)).astype(BF16)
    return _dot(t, hi, NN) + _dot(t, mid, NN) + _dot(t, lo, NN)


def _gla_gates(glr_ref, gw2_ref, gb_ref, rows, row0):
    g_raw = _dot(glr_ref[rows, :].astype(BF16), gw2_ref[...], NN) + gb_ref[...]
    logsig = jnp.minimum(g_raw, 0.0) - jnp.log(1.0 + jnp.exp(-jnp.abs(g_raw)))
    rid = row0 + lax.broadcasted_iota(jnp.int32, g_raw.shape, 0)
    live = rid >= PAD
    return g_raw, jnp.where(live, logsig / TAU, 0.0), live


def _tri_masks():
    r = lax.broadcasted_iota(jnp.int32, (CHUNK, CHUNK), 0)
    c = lax.broadcasted_iota(jnp.int32, (CHUNK, CHUNK), 1)
    return r >= c


def _gla_specs(rev):
    n = NCH // CPS
    R = CPS * CHUNK
    st = (lambda s: n - 1 - s) if rev else (lambda s: s)
    return R, n, st, [
        pl.BlockSpec((R, KW), lambda s: (st(s), 0)),
        pl.BlockSpec((R, KW), lambda s: (st(s), 1)),
        pl.BlockSpec((R, GW), lambda s: (st(s), 1)),
        pl.BlockSpec((R, GW), lambda s: (st(s), 2)),
        pl.BlockSpec((R, 128), lambda s: (st(s), GLR_BLK)),
    ]


def _gla_fwd(proj, gw2p, gate_b, gnw, deps=()):
    R, n, st, pspecs = _gla_specs(False)

    def body(q_ref, k_ref, v_ref, r_ref, glr_ref, gw2_ref, gb_ref, gnw_ref, og_ref, o_ref, st_ref, state):
        s = pl.program_id(0)

        @pl.when(s == 0)
        def _():
            state[...] = jnp.zeros_like(state)

        causal = _tri_masks()
        tri = causal.astype(BF16)
        for c in range(CPS):
            rows = slice(c * CHUNK, (c + 1) * CHUNK)
            _, logg, _ = _gla_gates(glr_ref, gw2_ref, gb_ref, rows, s * R + c * CHUNK)
            G = _tri_sum(tri, logg)
            g_last = G[CHUNK - 1:CHUNK, :]
            q_dec = (q_ref[rows, :] * (DK ** -0.5) * jnp.exp(G)).astype(BF16)
            kk = k_ref[rows, :]
            k_inv = (kk * jnp.exp(-G)).astype(BF16)
            k_end = (kk * jnp.exp(g_last - G)).astype(BF16)
            decay = jnp.exp(g_last)
            for h in range(H):
                lk = slice(h * DK, (h + 1) * DK)
                lv = slice(h * DV, (h + 1) * DV)
                v = v_ref[rows, lv].astype(BF16)
                S = state[h]
                st_ref[c, h] = S
                A = jnp.where(causal, _dot(q_dec[:, lk], k_inv[:, lk], NT), 0.0).astype(BF16)
                o = _dot(A, v, NN) + _dot(q_dec[:, lk], S.astype(BF16), NT)
                state[h] = decay[:, lk] * S + _dot(v, k_end[:, lk], TN)
                o_ref[rows, lv] = o
                on = o * lax.rsqrt(jnp.mean(o * o, axis=-1, keepdims=True) + EPS) * gnw_ref[...]
                rr = r_ref[rows, lv]
                og_ref[rows, lv] = (on * (rr * jax.nn.sigmoid(rr))).astype(BF16)

    full = lambda shape: pl.BlockSpec(shape, lambda s: (0,) * len(shape))
    body, dep_ins, dep_specs = _after(body, 8, deps)
    return pl.pallas_call(
        body, name="gla_fwd", grid=(n,),
        in_specs=pspecs + [full((128, KW)), full((1, KW)), full((1, DV))] + dep_specs,
        out_specs=[pl.BlockSpec((R, GW), lambda s: (s, 0)), pl.BlockSpec((R, GW), lambda s: (s, 0)),
                   pl.BlockSpec((CPS, H, DV, DK), lambda s: (s, 0, 0, 0))],
        out_shape=[_sds((LP, GW), BF16), _sds((LP, GW), F32), _sds((NCH, H, DV, DK), F32)],
        scratch_shapes=[pltpu.VMEM((H, DV, DK), F32)],
        compiler_params=_params(("arbitrary",)),
    )(proj, proj, proj, proj, proj, gw2p, gate_b, gnw, *dep_ins)


def _gla_bwd(proj, dmixed, o_saved, st_saved, gw2p, gate_b, gnw, deps=()):
    R, n, st, pspecs = _gla_specs(True)

    def body(q_ref, k_ref, v_ref, r_ref, glr_ref, dog_ref, o_ref, st_ref, gw2_ref, gb_ref, gnw_ref,
             dqkvr_ref, dglr_ref, ggn_ref, ggb_ref, ggw_ref, gstate):
        s = pl.program_id(0)

        @pl.when(s == 0)
        def _():
            gstate[...] = jnp.zeros_like(gstate)
            ggn_ref[...] = jnp.zeros_like(ggn_ref)
            ggb_ref[...] = jnp.zeros_like(ggb_ref)
            ggw_ref[...] = jnp.zeros_like(ggw_ref)

        causal = _tri_masks()
        tri = causal.astype(BF16)
        tri_up = (lax.broadcasted_iota(jnp.int32, (CHUNK, CHUNK), 0)
                  <= lax.broadcasted_iota(jnp.int32, (CHUNK, CHUNK), 1)).astype(BF16)
        gnw = gnw_ref[...]
        for c in reversed(range(CPS)):
            rows = slice(c * CHUNK, (c + 1) * CHUNK)
            g_raw, logg, live = _gla_gates(glr_ref, gw2_ref, gb_ref, rows, (n - 1 - s) * R + c * CHUNK)
            G = _tri_sum(tri, logg)
            g_last = G[CHUNK - 1:CHUNK, :]
            e_g, e_gi, e_end = jnp.exp(G), jnp.exp(-G), jnp.exp(g_last - G)
            q_dec = q_ref[rows, :] * (DK ** -0.5) * e_g
            kk = k_ref[rows, :]
            k_inv, k_end = kk * e_gi, kk * e_end
            q_dec_b, k_inv_b, k_end_b = q_dec.astype(BF16), k_inv.astype(BF16), k_end.astype(BF16)
            decay = jnp.exp(g_last)
            d_g, d_gl = [], []
            for h in range(H):
                lk = slice(h * DK, (h + 1) * DK)
                lv = slice(h * DV, (h + 1) * DV)
                o = o_ref[rows, lv]
                rr = r_ref[rows, lv]
                dog = dog_ref[rows, lv]
                rstd = lax.rsqrt(jnp.mean(o * o, axis=-1, keepdims=True) + EPS)
                ohat = o * rstd
                sr = jax.nn.sigmoid(rr)
                don = dog * (rr * sr)
                dqkvr_ref[rows, 2 * KW + GW + h * DV:2 * KW + GW + (h + 1) * DV] = (
                    dog * (ohat * gnw) * (sr * (1.0 + rr * (1.0 - sr)))).astype(BF16)
                ggn_ref[...] += jnp.sum(don * ohat, axis=0, keepdims=True)
                dohat = don * gnw
                do = (rstd * (dohat - ohat * jnp.mean(dohat * ohat, axis=-1, keepdims=True))).astype(BF16)
                v = v_ref[rows, lv].astype(BF16)
                S = st_ref[c, h]
                gS = gstate[h]
                S_b, gS_b = S.astype(BF16), gS.astype(BF16)
                qd, ki, ke = q_dec_b[:, lk], k_inv_b[:, lk], k_end_b[:, lk]
                A = jnp.where(causal, _dot(qd, ki, NT), 0.0).astype(BF16)
                dA = jnp.where(causal, _dot(do, v, NT), 0.0).astype(BF16)
                dv = _dot(A, do, TN) + _dot(ke, gS_b, NT)
                dq_dec = _dot(dA, ki, NN) + _dot(do, S_b, NN)
                dk_inv = _dot(dA, qd, TN)
                dk_end = _dot(v, gS_b, NN)
                d_decay = jnp.sum(gS * S, axis=0, keepdims=True)
                gstate[h] = decay[:, lk] * gS + _dot(do, qd, TN)
                dqkvr_ref[rows, lk] = (dq_dec * e_g[:, lk] * (DK ** -0.5)).astype(BF16)
                dqkvr_ref[rows, KW + h * DK:KW + (h + 1) * DK] = (
                    dk_inv * e_gi[:, lk] + dk_end * e_end[:, lk]).astype(BF16)
                dqkvr_ref[rows, 2 * KW + h * DV:2 * KW + (h + 1) * DV] = dv.astype(BF16)
                ke_prod = dk_end * k_end[:, lk]
                d_g.append(dq_dec * q_dec[:, lk] - dk_inv * k_inv[:, lk] - ke_prod)
                d_gl.append(jnp.sum(ke_prod, axis=0, keepdims=True) + d_decay * decay[:, lk])
            dlogg = _tri_sum(tri_up, jnp.concatenate(d_g, axis=1)) + jnp.concatenate(d_gl, axis=1)
            dg_raw = jnp.where(live, dlogg * (1.0 / TAU) * jax.nn.sigmoid(-g_raw), 0.0)
            ggb_ref[...] += jnp.sum(dg_raw, axis=0, keepdims=True)
            dg_b = dg_raw.astype(BF16)
            ggw_ref[...] += _dot(glr_ref[rows, :].astype(BF16), dg_b, TN)
            dglr_ref[rows, :] = _dot(dg_b, gw2_ref[...], NT).astype(BF16)

    full = lambda shape: pl.BlockSpec(shape, lambda s: (0,) * len(shape))
    body, dep_ins, dep_specs = _after(body, 11, deps)
    return pl.pallas_call(
        body, name="gla_bwd", grid=(n,),
        in_specs=pspecs + [pl.BlockSpec((R, GW), lambda s: (st(s), 0)), pl.BlockSpec((R, GW), lambda s: (st(s), 0)),
                           pl.BlockSpec((CPS, H, DV, DK), lambda s: (st(s), 0, 0, 0)),
                           full((128, KW)), full((1, KW)), full((1, DV))] + dep_specs,
        out_specs=[pl.BlockSpec((R, 2 * KW + 2 * GW), lambda s: (st(s), 0)), pl.BlockSpec((R, 128), lambda s: (st(s), 0)),
                   full((1, DV)), full((1, KW)), full((128, KW))],
        out_shape=[_sds((LP, 2 * KW + 2 * GW), BF16), _sds((LP, 128), BF16),
                   _sds((1, DV), F32), _sds((1, KW), F32), _sds((128, KW), F32)],
        scratch_shapes=[pltpu.VMEM((H, DV, DK), F32)],
        compiler_params=_params(("arbitrary",)),
    )(proj, proj, proj, proj, proj, dmixed, o_saved, st_saved, gw2p, gate_b, gnw, *dep_ins)


def _pool_pre(x, win, rid):
    s, step = x, 1
    while step < win:
        s = s + pltpu.roll(s, step, 0)
        step *= 2
    cnt = jnp.clip(rid - (PAD - 1), 1, win).astype(F32)
    live = rid >= PAD
    return jnp.where(live, s / cnt - x, 0.0), cnt, live


def _pool_fwd(proj, pool_w, pool_scale):
    def body(pu_ref, w_ref, sc_ref, o_ref):
        rid = lax.broadcasted_iota(jnp.int32, (LP, GC), 0)
        for g, win in enumerate(WINDOWS):
            @pl.when(pl.program_id(0) == g)
            def _():
                y, _, _ = _pool_pre(pu_ref[...], win, rid)
                o_ref[...] = (_dot(y.astype(BF16), w_ref[...], NN) * sc_ref[...]).astype(BF16)

    col = lambda base: pl.BlockSpec((LP, GC), lambda g: (0, base + g))
    return pl.pallas_call(
        body, name="pool_fwd", grid=(len(WINDOWS),),
        in_specs=[col(POOL_BLK), pl.BlockSpec((None, GC, GC), lambda g: (g, 0, 0)),
                  pl.BlockSpec((1, GC), lambda g: (0, g))],
        out_specs=col(0), out_shape=_sds((LP, PW), BF16), compiler_params=_params(("parallel",)),
    )(proj, pool_w, pool_scale)


def _pool_bwd(proj, dmixed, pool_w, pool_scale):
    def body(pu_ref, do_ref, w_ref, sc_ref, dpu_ref, dw_ref, dsc_ref):
        rid = lax.broadcasted_iota(jnp.int32, (LP, GC), 0)
        for g, win in enumerate(WINDOWS):
            @pl.when(pl.program_id(0) == g)
            def _():
                y, cnt, live = _pool_pre(pu_ref[...], win, rid)
                y_b = y.astype(BF16)
                w = w_ref[...]
                do = do_ref[...]
                dsc_ref[...] = jnp.sum(do * _dot(y_b, w, NN), axis=0, keepdims=True)
                dyw = (do * sc_ref[...]).astype(BF16)
                dw_ref[...] = _dot(y_b, dyw, TN)
                dy = jnp.where(live, _dot(dyw, w, NT), 0.0)
                s, step = dy / cnt, 1
                while step < win:
                    s = s + pltpu.roll(s, LP - step, 0)
                    step *= 2
                dpu_ref[...] = (s - dy).astype(BF16)

    col = lambda base: pl.BlockSpec((LP, GC), lambda g: (0, base + g))
    mat = pl.BlockSpec((None, GC, GC), lambda g: (g, 0, 0))
    vec = pl.BlockSpec((1, GC), lambda g: (0, g))
    return pl.pallas_call(
        body, name="pool_bwd", grid=(len(WINDOWS),),
        in_specs=[col(POOL_BLK), col(GW // GC), mat, vec], out_specs=[col(0), mat, vec],
        out_shape=[_sds((LP, PW), BF16), _sds((4, GC, GC), F32), _sds((1, PW), F32)],
        compiler_params=_params(("parallel",)),
    )(proj, dmixed, pool_w, pool_scale)


def _adamw_math(w, g, m, v):
    m = B1 * m + (1.0 - B1) * g
    v = B2 * v + (1.0 - B2) * (g * g)
    m_hat = m / (1.0 - B1 ** STEP)
    v_hat = v / (1.0 - B2 ** STEP)
    return -LR * (m_hat / (jnp.sqrt(v_hat) + AEPS) + WD * w), m, v


def _adamw_landed(sums, landed, my_chip, w, m, v, rows, name):
    r, c = w.shape

    def body(chip_ref, s_ref, l_ref, w_ref, m_ref, v_ref, g_ref, d_ref, mo_ref, vo_ref):
        g = s_ref[...].astype(F32)
        for k in range(3):
            g = g + l_ref[k].astype(F32)
        g_ref[...] = g
        d_ref[...], mo_ref[...], vo_ref[...] = _adamw_math(w_ref[...], g, m_ref[...], v_ref[...])

    blk = pl.BlockSpec((rows, c), lambda i, chip_ref: (i, 0))
    return pl.pallas_call(
        body, name=name, out_shape=[_sds((r, c), F32)] * 4,
        grid_spec=pltpu.PrefetchScalarGridSpec(
            num_scalar_prefetch=1, grid=(r // rows,),
            in_specs=[pl.BlockSpec((None, rows, c), lambda i, chip_ref: (chip_ref[0], i, 0)),
                      pl.BlockSpec((3, rows, c), lambda i, chip_ref: (0, i, 0)), blk, blk, blk],
            out_specs=[blk] * 4),
        compiler_params=_params(("parallel",)),
    )(my_chip, sums, landed, w, m, v)


def _adamw_small(g, w, m, v):
    def body(g_ref, w_ref, m_ref, v_ref, d_ref, mo_ref, vo_ref):
        d_ref[...], mo_ref[...], vo_ref[...] = _adamw_math(w_ref[...], g_ref[...], m_ref[...], v_ref[...])

    return pl.pallas_call(body, name="adamw_small", out_shape=[_sds(w.shape, F32)] * 3)(g, w, m, v)


SMALL_REPL = (("norm1_w", D), ("norm2_w", D), ("final_norm_w", D), ("pool_scale", PW), ("gate_b", KW),
              ("gla_norm_w", DV))


def _pack_rows(vecs, rows):
    flat = jnp.concatenate([jnp.ravel(v) for v in vecs])
    return jnp.pad(flat, (0, rows * 1024 - flat.shape[0])).reshape(rows, 1024)


def kernel(x, meta_tokens, norm1_w, w_in, gate_w2, gate_b, gla_norm_w, pool_w, pool_scale, w_out, norm2_w, mlp_w1, mlp_w2, final_norm_w, loss_target, m_meta_tokens, m_norm1_w, m_w_in, m_gate_w2, m_gate_b, m_gla_norm_w, m_pool_w, m_pool_scale, m_w_out, m_norm2_w, m_mlp_w1, m_mlp_w2, m_final_norm_w, v_meta_tokens, v_norm1_w, v_w_in, v_gate_w2, v_gate_b, v_gla_norm_w, v_pool_w, v_pool_scale, v_w_out, v_norm2_w, v_mlp_w1, v_mlp_w2, v_final_norm_w):
    me = 4 * lax.axis_index("x") + 2 * lax.axis_index("y") + lax.axis_index("c")
    W = dict(meta_tokens=meta_tokens, norm1_w=norm1_w, w_in=w_in, gate_w2=gate_w2, gate_b=gate_b,
             gla_norm_w=gla_norm_w, pool_w=pool_w, pool_scale=pool_scale, w_out=w_out, norm2_w=norm2_w,
             mlp_w1=mlp_w1, mlp_w2=mlp_w2, final_norm_w=final_norm_w)
    Mo = dict(meta_tokens=m_meta_tokens, norm1_w=m_norm1_w, w_in=m_w_in, gate_w2=m_gate_w2, gate_b=m_gate_b,
              gla_norm_w=m_gla_norm_w, pool_w=m_pool_w, pool_scale=m_pool_scale, w_out=m_w_out, norm2_w=m_norm2_w,
              mlp_w1=m_mlp_w1, mlp_w2=m_mlp_w2, final_norm_w=m_final_norm_w)
    Vo = dict(meta_tokens=v_meta_tokens, norm1_w=v_norm1_w, w_in=v_w_in, gate_w2=v_gate_w2, gate_b=v_gate_b,
              gla_norm_w=v_gla_norm_w, pool_w=v_pool_w, pool_scale=v_pool_scale, w_out=v_w_out, norm2_w=v_norm2_w,
              mlp_w1=v_mlp_w1, mlp_w2=v_mlp_w2, final_norm_w=v_final_norm_w)

    names = ["w_in", "w_out", "pool_w", "mlp_w1", "mlp_w2"]
    ex = _Exchange(dict(w_in=w_in[0], w_out=w_out[0], pool_w=pool_w[0].reshape(4 * 32, GC), mlp_w1=mlp_w1[0],
                        mlp_w2=mlp_w2[0]))
    small = _exchange_small(_pack_rows([meta_tokens, gate_w2[0]], 8), False, "gather_small")
    meta_full = small[:, 0:4].reshape(NDEV, N_META, D // NDEV).transpose(1, 0, 2).reshape(N_META, D)
    gw2_full = small[:, 4].reshape(NDEV, RANK, KW // NDEV).transpose(1, 0, 2).reshape(RANK, KW)
    gw2p = jnp.pad(gw2_full, ((0, 128 - RANK), (0, 0))).astype(BF16)

    step = _layer_step(x[0], loss_target[0], meta_full, gw2p, ex, norm1_w, gate_b, gla_norm_w, pool_scale, norm2_w,
                       final_norm_w.reshape(1, D))
    grad_x = step["dh0"][ROW_X:][None]

    loss_part = 0.5 * jnp.sum(step["sq"]) / D
    packed = jnp.concatenate([
        _pack_rows([step[k] for k, _ in SMALL_REPL] + [loss_part], 8),
        step["gate_w2"][:RANK].reshape(8, 1024), step["dh0"][PAD:ROW_X].reshape(32, 1024)], axis=0)
    red = _exchange_small(packed, True, "reduce_small")
    loss = red[7, 768]
    g_gw2_mine = lax.dynamic_slice(red[8:16].reshape(RANK, KW), (0, me * (KW // NDEV)), (RANK, KW // NDEV))
    g_meta_mine = lax.dynamic_slice(red[16:48].reshape(N_META, D), (0, me * (D // NDEV)), (N_META, D // NDEV))

    shard2d = dict(w_in=(D, D_IN // NDEV), w_out=(D // NDEV, D), mlp_w1=(D, DFF // NDEV), mlp_w2=(DFF // NDEV, D),
                   pool_w=(4 * 32, GC))
    out = {}
    for group in ("down", "up", "mix", "w_in"):
        for k, (sums, landed) in ex.grad_finish(group, red).items():
            res = _adamw_landed(sums, landed, ex.my_chip, W[k].reshape(shard2d[k]), Mo[k].reshape(shard2d[k]),
                                Vo[k].reshape(shard2d[k]), SHARD_ROWS[k], "adamw_" + k)
            out[k] = [a.reshape(W[k].shape) for a in res]

    def small_pack(P):
        return jnp.concatenate([_pack_rows([P[k] for k, _ in SMALL_REPL], 8),
                                _pack_rows([P["meta_tokens"], P["gate_w2"]], 8)], axis=0)

    g_small = jnp.concatenate([red[0:8], _pack_rows([g_meta_mine, g_gw2_mine], 8)], axis=0)
    g_small = g_small.at[7, 768].set(0.0)
    res_small = _adamw_small(g_small, small_pack(W), small_pack(Mo), small_pack(Vo))
    res_small = [g_small] + list(res_small)
    off = 0
    for k, nel in SMALL_REPL:
        out[k] = [a[0:8].reshape(-1)[off:off + nel].reshape(W[k].shape) for a in res_small]
        off += nel
    out["meta_tokens"] = [a[8:12].reshape(N_META, D // NDEV) for a in res_small]
    out["gate_w2"] = [a[12].reshape(1, RANK, KW // NDEV) for a in res_small]

    order = ["meta_tokens", "norm1_w", "w_in", "gate_w2", "gate_b", "gla_norm_w", "pool_w", "pool_scale", "w_out",
             "norm2_w", "mlp_w1", "mlp_w2", "final_norm_w"]
    return (loss, grad_x, *[out[k][0] for k in order], *[out[k][1] for k in order],
            *[out[k][2] for k in order], *[out[k][3] for k in order])


SHARD_ROWS = dict(w_in=256, w_out=64, mlp_w1=128, mlp_w2=64, pool_w=128)
C_GLR = 2 * KW + 2 * GW
GATHER_GROUPS = dict(w_in=("w_in",), mix=("w_out", "pool_w"), up=("mlp_w1",), down=("mlp_w2",))
GRAD_GROUPS = dict(down=("mlp_w2",), up=("mlp_w1",), mix=("w_out",), w_in=("pool_w", "w_in"))


class _Exchange:
    def __init__(self, shards):
        names = list(shards)
        started, _ = _gather_start([shards[k].astype(BF16) for k in names], "gather_start")
        self.state = dict(zip(names, started))
        self.my_c = lax.axis_index("c").astype(jnp.int32).reshape(1)
        self.my_chip = (2 * lax.axis_index("x") + lax.axis_index("y")).astype(jnp.int32).reshape(1)
        self.sibling, self.chips = {}, {}

    def forward(self, group, after):
        ks = GATHER_GROUPS[group]
        fwd, token = _gather_forward([self.state[k] for k in ks], after, "gather_forward_" + group)
        self.state.update(zip(ks, fwd))
        return token

    def weights(self, group, after):
        ks = GATHER_GROUPS[group]
        g = dict(zip(ks, _gather_finish([self.state[k] for k in ks], after, "gather_finish_" + group)))
        if group == "w_in":
            nat = g["w_in"].transpose(1, 0, 2).reshape(D, D_IN)
            return jnp.concatenate([nat[:, :C_GLR], nat[:, C_GLR + RANK:], nat[:, C_GLR:C_GLR + RANK],
                                    jnp.zeros((D, D_INP - D_IN), BF16)], axis=1)
        if group == "mix":
            return (g["w_out"].reshape(D, D),
                    g["pool_w"].reshape(NDEV, 4, 32, GC).transpose(1, 0, 2, 3).reshape(4, GC, GC))
        return g["mlp_w1"] if group == "up" else g["mlp_w2"].reshape(DFF, D)

    def grad(self, group, grads):
        parts = dict(grads)
        if group == "w_in":
            g = parts["w_in"]
            nat = jnp.concatenate([g[:, :C_GLR], g[:, C_GLR + PW:C_GLR + PW + RANK], g[:, C_GLR:C_GLR + PW]], axis=1)
            parts["w_in"] = nat.reshape(D, 4, 2, D_IN // NDEV).transpose(2, 1, 0, 3)
            parts["pool_w"] = (parts["pool_w"].astype(BF16).reshape(4, 4, 2, 32, GC).transpose(2, 1, 0, 3, 4)
                               .reshape(2, 4, 4 * 32, GC))
        ks = GRAD_GROUPS[group]
        started, token = _to_sibling_start([parts[k] for k in ks], "grad_sibling_start_" + group)
        self.sibling[group] = started
        return token

    def grad_mid(self, group, after):
        ks = GRAD_GROUPS[group]
        both = _to_sibling_finish(self.sibling[group], after, "grad_sibling_finish_" + group)
        sums = [_chip_sum(p, s, self.my_c, SHARD_ROWS[k], "chip_sum_" + k) for k, (p, s) in zip(ks, both)]
        self.chips[group], token = _to_chips_start(sums, "grad_chips_start_" + group)
        return token

    def grad_finish(self, group, after):
        done = _to_chips_finish(self.chips[group], after, "grad_chips_finish_" + group)
        return dict(zip(GRAD_GROUPS[group], done))


def _layer_step(x, target, meta_full, gw2p, ex, norm1_w, gate_b, gla_norm_w, pool_scale, norm2_w, final_norm_w):
    h0 = jnp.concatenate([jnp.zeros((PAD, D), F32), meta_full, x], axis=0)
    u1 = _rmsnorm_fwd(h0, norm1_w, "rmsnorm1")
    win_p = ex.weights("w_in", ex.forward("w_in", u1))
    proj = _matmul(u1, win_p, mode="nn", tm=1056, tn=1408, tk=512, name="proj")
    tok = ex.forward("mix", proj)
    og, o_saved, st_saved = _gla_fwd(proj, gw2p, gate_b, gla_norm_w, deps=[tok])
    wout_f, poolw_f = ex.weights("mix", og)
    op = _pool_fwd(proj, poolw_f, pool_scale)
    mixed = jnp.concatenate([og, op], axis=1)
    h1 = _matmul(mixed, wout_f, mode="nn", tm=1056, tn=1024, tk=512, name="mix_out", epi="add", extra=h0)
    tok = ex.forward("up", h1)
    u2 = _rmsnorm_fwd(h1, norm2_w, "rmsnorm2", deps=[tok])
    w1_g = ex.weights("up", u2)
    z, act = _matmul(u2, w1_g, mode="nn", tm=1056, tn=1024, tk=512, name="mlp_up", epi="relu2", b_slots=True)
    w2_f = ex.weights("down", ex.forward("down", act))
    h2 = _matmul(act, w2_f, mode="nn", tm=1056, tn=1024, tk=512, name="mlp_down", epi="add", extra=h1)
    dh2, dh2b, sq, g_fnw = _loss_head(h2, final_norm_w, target)

    g_w2 = _matmul(act, dh2b, mode="tn", tm=512, tn=1024, tk=LP, name="d_mlp_w2", out_dtype=BF16, out_slots="rows")
    tok = ex.grad("down", dict(mlp_w2=g_w2))
    dz = _matmul(dh2b, w2_f, mode="nt", tm=1056, tn=1024, tk=512, name="d_act", out_dtype=BF16, epi="dz", extra=z,
                 deps=[tok])
    tok = ex.grad_mid("down", dz)
    g_w1 = _matmul(u2, dz, mode="tn", tm=512, tn=1024, tk=LP, name="d_mlp_w1", out_dtype=BF16, out_slots="cols",
                   deps=[tok])
    tok = ex.grad("up", dict(mlp_w1=g_w1))
    du2 = _matmul(dz, w1_g, mode="nt", tm=1056, tn=1024, tk=1024, name="d_u2", b_slots=True, deps=[tok])
    tok = ex.grad_mid("up", du2)
    dh1, dh1b, g_n2 = _rmsnorm_bwd(h1, norm2_w, du2, dh2, "rmsnorm2_bwd", deps=[tok])
    g_wout = _matmul(mixed, dh1b, mode="tn", tm=256, tn=1024, tk=LP, name="d_w_out", out_dtype=BF16, out_slots="rows")
    tok = ex.grad("mix", dict(w_out=g_wout))
    dmixed = _matmul(dh1b, wout_f, mode="nt", tm=1056, tn=1024, tk=512, name="d_mixed", deps=[tok])
    tok = ex.grad_mid("mix", dmixed)
    dqkvr, dglr, g_gnw, g_gb, g_gw2 = _gla_bwd(proj, dmixed, o_saved, st_saved, gw2p, gate_b, gla_norm_w, deps=[tok])
    dpu, g_poolw, g_psc = _pool_bwd(proj, dmixed, poolw_f, pool_scale)
    dproj = jnp.concatenate([dqkvr, dpu, dglr], axis=1)
    g_win_p = _matmul(u1, dproj, mode="tn", tm=512, tn=1408, tk=LP, name="d_w_in", out_dtype=BF16)
    tok = ex.grad("w_in", dict(pool_w=g_poolw, w_in=g_win_p))
    du1 = _matmul(dproj, win_p, mode="nt", tm=1056, tn=1024, tk=1408, name="d_u1", deps=[tok])
    tok = ex.grad_mid("w_in", du1)
    dh0, _, g_n1 = _rmsnorm_bwd(h0, norm1_w, du1, dh1, "rmsnorm1_bwd", deps=[tok])
    return dict(dh0=dh0, sq=sq, gate_w2=g_gw2, norm1_w=g_n1, norm2_w=g_n2, final_norm_w=g_fnw, pool_scale=g_psc, gate_b=g_gb, gla_norm_w=g_gnw)
```

```python
import functools

import jax
import jax.numpy as jnp
from jax import lax
from jax.experimental import pallas as pl
from jax.experimental.pallas import tpu as pltpu

F32, BF16 = jnp.float32, jnp.bfloat16
MESH = pl.DeviceIdType.MESH

NDEV = 8
D = 2048
SEQ = 2048
N_META = 16
CHUNK = 64
PAD = (-N_META) % CHUNK
ROW_X = PAD + N_META
LP = ROW_X + SEQ
NCH = LP // CHUNK
H = 4
DK = 128
DV = 256
KW = H * DK
GW = H * DV
PW = 1024
RANK = 16
TAU = 16.0
WINDOWS = (2, 4, 8, 16)
GC = 256
DFF = 4 * D
EPS = 1e-6
D_IN = 2 * KW + 2 * GW + RANK + PW
D_INP = 4224
GLR_BLK = (2 * KW + 2 * GW + PW) // 128
POOL_BLK = (2 * KW + 2 * GW) // GC
LR, B1, B2, AEPS, WD, STEP = 0.001, 0.9, 0.999, 1e-08, 0.01, 10
VMEM_LIMIT = 48 * 1024 * 1024
CPS = 3


def _params(sem=None):
    return pltpu.CompilerParams(dimension_semantics=sem, vmem_limit_bytes=VMEM_LIMIT)


def _sds(shape, dtype):
    return jax.ShapeDtypeStruct(shape, dtype)


def _me():
    return lax.axis_index("x"), lax.axis_index("y"), lax.axis_index("c")


def _peer(j):
    x, y, c = _me()
    return (x ^ ((j >> 2) & 1), y ^ ((j >> 1) & 1), c ^ (j & 1))


def _slot(dev):
    return 4 * dev[0] + 2 * dev[1] + dev[2]


HBM_SPEC = pl.BlockSpec(memory_space=pltpu.HBM)
SEM_SPEC = pl.BlockSpec(memory_space=pltpu.SEMAPHORE)
ANY_SPEC = pl.BlockSpec(memory_space=pl.ANY)
EFFECT = pltpu.SideEffectType.DATAFLOW_SIDE_EFFECTING
SIBLING = 1
OTHER_CHIPS = (2, 4, 6)


def _in_hbm(a):
    return pltpu.with_memory_space_constraint(a, pltpu.HBM)


def _chip(dev):
    return 2 * dev[0] + dev[1]


def _rcopy(src, dst, send_sem, recv_sem, to):
    return pltpu.make_async_remote_copy(src_ref=src, dst_ref=dst, send_sem=send_sem, recv_sem=recv_sem,
                                        device_id=to, device_id_type=MESH)


def _split_call(body, name, ins, in_specs, out_shape, out_specs, aliases, scratch=()):
    n = len(ins) + len(out_shape)

    def with_token(*refs):
        body(*refs[:n], *refs[n + 1:])
        refs[n][...] = jnp.zeros_like(refs[n])

    return pl.pallas_call(
        with_token, name=name, in_specs=in_specs, out_shape=list(out_shape) + [_sds((8, 128), F32)],
        out_specs=list(out_specs) + [pl.BlockSpec(memory_space=pltpu.VMEM)],
        input_output_aliases=aliases, scratch_shapes=list(scratch),
        compiler_params=pltpu.CompilerParams(has_side_effects=EFFECT),
    )(*ins)


def _after(body, n_in, deps):
    deps = [d for d in deps if d is not None]
    if not deps:
        return body, [], []
    return (lambda *refs: body(*refs[:n_in], *refs[n_in + len(deps):])), deps, [ANY_SPEC] * len(deps)


def _gather_start(shards, name):
    n = len(shards)
    me = _slot(_me())
    lands = [lax.dynamic_update_slice(lax.empty((NDEV,) + s.shape, s.dtype), s[None], (me, 0, 0)) for s in shards]

    def body(*refs):
        src, land = refs[:n], refs[n:2 * n]
        outs = refs[2 * n:]
        for i in range(n):
            send_sems, recv_sems = outs[4 * i], outs[4 * i + 1]
            for k, rel in enumerate((SIBLING,) + OTHER_CHIPS):
                _rcopy(src[i], land[i].at[_slot(_me())], send_sems.at[k], recv_sems.at[k], _peer(rel)).start()

    out_shape, out_specs, aliases = [], [], {}
    for i, s in enumerate(shards):
        out_shape += [pltpu.SemaphoreType.DMA((4,)), pltpu.SemaphoreType.DMA((4,)), pltpu.HBM(s.shape, s.dtype),
                      pltpu.HBM((NDEV,) + s.shape, s.dtype)]
        out_specs += [SEM_SPEC, SEM_SPEC, HBM_SPEC, HBM_SPEC]
        aliases[i] = 4 * i + 2
        aliases[n + i] = 4 * i + 3
    res = _split_call(body, name, [_in_hbm(s) for s in shards] + [_in_hbm(l) for l in lands], [HBM_SPEC] * (2 * n),
                      out_shape, out_specs, aliases)
    return [tuple(res[4 * i:4 * i + 4]) for i in range(n)], res[-1]


def _gather_forward(started, after, name):
    n = len(started)

    def body(*refs):
        land, recv1 = refs[:n], refs[n:2 * n]
        outs = refs[2 * n + 1:]
        for i in range(n):
            send2, recv2 = outs[3 * i + 1], outs[3 * i + 2]
            for k, rel in enumerate(OTHER_CHIPS):
                blk = land[i].at[_slot(_peer(rel))]
                _rcopy(blk, blk, send2.at[k], recv1[i].at[1 + k], _peer(rel)).wait_recv()
                _rcopy(blk, blk, send2.at[k], recv2.at[k], _peer(SIBLING)).start()

    ins = [_in_hbm(st[3]) for st in started] + [st[1] for st in started] + [after]
    out_shape, out_specs, aliases = [], [], {}
    for i, st in enumerate(started):
        out_shape += [pltpu.HBM(st[3].shape, st[3].dtype), pltpu.SemaphoreType.DMA((3,)), pltpu.SemaphoreType.DMA((3,))]
        out_specs += [HBM_SPEC, SEM_SPEC, SEM_SPEC]
        aliases[i] = 3 * i
    res = _split_call(body, name, ins, [HBM_SPEC] * n + [SEM_SPEC] * n + [ANY_SPEC], out_shape, out_specs, aliases)
    return [(st[0], st[1], st[2], res[3 * i], res[3 * i + 1], res[3 * i + 2]) for i, st in enumerate(started)], res[-1]


def _gather_finish(forwarded, after, name):
    n = len(forwarded)

    def body(*refs):
        for i in range(n):
            send1, recv1, src, land, send2, recv2 = refs[6 * i:6 * i + 6]
            me = _slot(_me())
            sib = _slot(_peer(SIBLING))
            for k, rel in enumerate((SIBLING,) + OTHER_CHIPS):
                _rcopy(src, land.at[me], send1.at[k], recv1.at[k], _peer(rel)).wait_send()
            _rcopy(src, land.at[sib], send1.at[0], recv1.at[0], _peer(SIBLING)).wait_recv()
            for k, rel in enumerate(OTHER_CHIPS):
                mine, theirs = land.at[_slot(_peer(rel))], land.at[_slot(_peer(rel ^ SIBLING))]
                _rcopy(mine, mine, send2.at[k], recv2.at[k], _peer(SIBLING)).wait_send()
                _rcopy(theirs, theirs, send2.at[k], recv2.at[k], _peer(SIBLING)).wait_recv()

    ins, in_specs, out_shape, aliases = [], [], [], {}
    for i, f in enumerate(forwarded):
        ins += [f[0], f[1], _in_hbm(f[2]), _in_hbm(f[3]), f[4], f[5]]
        in_specs += [SEM_SPEC, SEM_SPEC, HBM_SPEC, HBM_SPEC, SEM_SPEC, SEM_SPEC]
        out_shape.append(pltpu.HBM(f[3].shape, f[3].dtype))
        aliases[6 * i + 3] = i
    res = _split_call(body, name, ins + [after], in_specs + [ANY_SPEC], out_shape, [HBM_SPEC] * n, aliases)
    return list(res[:-1])


def _to_sibling_start(parts, name):
    n = len(parts)
    lands = [lax.empty(p.shape[1:], p.dtype) for p in parts]

    def body(*refs):
        src, land = refs[:n], refs[n:2 * n]
        outs = refs[2 * n:]
        other = 1 - lax.axis_index("c")
        for i in range(n):
            _rcopy(src[i].at[other], land[i], outs[4 * i], outs[4 * i + 1], _peer(SIBLING)).start()

    out_shape, out_specs, aliases = [], [], {}
    for i, p in enumerate(parts):
        out_shape += [pltpu.SemaphoreType.DMA(()), pltpu.SemaphoreType.DMA(()), pltpu.HBM(p.shape, p.dtype),
                      pltpu.HBM(p.shape[1:], p.dtype)]
        out_specs += [SEM_SPEC, SEM_SPEC, HBM_SPEC, HBM_SPEC]
        aliases[i] = 4 * i + 2
        aliases[n + i] = 4 * i + 3
    res = _split_call(body, name, [_in_hbm(p) for p in parts] + [_in_hbm(l) for l in lands], [HBM_SPEC] * (2 * n),
                      out_shape, out_specs, aliases)
    return [tuple(res[4 * i:4 * i + 4]) for i in range(n)], res[-1]


def _to_sibling_finish(started, after, name):
    n = len(started)

    def body(*refs):
        for i in range(n):
            send, recv, src, land = refs[4 * i:4 * i + 4]
            cp = _rcopy(src.at[0], land, send, recv, _peer(SIBLING))
            cp.wait_send()
            cp.wait_recv()

    ins, in_specs, out_shape, aliases = [], [], [], {}
    for i, st in enumerate(started):
        ins += [st[0], st[1], _in_hbm(st[2]), _in_hbm(st[3])]
        in_specs += [SEM_SPEC, SEM_SPEC, HBM_SPEC, HBM_SPEC]
        out_shape += [pltpu.HBM(st[2].shape, st[2].dtype), pltpu.HBM(st[3].shape, st[3].dtype)]
        aliases[4 * i + 2] = 2 * i
        aliases[4 * i + 3] = 2 * i + 1
    res = _split_call(body, name, ins + [after], in_specs + [ANY_SPEC], out_shape, [HBM_SPEC] * (2 * n), aliases)
    return [(res[2 * i], res[2 * i + 1]) for i in range(n)]


def _chip_sum(parts, from_sibling, my_c, tile, name):
    _, _, r, c = parts.shape
    tr, tc = tile

    def body(c_ref, p_ref, s_ref, o_ref):
        o_ref[...] = (p_ref[...].astype(F32) + s_ref[...].astype(F32)).astype(o_ref.dtype)

    blk = pl.BlockSpec((4, tr, tc), lambda i, j, c_ref: (0, i, j))
    return pl.pallas_call(
        body, name=name, out_shape=_sds((4, r, c), parts.dtype),
        grid_spec=pltpu.PrefetchScalarGridSpec(
            num_scalar_prefetch=1, grid=(r // tr, c // tc),
            in_specs=[pl.BlockSpec((None, 4, tr, tc), lambda i, j, c_ref: (c_ref[0], 0, i, j)), blk], out_specs=blk),
        compiler_params=_params(("parallel", "parallel")),
    )(my_c, parts, from_sibling)


def _landed_sum(sums, landed, my_chip, tile, name):
    _, r, c = sums.shape
    tr, tc = tile

    def body(chip_ref, s_ref, l_ref, o_ref):
        g = s_ref[...].astype(F32)
        for k in range(3):
            g = g + l_ref[k].astype(F32)
        o_ref[...] = g

    return pl.pallas_call(
        body, name=name, out_shape=_sds((r, c), F32),
        grid_spec=pltpu.PrefetchScalarGridSpec(
            num_scalar_prefetch=1, grid=(r // tr, c // tc),
            in_specs=[pl.BlockSpec((None, tr, tc), lambda i, j, chip_ref: (chip_ref[0], i, j)),
                      pl.BlockSpec((3, tr, tc), lambda i, j, chip_ref: (0, i, j))],
            out_specs=pl.BlockSpec((tr, tc), lambda i, j, chip_ref: (i, j))),
        compiler_params=_params(("parallel", "parallel")),
    )(my_chip, sums, landed)


def _to_chips_start(sums, name):
    n = len(sums)
    lands = [lax.empty((3,) + s.shape[1:], s.dtype) for s in sums]

    def body(*refs):
        src, land = refs[:n], refs[n:2 * n]
        outs = refs[2 * n:]
        for i in range(n):
            for k, rel in enumerate(OTHER_CHIPS):
                to = _peer(rel)
                _rcopy(src[i].at[_chip(to)], land[i].at[k], outs[4 * i].at[k], outs[4 * i + 1].at[k], to).start()

    out_shape, out_specs, aliases = [], [], {}
    for i, s in enumerate(sums):
        out_shape += [pltpu.SemaphoreType.DMA((3,)), pltpu.SemaphoreType.DMA((3,)), pltpu.HBM(s.shape, s.dtype),
                      pltpu.HBM((3,) + s.shape[1:], s.dtype)]
        out_specs += [SEM_SPEC, SEM_SPEC, HBM_SPEC, HBM_SPEC]
        aliases[i] = 4 * i + 2
        aliases[n + i] = 4 * i + 3
    res = _split_call(body, name, [_in_hbm(s) for s in sums] + [_in_hbm(l) for l in lands], [HBM_SPEC] * (2 * n),
                      out_shape, out_specs, aliases)
    return [tuple(res[4 * i:4 * i + 4]) for i in range(n)], res[-1]


def _to_chips_finish(started, after, name):
    n = len(started)

    def body(*refs):
        for i in range(n):
            send, recv, src, land = refs[4 * i:4 * i + 4]
            for k, rel in enumerate(OTHER_CHIPS):
                cp = _rcopy(src.at[0], land.at[k], send.at[k], recv.at[k], _peer(rel))
                cp.wait_send()
                cp.wait_recv()

    ins, in_specs, out_shape, aliases = [], [], [], {}
    for i, st in enumerate(started):
        ins += [st[0], st[1], _in_hbm(st[2]), _in_hbm(st[3])]
        in_specs += [SEM_SPEC, SEM_SPEC, HBM_SPEC, HBM_SPEC]
        out_shape += [pltpu.HBM(st[2].shape, st[2].dtype), pltpu.HBM(st[3].shape, st[3].dtype)]
        aliases[4 * i + 2] = 2 * i
        aliases[4 * i + 3] = 2 * i + 1
    res = _split_call(body, name, ins + [after], in_specs + [ANY_SPEC], out_shape, [HBM_SPEC] * (2 * n), aliases)
    return [(res[2 * i], res[2 * i + 1]) for i in range(n)]


def _exchange_small(v, reduce, name, deps=()):
    r, c = v.shape

    def body(v_ref, o_ref, land, send_sems, recv_sems):
        me = _slot(_me())
        copies = []
        for j in range(1, NDEV):
            cp = pltpu.make_async_remote_copy(
                src_ref=v_ref, dst_ref=land.at[me], send_sem=send_sems.at[j - 1],
                recv_sem=recv_sems.at[j - 1], device_id=_peer(j), device_id_type=MESH)
            cp.start()
            copies.append(cp)
        land[me] = v_ref[...]
        for cp in copies:
            cp.wait()
        if reduce:
            acc = land[0]
            for k in range(1, NDEV):
                acc = acc + land[k]
            o_ref[...] = acc
        else:
            o_ref[...] = land[...]

    vm = pl.BlockSpec(memory_space=pltpu.VMEM)
    body, dep_ins, dep_specs = _after(body, 1, deps)
    return pl.pallas_call(
        body, name=name,
        out_shape=_sds((r, c) if reduce else (NDEV, r, c), F32),
        in_specs=[vm] + dep_specs, out_specs=vm,
        scratch_shapes=[pltpu.VMEM((NDEV, r, c), F32), pltpu.SemaphoreType.DMA((NDEV - 1,)),
                        pltpu.SemaphoreType.DMA((NDEV - 1,))],
        compiler_params=_params(),
    )(v, *dep_ins)


def _matmul(a, b, *, mode, tm, tn, tk, name, out_dtype=F32, epi=None, extra=None, b_slots=False, out_slots=False,
            deps=()):
    slot_w = b.shape[-1] if b_slots else None
    if mode == "nn":
        M, K = a.shape
        N = NDEV * slot_w if b_slots else b.shape[1]
    elif mode == "tn":
        K, M = a.shape
        N = b.shape[1]
    else:
        M, K = a.shape
        N = b.shape[-2]
        if b_slots:
            assert K == NDEV * slot_w and tk == slot_w
    if mode == "nn" and b_slots:
        assert tn == slot_w
    if out_slots == "cols":
        assert tn * NDEV == N
    if out_slots == "rows":
        assert (M // NDEV) % tm == 0
    assert M % tm == 0 and N % tn == 0 and K % tk == 0, (name, M, N, K, tm, tn, tk)
    nk = K // tk
    dims = {"nn": ((1,), (0,)), "tn": ((0,), (0,)), "nt": ((1,), (1,))}[mode]

    if mode == "tn":
        a_spec = pl.BlockSpec((tk, tm), lambda i, j, k: (k, i))
    else:
        a_spec = pl.BlockSpec((tm, tk), lambda i, j, k: (i, k))
    if mode == "nt":
        b_spec = (pl.BlockSpec((None, tn, tk), lambda i, j, k: (k, j, 0)) if b_slots
                  else pl.BlockSpec((tn, tk), lambda i, j, k: (j, k)))
    else:
        b_spec = (pl.BlockSpec((None, tk, tn), lambda i, j, k: (j, k, 0)) if b_slots
                  else pl.BlockSpec((tk, tn), lambda i, j, k: (k, j)))
    tile = pl.BlockSpec((tm, tn), lambda i, j, k: (i, j))
    if out_slots == "cols":
        out_spec = pl.BlockSpec((None, None, tm, tn), lambda i, j, k: (j % 2, j // 2, i, 0))
        out_shape = _sds((2, 4, M, tn), out_dtype)
    elif out_slots == "rows":
        per = M // NDEV // tm
        out_spec = pl.BlockSpec((None, None, tm, tn), lambda i, j, k: ((i // per) % 2, (i // per) // 2, i % per, j))
        out_shape = _sds((2, 4, M // NDEV, N), out_dtype)
    else:
        out_spec, out_shape = tile, _sds((M, N), out_dtype)
    ins, in_specs = [a, b], [a_spec, b_spec]
    if epi in ("add", "dz"):
        ins.append(extra)
        in_specs.append(tile)
    if epi == "relu2":
        out_specs, out_shapes = [tile, tile], [_sds((M, N), F32), _sds((M, N), BF16)]
    else:
        out_specs, out_shapes = out_spec, out_shape
    n_in = len(ins)

    def body(*refs):
        outs = refs[n_in:-1] if nk > 1 else refs[n_in:]

        def finish(p):
            if epi is None:
                outs[0][...] = p.astype(out_dtype)
            elif epi == "add":
                outs[0][...] = (p + refs[2][...]).astype(out_dtype)
            elif epi == "relu2":
                outs[0][...] = p
                rz = jnp.maximum(p, 0.0)
                outs[1][...] = (rz * rz).astype(BF16)
            else:
                outs[0][...] = (p * (2.0 * jnp.maximum(refs[2][...], 0.0))).astype(out_dtype)

        def product():
            return lax.dot_general(refs[0][...].astype(BF16), refs[1][...].astype(BF16), (dims, ((), ())),
                                   preferred_element_type=F32)

        if nk == 1:
            finish(product())
            return
        acc = refs[-1]
        k = pl.program_id(2)

        @pl.when(k == 0)
        def _():
            acc[...] = jnp.zeros_like(acc)

        acc[...] += product()

        @pl.when(k == nk - 1)
        def _():
            finish(acc[...])

    body, dep_ins, dep_specs = _after(body, n_in, deps)
    return pl.pallas_call(
        body, name=name, grid=(M // tm, N // tn, nk),
        in_specs=in_specs + dep_specs, out_specs=out_specs, out_shape=out_shapes,
        scratch_shapes=[pltpu.VMEM((tm, tn), F32)] if nk > 1 else [],
        compiler_params=_params(("parallel", "parallel", "arbitrary")),
    )(*ins, *dep_ins)


ROWS = 352


def _rmsnorm_fwd(h, w, name, deps=()):
    def body(h_ref, w_ref, u_ref):
        x = h_ref[...]
        rstd = lax.rsqrt(jnp.mean(x * x, axis=-1, keepdims=True) + EPS)
        u_ref[...] = (x * rstd * w_ref[...]).astype(BF16)

    row = pl.BlockSpec((ROWS, D), lambda i: (i, 0))
    body, dep_ins, dep_specs = _after(body, 2, deps)
    return pl.pallas_call(
        body, name=name, grid=(LP // ROWS,), in_specs=[row, pl.BlockSpec((1, D), lambda i: (0, 0))] + dep_specs,
        out_specs=row, out_shape=_sds((LP, D), BF16), compiler_params=_params(("parallel",)),
    )(h, w, *dep_ins)


def _rmsnorm_bwd(h, w, du, dres, name, deps=()):
    def body(h_ref, w_ref, du_ref, dres_ref, dh_ref, dhb_ref, gw_ref):
        x = h_ref[...]
        rstd = lax.rsqrt(jnp.mean(x * x, axis=-1, keepdims=True) + EPS)
        xhat = x * rstd
        dy = du_ref[...]
        dxh = dy * w_ref[...]
        dh = dres_ref[...] + rstd * (dxh - xhat * jnp.mean(dxh * xhat, axis=-1, keepdims=True))
        dh_ref[...] = dh
        dhb_ref[...] = dh.astype(BF16)
        part = jnp.sum(dy * xhat, axis=0, keepdims=True)

        @pl.when(pl.program_id(0) == 0)
        def _():
            gw_ref[...] = part

        @pl.when(pl.program_id(0) > 0)
        def _():
            gw_ref[...] += part

    row = pl.BlockSpec((ROWS, D), lambda i: (i, 0))
    vec = pl.BlockSpec((1, D), lambda i: (0, 0))
    body, dep_ins, dep_specs = _after(body, 4, deps)
    return pl.pallas_call(
        body, name=name, grid=(LP // ROWS,), in_specs=[row, vec, row, row] + dep_specs, out_specs=[row, row, vec],
        out_shape=[_sds((LP, D), F32), _sds((LP, D), BF16), _sds((1, D), F32)],
        compiler_params=_params(("arbitrary",)),
    )(h, w, du, dres, *dep_ins)


def _loss_head(h2, wf, target):
    def body(h_ref, w_ref, t_ref, dh_ref, dhb_ref, sq_ref, gw_ref):
        i = pl.program_id(0)

        @pl.when(i == 0)
        def _():
            dh_ref[...] = jnp.zeros_like(dh_ref)
            dhb_ref[...] = jnp.zeros_like(dhb_ref)
            sq_ref[...] = jnp.zeros_like(sq_ref)
            gw_ref[...] = jnp.zeros_like(gw_ref)

        @pl.when(i > 0)
        def _():
            x = h_ref[...]
            rstd = lax.rsqrt(jnp.mean(x * x, axis=-1, keepdims=True) + EPS)
            xhat = x * rstd
            w = w_ref[...]
            err = xhat * w - t_ref[...]
            sq_ref[...] += jnp.sum(err * err, axis=0, keepdims=True)
            dy = err * (1.0 / D)
            gw_ref[...] += jnp.sum(dy * xhat, axis=0, keepdims=True)
            dxh = dy * w
            dh = rstd * (dxh - xhat * jnp.mean(dxh * xhat, axis=-1, keepdims=True))
            dh_ref[...] = dh
            dhb_ref[...] = dh.astype(BF16)

    row = pl.BlockSpec((CHUNK, D), lambda i: (i, 0))
    vec = pl.BlockSpec((1, D), lambda i: (0, 0))
    return pl.pallas_call(
        body, name="loss_head", grid=(NCH,),
        in_specs=[row, vec, pl.BlockSpec((CHUNK, D), lambda i: (jnp.maximum(i - 1, 0), 0))],
        out_specs=[row, row, vec, vec],
        out_shape=[_sds((LP, D), F32), _sds((LP, D), BF16), _sds((1, D), F32), _sds((1, D), F32)],
        compiler_params=_params(("arbitrary",)),
    )(h2, wf, target)


def _dot(a, b, dims):
    return lax.dot_general(a, b, (dims, ((), ())), preferred_element_type=F32)


NN, TN, NT = ((1,), (0,)), ((0,), (0,)), ((1,), (1,))


def _tri_sum(t, x):
    hi = x.astype(BF16)
    r1 = x - hi.astype(F32)
    mid = r1.astype(BF16)
    lo = (r1 - mid.astype(F32)).astype(BF16)
    return _dot(t, hi, NN) + _dot(t, mid, NN) + _dot(t, lo, NN)


def _gla_gates(glr_ref, gw2_ref, gb_ref, rows, row0):
    g_raw = _dot(glr_ref[rows, :].astype(BF16), gw2_ref[...], NN) + gb_ref[...]
    logsig = jnp.minimum(g_raw, 0.0) - jnp.log(1.0 + jnp.exp(-jnp.abs(g_raw)))
    rid = row0 + lax.broadcasted_iota(jnp.int32, g_raw.shape, 0)
    live = rid >= PAD
    return g_raw, jnp.where(live, logsig / TAU, 0.0), live


def _tri_masks():
    r = lax.broadcasted_iota(jnp.int32, (CHUNK, CHUNK), 0)
    c = lax.broadcasted_iota(jnp.int32, (CHUNK, CHUNK), 1)
    return r >= c


def _gla_specs(rev):
    n = NCH // CPS
    R = CPS * CHUNK
    st = (lambda s: n - 1 - s) if rev else (lambda s: s)
    return R, n, st, [
        pl.BlockSpec((R, KW), lambda s: (st(s), 0)),
        pl.BlockSpec((R, KW), lambda s: (st(s), 1)),
        pl.BlockSpec((R, GW), lambda s: (st(s), 1)),
        pl.BlockSpec((R, GW), lambda s: (st(s), 2)),
        pl.BlockSpec((R, 128), lambda s: (st(s), GLR_BLK)),
    ]


def _gla_fwd(proj, gw2p, gate_b, gnw, deps=()):
    R, n, st, pspecs = _gla_specs(False)

    def body(q_ref, k_ref, v_ref, r_ref, glr_ref, gw2_ref, gb_ref, gnw_ref, og_ref, o_ref, st_ref, state):
        s = pl.program_id(0)

        @pl.when(s == 0)
        def _():
            state[...] = jnp.zeros_like(state)

        causal = _tri_masks()
        tri = causal.astype(BF16)
        for c in range(CPS):
            rows = slice(c * CHUNK, (c + 1) * CHUNK)
            _, logg, _ = _gla_gates(glr_ref, gw2_ref, gb_ref, rows, s * R + c * CHUNK)
            G = _tri_sum(tri, logg)
            g_last = G[CHUNK - 1:CHUNK, :]
            q_dec = (q_ref[rows, :] * (DK ** -0.5) * jnp.exp(G)).astype(BF16)
            kk = k_ref[rows, :]
            k_inv = (kk * jnp.exp(-G)).astype(BF16)
            k_end = (kk * jnp.exp(g_last - G)).astype(BF16)
            decay = jnp.exp(g_last)
            for h in range(H):
                lk = slice(h * DK, (h + 1) * DK)
                lv = slice(h * DV, (h + 1) * DV)
                v = v_ref[rows, lv].astype(BF16)
                S = state[h]
                st_ref[c, h] = S
                A = jnp.where(causal, _dot(q_dec[:, lk], k_inv[:, lk], NT), 0.0).astype(BF16)
                o = _dot(A, v, NN) + _dot(q_dec[:, lk], S.astype(BF16), NT)
                state[h] = decay[:, lk] * S + _dot(v, k_end[:, lk], TN)
                o_ref[rows, lv] = o
                on = o * lax.rsqrt(jnp.mean(o * o, axis=-1, keepdims=True) + EPS) * gnw_ref[...]
                rr = r_ref[rows, lv]
                og_ref[rows, lv] = (on * (rr * jax.nn.sigmoid(rr))).astype(BF16)

    full = lambda shape: pl.BlockSpec(shape, lambda s: (0,) * len(shape))
    body, dep_ins, dep_specs = _after(body, 8, deps)
    return pl.pallas_call(
        body, name="gla_fwd", grid=(n,),
        in_specs=pspecs + [full((128, KW)), full((1, KW)), full((1, DV))] + dep_specs,
        out_specs=[pl.BlockSpec((R, GW), lambda s: (s, 0)), pl.BlockSpec((R, GW), lambda s: (s, 0)),
                   pl.BlockSpec((CPS, H, DV, DK), lambda s: (s, 0, 0, 0))],
        out_shape=[_sds((LP, GW), BF16), _sds((LP, GW), F32), _sds((NCH, H, DV, DK), F32)],
        scratch_shapes=[pltpu.VMEM((H, DV, DK), F32)],
        compiler_params=_params(("arbitrary",)),
    )(proj, proj, proj, proj, proj, gw2p, gate_b, gnw, *dep_ins)


def _gla_bwd(proj, dmixed, o_saved, st_saved, gw2p, gate_b, gnw, deps=()):
    R, n, st, pspecs = _gla_specs(True)

    def body(q_ref, k_ref, v_ref, r_ref, glr_ref, dog_ref, o_ref, st_ref, gw2_ref, gb_ref, gnw_ref,
             dqkvr_ref, dglr_ref, ggn_ref, ggb_ref, ggw_ref, gstate):
        s = pl.program_id(0)

        @pl.when(s == 0)
        def _():
            gstate[...] = jnp.zeros_like(gstate)
            ggn_ref[...] = jnp.zeros_like(ggn_ref)
            ggb_ref[...] = jnp.zeros_like(ggb_ref)
            ggw_ref[...] = jnp.zeros_like(ggw_ref)

        causal = _tri_masks()
        tri = causal.astype(BF16)
        tri_up = (lax.broadcasted_iota(jnp.int32, (CHUNK, CHUNK), 0)
                  <= lax.broadcasted_iota(jnp.int32, (CHUNK, CHUNK), 1)).astype(BF16)
        gnw = gnw_ref[...]
        for c in reversed(range(CPS)):
            rows = slice(c * CHUNK, (c + 1) * CHUNK)
            g_raw, logg, live = _gla_gates(glr_ref, gw2_ref, gb_ref, rows, (n - 1 - s) * R + c * CHUNK)
            G = _tri_sum(tri, logg)
            g_last = G[CHUNK - 1:CHUNK, :]
            e_g, e_gi, e_end = jnp.exp(G), jnp.exp(-G), jnp.exp(g_last - G)
            q_dec = q_ref[rows, :] * (DK ** -0.5) * e_g
            kk = k_ref[rows, :]
            k_inv, k_end = kk * e_gi, kk * e_end
            q_dec_b, k_inv_b, k_end_b = q_dec.astype(BF16), k_inv.astype(BF16), k_end.astype(BF16)
            decay = jnp.exp(g_last)
            d_g, d_gl = [], []
            for h in range(H):
                lk = slice(h * DK, (h + 1) * DK)
                lv = slice(h * DV, (h + 1) * DV)
                o = o_ref[rows, lv]
                rr = r_ref[rows, lv]
                dog = dog_ref[rows, lv]
                rstd = lax.rsqrt(jnp.mean(o * o, axis=-1, keepdims=True) + EPS)
                ohat = o * rstd
                sr = jax.nn.sigmoid(rr)
                don = dog * (rr * sr)
                dqkvr_ref[rows, 2 * KW + GW + h * DV:2 * KW + GW + (h + 1) * DV] = (
                    dog * (ohat * gnw) * (sr * (1.0 + rr * (1.0 - sr)))).astype(BF16)
                ggn_ref[...] += jnp.sum(don * ohat, axis=0, keepdims=True)
                dohat = don * gnw
                do = (rstd * (dohat - ohat * jnp.mean(dohat * ohat, axis=-1, keepdims=True))).astype(BF16)
                v = v_ref[rows, lv].astype(BF16)
                S = st_ref[c, h]
                gS = gstate[h]
                S_b, gS_b = S.astype(BF16), gS.astype(BF16)
                qd, ki, ke = q_dec_b[:, lk], k_inv_b[:, lk], k_end_b[:, lk]
                A = jnp.where(causal, _dot(qd, ki, NT), 0.0).astype(BF16)
                dA = jnp.where(causal, _dot(do, v, NT), 0.0).astype(BF16)
                dv = _dot(A, do, TN) + _dot(ke, gS_b, NT)
                dq_dec = _dot(dA, ki, NN) + _dot(do, S_b, NN)
                dk_inv = _dot(dA, qd, TN)
                dk_end = _dot(v, gS_b, NN)
                d_decay = jnp.sum(gS * S, axis=0, keepdims=True)
                gstate[h] = decay[:, lk] * gS + _dot(do, qd, TN)
                dqkvr_ref[rows, lk] = (dq_dec * e_g[:, lk] * (DK ** -0.5)).astype(BF16)
                dqkvr_ref[rows, KW + h * DK:KW + (h + 1) * DK] = (
                    dk_inv * e_gi[:, lk] + dk_end * e_end[:, lk]).astype(BF16)
                dqkvr_ref[rows, 2 * KW + h * DV:2 * KW + (h + 1) * DV] = dv.astype(BF16)
                ke_prod = dk_end * k_end[:, lk]
                d_g.append(dq_dec * q_dec[:, lk] - dk_inv * k_inv[:, lk] - ke_prod)
                d_gl.append(jnp.sum(ke_prod, axis=0, keepdims=True) + d_decay * decay[:, lk])
            dlogg = _tri_sum(tri_up, jnp.concatenate(d_g, axis=1)) + jnp.concatenate(d_gl, axis=1)
            dg_raw = jnp.where(live, dlogg * (1.0 / TAU) * jax.nn.sigmoid(-g_raw), 0.0)
            ggb_ref[...] += jnp.sum(dg_raw, axis=0, keepdims=True)
            dg_b = dg_raw.astype(BF16)
            ggw_ref[...] += _dot(glr_ref[rows, :].astype(BF16), dg_b, TN)
            dglr_ref[rows, :] = _dot(dg_b, gw2_ref[...], NT).astype(BF16)

    full = lambda shape: pl.BlockSpec(shape, lambda s: (0,) * len(shape))
    body, dep_ins, dep_specs = _after(body, 11, deps)
    return pl.pallas_call(
        body, name="gla_bwd", grid=(n,),
        in_specs=pspecs + [pl.BlockSpec((R, GW), lambda s: (st(s), 0)), pl.BlockSpec((R, GW), lambda s: (st(s), 0)),
                           pl.BlockSpec((CPS, H, DV, DK), lambda s: (st(s), 0, 0, 0)),
                           full((128, KW)), full((1, KW)), full((1, DV))] + dep_specs,
        out_specs=[pl.BlockSpec((R, 2 * KW + 2 * GW), lambda s: (st(s), 0)), pl.BlockSpec((R, 128), lambda s: (st(s), 0)),
                   full((1, DV)), full((1, KW)), full((128, KW))],
        out_shape=[_sds((LP, 2 * KW + 2 * GW), BF16), _sds((LP, 128), BF16),
                   _sds((1, DV), F32), _sds((1, KW), F32), _sds((128, KW), F32)],
        scratch_shapes=[pltpu.VMEM((H, DV, DK), F32)],
        compiler_params=_params(("arbitrary",)),
    )(proj, proj, proj, proj, proj, dmixed, o_saved, st_saved, gw2p, gate_b, gnw, *dep_ins)


def _pool_pre(x, win, rid):
    s, step = x, 1
    while step < win:
        s = s + pltpu.roll(s, step, 0)
        step *= 2
    cnt = jnp.clip(rid - (PAD - 1), 1, win).astype(F32)
    live = rid >= PAD
    return jnp.where(live, s / cnt - x, 0.0), cnt, live


def _pool_fwd(proj, pool_w, pool_scale):
    def body(pu_ref, w_ref, sc_ref, o_ref):
        rid = lax.broadcasted_iota(jnp.int32, (LP, GC), 0)
        for g, win in enumerate(WINDOWS):
            @pl.when(pl.program_id(0) == g)
            def _():
                y, _, _ = _pool_pre(pu_ref[...], win, rid)
                o_ref[...] = (_dot(y.astype(BF16), w_ref[...], NN) * sc_ref[...]).astype(BF16)

    col = lambda base: pl.BlockSpec((LP, GC), lambda g: (0, base + g))
    return pl.pallas_call(
        body, name="pool_fwd", grid=(len(WINDOWS),),
        in_specs=[col(POOL_BLK), pl.BlockSpec((None, GC, GC), lambda g: (g, 0, 0)),
                  pl.BlockSpec((1, GC), lambda g: (0, g))],
        out_specs=col(0), out_shape=_sds((LP, PW), BF16), compiler_params=_params(("parallel",)),
    )(proj, pool_w, pool_scale)


def _pool_bwd(proj, dmixed, pool_w, pool_scale):
    def body(pu_ref, do_ref, w_ref, sc_ref, dpu_ref, dw_ref, dsc_ref):
        rid = lax.broadcasted_iota(jnp.int32, (LP, GC), 0)
        for g, win in enumerate(WINDOWS):
            @pl.when(pl.program_id(0) == g)
            def _():
                y, cnt, live = _pool_pre(pu_ref[...], win, rid)
                y_b = y.astype(BF16)
                w = w_ref[...]
                do = do_ref[...]
                dsc_ref[...] = jnp.sum(do * _dot(y_b, w, NN), axis=0, keepdims=True)
                dyw = (do * sc_ref[...]).astype(BF16)
                dw_ref[...] = _dot(y_b, dyw, TN)
                dy = jnp.where(live, _dot(dyw, w, NT), 0.0)
                s, step = dy / cnt, 1
                while step < win:
                    s = s + pltpu.roll(s, LP - step, 0)
                    step *= 2
                dpu_ref[...] = (s - dy).astype(BF16)

    col = lambda base: pl.BlockSpec((LP, GC), lambda g: (0, base + g))
    mat = pl.BlockSpec((None, GC, GC), lambda g: (g, 0, 0))
    vec = pl.BlockSpec((1, GC), lambda g: (0, g))
    return pl.pallas_call(
        body, name="pool_bwd", grid=(len(WINDOWS),),
        in_specs=[col(POOL_BLK), col(GW // GC), mat, vec], out_specs=[col(0), mat, vec],
        out_shape=[_sds((LP, PW), BF16), _sds((4, GC, GC), F32), _sds((1, PW), F32)],
        compiler_params=_params(("parallel",)),
    )(proj, dmixed, pool_w, pool_scale)


def _adamw_math(w, g, m, v):
    m = B1 * m + (1.0 - B1) * g
    v = B2 * v + (1.0 - B2) * (g * g)
    m_hat = m / (1.0 - B1 ** STEP)
    v_hat = v / (1.0 - B2 ** STEP)
    return -LR * (m_hat / (jnp.sqrt(v_hat) + AEPS) + WD * w), m, v


def _adamw_landed(sums, landed, my_chip, w, m, v, rows, name):
    _, r, c = w.shape

    def body(chip_ref, s_ref, l_ref, w_ref, m_ref, v_ref, g_ref, d_ref, mo_ref, vo_ref):
        g = s_ref[...].astype(F32)
        for k in range(3):
            g = g + l_ref[k].astype(F32)
        g_ref[...] = g
        d_ref[...], mo_ref[...], vo_ref[...] = _adamw_math(w_ref[...], g, m_ref[...], v_ref[...])

    blk = pl.BlockSpec((None, rows, c), lambda i, chip_ref: (0, i, 0))
    return pl.pallas_call(
        body, name=name, out_shape=[_sds((1, r, c), F32)] * 4,
        grid_spec=pltpu.PrefetchScalarGridSpec(
            num_scalar_prefetch=1, grid=(r // rows,),
            in_specs=[pl.BlockSpec((None, rows, c), lambda i, chip_ref: (chip_ref[0], i, 0)),
                      pl.BlockSpec((3, rows, c), lambda i, chip_ref: (0, i, 0)), blk, blk, blk],
            out_specs=[blk] * 4),
        compiler_params=_params(("parallel",)),
    )(my_chip, sums, landed, w, m, v)


def _adamw_given(g, w, m, v, rows, name):
    _, r, c = w.shape

    def body(g_ref, w_ref, m_ref, v_ref, go_ref, d_ref, mo_ref, vo_ref):
        g = g_ref[...]
        go_ref[...] = g
        d_ref[...], mo_ref[...], vo_ref[...] = _adamw_math(w_ref[...], g, m_ref[...], v_ref[...])

    blk = pl.BlockSpec((None, rows, c), lambda i: (0, i, 0))
    return pl.pallas_call(
        body, name=name, grid=(r // rows,), in_specs=[pl.BlockSpec((rows, c), lambda i: (i, 0)), blk, blk, blk],
        out_specs=[blk] * 4, out_shape=[_sds((1, r, c), F32)] * 4, compiler_params=_params(("parallel",)),
    )(g, w, m, v)


def _adamw_small(g, w, m, v):
    def body(g_ref, w_ref, m_ref, v_ref, d_ref, mo_ref, vo_ref):
        d_ref[...], mo_ref[...], vo_ref[...] = _adamw_math(w_ref[...], g_ref[...], m_ref[...], v_ref[...])

    return pl.pallas_call(body, name="adamw_small", out_shape=[_sds(w.shape, F32)] * 3)(g, w, m, v)


SMALL_REPL = (("norm1_w", D), ("norm2_w", D), ("final_norm_w", D), ("pool_scale", PW), ("gate_b", KW),
              ("gla_norm_w", DV))


def _pack_rows(vecs, rows):
    flat = jnp.concatenate([jnp.ravel(v) for v in vecs])
    return jnp.pad(flat, (0, rows * 1024 - flat.shape[0])).reshape(rows, 1024)


def kernel(x, meta_tokens, norm1_w, w_in, gate_w2, gate_b, gla_norm_w, pool_w, pool_scale, w_out, norm2_w, mlp_w1, mlp_w2, final_norm_w, loss_target, m_meta_tokens, m_norm1_w, m_w_in, m_gate_w2, m_gate_b, m_gla_norm_w, m_pool_w, m_pool_scale, m_w_out, m_norm2_w, m_mlp_w1, m_mlp_w2, m_final_norm_w, v_meta_tokens, v_norm1_w, v_w_in, v_gate_w2, v_gate_b, v_gla_norm_w, v_pool_w, v_pool_scale, v_w_out, v_norm2_w, v_mlp_w1, v_mlp_w2, v_final_norm_w):
    me = 4 * lax.axis_index("x") + 2 * lax.axis_index("y") + lax.axis_index("c")
    W = dict(meta_tokens=meta_tokens, norm1_w=norm1_w, w_in=w_in, gate_w2=gate_w2, gate_b=gate_b,
             gla_norm_w=gla_norm_w, pool_w=pool_w, pool_scale=pool_scale, w_out=w_out, norm2_w=norm2_w,
             mlp_w1=mlp_w1, mlp_w2=mlp_w2, final_norm_w=final_norm_w)
    Mo = dict(meta_tokens=m_meta_tokens, norm1_w=m_norm1_w, w_in=m_w_in, gate_w2=m_gate_w2, gate_b=m_gate_b,
              gla_norm_w=m_gla_norm_w, pool_w=m_pool_w, pool_scale=m_pool_scale, w_out=m_w_out, norm2_w=m_norm2_w,
              mlp_w1=m_mlp_w1, mlp_w2=m_mlp_w2, final_norm_w=m_final_norm_w)
    Vo = dict(meta_tokens=v_meta_tokens, norm1_w=v_norm1_w, w_in=v_w_in, gate_w2=v_gate_w2, gate_b=v_gate_b,
              gla_norm_w=v_gla_norm_w, pool_w=v_pool_w, pool_scale=v_pool_scale, w_out=v_w_out, norm2_w=v_norm2_w,
              mlp_w1=v_mlp_w1, mlp_w2=v_mlp_w2, final_norm_w=v_final_norm_w)

    ex = _Exchange(dict(w_in=w_in[0].T, w_out=w_out[0], pool_w=pool_w[0].reshape(4 * 32, GC), mlp_w1=mlp_w1[0],
                        mlp_w2=mlp_w2[0]))
    small = _exchange_small(_pack_rows([meta_tokens, gate_w2[0]], 8), False, "gather_small", deps=[ex.started])
    meta_full = small[:, 0:4].reshape(NDEV, N_META, D // NDEV).transpose(1, 0, 2).reshape(N_META, D)
    gw2_full = small[:, 4].reshape(NDEV, RANK, KW // NDEV).transpose(1, 0, 2).reshape(RANK, KW)
    gw2p = jnp.pad(gw2_full, ((0, 128 - RANK), (0, 0))).astype(BF16)

    step = _layer_step(x[0], loss_target[0], meta_full, gw2p, ex, norm1_w, gate_b, gla_norm_w, pool_scale, norm2_w,
                       final_norm_w.reshape(1, D))
    grad_x = step["dh0"][ROW_X:][None]

    loss_part = 0.5 * jnp.sum(step["sq"]) / D
    packed = jnp.concatenate([
        _pack_rows([step[k] for k, _ in SMALL_REPL] + [loss_part], 8),
        step["gate_w2"][:RANK].reshape(8, 1024), step["dh0"][PAD:ROW_X].reshape(32, 1024)], axis=0)
    red = _exchange_small(packed, True, "reduce_small")
    loss = red[7, 768]
    g_gw2_mine = lax.dynamic_slice(red[8:16].reshape(RANK, KW), (0, me * (KW // NDEV)), (RANK, KW // NDEV))
    g_meta_mine = lax.dynamic_slice(red[16:48].reshape(N_META, D), (0, me * (D // NDEV)), (N_META, D // NDEV))

    last = ex.grad_mid("w_in", red)
    out = {}
    for group in ("down", "up", "mix"):
        for k, (sums, landed) in ex.grad_finish(group, last).items():
            out[k] = _adamw_landed(sums, landed, ex.my_chip, W[k], Mo[k], Vo[k], SHARD_ROWS[k], "adamw_" + k)
            last = out[k][1]
    done = ex.grad_finish("w_in", last)
    poolw3 = lambda a: a.reshape(1, 4 * 32, GC)
    res = _adamw_landed(*done["pool_w"], ex.my_chip, poolw3(pool_w), poolw3(m_pool_w), poolw3(v_pool_w),
                        SHARD_ROWS["pool_w"], "adamw_pool_w")
    out["pool_w"] = [a.reshape(pool_w.shape) for a in res]
    g_win_t = _landed_sum(*done["w_in"], ex.my_chip, (D_IN // NDEV, 512), "w_in_grad_sum")
    out["w_in"] = _adamw_given(g_win_t.T, w_in, m_w_in, v_w_in, SHARD_ROWS["w_in"], "adamw_w_in")

    def small_pack(P):
        return jnp.concatenate([_pack_rows([P[k] for k, _ in SMALL_REPL], 8),
                                _pack_rows([P["meta_tokens"], P["gate_w2"]], 8)], axis=0)

    g_small = jnp.concatenate([red[0:8], _pack_rows([g_meta_mine, g_gw2_mine], 8)], axis=0)
    g_small = g_small.at[7, 768].set(0.0)
    res_small = _adamw_small(g_small, small_pack(W), small_pack(Mo), small_pack(Vo))
    res_small = [g_small] + list(res_small)
    off = 0
    for k, nel in SMALL_REPL:
        out[k] = [a[0:8].reshape(-1)[off:off + nel].reshape(W[k].shape) for a in res_small]
        off += nel
    out["meta_tokens"] = [a[8:12].reshape(N_META, D // NDEV) for a in res_small]
    out["gate_w2"] = [a[12].reshape(1, RANK, KW // NDEV) for a in res_small]

    order = ["meta_tokens", "norm1_w", "w_in", "gate_w2", "gate_b", "gla_norm_w", "pool_w", "pool_scale", "w_out",
             "norm2_w", "mlp_w1", "mlp_w2", "final_norm_w"]
    return (loss, grad_x, *[out[k][0] for k in order], *[out[k][1] for k in order],
            *[out[k][2] for k in order], *[out[k][3] for k in order])


SHARD_ROWS = dict(w_in=256, w_out=64, mlp_w1=128, mlp_w2=64, pool_w=128)
C_GLR = 2 * KW + 2 * GW
GATHER_GROUPS = dict(w_in=("w_in",), mix=("w_out", "pool_w"), up=("mlp_w1",), down=("mlp_w2",))
GRAD_GROUPS = dict(down=("mlp_w2",), up=("mlp_w1",), mix=("w_out",), w_in=("pool_w", "w_in"))


class _Exchange:
    def __init__(self, shards):
        names = list(shards)
        started, self.started = _gather_start([shards[k].astype(BF16) for k in names], "gather_start")
        self.state = dict(zip(names, started))
        self.my_c = lax.axis_index("c").astype(jnp.int32).reshape(1)
        self.my_chip = (2 * lax.axis_index("x") + lax.axis_index("y")).astype(jnp.int32).reshape(1)
        self.sibling, self.chips = {}, {}

    def forward(self, group, after):
        ks = GATHER_GROUPS[group]
        fwd, token = _gather_forward([self.state[k] for k in ks], after, "gather_forward_" + group)
        self.state.update(zip(ks, fwd))
        return token

    def weights(self, group, after):
        ks = GATHER_GROUPS[group]
        g = dict(zip(ks, _gather_finish([self.state[k] for k in ks], after, "gather_finish_" + group)))
        if group == "w_in":
            nat = g["w_in"].reshape(D_IN, D)
            return jnp.concatenate([nat[:C_GLR], nat[C_GLR + RANK:], nat[C_GLR:C_GLR + RANK],
                                    jnp.zeros((D_INP - D_IN, D), BF16)], axis=0)
        if group == "mix":
            return (g["w_out"].reshape(D, D),
                    g["pool_w"].reshape(NDEV, 4, 32, GC).transpose(1, 0, 2, 3).reshape(4, GC, GC))
        return g["mlp_w1"] if group == "up" else g["mlp_w2"].reshape(DFF, D)

    def grad(self, group, grads):
        parts = dict(grads)
        if group == "w_in":
            g = parts["w_in"]
            nat = jnp.concatenate([g[:C_GLR], g[C_GLR + PW:C_GLR + PW + RANK], g[C_GLR:C_GLR + PW]], axis=0)
            parts["w_in"] = nat.reshape(4, 2, D_IN // NDEV, D).transpose(1, 0, 2, 3)
            parts["pool_w"] = (parts["pool_w"].astype(BF16).reshape(4, 4, 2, 32, GC).transpose(2, 1, 0, 3, 4)
                               .reshape(2, 4, 4 * 32, GC))
        ks = GRAD_GROUPS[group]
        started, token = _to_sibling_start([parts[k] for k in ks], "grad_sibling_start_" + group)
        self.sibling[group] = started
        return token

    def grad_mid(self, group, after):
        ks = GRAD_GROUPS[group]
        both = _to_sibling_finish(self.sibling[group], after, "grad_sibling_finish_" + group)
        tile = lambda k, p: (p.shape[2], 512) if k == "w_in" else (SHARD_ROWS[k], p.shape[3])
        sums = [_chip_sum(p, s, self.my_c, tile(k, p), "chip_sum_" + k) for k, (p, s) in zip(ks, both)]
        self.chips[group], token = _to_chips_start(sums, "grad_chips_start_" + group)
        return token

    def grad_finish(self, group, after):
        done = _to_chips_finish(self.chips[group], after, "grad_chips_finish_" + group)
        return dict(zip(GRAD_GROUPS[group], done))


def _layer_step(x, target, meta_full, gw2p, ex, norm1_w, gate_b, gla_norm_w, pool_scale, norm2_w, final_norm_w):
    h0 = jnp.concatenate([jnp.zeros((PAD, D), F32), meta_full, x], axis=0)
    u1 = _rmsnorm_fwd(h0, norm1_w, "rmsnorm1")
    win_p = ex.weights("w_in", ex.forward("w_in", u1))
    proj = _matmul(u1, win_p, mode="nt", tm=1056, tn=1408, tk=512, name="proj")
    tok = ex.forward("mix", proj)
    og, o_saved, st_saved = _gla_fwd(proj, gw2p, gate_b, gla_norm_w, deps=[tok])
    wout_f, poolw_f = ex.weights("mix", og)
    op = _pool_fwd(proj, poolw_f, pool_scale)
    mixed = jnp.concatenate([og, op], axis=1)
    h1 = _matmul(mixed, wout_f, mode="nn", tm=1056, tn=1024, tk=512, name="mix_out", epi="add", extra=h0)
    tok = ex.forward("up", h1)
    u2 = _rmsnorm_fwd(h1, norm2_w, "rmsnorm2", deps=[tok])
    w1_g = ex.weights("up", u2)
    z, act = _matmul(u2, w1_g, mode="nn", tm=1056, tn=1024, tk=512, name="mlp_up", epi="relu2", b_slots=True)
    w2_f = ex.weights("down", ex.forward("down", act))
    h2 = _matmul(act, w2_f, mode="nn", tm=1056, tn=1024, tk=512, name="mlp_down", epi="add", extra=h1)
    dh2, dh2b, sq, g_fnw = _loss_head(h2, final_norm_w, target)

    g_w2 = _matmul(act, dh2b, mode="tn", tm=512, tn=1024, tk=LP, name="d_mlp_w2", out_dtype=BF16, out_slots="rows")
    tok = ex.grad("down", dict(mlp_w2=g_w2))
    dz = _matmul(dh2b, w2_f, mode="nt", tm=1056, tn=1024, tk=512, name="d_act", out_dtype=BF16, epi="dz", extra=z,
                 deps=[tok])
    tok = ex.grad_mid("down", dz)
    g_w1 = _matmul(u2, dz, mode="tn", tm=512, tn=1024, tk=LP, name="d_mlp_w1", out_dtype=BF16, out_slots="cols",
                   deps=[tok])
    tok = ex.grad("up", dict(mlp_w1=g_w1))
    du2 = _matmul(dz, w1_g, mode="nt", tm=1056, tn=1024, tk=1024, name="d_u2", b_slots=True, deps=[tok])
    tok = ex.grad_mid("up", du2)
    dh1, dh1b, g_n2 = _rmsnorm_bwd(h1, norm2_w, du2, dh2, "rmsnorm2_bwd", deps=[tok])
    g_wout = _matmul(mixed, dh1b, mode="tn", tm=256, tn=1024, tk=LP, name="d_w_out", out_dtype=BF16, out_slots="rows")
    tok = ex.grad("mix", dict(w_out=g_wout))
    dmixed = _matmul(dh1b, wout_f, mode="nt", tm=1056, tn=1024, tk=512, name="d_mixed", deps=[tok])
    tok = ex.grad_mid("mix", dmixed)
    dqkvr, dglr, g_gnw, g_gb, g_gw2 = _gla_bwd(proj, dmixed, o_saved, st_saved, gw2p, gate_b, gla_norm_w, deps=[tok])
    dpu, g_poolw, g_psc = _pool_bwd(proj, dmixed, poolw_f, pool_scale)
    dproj = jnp.concatenate([dqkvr, dpu, dglr], axis=1)
    g_win_p = _matmul(dproj, u1, mode="tn", tm=1408, tn=1024, tk=LP, name="d_w_in", out_dtype=BF16)
    tok = ex.grad("w_in", dict(pool_w=g_poolw, w_in=g_win_p))
    du1 = _matmul(dproj, win_p, mode="nn", tm=1056, tn=1024, tk=1408, name="d_u1", deps=[tok])
    dh0, _, g_n1 = _rmsnorm_bwd(h0, norm1_w, du1, dh1, "rmsnorm1_bwd")
    return dict(dh0=dh0, sq=sq, gate_w2=g_gw2, norm1_w=g_n1, norm2_w=g_n2, final_norm_w=g_fnw, pool_scale=g_psc, gate_b=g_gb, gla_norm_w=g_gnw)
```

```python
import functools

import jax
import jax.numpy as jnp
from jax import lax
from jax.experimental import pallas as pl
from jax.experimental.pallas import tpu as pltpu

F32, BF16 = jnp.float32, jnp.bfloat16
MESH = pl.DeviceIdType.MESH

NDEV = 8
D = 2048
SEQ = 2048
N_META = 16
CHUNK = 64
PAD = (-N_META) % CHUNK
ROW_X = PAD + N_META
LP = ROW_X + SEQ
NCH = LP // CHUNK
H = 4
DK = 128
DV = 256
KW = H * DK
GW = H * DV
PW = 1024
RANK = 16
TAU = 16.0
WINDOWS = (2, 4, 8, 16)
GC = 256
DFF = 4 * D
EPS = 1e-6
D_IN = 2 * KW + 2 * GW + RANK + PW
D_INP = 4224
GLR_BLK = (2 * KW + 2 * GW + PW) // 128
POOL_BLK = (2 * KW + 2 * GW) // GC
LR, B1, B2, AEPS, WD, STEP = 0.001, 0.9, 0.999, 1e-08, 0.01, 10
VMEM_LIMIT = 48 * 1024 * 1024
CPS = 3


def _params(sem=None):
    return pltpu.CompilerParams(dimension_semantics=sem, vmem_limit_bytes=VMEM_LIMIT)


def _sds(shape, dtype):
    return jax.ShapeDtypeStruct(shape, dtype)


def _me():
    return lax.axis_index("x"), lax.axis_index("y"), lax.axis_index("c")


def _peer(j):
    x, y, c = _me()
    return (x ^ ((j >> 2) & 1), y ^ ((j >> 1) & 1), c ^ (j & 1))


def _slot(dev):
    return 4 * dev[0] + 2 * dev[1] + dev[2]


HBM_SPEC = pl.BlockSpec(memory_space=pltpu.HBM)
SEM_SPEC = pl.BlockSpec(memory_space=pltpu.SEMAPHORE)
ANY_SPEC = pl.BlockSpec(memory_space=pl.ANY)
EFFECT = pltpu.SideEffectType.DATAFLOW_SIDE_EFFECTING
SIBLING = 1
OTHER_CHIPS = (2, 4, 6)


def _in_hbm(a):
    return pltpu.with_memory_space_constraint(a, pltpu.HBM)


def _chip(dev):
    return 2 * dev[0] + dev[1]


def _rcopy(src, dst, send_sem, recv_sem, to):
    return pltpu.make_async_remote_copy(src_ref=src, dst_ref=dst, send_sem=send_sem, recv_sem=recv_sem,
                                        device_id=to, device_id_type=MESH)


def _split_call(body, name, ins, in_specs, out_shape, out_specs, aliases, scratch=()):
    n = len(ins) + len(out_shape)

    def with_token(*refs):
        body(*refs[:n], *refs[n + 1:])
        refs[n][...] = jnp.zeros_like(refs[n])

    return pl.pallas_call(
        with_token, name=name, in_specs=in_specs, out_shape=list(out_shape) + [_sds((8, 128), F32)],
        out_specs=list(out_specs) + [pl.BlockSpec(memory_space=pltpu.VMEM)],
        input_output_aliases=aliases, scratch_shapes=list(scratch),
        compiler_params=pltpu.CompilerParams(has_side_effects=EFFECT),
    )(*ins)


def _after(body, n_in, deps):
    deps = [d for d in deps if d is not None]
    if not deps:
        return body, [], []
    return (lambda *refs: body(*refs[:n_in], *refs[n_in + len(deps):])), deps, [ANY_SPEC] * len(deps)


def _gather_start(shards, name, after):
    n = len(shards)
    me = _slot(_me())
    lands = [lax.dynamic_update_slice(lax.empty((NDEV,) + s.shape, s.dtype), s[None], (me, 0, 0)) for s in shards]

    def body(*refs):
        src, land = refs[:n], refs[n:2 * n]
        outs = refs[2 * n + 1:]
        for i in range(n):
            send_sems, recv_sems = outs[4 * i], outs[4 * i + 1]
            for k, rel in enumerate((SIBLING,) + OTHER_CHIPS):
                _rcopy(src[i], land[i].at[_slot(_me())], send_sems.at[k], recv_sems.at[k], _peer(rel)).start()

    out_shape, out_specs, aliases = [], [], {}
    for i, s in enumerate(shards):
        out_shape += [pltpu.SemaphoreType.DMA((4,)), pltpu.SemaphoreType.DMA((4,)), pltpu.HBM(s.shape, s.dtype),
                      pltpu.HBM((NDEV,) + s.shape, s.dtype)]
        out_specs += [SEM_SPEC, SEM_SPEC, HBM_SPEC, HBM_SPEC]
        aliases[i] = 4 * i + 2
        aliases[n + i] = 4 * i + 3
    res = _split_call(body, name, [_in_hbm(s) for s in shards] + [_in_hbm(l) for l in lands] + [after],
                      [HBM_SPEC] * (2 * n) + [ANY_SPEC], out_shape, out_specs, aliases)
    return [tuple(res[4 * i:4 * i + 4]) for i in range(n)], res[-1]


def _gather_forward(started, after, name):
    n = len(started)

    def body(*refs):
        land, recv1 = refs[:n], refs[n:2 * n]
        outs = refs[2 * n + 1:]
        for i in range(n):
            send2, recv2 = outs[3 * i + 1], outs[3 * i + 2]
            for k, rel in enumerate(OTHER_CHIPS):
                blk = land[i].at[_slot(_peer(rel))]
                _rcopy(blk, blk, send2.at[k], recv1[i].at[1 + k], _peer(rel)).wait_recv()
                _rcopy(blk, blk, send2.at[k], recv2.at[k], _peer(SIBLING)).start()

    ins = [_in_hbm(st[3]) for st in started] + [st[1] for st in started] + [after]
    out_shape, out_specs, aliases = [], [], {}
    for i, st in enumerate(started):
        out_shape += [pltpu.HBM(st[3].shape, st[3].dtype), pltpu.SemaphoreType.DMA((3,)), pltpu.SemaphoreType.DMA((3,))]
        out_specs += [HBM_SPEC, SEM_SPEC, SEM_SPEC]
        aliases[i] = 3 * i
    res = _split_call(body, name, ins, [HBM_SPEC] * n + [SEM_SPEC] * n + [ANY_SPEC], out_shape, out_specs, aliases)
    return [(st[0], st[1], st[2], res[3 * i], res[3 * i + 1], res[3 * i + 2]) for i, st in enumerate(started)], res[-1]


def _gather_finish(forwarded, after, name):
    n = len(forwarded)

    def body(*refs):
        for i in range(n):
            send1, recv1, src, land, send2, recv2 = refs[6 * i:6 * i + 6]
            me = _slot(_me())
            sib = _slot(_peer(SIBLING))
            for k, rel in enumerate((SIBLING,) + OTHER_CHIPS):
                _rcopy(src, land.at[me], send1.at[k], recv1.at[k], _peer(rel)).wait_send()
            _rcopy(src, land.at[sib], send1.at[0], recv1.at[0], _peer(SIBLING)).wait_recv()
            for k, rel in enumerate(OTHER_CHIPS):
                mine, theirs = land.at[_slot(_peer(rel))], land.at[_slot(_peer(rel ^ SIBLING))]
                _rcopy(mine, mine, send2.at[k], recv2.at[k], _peer(SIBLING)).wait_send()
                _rcopy(theirs, theirs, send2.at[k], recv2.at[k], _peer(SIBLING)).wait_recv()

    ins, in_specs, out_shape, aliases = [], [], [], {}
    for i, f in enumerate(forwarded):
        ins += [f[0], f[1], _in_hbm(f[2]), _in_hbm(f[3]), f[4], f[5]]
        in_specs += [SEM_SPEC, SEM_SPEC, HBM_SPEC, HBM_SPEC, SEM_SPEC, SEM_SPEC]
        out_shape.append(pltpu.HBM(f[3].shape, f[3].dtype))
        aliases[6 * i + 3] = i
    res = _split_call(body, name, ins + [after], in_specs + [ANY_SPEC], out_shape, [HBM_SPEC] * n, aliases)
    return list(res[:-1])


def _to_sibling_start(parts, name):
    n = len(parts)
    lands = [lax.empty(p.shape[1:], p.dtype) for p in parts]

    def body(*refs):
        src, land = refs[:n], refs[n:2 * n]
        outs = refs[2 * n:]
        other = 1 - lax.axis_index("c")
        for i in range(n):
            _rcopy(src[i].at[other], land[i], outs[4 * i], outs[4 * i + 1], _peer(SIBLING)).start()

    out_shape, out_specs, aliases = [], [], {}
    for i, p in enumerate(parts):
        out_shape += [pltpu.SemaphoreType.DMA(()), pltpu.SemaphoreType.DMA(()), pltpu.HBM(p.shape, p.dtype),
                      pltpu.HBM(p.shape[1:], p.dtype)]
        out_specs += [SEM_SPEC, SEM_SPEC, HBM_SPEC, HBM_SPEC]
        aliases[i] = 4 * i + 2
        aliases[n + i] = 4 * i + 3
    res = _split_call(body, name, [_in_hbm(p) for p in parts] + [_in_hbm(l) for l in lands], [HBM_SPEC] * (2 * n),
                      out_shape, out_specs, aliases)
    return [tuple(res[4 * i:4 * i + 4]) for i in range(n)], res[-1]


def _to_sibling_finish(started, after, name):
    n = len(started)

    def body(*refs):
        for i in range(n):
            send, recv, src, land = refs[4 * i:4 * i + 4]
            cp = _rcopy(src.at[0], land, send, recv, _peer(SIBLING))
            cp.wait_send()
            cp.wait_recv()

    ins, in_specs, out_shape, aliases = [], [], [], {}
    for i, st in enumerate(started):
        ins += [st[0], st[1], _in_hbm(st[2]), _in_hbm(st[3])]
        in_specs += [SEM_SPEC, SEM_SPEC, HBM_SPEC, HBM_SPEC]
        out_shape += [pltpu.HBM(st[2].shape, st[2].dtype), pltpu.HBM(st[3].shape, st[3].dtype)]
        aliases[4 * i + 2] = 2 * i
        aliases[4 * i + 3] = 2 * i + 1
    res = _split_call(body, name, ins + [after], in_specs + [ANY_SPEC], out_shape, [HBM_SPEC] * (2 * n), aliases)
    return [(res[2 * i], res[2 * i + 1]) for i in range(n)]


def _chip_sum(parts, from_sibling, my_c, tile, name):
    _, _, r, c = parts.shape
    tr, tc = tile

    def body(c_ref, p_ref, s_ref, o_ref):
        o_ref[...] = (p_ref[...].astype(F32) + s_ref[...].astype(F32)).astype(o_ref.dtype)

    blk = pl.BlockSpec((4, tr, tc), lambda i, j, c_ref: (0, i, j))
    return pl.pallas_call(
        body, name=name, out_shape=_sds((4, r, c), parts.dtype),
        grid_spec=pltpu.PrefetchScalarGridSpec(
            num_scalar_prefetch=1, grid=(r // tr, c // tc),
            in_specs=[pl.BlockSpec((None, 4, tr, tc), lambda i, j, c_ref: (c_ref[0], 0, i, j)), blk], out_specs=blk),
        compiler_params=_params(("parallel", "parallel")),
    )(my_c, parts, from_sibling)


def _to_chips_start(sums, name):
    n = len(sums)
    lands = [lax.empty((3,) + s.shape[1:], s.dtype) for s in sums]

    def body(*refs):
        src, land = refs[:n], refs[n:2 * n]
        outs = refs[2 * n:]
        for i in range(n):
            for k, rel in enumerate(OTHER_CHIPS):
                to = _peer(rel)
                _rcopy(src[i].at[_chip(to)], land[i].at[k], outs[4 * i].at[k], outs[4 * i + 1].at[k], to).start()

    out_shape, out_specs, aliases = [], [], {}
    for i, s in enumerate(sums):
        out_shape += [pltpu.SemaphoreType.DMA((3,)), pltpu.SemaphoreType.DMA((3,)), pltpu.HBM(s.shape, s.dtype),
                      pltpu.HBM((3,) + s.shape[1:], s.dtype)]
        out_specs += [SEM_SPEC, SEM_SPEC, HBM_SPEC, HBM_SPEC]
        aliases[i] = 4 * i + 2
        aliases[n + i] = 4 * i + 3
    res = _split_call(body, name, [_in_hbm(s) for s in sums] + [_in_hbm(l) for l in lands], [HBM_SPEC] * (2 * n),
                      out_shape, out_specs, aliases)
    return [tuple(res[4 * i:4 * i + 4]) for i in range(n)], res[-1]


def _to_chips_finish(started, after, name):
    n = len(started)

    def body(*refs):
        for i in range(n):
            send, recv, src, land = refs[4 * i:4 * i + 4]
            for k, rel in enumerate(OTHER_CHIPS):
                cp = _rcopy(src.at[0], land.at[k], send.at[k], recv.at[k], _peer(rel))
                cp.wait_send()
                cp.wait_recv()

    ins, in_specs, out_shape, aliases = [], [], [], {}
    for i, st in enumerate(started):
        ins += [st[0], st[1], _in_hbm(st[2]), _in_hbm(st[3])]
        in_specs += [SEM_SPEC, SEM_SPEC, HBM_SPEC, HBM_SPEC]
        out_shape += [pltpu.HBM(st[2].shape, st[2].dtype), pltpu.HBM(st[3].shape, st[3].dtype)]
        aliases[4 * i + 2] = 2 * i
        aliases[4 * i + 3] = 2 * i + 1
    res = _split_call(body, name, ins + [after], in_specs + [ANY_SPEC], out_shape, [HBM_SPEC] * (2 * n), aliases)
    return [(res[2 * i], res[2 * i + 1]) for i in range(n)]


def _exchange_small(v, reduce, name, deps=()):
    r, c = v.shape

    def body(v_ref, o_ref, land, send_sems, recv_sems):
        me = _slot(_me())
        copies = []
        for j in range(1, NDEV):
            cp = pltpu.make_async_remote_copy(
                src_ref=v_ref, dst_ref=land.at[me], send_sem=send_sems.at[j - 1],
                recv_sem=recv_sems.at[j - 1], device_id=_peer(j), device_id_type=MESH)
            cp.start()
            copies.append(cp)
        land[me] = v_ref[...]
        for cp in copies:
            cp.wait()
        if reduce:
            acc = land[0]
            for k in range(1, NDEV):
                acc = acc + land[k]
            o_ref[...] = acc
        else:
            o_ref[...] = land[...]

    vm = pl.BlockSpec(memory_space=pltpu.VMEM)
    body, dep_ins, dep_specs = _after(body, 1, deps)
    return pl.pallas_call(
        body, name=name,
        out_shape=_sds((r, c) if reduce else (NDEV, r, c), F32),
        in_specs=[vm] + dep_specs, out_specs=vm,
        scratch_shapes=[pltpu.VMEM((NDEV, r, c), F32), pltpu.SemaphoreType.DMA((NDEV - 1,)),
                        pltpu.SemaphoreType.DMA((NDEV - 1,))],
        compiler_params=_params(),
    )(v, *dep_ins)


def _matmul(a, b, *, mode, tm, tn, tk, name, out_dtype=F32, epi=None, extra=None, b_slots=False, out_slots=False,
            deps=()):
    slot_w = b.shape[-1] if b_slots else None
    if mode == "nn":
        M, K = a.shape
        N = NDEV * slot_w if b_slots else b.shape[1]
    elif mode == "tn":
        K, M = a.shape
        N = b.shape[1]
    else:
        M, K = a.shape
        N = b.shape[-2]
        if b_slots:
            assert K == NDEV * slot_w and tk == slot_w
    if mode == "nn" and b_slots:
        assert tn == slot_w
    if out_slots == "cols":
        assert tn * NDEV == N
    if out_slots == "rows":
        assert (M // NDEV) % tm == 0
    assert M % tm == 0 and N % tn == 0 and K % tk == 0, (name, M, N, K, tm, tn, tk)
    nk = K // tk
    dims = {"nn": ((1,), (0,)), "tn": ((0,), (0,)), "nt": ((1,), (1,))}[mode]

    if mode == "tn":
        a_spec = pl.BlockSpec((tk, tm), lambda i, j, k: (k, i))
    else:
        a_spec = pl.BlockSpec((tm, tk), lambda i, j, k: (i, k))
    if mode == "nt":
        b_spec = (pl.BlockSpec((None, tn, tk), lambda i, j, k: (k, j, 0)) if b_slots
                  else pl.BlockSpec((tn, tk), lambda i, j, k: (j, k)))
    else:
        b_spec = (pl.BlockSpec((None, tk, tn), lambda i, j, k: (j, k, 0)) if b_slots
                  else pl.BlockSpec((tk, tn), lambda i, j, k: (k, j)))
    tile = pl.BlockSpec((tm, tn), lambda i, j, k: (i, j))
    if out_slots == "cols":
        out_spec = pl.BlockSpec((None, None, tm, tn), lambda i, j, k: (j % 2, j // 2, i, 0))
        out_shape = _sds((2, 4, M, tn), out_dtype)
    elif out_slots == "rows":
        per = M // NDEV // tm
        out_spec = pl.BlockSpec((None, None, tm, tn), lambda i, j, k: ((i // per) % 2, (i // per) // 2, i % per, j))
        out_shape = _sds((2, 4, M // NDEV, N), out_dtype)
    else:
        out_spec, out_shape = tile, _sds((M, N), out_dtype)
    ins, in_specs = [a, b], [a_spec, b_spec]
    if epi in ("add", "dz"):
        ins.append(extra)
        in_specs.append(tile)
    if epi == "relu2":
        out_specs, out_shapes = [tile, tile], [_sds((M, N), F32), _sds((M, N), BF16)]
    else:
        out_specs, out_shapes = out_spec, out_shape
    n_in = len(ins)

    def body(*refs):
        outs = refs[n_in:-1] if nk > 1 else refs[n_in:]

        def finish(p):
            if epi is None:
                outs[0][...] = p.astype(out_dtype)
            elif epi == "add":
                outs[0][...] = (p + refs[2][...]).astype(out_dtype)
            elif epi == "relu2":
                outs[0][...] = p
                rz = jnp.maximum(p, 0.0)
                outs[1][...] = (rz * rz).astype(BF16)
            else:
                outs[0][...] = (p * (2.0 * jnp.maximum(refs[2][...], 0.0))).astype(out_dtype)

        def product():
            return lax.dot_general(refs[0][...].astype(BF16), refs[1][...].astype(BF16), (dims, ((), ())),
                                   preferred_element_type=F32)

        if nk == 1:
            finish(product())
            return
        acc = refs[-1]
        k = pl.program_id(2)

        @pl.when(k == 0)
        def _():
            acc[...] = jnp.zeros_like(acc)

        acc[...] += product()

        @pl.when(k == nk - 1)
        def _():
            finish(acc[...])

    body, dep_ins, dep_specs = _after(body, n_in, deps)
    return pl.pallas_call(
        body, name=name, grid=(M // tm, N // tn, nk),
        in_specs=in_specs + dep_specs, out_specs=out_specs, out_shape=out_shapes,
        scratch_shapes=[pltpu.VMEM((tm, tn), F32)] if nk > 1 else [],
        compiler_params=_params(("parallel", "parallel", "arbitrary")),
    )(*ins, *dep_ins)


ROWS = 352


def _rmsnorm_fwd(h, w, name, deps=()):
    def body(h_ref, w_ref, u_ref):
        x = h_ref[...]
        rstd = lax.rsqrt(jnp.mean(x * x, axis=-1, keepdims=True) + EPS)
        u_ref[...] = (x * rstd * w_ref[...]).astype(BF16)

    row = pl.BlockSpec((ROWS, D), lambda i: (i, 0))
    body, dep_ins, dep_specs = _after(body, 2, deps)
    return pl.pallas_call(
        body, name=name, grid=(LP // ROWS,), in_specs=[row, pl.BlockSpec((1, D), lambda i: (0, 0))] + dep_specs,
        out_specs=row, out_shape=_sds((LP, D), BF16), compiler_params=_params(("parallel",)),
    )(h, w, *dep_ins)


def _rmsnorm_bwd(h, w, du, dres, name, deps=()):
    def body(h_ref, w_ref, du_ref, dres_ref, dh_ref, dhb_ref, gw_ref):
        x = h_ref[...]
        rstd = lax.rsqrt(jnp.mean(x * x, axis=-1, keepdims=True) + EPS)
        xhat = x * rstd
        dy = du_ref[...]
        dxh = dy * w_ref[...]
        dh = dres_ref[...] + rstd * (dxh - xhat * jnp.mean(dxh * xhat, axis=-1, keepdims=True))
        dh_ref[...] = dh
        dhb_ref[...] = dh.astype(BF16)
        part = jnp.sum(dy * xhat, axis=0, keepdims=True)

        @pl.when(pl.program_id(0) == 0)
        def _():
            gw_ref[...] = part

        @pl.when(pl.program_id(0) > 0)
        def _():
            gw_ref[...] += part

    row = pl.BlockSpec((ROWS, D), lambda i: (i, 0))
    vec = pl.BlockSpec((1, D), lambda i: (0, 0))
    body, dep_ins, dep_specs = _after(body, 4, deps)
    return pl.pallas_call(
        body, name=name, grid=(LP // ROWS,), in_specs=[row, vec, row, row] + dep_specs, out_specs=[row, row, vec],
        out_shape=[_sds((LP, D), F32), _sds((LP, D), BF16), _sds((1, D), F32)],
        compiler_params=_params(("arbitrary",)),
    )(h, w, du, dres, *dep_ins)


def _loss_head(h2, wf, target):
    def body(h_ref, w_ref, t_ref, dh_ref, dhb_ref, sq_ref, gw_ref):
        i = pl.program_id(0)

        @pl.when(i == 0)
        def _():
            dh_ref[...] = jnp.zeros_like(dh_ref)
            dhb_ref[...] = jnp.zeros_like(dhb_ref)
            sq_ref[...] = jnp.zeros_like(sq_ref)
            gw_ref[...] = jnp.zeros_like(gw_ref)

        @pl.when(i > 0)
        def _():
            x = h_ref[...]
            rstd = lax.rsqrt(jnp.mean(x * x, axis=-1, keepdims=True) + EPS)
            xhat = x * rstd
            w = w_ref[...]
            err = xhat * w - t_ref[...]
            sq_ref[...] += jnp.sum(err * err, axis=0, keepdims=True)
            dy = err * (1.0 / D)
            gw_ref[...] += jnp.sum(dy * xhat, axis=0, keepdims=True)
            dxh = dy * w
            dh = rstd * (dxh - xhat * jnp.mean(dxh * xhat, axis=-1, keepdims=True))
            dh_ref[...] = dh
            dhb_ref[...] = dh.astype(BF16)

    row = pl.BlockSpec((CHUNK, D), lambda i: (i, 0))
    vec = pl.BlockSpec((1, D), lambda i: (0, 0))
    return pl.pallas_call(
        body, name="loss_head", grid=(NCH,),
        in_specs=[row, vec, pl.BlockSpec((CHUNK, D), lambda i: (jnp.maximum(i - 1, 0), 0))],
        out_specs=[row, row, vec, vec],
        out_shape=[_sds((LP, D), F32), _sds((LP, D), BF16), _sds((1, D), F32), _sds((1, D), F32)],
        compiler_params=_params(("arbitrary",)),
    )(h2, wf, target)


def _dot(a, b, dims):
    return lax.dot_general(a, b, (dims, ((), ())), preferred_element_type=F32)


NN, TN, NT = ((1,), (0,)), ((0,), (0,)), ((1,), (1,))


def _tri_sum(t, x):
    hi = x.astype(BF16)
    r1 = x - hi.astype(F32)
    mid = r1.astype(BF16)
    lo = (r1 - mid.astype(F32)).astype(BF16)
    return _dot(t, hi, NN) + _dot(t, mid, NN) + _dot(t, lo, NN)


def _gla_gates(glr_ref, gw2_ref, gb_ref, rows, row0):
    g_raw = _dot(glr_ref[rows, :].astype(BF16), gw2_ref[...], NN) + gb_ref[...]
    logsig = jnp.minimum(g_raw, 0.0) - jnp.log(1.0 + jnp.exp(-jnp.abs(g_raw)))
    rid = row0 + lax.broadcasted_iota(jnp.int32, g_raw.shape, 0)
    live = rid >= PAD
    return g_raw, jnp.where(live, logsig / TAU, 0.0), live


def _tri_masks():
    r = lax.broadcasted_iota(jnp.int32, (CHUNK, CHUNK), 0)
    c = lax.broadcasted_iota(jnp.int32, (CHUNK, CHUNK), 1)
    return r >= c


def _gla_specs(rev):
    n = NCH // CPS
    R = CPS * CHUNK
    st = (lambda s: n - 1 - s) if rev else (lambda s: s)
    return R, n, st, [
        pl.BlockSpec((R, KW), lambda s: (st(s), 0)),
        pl.BlockSpec((R, KW), lambda s: (st(s), 1)),
        pl.BlockSpec((R, GW), lambda s: (st(s), 1)),
        pl.BlockSpec((R, GW), lambda s: (st(s), 2)),
        pl.BlockSpec((R, 128), lambda s: (st(s), GLR_BLK)),
    ]


def _gla_fwd(proj, gw2p, gate_b, gnw, deps=()):
    R, n, st, pspecs = _gla_specs(False)

    def body(q_ref, k_ref, v_ref, r_ref, glr_ref, gw2_ref, gb_ref, gnw_ref, og_ref, o_ref, st_ref, state):
        s = pl.program_id(0)

        @pl.when(s == 0)
        def _():
            state[...] = jnp.zeros_like(state)

        causal = _tri_masks()
        tri = causal.astype(BF16)
        for c in range(CPS):
            rows = slice(c * CHUNK, (c + 1) * CHUNK)
            _, logg, _ = _gla_gates(glr_ref, gw2_ref, gb_ref, rows, s * R + c * CHUNK)
            G = _tri_sum(tri, logg)
            g_last = G[CHUNK - 1:CHUNK, :]
            q_dec = (q_ref[rows, :] * (DK ** -0.5) * jnp.exp(G)).astype(BF16)
            kk = k_ref[rows, :]
            k_inv = (kk * jnp.exp(-G)).astype(BF16)
            k_end = (kk * jnp.exp(g_last - G)).astype(BF16)
            decay = jnp.exp(g_last)
            for h in range(H):
                lk = slice(h * DK, (h + 1) * DK)
                lv = slice(h * DV, (h + 1) * DV)
                v = v_ref[rows, lv].astype(BF16)
                S = state[h]
                st_ref[c, h] = S
                A = jnp.where(causal, _dot(q_dec[:, lk], k_inv[:, lk], NT), 0.0).astype(BF16)
                o = _dot(A, v, NN) + _dot(q_dec[:, lk], S.astype(BF16), NT)
                state[h] = decay[:, lk] * S + _dot(v, k_end[:, lk], TN)
                o_ref[rows, lv] = o
                on = o * lax.rsqrt(jnp.mean(o * o, axis=-1, keepdims=True) + EPS) * gnw_ref[...]
                rr = r_ref[rows, lv]
                og_ref[rows, lv] = (on * (rr * jax.nn.sigmoid(rr))).astype(BF16)

    full = lambda shape: pl.BlockSpec(shape, lambda s: (0,) * len(shape))
    body, dep_ins, dep_specs = _after(body, 8, deps)
    return pl.pallas_call(
        body, name="gla_fwd", grid=(n,),
        in_specs=pspecs + [full((128, KW)), full((1, KW)), full((1, DV))] + dep_specs,
        out_specs=[pl.BlockSpec((R, GW), lambda s: (s, 0)), pl.BlockSpec((R, GW), lambda s: (s, 0)),
                   pl.BlockSpec((CPS, H, DV, DK), lambda s: (s, 0, 0, 0))],
        out_shape=[_sds((LP, GW), BF16), _sds((LP, GW), F32), _sds((NCH, H, DV, DK), F32)],
        scratch_shapes=[pltpu.VMEM((H, DV, DK), F32)],
        compiler_params=_params(("arbitrary",)),
    )(proj, proj, proj, proj, proj, gw2p, gate_b, gnw, *dep_ins)


def _gla_bwd(proj, dmixed, o_saved, st_saved, gw2p, gate_b, gnw, deps=()):
    R, n, st, pspecs = _gla_specs(True)

    def body(q_ref, k_ref, v_ref, r_ref, glr_ref, dog_ref, o_ref, st_ref, gw2_ref, gb_ref, gnw_ref,
             dqkvr_ref, dglr_ref, ggn_ref, ggb_ref, ggw_ref, gstate):
        s = pl.program_id(0)

        @pl.when(s == 0)
        def _():
            gstate[...] = jnp.zeros_like(gstate)
            ggn_ref[...] = jnp.zeros_like(ggn_ref)
            ggb_ref[...] = jnp.zeros_like(ggb_ref)
            ggw_ref[...] = jnp.zeros_like(ggw_ref)

        causal = _tri_masks()
        tri = causal.astype(BF16)
        tri_up = (lax.broadcasted_iota(jnp.int32, (CHUNK, CHUNK), 0)
                  <= lax.broadcasted_iota(jnp.int32, (CHUNK, CHUNK), 1)).astype(BF16)
        gnw = gnw_ref[...]
        for c in reversed(range(CPS)):
            rows = slice(c * CHUNK, (c + 1) * CHUNK)
            g_raw, logg, live = _gla_gates(glr_ref, gw2_ref, gb_ref, rows, (n - 1 - s) * R + c * CHUNK)
            G = _tri_sum(tri, logg)
            g_last = G[CHUNK - 1:CHUNK, :]
            e_g, e_gi, e_end = jnp.exp(G), jnp.exp(-G), jnp.exp(g_last - G)
            q_dec = q_ref[rows, :] * (DK ** -0.5) * e_g
            kk = k_ref[rows, :]
            k_inv, k_end = kk * e_gi, kk * e_end
            q_dec_b, k_inv_b, k_end_b = q_dec.astype(BF16), k_inv.astype(BF16), k_end.astype(BF16)
            decay = jnp.exp(g_last)
            d_g, d_gl = [], []
            for h in range(H):
                lk = slice(h * DK, (h + 1) * DK)
                lv = slice(h * DV, (h + 1) * DV)
                o = o_ref[rows, lv]
                rr = r_ref[rows, lv]
                dog = dog_ref[rows, lv]
                rstd = lax.rsqrt(jnp.mean(o * o, axis=-1, keepdims=True) + EPS)
                ohat = o * rstd
                sr = jax.nn.sigmoid(rr)
                don = dog * (rr * sr)
                dqkvr_ref[rows, 2 * KW + GW + h * DV:2 * KW + GW + (h + 1) * DV] = (
                    dog * (ohat * gnw) * (sr * (1.0 + rr * (1.0 - sr)))).astype(BF16)
                ggn_ref[...] += jnp.sum(don * ohat, axis=0, keepdims=True)
                dohat = don * gnw
                do = (rstd * (dohat - ohat * jnp.mean(dohat * ohat, axis=-1, keepdims=True))).astype(BF16)
                v = v_ref[rows, lv].astype(BF16)
                S = st_ref[c, h]
                gS = gstate[h]
                S_b, gS_b = S.astype(BF16), gS.astype(BF16)
                qd, ki, ke = q_dec_b[:, lk], k_inv_b[:, lk], k_end_b[:, lk]
                A = jnp.where(causal, _dot(qd, ki, NT), 0.0).astype(BF16)
                dA = jnp.where(causal, _dot(do, v, NT), 0.0).astype(BF16)
                dv = _dot(A, do, TN) + _dot(ke, gS_b, NT)
                dq_dec = _dot(dA, ki, NN) + _dot(do, S_b, NN)
                dk_inv = _dot(dA, qd, TN)
                dk_end = _dot(v, gS_b, NN)
                d_decay = jnp.sum(gS * S, axis=0, keepdims=True)
                gstate[h] = decay[:, lk] * gS + _dot(do, qd, TN)
                dqkvr_ref[rows, lk] = (dq_dec * e_g[:, lk] * (DK ** -0.5)).astype(BF16)
                dqkvr_ref[rows, KW + h * DK:KW + (h + 1) * DK] = (
                    dk_inv * e_gi[:, lk] + dk_end * e_end[:, lk]).astype(BF16)
                dqkvr_ref[rows, 2 * KW + h * DV:2 * KW + (h + 1) * DV] = dv.astype(BF16)
                ke_prod = dk_end * k_end[:, lk]
                d_g.append(dq_dec * q_dec[:, lk] - dk_inv * k_inv[:, lk] - ke_prod)
                d_gl.append(jnp.sum(ke_prod, axis=0, keepdims=True) + d_decay * decay[:, lk])
            dlogg = _tri_sum(tri_up, jnp.concatenate(d_g, axis=1)) + jnp.concatenate(d_gl, axis=1)
            dg_raw = jnp.where(live, dlogg * (1.0 / TAU) * jax.nn.sigmoid(-g_raw), 0.0)
            ggb_ref[...] += jnp.sum(dg_raw, axis=0, keepdims=True)
            dg_b = dg_raw.astype(BF16)
            ggw_ref[...] += _dot(glr_ref[rows, :].astype(BF16), dg_b, TN)
            dglr_ref[rows, :] = _dot(dg_b, gw2_ref[...], NT).astype(BF16)

    full = lambda shape: pl.BlockSpec(shape, lambda s: (0,) * len(shape))
    body, dep_ins, dep_specs = _after(body, 11, deps)
    return pl.pallas_call(
        body, name="gla_bwd", grid=(n,),
        in_specs=pspecs + [pl.BlockSpec((R, GW), lambda s: (st(s), 0)), pl.BlockSpec((R, GW), lambda s: (st(s), 0)),
                           pl.BlockSpec((CPS, H, DV, DK), lambda s: (st(s), 0, 0, 0)),
                           full((128, KW)), full((1, KW)), full((1, DV))] + dep_specs,
        out_specs=[pl.BlockSpec((R, 2 * KW + 2 * GW), lambda s: (st(s), 0)), pl.BlockSpec((R, 128), lambda s: (st(s), 0)),
                   full((1, DV)), full((1, KW)), full((128, KW))],
        out_shape=[_sds((LP, 2 * KW + 2 * GW), BF16), _sds((LP, 128), BF16),
                   _sds((1, DV), F32), _sds((1, KW), F32), _sds((128, KW), F32)],
        scratch_shapes=[pltpu.VMEM((H, DV, DK), F32)],
        compiler_params=_params(("arbitrary",)),
    )(proj, proj, proj, proj, proj, dmixed, o_saved, st_saved, gw2p, gate_b, gnw, *dep_ins)


def _pool_pre(x, win, rid):
    s, step = x, 1
    while step < win:
        s = s + pltpu.roll(s, step, 0)
        step *= 2
    cnt = jnp.clip(rid - (PAD - 1), 1, win).astype(F32)
    live = rid >= PAD
    return jnp.where(live, s / cnt - x, 0.0), cnt, live


def _pool_fwd(proj, pool_w, pool_scale):
    def body(pu_ref, w_ref, sc_ref, o_ref):
        rid = lax.broadcasted_iota(jnp.int32, (LP, GC), 0)
        for g, win in enumerate(WINDOWS):
            @pl.when(pl.program_id(0) == g)
            def _():
                y, _, _ = _pool_pre(pu_ref[...], win, rid)
                o_ref[...] = (_dot(y.astype(BF16), w_ref[...], NN) * sc_ref[...]).astype(BF16)

    col = lambda base: pl.BlockSpec((LP, GC), lambda g: (0, base + g))
    return pl.pallas_call(
        body, name="pool_fwd", grid=(len(WINDOWS),),
        in_specs=[col(POOL_BLK), pl.BlockSpec((None, GC, GC), lambda g: (g, 0, 0)),
                  pl.BlockSpec((1, GC), lambda g: (0, g))],
        out_specs=col(0), out_shape=_sds((LP, PW), BF16), compiler_params=_params(("parallel",)),
    )(proj, pool_w, pool_scale)


def _pool_bwd(proj, dmixed, pool_w, pool_scale):
    def body(pu_ref, do_ref, w_ref, sc_ref, dpu_ref, dw_ref, dsc_ref):
        rid = lax.broadcasted_iota(jnp.int32, (LP, GC), 0)
        for g, win in enumerate(WINDOWS):
            @pl.when(pl.program_id(0) == g)
            def _():
                y, cnt, live = _pool_pre(pu_ref[...], win, rid)
                y_b = y.astype(BF16)
                w = w_ref[...]
                do = do_ref[...]
                dsc_ref[...] = jnp.sum(do * _dot(y_b, w, NN), axis=0, keepdims=True)
                dyw = (do * sc_ref[...]).astype(BF16)
                dw_ref[...] = _dot(y_b, dyw, TN)
                dy = jnp.where(live, _dot(dyw, w, NT), 0.0)
                s, step = dy / cnt, 1
                while step < win:
                    s = s + pltpu.roll(s, LP - step, 0)
                    step *= 2
                dpu_ref[...] = (s - dy).astype(BF16)

    col = lambda base: pl.BlockSpec((LP, GC), lambda g: (0, base + g))
    mat = pl.BlockSpec((None, GC, GC), lambda g: (g, 0, 0))
    vec = pl.BlockSpec((1, GC), lambda g: (0, g))
    return pl.pallas_call(
        body, name="pool_bwd", grid=(len(WINDOWS),),
        in_specs=[col(POOL_BLK), col(GW // GC), mat, vec], out_specs=[col(0), mat, vec],
        out_shape=[_sds((LP, PW), BF16), _sds((4, GC, GC), F32), _sds((1, PW), F32)],
        compiler_params=_params(("parallel",)),
    )(proj, dmixed, pool_w, pool_scale)


def _adamw_math(w, g, m, v):
    m = B1 * m + (1.0 - B1) * g
    v = B2 * v + (1.0 - B2) * (g * g)
    m_hat = m / (1.0 - B1 ** STEP)
    v_hat = v / (1.0 - B2 ** STEP)
    return -LR * (m_hat / (jnp.sqrt(v_hat) + AEPS) + WD * w), m, v


def _adamw_landed(sums, landed, my_chip, w, m, v, rows, name, cols=None):
    _, r, c = w.shape

    def body(chip_ref, s_ref, l_ref, w_ref, m_ref, v_ref, g_ref, d_ref, mo_ref, vo_ref):
        g = s_ref[...].astype(F32)
        for k in range(3):
            g = g + l_ref[k].astype(F32)
        g_ref[...] = g
        d_ref[...], mo_ref[...], vo_ref[...] = _adamw_math(w_ref[...], g, m_ref[...], v_ref[...])

    cols = cols or c
    blk = pl.BlockSpec((None, rows, cols), lambda i, j, chip_ref: (0, i, j))
    return pl.pallas_call(
        body, name=name, out_shape=[_sds((1, r, c), F32)] * 4,
        grid_spec=pltpu.PrefetchScalarGridSpec(
            num_scalar_prefetch=1, grid=(r // rows, c // cols),
            in_specs=[pl.BlockSpec((None, rows, cols), lambda i, j, chip_ref: (chip_ref[0], i, j)),
                      pl.BlockSpec((3, rows, cols), lambda i, j, chip_ref: (0, i, j)), blk, blk, blk],
            out_specs=[blk] * 4),
        compiler_params=_params(("parallel", "parallel")),
    )(my_chip, sums, landed, w, m, v)


def _adamw_small(g, w, m, v):
    def body(g_ref, w_ref, m_ref, v_ref, d_ref, mo_ref, vo_ref):
        d_ref[...], mo_ref[...], vo_ref[...] = _adamw_math(w_ref[...], g_ref[...], m_ref[...], v_ref[...])

    return pl.pallas_call(body, name="adamw_small", out_shape=[_sds(w.shape, F32)] * 3)(g, w, m, v)


SMALL_REPL = (("norm1_w", D), ("norm2_w", D), ("final_norm_w", D), ("pool_scale", PW), ("gate_b", KW),
              ("gla_norm_w", DV))


def _pack_rows(vecs, rows):
    flat = jnp.concatenate([jnp.ravel(v) for v in vecs])
    return jnp.pad(flat, (0, rows * 1024 - flat.shape[0])).reshape(rows, 1024)


def kernel(x, meta_tokens, norm1_w, w_in, gate_w2, gate_b, gla_norm_w, pool_w, pool_scale, w_out, norm2_w, mlp_w1, mlp_w2, final_norm_w, loss_target, m_meta_tokens, m_norm1_w, m_w_in, m_gate_w2, m_gate_b, m_gla_norm_w, m_pool_w, m_pool_scale, m_w_out, m_norm2_w, m_mlp_w1, m_mlp_w2, m_final_norm_w, v_meta_tokens, v_norm1_w, v_w_in, v_gate_w2, v_gate_b, v_gla_norm_w, v_pool_w, v_pool_scale, v_w_out, v_norm2_w, v_mlp_w1, v_mlp_w2, v_final_norm_w):
    me = 4 * lax.axis_index("x") + 2 * lax.axis_index("y") + lax.axis_index("c")
    W = dict(meta_tokens=meta_tokens, norm1_w=norm1_w, w_in=w_in, gate_w2=gate_w2, gate_b=gate_b,
             gla_norm_w=gla_norm_w, pool_w=pool_w, pool_scale=pool_scale, w_out=w_out, norm2_w=norm2_w,
             mlp_w1=mlp_w1, mlp_w2=mlp_w2, final_norm_w=final_norm_w)
    Mo = dict(meta_tokens=m_meta_tokens, norm1_w=m_norm1_w, w_in=m_w_in, gate_w2=m_gate_w2, gate_b=m_gate_b,
              gla_norm_w=m_gla_norm_w, pool_w=m_pool_w, pool_scale=m_pool_scale, w_out=m_w_out, norm2_w=m_norm2_w,
              mlp_w1=m_mlp_w1, mlp_w2=m_mlp_w2, final_norm_w=m_final_norm_w)
    Vo = dict(meta_tokens=v_meta_tokens, norm1_w=v_norm1_w, w_in=v_w_in, gate_w2=v_gate_w2, gate_b=v_gate_b,
              gla_norm_w=v_gla_norm_w, pool_w=v_pool_w, pool_scale=v_pool_scale, w_out=v_w_out, norm2_w=v_norm2_w,
              mlp_w1=v_mlp_w1, mlp_w2=v_mlp_w2, final_norm_w=v_final_norm_w)

    small = _exchange_small(_pack_rows([meta_tokens, gate_w2[0]], 8), False, "gather_small")
    ex = _Exchange(dict(w_in=w_in[0].T, w_out=w_out[0], pool_w=pool_w[0].reshape(4 * 32, GC), mlp_w1=mlp_w1[0],
                        mlp_w2=mlp_w2[0]), small)
    meta_full = small[:, 0:4].reshape(NDEV, N_META, D // NDEV).transpose(1, 0, 2).reshape(N_META, D)
    gw2_full = small[:, 4].reshape(NDEV, RANK, KW // NDEV).transpose(1, 0, 2).reshape(RANK, KW)
    gw2p = jnp.pad(gw2_full, ((0, 128 - RANK), (0, 0))).astype(BF16)

    step = _layer_step(x[0], loss_target[0], meta_full, gw2p, ex, norm1_w, gate_b, gla_norm_w, pool_scale, norm2_w,
                       final_norm_w.reshape(1, D))
    grad_x = step["dh0"][ROW_X:][None]

    loss_part = 0.5 * jnp.sum(step["sq"]) / D
    packed = jnp.concatenate([
        _pack_rows([step[k] for k, _ in SMALL_REPL] + [loss_part], 8),
        step["gate_w2"][:RANK].reshape(8, 1024), step["dh0"][PAD:ROW_X].reshape(32, 1024)], axis=0)
    red = _exchange_small(packed, True, "reduce_small")
    loss = red[7, 768]
    g_gw2_mine = lax.dynamic_slice(red[8:16].reshape(RANK, KW), (0, me * (KW // NDEV)), (RANK, KW // NDEV))
    g_meta_mine = lax.dynamic_slice(red[16:48].reshape(N_META, D), (0, me * (D // NDEV)), (N_META, D // NDEV))

    last = ex.grad_mid("w_in", red)
    out = {}
    for group in ("down", "up", "mix"):
        for k, (sums, landed) in ex.grad_finish(group, last).items():
            out[k] = _adamw_landed(sums, landed, ex.my_chip, W[k], Mo[k], Vo[k], SHARD_ROWS[k], "adamw_" + k)
            last = out[k][1]
    done = ex.grad_finish("w_in", last)
    poolw3 = lambda a: a.reshape(1, 4 * 32, GC)
    res = _adamw_landed(*done["pool_w"], ex.my_chip, poolw3(pool_w), poolw3(m_pool_w), poolw3(v_pool_w),
                        SHARD_ROWS["pool_w"], "adamw_pool_w")
    out["pool_w"] = [a.reshape(pool_w.shape) for a in res]
    tr = lambda a: a[0].T[None]
    res = _adamw_landed(*done["w_in"], ex.my_chip, tr(w_in), tr(m_w_in), tr(v_w_in), D_IN // NDEV, "adamw_w_in",
                        cols=256)
    out["w_in"] = [a[0].T[None] for a in res]

    def small_pack(P):
        return jnp.concatenate([_pack_rows([P[k] for k, _ in SMALL_REPL], 8),
                                _pack_rows([P["meta_tokens"], P["gate_w2"]], 8)], axis=0)

    g_small = jnp.concatenate([red[0:8], _pack_rows([g_meta_mine, g_gw2_mine], 8)], axis=0)
    g_small = g_small.at[7, 768].set(0.0)
    res_small = _adamw_small(g_small, small_pack(W), small_pack(Mo), small_pack(Vo))
    res_small = [g_small] + list(res_small)
    off = 0
    for k, nel in SMALL_REPL:
        out[k] = [a[0:8].reshape(-1)[off:off + nel].reshape(W[k].shape) for a in res_small]
        off += nel
    out["meta_tokens"] = [a[8:12].reshape(N_META, D // NDEV) for a in res_small]
    out["gate_w2"] = [a[12].reshape(1, RANK, KW // NDEV) for a in res_small]

    order = ["meta_tokens", "norm1_w", "w_in", "gate_w2", "gate_b", "gla_norm_w", "pool_w", "pool_scale", "w_out",
             "norm2_w", "mlp_w1", "mlp_w2", "final_norm_w"]
    return (loss, grad_x, *[out[k][0] for k in order], *[out[k][1] for k in order],
            *[out[k][2] for k in order], *[out[k][3] for k in order])


SHARD_ROWS = dict(w_in=256, w_out=64, mlp_w1=128, mlp_w2=64, pool_w=128)
C_GLR = 2 * KW + 2 * GW
GATHER_GROUPS = dict(w_in=("w_in",), mix=("w_out", "pool_w"), up=("mlp_w1",), down=("mlp_w2",))
GRAD_GROUPS = dict(down=("mlp_w2",), up=("mlp_w1",), mix=("w_out",), w_in=("pool_w", "w_in"))


class _Exchange:
    def __init__(self, shards, after):
        names = list(shards)
        started, _ = _gather_start([shards[k].astype(BF16) for k in names], "gather_start", after)
        self.state = dict(zip(names, started))
        self.my_c = lax.axis_index("c").astype(jnp.int32).reshape(1)
        self.my_chip = (2 * lax.axis_index("x") + lax.axis_index("y")).astype(jnp.int32).reshape(1)
        self.sibling, self.chips = {}, {}

    def forward(self, group, after):
        ks = GATHER_GROUPS[group]
        fwd, token = _gather_forward([self.state[k] for k in ks], after, "gather_forward_" + group)
        self.state.update(zip(ks, fwd))
        return token

    def weights(self, group, after):
        ks = GATHER_GROUPS[group]
        g = dict(zip(ks, _gather_finish([self.state[k] for k in ks], after, "gather_finish_" + group)))
        if group == "w_in":
            nat = g["w_in"].reshape(D_IN, D)
            return jnp.concatenate([nat[:C_GLR], nat[C_GLR + RANK:], nat[C_GLR:C_GLR + RANK],
                                    jnp.zeros((D_INP - D_IN, D), BF16)], axis=0)
        if group == "mix":
            return (g["w_out"].reshape(D, D),
                    g["pool_w"].reshape(NDEV, 4, 32, GC).transpose(1, 0, 2, 3).reshape(4, GC, GC))
        return g["mlp_w1"] if group == "up" else g["mlp_w2"].reshape(DFF, D)

    def grad(self, group, grads):
        parts = dict(grads)
        if group == "w_in":
            g = parts["w_in"]
            nat = jnp.concatenate([g[:C_GLR], g[C_GLR + PW:C_GLR + PW + RANK], g[C_GLR:C_GLR + PW]], axis=0)
            parts["w_in"] = nat.reshape(4, 2, D_IN // NDEV, D).transpose(1, 0, 2, 3)
            parts["pool_w"] = (parts["pool_w"].astype(BF16).reshape(4, 4, 2, 32, GC).transpose(2, 1, 0, 3, 4)
                               .reshape(2, 4, 4 * 32, GC))
        ks = GRAD_GROUPS[group]
        started, token = _to_sibling_start([parts[k] for k in ks], "grad_sibling_start_" + group)
        self.sibling[group] = started
        return token

    def grad_mid(self, group, after):
        ks = GRAD_GROUPS[group]
        both = _to_sibling_finish(self.sibling[group], after, "grad_sibling_finish_" + group)
        tile = lambda k, p: (p.shape[2], 512) if k == "w_in" else (SHARD_ROWS[k], p.shape[3])
        sums = [_chip_sum(p, s, self.my_c, tile(k, p), "chip_sum_" + k) for k, (p, s) in zip(ks, both)]
        self.chips[group], token = _to_chips_start(sums, "grad_chips_start_" + group)
        return token

    def grad_finish(self, group, after):
        done = _to_chips_finish(self.chips[group], after, "grad_chips_finish_" + group)
        return dict(zip(GRAD_GROUPS[group], done))


def _layer_step(x, target, meta_full, gw2p, ex, norm1_w, gate_b, gla_norm_w, pool_scale, norm2_w, final_norm_w):
    h0 = jnp.concatenate([jnp.zeros((PAD, D), F32), meta_full, x], axis=0)
    u1 = _rmsnorm_fwd(h0, norm1_w, "rmsnorm1")
    win_p = ex.weights("w_in", ex.forward("w_in", u1))
    proj = _matmul(u1, win_p, mode="nt", tm=1056, tn=1408, tk=2048, name="proj")
    tok = ex.forward("mix", proj)
    og, o_saved, st_saved = _gla_fwd(proj, gw2p, gate_b, gla_norm_w, deps=[tok])
    wout_f, poolw_f = ex.weights("mix", og)
    op = _pool_fwd(proj, poolw_f, pool_scale)
    mixed = jnp.concatenate([og, op], axis=1)
    h1 = _matmul(mixed, wout_f, mode="nn", tm=1056, tn=1024, tk=2048, name="mix_out", epi="add", extra=h0)
    tok = ex.forward("up", h1)
    u2 = _rmsnorm_fwd(h1, norm2_w, "rmsnorm2", deps=[tok])
    w1_g = ex.weights("up", u2)
    z, act = _matmul(u2, w1_g, mode="nn", tm=1056, tn=1024, tk=2048, name="mlp_up", epi="relu2", b_slots=True)
    w2_f = ex.weights("down", ex.forward("down", act))
    h2 = _matmul(act, w2_f, mode="nn", tm=1056, tn=1024, tk=2048, name="mlp_down", epi="add", extra=h1)
    dh2, dh2b, sq, g_fnw = _loss_head(h2, final_norm_w, target)

    g_w2 = _matmul(act, dh2b, mode="tn", tm=512, tn=1024, tk=LP, name="d_mlp_w2", out_dtype=BF16, out_slots="rows")
    tok = ex.grad("down", dict(mlp_w2=g_w2))
    dz = _matmul(dh2b, w2_f, mode="nt", tm=1056, tn=1024, tk=2048, name="d_act", out_dtype=BF16, epi="dz", extra=z,
                 deps=[tok])
    tok = ex.grad_mid("down", dz)
    g_w1 = _matmul(u2, dz, mode="tn", tm=512, tn=1024, tk=LP, name="d_mlp_w1", out_dtype=BF16, out_slots="cols",
                   deps=[tok])
    tok = ex.grad("up", dict(mlp_w1=g_w1))
    du2 = _matmul(dz, w1_g, mode="nt", tm=1056, tn=1024, tk=1024, name="d_u2", b_slots=True, deps=[tok])
    tok = ex.grad_mid("up", du2)
    dh1, dh1b, g_n2 = _rmsnorm_bwd(h1, norm2_w, du2, dh2, "rmsnorm2_bwd", deps=[tok])
    g_wout = _matmul(mixed, dh1b, mode="tn", tm=256, tn=1024, tk=LP, name="d_w_out", out_dtype=BF16, out_slots="rows")
    tok = ex.grad("mix", dict(w_out=g_wout))
    dmixed = _matmul(dh1b, wout_f, mode="nt", tm=1056, tn=1024, tk=2048, name="d_mixed", deps=[tok])
    tok = ex.grad_mid("mix", dmixed)
    dqkvr, dglr, g_gnw, g_gb, g_gw2 = _gla_bwd(proj, dmixed, o_saved, st_saved, gw2p, gate_b, gla_norm_w, deps=[tok])
    dpu, g_poolw, g_psc = _pool_bwd(proj, dmixed, poolw_f, pool_scale)
    dproj = jnp.concatenate([dqkvr, dpu, dglr], axis=1)
    g_win_p = _matmul(dproj, u1, mode="tn", tm=1408, tn=1024, tk=LP, name="d_w_in", out_dtype=BF16)
    tok = ex.grad("w_in", dict(pool_w=g_poolw, w_in=g_win_p))
    du1 = _matmul(dproj, win_p, mode="nn", tm=1056, tn=1024, tk=1408, name="d_u1", deps=[tok])
    dh0, _, g_n1 = _rmsnorm_bwd(h0, norm1_w, du1, dh1, "rmsnorm1_bwd")
    return dict(dh0=dh0, sq=sq, gate_w2=g_gw2, norm1_w=g_n1, norm2_w=g_n2, final_norm_w=g_fnw, pool_scale=g_psc, gate_b=g_gb, gla_norm_w=g_gnw)
```

```python
import functools

import jax
import jax.numpy as jnp
from jax import lax
from jax.experimental import pallas as pl
from jax.experimental.pallas import tpu as pltpu

F32, BF16 = jnp.float32, jnp.bfloat16
MESH = pl.DeviceIdType.MESH

NDEV = 8
D = 2048
SEQ = 2048
N_META = 16
CHUNK = 64
PAD = (-N_META) % CHUNK
ROW_X = PAD + N_META
LP = ROW_X + SEQ
NCH = LP // CHUNK
H = 4
DK = 128
DV = 256
KW = H * DK
GW = H * DV
PW = 1024
RANK = 16
TAU = 16.0
WINDOWS = (2, 4, 8, 16)
GC = 256
DFF = 4 * D
EPS = 1e-6
D_IN = 2 * KW + 2 * GW + RANK + PW
D_INP = 4224
GLR_BLK = (2 * KW + 2 * GW + PW) // 128
POOL_BLK = (2 * KW + 2 * GW) // GC
LR, B1, B2, AEPS, WD, STEP = 0.001, 0.9, 0.999, 1e-08, 0.01, 10
VMEM_LIMIT = 48 * 1024 * 1024
CPS = 3


def _params(sem=None):
    return pltpu.CompilerParams(dimension_semantics=sem, vmem_limit_bytes=VMEM_LIMIT)


def _sds(shape, dtype):
    return jax.ShapeDtypeStruct(shape, dtype)


def _me():
    return lax.axis_index("x"), lax.axis_index("y"), lax.axis_index("c")


def _peer(j):
    x, y, c = _me()
    return (x ^ ((j >> 2) & 1), y ^ ((j >> 1) & 1), c ^ (j & 1))


def _slot(dev):
    return 4 * dev[0] + 2 * dev[1] + dev[2]


HBM_SPEC = pl.BlockSpec(memory_space=pltpu.HBM)
SEM_SPEC = pl.BlockSpec(memory_space=pltpu.SEMAPHORE)
ANY_SPEC = pl.BlockSpec(memory_space=pl.ANY)
EFFECT = pltpu.SideEffectType.DATAFLOW_SIDE_EFFECTING
SIBLING = 1
OTHER_CHIPS = (2, 4, 6)


def _in_hbm(a):
    return pltpu.with_memory_space_constraint(a, pltpu.HBM)


def _chip(dev):
    return 2 * dev[0] + dev[1]


def _rcopy(src, dst, send_sem, recv_sem, to):
    return pltpu.make_async_remote_copy(src_ref=src, dst_ref=dst, send_sem=send_sem, recv_sem=recv_sem,
                                        device_id=to, device_id_type=MESH)


def _split_call(body, name, ins, in_specs, out_shape, out_specs, aliases, scratch=()):
    n = len(ins) + len(out_shape)

    def with_token(*refs):
        body(*refs[:n], *refs[n + 1:])
        refs[n][...] = jnp.zeros_like(refs[n])

    return pl.pallas_call(
        with_token, name=name, in_specs=in_specs, out_shape=list(out_shape) + [_sds((8, 128), F32)],
        out_specs=list(out_specs) + [pl.BlockSpec(memory_space=pltpu.VMEM)],
        input_output_aliases=aliases, scratch_shapes=list(scratch),
        compiler_params=pltpu.CompilerParams(has_side_effects=EFFECT),
    )(*ins)


def _after(body, n_in, deps):
    deps = [d for d in deps if d is not None]
    if not deps:
        return body, [], []
    return (lambda *refs: body(*refs[:n_in], *refs[n_in + len(deps):])), deps, [ANY_SPEC] * len(deps)


def _gather_start(shards, name, after):
    n = len(shards)
    me = _slot(_me())
    lands = [lax.dynamic_update_slice(lax.empty((NDEV,) + s.shape, s.dtype), s[None], (me, 0, 0)) for s in shards]

    def body(*refs):
        src, land = refs[:n], refs[n:2 * n]
        outs = refs[2 * n + 1:]
        for i in range(n):
            send_sems, recv_sems = outs[4 * i], outs[4 * i + 1]
            for k, rel in enumerate((SIBLING,) + OTHER_CHIPS):
                _rcopy(src[i], land[i].at[_slot(_me())], send_sems.at[k], recv_sems.at[k], _peer(rel)).start()

    out_shape, out_specs, aliases = [], [], {}
    for i, s in enumerate(shards):
        out_shape += [pltpu.SemaphoreType.DMA((4,)), pltpu.SemaphoreType.DMA((4,)), pltpu.HBM(s.shape, s.dtype),
                      pltpu.HBM((NDEV,) + s.shape, s.dtype)]
        out_specs += [SEM_SPEC, SEM_SPEC, HBM_SPEC, HBM_SPEC]
        aliases[i] = 4 * i + 2
        aliases[n + i] = 4 * i + 3
    res = _split_call(body, name, [_in_hbm(s) for s in shards] + [_in_hbm(l) for l in lands] + [after],
                      [HBM_SPEC] * (2 * n) + [ANY_SPEC], out_shape, out_specs, aliases)
    return [tuple(res[4 * i:4 * i + 4]) for i in range(n)], res[-1]


def _gather_forward(started, after, name):
    n = len(started)

    def body(*refs):
        land, recv1 = refs[:n], refs[n:2 * n]
        outs = refs[2 * n + 1:]
        for i in range(n):
            send2, recv2 = outs[3 * i + 1], outs[3 * i + 2]
            for k, rel in enumerate(OTHER_CHIPS):
                blk = land[i].at[_slot(_peer(rel))]
                _rcopy(blk, blk, send2.at[k], recv1[i].at[1 + k], _peer(rel)).wait_recv()
                _rcopy(blk, blk, send2.at[k], recv2.at[k], _peer(SIBLING)).start()

    ins = [_in_hbm(st[3]) for st in started] + [st[1] for st in started] + [after]
    out_shape, out_specs, aliases = [], [], {}
    for i, st in enumerate(started):
        out_shape += [pltpu.HBM(st[3].shape, st[3].dtype), pltpu.SemaphoreType.DMA((3,)), pltpu.SemaphoreType.DMA((3,))]
        out_specs += [HBM_SPEC, SEM_SPEC, SEM_SPEC]
        aliases[i] = 3 * i
    res = _split_call(body, name, ins, [HBM_SPEC] * n + [SEM_SPEC] * n + [ANY_SPEC], out_shape, out_specs, aliases)
    return [(st[0], st[1], st[2], res[3 * i], res[3 * i + 1], res[3 * i + 2]) for i, st in enumerate(started)], res[-1]


def _gather_finish(forwarded, after, name):
    n = len(forwarded)

    def body(*refs):
        for i in range(n):
            send1, recv1, src, land, send2, recv2 = refs[6 * i:6 * i + 6]
            me = _slot(_me())
            sib = _slot(_peer(SIBLING))
            for k, rel in enumerate((SIBLING,) + OTHER_CHIPS):
                _rcopy(src, land.at[me], send1.at[k], recv1.at[k], _peer(rel)).wait_send()
            _rcopy(src, land.at[sib], send1.at[0], recv1.at[0], _peer(SIBLING)).wait_recv()
            for k, rel in enumerate(OTHER_CHIPS):
                mine, theirs = land.at[_slot(_peer(rel))], land.at[_slot(_peer(rel ^ SIBLING))]
                _rcopy(mine, mine, send2.at[k], recv2.at[k], _peer(SIBLING)).wait_send()
                _rcopy(theirs, theirs, send2.at[k], recv2.at[k], _peer(SIBLING)).wait_recv()

    ins, in_specs, out_shape, aliases = [], [], [], {}
    for i, f in enumerate(forwarded):
        ins += [f[0], f[1], _in_hbm(f[2]), _in_hbm(f[3]), f[4], f[5]]
        in_specs += [SEM_SPEC, SEM_SPEC, HBM_SPEC, HBM_SPEC, SEM_SPEC, SEM_SPEC]
        out_shape.append(pltpu.HBM(f[3].shape, f[3].dtype))
        aliases[6 * i + 3] = i
    res = _split_call(body, name, ins + [after], in_specs + [ANY_SPEC], out_shape, [HBM_SPEC] * n, aliases)
    return list(res[:-1])


def _to_sibling_start(parts, name):
    n = len(parts)
    lands = [lax.empty(p.shape[1:], p.dtype) for p in parts]

    def body(*refs):
        src, land = refs[:n], refs[n:2 * n]
        outs = refs[2 * n:]
        other = 1 - lax.axis_index("c")
        for i in range(n):
            _rcopy(src[i].at[other], land[i], outs[4 * i], outs[4 * i + 1], _peer(SIBLING)).start()

    out_shape, out_specs, aliases = [], [], {}
    for i, p in enumerate(parts):
        out_shape += [pltpu.SemaphoreType.DMA(()), pltpu.SemaphoreType.DMA(()), pltpu.HBM(p.shape, p.dtype),
                      pltpu.HBM(p.shape[1:], p.dtype)]
        out_specs += [SEM_SPEC, SEM_SPEC, HBM_SPEC, HBM_SPEC]
        aliases[i] = 4 * i + 2
        aliases[n + i] = 4 * i + 3
    res = _split_call(body, name, [_in_hbm(p) for p in parts] + [_in_hbm(l) for l in lands], [HBM_SPEC] * (2 * n),
                      out_shape, out_specs, aliases)
    return [tuple(res[4 * i:4 * i + 4]) for i in range(n)], res[-1]


def _to_sibling_finish(started, after, name):
    n = len(started)

    def body(*refs):
        for i in range(n):
            send, recv, src, land = refs[4 * i:4 * i + 4]
            cp = _rcopy(src.at[0], land, send, recv, _peer(SIBLING))
            cp.wait_send()
            cp.wait_recv()

    ins, in_specs, out_shape, aliases = [], [], [], {}
    for i, st in enumerate(started):
        ins += [st[0], st[1], _in_hbm(st[2]), _in_hbm(st[3])]
        in_specs += [SEM_SPEC, SEM_SPEC, HBM_SPEC, HBM_SPEC]
        out_shape += [pltpu.HBM(st[2].shape, st[2].dtype), pltpu.HBM(st[3].shape, st[3].dtype)]
        aliases[4 * i + 2] = 2 * i
        aliases[4 * i + 3] = 2 * i + 1
    res = _split_call(body, name, ins + [after], in_specs + [ANY_SPEC], out_shape, [HBM_SPEC] * (2 * n), aliases)
    return [(res[2 * i], res[2 * i + 1]) for i in range(n)]


def _chip_sum(parts, from_sibling, my_c, tile, name):
    _, _, r, c = parts.shape
    tr, tc = tile

    def body(c_ref, p_ref, s_ref, o_ref):
        o_ref[...] = (p_ref[...].astype(F32) + s_ref[...].astype(F32)).astype(o_ref.dtype)

    blk = pl.BlockSpec((4, tr, tc), lambda i, j, c_ref: (0, i, j))
    return pl.pallas_call(
        body, name=name, out_shape=_sds((4, r, c), parts.dtype),
        grid_spec=pltpu.PrefetchScalarGridSpec(
            num_scalar_prefetch=1, grid=(r // tr, c // tc),
            in_specs=[pl.BlockSpec((None, 4, tr, tc), lambda i, j, c_ref: (c_ref[0], 0, i, j)), blk], out_specs=blk),
        compiler_params=_params(("parallel", "parallel")),
    )(my_c, parts, from_sibling)


def _to_chips_start(sums, name):
    n = len(sums)
    lands = [lax.empty((3,) + s.shape[1:], s.dtype) for s in sums]

    def body(*refs):
        src, land = refs[:n], refs[n:2 * n]
        outs = refs[2 * n:]
        for i in range(n):
            for k, rel in enumerate(OTHER_CHIPS):
                to = _peer(rel)
                _rcopy(src[i].at[_chip(to)], land[i].at[k], outs[4 * i].at[k], outs[4 * i + 1].at[k], to).start()

    out_shape, out_specs, aliases = [], [], {}
    for i, s in enumerate(sums):
        out_shape += [pltpu.SemaphoreType.DMA((3,)), pltpu.SemaphoreType.DMA((3,)), pltpu.HBM(s.shape, s.dtype),
                      pltpu.HBM((3,) + s.shape[1:], s.dtype)]
        out_specs += [SEM_SPEC, SEM_SPEC, HBM_SPEC, HBM_SPEC]
        aliases[i] = 4 * i + 2
        aliases[n + i] = 4 * i + 3
    res = _split_call(body, name, [_in_hbm(s) for s in sums] + [_in_hbm(l) for l in lands], [HBM_SPEC] * (2 * n),
                      out_shape, out_specs, aliases)
    return [tuple(res[4 * i:4 * i + 4]) for i in range(n)], res[-1]


def _to_chips_finish(started, after, name):
    n = len(started)

    def body(*refs):
        for i in range(n):
            send, recv, src, land = refs[4 * i:4 * i + 4]
            for k, rel in enumerate(OTHER_CHIPS):
                cp = _rcopy(src.at[0], land.at[k], send.at[k], recv.at[k], _peer(rel))
                cp.wait_send()
                cp.wait_recv()

    ins, in_specs, out_shape, aliases = [], [], [], {}
    for i, st in enumerate(started):
        ins += [st[0], st[1], _in_hbm(st[2]), _in_hbm(st[3])]
        in_specs += [SEM_SPEC, SEM_SPEC, HBM_SPEC, HBM_SPEC]
        out_shape += [pltpu.HBM(st[2].shape, st[2].dtype), pltpu.HBM(st[3].shape, st[3].dtype)]
        aliases[4 * i + 2] = 2 * i
        aliases[4 * i + 3] = 2 * i + 1
    res = _split_call(body, name, ins + [after], in_specs + [ANY_SPEC], out_shape, [HBM_SPEC] * (2 * n), aliases)
    return [(res[2 * i], res[2 * i + 1]) for i in range(n)]


def _exchange_small(v, reduce, name, deps=()):
    r, c = v.shape

    def body(v_ref, o_ref, land, send_sems, recv_sems):
        me = _slot(_me())
        copies = []
        for j in range(1, NDEV):
            cp = pltpu.make_async_remote_copy(
                src_ref=v_ref, dst_ref=land.at[me], send_sem=send_sems.at[j - 1],
                recv_sem=recv_sems.at[j - 1], device_id=_peer(j), device_id_type=MESH)
            cp.start()
            copies.append(cp)
        land[me] = v_ref[...]
        for cp in copies:
            cp.wait()
        if reduce:
            acc = land[0]
            for k in range(1, NDEV):
                acc = acc + land[k]
            o_ref[...] = acc
        else:
            o_ref[...] = land[...]

    vm = pl.BlockSpec(memory_space=pltpu.VMEM)
    body, dep_ins, dep_specs = _after(body, 1, deps)
    return pl.pallas_call(
        body, name=name,
        out_shape=_sds((r, c) if reduce else (NDEV, r, c), F32),
        in_specs=[vm] + dep_specs, out_specs=vm,
        scratch_shapes=[pltpu.VMEM((NDEV, r, c), F32), pltpu.SemaphoreType.DMA((NDEV - 1,)),
                        pltpu.SemaphoreType.DMA((NDEV - 1,))],
        compiler_params=_params(),
    )(v, *dep_ins)


def _matmul(a, b, *, mode, tm, tn, tk, name, out_dtype=F32, epi=None, extra=None, b_slots=False, out_slots=False,
            deps=()):
    slot_w = b.shape[-1] if b_slots else None
    if mode == "nn":
        M, K = a.shape
        N = NDEV * slot_w if b_slots else b.shape[1]
    elif mode == "tn":
        K, M = a.shape
        N = b.shape[1]
    else:
        M, K = a.shape
        N = b.shape[-2]
        if b_slots:
            assert K == NDEV * slot_w and tk == slot_w
    if mode == "nn" and b_slots:
        assert tn == slot_w
    if out_slots == "cols":
        assert tn * NDEV == N
    if out_slots == "rows":
        assert (M // NDEV) % tm == 0
    assert M % tm == 0 and N % tn == 0 and K % tk == 0, (name, M, N, K, tm, tn, tk)
    nk = K // tk
    dims = {"nn": ((1,), (0,)), "tn": ((0,), (0,)), "nt": ((1,), (1,))}[mode]

    if mode == "tn":
        a_spec = pl.BlockSpec((tk, tm), lambda i, j, k: (k, i))
    else:
        a_spec = pl.BlockSpec((tm, tk), lambda i, j, k: (i, k))
    if mode == "nt":
        b_spec = (pl.BlockSpec((None, tn, tk), lambda i, j, k: (k, j, 0)) if b_slots
                  else pl.BlockSpec((tn, tk), lambda i, j, k: (j, k)))
    else:
        b_spec = (pl.BlockSpec((None, tk, tn), lambda i, j, k: (j, k, 0)) if b_slots
                  else pl.BlockSpec((tk, tn), lambda i, j, k: (k, j)))
    tile = pl.BlockSpec((tm, tn), lambda i, j, k: (i, j))
    if out_slots == "cols":
        out_spec = pl.BlockSpec((None, None, tm, tn), lambda i, j, k: (j % 2, j // 2, i, 0))
        out_shape = _sds((2, 4, M, tn), out_dtype)
    elif out_slots == "rows":
        per = M // NDEV // tm
        out_spec = pl.BlockSpec((None, None, tm, tn), lambda i, j, k: ((i // per) % 2, (i // per) // 2, i % per, j))
        out_shape = _sds((2, 4, M // NDEV, N), out_dtype)
    else:
        out_spec, out_shape = tile, _sds((M, N), out_dtype)
    ins, in_specs = [a, b], [a_spec, b_spec]
    if epi in ("add", "dz"):
        ins.append(extra)
        in_specs.append(tile)
    if epi == "relu2":
        out_specs, out_shapes = [tile, tile], [_sds((M, N), F32), _sds((M, N), BF16)]
    else:
        out_specs, out_shapes = out_spec, out_shape
    n_in = len(ins)

    def body(*refs):
        outs = refs[n_in:-1] if nk > 1 else refs[n_in:]

        def finish(p):
            if epi is None:
                outs[0][...] = p.astype(out_dtype)
            elif epi == "add":
                outs[0][...] = (p + refs[2][...]).astype(out_dtype)
            elif epi == "relu2":
                outs[0][...] = p
                rz = jnp.maximum(p, 0.0)
                outs[1][...] = (rz * rz).astype(BF16)
            else:
                outs[0][...] = (p * (2.0 * jnp.maximum(refs[2][...], 0.0))).astype(out_dtype)

        def product():
            return lax.dot_general(refs[0][...].astype(BF16), refs[1][...].astype(BF16), (dims, ((), ())),
                                   preferred_element_type=F32)

        if nk == 1:
            finish(product())
            return
        acc = refs[-1]
        k = pl.program_id(2)

        @pl.when(k == 0)
        def _():
            acc[...] = jnp.zeros_like(acc)

        acc[...] += product()

        @pl.when(k == nk - 1)
        def _():
            finish(acc[...])

    body, dep_ins, dep_specs = _after(body, n_in, deps)
    return pl.pallas_call(
        body, name=name, grid=(M // tm, N // tn, nk),
        in_specs=in_specs + dep_specs, out_specs=out_specs, out_shape=out_shapes,
        scratch_shapes=[pltpu.VMEM((tm, tn), F32)] if nk > 1 else [],
        compiler_params=_params(("parallel", "parallel", "arbitrary")),
    )(*ins, *dep_ins)


ROWS = 352


def _rmsnorm_fwd(h, w, name, deps=()):
    def body(h_ref, w_ref, u_ref):
        x = h_ref[...]
        rstd = lax.rsqrt(jnp.mean(x * x, axis=-1, keepdims=True) + EPS)
        u_ref[...] = (x * rstd * w_ref[...]).astype(BF16)

    row = pl.BlockSpec((ROWS, D), lambda i: (i, 0))
    body, dep_ins, dep_specs = _after(body, 2, deps)
    return pl.pallas_call(
        body, name=name, grid=(LP // ROWS,), in_specs=[row, pl.BlockSpec((1, D), lambda i: (0, 0))] + dep_specs,
        out_specs=row, out_shape=_sds((LP, D), BF16), compiler_params=_params(("parallel",)),
    )(h, w, *dep_ins)


def _rmsnorm_bwd(h, w, du, dres, name, deps=()):
    def body(h_ref, w_ref, du_ref, dres_ref, dh_ref, dhb_ref, gw_ref):
        x = h_ref[...]
        rstd = lax.rsqrt(jnp.mean(x * x, axis=-1, keepdims=True) + EPS)
        xhat = x * rstd
        dy = du_ref[...]
        dxh = dy * w_ref[...]
        dh = dres_ref[...] + rstd * (dxh - xhat * jnp.mean(dxh * xhat, axis=-1, keepdims=True))
        dh_ref[...] = dh
        dhb_ref[...] = dh.astype(BF16)
        part = jnp.sum(dy * xhat, axis=0, keepdims=True)

        @pl.when(pl.program_id(0) == 0)
        def _():
            gw_ref[...] = part

        @pl.when(pl.program_id(0) > 0)
        def _():
            gw_ref[...] += part

    row = pl.BlockSpec((ROWS, D), lambda i: (i, 0))
    vec = pl.BlockSpec((1, D), lambda i: (0, 0))
    body, dep_ins, dep_specs = _after(body, 4, deps)
    return pl.pallas_call(
        body, name=name, grid=(LP // ROWS,), in_specs=[row, vec, row, row] + dep_specs, out_specs=[row, row, vec],
        out_shape=[_sds((LP, D), F32), _sds((LP, D), BF16), _sds((1, D), F32)],
        compiler_params=_params(("arbitrary",)),
    )(h, w, du, dres, *dep_ins)


def _loss_head(h2, wf, target):
    def body(h_ref, w_ref, t_ref, dh_ref, dhb_ref, sq_ref, gw_ref):
        i = pl.program_id(0)

        @pl.when(i == 0)
        def _():
            sq_ref[...] = jnp.zeros_like(sq_ref)
            gw_ref[...] = jnp.zeros_like(gw_ref)

        def rows(t, live):
            x = h_ref[...]
            rstd = lax.rsqrt(jnp.mean(x * x, axis=-1, keepdims=True) + EPS)
            xhat = x * rstd
            w = w_ref[...]
            err = xhat * w - t
            if live is not None:
                err = jnp.where(live, err, 0.0)
            sq_ref[...] += jnp.sum(err * err, axis=0, keepdims=True)
            dy = err * (1.0 / D)
            gw_ref[...] += jnp.sum(dy * xhat, axis=0, keepdims=True)
            dxh = dy * w
            dh = rstd * (dxh - xhat * jnp.mean(dxh * xhat, axis=-1, keepdims=True))
            dh_ref[...] = dh
            dhb_ref[...] = dh.astype(BF16)

        @pl.when(i == 0)
        def _():
            rid = lax.broadcasted_iota(jnp.int32, (ROWS, D), 0)
            rows(pltpu.roll(t_ref[...], ROW_X, 0), rid >= ROW_X)

        @pl.when(i > 0)
        def _():
            rows(t_ref[...], None)

    row = pl.BlockSpec((ROWS, D), lambda i: (i, 0))
    vec = pl.BlockSpec((1, D), lambda i: (0, 0))
    tgt = pl.BlockSpec((pl.Element(ROWS), pl.Element(D)),
                       lambda i: (pl.multiple_of(jnp.maximum(ROWS * i - ROW_X, 0), 8), 0))
    return pl.pallas_call(
        body, name="loss_head", grid=(LP // ROWS,),
        in_specs=[row, vec, tgt],
        out_specs=[row, row, vec, vec],
        out_shape=[_sds((LP, D), F32), _sds((LP, D), BF16), _sds((1, D), F32), _sds((1, D), F32)],
        compiler_params=_params(("arbitrary",)),
    )(h2, wf, target)


def _dot(a, b, dims):
    return lax.dot_general(a, b, (dims, ((), ())), preferred_element_type=F32)


NN, TN, NT = ((1,), (0,)), ((0,), (0,)), ((1,), (1,))


def _tri_sum(t, x):
    hi = x.astype(BF16)
    r1 = x - hi.astype(F32)
    mid = r1.astype(BF16)
    lo = (r1 - mid.astype(F32)).astype(BF16)
    return _dot(t, hi, NN) + _dot(t, mid, NN) + _dot(t, lo, NN)


def _gla_gates(glr_ref, gw2_ref, gb_ref, rows, row0):
    g_raw = _dot(glr_ref[rows, :].astype(BF16), gw2_ref[...], NN) + gb_ref[...]
    logsig = jnp.minimum(g_raw, 0.0) - jnp.log(1.0 + jnp.exp(-jnp.abs(g_raw)))
    rid = row0 + lax.broadcasted_iota(jnp.int32, g_raw.shape, 0)
    live = rid >= PAD
    return g_raw, jnp.where(live, logsig / TAU, 0.0), live


def _tri_masks():
    r = lax.broadcasted_iota(jnp.int32, (CHUNK, CHUNK), 0)
    c = lax.broadcasted_iota(jnp.int32, (CHUNK, CHUNK), 1)
    return r >= c


def _gla_specs(rev):
    n = NCH // CPS
    R = CPS * CHUNK
    st = (lambda s: n - 1 - s) if rev else (lambda s: s)
    return R, n, st, [
        pl.BlockSpec((R, KW), lambda s: (st(s), 0)),
        pl.BlockSpec((R, KW), lambda s: (st(s), 1)),
        pl.BlockSpec((R, GW), lambda s: (st(s), 1)),
        pl.BlockSpec((R, GW), lambda s: (st(s), 2)),
        pl.BlockSpec((R, 128), lambda s: (st(s), GLR_BLK)),
    ]


def _gla_fwd(proj, gw2p, gate_b, gnw, deps=()):
    R, n, st, pspecs = _gla_specs(False)

    def body(q_ref, k_ref, v_ref, r_ref, glr_ref, gw2_ref, gb_ref, gnw_ref, og_ref, o_ref, st_ref, state):
        s = pl.program_id(0)

        @pl.when(s == 0)
        def _():
            state[...] = jnp.zeros_like(state)

        causal = _tri_masks()
        tri = causal.astype(BF16)
        for c in range(CPS):
            rows = slice(c * CHUNK, (c + 1) * CHUNK)
            _, logg, _ = _gla_gates(glr_ref, gw2_ref, gb_ref, rows, s * R + c * CHUNK)
            G = _tri_sum(tri, logg)
            g_last = G[CHUNK - 1:CHUNK, :]
            q_dec = (q_ref[rows, :] * (DK ** -0.5) * jnp.exp(G)).astype(BF16)
            kk = k_ref[rows, :]
            k_inv = (kk * jnp.exp(-G)).astype(BF16)
            k_end = (kk * jnp.exp(g_last - G)).astype(BF16)
            decay = jnp.exp(g_last)
            for h in range(H):
                lk = slice(h * DK, (h + 1) * DK)
                lv = slice(h * DV, (h + 1) * DV)
                v = v_ref[rows, lv].astype(BF16)
                S = state[h]
                st_ref[c, h] = S
                A = jnp.where(causal, _dot(q_dec[:, lk], k_inv[:, lk], NT), 0.0).astype(BF16)
                o = _dot(A, v, NN) + _dot(q_dec[:, lk], S.astype(BF16), NT)
                state[h] = decay[:, lk] * S + _dot(v, k_end[:, lk], TN)
                o_ref[rows, lv] = o
                on = o * lax.rsqrt(jnp.mean(o * o, axis=-1, keepdims=True) + EPS) * gnw_ref[...]
                rr = r_ref[rows, lv]
                og_ref[rows, lv] = (on * (rr * jax.nn.sigmoid(rr))).astype(BF16)

    full = lambda shape: pl.BlockSpec(shape, lambda s: (0,) * len(shape))
    body, dep_ins, dep_specs = _after(body, 8, deps)
    return pl.pallas_call(
        body, name="gla_fwd", grid=(n,),
        in_specs=pspecs + [full((128, KW)), full((1, KW)), full((1, DV))] + dep_specs,
        out_specs=[pl.BlockSpec((R, GW), lambda s: (s, 0)), pl.BlockSpec((R, GW), lambda s: (s, 0)),
                   pl.BlockSpec((CPS, H, DV, DK), lambda s: (s, 0, 0, 0))],
        out_shape=[_sds((LP, GW + PW), BF16), _sds((LP, GW), F32), _sds((NCH, H, DV, DK), F32)],
        scratch_shapes=[pltpu.VMEM((H, DV, DK), F32)],
        compiler_params=_params(("arbitrary",)),
    )(proj, proj, proj, proj, proj, gw2p, gate_b, gnw, *dep_ins)


def _gla_bwd(proj, dmixed, o_saved, st_saved, gw2p, gate_b, gnw, deps=()):
    R, n, st, pspecs = _gla_specs(True)

    def body(q_ref, k_ref, v_ref, r_ref, glr_ref, dog_ref, o_ref, st_ref, gw2_ref, gb_ref, gnw_ref,
             dqkvr_ref, dglr_ref, ggn_ref, ggb_ref, ggw_ref, gstate):
        s = pl.program_id(0)

        @pl.when(s == 0)
        def _():
            gstate[...] = jnp.zeros_like(gstate)
            ggn_ref[...] = jnp.zeros_like(ggn_ref)
            ggb_ref[...] = jnp.zeros_like(ggb_ref)
            ggw_ref[...] = jnp.zeros_like(ggw_ref)

        causal = _tri_masks()
        tri = causal.astype(BF16)
        tri_up = (lax.broadcasted_iota(jnp.int32, (CHUNK, CHUNK), 0)
                  <= lax.broadcasted_iota(jnp.int32, (CHUNK, CHUNK), 1)).astype(BF16)
        gnw = gnw_ref[...]
        for c in reversed(range(CPS)):
            rows = slice(c * CHUNK, (c + 1) * CHUNK)
            g_raw, logg, live = _gla_gates(glr_ref, gw2_ref, gb_ref, rows, (n - 1 - s) * R + c * CHUNK)
            G = _tri_sum(tri, logg)
            g_last = G[CHUNK - 1:CHUNK, :]
            e_g, e_gi, e_end = jnp.exp(G), jnp.exp(-G), jnp.exp(g_last - G)
            q_dec = q_ref[rows, :] * (DK ** -0.5) * e_g
            kk = k_ref[rows, :]
            k_inv, k_end = kk * e_gi, kk * e_end
            q_dec_b, k_inv_b, k_end_b = q_dec.astype(BF16), k_inv.astype(BF16), k_end.astype(BF16)
            decay = jnp.exp(g_last)
            d_g, d_gl = [], []
            for h in range(H):
                lk = slice(h * DK, (h + 1) * DK)
                lv = slice(h * DV, (h + 1) * DV)
                o = o_ref[rows, lv]
                rr = r_ref[rows, lv]
                dog = dog_ref[rows, lv]
                rstd = lax.rsqrt(jnp.mean(o * o, axis=-1, keepdims=True) + EPS)
                ohat = o * rstd
                sr = jax.nn.sigmoid(rr)
                don = dog * (rr * sr)
                dqkvr_ref[rows, 2 * KW + GW + h * DV:2 * KW + GW + (h + 1) * DV] = (
                    dog * (ohat * gnw) * (sr * (1.0 + rr * (1.0 - sr)))).astype(BF16)
                ggn_ref[...] += jnp.sum(don * ohat, axis=0, keepdims=True)
                dohat = don * gnw
                do = (rstd * (dohat - ohat * jnp.mean(dohat * ohat, axis=-1, keepdims=True))).astype(BF16)
                v = v_ref[rows, lv].astype(BF16)
                S = st_ref[c, h]
                gS = gstate[h]
                S_b, gS_b = S.astype(BF16), gS.astype(BF16)
                qd, ki, ke = q_dec_b[:, lk], k_inv_b[:, lk], k_end_b[:, lk]
                A = jnp.where(causal, _dot(qd, ki, NT), 0.0).astype(BF16)
                dA = jnp.where(causal, _dot(do, v, NT), 0.0).astype(BF16)
                dv = _dot(A, do, TN) + _dot(ke, gS_b, NT)
                dq_dec = _dot(dA, ki, NN) + _dot(do, S_b, NN)
                dk_inv = _dot(dA, qd, TN)
                dk_end = _dot(v, gS_b, NN)
                d_decay = jnp.sum(gS * S, axis=0, keepdims=True)
                gstate[h] = decay[:, lk] * gS + _dot(do, qd, TN)
                dqkvr_ref[rows, lk] = (dq_dec * e_g[:, lk] * (DK ** -0.5)).astype(BF16)
                dqkvr_ref[rows, KW + h * DK:KW + (h + 1) * DK] = (
                    dk_inv * e_gi[:, lk] + dk_end * e_end[:, lk]).astype(BF16)
                dqkvr_ref[rows, 2 * KW + h * DV:2 * KW + (h + 1) * DV] = dv.astype(BF16)
                ke_prod = dk_end * k_end[:, lk]
                d_g.append(dq_dec * q_dec[:, lk] - dk_inv * k_inv[:, lk] - ke_prod)
                d_gl.append(jnp.sum(ke_prod, axis=0, keepdims=True) + d_decay * decay[:, lk])
            dlogg = _tri_sum(tri_up, jnp.concatenate(d_g, axis=1)) + jnp.concatenate(d_gl, axis=1)
            dg_raw = jnp.where(live, dlogg * (1.0 / TAU) * jax.nn.sigmoid(-g_raw), 0.0)
            ggb_ref[...] += jnp.sum(dg_raw, axis=0, keepdims=True)
            dg_b = dg_raw.astype(BF16)
            ggw_ref[...] += _dot(glr_ref[rows, :].astype(BF16), dg_b, TN)
            dglr_ref[rows, :] = _dot(dg_b, gw2_ref[...], NT).astype(BF16)

    full = lambda shape: pl.BlockSpec(shape, lambda s: (0,) * len(shape))
    body, dep_ins, dep_specs = _after(body, 11, deps)
    return pl.pallas_call(
        body, name="gla_bwd", grid=(n,),
        in_specs=pspecs + [pl.BlockSpec((R, GW), lambda s: (st(s), 0)), pl.BlockSpec((R, GW), lambda s: (st(s), 0)),
                           pl.BlockSpec((CPS, H, DV, DK), lambda s: (st(s), 0, 0, 0)),
                           full((128, KW)), full((1, KW)), full((1, DV))] + dep_specs,
        out_specs=[pl.BlockSpec((R, 2 * KW + 2 * GW), lambda s: (st(s), 0)), pl.BlockSpec((R, 128), lambda s: (st(s), 0)),
                   full((1, DV)), full((1, KW)), full((128, KW))],
        out_shape=[_sds((LP, D_INP), BF16), _sds((LP, 128), BF16),
                   _sds((1, DV), F32), _sds((1, KW), F32), _sds((128, KW), F32)],
        scratch_shapes=[pltpu.VMEM((H, DV, DK), F32)],
        compiler_params=_params(("arbitrary",)),
    )(proj, proj, proj, proj, proj, dmixed, o_saved, st_saved, gw2p, gate_b, gnw, *dep_ins)


def _pool_pre(x, win, rid):
    s, step = x, 1
    while step < win:
        s = s + pltpu.roll(s, step, 0)
        step *= 2
    cnt = jnp.clip(rid - (PAD - 1), 1, win).astype(F32)
    live = rid >= PAD
    return jnp.where(live, s / cnt - x, 0.0), cnt, live


def _pool_fwd(proj, pool_w, pool_scale, mixed):
    def body(pu_ref, w_ref, sc_ref, _, o_ref):
        rid = lax.broadcasted_iota(jnp.int32, (LP, GC), 0)
        for g, win in enumerate(WINDOWS):
            @pl.when(pl.program_id(0) == g)
            def _():
                y, _, _ = _pool_pre(pu_ref[...], win, rid)
                o_ref[...] = (_dot(y.astype(BF16), w_ref[...], NN) * sc_ref[...]).astype(BF16)

    col = lambda base: pl.BlockSpec((LP, GC), lambda g: (0, base + g))
    return pl.pallas_call(
        body, name="pool_fwd", grid=(len(WINDOWS),),
        in_specs=[col(POOL_BLK), pl.BlockSpec((None, GC, GC), lambda g: (g, 0, 0)),
                  pl.BlockSpec((1, GC), lambda g: (0, g)), ANY_SPEC],
        out_specs=col(GW // GC), out_shape=_sds(mixed.shape, BF16), input_output_aliases={3: 0},
        compiler_params=_params(("parallel",)),
    )(proj, pool_w, pool_scale, mixed)


def _pool_bwd(proj, dmixed, pool_w, pool_scale, dproj):
    def body(pu_ref, do_ref, w_ref, sc_ref, _, dpu_ref, dw_ref, dsc_ref):
        rid = lax.broadcasted_iota(jnp.int32, (LP, GC), 0)
        for g, win in enumerate(WINDOWS):
            @pl.when(pl.program_id(0) == g)
            def _():
                y, cnt, live = _pool_pre(pu_ref[...], win, rid)
                y_b = y.astype(BF16)
                w = w_ref[...]
                do = do_ref[...]
                dsc_ref[...] = jnp.sum(do * _dot(y_b, w, NN), axis=0, keepdims=True)
                dyw = (do * sc_ref[...]).astype(BF16)
                dw_ref[...] = _dot(y_b, dyw, TN)
                dy = jnp.where(live, _dot(dyw, w, NT), 0.0)
                s, step = dy / cnt, 1
                while step < win:
                    s = s + pltpu.roll(s, LP - step, 0)
                    step *= 2
                dpu_ref[...] = (s - dy).astype(BF16)

    col = lambda base: pl.BlockSpec((LP, GC), lambda g: (0, base + g))
    mat = pl.BlockSpec((None, GC, GC), lambda g: (g, 0, 0))
    vec = pl.BlockSpec((1, GC), lambda g: (0, g))
    return pl.pallas_call(
        body, name="pool_bwd", grid=(len(WINDOWS),),
        in_specs=[col(POOL_BLK), col(GW // GC), mat, vec, ANY_SPEC], out_specs=[col(POOL_BLK), mat, vec],
        out_shape=[_sds(dproj.shape, BF16), _sds((4, GC, GC), F32), _sds((1, PW), F32)],
        input_output_aliases={4: 0}, compiler_params=_params(("parallel",)),
    )(proj, dmixed, pool_w, pool_scale, dproj)


def _adamw_math(w, g, m, v):
    m = B1 * m + (1.0 - B1) * g
    v = B2 * v + (1.0 - B2) * (g * g)
    m_hat = m / (1.0 - B1 ** STEP)
    v_hat = v / (1.0 - B2 ** STEP)
    return -LR * (m_hat / (jnp.sqrt(v_hat) + AEPS) + WD * w), m, v


def _adamw_landed(sums, landed, my_chip, w, m, v, rows, name, cols=None):
    _, r, c = w.shape

    def body(chip_ref, s_ref, l_ref, w_ref, m_ref, v_ref, g_ref, d_ref, mo_ref, vo_ref):
        g = s_ref[...].astype(F32)
        for k in range(3):
            g = g + l_ref[k].astype(F32)
        g_ref[...] = g
        d_ref[...], mo_ref[...], vo_ref[...] = _adamw_math(w_ref[...], g, m_ref[...], v_ref[...])

    cols = cols or c
    blk = pl.BlockSpec((None, rows, cols), lambda i, j, chip_ref: (0, i, j))
    return pl.pallas_call(
        body, name=name, out_shape=[_sds((1, r, c), F32)] * 4,
        grid_spec=pltpu.PrefetchScalarGridSpec(
            num_scalar_prefetch=1, grid=(r // rows, c // cols),
            in_specs=[pl.BlockSpec((None, rows, cols), lambda i, j, chip_ref: (chip_ref[0], i, j)),
                      pl.BlockSpec((3, rows, cols), lambda i, j, chip_ref: (0, i, j)), blk, blk, blk],
            out_specs=[blk] * 4),
        compiler_params=_params(("parallel", "parallel")),
    )(my_chip, sums, landed, w, m, v)


def _adamw_small(g, w, m, v):
    def body(g_ref, w_ref, m_ref, v_ref, d_ref, mo_ref, vo_ref):
        d_ref[...], mo_ref[...], vo_ref[...] = _adamw_math(w_ref[...], g_ref[...], m_ref[...], v_ref[...])

    return pl.pallas_call(body, name="adamw_small", out_shape=[_sds(w.shape, F32)] * 3)(g, w, m, v)


SMALL_REPL = (("norm1_w", D), ("norm2_w", D), ("final_norm_w", D), ("pool_scale", PW), ("gate_b", KW),
              ("gla_norm_w", DV))


def _pack_rows(vecs, rows):
    flat = jnp.concatenate([jnp.ravel(v) for v in vecs])
    return jnp.pad(flat, (0, rows * 1024 - flat.shape[0])).reshape(rows, 1024)


def kernel(x, meta_tokens, norm1_w, w_in, gate_w2, gate_b, gla_norm_w, pool_w, pool_scale, w_out, norm2_w, mlp_w1, mlp_w2, final_norm_w, loss_target, m_meta_tokens, m_norm1_w, m_w_in, m_gate_w2, m_gate_b, m_gla_norm_w, m_pool_w, m_pool_scale, m_w_out, m_norm2_w, m_mlp_w1, m_mlp_w2, m_final_norm_w, v_meta_tokens, v_norm1_w, v_w_in, v_gate_w2, v_gate_b, v_gla_norm_w, v_pool_w, v_pool_scale, v_w_out, v_norm2_w, v_mlp_w1, v_mlp_w2, v_final_norm_w):
    me = 4 * lax.axis_index("x") + 2 * lax.axis_index("y") + lax.axis_index("c")
    W = dict(meta_tokens=meta_tokens, norm1_w=norm1_w, w_in=w_in, gate_w2=gate_w2, gate_b=gate_b,
             gla_norm_w=gla_norm_w, pool_w=pool_w, pool_scale=pool_scale, w_out=w_out, norm2_w=norm2_w,
             mlp_w1=mlp_w1, mlp_w2=mlp_w2, final_norm_w=final_norm_w)
    Mo = dict(meta_tokens=m_meta_tokens, norm1_w=m_norm1_w, w_in=m_w_in, gate_w2=m_gate_w2, gate_b=m_gate_b,
              gla_norm_w=m_gla_norm_w, pool_w=m_pool_w, pool_scale=m_pool_scale, w_out=m_w_out, norm2_w=m_norm2_w,
              mlp_w1=m_mlp_w1, mlp_w2=m_mlp_w2, final_norm_w=m_final_norm_w)
    Vo = dict(meta_tokens=v_meta_tokens, norm1_w=v_norm1_w, w_in=v_w_in, gate_w2=v_gate_w2, gate_b=v_gate_b,
              gla_norm_w=v_gla_norm_w, pool_w=v_pool_w, pool_scale=v_pool_scale, w_out=v_w_out, norm2_w=v_norm2_w,
              mlp_w1=v_mlp_w1, mlp_w2=v_mlp_w2, final_norm_w=v_final_norm_w)

    small = _exchange_small(_pack_rows([meta_tokens, gate_w2[0]], 8), False, "gather_small")
    ex = _Exchange(dict(w_in=w_in[0].T, w_out=w_out[0], pool_w=pool_w[0].reshape(4 * 32, GC), mlp_w1=mlp_w1[0],
                        mlp_w2=mlp_w2[0]), small)
    meta_full = small[:, 0:4].reshape(NDEV, N_META, D // NDEV).transpose(1, 0, 2).reshape(N_META, D)
    gw2_full = small[:, 4].reshape(NDEV, RANK, KW // NDEV).transpose(1, 0, 2).reshape(RANK, KW)
    gw2p = jnp.pad(gw2_full, ((0, 128 - RANK), (0, 0))).astype(BF16)

    step = _layer_step(x[0], loss_target[0], meta_full, gw2p, ex, norm1_w, gate_b, gla_norm_w, pool_scale, norm2_w,
                       final_norm_w.reshape(1, D))
    grad_x = step["dh0"][ROW_X:][None]

    last = step["dh0"]
    out = {}
    for group in ("down", "up", "mix"):
        for k, (sums, landed) in ex.grad_finish(group, last).items():
            out[k] = _adamw_landed(sums, landed, ex.my_chip, W[k], Mo[k], Vo[k], SHARD_ROWS[k], "adamw_" + k)
            last = out[k][1]

    loss_part = 0.5 * jnp.sum(step["sq"]) / D
    packed = jnp.concatenate([
        _pack_rows([step[k] for k, _ in SMALL_REPL] + [loss_part], 8),
        step["gate_w2"][:RANK].reshape(8, 1024), step["dh0"][PAD:ROW_X].reshape(32, 1024)], axis=0)
    red = _exchange_small(packed, True, "reduce_small", deps=[last])
    loss = red[7, 768]
    g_gw2_mine = lax.dynamic_slice(red[8:16].reshape(RANK, KW), (0, me * (KW // NDEV)), (RANK, KW // NDEV))
    g_meta_mine = lax.dynamic_slice(red[16:48].reshape(N_META, D), (0, me * (D // NDEV)), (N_META, D // NDEV))

    done = ex.grad_finish("w_in", red)
    poolw3 = lambda a: a.reshape(1, 4 * 32, GC)
    res = _adamw_landed(*done["pool_w"], ex.my_chip, poolw3(pool_w), poolw3(m_pool_w), poolw3(v_pool_w),
                        SHARD_ROWS["pool_w"], "adamw_pool_w")
    out["pool_w"] = [a.reshape(pool_w.shape) for a in res]
    tr = lambda a: a[0].T[None]
    res = _adamw_landed(*done["w_in"], ex.my_chip, tr(w_in), tr(m_w_in), tr(v_w_in), D_IN // NDEV, "adamw_w_in",
                        cols=256)
    out["w_in"] = [a[0].T[None] for a in res]

    def small_pack(P):
        return jnp.concatenate([_pack_rows([P[k] for k, _ in SMALL_REPL], 8),
                                _pack_rows([P["meta_tokens"], P["gate_w2"]], 8)], axis=0)

    g_small = jnp.concatenate([red[0:8], _pack_rows([g_meta_mine, g_gw2_mine], 8)], axis=0)
    g_small = g_small.at[7, 768].set(0.0)
    res_small = _adamw_small(g_small, small_pack(W), small_pack(Mo), small_pack(Vo))
    res_small = [g_small] + list(res_small)
    off = 0
    for k, nel in SMALL_REPL:
        out[k] = [a[0:8].reshape(-1)[off:off + nel].reshape(W[k].shape) for a in res_small]
        off += nel
    out["meta_tokens"] = [a[8:12].reshape(N_META, D // NDEV) for a in res_small]
    out["gate_w2"] = [a[12].reshape(1, RANK, KW // NDEV) for a in res_small]

    order = ["meta_tokens", "norm1_w", "w_in", "gate_w2", "gate_b", "gla_norm_w", "pool_w", "pool_scale", "w_out",
             "norm2_w", "mlp_w1", "mlp_w2", "final_norm_w"]
    return (loss, grad_x, *[out[k][0] for k in order], *[out[k][1] for k in order],
            *[out[k][2] for k in order], *[out[k][3] for k in order])


SHARD_ROWS = dict(w_in=256, w_out=64, mlp_w1=128, mlp_w2=64, pool_w=128)
C_GLR = 2 * KW + 2 * GW
GATHER_GROUPS = dict(w_in=("w_in",), mix=("w_out", "pool_w"), up=("mlp_w1",), down=("mlp_w2",))
GRAD_GROUPS = dict(down=("mlp_w2",), up=("mlp_w1",), mix=("w_out",), w_in=("pool_w", "w_in"))


class _Exchange:
    def __init__(self, shards, after):
        names = list(shards)
        first, token = _gather_start([shards[names[0]].astype(BF16)], "gather_start_" + names[0], after)
        rest, _ = _gather_start([shards[k].astype(BF16) for k in names[1:]], "gather_start_rest", token)
        self.state = dict(zip(names, first + rest))
        self.my_c = lax.axis_index("c").astype(jnp.int32).reshape(1)
        self.my_chip = (2 * lax.axis_index("x") + lax.axis_index("y")).astype(jnp.int32).reshape(1)
        self.sibling, self.chips = {}, {}

    def forward(self, group, after):
        ks = GATHER_GROUPS[group]
        fwd, token = _gather_forward([self.state[k] for k in ks], after, "gather_forward_" + group)
        self.state.update(zip(ks, fwd))
        return token

    def weights(self, group, after):
        ks = GATHER_GROUPS[group]
        g = dict(zip(ks, _gather_finish([self.state[k] for k in ks], after, "gather_finish_" + group)))
        if group == "w_in":
            nat = g["w_in"].reshape(D_IN, D)
            return jnp.concatenate([nat[:C_GLR], nat[C_GLR + RANK:], nat[C_GLR:C_GLR + RANK],
                                    jnp.zeros((D_INP - D_IN, D), BF16)], axis=0)
        if group == "mix":
            return (g["w_out"].reshape(D, D),
                    g["pool_w"].reshape(NDEV, 4, 32, GC).transpose(1, 0, 2, 3).reshape(4, GC, GC))
        return g["mlp_w1"] if group == "up" else g["mlp_w2"].reshape(DFF, D)

    def grad(self, group, grads):
        parts = dict(grads)
        if group == "w_in":
            g = parts["w_in"]
            nat = jnp.concatenate([g[:C_GLR], g[C_GLR + PW:C_GLR + PW + RANK], g[C_GLR:C_GLR + PW]], axis=0)
            parts["w_in"] = nat.reshape(4, 2, D_IN // NDEV, D).transpose(1, 0, 2, 3)
            parts["pool_w"] = (parts["pool_w"].astype(BF16).reshape(4, 4, 2, 32, GC).transpose(2, 1, 0, 3, 4)
                               .reshape(2, 4, 4 * 32, GC))
        ks = GRAD_GROUPS[group]
        started, token = _to_sibling_start([parts[k] for k in ks], "grad_sibling_start_" + group)
        self.sibling[group] = started
        return token

    def grad_mid(self, group, after):
        ks = GRAD_GROUPS[group]
        both = _to_sibling_finish(self.sibling[group], after, "grad_sibling_finish_" + group)
        tile = lambda k, p: (p.shape[2], 512) if k == "w_in" else (SHARD_ROWS[k], p.shape[3])
        sums = [_chip_sum(p, s, self.my_c, tile(k, p), "chip_sum_" + k) for k, (p, s) in zip(ks, both)]
        self.chips[group], token = _to_chips_start(sums, "grad_chips_start_" + group)
        return token

    def grad_finish(self, group, after):
        done = _to_chips_finish(self.chips[group], after, "grad_chips_finish_" + group)
        return dict(zip(GRAD_GROUPS[group], done))


def _layer_step(x, target, meta_full, gw2p, ex, norm1_w, gate_b, gla_norm_w, pool_scale, norm2_w, final_norm_w):
    h0 = jnp.concatenate([jnp.zeros((PAD, D), F32), meta_full, x], axis=0)
    u1 = _rmsnorm_fwd(h0, norm1_w, "rmsnorm1")
    win_p = ex.weights("w_in", ex.forward("w_in", u1))
    proj = _matmul(u1, win_p, mode="nt", tm=1056, tn=1408, tk=2048, name="proj")
    tok = ex.forward("mix", proj)
    mixed, o_saved, st_saved = _gla_fwd(proj, gw2p, gate_b, gla_norm_w, deps=[tok])
    wout_f, poolw_f = ex.weights("mix", mixed)
    mixed = _pool_fwd(proj, poolw_f, pool_scale, mixed)
    h1 = _matmul(mixed, wout_f, mode="nn", tm=1056, tn=1024, tk=2048, name="mix_out", epi="add", extra=h0)
    tok = ex.forward("up", h1)
    u2 = _rmsnorm_fwd(h1, norm2_w, "rmsnorm2", deps=[tok])
    w1_g = ex.weights("up", u2)
    z, act = _matmul(u2, w1_g, mode="nn", tm=1056, tn=1024, tk=2048, name="mlp_up", epi="relu2", b_slots=True)
    w2_f = ex.weights("down", ex.forward("down", act))
    h2 = _matmul(act, w2_f, mode="nn", tm=1056, tn=1024, tk=2048, name="mlp_down", epi="add", extra=h1)
    dh2, dh2b, sq, g_fnw = _loss_head(h2, final_norm_w, target)

    g_w2 = _matmul(act, dh2b, mode="tn", tm=512, tn=1024, tk=LP, name="d_mlp_w2", out_dtype=BF16, out_slots="rows")
    tok = ex.grad("down", dict(mlp_w2=g_w2))
    dz = _matmul(dh2b, w2_f, mode="nt", tm=1056, tn=1024, tk=2048, name="d_act", out_dtype=BF16, epi="dz", extra=z,
                 deps=[tok])
    tok = ex.grad_mid("down", dz)
    g_w1 = _matmul(u2, dz, mode="tn", tm=512, tn=1024, tk=LP, name="d_mlp_w1", out_dtype=BF16, out_slots="cols",
                   deps=[tok])
    tok = ex.grad("up", dict(mlp_w1=g_w1))
    du2 = _matmul(dz, w1_g, mode="nt", tm=1056, tn=1024, tk=1024, name="d_u2", b_slots=True, deps=[tok])
    tok = ex.grad_mid("up", du2)
    dh1, dh1b, g_n2 = _rmsnorm_bwd(h1, norm2_w, du2, dh2, "rmsnorm2_bwd", deps=[tok])
    g_wout = _matmul(mixed, dh1b, mode="tn", tm=256, tn=1024, tk=LP, name="d_w_out", out_dtype=BF16, out_slots="rows")
    tok = ex.grad("mix", dict(w_out=g_wout))
    dmixed = _matmul(dh1b, wout_f, mode="nt", tm=1056, tn=1024, tk=2048, name="d_mixed", deps=[tok])
    tok = ex.grad_mid("mix", dmixed)
    dproj, dglr, g_gnw, g_gb, g_gw2 = _gla_bwd(proj, dmixed, o_saved, st_saved, gw2p, gate_b, gla_norm_w, deps=[tok])
    dproj, g_poolw, g_psc = _pool_bwd(proj, dmixed, poolw_f, pool_scale, dproj)
    dproj = lax.dynamic_update_slice(dproj, dglr, (0, GLR_BLK * 128))
    g_win_p = _matmul(dproj, u1, mode="tn", tm=1408, tn=1024, tk=LP, name="d_w_in", out_dtype=BF16)
    tok = ex.grad("w_in", dict(pool_w=g_poolw, w_in=g_win_p))
    du1 = _matmul(dproj, win_p, mode="nn", tm=1056, tn=1024, tk=1408, name="d_u1", deps=[tok])
    tok = ex.grad_mid("w_in", du1)
    dh0, _, g_n1 = _rmsnorm_bwd(h0, norm1_w, du1, dh1, "rmsnorm1_bwd", deps=[tok])
    return dict(dh0=dh0, sq=sq, gate_w2=g_gw2, norm1_w=g_n1, norm2_w=g_n2, final_norm_w=g_fnw, pool_scale=g_psc,
                gate_b=g_gb, gla_norm_w=g_gnw)
```

```python
import functools

import jax
import jax.numpy as jnp
from jax import lax
from jax.experimental import pallas as pl
from jax.experimental.pallas import tpu as pltpu

F32, BF16 = jnp.float32, jnp.bfloat16
MESH = pl.DeviceIdType.MESH

NDEV = 8
D = 2048
SEQ = 2048
N_META = 16
CHUNK = 64
PAD = (-N_META) % CHUNK
ROW_X = PAD + N_META
LP = ROW_X + SEQ
NCH = LP // CHUNK
H = 4
DK = 128
DV = 256
KW = H * DK
GW = H * DV
PW = 1024
RANK = 16
TAU = 16.0
WINDOWS = (2, 4, 8, 16)
GC = 256
DFF = 4 * D
EPS = 1e-6
D_IN = 2 * KW + 2 * GW + RANK + PW
D_INP = 4224
GLR_BLK = (2 * KW + 2 * GW + PW) // 128
POOL_BLK = (2 * KW + 2 * GW) // GC
LR, B1, B2, AEPS, WD, STEP = 0.001, 0.9, 0.999, 1e-08, 0.01, 10
VMEM_LIMIT = 48 * 1024 * 1024
CPS = 3


def _params(sem=None):
    return pltpu.CompilerParams(dimension_semantics=sem, vmem_limit_bytes=VMEM_LIMIT)


def _sds(shape, dtype):
    return jax.ShapeDtypeStruct(shape, dtype)


def _me():
    return lax.axis_index("x"), lax.axis_index("y"), lax.axis_index("c")


def _peer(j):
    x, y, c = _me()
    return (x ^ ((j >> 2) & 1), y ^ ((j >> 1) & 1), c ^ (j & 1))


def _slot(dev):
    return 4 * dev[0] + 2 * dev[1] + dev[2]


HBM_SPEC = pl.BlockSpec(memory_space=pltpu.HBM)
SEM_SPEC = pl.BlockSpec(memory_space=pltpu.SEMAPHORE)
ANY_SPEC = pl.BlockSpec(memory_space=pl.ANY)
EFFECT = pltpu.SideEffectType.DATAFLOW_SIDE_EFFECTING
SIBLING = 1
OTHER_CHIPS = (2, 4, 6)


def _in_hbm(a):
    return pltpu.with_memory_space_constraint(a, pltpu.HBM)


def _chip(dev):
    return 2 * dev[0] + dev[1]


def _rcopy(src, dst, send_sem, recv_sem, to):
    return pltpu.make_async_remote_copy(src_ref=src, dst_ref=dst, send_sem=send_sem, recv_sem=recv_sem,
                                        device_id=to, device_id_type=MESH)


def _split_call(body, name, ins, in_specs, out_shape, out_specs, aliases, scratch=()):
    n = len(ins) + len(out_shape)

    def with_token(*refs):
        body(*refs[:n], *refs[n + 1:])
        refs[n][...] = jnp.zeros_like(refs[n])

    return pl.pallas_call(
        with_token, name=name, in_specs=in_specs, out_shape=list(out_shape) + [_sds((8, 128), F32)],
        out_specs=list(out_specs) + [pl.BlockSpec(memory_space=pltpu.VMEM)],
        input_output_aliases=aliases, scratch_shapes=list(scratch),
        compiler_params=pltpu.CompilerParams(has_side_effects=EFFECT),
    )(*ins)


def _after(body, n_in, deps):
    deps = [d for d in deps if d is not None]
    if not deps:
        return body, [], []
    return (lambda *refs: body(*refs[:n_in], *refs[n_in + len(deps):])), deps, [ANY_SPEC] * len(deps)


def _gather_start(shards, name, after):
    n = len(shards)
    me = _slot(_me())
    lands = [lax.dynamic_update_slice(lax.empty((NDEV,) + s.shape, s.dtype), s[None], (me, 0, 0)) for s in shards]

    def body(*refs):
        src, land = refs[:n], refs[n:2 * n]
        outs = refs[2 * n + 1:]
        for i in range(n):
            send_sems, recv_sems = outs[4 * i], outs[4 * i + 1]
            for k, rel in enumerate((SIBLING,) + OTHER_CHIPS):
                _rcopy(src[i], land[i].at[_slot(_me())], send_sems.at[k], recv_sems.at[k], _peer(rel)).start()

    out_shape, out_specs, aliases = [], [], {}
    for i, s in enumerate(shards):
        out_shape += [pltpu.SemaphoreType.DMA((4,)), pltpu.SemaphoreType.DMA((4,)), pltpu.HBM(s.shape, s.dtype),
                      pltpu.HBM((NDEV,) + s.shape, s.dtype)]
        out_specs += [SEM_SPEC, SEM_SPEC, HBM_SPEC, HBM_SPEC]
        aliases[i] = 4 * i + 2
        aliases[n + i] = 4 * i + 3
    res = _split_call(body, name, [_in_hbm(s) for s in shards] + [_in_hbm(l) for l in lands] + [after],
                      [HBM_SPEC] * (2 * n) + [ANY_SPEC], out_shape, out_specs, aliases)
    return [tuple(res[4 * i:4 * i + 4]) for i in range(n)], res[-1]


def _gather_forward(started, after, name):
    n = len(started)

    def body(*refs):
        land, recv1 = refs[:n], refs[n:2 * n]
        outs = refs[2 * n + 1:]
        for i in range(n):
            send2, recv2 = outs[3 * i + 1], outs[3 * i + 2]
            for k, rel in enumerate(OTHER_CHIPS):
                blk = land[i].at[_slot(_peer(rel))]
                _rcopy(blk, blk, send2.at[k], recv1[i].at[1 + k], _peer(rel)).wait_recv()
                _rcopy(blk, blk, send2.at[k], recv2.at[k], _peer(SIBLING)).start()

    ins = [_in_hbm(st[3]) for st in started] + [st[1] for st in started] + [after]
    out_shape, out_specs, aliases = [], [], {}
    for i, st in enumerate(started):
        out_shape += [pltpu.HBM(st[3].shape, st[3].dtype), pltpu.SemaphoreType.DMA((3,)), pltpu.SemaphoreType.DMA((3,))]
        out_specs += [HBM_SPEC, SEM_SPEC, SEM_SPEC]
        aliases[i] = 3 * i
    res = _split_call(body, name, ins, [HBM_SPEC] * n + [SEM_SPEC] * n + [ANY_SPEC], out_shape, out_specs, aliases)
    return [(st[0], st[1], st[2], res[3 * i], res[3 * i + 1], res[3 * i + 2]) for i, st in enumerate(started)], res[-1]


def _gather_finish(forwarded, after, name):
    n = len(forwarded)

    def body(*refs):
        for i in range(n):
            send1, recv1, src, land, send2, recv2 = refs[6 * i:6 * i + 6]
            me = _slot(_me())
            sib = _slot(_peer(SIBLING))
            for k, rel in enumerate((SIBLING,) + OTHER_CHIPS):
                _rcopy(src, land.at[me], send1.at[k], recv1.at[k], _peer(rel)).wait_send()
            _rcopy(src, land.at[sib], send1.at[0], recv1.at[0], _peer(SIBLING)).wait_recv()
            for k, rel in enumerate(OTHER_CHIPS):
                mine, theirs = land.at[_slot(_peer(rel))], land.at[_slot(_peer(rel ^ SIBLING))]
                _rcopy(mine, mine, send2.at[k], recv2.at[k], _peer(SIBLING)).wait_send()
                _rcopy(theirs, theirs, send2.at[k], recv2.at[k], _peer(SIBLING)).wait_recv()

    ins, in_specs, out_shape, aliases = [], [], [], {}
    for i, f in enumerate(forwarded):
        ins += [f[0], f[1], _in_hbm(f[2]), _in_hbm(f[3]), f[4], f[5]]
        in_specs += [SEM_SPEC, SEM_SPEC, HBM_SPEC, HBM_SPEC, SEM_SPEC, SEM_SPEC]
        out_shape.append(pltpu.HBM(f[3].shape, f[3].dtype))
        aliases[6 * i + 3] = i
    res = _split_call(body, name, ins + [after], in_specs + [ANY_SPEC], out_shape, [HBM_SPEC] * n, aliases)
    return list(res[:-1])


def _to_sibling_start(parts, name):
    n = len(parts)
    lands = [lax.empty(p.shape[1:], p.dtype) for p in parts]

    def body(*refs):
        src, land = refs[:n], refs[n:2 * n]
        outs = refs[2 * n:]
        other = 1 - lax.axis_index("c")
        for i in range(n):
            _rcopy(src[i].at[other], land[i], outs[4 * i], outs[4 * i + 1], _peer(SIBLING)).start()

    out_shape, out_specs, aliases = [], [], {}
    for i, p in enumerate(parts):
        out_shape += [pltpu.SemaphoreType.DMA(()), pltpu.SemaphoreType.DMA(()), pltpu.HBM(p.shape, p.dtype),
                      pltpu.HBM(p.shape[1:], p.dtype)]
        out_specs += [SEM_SPEC, SEM_SPEC, HBM_SPEC, HBM_SPEC]
        aliases[i] = 4 * i + 2
        aliases[n + i] = 4 * i + 3
    res = _split_call(body, name, [_in_hbm(p) for p in parts] + [_in_hbm(l) for l in lands], [HBM_SPEC] * (2 * n),
                      out_shape, out_specs, aliases)
    return [tuple(res[4 * i:4 * i + 4]) for i in range(n)], res[-1]


def _to_sibling_finish(started, after, name):
    n = len(started)

    def body(*refs):
        for i in range(n):
            send, recv, src, land = refs[4 * i:4 * i + 4]
            cp = _rcopy(src.at[0], land, send, recv, _peer(SIBLING))
            cp.wait_send()
            cp.wait_recv()

    ins, in_specs, out_shape, aliases = [], [], [], {}
    for i, st in enumerate(started):
        ins += [st[0], st[1], _in_hbm(st[2]), _in_hbm(st[3])]
        in_specs += [SEM_SPEC, SEM_SPEC, HBM_SPEC, HBM_SPEC]
        out_shape += [pltpu.HBM(st[2].shape, st[2].dtype), pltpu.HBM(st[3].shape, st[3].dtype)]
        aliases[4 * i + 2] = 2 * i
        aliases[4 * i + 3] = 2 * i + 1
    res = _split_call(body, name, ins + [after], in_specs + [ANY_SPEC], out_shape, [HBM_SPEC] * (2 * n), aliases)
    return [(res[2 * i], res[2 * i + 1]) for i in range(n)]


def _chip_sum(parts, from_sibling, my_c, tile, name):
    _, _, r, c = parts.shape
    tr, tc = tile

    def body(c_ref, p_ref, s_ref, o_ref):
        o_ref[...] = (p_ref[...].astype(F32) + s_ref[...].astype(F32)).astype(o_ref.dtype)

    blk = pl.BlockSpec((4, tr, tc), lambda i, j, c_ref: (0, i, j))
    return pl.pallas_call(
        body, name=name, out_shape=_sds((4, r, c), parts.dtype),
        grid_spec=pltpu.PrefetchScalarGridSpec(
            num_scalar_prefetch=1, grid=(r // tr, c // tc),
            in_specs=[pl.BlockSpec((None, 4, tr, tc), lambda i, j, c_ref: (c_ref[0], 0, i, j)), blk], out_specs=blk),
        compiler_params=_params(("parallel", "parallel")),
    )(my_c, parts, from_sibling)


def _to_chips_start(sums, name):
    n = len(sums)
    lands = [lax.empty((3,) + s.shape[1:], s.dtype) for s in sums]

    def body(*refs):
        src, land = refs[:n], refs[n:2 * n]
        outs = refs[2 * n:]
        for i in range(n):
            for k, rel in enumerate(OTHER_CHIPS):
                to = _peer(rel)
                _rcopy(src[i].at[_chip(to)], land[i].at[k], outs[4 * i].at[k], outs[4 * i + 1].at[k], to).start()

    out_shape, out_specs, aliases = [], [], {}
    for i, s in enumerate(sums):
        out_shape += [pltpu.SemaphoreType.DMA((3,)), pltpu.SemaphoreType.DMA((3,)), pltpu.HBM(s.shape, s.dtype),
                      pltpu.HBM((3,) + s.shape[1:], s.dtype)]
        out_specs += [SEM_SPEC, SEM_SPEC, HBM_SPEC, HBM_SPEC]
        aliases[i] = 4 * i + 2
        aliases[n + i] = 4 * i + 3
    res = _split_call(body, name, [_in_hbm(s) for s in sums] + [_in_hbm(l) for l in lands], [HBM_SPEC] * (2 * n),
                      out_shape, out_specs, aliases)
    return [tuple(res[4 * i:4 * i + 4]) for i in range(n)], res[-1]


def _to_chips_finish(started, after, name):
    n = len(started)

    def body(*refs):
        for i in range(n):
            send, recv, src, land = refs[4 * i:4 * i + 4]
            for k, rel in enumerate(OTHER_CHIPS):
                cp = _rcopy(src.at[0], land.at[k], send.at[k], recv.at[k], _peer(rel))
                cp.wait_send()
                cp.wait_recv()

    ins, in_specs, out_shape, aliases = [], [], [], {}
    for i, st in enumerate(started):
        ins += [st[0], st[1], _in_hbm(st[2]), _in_hbm(st[3])]
        in_specs += [SEM_SPEC, SEM_SPEC, HBM_SPEC, HBM_SPEC]
        out_shape += [pltpu.HBM(st[2].shape, st[2].dtype), pltpu.HBM(st[3].shape, st[3].dtype)]
        aliases[4 * i + 2] = 2 * i
        aliases[4 * i + 3] = 2 * i + 1
    res = _split_call(body, name, ins + [after], in_specs + [ANY_SPEC], out_shape, [HBM_SPEC] * (2 * n), aliases)
    return [(res[2 * i], res[2 * i + 1]) for i in range(n)]


def _exchange_small(v, reduce, name, deps=()):
    r, c = v.shape

    def body(v_ref, o_ref, land, send_sems, recv_sems):
        me = _slot(_me())
        copies = []
        for j in range(1, NDEV):
            cp = pltpu.make_async_remote_copy(
                src_ref=v_ref, dst_ref=land.at[me], send_sem=send_sems.at[j - 1],
                recv_sem=recv_sems.at[j - 1], device_id=_peer(j), device_id_type=MESH)
            cp.start()
            copies.append(cp)
        land[me] = v_ref[...]
        for cp in copies:
            cp.wait()
        if reduce:
            acc = land[0]
            for k in range(1, NDEV):
                acc = acc + land[k]
            o_ref[...] = acc
        else:
            o_ref[...] = land[...]

    vm = pl.BlockSpec(memory_space=pltpu.VMEM)
    body, dep_ins, dep_specs = _after(body, 1, deps)
    return pl.pallas_call(
        body, name=name,
        out_shape=_sds((r, c) if reduce else (NDEV, r, c), F32),
        in_specs=[vm] + dep_specs, out_specs=vm,
        scratch_shapes=[pltpu.VMEM((NDEV, r, c), F32), pltpu.SemaphoreType.DMA((NDEV - 1,)),
                        pltpu.SemaphoreType.DMA((NDEV - 1,))],
        compiler_params=_params(),
    )(v, *dep_ins)


def _matmul(a, b, *, mode, tm, tn, tk, name, out_dtype=F32, epi=None, extra=None, b_slots=False, out_slots=False,
            deps=(), col_block=0, into=None, n_total=None):
    b_pair = b if isinstance(b, tuple) else None
    if b_pair:
        assert mode == "nt" and tk == 2 * b[0].shape[1] == a.shape[1] and not b_slots
        b = b[0]
    slot_w = b.shape[-1] if b_slots else None
    if mode == "nn":
        M, K = a.shape
        N = NDEV * slot_w if b_slots else b.shape[1]
    elif mode == "tn":
        K, M = a.shape
        N = b.shape[1]
    else:
        M, K = a.shape
        N = b.shape[-2]
        if b_slots:
            assert K == NDEV * slot_w and tk == slot_w
    if mode == "nn" and b_slots:
        assert tn == slot_w
    if out_slots == "cols":
        assert tn * NDEV == N
    if out_slots == "rows":
        assert (M // NDEV) % tm == 0
    assert M % tm == 0 and N % tn == 0 and K % tk == 0, (name, M, N, K, tm, tn, tk)
    nk = K // tk
    dims = {"nn": ((1,), (0,)), "tn": ((0,), (0,)), "nt": ((1,), (1,))}[mode]

    if mode == "tn":
        a_spec = pl.BlockSpec((tk, tm), lambda i, j, k: (k, i))
    else:
        a_spec = pl.BlockSpec((tm, tk), lambda i, j, k: (i, k))
    if b_pair:
        b_spec = pl.BlockSpec((tn, tk // 2), lambda i, j, k: (j, 0))
    elif mode == "nt":
        b_spec = (pl.BlockSpec((None, tn, tk), lambda i, j, k: (k, j, 0)) if b_slots
                  else pl.BlockSpec((tn, tk), lambda i, j, k: (j, k)))
    else:
        b_spec = (pl.BlockSpec((None, tk, tn), lambda i, j, k: (j, k, 0)) if b_slots
                  else pl.BlockSpec((tk, tn), lambda i, j, k: (k, j)))
    tile = pl.BlockSpec((tm, tn), lambda i, j, k: (i, j + col_block))
    if out_slots == "cols":
        out_spec = pl.BlockSpec((None, None, tm, tn), lambda i, j, k: (j % 2, j // 2, i, 0))
        out_shape = _sds((2, 4, M, tn), out_dtype)
    elif out_slots == "rows":
        per = M // NDEV // tm
        out_spec = pl.BlockSpec((None, None, tm, tn), lambda i, j, k: ((i // per) % 2, (i // per) // 2, i % per, j))
        out_shape = _sds((2, 4, M // NDEV, N), out_dtype)
    else:
        out_spec, out_shape = tile, _sds((M, n_total or N), out_dtype)
    ins, in_specs = [a, b], [a_spec, b_spec]
    if b_pair:
        ins.append(b_pair[1])
        in_specs.append(b_spec)
    n_b = len(ins) - 1
    if epi in ("add", "dz"):
        ins.append(extra)
        in_specs.append(tile)
    aliases = {}
    if into is not None:
        aliases[len(ins)] = 0
        ins.append(into)
        in_specs.append(ANY_SPEC)
    if epi == "relu2":
        out_specs, out_shapes = [tile, tile], [_sds((M, N), F32), _sds((M, N), BF16)]
    else:
        out_specs, out_shapes = out_spec, out_shape
    n_in = len(ins)

    def body(*refs):
        outs = refs[n_in:-1] if nk > 1 else refs[n_in:]
        extra_ref = refs[1 + n_b]

        def finish(p):
            if epi is None:
                outs[0][...] = p.astype(out_dtype)
            elif epi == "add":
                outs[0][...] = (p + extra_ref[...]).astype(out_dtype)
            elif epi == "relu2":
                outs[0][...] = p
                rz = jnp.maximum(p, 0.0)
                outs[1][...] = (rz * rz).astype(BF16)
            else:
                outs[0][...] = (p * (2.0 * jnp.maximum(extra_ref[...], 0.0))).astype(out_dtype)

        def product():
            av = refs[0][...].astype(BF16)
            if b_pair:
                half = tk // 2
                return (lax.dot_general(av[:, :half], refs[1][...], (dims, ((), ())), preferred_element_type=F32)
                        + lax.dot_general(av[:, half:], refs[2][...], (dims, ((), ())), preferred_element_type=F32))
            return lax.dot_general(av, refs[1][...].astype(BF16), (dims, ((), ())), preferred_element_type=F32)

        if nk == 1:
            finish(product())
            return
        acc = refs[-1]
        k = pl.program_id(2)

        @pl.when(k == 0)
        def _():
            acc[...] = jnp.zeros_like(acc)

        acc[...] += product()

        @pl.when(k == nk - 1)
        def _():
            finish(acc[...])

    body, dep_ins, dep_specs = _after(body, n_in, deps)
    return pl.pallas_call(
        body, name=name, grid=(M // tm, N // tn, nk),
        in_specs=in_specs + dep_specs, out_specs=out_specs, out_shape=out_shapes, input_output_aliases=aliases,
        scratch_shapes=[pltpu.VMEM((tm, tn), F32)] if nk > 1 else [],
        compiler_params=_params(("parallel", "parallel", "arbitrary")),
    )(*ins, *dep_ins)


ROWS = 352


def _rmsnorm_fwd(h, w, name, deps=()):
    def body(h_ref, w_ref, u_ref):
        x = h_ref[...]
        rstd = lax.rsqrt(jnp.mean(x * x, axis=-1, keepdims=True) + EPS)
        u_ref[...] = (x * rstd * w_ref[...]).astype(BF16)

    row = pl.BlockSpec((ROWS, D), lambda i: (i, 0))
    body, dep_ins, dep_specs = _after(body, 2, deps)
    return pl.pallas_call(
        body, name=name, grid=(LP // ROWS,), in_specs=[row, pl.BlockSpec((1, D), lambda i: (0, 0))] + dep_specs,
        out_specs=row, out_shape=_sds((LP, D), BF16), compiler_params=_params(("parallel",)),
    )(h, w, *dep_ins)


TOKEN_ROWS = 512


def _rmsnorm_bwd_input(h, w, du, dres, name, deps=()):
    def math(h_ref, w_ref, du_ref, dres_ref):
        x = h_ref[...]
        rstd = lax.rsqrt(jnp.mean(x * x, axis=-1, keepdims=True) + EPS)
        xhat = x * rstd
        dy = du_ref[...]
        dxh = dy * w_ref[...]
        dh = dres_ref[...] + rstd * (dxh - xhat * jnp.mean(dxh * xhat, axis=-1, keepdims=True))
        return dh, jnp.sum(dy * xhat, axis=0, keepdims=True)

    def body(h_ref, w_ref, du_ref, dres_ref, hh_ref, duh_ref, dresh_ref, dx_ref, dhead_ref, gw_ref):
        dx_ref[...], part = math(h_ref, w_ref, du_ref, dres_ref)

        @pl.when(pl.program_id(0) == 0)
        def _():
            dhead_ref[...], head = math(hh_ref, w_ref, duh_ref, dresh_ref)
            gw_ref[...] = part + head

        @pl.when(pl.program_id(0) > 0)
        def _():
            gw_ref[...] += part

    rows = pl.BlockSpec((pl.Element(TOKEN_ROWS), pl.Element(D)),
                        lambda i: (pl.multiple_of(ROW_X + TOKEN_ROWS * i, 8), 0))
    head = pl.BlockSpec((ROW_X, D), lambda i: (0, 0))
    vec = pl.BlockSpec((1, D), lambda i: (0, 0))
    body, dep_ins, dep_specs = _after(body, 7, deps)
    return pl.pallas_call(
        body, name=name, grid=(SEQ // TOKEN_ROWS,),
        in_specs=[rows, vec, rows, rows, head, head, head] + dep_specs,
        out_specs=[pl.BlockSpec((TOKEN_ROWS, D), lambda i: (i, 0)), head, vec],
        out_shape=[_sds((SEQ, D), F32), _sds((ROW_X, D), F32), _sds((1, D), F32)],
        compiler_params=_params(("arbitrary",)),
    )(h, w, du, dres, h, du, dres, *dep_ins)


def _rmsnorm_bwd(h, w, du, dres, name, deps=()):
    def body(h_ref, w_ref, du_ref, dres_ref, dh_ref, dhb_ref, gw_ref):
        x = h_ref[...]
        rstd = lax.rsqrt(jnp.mean(x * x, axis=-1, keepdims=True) + EPS)
        xhat = x * rstd
        dy = du_ref[...]
        dxh = dy * w_ref[...]
        dh = dres_ref[...] + rstd * (dxh - xhat * jnp.mean(dxh * xhat, axis=-1, keepdims=True))
        dh_ref[...] = dh
        dhb_ref[...] = dh.astype(BF16)
        part = jnp.sum(dy * xhat, axis=0, keepdims=True)

        @pl.when(pl.program_id(0) == 0)
        def _():
            gw_ref[...] = part

        @pl.when(pl.program_id(0) > 0)
        def _():
            gw_ref[...] += part

    row = pl.BlockSpec((ROWS, D), lambda i: (i, 0))
    vec = pl.BlockSpec((1, D), lambda i: (0, 0))
    body, dep_ins, dep_specs = _after(body, 4, deps)
    return pl.pallas_call(
        body, name=name, grid=(LP // ROWS,), in_specs=[row, vec, row, row] + dep_specs, out_specs=[row, row, vec],
        out_shape=[_sds((LP, D), F32), _sds((LP, D), BF16), _sds((1, D), F32)],
        compiler_params=_params(("arbitrary",)),
    )(h, w, du, dres, *dep_ins)


def _loss_head(h2, wf, target):
    def body(h_ref, w_ref, t_ref, dh_ref, dhb_ref, sq_ref, gw_ref):
        i = pl.program_id(0)

        @pl.when(i == 0)
        def _():
            sq_ref[...] = jnp.zeros_like(sq_ref)
            gw_ref[...] = jnp.zeros_like(gw_ref)

        def rows(t, live):
            x = h_ref[...]
            rstd = lax.rsqrt(jnp.mean(x * x, axis=-1, keepdims=True) + EPS)
            xhat = x * rstd
            w = w_ref[...]
            err = xhat * w - t
            if live is not None:
                err = jnp.where(live, err, 0.0)
            sq_ref[...] += jnp.sum(err * err, axis=0, keepdims=True)
            dy = err * (1.0 / D)
            gw_ref[...] += jnp.sum(dy * xhat, axis=0, keepdims=True)
            dxh = dy * w
            dh = rstd * (dxh - xhat * jnp.mean(dxh * xhat, axis=-1, keepdims=True))
            dh_ref[...] = dh
            dhb_ref[...] = dh.astype(BF16)

        @pl.when(i == 0)
        def _():
            rid = lax.broadcasted_iota(jnp.int32, (ROWS, D), 0)
            rows(pltpu.roll(t_ref[...], ROW_X, 0), rid >= ROW_X)

        @pl.when(i > 0)
        def _():
            rows(t_ref[...], None)

    row = pl.BlockSpec((ROWS, D), lambda i: (i, 0))
    vec = pl.BlockSpec((1, D), lambda i: (0, 0))
    tgt = pl.BlockSpec((pl.Element(ROWS), pl.Element(D)),
                       lambda i: (pl.multiple_of(jnp.maximum(ROWS * i - ROW_X, 0), 8), 0))
    return pl.pallas_call(
        body, name="loss_head", grid=(LP // ROWS,),
        in_specs=[row, vec, tgt],
        out_specs=[row, row, vec, vec],
        out_shape=[_sds((LP, D), F32), _sds((LP, D), BF16), _sds((1, D), F32), _sds((1, D), F32)],
        compiler_params=_params(("arbitrary",)),
    )(h2, wf, target)


def _dot(a, b, dims):
    return lax.dot_general(a, b, (dims, ((), ())), preferred_element_type=F32)


NN, TN, NT = ((1,), (0,)), ((0,), (0,)), ((1,), (1,))


def _tri_sum(t, x):
    hi = x.astype(BF16)
    r1 = x - hi.astype(F32)
    mid = r1.astype(BF16)
    lo = (r1 - mid.astype(F32)).astype(BF16)
    return _dot(t, hi, NN) + _dot(t, mid, NN) + _dot(t, lo, NN)


def _gla_gates(glr_ref, gw2_ref, gb_ref, rows, row0):
    g_raw = _dot(glr_ref[rows, :].astype(BF16), gw2_ref[...], NN) + gb_ref[...]
    logsig = jnp.minimum(g_raw, 0.0) - jnp.log(1.0 + jnp.exp(-jnp.abs(g_raw)))
    rid = row0 + lax.broadcasted_iota(jnp.int32, g_raw.shape, 0)
    live = rid >= PAD
    return g_raw, jnp.where(live, logsig / TAU, 0.0), live


def _tri_masks():
    r = lax.broadcasted_iota(jnp.int32, (CHUNK, CHUNK), 0)
    c = lax.broadcasted_iota(jnp.int32, (CHUNK, CHUNK), 1)
    return r >= c


def _gla_specs(rev):
    n = NCH // CPS
    R = CPS * CHUNK
    st = (lambda s: n - 1 - s) if rev else (lambda s: s)
    return R, n, st, [
        pl.BlockSpec((R, KW), lambda s: (st(s), 0)),
        pl.BlockSpec((R, KW), lambda s: (st(s), 1)),
        pl.BlockSpec((R, GW), lambda s: (st(s), 1)),
        pl.BlockSpec((R, GW), lambda s: (st(s), 2)),
        pl.BlockSpec((R, 128), lambda s: (st(s), GLR_BLK)),
    ]


def _gla_fwd(proj, gw2p, gate_b, gnw, deps=()):
    R, n, st, pspecs = _gla_specs(False)

    def body(q_ref, k_ref, v_ref, r_ref, glr_ref, gw2_ref, gb_ref, gnw_ref, og_ref, o_ref, st_ref, state):
        s = pl.program_id(0)

        @pl.when(s == 0)
        def _():
            state[...] = jnp.zeros_like(state)

        causal = _tri_masks()
        tri = causal.astype(BF16)
        for c in range(CPS):
            rows = slice(c * CHUNK, (c + 1) * CHUNK)
            _, logg, _ = _gla_gates(glr_ref, gw2_ref, gb_ref, rows, s * R + c * CHUNK)
            G = _tri_sum(tri, logg)
            g_last = G[CHUNK - 1:CHUNK, :]
            q_dec = (q_ref[rows, :] * (DK ** -0.5) * jnp.exp(G)).astype(BF16)
            kk = k_ref[rows, :]
            k_inv = (kk * jnp.exp(-G)).astype(BF16)
            k_end = (kk * jnp.exp(g_last - G)).astype(BF16)
            decay = jnp.exp(g_last)
            for h in range(H):
                lk = slice(h * DK, (h + 1) * DK)
                lv = slice(h * DV, (h + 1) * DV)
                v = v_ref[rows, lv].astype(BF16)
                S = state[h]
                st_ref[c, h] = S
                A = jnp.where(causal, _dot(q_dec[:, lk], k_inv[:, lk], NT), 0.0).astype(BF16)
                o = _dot(A, v, NN) + _dot(q_dec[:, lk], S.astype(BF16), NT)
                state[h] = decay[:, lk] * S + _dot(v, k_end[:, lk], TN)
                o_ref[rows, lv] = o
                on = o * lax.rsqrt(jnp.mean(o * o, axis=-1, keepdims=True) + EPS) * gnw_ref[...]
                rr = r_ref[rows, lv]
                og_ref[rows, lv] = (on * (rr * jax.nn.sigmoid(rr))).astype(BF16)

    full = lambda shape: pl.BlockSpec(shape, lambda s: (0,) * len(shape))
    body, dep_ins, dep_specs = _after(body, 8, deps)
    return pl.pallas_call(
        body, name="gla_fwd", grid=(n,),
        in_specs=pspecs + [full((128, KW)), full((1, KW)), full((1, DV))] + dep_specs,
        out_specs=[pl.BlockSpec((R, GW), lambda s: (s, 0)), pl.BlockSpec((R, GW), lambda s: (s, 0)),
                   pl.BlockSpec((CPS, H, DV, DK), lambda s: (s, 0, 0, 0))],
        out_shape=[_sds((LP, GW + PW), BF16), _sds((LP, GW), F32), _sds((NCH, H, DV, DK), F32)],
        scratch_shapes=[pltpu.VMEM((H, DV, DK), F32)],
        compiler_params=_params(("arbitrary",)),
    )(proj, proj, proj, proj, proj, gw2p, gate_b, gnw, *dep_ins)


def _gla_bwd(proj, dmixed, o_saved, st_saved, gw2p, gate_b, gnw, deps=()):
    R, n, st, pspecs = _gla_specs(True)

    def body(q_ref, k_ref, v_ref, r_ref, glr_ref, dog_ref, o_ref, st_ref, gw2_ref, gb_ref, gnw_ref,
             dqkvr_ref, dglr_ref, ggn_ref, ggb_ref, ggw_ref, gstate):
        s = pl.program_id(0)

        @pl.when(s == 0)
        def _():
            gstate[...] = jnp.zeros_like(gstate)
            ggn_ref[...] = jnp.zeros_like(ggn_ref)
            ggb_ref[...] = jnp.zeros_like(ggb_ref)
            ggw_ref[...] = jnp.zeros_like(ggw_ref)

        causal = _tri_masks()
        tri = causal.astype(BF16)
        tri_up = (lax.broadcasted_iota(jnp.int32, (CHUNK, CHUNK), 0)
                  <= lax.broadcasted_iota(jnp.int32, (CHUNK, CHUNK), 1)).astype(BF16)
        gnw = gnw_ref[...]
        for c in reversed(range(CPS)):
            rows = slice(c * CHUNK, (c + 1) * CHUNK)
            g_raw, logg, live = _gla_gates(glr_ref, gw2_ref, gb_ref, rows, (n - 1 - s) * R + c * CHUNK)
            G = _tri_sum(tri, logg)
            g_last = G[CHUNK - 1:CHUNK, :]
            e_g, e_gi, e_end = jnp.exp(G), jnp.exp(-G), jnp.exp(g_last - G)
            q_dec = q_ref[rows, :] * (DK ** -0.5) * e_g
            kk = k_ref[rows, :]
            k_inv, k_end = kk * e_gi, kk * e_end
            q_dec_b, k_inv_b, k_end_b = q_dec.astype(BF16), k_inv.astype(BF16), k_end.astype(BF16)
            decay = jnp.exp(g_last)
            d_g, d_gl = [], []
            for h in range(H):
                lk = slice(h * DK, (h + 1) * DK)
                lv = slice(h * DV, (h + 1) * DV)
                o = o_ref[rows, lv]
                rr = r_ref[rows, lv]
                dog = dog_ref[rows, lv]
                rstd = lax.rsqrt(jnp.mean(o * o, axis=-1, keepdims=True) + EPS)
                ohat = o * rstd
                sr = jax.nn.sigmoid(rr)
                don = dog * (rr * sr)
                dqkvr_ref[rows, 2 * KW + GW + h * DV:2 * KW + GW + (h + 1) * DV] = (
                    dog * (ohat * gnw) * (sr * (1.0 + rr * (1.0 - sr)))).astype(BF16)
                ggn_ref[...] += jnp.sum(don * ohat, axis=0, keepdims=True)
                dohat = don * gnw
                do = (rstd * (dohat - ohat * jnp.mean(dohat * ohat, axis=-1, keepdims=True))).astype(BF16)
                v = v_ref[rows, lv].astype(BF16)
                S = st_ref[c, h]
                gS = gstate[h]
                S_b, gS_b = S.astype(BF16), gS.astype(BF16)
                qd, ki, ke = q_dec_b[:, lk], k_inv_b[:, lk], k_end_b[:, lk]
                A = jnp.where(causal, _dot(qd, ki, NT), 0.0).astype(BF16)
                dA = jnp.where(causal, _dot(do, v, NT), 0.0).astype(BF16)
                dv = _dot(A, do, TN) + _dot(ke, gS_b, NT)
                dq_dec = _dot(dA, ki, NN) + _dot(do, S_b, NN)
                dk_inv = _dot(dA, qd, TN)
                dk_end = _dot(v, gS_b, NN)
                d_decay = jnp.sum(gS * S, axis=0, keepdims=True)
                gstate[h] = decay[:, lk] * gS + _dot(do, qd, TN)
                dqkvr_ref[rows, lk] = (dq_dec * e_g[:, lk] * (DK ** -0.5)).astype(BF16)
                dqkvr_ref[rows, KW + h * DK:KW + (h + 1) * DK] = (
                    dk_inv * e_gi[:, lk] + dk_end * e_end[:, lk]).astype(BF16)
                dqkvr_ref[rows, 2 * KW + h * DV:2 * KW + (h + 1) * DV] = dv.astype(BF16)
                ke_prod = dk_end * k_end[:, lk]
                d_g.append(dq_dec * q_dec[:, lk] - dk_inv * k_inv[:, lk] - ke_prod)
                d_gl.append(jnp.sum(ke_prod, axis=0, keepdims=True) + d_decay * decay[:, lk])
            dlogg = _tri_sum(tri_up, jnp.concatenate(d_g, axis=1)) + jnp.concatenate(d_gl, axis=1)
            dg_raw = jnp.where(live, dlogg * (1.0 / TAU) * jax.nn.sigmoid(-g_raw), 0.0)
            ggb_ref[...] += jnp.sum(dg_raw, axis=0, keepdims=True)
            dg_b = dg_raw.astype(BF16)
            ggw_ref[...] += _dot(glr_ref[rows, :].astype(BF16), dg_b, TN)
            dglr_ref[rows, :] = _dot(dg_b, gw2_ref[...], NT).astype(BF16)

    full = lambda shape: pl.BlockSpec(shape, lambda s: (0,) * len(shape))
    body, dep_ins, dep_specs = _after(body, 11, deps)
    return pl.pallas_call(
        body, name="gla_bwd", grid=(n,),
        in_specs=pspecs + [pl.BlockSpec((R, GW), lambda s: (st(s), 0)), pl.BlockSpec((R, GW), lambda s: (st(s), 0)),
                           pl.BlockSpec((CPS, H, DV, DK), lambda s: (st(s), 0, 0, 0)),
                           full((128, KW)), full((1, KW)), full((1, DV))] + dep_specs,
        out_specs=[pl.BlockSpec((R, 2 * KW + 2 * GW), lambda s: (st(s), 0)), pl.BlockSpec((R, 128), lambda s: (st(s), 0)),
                   full((1, DV)), full((1, KW)), full((128, KW))],
        out_shape=[_sds((LP, D_INP), BF16), _sds((LP, 128), BF16),
                   _sds((1, DV), F32), _sds((1, KW), F32), _sds((128, KW), F32)],
        scratch_shapes=[pltpu.VMEM((H, DV, DK), F32)],
        compiler_params=_params(("arbitrary",)),
    )(proj, proj, proj, proj, proj, dmixed, o_saved, st_saved, gw2p, gate_b, gnw, *dep_ins)


def _pool_pre(x, win, rid):
    s, step = x, 1
    while step < win:
        s = s + pltpu.roll(s, step, 0)
        step *= 2
    cnt = jnp.clip(rid - (PAD - 1), 1, win).astype(F32)
    live = rid >= PAD
    return jnp.where(live, s / cnt - x, 0.0), cnt, live


def _pool_fwd(proj, pool_w, pool_scale, mixed):
    def body(pu_ref, w_ref, sc_ref, _, o_ref):
        rid = lax.broadcasted_iota(jnp.int32, (LP, GC), 0)
        for g, win in enumerate(WINDOWS):
            @pl.when(pl.program_id(0) == g)
            def _():
                y, _, _ = _pool_pre(pu_ref[...], win, rid)
                o_ref[...] = (_dot(y.astype(BF16), w_ref[...], NN) * sc_ref[...]).astype(BF16)

    col = lambda base: pl.BlockSpec((LP, GC), lambda g: (0, base + g))
    return pl.pallas_call(
        body, name="pool_fwd", grid=(len(WINDOWS),),
        in_specs=[col(POOL_BLK), pl.BlockSpec((None, GC, GC), lambda g: (g, 0, 0)),
                  pl.BlockSpec((1, GC), lambda g: (0, g)), ANY_SPEC],
        out_specs=col(GW // GC), out_shape=_sds(mixed.shape, BF16), input_output_aliases={3: 0},
        compiler_params=_params(("parallel",)),
    )(proj, pool_w, pool_scale, mixed)


def _pool_bwd(proj, dmixed, pool_w, pool_scale, dproj):
    def body(pu_ref, do_ref, w_ref, sc_ref, _, dpu_ref, dw_ref, dsc_ref):
        rid = lax.broadcasted_iota(jnp.int32, (LP, GC), 0)
        for g, win in enumerate(WINDOWS):
            @pl.when(pl.program_id(0) == g)
            def _():
                y, cnt, live = _pool_pre(pu_ref[...], win, rid)
                y_b = y.astype(BF16)
                w = w_ref[...]
                do = do_ref[...]
                dsc_ref[...] = jnp.sum(do * _dot(y_b, w, NN), axis=0, keepdims=True)
                dyw = (do * sc_ref[...]).astype(BF16)
                dw_ref[...] = _dot(y_b, dyw, TN)
                dy = jnp.where(live, _dot(dyw, w, NT), 0.0)
                s, step = dy / cnt, 1
                while step < win:
                    s = s + pltpu.roll(s, LP - step, 0)
                    step *= 2
                dpu_ref[...] = (s - dy).astype(BF16)

    col = lambda base: pl.BlockSpec((LP, GC), lambda g: (0, base + g))
    mat = pl.BlockSpec((None, GC, GC), lambda g: (g, 0, 0))
    vec = pl.BlockSpec((1, GC), lambda g: (0, g))
    return pl.pallas_call(
        body, name="pool_bwd", grid=(len(WINDOWS),),
        in_specs=[col(POOL_BLK), col(GW // GC), mat, vec, ANY_SPEC], out_specs=[col(POOL_BLK), mat, vec],
        out_shape=[_sds(dproj.shape, BF16), _sds((4, GC, GC), F32), _sds((1, PW), F32)],
        input_output_aliases={4: 0}, compiler_params=_params(("parallel",)),
    )(proj, dmixed, pool_w, pool_scale, dproj)


def _adamw_math(w, g, m, v):
    m = B1 * m + (1.0 - B1) * g
    v = B2 * v + (1.0 - B2) * (g * g)
    m_hat = m * (1.0 / (1.0 - B1 ** STEP))
    v_hat = v * (1.0 / (1.0 - B2 ** STEP))
    return -LR * (m_hat / (jnp.sqrt(v_hat) + AEPS) + WD * w), m, v


def _adamw_landed(sums, landed, my_chip, w, m, v, rows, name, cols=None):
    _, r, c = w.shape

    def body(chip_ref, s_ref, l_ref, w_ref, m_ref, v_ref, g_ref, d_ref, mo_ref, vo_ref):
        g = s_ref[...].astype(F32)
        for k in range(3):
            g = g + l_ref[k].astype(F32)
        g_ref[...] = g
        d_ref[...], mo_ref[...], vo_ref[...] = _adamw_math(w_ref[...], g, m_ref[...], v_ref[...])

    cols = cols or c
    blk = pl.BlockSpec((None, rows, cols), lambda i, j, chip_ref: (0, i, j))
    return pl.pallas_call(
        body, name=name, out_shape=[_sds((1, r, c), F32)] * 4,
        grid_spec=pltpu.PrefetchScalarGridSpec(
            num_scalar_prefetch=1, grid=(r // rows, c // cols),
            in_specs=[pl.BlockSpec((None, rows, cols), lambda i, j, chip_ref: (chip_ref[0], i, j)),
                      pl.BlockSpec((3, rows, cols), lambda i, j, chip_ref: (0, i, j)), blk, blk, blk],
            out_specs=[blk] * 4),
        compiler_params=_params(("parallel", "parallel")),
    )(my_chip, sums, landed, w, m, v)


def _adamw_small(g, w, m, v):
    def body(g_ref, w_ref, m_ref, v_ref, d_ref, mo_ref, vo_ref):
        d_ref[...], mo_ref[...], vo_ref[...] = _adamw_math(w_ref[...], g_ref[...], m_ref[...], v_ref[...])

    return pl.pallas_call(body, name="adamw_small", out_shape=[_sds(w.shape, F32)] * 3)(g, w, m, v)


SMALL_REPL = (("norm1_w", D), ("norm2_w", D), ("final_norm_w", D), ("pool_scale", PW), ("gate_b", KW),
              ("gla_norm_w", DV))


def _pack_rows(vecs, rows):
    flat = jnp.concatenate([jnp.ravel(v) for v in vecs])
    return jnp.pad(flat, (0, rows * 1024 - flat.shape[0])).reshape(rows, 1024)


def kernel(x, meta_tokens, norm1_w, w_in, gate_w2, gate_b, gla_norm_w, pool_w, pool_scale, w_out, norm2_w, mlp_w1, mlp_w2, final_norm_w, loss_target, m_meta_tokens, m_norm1_w, m_w_in, m_gate_w2, m_gate_b, m_gla_norm_w, m_pool_w, m_pool_scale, m_w_out, m_norm2_w, m_mlp_w1, m_mlp_w2, m_final_norm_w, v_meta_tokens, v_norm1_w, v_w_in, v_gate_w2, v_gate_b, v_gla_norm_w, v_pool_w, v_pool_scale, v_w_out, v_norm2_w, v_mlp_w1, v_mlp_w2, v_final_norm_w):
    me = 4 * lax.axis_index("x") + 2 * lax.axis_index("y") + lax.axis_index("c")
    W = dict(meta_tokens=meta_tokens, norm1_w=norm1_w, w_in=w_in, gate_w2=gate_w2, gate_b=gate_b,
             gla_norm_w=gla_norm_w, pool_w=pool_w, pool_scale=pool_scale, w_out=w_out, norm2_w=norm2_w,
             mlp_w1=mlp_w1, mlp_w2=mlp_w2, final_norm_w=final_norm_w)
    Mo = dict(meta_tokens=m_meta_tokens, norm1_w=m_norm1_w, w_in=m_w_in, gate_w2=m_gate_w2, gate_b=m_gate_b,
              gla_norm_w=m_gla_norm_w, pool_w=m_pool_w, pool_scale=m_pool_scale, w_out=m_w_out, norm2_w=m_norm2_w,
              mlp_w1=m_mlp_w1, mlp_w2=m_mlp_w2, final_norm_w=m_final_norm_w)
    Vo = dict(meta_tokens=v_meta_tokens, norm1_w=v_norm1_w, w_in=v_w_in, gate_w2=v_gate_w2, gate_b=v_gate_b,
              gla_norm_w=v_gla_norm_w, pool_w=v_pool_w, pool_scale=v_pool_scale, w_out=v_w_out, norm2_w=v_norm2_w,
              mlp_w1=v_mlp_w1, mlp_w2=v_mlp_w2, final_norm_w=v_final_norm_w)

    small = _exchange_small(_pack_rows([meta_tokens, gate_w2[0]], 8), False, "gather_small")
    ex = _Exchange(dict(w_in=w_in[0].T, w_out=w_out[0], pool_w=pool_w[0].reshape(4 * 32, GC), mlp_w1=mlp_w1[0],
                        mlp_w2a=mlp_w2[0][:, :D // 2], mlp_w2b=mlp_w2[0][:, D // 2:]), small)
    meta_full = small[:, 0:4].reshape(NDEV, N_META, D // NDEV).transpose(1, 0, 2).reshape(N_META, D)
    gw2_full = small[:, 4].reshape(NDEV, RANK, KW // NDEV).transpose(1, 0, 2).reshape(RANK, KW)
    gw2p = jnp.pad(gw2_full, ((0, 128 - RANK), (0, 0))).astype(BF16)

    step = _layer_step(x[0], loss_target[0], meta_full, gw2p, ex, norm1_w, gate_b, gla_norm_w, pool_scale, norm2_w,
                       final_norm_w.reshape(1, D))
    grad_x = step["dx"][None]

    last = step["dx"]
    out = {}
    for group in ("down", "up", "mix"):
        for k, (sums, landed) in ex.grad_finish(group, last).items():
            out[k] = _adamw_landed(sums, landed, ex.my_chip, W[k], Mo[k], Vo[k], SHARD_ROWS[k], "adamw_" + k)
            last = out[k][1]

    loss_part = 0.5 * jnp.sum(step["sq"]) / D
    packed = jnp.concatenate([
        _pack_rows([step[k] for k, _ in SMALL_REPL] + [loss_part], 8),
        step["gate_w2"][:RANK].reshape(8, 1024), step["dhead"][PAD:].reshape(32, 1024)], axis=0)
    red = _exchange_small(packed, True, "reduce_small", deps=[last])
    loss = red[7, 768]
    g_gw2_mine = lax.dynamic_slice(red[8:16].reshape(RANK, KW), (0, me * (KW // NDEV)), (RANK, KW // NDEV))
    g_meta_mine = lax.dynamic_slice(red[16:48].reshape(N_META, D), (0, me * (D // NDEV)), (N_META, D // NDEV))

    done = ex.grad_finish("w_in", red)
    poolw3 = lambda a: a.reshape(1, 4 * 32, GC)
    res = _adamw_landed(*done["pool_w"], ex.my_chip, poolw3(pool_w), poolw3(m_pool_w), poolw3(v_pool_w),
                        SHARD_ROWS["pool_w"], "adamw_pool_w")
    out["pool_w"] = [a.reshape(pool_w.shape) for a in res]
    tr = lambda a: a[0].T[None]
    res = _adamw_landed(*done["w_in"], ex.my_chip, tr(w_in), tr(m_w_in), tr(v_w_in), D_IN // NDEV, "adamw_w_in",
                        cols=256)
    out["w_in"] = [a[0].T[None] for a in res]

    def small_pack(P):
        return jnp.concatenate([_pack_rows([P[k] for k, _ in SMALL_REPL], 8),
                                _pack_rows([P["meta_tokens"], P["gate_w2"]], 8)], axis=0)

    g_small = jnp.concatenate([red[0:8], _pack_rows([g_meta_mine, g_gw2_mine], 8)], axis=0)
    g_small = g_small.at[7, 768].set(0.0)
    res_small = _adamw_small(g_small, small_pack(W), small_pack(Mo), small_pack(Vo))
    res_small = [g_small] + list(res_small)
    off = 0
    for k, nel in SMALL_REPL:
        out[k] = [a[0:8].reshape(-1)[off:off + nel].reshape(W[k].shape) for a in res_small]
        off += nel
    out["meta_tokens"] = [a[8:12].reshape(N_META, D // NDEV) for a in res_small]
    out["gate_w2"] = [a[12].reshape(1, RANK, KW // NDEV) for a in res_small]

    order = ["meta_tokens", "norm1_w", "w_in", "gate_w2", "gate_b", "gla_norm_w", "pool_w", "pool_scale", "w_out",
             "norm2_w", "mlp_w1", "mlp_w2", "final_norm_w"]
    return (loss, grad_x, *[out[k][0] for k in order], *[out[k][1] for k in order],
            *[out[k][2] for k in order], *[out[k][3] for k in order])


SHARD_ROWS = dict(w_out=256, mlp_w1=512, mlp_w2=256, pool_w=128)
C_GLR = 2 * KW + 2 * GW
GATHER_GROUPS = dict(w_in=("w_in",), mix=("w_out", "pool_w"), up=("mlp_w1",), down_a=("mlp_w2a",),
                     down_b=("mlp_w2b",))
GRAD_GROUPS = dict(down=("mlp_w2",), up=("mlp_w1",), mix=("w_out",), w_in=("pool_w", "w_in"))


class _Exchange:
    def __init__(self, shards, after):
        names = list(shards)
        first, token = _gather_start([shards[names[0]].astype(BF16)], "gather_start_" + names[0], after)
        rest, self.started = _gather_start([shards[k].astype(BF16) for k in names[1:]], "gather_start_rest", token)
        self.state = dict(zip(names, first + rest))
        self.my_c = lax.axis_index("c").astype(jnp.int32).reshape(1)
        self.my_chip = (2 * lax.axis_index("x") + lax.axis_index("y")).astype(jnp.int32).reshape(1)
        self.sibling, self.chips = {}, {}

    def forward(self, group, after):
        ks = GATHER_GROUPS[group]
        fwd, token = _gather_forward([self.state[k] for k in ks], after, "gather_forward_" + group)
        self.state.update(zip(ks, fwd))
        return token

    def weights(self, group, after):
        ks = GATHER_GROUPS[group]
        g = dict(zip(ks, _gather_finish([self.state[k] for k in ks], after, "gather_finish_" + group)))
        if group == "w_in":
            nat = g["w_in"].reshape(D_IN, D)
            return jnp.concatenate([nat[:C_GLR], nat[C_GLR + RANK:], nat[C_GLR:C_GLR + RANK],
                                    jnp.zeros((D_INP - D_IN, D), BF16)], axis=0)
        if group == "mix":
            return (g["w_out"].reshape(D, D),
                    g["pool_w"].reshape(NDEV, 4, 32, GC).transpose(1, 0, 2, 3).reshape(4, GC, GC))
        return g["mlp_w1"] if group == "up" else g[ks[0]].reshape(DFF, D // 2)

    def grad(self, group, grads):
        parts = dict(grads)
        if group == "w_in":
            g = parts["w_in"]
            nat = jnp.concatenate([g[:C_GLR], g[C_GLR + PW:C_GLR + PW + RANK], g[C_GLR:C_GLR + PW]], axis=0)
            parts["w_in"] = nat.reshape(4, 2, D_IN // NDEV, D).transpose(1, 0, 2, 3)
            parts["pool_w"] = (parts["pool_w"].astype(BF16).reshape(4, 4, 2, 32, GC).transpose(2, 1, 0, 3, 4)
                               .reshape(2, 4, 4 * 32, GC))
        ks = GRAD_GROUPS[group]
        started, token = _to_sibling_start([parts[k] for k in ks], "grad_sibling_start_" + group)
        self.sibling[group] = started
        return token

    def grad_mid(self, group, after):
        ks = GRAD_GROUPS[group]
        both = _to_sibling_finish(self.sibling[group], after, "grad_sibling_finish_" + group)
        tile = lambda k, p: (p.shape[2], 512) if k == "w_in" else (SHARD_ROWS[k], p.shape[3])
        sums = [_chip_sum(p, s, self.my_c, tile(k, p), "chip_sum_" + k) for k, (p, s) in zip(ks, both)]
        self.chips[group], token = _to_chips_start(sums, "grad_chips_start_" + group)
        return token

    def grad_finish(self, group, after):
        done = _to_chips_finish(self.chips[group], after, "grad_chips_finish_" + group)
        return dict(zip(GRAD_GROUPS[group], done))


def _layer_step(x, target, meta_full, gw2p, ex, norm1_w, gate_b, gla_norm_w, pool_scale, norm2_w, final_norm_w):
    h0 = jnp.concatenate([jnp.zeros((PAD, D), F32), meta_full, x], axis=0)
    u1 = _rmsnorm_fwd(h0, norm1_w, "rmsnorm1", deps=[ex.started])
    win_p = ex.weights("w_in", ex.forward("w_in", u1))
    proj = _matmul(u1, win_p, mode="nt", tm=1056, tn=1408, tk=2048, name="proj")
    tok = ex.forward("mix", proj)
    mixed, o_saved, st_saved = _gla_fwd(proj, gw2p, gate_b, gla_norm_w, deps=[tok])
    wout_f, poolw_f = ex.weights("mix", mixed)
    mixed = _pool_fwd(proj, poolw_f, pool_scale, mixed)
    h1 = _matmul(mixed, wout_f, mode="nn", tm=1056, tn=1024, tk=2048, name="mix_out", epi="add", extra=h0)
    tok = ex.forward("up", h1)
    u2 = _rmsnorm_fwd(h1, norm2_w, "rmsnorm2", deps=[tok])
    w1_g = ex.weights("up", u2)
    z, act = _matmul(u2, w1_g, mode="nn", tm=1056, tn=1024, tk=2048, name="mlp_up", epi="relu2", b_slots=True)
    w2a = ex.weights("down_a", ex.forward("down_a", act))
    h2 = _matmul(act, w2a, mode="nn", tm=1056, tn=1024, tk=2048, name="mlp_down_a", epi="add", extra=h1, n_total=D)
    w2b = ex.weights("down_b", ex.forward("down_b", h2))
    h2 = _matmul(act, w2b, mode="nn", tm=1056, tn=1024, tk=2048, name="mlp_down_b", epi="add", extra=h1, n_total=D,
                 col_block=1, into=h2)
    dh2, dh2b, sq, g_fnw = _loss_head(h2, final_norm_w, target)

    g_w2 = _matmul(act, dh2b, mode="tn", tm=512, tn=1024, tk=LP, name="d_mlp_w2", out_dtype=BF16, out_slots="rows")
    tok = ex.grad("down", dict(mlp_w2=g_w2))
    dz = _matmul(dh2b, (w2a, w2b), mode="nt", tm=1056, tn=1024, tk=2048, name="d_act", out_dtype=BF16, epi="dz",
                 extra=z, deps=[tok])
    tok = ex.grad_mid("down", dz)
    g_w1 = _matmul(u2, dz, mode="tn", tm=512, tn=1024, tk=LP, name="d_mlp_w1", out_dtype=BF16, out_slots="cols",
                   deps=[tok])
    tok = ex.grad("up", dict(mlp_w1=g_w1))
    du2 = _matmul(dz, w1_g, mode="nt", tm=1056, tn=1024, tk=1024, name="d_u2", b_slots=True, deps=[tok])
    tok = ex.grad_mid("up", du2)
    dh1, dh1b, g_n2 = _rmsnorm_bwd(h1, norm2_w, du2, dh2, "rmsnorm2_bwd", deps=[tok])
    g_wout = _matmul(mixed, dh1b, mode="tn", tm=256, tn=1024, tk=LP, name="d_w_out", out_dtype=BF16, out_slots="rows")
    tok = ex.grad("mix", dict(w_out=g_wout))
    dmixed = _matmul(dh1b, wout_f, mode="nt", tm=1056, tn=1024, tk=2048, name="d_mixed", deps=[tok])
    tok = ex.grad_mid("mix", dmixed)
    dproj, dglr, g_gnw, g_gb, g_gw2 = _gla_bwd(proj, dmixed, o_saved, st_saved, gw2p, gate_b, gla_norm_w, deps=[tok])
    dproj, g_poolw, g_psc = _pool_bwd(proj, dmixed, poolw_f, pool_scale, dproj)
    dproj = lax.dynamic_update_slice(dproj, dglr, (0, GLR_BLK * 128))
    g_win_p = _matmul(dproj, u1, mode="tn", tm=1408, tn=1024, tk=LP, name="d_w_in", out_dtype=BF16)
    tok = ex.grad("w_in", dict(pool_w=g_poolw, w_in=g_win_p))
    du1 = _matmul(dproj, win_p, mode="nn", tm=1056, tn=1024, tk=1408, name="d_u1", deps=[tok])
    tok = ex.grad_mid("w_in", du1)
    dx, dhead, g_n1 = _rmsnorm_bwd_input(h0, norm1_w, du1, dh1, "rmsnorm1_bwd", deps=[tok])
    return dict(dx=dx, dhead=dhead, sq=sq, gate_w2=g_gw2, norm1_w=g_n1, norm2_w=g_n2, final_norm_w=g_fnw, pool_scale=g_psc,
                gate_b=g_gb, gla_norm_w=g_gnw)
```

```python
import functools

import jax
import jax.numpy as jnp
from jax import lax
from jax.experimental import pallas as pl
from jax.experimental.pallas import tpu as pltpu

F32, BF16 = jnp.float32, jnp.bfloat16
MESH = pl.DeviceIdType.MESH

NDEV = 8
D = 2048
SEQ = 2048
N_META = 16
CHUNK = 64
PAD = (-N_META) % CHUNK
ROW_X = PAD + N_META
LP = ROW_X + SEQ
NCH = LP // CHUNK
H = 4
DK = 128
DV = 256
KW = H * DK
GW = H * DV
PW = 1024
RANK = 16
TAU = 16.0
WINDOWS = (2, 4, 8, 16)
GC = 256
DFF = 4 * D
EPS = 1e-6
D_IN = 2 * KW + 2 * GW + RANK + PW
D_INP = 4224
GLR_BLK = (2 * KW + 2 * GW + PW) // 128
POOL_BLK = (2 * KW + 2 * GW) // GC
LR, B1, B2, AEPS, WD, STEP = 0.001, 0.9, 0.999, 1e-08, 0.01, 10
VMEM_LIMIT = 48 * 1024 * 1024
CPS = 3


def _params(sem=None):
    return pltpu.CompilerParams(dimension_semantics=sem, vmem_limit_bytes=VMEM_LIMIT)


def _sds(shape, dtype):
    return jax.ShapeDtypeStruct(shape, dtype)


def _me():
    return lax.axis_index("x"), lax.axis_index("y"), lax.axis_index("c")


def _peer(j):
    x, y, c = _me()
    return (x ^ ((j >> 2) & 1), y ^ ((j >> 1) & 1), c ^ (j & 1))


def _slot(dev):
    return 4 * dev[0] + 2 * dev[1] + dev[2]


HBM_SPEC = pl.BlockSpec(memory_space=pltpu.HBM)
SEM_SPEC = pl.BlockSpec(memory_space=pltpu.SEMAPHORE)
ANY_SPEC = pl.BlockSpec(memory_space=pl.ANY)
EFFECT = pltpu.SideEffectType.DATAFLOW_SIDE_EFFECTING
SIBLING = 1
OTHER_CHIPS = (2, 4, 6)


def _in_hbm(a):
    return pltpu.with_memory_space_constraint(a, pltpu.HBM)


def _chip(dev):
    return 2 * dev[0] + dev[1]


def _rcopy(src, dst, send_sem, recv_sem, to):
    return pltpu.make_async_remote_copy(src_ref=src, dst_ref=dst, send_sem=send_sem, recv_sem=recv_sem,
                                        device_id=to, device_id_type=MESH)


def _split_call(body, name, ins, in_specs, out_shape, out_specs, aliases, scratch=()):
    n = len(ins) + len(out_shape)

    def with_token(*refs):
        body(*refs[:n], *refs[n + 1:])
        refs[n][...] = jnp.zeros_like(refs[n])

    return pl.pallas_call(
        with_token, name=name, in_specs=in_specs, out_shape=list(out_shape) + [_sds((8, 128), F32)],
        out_specs=list(out_specs) + [pl.BlockSpec(memory_space=pltpu.VMEM)],
        input_output_aliases=aliases, scratch_shapes=list(scratch),
        compiler_params=pltpu.CompilerParams(has_side_effects=EFFECT),
    )(*ins)


def _after(body, n_in, deps):
    deps = [d for d in deps if d is not None]
    if not deps:
        return body, [], []
    return (lambda *refs: body(*refs[:n_in], *refs[n_in + len(deps):])), deps, [ANY_SPEC] * len(deps)


def _gather_start(shards, name, after):
    n = len(shards)
    me = _slot(_me())
    lands = [lax.dynamic_update_slice(lax.empty((NDEV,) + s.shape, s.dtype), s[None], (me, 0, 0)) for s in shards]

    def body(*refs):
        src, land = refs[:n], refs[n:2 * n]
        outs = refs[2 * n + 1:]
        for i in range(n):
            send_sems, recv_sems = outs[4 * i], outs[4 * i + 1]
            for k, rel in enumerate((SIBLING,) + OTHER_CHIPS):
                _rcopy(src[i], land[i].at[_slot(_me())], send_sems.at[k], recv_sems.at[k], _peer(rel)).start()

    out_shape, out_specs, aliases = [], [], {}
    for i, s in enumerate(shards):
        out_shape += [pltpu.SemaphoreType.DMA((4,)), pltpu.SemaphoreType.DMA((4,)), pltpu.HBM(s.shape, s.dtype),
                      pltpu.HBM((NDEV,) + s.shape, s.dtype)]
        out_specs += [SEM_SPEC, SEM_SPEC, HBM_SPEC, HBM_SPEC]
        aliases[i] = 4 * i + 2
        aliases[n + i] = 4 * i + 3
    res = _split_call(body, name, [_in_hbm(s) for s in shards] + [_in_hbm(l) for l in lands] + [after],
                      [HBM_SPEC] * (2 * n) + [ANY_SPEC], out_shape, out_specs, aliases)
    return [tuple(res[4 * i:4 * i + 4]) for i in range(n)], res[-1]


def _gather_forward(started, after, name):
    n = len(started)

    def body(*refs):
        land, recv1 = refs[:n], refs[n:2 * n]
        outs = refs[2 * n + 1:]
        for i in range(n):
            send2, recv2 = outs[3 * i + 1], outs[3 * i + 2]
            for k, rel in enumerate(OTHER_CHIPS):
                blk = land[i].at[_slot(_peer(rel))]
                _rcopy(blk, blk, send2.at[k], recv1[i].at[1 + k], _peer(rel)).wait_recv()
                _rcopy(blk, blk, send2.at[k], recv2.at[k], _peer(SIBLING)).start()

    ins = [_in_hbm(st[3]) for st in started] + [st[1] for st in started] + [after]
    out_shape, out_specs, aliases = [], [], {}
    for i, st in enumerate(started):
        out_shape += [pltpu.HBM(st[3].shape, st[3].dtype), pltpu.SemaphoreType.DMA((3,)), pltpu.SemaphoreType.DMA((3,))]
        out_specs += [HBM_SPEC, SEM_SPEC, SEM_SPEC]
        aliases[i] = 3 * i
    res = _split_call(body, name, ins, [HBM_SPEC] * n + [SEM_SPEC] * n + [ANY_SPEC], out_shape, out_specs, aliases)
    return [(st[0], st[1], st[2], res[3 * i], res[3 * i + 1], res[3 * i + 2]) for i, st in enumerate(started)], res[-1]


def _gather_finish(forwarded, after, name):
    n = len(forwarded)

    def body(*refs):
        for i in range(n):
            send1, recv1, src, land, send2, recv2 = refs[6 * i:6 * i + 6]
            me = _slot(_me())
            sib = _slot(_peer(SIBLING))
            for k, rel in enumerate((SIBLING,) + OTHER_CHIPS):
                _rcopy(src, land.at[me], send1.at[k], recv1.at[k], _peer(rel)).wait_send()
            _rcopy(src, land.at[sib], send1.at[0], recv1.at[0], _peer(SIBLING)).wait_recv()
            for k, rel in enumerate(OTHER_CHIPS):
                mine, theirs = land.at[_slot(_peer(rel))], land.at[_slot(_peer(rel ^ SIBLING))]
                _rcopy(mine, mine, send2.at[k], recv2.at[k], _peer(SIBLING)).wait_send()
                _rcopy(theirs, theirs, send2.at[k], recv2.at[k], _peer(SIBLING)).wait_recv()

    ins, in_specs, out_shape, aliases = [], [], [], {}
    for i, f in enumerate(forwarded):
        ins += [f[0], f[1], _in_hbm(f[2]), _in_hbm(f[3]), f[4], f[5]]
        in_specs += [SEM_SPEC, SEM_SPEC, HBM_SPEC, HBM_SPEC, SEM_SPEC, SEM_SPEC]
        out_shape.append(pltpu.HBM(f[3].shape, f[3].dtype))
        aliases[6 * i + 3] = i
    res = _split_call(body, name, ins + [after], in_specs + [ANY_SPEC], out_shape, [HBM_SPEC] * n, aliases)
    return list(res[:-1])


def _to_sibling_start(parts, name):
    n = len(parts)
    lands = [lax.empty(p.shape[1:], p.dtype) for p in parts]

    def body(*refs):
        src, land = refs[:n], refs[n:2 * n]
        outs = refs[2 * n:]
        other = 1 - lax.axis_index("c")
        for i in range(n):
            _rcopy(src[i].at[other], land[i], outs[4 * i], outs[4 * i + 1], _peer(SIBLING)).start()

    out_shape, out_specs, aliases = [], [], {}
    for i, p in enumerate(parts):
        out_shape += [pltpu.SemaphoreType.DMA(()), pltpu.SemaphoreType.DMA(()), pltpu.HBM(p.shape, p.dtype),
                      pltpu.HBM(p.shape[1:], p.dtype)]
        out_specs += [SEM_SPEC, SEM_SPEC, HBM_SPEC, HBM_SPEC]
        aliases[i] = 4 * i + 2
        aliases[n + i] = 4 * i + 3
    res = _split_call(body, name, [_in_hbm(p) for p in parts] + [_in_hbm(l) for l in lands], [HBM_SPEC] * (2 * n),
                      out_shape, out_specs, aliases)
    return [tuple(res[4 * i:4 * i + 4]) for i in range(n)], res[-1]


def _to_sibling_finish(started, after, name):
    n = len(started)

    def body(*refs):
        for i in range(n):
            send, recv, src, land = refs[4 * i:4 * i + 4]
            cp = _rcopy(src.at[0], land, send, recv, _peer(SIBLING))
            cp.wait_send()
            cp.wait_recv()

    ins, in_specs, out_shape, aliases = [], [], [], {}
    for i, st in enumerate(started):
        ins += [st[0], st[1], _in_hbm(st[2]), _in_hbm(st[3])]
        in_specs += [SEM_SPEC, SEM_SPEC, HBM_SPEC, HBM_SPEC]
        out_shape += [pltpu.HBM(st[2].shape, st[2].dtype), pltpu.HBM(st[3].shape, st[3].dtype)]
        aliases[4 * i + 2] = 2 * i
        aliases[4 * i + 3] = 2 * i + 1
    res = _split_call(body, name, ins + [after], in_specs + [ANY_SPEC], out_shape, [HBM_SPEC] * (2 * n), aliases)
    return [(res[2 * i], res[2 * i + 1]) for i in range(n)]


def _chip_sum(parts, from_sibling, my_c, tile, name):
    _, _, r, c = parts.shape
    tr, tc = tile

    def body(c_ref, p_ref, s_ref, o_ref):
        o_ref[...] = (p_ref[...].astype(F32) + s_ref[...].astype(F32)).astype(o_ref.dtype)

    blk = pl.BlockSpec((4, tr, tc), lambda i, j, c_ref: (0, i, j))
    return pl.pallas_call(
        body, name=name, out_shape=_sds((4, r, c), parts.dtype),
        grid_spec=pltpu.PrefetchScalarGridSpec(
            num_scalar_prefetch=1, grid=(r // tr, c // tc),
            in_specs=[pl.BlockSpec((None, 4, tr, tc), lambda i, j, c_ref: (c_ref[0], 0, i, j)), blk], out_specs=blk),
        compiler_params=_params(("parallel", "parallel")),
    )(my_c, parts, from_sibling)


def _to_chips_start(sums, name):
    n = len(sums)
    lands = [lax.empty((3,) + s.shape[1:], s.dtype) for s in sums]

    def body(*refs):
        src, land = refs[:n], refs[n:2 * n]
        outs = refs[2 * n:]
        for i in range(n):
            for k, rel in enumerate(OTHER_CHIPS):
                to = _peer(rel)
                _rcopy(src[i].at[_chip(to)], land[i].at[k], outs[4 * i].at[k], outs[4 * i + 1].at[k], to).start()

    out_shape, out_specs, aliases = [], [], {}
    for i, s in enumerate(sums):
        out_shape += [pltpu.SemaphoreType.DMA((3,)), pltpu.SemaphoreType.DMA((3,)), pltpu.HBM(s.shape, s.dtype),
                      pltpu.HBM((3,) + s.shape[1:], s.dtype)]
        out_specs += [SEM_SPEC, SEM_SPEC, HBM_SPEC, HBM_SPEC]
        aliases[i] = 4 * i + 2
        aliases[n + i] = 4 * i + 3
    res = _split_call(body, name, [_in_hbm(s) for s in sums] + [_in_hbm(l) for l in lands], [HBM_SPEC] * (2 * n),
                      out_shape, out_specs, aliases)
    return [tuple(res[4 * i:4 * i + 4]) for i in range(n)], res[-1]


def _to_chips_finish(started, after, name):
    n = len(started)

    def body(*refs):
        for i in range(n):
            send, recv, src, land = refs[4 * i:4 * i + 4]
            for k, rel in enumerate(OTHER_CHIPS):
                cp = _rcopy(src.at[0], land.at[k], send.at[k], recv.at[k], _peer(rel))
                cp.wait_send()
                cp.wait_recv()

    ins, in_specs, out_shape, aliases = [], [], [], {}
    for i, st in enumerate(started):
        ins += [st[0], st[1], _in_hbm(st[2]), _in_hbm(st[3])]
        in_specs += [SEM_SPEC, SEM_SPEC, HBM_SPEC, HBM_SPEC]
        out_shape += [pltpu.HBM(st[2].shape, st[2].dtype), pltpu.HBM(st[3].shape, st[3].dtype)]
        aliases[4 * i + 2] = 2 * i
        aliases[4 * i + 3] = 2 * i + 1
    res = _split_call(body, name, ins + [after], in_specs + [ANY_SPEC], out_shape, [HBM_SPEC] * (2 * n), aliases)
    return [(res[2 * i], res[2 * i + 1]) for i in range(n)]


def _exchange_small(v, reduce, name, deps=()):
    r, c = v.shape

    def body(v_ref, o_ref, land, send_sems, recv_sems):
        me = _slot(_me())
        copies = []
        for j in range(1, NDEV):
            cp = pltpu.make_async_remote_copy(
                src_ref=v_ref, dst_ref=land.at[me], send_sem=send_sems.at[j - 1],
                recv_sem=recv_sems.at[j - 1], device_id=_peer(j), device_id_type=MESH)
            cp.start()
            copies.append(cp)
        land[me] = v_ref[...]
        for cp in copies:
            cp.wait()
        if reduce:
            acc = land[0]
            for k in range(1, NDEV):
                acc = acc + land[k]
            o_ref[...] = acc
        else:
            o_ref[...] = land[...]

    vm = pl.BlockSpec(memory_space=pltpu.VMEM)
    body, dep_ins, dep_specs = _after(body, 1, deps)
    return pl.pallas_call(
        body, name=name,
        out_shape=_sds((r, c) if reduce else (NDEV, r, c), F32),
        in_specs=[vm] + dep_specs, out_specs=vm,
        scratch_shapes=[pltpu.VMEM((NDEV, r, c), F32), pltpu.SemaphoreType.DMA((NDEV - 1,)),
                        pltpu.SemaphoreType.DMA((NDEV - 1,))],
        compiler_params=_params(),
    )(v, *dep_ins)


def _matmul(a, b, *, mode, tm, tn, tk, name, out_dtype=F32, epi=None, extra=None, b_slots=False, out_slots=False,
            deps=(), col_block=0, into=None, n_total=None):
    b_pair = b if isinstance(b, tuple) else None
    if b_pair:
        assert mode == "nt" and tk == 2 * b[0].shape[1] == a.shape[1] and not b_slots
        b = b[0]
    slot_w = b.shape[-1] if b_slots else None
    if mode == "nn":
        M, K = a.shape
        N = NDEV * slot_w if b_slots else b.shape[1]
    elif mode == "tn":
        K, M = a.shape
        N = b.shape[1]
    else:
        M, K = a.shape
        N = b.shape[-2]
        if b_slots:
            assert K == NDEV * slot_w and tk % slot_w == 0
    if mode == "nn" and b_slots:
        assert tn == slot_w
    if out_slots == "cols":
        assert tn * NDEV == N
    if out_slots == "rows":
        assert (M // NDEV) % tm == 0
    assert M % tm == 0 and N % tn == 0 and K % tk == 0, (name, M, N, K, tm, tn, tk)
    nk = K // tk
    dims = {"nn": ((1,), (0,)), "tn": ((0,), (0,)), "nt": ((1,), (1,))}[mode]

    if mode == "tn":
        a_spec = pl.BlockSpec((tk, tm), lambda i, j, k: (k, i))
    else:
        a_spec = pl.BlockSpec((tm, tk), lambda i, j, k: (i, k))
    if b_pair:
        b_spec = pl.BlockSpec((tn, tk // 2), lambda i, j, k: (j, 0))
    elif mode == "nt":
        b_spec = (pl.BlockSpec((tk // slot_w, tn, slot_w), lambda i, j, k: (k, j, 0)) if b_slots
                  else pl.BlockSpec((tn, tk), lambda i, j, k: (j, k)))
    else:
        b_spec = (pl.BlockSpec((None, tk, tn), lambda i, j, k: (j, k, 0)) if b_slots
                  else pl.BlockSpec((tk, tn), lambda i, j, k: (k, j)))
    tile = pl.BlockSpec((tm, tn), lambda i, j, k: (i, j + col_block))
    if out_slots == "cols":
        out_spec = pl.BlockSpec((None, None, tm, tn), lambda i, j, k: (j % 2, j // 2, i, 0))
        out_shape = _sds((2, 4, M, tn), out_dtype)
    elif out_slots == "rows":
        per = M // NDEV // tm
        out_spec = pl.BlockSpec((None, None, tm, tn), lambda i, j, k: ((i // per) % 2, (i // per) // 2, i % per, j))
        out_shape = _sds((2, 4, M // NDEV, N), out_dtype)
    else:
        out_spec, out_shape = tile, _sds((M, n_total or N), out_dtype)
    ins, in_specs = [a, b], [a_spec, b_spec]
    if b_pair:
        ins.append(b_pair[1])
        in_specs.append(b_spec)
    n_b = len(ins) - 1
    if epi in ("add", "dz"):
        ins.append(extra)
        in_specs.append(tile)
    aliases = {}
    if into is not None:
        aliases[len(ins)] = 0
        ins.append(into)
        in_specs.append(ANY_SPEC)
    if epi == "relu2":
        out_specs, out_shapes = [tile, tile], [_sds((M, N), F32), _sds((M, N), BF16)]
    else:
        out_specs, out_shapes = out_spec, out_shape
    n_in = len(ins)

    def body(*refs):
        outs = refs[n_in:-1] if nk > 1 else refs[n_in:]
        extra_ref = refs[1 + n_b]

        def finish(p):
            if epi is None:
                outs[0][...] = p.astype(out_dtype)
            elif epi == "add":
                outs[0][...] = (p + extra_ref[...]).astype(out_dtype)
            elif epi == "relu2":
                outs[0][...] = p
                rz = jnp.maximum(p, 0.0)
                outs[1][...] = (rz * rz).astype(BF16)
            else:
                outs[0][...] = (p * (2.0 * jnp.maximum(extra_ref[...], 0.0))).astype(out_dtype)

        def product():
            av = refs[0][...].astype(BF16)
            if b_pair:
                half = tk // 2
                return (lax.dot_general(av[:, :half], refs[1][...], (dims, ((), ())), preferred_element_type=F32)
                        + lax.dot_general(av[:, half:], refs[2][...], (dims, ((), ())), preferred_element_type=F32))
            if mode == "nt" and b_slots:
                return sum(lax.dot_general(av[:, s * slot_w:(s + 1) * slot_w], refs[1][s], (dims, ((), ())),
                                           preferred_element_type=F32) for s in range(tk // slot_w))
            return lax.dot_general(av, refs[1][...].astype(BF16), (dims, ((), ())), preferred_element_type=F32)

        if nk == 1:
            finish(product())
            return
        acc = refs[-1]
        k = pl.program_id(2)

        @pl.when(k == 0)
        def _():
            acc[...] = jnp.zeros_like(acc)

        acc[...] += product()

        @pl.when(k == nk - 1)
        def _():
            finish(acc[...])

    body, dep_ins, dep_specs = _after(body, n_in, deps)
    return pl.pallas_call(
        body, name=name, grid=(M // tm, N // tn, nk),
        in_specs=in_specs + dep_specs, out_specs=out_specs, out_shape=out_shapes, input_output_aliases=aliases,
        scratch_shapes=[pltpu.VMEM((tm, tn), F32)] if nk > 1 else [],
        compiler_params=_params(("parallel", "parallel", "arbitrary")),
    )(*ins, *dep_ins)


ROWS = 352


def _rmsnorm_fwd(h, w, name, deps=()):
    def body(h_ref, w_ref, u_ref):
        x = h_ref[...]
        rstd = lax.rsqrt(jnp.mean(x * x, axis=-1, keepdims=True) + EPS)
        u_ref[...] = (x * rstd * w_ref[...]).astype(BF16)

    row = pl.BlockSpec((ROWS, D), lambda i: (i, 0))
    body, dep_ins, dep_specs = _after(body, 2, deps)
    return pl.pallas_call(
        body, name=name, grid=(LP // ROWS,), in_specs=[row, pl.BlockSpec((1, D), lambda i: (0, 0))] + dep_specs,
        out_specs=row, out_shape=_sds((LP, D), BF16), compiler_params=_params(("parallel",)),
    )(h, w, *dep_ins)


TOKEN_ROWS = 512


def _rmsnorm_bwd_input(h, w, du, dres, name, deps=()):
    def math(h_ref, w_ref, du_ref, dres_ref):
        x = h_ref[...]
        rstd = lax.rsqrt(jnp.mean(x * x, axis=-1, keepdims=True) + EPS)
        xhat = x * rstd
        dy = du_ref[...]
        dxh = dy * w_ref[...]
        dh = dres_ref[...] + rstd * (dxh - xhat * jnp.mean(dxh * xhat, axis=-1, keepdims=True))
        return dh, jnp.sum(dy * xhat, axis=0, keepdims=True)

    def body(h_ref, w_ref, du_ref, dres_ref, hh_ref, duh_ref, dresh_ref, dx_ref, dhead_ref, gw_ref):
        dx_ref[...], part = math(h_ref, w_ref, du_ref, dres_ref)

        @pl.when(pl.program_id(0) == 0)
        def _():
            dhead_ref[...], head = math(hh_ref, w_ref, duh_ref, dresh_ref)
            gw_ref[...] = part + head

        @pl.when(pl.program_id(0) > 0)
        def _():
            gw_ref[...] += part

    rows = pl.BlockSpec((pl.Element(TOKEN_ROWS), pl.Element(D)),
                        lambda i: (pl.multiple_of(ROW_X + TOKEN_ROWS * i, 8), 0))
    head = pl.BlockSpec((ROW_X, D), lambda i: (0, 0))
    vec = pl.BlockSpec((1, D), lambda i: (0, 0))
    body, dep_ins, dep_specs = _after(body, 7, deps)
    return pl.pallas_call(
        body, name=name, grid=(SEQ // TOKEN_ROWS,),
        in_specs=[rows, vec, rows, rows, head, head, head] + dep_specs,
        out_specs=[pl.BlockSpec((TOKEN_ROWS, D), lambda i: (i, 0)), head, vec],
        out_shape=[_sds((SEQ, D), F32), _sds((ROW_X, D), F32), _sds((1, D), F32)],
        compiler_params=_params(("arbitrary",)),
    )(h, w, du, dres, h, du, dres, *dep_ins)


def _rmsnorm_bwd(h, w, du, dres, name, deps=()):
    def body(h_ref, w_ref, du_ref, dres_ref, dh_ref, dhb_ref, gw_ref):
        x = h_ref[...]
        rstd = lax.rsqrt(jnp.mean(x * x, axis=-1, keepdims=True) + EPS)
        xhat = x * rstd
        dy = du_ref[...]
        dxh = dy * w_ref[...]
        dh = dres_ref[...] + rstd * (dxh - xhat * jnp.mean(dxh * xhat, axis=-1, keepdims=True))
        dh_ref[...] = dh
        dhb_ref[...] = dh.astype(BF16)
        part = jnp.sum(dy * xhat, axis=0, keepdims=True)

        @pl.when(pl.program_id(0) == 0)
        def _():
            gw_ref[...] = part

        @pl.when(pl.program_id(0) > 0)
        def _():
            gw_ref[...] += part

    row = pl.BlockSpec((ROWS, D), lambda i: (i, 0))
    vec = pl.BlockSpec((1, D), lambda i: (0, 0))
    body, dep_ins, dep_specs = _after(body, 4, deps)
    return pl.pallas_call(
        body, name=name, grid=(LP // ROWS,), in_specs=[row, vec, row, row] + dep_specs, out_specs=[row, row, vec],
        out_shape=[_sds((LP, D), F32), _sds((LP, D), BF16), _sds((1, D), F32)],
        compiler_params=_params(("arbitrary",)),
    )(h, w, du, dres, *dep_ins)


def _loss_head(h2, wf, target):
    def body(h_ref, w_ref, t_ref, dh_ref, dhb_ref, sq_ref, gw_ref):
        i = pl.program_id(0)

        @pl.when(i == 0)
        def _():
            sq_ref[...] = jnp.zeros_like(sq_ref)
            gw_ref[...] = jnp.zeros_like(gw_ref)

        def rows(t, live):
            x = h_ref[...]
            rstd = lax.rsqrt(jnp.mean(x * x, axis=-1, keepdims=True) + EPS)
            xhat = x * rstd
            w = w_ref[...]
            err = xhat * w - t
            if live is not None:
                err = jnp.where(live, err, 0.0)
            sq_ref[...] += jnp.sum(err * err, axis=0, keepdims=True)
            dy = err * (1.0 / D)
            gw_ref[...] += jnp.sum(dy * xhat, axis=0, keepdims=True)
            dxh = dy * w
            dh = rstd * (dxh - xhat * jnp.mean(dxh * xhat, axis=-1, keepdims=True))
            dh_ref[...] = dh
            dhb_ref[...] = dh.astype(BF16)

        @pl.when(i == 0)
        def _():
            rid = lax.broadcasted_iota(jnp.int32, (ROWS, D), 0)
            rows(pltpu.roll(t_ref[...], ROW_X, 0), rid >= ROW_X)

        @pl.when(i > 0)
        def _():
            rows(t_ref[...], None)

    row = pl.BlockSpec((ROWS, D), lambda i: (i, 0))
    vec = pl.BlockSpec((1, D), lambda i: (0, 0))
    tgt = pl.BlockSpec((pl.Element(ROWS), pl.Element(D)),
                       lambda i: (pl.multiple_of(jnp.maximum(ROWS * i - ROW_X, 0), 8), 0))
    return pl.pallas_call(
        body, name="loss_head", grid=(LP // ROWS,),
        in_specs=[row, vec, tgt],
        out_specs=[row, row, vec, vec],
        out_shape=[_sds((LP, D), F32), _sds((LP, D), BF16), _sds((1, D), F32), _sds((1, D), F32)],
        compiler_params=_params(("arbitrary",)),
    )(h2, wf, target)


def _dot(a, b, dims):
    return lax.dot_general(a, b, (dims, ((), ())), preferred_element_type=F32)


NN, TN, NT = ((1,), (0,)), ((0,), (0,)), ((1,), (1,))


def _tri_sum(t, x):
    hi = x.astype(BF16)
    r1 = x - hi.astype(F32)
    mid = r1.astype(BF16)
    lo = (r1 - mid.astype(F32)).astype(BF16)
    return _dot(t, hi, NN) + _dot(t, mid, NN) + _dot(t, lo, NN)


def _gla_gates(glr_ref, gw2_ref, gb_ref, rows, row0):
    g_raw = _dot(glr_ref[rows, :].astype(BF16), gw2_ref[...], NN) + gb_ref[...]
    logsig = jnp.minimum(g_raw, 0.0) - jnp.log(1.0 + jnp.exp(-jnp.abs(g_raw)))
    rid = row0 + lax.broadcasted_iota(jnp.int32, g_raw.shape, 0)
    live = rid >= PAD
    return g_raw, jnp.where(live, logsig / TAU, 0.0), live


def _tri_masks():
    r = lax.broadcasted_iota(jnp.int32, (CHUNK, CHUNK), 0)
    c = lax.broadcasted_iota(jnp.int32, (CHUNK, CHUNK), 1)
    return r >= c


def _gla_specs(rev):
    n = NCH // CPS
    R = CPS * CHUNK
    st = (lambda s: n - 1 - s) if rev else (lambda s: s)
    return R, n, st, [
        pl.BlockSpec((R, KW), lambda s: (st(s), 0)),
        pl.BlockSpec((R, KW), lambda s: (st(s), 1)),
        pl.BlockSpec((R, GW), lambda s: (st(s), 1)),
        pl.BlockSpec((R, GW), lambda s: (st(s), 2)),
        pl.BlockSpec((R, 128), lambda s: (st(s), GLR_BLK)),
    ]


def _gla_fwd(proj, gw2p, gate_b, gnw, deps=()):
    R, n, st, pspecs = _gla_specs(False)

    def body(q_ref, k_ref, v_ref, r_ref, glr_ref, gw2_ref, gb_ref, gnw_ref, og_ref, o_ref, st_ref, state):
        s = pl.program_id(0)

        @pl.when(s == 0)
        def _():
            state[...] = jnp.zeros_like(state)

        causal = _tri_masks()
        tri = causal.astype(BF16)
        for c in range(CPS):
            rows = slice(c * CHUNK, (c + 1) * CHUNK)
            _, logg, _ = _gla_gates(glr_ref, gw2_ref, gb_ref, rows, s * R + c * CHUNK)
            G = _tri_sum(tri, logg)
            g_last = G[CHUNK - 1:CHUNK, :]
            q_dec = (q_ref[rows, :] * (DK ** -0.5) * jnp.exp(G)).astype(BF16)
            kk = k_ref[rows, :]
            k_inv = (kk * jnp.exp(-G)).astype(BF16)
            k_end = (kk * jnp.exp(g_last - G)).astype(BF16)
            decay = jnp.exp(g_last)
            for h in range(H):
                lk = slice(h * DK, (h + 1) * DK)
                lv = slice(h * DV, (h + 1) * DV)
                v = v_ref[rows, lv].astype(BF16)
                S = state[h]
                st_ref[c, h] = S
                A = jnp.where(causal, _dot(q_dec[:, lk], k_inv[:, lk], NT), 0.0).astype(BF16)
                o = _dot(A, v, NN) + _dot(q_dec[:, lk], S.astype(BF16), NT)
                state[h] = decay[:, lk] * S + _dot(v, k_end[:, lk], TN)
                o_ref[rows, lv] = o
                on = o * lax.rsqrt(jnp.mean(o * o, axis=-1, keepdims=True) + EPS) * gnw_ref[...]
                rr = r_ref[rows, lv]
                og_ref[rows, lv] = (on * (rr * jax.nn.sigmoid(rr))).astype(BF16)

    full = lambda shape: pl.BlockSpec(shape, lambda s: (0,) * len(shape))
    body, dep_ins, dep_specs = _after(body, 8, deps)
    return pl.pallas_call(
        body, name="gla_fwd", grid=(n,),
        in_specs=pspecs + [full((128, KW)), full((1, KW)), full((1, DV))] + dep_specs,
        out_specs=[pl.BlockSpec((R, GW), lambda s: (s, 0)), pl.BlockSpec((R, GW), lambda s: (s, 0)),
                   pl.BlockSpec((CPS, H, DV, DK), lambda s: (s, 0, 0, 0))],
        out_shape=[_sds((LP, GW + PW), BF16), _sds((LP, GW), F32), _sds((NCH, H, DV, DK), F32)],
        scratch_shapes=[pltpu.VMEM((H, DV, DK), F32)],
        compiler_params=_params(("arbitrary",)),
    )(proj, proj, proj, proj, proj, gw2p, gate_b, gnw, *dep_ins)


def _gla_bwd(proj, dmixed, o_saved, st_saved, gw2p, gate_b, gnw, deps=()):
    R, n, st, pspecs = _gla_specs(True)

    def body(q_ref, k_ref, v_ref, r_ref, glr_ref, dog_ref, o_ref, st_ref, gw2_ref, gb_ref, gnw_ref,
             dqkvr_ref, dglr_ref, ggn_ref, ggb_ref, ggw_ref, gstate):
        s = pl.program_id(0)

        @pl.when(s == 0)
        def _():
            gstate[...] = jnp.zeros_like(gstate)
            ggn_ref[...] = jnp.zeros_like(ggn_ref)
            ggb_ref[...] = jnp.zeros_like(ggb_ref)
            ggw_ref[...] = jnp.zeros_like(ggw_ref)

        causal = _tri_masks()
        tri = causal.astype(BF16)
        tri_up = (lax.broadcasted_iota(jnp.int32, (CHUNK, CHUNK), 0)
                  <= lax.broadcasted_iota(jnp.int32, (CHUNK, CHUNK), 1)).astype(BF16)
        gnw = gnw_ref[...]
        for c in reversed(range(CPS)):
            rows = slice(c * CHUNK, (c + 1) * CHUNK)
            g_raw, logg, live = _gla_gates(glr_ref, gw2_ref, gb_ref, rows, (n - 1 - s) * R + c * CHUNK)
            G = _tri_sum(tri, logg)
            g_last = G[CHUNK - 1:CHUNK, :]
            e_g, e_gi, e_end = jnp.exp(G), jnp.exp(-G), jnp.exp(g_last - G)
            q_dec = q_ref[rows, :] * (DK ** -0.5) * e_g
            kk = k_ref[rows, :]
            k_inv, k_end = kk * e_gi, kk * e_end
            q_dec_b, k_inv_b, k_end_b = q_dec.astype(BF16), k_inv.astype(BF16), k_end.astype(BF16)
            decay = jnp.exp(g_last)
            d_g, d_gl = [], []
            for h in range(H):
                lk = slice(h * DK, (h + 1) * DK)
                lv = slice(h * DV, (h + 1) * DV)
                o = o_ref[rows, lv]
                rr = r_ref[rows, lv]
                dog = dog_ref[rows, lv]
                rstd = lax.rsqrt(jnp.mean(o * o, axis=-1, keepdims=True) + EPS)
                ohat = o * rstd
                sr = jax.nn.sigmoid(rr)
                don = dog * (rr * sr)
                dqkvr_ref[rows, 2 * KW + GW + h * DV:2 * KW + GW + (h + 1) * DV] = (
                    dog * (ohat * gnw) * (sr * (1.0 + rr * (1.0 - sr)))).astype(BF16)
                ggn_ref[...] += jnp.sum(don * ohat, axis=0, keepdims=True)
                dohat = don * gnw
                do = (rstd * (dohat - ohat * jnp.mean(dohat * ohat, axis=-1, keepdims=True))).astype(BF16)
                v = v_ref[rows, lv].astype(BF16)
                S = st_ref[c, h]
                gS = gstate[h]
                S_b, gS_b = S.astype(BF16), gS.astype(BF16)
                qd, ki, ke = q_dec_b[:, lk], k_inv_b[:, lk], k_end_b[:, lk]
                A = jnp.where(causal, _dot(qd, ki, NT), 0.0).astype(BF16)
                dA = jnp.where(causal, _dot(do, v, NT), 0.0).astype(BF16)
                dv = _dot(A, do, TN) + _dot(ke, gS_b, NT)
                dq_dec = _dot(dA, ki, NN) + _dot(do, S_b, NN)
                dk_inv = _dot(dA, qd, TN)
                dk_end = _dot(v, gS_b, NN)
                d_decay = jnp.sum(gS * S, axis=0, keepdims=True)
                gstate[h] = decay[:, lk] * gS + _dot(do, qd, TN)
                dqkvr_ref[rows, lk] = (dq_dec * e_g[:, lk] * (DK ** -0.5)).astype(BF16)
                dqkvr_ref[rows, KW + h * DK:KW + (h + 1) * DK] = (
                    dk_inv * e_gi[:, lk] + dk_end * e_end[:, lk]).astype(BF16)
                dqkvr_ref[rows, 2 * KW + h * DV:2 * KW + (h + 1) * DV] = dv.astype(BF16)
                ke_prod = dk_end * k_end[:, lk]
                d_g.append(dq_dec * q_dec[:, lk] - dk_inv * k_inv[:, lk] - ke_prod)
                d_gl.append(jnp.sum(ke_prod, axis=0, keepdims=True) + d_decay * decay[:, lk])
            dlogg = _tri_sum(tri_up, jnp.concatenate(d_g, axis=1)) + jnp.concatenate(d_gl, axis=1)
            dg_raw = jnp.where(live, dlogg * (1.0 / TAU) * jax.nn.sigmoid(-g_raw), 0.0)
            ggb_ref[...] += jnp.sum(dg_raw, axis=0, keepdims=True)
            dg_b = dg_raw.astype(BF16)
            ggw_ref[...] += _dot(glr_ref[rows, :].astype(BF16), dg_b, TN)
            dglr_ref[rows, :] = _dot(dg_b, gw2_ref[...], NT).astype(BF16)

    full = lambda shape: pl.BlockSpec(shape, lambda s: (0,) * len(shape))
    body, dep_ins, dep_specs = _after(body, 11, deps)
    return pl.pallas_call(
        body, name="gla_bwd", grid=(n,),
        in_specs=pspecs + [pl.BlockSpec((R, GW), lambda s: (st(s), 0)), pl.BlockSpec((R, GW), lambda s: (st(s), 0)),
                           pl.BlockSpec((CPS, H, DV, DK), lambda s: (st(s), 0, 0, 0)),
                           full((128, KW)), full((1, KW)), full((1, DV))] + dep_specs,
        out_specs=[pl.BlockSpec((R, 2 * KW + 2 * GW), lambda s: (st(s), 0)), pl.BlockSpec((R, 128), lambda s: (st(s), 0)),
                   full((1, DV)), full((1, KW)), full((128, KW))],
        out_shape=[_sds((LP, D_INP), BF16), _sds((LP, 128), BF16),
                   _sds((1, DV), F32), _sds((1, KW), F32), _sds((128, KW), F32)],
        scratch_shapes=[pltpu.VMEM((H, DV, DK), F32)],
        compiler_params=_params(("arbitrary",)),
    )(proj, proj, proj, proj, proj, dmixed, o_saved, st_saved, gw2p, gate_b, gnw, *dep_ins)


def _pool_pre(x, win, rid):
    s, step = x, 1
    while step < win:
        s = s + pltpu.roll(s, step, 0)
        step *= 2
    cnt = jnp.clip(rid - (PAD - 1), 1, win).astype(F32)
    live = rid >= PAD
    return jnp.where(live, s / cnt - x, 0.0), cnt, live


def _pool_fwd(proj, pool_w, pool_scale, mixed):
    def body(pu_ref, w_ref, sc_ref, _, o_ref):
        rid = lax.broadcasted_iota(jnp.int32, (LP, GC), 0)
        for g, win in enumerate(WINDOWS):
            @pl.when(pl.program_id(0) == g)
            def _():
                y, _, _ = _pool_pre(pu_ref[...], win, rid)
                o_ref[...] = (_dot(y.astype(BF16), w_ref[...], NN) * sc_ref[...]).astype(BF16)

    col = lambda base: pl.BlockSpec((LP, GC), lambda g: (0, base + g))
    return pl.pallas_call(
        body, name="pool_fwd", grid=(len(WINDOWS),),
        in_specs=[col(POOL_BLK), pl.BlockSpec((None, GC, GC), lambda g: (g, 0, 0)),
                  pl.BlockSpec((1, GC), lambda g: (0, g)), ANY_SPEC],
        out_specs=col(GW // GC), out_shape=_sds(mixed.shape, BF16), input_output_aliases={3: 0},
        compiler_params=_params(("parallel",)),
    )(proj, pool_w, pool_scale, mixed)


def _pool_bwd(proj, dmixed, pool_w, pool_scale, dproj):
    def body(pu_ref, do_ref, w_ref, sc_ref, _, dpu_ref, dw_ref, dsc_ref):
        rid = lax.broadcasted_iota(jnp.int32, (LP, GC), 0)
        for g, win in enumerate(WINDOWS):
            @pl.when(pl.program_id(0) == g)
            def _():
                y, cnt, live = _pool_pre(pu_ref[...], win, rid)
                y_b = y.astype(BF16)
                w = w_ref[...]
                do = do_ref[...]
                dsc_ref[...] = jnp.sum(do * _dot(y_b, w, NN), axis=0, keepdims=True)
                dyw = (do * sc_ref[...]).astype(BF16)
                dw_ref[...] = _dot(y_b, dyw, TN)
                dy = jnp.where(live, _dot(dyw, w, NT), 0.0)
                s, step = dy / cnt, 1
                while step < win:
                    s = s + pltpu.roll(s, LP - step, 0)
                    step *= 2
                dpu_ref[...] = (s - dy).astype(BF16)

    col = lambda base: pl.BlockSpec((LP, GC), lambda g: (0, base + g))
    mat = pl.BlockSpec((None, GC, GC), lambda g: (g, 0, 0))
    vec = pl.BlockSpec((1, GC), lambda g: (0, g))
    return pl.pallas_call(
        body, name="pool_bwd", grid=(len(WINDOWS),),
        in_specs=[col(POOL_BLK), col(GW // GC), mat, vec, ANY_SPEC], out_specs=[col(POOL_BLK), mat, vec],
        out_shape=[_sds(dproj.shape, BF16), _sds((4, GC, GC), F32), _sds((1, PW), F32)],
        input_output_aliases={4: 0}, compiler_params=_params(("parallel",)),
    )(proj, dmixed, pool_w, pool_scale, dproj)


def _adamw_math(w, g, m, v):
    m = B1 * m + (1.0 - B1) * g
    v = B2 * v + (1.0 - B2) * (g * g)
    m_hat = m * (1.0 / (1.0 - B1 ** STEP))
    v_hat = v * (1.0 / (1.0 - B2 ** STEP))
    return -LR * (m_hat / (jnp.sqrt(v_hat) + AEPS) + WD * w), m, v


def _adamw_landed(sums, landed, my_chip, w, m, v, rows, name, cols=None):
    _, r, c = w.shape

    def body(chip_ref, s_ref, l_ref, w_ref, m_ref, v_ref, g_ref, d_ref, mo_ref, vo_ref):
        g = s_ref[...].astype(F32)
        for k in range(3):
            g = g + l_ref[k].astype(F32)
        g_ref[...] = g
        d_ref[...], mo_ref[...], vo_ref[...] = _adamw_math(w_ref[...], g, m_ref[...], v_ref[...])

    cols = cols or c
    blk = pl.BlockSpec((None, rows, cols), lambda i, j, chip_ref: (0, i, j))
    return pl.pallas_call(
        body, name=name, out_shape=[_sds((1, r, c), F32)] * 4,
        grid_spec=pltpu.PrefetchScalarGridSpec(
            num_scalar_prefetch=1, grid=(r // rows, c // cols),
            in_specs=[pl.BlockSpec((None, rows, cols), lambda i, j, chip_ref: (chip_ref[0], i, j)),
                      pl.BlockSpec((3, rows, cols), lambda i, j, chip_ref: (0, i, j)), blk, blk, blk],
            out_specs=[blk] * 4),
        compiler_params=_params(("parallel", "parallel")),
    )(my_chip, sums, landed, w, m, v)


def _adamw_small(g, w, m, v):
    def body(g_ref, w_ref, m_ref, v_ref, d_ref, mo_ref, vo_ref):
        d_ref[...], mo_ref[...], vo_ref[...] = _adamw_math(w_ref[...], g_ref[...], m_ref[...], v_ref[...])

    return pl.pallas_call(body, name="adamw_small", out_shape=[_sds(w.shape, F32)] * 3)(g, w, m, v)


SMALL_REPL = (("norm1_w", D), ("norm2_w", D), ("final_norm_w", D), ("pool_scale", PW), ("gate_b", KW),
              ("gla_norm_w", DV))


def _pack_rows(vecs, rows):
    flat = jnp.concatenate([jnp.ravel(v) for v in vecs])
    return jnp.pad(flat, (0, rows * 1024 - flat.shape[0])).reshape(rows, 1024)


def kernel(x, meta_tokens, norm1_w, w_in, gate_w2, gate_b, gla_norm_w, pool_w, pool_scale, w_out, norm2_w, mlp_w1, mlp_w2, final_norm_w, loss_target, m_meta_tokens, m_norm1_w, m_w_in, m_gate_w2, m_gate_b, m_gla_norm_w, m_pool_w, m_pool_scale, m_w_out, m_norm2_w, m_mlp_w1, m_mlp_w2, m_final_norm_w, v_meta_tokens, v_norm1_w, v_w_in, v_gate_w2, v_gate_b, v_gla_norm_w, v_pool_w, v_pool_scale, v_w_out, v_norm2_w, v_mlp_w1, v_mlp_w2, v_final_norm_w):
    me = 4 * lax.axis_index("x") + 2 * lax.axis_index("y") + lax.axis_index("c")
    W = dict(meta_tokens=meta_tokens, norm1_w=norm1_w, w_in=w_in, gate_w2=gate_w2, gate_b=gate_b,
             gla_norm_w=gla_norm_w, pool_w=pool_w, pool_scale=pool_scale, w_out=w_out, norm2_w=norm2_w,
             mlp_w1=mlp_w1, mlp_w2=mlp_w2, final_norm_w=final_norm_w)
    Mo = dict(meta_tokens=m_meta_tokens, norm1_w=m_norm1_w, w_in=m_w_in, gate_w2=m_gate_w2, gate_b=m_gate_b,
              gla_norm_w=m_gla_norm_w, pool_w=m_pool_w, pool_scale=m_pool_scale, w_out=m_w_out, norm2_w=m_norm2_w,
              mlp_w1=m_mlp_w1, mlp_w2=m_mlp_w2, final_norm_w=m_final_norm_w)
    Vo = dict(meta_tokens=v_meta_tokens, norm1_w=v_norm1_w, w_in=v_w_in, gate_w2=v_gate_w2, gate_b=v_gate_b,
              gla_norm_w=v_gla_norm_w, pool_w=v_pool_w, pool_scale=v_pool_scale, w_out=v_w_out, norm2_w=v_norm2_w,
              mlp_w1=v_mlp_w1, mlp_w2=v_mlp_w2, final_norm_w=v_final_norm_w)

    ex = _Exchange(dict(small=_pack_rows([meta_tokens, gate_w2[0]], 8), w_in=w_in[0].T.astype(BF16)),
                   dict(w_out=w_out[0], pool_w=pool_w[0].reshape(4 * 32, GC), mlp_w1=mlp_w1[0],
                        mlp_w2a=mlp_w2[0][:, :D // 2], mlp_w2b=mlp_w2[0][:, D // 2:]))
    step = _layer_step(x[0], loss_target[0], ex, norm1_w, gate_b, gla_norm_w, pool_scale, norm2_w,
                       final_norm_w.reshape(1, D))
    grad_x = step["dx"][None]

    last = step["dx"]
    out = {}
    for group in ("down", "up", "mix"):
        for k, (sums, landed) in ex.grad_finish(group, last).items():
            out[k] = _adamw_landed(sums, landed, ex.my_chip, W[k], Mo[k], Vo[k], SHARD_ROWS[k], "adamw_" + k)
            last = out[k][1]

    loss_part = 0.5 * jnp.sum(step["sq"]) / D
    packed = jnp.concatenate([
        _pack_rows([step[k] for k, _ in SMALL_REPL] + [loss_part], 8),
        step["gate_w2"][:RANK].reshape(8, 1024), step["dhead"][PAD:].reshape(32, 1024)], axis=0)
    red = _exchange_small(packed, True, "reduce_small", deps=[last])
    loss = red[7, 768]
    g_gw2_mine = lax.dynamic_slice(red[8:16].reshape(RANK, KW), (0, me * (KW // NDEV)), (RANK, KW // NDEV))
    g_meta_mine = lax.dynamic_slice(red[16:48].reshape(N_META, D), (0, me * (D // NDEV)), (N_META, D // NDEV))

    done = ex.grad_finish("w_in", red)
    poolw3 = lambda a: a.reshape(1, 4 * 32, GC)
    res = _adamw_landed(*done["pool_w"], ex.my_chip, poolw3(pool_w), poolw3(m_pool_w), poolw3(v_pool_w),
                        SHARD_ROWS["pool_w"], "adamw_pool_w")
    out["pool_w"] = [a.reshape(pool_w.shape) for a in res]
    tr = lambda a: a[0].T[None]
    res = _adamw_landed(*done["w_in"], ex.my_chip, tr(w_in), tr(m_w_in), tr(v_w_in), D_IN // NDEV, "adamw_w_in",
                        cols=256)
    out["w_in"] = [a[0].T[None] for a in res]

    def small_pack(P):
        return jnp.concatenate([_pack_rows([P[k] for k, _ in SMALL_REPL], 8),
                                _pack_rows([P["meta_tokens"], P["gate_w2"]], 8)], axis=0)

    g_small = jnp.concatenate([red[0:8], _pack_rows([g_meta_mine, g_gw2_mine], 8)], axis=0)
    g_small = g_small.at[7, 768].set(0.0)
    res_small = _adamw_small(g_small, small_pack(W), small_pack(Mo), small_pack(Vo))
    res_small = [g_small] + list(res_small)
    off = 0
    for k, nel in SMALL_REPL:
        out[k] = [a[0:8].reshape(-1)[off:off + nel].reshape(W[k].shape) for a in res_small]
        off += nel
    out["meta_tokens"] = [a[8:12].reshape(N_META, D // NDEV) for a in res_small]
    out["gate_w2"] = [a[12].reshape(1, RANK, KW // NDEV) for a in res_small]

    order = ["meta_tokens", "norm1_w", "w_in", "gate_w2", "gate_b", "gla_norm_w", "pool_w", "pool_scale", "w_out",
             "norm2_w", "mlp_w1", "mlp_w2", "final_norm_w"]
    return (loss, grad_x, *[out[k][0] for k in order], *[out[k][1] for k in order],
            *[out[k][2] for k in order], *[out[k][3] for k in order])


SHARD_ROWS = dict(w_out=256, mlp_w1=512, mlp_w2=256, pool_w=128)
C_GLR = 2 * KW + 2 * GW
GATHER_GROUPS = dict(w_in=("small", "w_in"), mix=("w_out", "pool_w"), up=("mlp_w1",), down_a=("mlp_w2a",),
                     down_b=("mlp_w2b",))
GRAD_GROUPS = dict(down=("mlp_w2",), up=("mlp_w1",), mix=("w_out",), w_in=("pool_w", "w_in"))


class _Exchange:
    def __init__(self, first, rest):
        head, token = _gather_start(list(first.values()), "gather_start_first", first["small"])
        token, later = lax.optimization_barrier((token, list(rest.values())))
        tail, self.started = _gather_start([v.astype(BF16) for v in later], "gather_start_rest", token)
        self.state = dict(zip(list(first) + list(rest), head + tail))
        self.my_c = lax.axis_index("c").astype(jnp.int32).reshape(1)
        self.my_chip = (2 * lax.axis_index("x") + lax.axis_index("y")).astype(jnp.int32).reshape(1)
        self.sibling, self.chips = {}, {}

    def forward(self, group, after):
        ks = GATHER_GROUPS[group]
        fwd, token = _gather_forward([self.state[k] for k in ks], after, "gather_forward_" + group)
        self.state.update(zip(ks, fwd))
        return token

    def weights(self, group, after):
        ks = GATHER_GROUPS[group]
        g = dict(zip(ks, _gather_finish([self.state[k] for k in ks], after, "gather_finish_" + group)))
        if group == "w_in":
            nat = g["w_in"].reshape(D_IN, D)
            return jnp.concatenate([nat[:C_GLR], nat[C_GLR + RANK:], nat[C_GLR:C_GLR + RANK],
                                    jnp.zeros((D_INP - D_IN, D), BF16)], axis=0), g["small"]
        if group == "mix":
            return (g["w_out"].reshape(D, D),
                    g["pool_w"].reshape(NDEV, 4, 32, GC).transpose(1, 0, 2, 3).reshape(4, GC, GC))
        return g["mlp_w1"] if group == "up" else g[ks[0]].reshape(DFF, D // 2)

    def grad(self, group, grads):
        parts = dict(grads)
        if group == "w_in":
            g = parts["w_in"]
            nat = jnp.concatenate([g[:C_GLR], g[C_GLR + PW:C_GLR + PW + RANK], g[C_GLR:C_GLR + PW]], axis=0)
            parts["w_in"] = nat.reshape(4, 2, D_IN // NDEV, D).transpose(1, 0, 2, 3)
            parts["pool_w"] = (parts["pool_w"].astype(BF16).reshape(4, 4, 2, 32, GC).transpose(2, 1, 0, 3, 4)
                               .reshape(2, 4, 4 * 32, GC))
        ks = GRAD_GROUPS[group]
        started, token = _to_sibling_start([parts[k] for k in ks], "grad_sibling_start_" + group)
        self.sibling[group] = started
        return token

    def grad_mid(self, group, after):
        ks = GRAD_GROUPS[group]
        both = _to_sibling_finish(self.sibling[group], after, "grad_sibling_finish_" + group)
        tile = lambda k, p: (p.shape[2], 512) if k == "w_in" else (SHARD_ROWS[k], p.shape[3])
        sums = [_chip_sum(p, s, self.my_c, tile(k, p), "chip_sum_" + k) for k, (p, s) in zip(ks, both)]
        self.chips[group], token = _to_chips_start(sums, "grad_chips_start_" + group)
        return token

    def grad_finish(self, group, after):
        done = _to_chips_finish(self.chips[group], after, "grad_chips_finish_" + group)
        return dict(zip(GRAD_GROUPS[group], done))


def _layer_step(x, target, ex, norm1_w, gate_b, gla_norm_w, pool_scale, norm2_w, final_norm_w):
    win_p, small = ex.weights("w_in", ex.forward("w_in", ex.started))
    meta_full = small[:, 0:4].reshape(NDEV, N_META, D // NDEV).transpose(1, 0, 2).reshape(N_META, D)
    gw2_full = small[:, 4].reshape(NDEV, RANK, KW // NDEV).transpose(1, 0, 2).reshape(RANK, KW)
    gw2p = jnp.pad(gw2_full, ((0, 128 - RANK), (0, 0))).astype(BF16)
    h0 = jnp.concatenate([jnp.zeros((PAD, D), F32), meta_full, x], axis=0)
    u1 = _rmsnorm_fwd(h0, norm1_w, "rmsnorm1")
    proj = _matmul(u1, win_p, mode="nt", tm=1056, tn=1408, tk=2048, name="proj")
    tok = ex.forward("mix", proj)
    mixed, o_saved, st_saved = _gla_fwd(proj, gw2p, gate_b, gla_norm_w, deps=[tok])
    wout_f, poolw_f = ex.weights("mix", mixed)
    mixed = _pool_fwd(proj, poolw_f, pool_scale, mixed)
    h1 = _matmul(mixed, wout_f, mode="nn", tm=1056, tn=1024, tk=2048, name="mix_out", epi="add", extra=h0)
    tok = ex.forward("up", h1)
    u2 = _rmsnorm_fwd(h1, norm2_w, "rmsnorm2", deps=[tok])
    w1_g = ex.weights("up", u2)
    z, act = _matmul(u2, w1_g, mode="nn", tm=1056, tn=1024, tk=2048, name="mlp_up", epi="relu2", b_slots=True)
    w2a = ex.weights("down_a", ex.forward("down_a", act))
    h2 = _matmul(act, w2a, mode="nn", tm=1056, tn=1024, tk=2048, name="mlp_down_a", epi="add", extra=h1, n_total=D)
    w2b = ex.weights("down_b", ex.forward("down_b", h2))
    h2 = _matmul(act, w2b, mode="nn", tm=1056, tn=1024, tk=2048, name="mlp_down_b", epi="add", extra=h1, n_total=D,
                 col_block=1, into=h2)
    dh2, dh2b, sq, g_fnw = _loss_head(h2, final_norm_w, target)

    g_w2 = _matmul(act, dh2b, mode="tn", tm=512, tn=2048, tk=LP, name="d_mlp_w2", out_dtype=BF16, out_slots="rows")
    tok = ex.grad("down", dict(mlp_w2=g_w2))
    dz = _matmul(dh2b, (w2a, w2b), mode="nt", tm=1056, tn=1024, tk=2048, name="d_act", out_dtype=BF16, epi="dz",
                 extra=z, deps=[tok])
    tok = ex.grad_mid("down", dz)
    g_w1 = _matmul(u2, dz, mode="tn", tm=1024, tn=1024, tk=LP, name="d_mlp_w1", out_dtype=BF16, out_slots="cols",
                   deps=[tok])
    tok = ex.grad("up", dict(mlp_w1=g_w1))
    du2 = _matmul(dz, w1_g, mode="nt", tm=1056, tn=1024, tk=2048, name="d_u2", b_slots=True, deps=[tok])
    tok = ex.grad_mid("up", du2)
    dh1, dh1b, g_n2 = _rmsnorm_bwd(h1, norm2_w, du2, dh2, "rmsnorm2_bwd", deps=[tok])
    g_wout = _matmul(mixed, dh1b, mode="tn", tm=256, tn=2048, tk=LP, name="d_w_out", out_dtype=BF16, out_slots="rows")
    tok = ex.grad("mix", dict(w_out=g_wout))
    dmixed = _matmul(dh1b, wout_f, mode="nt", tm=1056, tn=1024, tk=2048, name="d_mixed", deps=[tok])
    tok = ex.grad_mid("mix", dmixed)
    dproj, dglr, g_gnw, g_gb, g_gw2 = _gla_bwd(proj, dmixed, o_saved, st_saved, gw2p, gate_b, gla_norm_w, deps=[tok])
    dproj, g_poolw, g_psc = _pool_bwd(proj, dmixed, poolw_f, pool_scale, dproj)
    dproj = lax.dynamic_update_slice(dproj, dglr, (0, GLR_BLK * 128))
    g_win_p = _matmul(dproj, u1, mode="tn", tm=384, tn=2048, tk=LP, name="d_w_in", out_dtype=BF16)
    tok = ex.grad("w_in", dict(pool_w=g_poolw, w_in=g_win_p))
    du1 = _matmul(dproj, win_p, mode="nn", tm=1056, tn=1024, tk=1408, name="d_u1", deps=[tok])
    tok = ex.grad_mid("w_in", du1)
    dx, dhead, g_n1 = _rmsnorm_bwd_input(h0, norm1_w, du1, dh1, "rmsnorm1_bwd", deps=[tok])
    return dict(dx=dx, dhead=dhead, sq=sq, gate_w2=g_gw2, norm1_w=g_n1, norm2_w=g_n2, final_norm_w=g_fnw, pool_scale=g_psc,
                gate_b=g_gb, gla_norm_w=g_gnw)
```

```python
import functools

import jax
import jax.numpy as jnp
from jax import lax
from jax.experimental import pallas as pl
from jax.experimental.pallas import tpu as pltpu

F32, BF16 = jnp.float32, jnp.bfloat16
MESH = pl.DeviceIdType.MESH

NDEV = 8
D = 2048
SEQ = 2048
N_META = 16
CHUNK = 64
PAD = (-N_META) % CHUNK
ROW_X = PAD + N_META
LP = ROW_X + SEQ
NCH = LP // CHUNK
H = 4
DK = 128
DV = 256
KW = H * DK
GW = H * DV
PW = 1024
RANK = 16
TAU = 16.0
WINDOWS = (2, 4, 8, 16)
GC = 256
DFF = 4 * D
EPS = 1e-6
D_IN = 2 * KW + 2 * GW + RANK + PW
D_INP = 4224
GLR_BLK = (2 * KW + 2 * GW + PW) // 128
POOL_BLK = (2 * KW + 2 * GW) // GC
LR, B1, B2, AEPS, WD, STEP = 0.001, 0.9, 0.999, 1e-08, 0.01, 10
VMEM_LIMIT = 48 * 1024 * 1024
CPS = 3


def _params(sem=None):
    return pltpu.CompilerParams(dimension_semantics=sem, vmem_limit_bytes=VMEM_LIMIT)


def _sds(shape, dtype):
    return jax.ShapeDtypeStruct(shape, dtype)


def _me():
    return lax.axis_index("x"), lax.axis_index("y"), lax.axis_index("c")


def _peer(j):
    x, y, c = _me()
    return (x ^ ((j >> 2) & 1), y ^ ((j >> 1) & 1), c ^ (j & 1))


def _slot(dev):
    return 4 * dev[0] + 2 * dev[1] + dev[2]


HBM_SPEC = pl.BlockSpec(memory_space=pltpu.HBM)
SEM_SPEC = pl.BlockSpec(memory_space=pltpu.SEMAPHORE)
ANY_SPEC = pl.BlockSpec(memory_space=pl.ANY)
EFFECT = pltpu.SideEffectType.DATAFLOW_SIDE_EFFECTING
SIBLING = 1
OTHER_CHIPS = (2, 4, 6)


def _in_hbm(a):
    return pltpu.with_memory_space_constraint(a, pltpu.HBM)


def _chip(dev):
    return 2 * dev[0] + dev[1]


def _rcopy(src, dst, send_sem, recv_sem, to):
    return pltpu.make_async_remote_copy(src_ref=src, dst_ref=dst, send_sem=send_sem, recv_sem=recv_sem,
                                        device_id=to, device_id_type=MESH)


def _split_call(body, name, ins, in_specs, out_shape, out_specs, aliases, scratch=()):
    n = len(ins) + len(out_shape)

    def with_token(*refs):
        body(*refs[:n], *refs[n + 1:])
        refs[n][...] = jnp.zeros_like(refs[n])

    return pl.pallas_call(
        with_token, name=name, in_specs=in_specs, out_shape=list(out_shape) + [_sds((8, 128), F32)],
        out_specs=list(out_specs) + [pl.BlockSpec(memory_space=pltpu.VMEM)],
        input_output_aliases=aliases, scratch_shapes=list(scratch),
        compiler_params=pltpu.CompilerParams(has_side_effects=EFFECT),
    )(*ins)


def _after(body, n_in, deps):
    deps = [d for d in deps if d is not None]
    if not deps:
        return body, [], []
    return (lambda *refs: body(*refs[:n_in], *refs[n_in + len(deps):])), deps, [ANY_SPEC] * len(deps)


def _gather_start(shards, name, after):
    n = len(shards)
    me = _slot(_me())
    lands = [lax.dynamic_update_slice(lax.empty((NDEV,) + s.shape, s.dtype), s[None], (me, 0, 0)) for s in shards]

    def body(*refs):
        src, land = refs[:n], refs[n:2 * n]
        outs = refs[2 * n + 1:]
        for i in range(n):
            send_sems, recv_sems = outs[4 * i], outs[4 * i + 1]
            for k, rel in enumerate((SIBLING,) + OTHER_CHIPS):
                _rcopy(src[i], land[i].at[_slot(_me())], send_sems.at[k], recv_sems.at[k], _peer(rel)).start()

    out_shape, out_specs, aliases = [], [], {}
    for i, s in enumerate(shards):
        out_shape += [pltpu.SemaphoreType.DMA((4,)), pltpu.SemaphoreType.DMA((4,)), pltpu.HBM(s.shape, s.dtype),
                      pltpu.HBM((NDEV,) + s.shape, s.dtype)]
        out_specs += [SEM_SPEC, SEM_SPEC, HBM_SPEC, HBM_SPEC]
        aliases[i] = 4 * i + 2
        aliases[n + i] = 4 * i + 3
    res = _split_call(body, name, [_in_hbm(s) for s in shards] + [_in_hbm(l) for l in lands] + [after],
                      [HBM_SPEC] * (2 * n) + [ANY_SPEC], out_shape, out_specs, aliases)
    return [tuple(res[4 * i:4 * i + 4]) for i in range(n)], res[-1]


def _gather_forward(started, after, name):
    n = len(started)

    def body(*refs):
        land, recv1 = refs[:n], refs[n:2 * n]
        outs = refs[2 * n + 1:]
        for i in range(n):
            send2, recv2 = outs[3 * i + 1], outs[3 * i + 2]
            for k, rel in enumerate(OTHER_CHIPS):
                blk = land[i].at[_slot(_peer(rel))]
                _rcopy(blk, blk, send2.at[k], recv1[i].at[1 + k], _peer(rel)).wait_recv()
                _rcopy(blk, blk, send2.at[k], recv2.at[k], _peer(SIBLING)).start()

    ins = [_in_hbm(st[3]) for st in started] + [st[1] for st in started] + [after]
    out_shape, out_specs, aliases = [], [], {}
    for i, st in enumerate(started):
        out_shape += [pltpu.HBM(st[3].shape, st[3].dtype), pltpu.SemaphoreType.DMA((3,)), pltpu.SemaphoreType.DMA((3,))]
        out_specs += [HBM_SPEC, SEM_SPEC, SEM_SPEC]
        aliases[i] = 3 * i
    res = _split_call(body, name, ins, [HBM_SPEC] * n + [SEM_SPEC] * n + [ANY_SPEC], out_shape, out_specs, aliases)
    return [(st[0], st[1], st[2], res[3 * i], res[3 * i + 1], res[3 * i + 2]) for i, st in enumerate(started)], res[-1]


def _gather_finish(forwarded, after, name):
    n = len(forwarded)

    def body(*refs):
        for i in range(n):
            send1, recv1, src, land, send2, recv2 = refs[6 * i:6 * i + 6]
            me = _slot(_me())
            sib = _slot(_peer(SIBLING))
            for k, rel in enumerate((SIBLING,) + OTHER_CHIPS):
                _rcopy(src, land.at[me], send1.at[k], recv1.at[k], _peer(rel)).wait_send()
            _rcopy(src, land.at[sib], send1.at[0], recv1.at[0], _peer(SIBLING)).wait_recv()
            for k, rel in enumerate(OTHER_CHIPS):
                mine, theirs = land.at[_slot(_peer(rel))], land.at[_slot(_peer(rel ^ SIBLING))]
                _rcopy(mine, mine, send2.at[k], recv2.at[k], _peer(SIBLING)).wait_send()
                _rcopy(theirs, theirs, send2.at[k], recv2.at[k], _peer(SIBLING)).wait_recv()

    ins, in_specs, out_shape, aliases = [], [], [], {}
    for i, f in enumerate(forwarded):
        ins += [f[0], f[1], _in_hbm(f[2]), _in_hbm(f[3]), f[4], f[5]]
        in_specs += [SEM_SPEC, SEM_SPEC, HBM_SPEC, HBM_SPEC, SEM_SPEC, SEM_SPEC]
        out_shape.append(pltpu.HBM(f[3].shape, f[3].dtype))
        aliases[6 * i + 3] = i
    res = _split_call(body, name, ins + [after], in_specs + [ANY_SPEC], out_shape, [HBM_SPEC] * n, aliases)
    return list(res[:-1])


def _to_sibling_start(parts, name):
    n = len(parts)
    lands = [lax.empty(p.shape[1:], p.dtype) for p in parts]

    def body(*refs):
        src, land = refs[:n], refs[n:2 * n]
        outs = refs[2 * n:]
        other = 1 - lax.axis_index("c")
        for i in range(n):
            _rcopy(src[i].at[other], land[i], outs[4 * i], outs[4 * i + 1], _peer(SIBLING)).start()

    out_shape, out_specs, aliases = [], [], {}
    for i, p in enumerate(parts):
        out_shape += [pltpu.SemaphoreType.DMA(()), pltpu.SemaphoreType.DMA(()), pltpu.HBM(p.shape, p.dtype),
                      pltpu.HBM(p.shape[1:], p.dtype)]
        out_specs += [SEM_SPEC, SEM_SPEC, HBM_SPEC, HBM_SPEC]
        aliases[i] = 4 * i + 2
        aliases[n + i] = 4 * i + 3
    res = _split_call(body, name, [_in_hbm(p) for p in parts] + [_in_hbm(l) for l in lands], [HBM_SPEC] * (2 * n),
                      out_shape, out_specs, aliases)
    return [tuple(res[4 * i:4 * i + 4]) for i in range(n)], res[-1]


def _to_sibling_finish(started, after, name):
    n = len(started)

    def body(*refs):
        for i in range(n):
            send, recv, src, land = refs[4 * i:4 * i + 4]
            cp = _rcopy(src.at[0], land, send, recv, _peer(SIBLING))
            cp.wait_send()
            cp.wait_recv()

    ins, in_specs, out_shape, aliases = [], [], [], {}
    for i, st in enumerate(started):
        ins += [st[0], st[1], _in_hbm(st[2]), _in_hbm(st[3])]
        in_specs += [SEM_SPEC, SEM_SPEC, HBM_SPEC, HBM_SPEC]
        out_shape += [pltpu.HBM(st[2].shape, st[2].dtype), pltpu.HBM(st[3].shape, st[3].dtype)]
        aliases[4 * i + 2] = 2 * i
        aliases[4 * i + 3] = 2 * i + 1
    res = _split_call(body, name, ins + [after], in_specs + [ANY_SPEC], out_shape, [HBM_SPEC] * (2 * n), aliases)
    return [(res[2 * i], res[2 * i + 1]) for i in range(n)]


def _chip_sum(parts, from_sibling, my_c, tile, name):
    _, _, r, c = parts.shape
    tr, tc = tile

    def body(c_ref, p_ref, s_ref, o_ref):
        o_ref[...] = (p_ref[...].astype(F32) + s_ref[...].astype(F32)).astype(o_ref.dtype)

    blk = pl.BlockSpec((4, tr, tc), lambda i, j, c_ref: (0, i, j))
    return pl.pallas_call(
        body, name=name, out_shape=_sds((4, r, c), parts.dtype),
        grid_spec=pltpu.PrefetchScalarGridSpec(
            num_scalar_prefetch=1, grid=(r // tr, c // tc),
            in_specs=[pl.BlockSpec((None, 4, tr, tc), lambda i, j, c_ref: (c_ref[0], 0, i, j)), blk], out_specs=blk),
        compiler_params=_params(("parallel", "parallel")),
    )(my_c, parts, from_sibling)


def _to_chips_start(sums, name):
    n = len(sums)
    lands = [lax.empty((3,) + s.shape[1:], s.dtype) for s in sums]

    def body(*refs):
        src, land = refs[:n], refs[n:2 * n]
        outs = refs[2 * n:]
        for i in range(n):
            for k, rel in enumerate(OTHER_CHIPS):
                to = _peer(rel)
                _rcopy(src[i].at[_chip(to)], land[i].at[k], outs[4 * i].at[k], outs[4 * i + 1].at[k], to).start()

    out_shape, out_specs, aliases = [], [], {}
    for i, s in enumerate(sums):
        out_shape += [pltpu.SemaphoreType.DMA((3,)), pltpu.SemaphoreType.DMA((3,)), pltpu.HBM(s.shape, s.dtype),
                      pltpu.HBM((3,) + s.shape[1:], s.dtype)]
        out_specs += [SEM_SPEC, SEM_SPEC, HBM_SPEC, HBM_SPEC]
        aliases[i] = 4 * i + 2
        aliases[n + i] = 4 * i + 3
    res = _split_call(body, name, [_in_hbm(s) for s in sums] + [_in_hbm(l) for l in lands], [HBM_SPEC] * (2 * n),
                      out_shape, out_specs, aliases)
    return [tuple(res[4 * i:4 * i + 4]) for i in range(n)], res[-1]


def _to_chips_finish(started, after, name):
    n = len(started)

    def body(*refs):
        for i in range(n):
            send, recv, src, land = refs[4 * i:4 * i + 4]
            for k, rel in enumerate(OTHER_CHIPS):
                cp = _rcopy(src.at[0], land.at[k], send.at[k], recv.at[k], _peer(rel))
                cp.wait_send()
                cp.wait_recv()

    ins, in_specs, out_shape, aliases = [], [], [], {}
    for i, st in enumerate(started):
        ins += [st[0], st[1], _in_hbm(st[2]), _in_hbm(st[3])]
        in_specs += [SEM_SPEC, SEM_SPEC, HBM_SPEC, HBM_SPEC]
        out_shape += [pltpu.HBM(st[2].shape, st[2].dtype), pltpu.HBM(st[3].shape, st[3].dtype)]
        aliases[4 * i + 2] = 2 * i
        aliases[4 * i + 3] = 2 * i + 1
    res = _split_call(body, name, ins + [after], in_specs + [ANY_SPEC], out_shape, [HBM_SPEC] * (2 * n), aliases)
    return [(res[2 * i], res[2 * i + 1]) for i in range(n)]


def _reduce_small(v, name, deps=()):
    _, r, c = v.shape

    def body(v_ref, o_ref, land, send_sems, recv_sems):
        me = _slot(_me())
        copies = []
        for j in range(1, NDEV):
            to = _peer(j)
            cp = _rcopy(v_ref.at[_slot(to)], land.at[me], send_sems.at[j - 1], recv_sems.at[j - 1], to)
            cp.start()
            copies.append(cp)
        land[me] = v_ref[me]
        for cp in copies:
            cp.wait()
        acc = land[0]
        for k in range(1, NDEV):
            acc = acc + land[k]
        o_ref[...] = acc

    vm = pl.BlockSpec(memory_space=pltpu.VMEM)
    body, dep_ins, dep_specs = _after(body, 1, deps)
    return pl.pallas_call(
        body, name=name, out_shape=_sds((r, c), F32), in_specs=[vm] + dep_specs, out_specs=vm,
        scratch_shapes=[pltpu.VMEM((NDEV, r, c), F32), pltpu.SemaphoreType.DMA((NDEV - 1,)),
                        pltpu.SemaphoreType.DMA((NDEV - 1,))],
        compiler_params=_params(),
    )(v, *dep_ins)


def _matmul(a, b, *, mode, tm, tn, tk, name, out_dtype=F32, epi=None, extra=None, b_slots=False, out_slots=False,
            deps=(), col_block=0, into=None, n_total=None):
    b_pair = b if isinstance(b, tuple) else None
    if b_pair:
        assert mode == "nt" and tk == 2 * b[0].shape[1] == a.shape[1] and not b_slots
        b = b[0]
    slot_w = b.shape[-1] if b_slots else None
    if mode == "nn":
        M, K = a.shape
        N = NDEV * slot_w if b_slots else b.shape[1]
    elif mode == "tn":
        K, M = a.shape
        N = b.shape[1]
    else:
        M, K = a.shape
        N = b.shape[-2]
        if b_slots:
            assert K == NDEV * slot_w and tk % slot_w == 0
    if mode == "nn" and b_slots:
        assert tn == slot_w
    if out_slots == "cols":
        assert tn * NDEV == N
    if out_slots == "rows":
        assert (M // NDEV) % tm == 0
    assert M % tm == 0 and N % tn == 0 and K % tk == 0, (name, M, N, K, tm, tn, tk)
    nk = K // tk
    dims = {"nn": ((1,), (0,)), "tn": ((0,), (0,)), "nt": ((1,), (1,))}[mode]

    if mode == "tn":
        a_spec = pl.BlockSpec((tk, tm), lambda i, j, k: (k, i))
    else:
        a_spec = pl.BlockSpec((tm, tk), lambda i, j, k: (i, k))
    if b_pair:
        b_spec = pl.BlockSpec((tn, tk // 2), lambda i, j, k: (j, 0))
    elif mode == "nt":
        b_spec = (pl.BlockSpec((tk // slot_w, tn, slot_w), lambda i, j, k: (k, j, 0)) if b_slots
                  else pl.BlockSpec((tn, tk), lambda i, j, k: (j, k)))
    else:
        b_spec = (pl.BlockSpec((None, tk, tn), lambda i, j, k: (j, k, 0)) if b_slots
                  else pl.BlockSpec((tk, tn), lambda i, j, k: (k, j)))
    tile = pl.BlockSpec((tm, tn), lambda i, j, k: (i, j + col_block))
    if out_slots == "cols":
        out_spec = pl.BlockSpec((None, None, tm, tn), lambda i, j, k: (j % 2, j // 2, i, 0))
        out_shape = _sds((2, 4, M, tn), out_dtype)
    elif out_slots == "rows":
        per = M // NDEV // tm
        out_spec = pl.BlockSpec((None, None, tm, tn), lambda i, j, k: ((i // per) % 2, (i // per) // 2, i % per, j))
        out_shape = _sds((2, 4, M // NDEV, N), out_dtype)
    else:
        out_spec, out_shape = tile, _sds((M, n_total or N), out_dtype)
    ins, in_specs = [a, b], [a_spec, b_spec]
    if b_pair:
        ins.append(b_pair[1])
        in_specs.append(b_spec)
    n_b = len(ins) - 1
    if epi in ("add", "dz"):
        ins.append(extra)
        in_specs.append(tile)
    aliases = {}
    if into is not None:
        aliases[len(ins)] = 0
        ins.append(into)
        in_specs.append(ANY_SPEC)
    if epi == "relu2":
        out_specs, out_shapes = [tile, tile], [_sds((M, N), F32), _sds((M, N), BF16)]
    else:
        out_specs, out_shapes = out_spec, out_shape
    n_in = len(ins)

    def body(*refs):
        outs = refs[n_in:-1] if nk > 1 else refs[n_in:]
        extra_ref = refs[1 + n_b]

        def finish(p):
            if epi is None:
                outs[0][...] = p.astype(out_dtype)
            elif epi == "add":
                outs[0][...] = (p + extra_ref[...]).astype(out_dtype)
            elif epi == "relu2":
                outs[0][...] = p
                rz = jnp.maximum(p, 0.0)
                outs[1][...] = (rz * rz).astype(BF16)
            else:
                outs[0][...] = (p * (2.0 * jnp.maximum(extra_ref[...], 0.0))).astype(out_dtype)

        def product():
            av = refs[0][...].astype(BF16)
            if b_pair:
                half = tk // 2
                return (lax.dot_general(av[:, :half], refs[1][...], (dims, ((), ())), preferred_element_type=F32)
                        + lax.dot_general(av[:, half:], refs[2][...], (dims, ((), ())), preferred_element_type=F32))
            if mode == "nt" and b_slots:
                return sum(lax.dot_general(av[:, s * slot_w:(s + 1) * slot_w], refs[1][s], (dims, ((), ())),
                                           preferred_element_type=F32) for s in range(tk // slot_w))
            return lax.dot_general(av, refs[1][...].astype(BF16), (dims, ((), ())), preferred_element_type=F32)

        if nk == 1:
            finish(product())
            return
        acc = refs[-1]
        k = pl.program_id(2)

        @pl.when(k == 0)
        def _():
            acc[...] = jnp.zeros_like(acc)

        acc[...] += product()

        @pl.when(k == nk - 1)
        def _():
            finish(acc[...])

    body, dep_ins, dep_specs = _after(body, n_in, deps)
    return pl.pallas_call(
        body, name=name, grid=(M // tm, N // tn, nk),
        in_specs=in_specs + dep_specs, out_specs=out_specs, out_shape=out_shapes, input_output_aliases=aliases,
        scratch_shapes=[pltpu.VMEM((tm, tn), F32)] if nk > 1 else [],
        compiler_params=_params(("parallel", "parallel", "arbitrary")),
    )(*ins, *dep_ins)


ROWS = 352


def _rmsnorm_fwd(h, w, name, deps=()):
    def body(h_ref, w_ref, u_ref):
        x = h_ref[...]
        rstd = lax.rsqrt(jnp.mean(x * x, axis=-1, keepdims=True) + EPS)
        u_ref[...] = (x * rstd * w_ref[...]).astype(BF16)

    row = pl.BlockSpec((ROWS, D), lambda i: (i, 0))
    body, dep_ins, dep_specs = _after(body, 2, deps)
    return pl.pallas_call(
        body, name=name, grid=(LP // ROWS,), in_specs=[row, pl.BlockSpec((1, D), lambda i: (0, 0))] + dep_specs,
        out_specs=row, out_shape=_sds((LP, D), BF16), compiler_params=_params(("parallel",)),
    )(h, w, *dep_ins)


TOKEN_ROWS = 512


def _rmsnorm_bwd_input(h, w, du, dres, name, deps=()):
    def math(h_ref, w_ref, du_ref, dres_ref):
        x = h_ref[...]
        rstd = lax.rsqrt(jnp.mean(x * x, axis=-1, keepdims=True) + EPS)
        xhat = x * rstd
        dy = du_ref[...]
        dxh = dy * w_ref[...]
        dh = dres_ref[...] + rstd * (dxh - xhat * jnp.mean(dxh * xhat, axis=-1, keepdims=True))
        return dh, jnp.sum(dy * xhat, axis=0, keepdims=True)

    def body(h_ref, w_ref, du_ref, dres_ref, hh_ref, duh_ref, dresh_ref, dx_ref, dhead_ref, gw_ref):
        dx_ref[...], part = math(h_ref, w_ref, du_ref, dres_ref)

        @pl.when(pl.program_id(0) == 0)
        def _():
            dhead_ref[...], head = math(hh_ref, w_ref, duh_ref, dresh_ref)
            gw_ref[...] = part + head

        @pl.when(pl.program_id(0) > 0)
        def _():
            gw_ref[...] += part

    rows = pl.BlockSpec((pl.Element(TOKEN_ROWS), pl.Element(D)),
                        lambda i: (pl.multiple_of(ROW_X + TOKEN_ROWS * i, 8), 0))
    head = pl.BlockSpec((ROW_X, D), lambda i: (0, 0))
    vec = pl.BlockSpec((1, D), lambda i: (0, 0))
    body, dep_ins, dep_specs = _after(body, 7, deps)
    return pl.pallas_call(
        body, name=name, grid=(SEQ // TOKEN_ROWS,),
        in_specs=[rows, vec, rows, rows, head, head, head] + dep_specs,
        out_specs=[pl.BlockSpec((TOKEN_ROWS, D), lambda i: (i, 0)), head, vec],
        out_shape=[_sds((SEQ, D), F32), _sds((ROW_X, D), F32), _sds((1, D), F32)],
        compiler_params=_params(("arbitrary",)),
    )(h, w, du, dres, h, du, dres, *dep_ins)


def _rmsnorm_bwd(h, w, du, dres, name, deps=()):
    def body(h_ref, w_ref, du_ref, dres_ref, dh_ref, dhb_ref, gw_ref):
        x = h_ref[...]
        rstd = lax.rsqrt(jnp.mean(x * x, axis=-1, keepdims=True) + EPS)
        xhat = x * rstd
        dy = du_ref[...]
        dxh = dy * w_ref[...]
        dh = dres_ref[...] + rstd * (dxh - xhat * jnp.mean(dxh * xhat, axis=-1, keepdims=True))
        dh_ref[...] = dh
        dhb_ref[...] = dh.astype(BF16)
        part = jnp.sum(dy * xhat, axis=0, keepdims=True)

        @pl.when(pl.program_id(0) == 0)
        def _():
            gw_ref[...] = part

        @pl.when(pl.program_id(0) > 0)
        def _():
            gw_ref[...] += part

    row = pl.BlockSpec((ROWS, D), lambda i: (i, 0))
    vec = pl.BlockSpec((1, D), lambda i: (0, 0))
    body, dep_ins, dep_specs = _after(body, 4, deps)
    return pl.pallas_call(
        body, name=name, grid=(LP // ROWS,), in_specs=[row, vec, row, row] + dep_specs, out_specs=[row, row, vec],
        out_shape=[_sds((LP, D), F32), _sds((LP, D), BF16), _sds((1, D), F32)],
        compiler_params=_params(("arbitrary",)),
    )(h, w, du, dres, *dep_ins)


def _loss_head(h2, wf, target):
    def body(h_ref, w_ref, t_ref, dh_ref, dhb_ref, sq_ref, gw_ref):
        i = pl.program_id(0)

        @pl.when(i == 0)
        def _():
            sq_ref[...] = jnp.zeros_like(sq_ref)
            gw_ref[...] = jnp.zeros_like(gw_ref)

        def rows(t, live):
            x = h_ref[...]
            rstd = lax.rsqrt(jnp.mean(x * x, axis=-1, keepdims=True) + EPS)
            xhat = x * rstd
            w = w_ref[...]
            err = xhat * w - t
            if live is not None:
                err = jnp.where(live, err, 0.0)
            sq_ref[...] += jnp.sum(err * err, axis=0, keepdims=True)
            dy = err * (1.0 / D)
            gw_ref[...] += jnp.sum(dy * xhat, axis=0, keepdims=True)
            dxh = dy * w
            dh = rstd * (dxh - xhat * jnp.mean(dxh * xhat, axis=-1, keepdims=True))
            dh_ref[...] = dh
            dhb_ref[...] = dh.astype(BF16)

        @pl.when(i == 0)
        def _():
            rid = lax.broadcasted_iota(jnp.int32, (ROWS, D), 0)
            rows(pltpu.roll(t_ref[...], ROW_X, 0), rid >= ROW_X)

        @pl.when(i > 0)
        def _():
            rows(t_ref[...], None)

    row = pl.BlockSpec((ROWS, D), lambda i: (i, 0))
    vec = pl.BlockSpec((1, D), lambda i: (0, 0))
    tgt = pl.BlockSpec((pl.Element(ROWS), pl.Element(D)),
                       lambda i: (pl.multiple_of(jnp.maximum(ROWS * i - ROW_X, 0), 8), 0))
    return pl.pallas_call(
        body, name="loss_head", grid=(LP // ROWS,),
        in_specs=[row, vec, tgt],
        out_specs=[row, row, vec, vec],
        out_shape=[_sds((LP, D), F32), _sds((LP, D), BF16), _sds((1, D), F32), _sds((1, D), F32)],
        compiler_params=_params(("arbitrary",)),
    )(h2, wf, target)


def _dot(a, b, dims):
    return lax.dot_general(a, b, (dims, ((), ())), preferred_element_type=F32)


NN, TN, NT = ((1,), (0,)), ((0,), (0,)), ((1,), (1,))


def _tri_sum(t, x):
    hi = x.astype(BF16)
    r1 = x - hi.astype(F32)
    mid = r1.astype(BF16)
    lo = (r1 - mid.astype(F32)).astype(BF16)
    return _dot(t, hi, NN) + _dot(t, mid, NN) + _dot(t, lo, NN)


def _gla_gates(glr_ref, gw2_ref, gb_ref, rows, row0):
    g_raw = _dot(glr_ref[rows, :].astype(BF16), gw2_ref[...], NN) + gb_ref[...]
    logsig = jnp.minimum(g_raw, 0.0) - jnp.log(1.0 + jnp.exp(-jnp.abs(g_raw)))
    rid = row0 + lax.broadcasted_iota(jnp.int32, g_raw.shape, 0)
    live = rid >= PAD
    return g_raw, jnp.where(live, logsig / TAU, 0.0), live


def _tri_masks():
    r = lax.broadcasted_iota(jnp.int32, (CHUNK, CHUNK), 0)
    c = lax.broadcasted_iota(jnp.int32, (CHUNK, CHUNK), 1)
    return r >= c


def _gla_specs(rev):
    n = NCH // CPS
    R = CPS * CHUNK
    st = (lambda s: n - 1 - s) if rev else (lambda s: s)
    return R, n, st, [
        pl.BlockSpec((R, KW), lambda s: (st(s), 0)),
        pl.BlockSpec((R, KW), lambda s: (st(s), 1)),
        pl.BlockSpec((R, GW), lambda s: (st(s), 1)),
        pl.BlockSpec((R, GW), lambda s: (st(s), 2)),
        pl.BlockSpec((R, 128), lambda s: (st(s), GLR_BLK)),
    ]


def _gla_fwd(proj, gw2p, gate_b, gnw, deps=()):
    R, n, st, pspecs = _gla_specs(False)

    def body(q_ref, k_ref, v_ref, r_ref, glr_ref, gw2_ref, gb_ref, gnw_ref, og_ref, o_ref, st_ref, state):
        s = pl.program_id(0)

        @pl.when(s == 0)
        def _():
            state[...] = jnp.zeros_like(state)

        causal = _tri_masks()
        tri = causal.astype(BF16)
        for c in range(CPS):
            rows = slice(c * CHUNK, (c + 1) * CHUNK)
            _, logg, _ = _gla_gates(glr_ref, gw2_ref, gb_ref, rows, s * R + c * CHUNK)
            G = _tri_sum(tri, logg)
            g_last = G[CHUNK - 1:CHUNK, :]
            q_dec = (q_ref[rows, :] * (DK ** -0.5) * jnp.exp(G)).astype(BF16)
            kk = k_ref[rows, :]
            k_inv = (kk * jnp.exp(-G)).astype(BF16)
            k_end = (kk * jnp.exp(g_last - G)).astype(BF16)
            decay = jnp.exp(g_last)
            for h in range(H):
                lk = slice(h * DK, (h + 1) * DK)
                lv = slice(h * DV, (h + 1) * DV)
                v = v_ref[rows, lv].astype(BF16)
                S = state[h]
                st_ref[c, h] = S
                A = jnp.where(causal, _dot(q_dec[:, lk], k_inv[:, lk], NT), 0.0).astype(BF16)
                o = _dot(A, v, NN) + _dot(q_dec[:, lk], S.astype(BF16), NT)
                state[h] = decay[:, lk] * S + _dot(v, k_end[:, lk], TN)
                o_ref[rows, lv] = o
                on = o * lax.rsqrt(jnp.mean(o * o, axis=-1, keepdims=True) + EPS) * gnw_ref[...]
                rr = r_ref[rows, lv]
                og_ref[rows, lv] = (on * (rr * jax.nn.sigmoid(rr))).astype(BF16)

    full = lambda shape: pl.BlockSpec(shape, lambda s: (0,) * len(shape))
    body, dep_ins, dep_specs = _after(body, 8, deps)
    return pl.pallas_call(
        body, name="gla_fwd", grid=(n,),
        in_specs=pspecs + [full((128, KW)), full((1, KW)), full((1, DV))] + dep_specs,
        out_specs=[pl.BlockSpec((R, GW), lambda s: (s, 0)), pl.BlockSpec((R, GW), lambda s: (s, 0)),
                   pl.BlockSpec((CPS, H, DV, DK), lambda s: (s, 0, 0, 0))],
        out_shape=[_sds((LP, GW + PW), BF16), _sds((LP, GW), F32), _sds((NCH, H, DV, DK), F32)],
        scratch_shapes=[pltpu.VMEM((H, DV, DK), F32)],
        compiler_params=_params(("arbitrary",)),
    )(proj, proj, proj, proj, proj, gw2p, gate_b, gnw, *dep_ins)


def _gla_bwd(proj, dmixed, o_saved, st_saved, gw2p, gate_b, gnw, deps=()):
    R, n, st, pspecs = _gla_specs(True)

    def body(q_ref, k_ref, v_ref, r_ref, glr_ref, dog_ref, o_ref, st_ref, gw2_ref, gb_ref, gnw_ref,
             dqkvr_ref, dglr_ref, ggn_ref, ggb_ref, ggw_ref, gstate):
        s = pl.program_id(0)

        @pl.when(s == 0)
        def _():
            gstate[...] = jnp.zeros_like(gstate)
            ggn_ref[...] = jnp.zeros_like(ggn_ref)
            ggb_ref[...] = jnp.zeros_like(ggb_ref)
            ggw_ref[...] = jnp.zeros_like(ggw_ref)

        causal = _tri_masks()
        tri = causal.astype(BF16)
        tri_up = (lax.broadcasted_iota(jnp.int32, (CHUNK, CHUNK), 0)
                  <= lax.broadcasted_iota(jnp.int32, (CHUNK, CHUNK), 1)).astype(BF16)
        gnw = gnw_ref[...]
        for c in reversed(range(CPS)):
            rows = slice(c * CHUNK, (c + 1) * CHUNK)
            g_raw, logg, live = _gla_gates(glr_ref, gw2_ref, gb_ref, rows, (n - 1 - s) * R + c * CHUNK)
            G = _tri_sum(tri, logg)
            g_last = G[CHUNK - 1:CHUNK, :]
            e_g, e_gi, e_end = jnp.exp(G), jnp.exp(-G), jnp.exp(g_last - G)
            q_dec = q_ref[rows, :] * (DK ** -0.5) * e_g
            kk = k_ref[rows, :]
            k_inv, k_end = kk * e_gi, kk * e_end
            q_dec_b, k_inv_b, k_end_b = q_dec.astype(BF16), k_inv.astype(BF16), k_end.astype(BF16)
            decay = jnp.exp(g_last)
            d_g, d_gl = [], []
            for h in range(H):
                lk = slice(h * DK, (h + 1) * DK)
                lv = slice(h * DV, (h + 1) * DV)
                o = o_ref[rows, lv]
                rr = r_ref[rows, lv]
                dog = dog_ref[rows, lv]
                rstd = lax.rsqrt(jnp.mean(o * o, axis=-1, keepdims=True) + EPS)
                ohat = o * rstd
                sr = jax.nn.sigmoid(rr)
                don = dog * (rr * sr)
                dqkvr_ref[rows, 2 * KW + GW + h * DV:2 * KW + GW + (h + 1) * DV] = (
                    dog * (ohat * gnw) * (sr * (1.0 + rr * (1.0 - sr)))).astype(BF16)
                ggn_ref[...] += jnp.sum(don * ohat, axis=0, keepdims=True)
                dohat = don * gnw
                do = (rstd * (dohat - ohat * jnp.mean(dohat * ohat, axis=-1, keepdims=True))).astype(BF16)
                v = v_ref[rows, lv].astype(BF16)
                S = st_ref[c, h]
                gS = gstate[h]
                S_b, gS_b = S.astype(BF16), gS.astype(BF16)
                qd, ki, ke = q_dec_b[:, lk], k_inv_b[:, lk], k_end_b[:, lk]
                A = jnp.where(causal, _dot(qd, ki, NT), 0.0).astype(BF16)
                dA = jnp.where(causal, _dot(do, v, NT), 0.0).astype(BF16)
                dv = _dot(A, do, TN) + _dot(ke, gS_b, NT)
                dq_dec = _dot(dA, ki, NN) + _dot(do, S_b, NN)
                dk_inv = _dot(dA, qd, TN)
                dk_end = _dot(v, gS_b, NN)
                d_decay = jnp.sum(gS * S, axis=0, keepdims=True)
                gstate[h] = decay[:, lk] * gS + _dot(do, qd, TN)
                dqkvr_ref[rows, lk] = (dq_dec * e_g[:, lk] * (DK ** -0.5)).astype(BF16)
                dqkvr_ref[rows, KW + h * DK:KW + (h + 1) * DK] = (
                    dk_inv * e_gi[:, lk] + dk_end * e_end[:, lk]).astype(BF16)
                dqkvr_ref[rows, 2 * KW + h * DV:2 * KW + (h + 1) * DV] = dv.astype(BF16)
                ke_prod = dk_end * k_end[:, lk]
                d_g.append(dq_dec * q_dec[:, lk] - dk_inv * k_inv[:, lk] - ke_prod)
                d_gl.append(jnp.sum(ke_prod, axis=0, keepdims=True) + d_decay * decay[:, lk])
            dlogg = _tri_sum(tri_up, jnp.concatenate(d_g, axis=1)) + jnp.concatenate(d_gl, axis=1)
            dg_raw = jnp.where(live, dlogg * (1.0 / TAU) * jax.nn.sigmoid(-g_raw), 0.0)
            ggb_ref[...] += jnp.sum(dg_raw, axis=0, keepdims=True)
            dg_b = dg_raw.astype(BF16)
            ggw_ref[...] += _dot(glr_ref[rows, :].astype(BF16), dg_b, TN)
            dglr_ref[rows, :] = _dot(dg_b, gw2_ref[...], NT).astype(BF16)

    full = lambda shape: pl.BlockSpec(shape, lambda s: (0,) * len(shape))
    body, dep_ins, dep_specs = _after(body, 11, deps)
    return pl.pallas_call(
        body, name="gla_bwd", grid=(n,),
        in_specs=pspecs + [pl.BlockSpec((R, GW), lambda s: (st(s), 0)), pl.BlockSpec((R, GW), lambda s: (st(s), 0)),
                           pl.BlockSpec((CPS, H, DV, DK), lambda s: (st(s), 0, 0, 0)),
                           full((128, KW)), full((1, KW)), full((1, DV))] + dep_specs,
        out_specs=[pl.BlockSpec((R, 2 * KW + 2 * GW), lambda s: (st(s), 0)), pl.BlockSpec((R, 128), lambda s: (st(s), 0)),
                   full((1, DV)), full((1, KW)), full((128, KW))],
        out_shape=[_sds((LP, D_INP), BF16), _sds((LP, 128), BF16),
                   _sds((1, DV), F32), _sds((1, KW), F32), _sds((128, KW), F32)],
        scratch_shapes=[pltpu.VMEM((H, DV, DK), F32)],
        compiler_params=_params(("arbitrary",)),
    )(proj, proj, proj, proj, proj, dmixed, o_saved, st_saved, gw2p, gate_b, gnw, *dep_ins)


def _pool_pre(x, win, rid):
    s, step = x, 1
    while step < win:
        s = s + pltpu.roll(s, step, 0)
        step *= 2
    cnt = jnp.clip(rid - (PAD - 1), 1, win).astype(F32)
    live = rid >= PAD
    return jnp.where(live, s / cnt - x, 0.0), cnt, live


def _pool_fwd(proj, pool_w, pool_scale, mixed):
    def body(pu_ref, w_ref, sc_ref, _, o_ref):
        rid = lax.broadcasted_iota(jnp.int32, (LP, GC), 0)
        for g, win in enumerate(WINDOWS):
            @pl.when(pl.program_id(0) == g)
            def _():
                y, _, _ = _pool_pre(pu_ref[...], win, rid)
                o_ref[...] = (_dot(y.astype(BF16), w_ref[...], NN) * sc_ref[...]).astype(BF16)

    col = lambda base: pl.BlockSpec((LP, GC), lambda g: (0, base + g))
    return pl.pallas_call(
        body, name="pool_fwd", grid=(len(WINDOWS),),
        in_specs=[col(POOL_BLK), pl.BlockSpec((None, GC, GC), lambda g: (g, 0, 0)),
                  pl.BlockSpec((1, GC), lambda g: (0, g)), ANY_SPEC],
        out_specs=col(GW // GC), out_shape=_sds(mixed.shape, BF16), input_output_aliases={3: 0},
        compiler_params=_params(("parallel",)),
    )(proj, pool_w, pool_scale, mixed)


def _pool_bwd(proj, dmixed, pool_w, pool_scale, dproj):
    def body(pu_ref, do_ref, w_ref, sc_ref, _, dpu_ref, dw_ref, dsc_ref):
        rid = lax.broadcasted_iota(jnp.int32, (LP, GC), 0)
        for g, win in enumerate(WINDOWS):
            @pl.when(pl.program_id(0) == g)
            def _():
                y, cnt, live = _pool_pre(pu_ref[...], win, rid)
                y_b = y.astype(BF16)
                w = w_ref[...]
                do = do_ref[...]
                dsc_ref[...] = jnp.sum(do * _dot(y_b, w, NN), axis=0, keepdims=True)
                dyw = (do * sc_ref[...]).astype(BF16)
                dw_ref[...] = _dot(y_b, dyw, TN)
                dy = jnp.where(live, _dot(dyw, w, NT), 0.0)
                s, step = dy / cnt, 1
                while step < win:
                    s = s + pltpu.roll(s, LP - step, 0)
                    step *= 2
                dpu_ref[...] = (s - dy).astype(BF16)

    col = lambda base: pl.BlockSpec((LP, GC), lambda g: (0, base + g))
    mat = pl.BlockSpec((None, GC, GC), lambda g: (g, 0, 0))
    vec = pl.BlockSpec((1, GC), lambda g: (0, g))
    return pl.pallas_call(
        body, name="pool_bwd", grid=(len(WINDOWS),),
        in_specs=[col(POOL_BLK), col(GW // GC), mat, vec, ANY_SPEC], out_specs=[col(POOL_BLK), mat, vec],
        out_shape=[_sds(dproj.shape, BF16), _sds((4, GC, GC), F32), _sds((1, PW), F32)],
        input_output_aliases={4: 0}, compiler_params=_params(("parallel",)),
    )(proj, dmixed, pool_w, pool_scale, dproj)


def _adamw_math(w, g, m, v):
    m = B1 * m + (1.0 - B1) * g
    v = B2 * v + (1.0 - B2) * (g * g)
    m_hat = m * (1.0 / (1.0 - B1 ** STEP))
    v_hat = v * (1.0 / (1.0 - B2 ** STEP))
    return -LR * (m_hat / (jnp.sqrt(v_hat) + AEPS) + WD * w), m, v


def _adamw_landed(sums, landed, my_chip, w, m, v, rows, name, cols=None):
    _, r, c = w.shape

    def body(chip_ref, s_ref, l_ref, w_ref, m_ref, v_ref, g_ref, d_ref, mo_ref, vo_ref):
        g = s_ref[...].astype(F32)
        for k in range(3):
            g = g + l_ref[k].astype(F32)
        g_ref[...] = g
        d_ref[...], mo_ref[...], vo_ref[...] = _adamw_math(w_ref[...], g, m_ref[...], v_ref[...])

    cols = cols or c
    blk = pl.BlockSpec((None, rows, cols), lambda i, j, chip_ref: (0, i, j))
    return pl.pallas_call(
        body, name=name, out_shape=[_sds((1, r, c), F32)] * 4,
        grid_spec=pltpu.PrefetchScalarGridSpec(
            num_scalar_prefetch=1, grid=(r // rows, c // cols),
            in_specs=[pl.BlockSpec((None, rows, cols), lambda i, j, chip_ref: (chip_ref[0], i, j)),
                      pl.BlockSpec((3, rows, cols), lambda i, j, chip_ref: (0, i, j)), blk, blk, blk],
            out_specs=[blk] * 4),
        compiler_params=_params(("parallel", "parallel")),
    )(my_chip, sums, landed, w, m, v)


def _adamw_small(g, w, m, v):
    def body(g_ref, w_ref, m_ref, v_ref, d_ref, mo_ref, vo_ref):
        d_ref[...], mo_ref[...], vo_ref[...] = _adamw_math(w_ref[...], g_ref[...], m_ref[...], v_ref[...])

    return pl.pallas_call(body, name="adamw_small", out_shape=[_sds(w.shape, F32)] * 3)(g, w, m, v)


SMALL_REPL = (("norm1_w", D), ("norm2_w", D), ("final_norm_w", D), ("pool_scale", PW), ("gate_b", KW),
              ("gla_norm_w", DV))


def _pack_rows(vecs, rows):
    flat = jnp.concatenate([jnp.ravel(v) for v in vecs])
    return jnp.pad(flat, (0, rows * 1024 - flat.shape[0])).reshape(rows, 1024)


def kernel(x, meta_tokens, norm1_w, w_in, gate_w2, gate_b, gla_norm_w, pool_w, pool_scale, w_out, norm2_w, mlp_w1, mlp_w2, final_norm_w, loss_target, m_meta_tokens, m_norm1_w, m_w_in, m_gate_w2, m_gate_b, m_gla_norm_w, m_pool_w, m_pool_scale, m_w_out, m_norm2_w, m_mlp_w1, m_mlp_w2, m_final_norm_w, v_meta_tokens, v_norm1_w, v_w_in, v_gate_w2, v_gate_b, v_gla_norm_w, v_pool_w, v_pool_scale, v_w_out, v_norm2_w, v_mlp_w1, v_mlp_w2, v_final_norm_w):
    W = dict(meta_tokens=meta_tokens, norm1_w=norm1_w, w_in=w_in, gate_w2=gate_w2, gate_b=gate_b,
             gla_norm_w=gla_norm_w, pool_w=pool_w, pool_scale=pool_scale, w_out=w_out, norm2_w=norm2_w,
             mlp_w1=mlp_w1, mlp_w2=mlp_w2, final_norm_w=final_norm_w)
    Mo = dict(meta_tokens=m_meta_tokens, norm1_w=m_norm1_w, w_in=m_w_in, gate_w2=m_gate_w2, gate_b=m_gate_b,
              gla_norm_w=m_gla_norm_w, pool_w=m_pool_w, pool_scale=m_pool_scale, w_out=m_w_out, norm2_w=m_norm2_w,
              mlp_w1=m_mlp_w1, mlp_w2=m_mlp_w2, final_norm_w=m_final_norm_w)
    Vo = dict(meta_tokens=v_meta_tokens, norm1_w=v_norm1_w, w_in=v_w_in, gate_w2=v_gate_w2, gate_b=v_gate_b,
              gla_norm_w=v_gla_norm_w, pool_w=v_pool_w, pool_scale=v_pool_scale, w_out=v_w_out, norm2_w=v_norm2_w,
              mlp_w1=v_mlp_w1, mlp_w2=v_mlp_w2, final_norm_w=v_final_norm_w)

    ex = _Exchange(dict(small=_pack_rows([meta_tokens, gate_w2[0]], 8), w_in=w_in[0].T.astype(BF16)),
                   dict(w_out=w_out[0], pool_w=pool_w[0].reshape(4 * 32, GC), mlp_w1=mlp_w1[0],
                        mlp_w2a=mlp_w2[0][:, :D // 2], mlp_w2b=mlp_w2[0][:, D // 2:]))
    step = _layer_step(x[0], loss_target[0], ex, norm1_w, gate_b, gla_norm_w, pool_scale, norm2_w,
                       final_norm_w.reshape(1, D))
    grad_x = step["dx"][None]

    last = step["dx"]
    out = {}
    for group in ("down", "up", "mix"):
        for k, (sums, landed) in ex.grad_finish(group, last).items():
            out[k] = _adamw_landed(sums, landed, ex.my_chip, W[k], Mo[k], Vo[k], SHARD_ROWS[k], "adamw_" + k)
            last = out[k][1]

    loss_part = 0.5 * jnp.sum(step["sq"]) / D
    to_all = _pack_rows([step[k] for k, _ in SMALL_REPL] + [loss_part], 8)
    cols = lambda g: g.reshape(g.shape[0], NDEV, -1).transpose(1, 0, 2).reshape(NDEV, -1, 1024)
    packed = jnp.concatenate([jnp.broadcast_to(to_all, (NDEV, 8, 1024)), cols(step["dhead"][PAD:]),
                              cols(step["gate_w2"][:RANK]), jnp.zeros((NDEV, 3, 1024), F32)], axis=1)
    red = _reduce_small(packed, "reduce_small", deps=[last])
    loss = red[7, 768]

    done = ex.grad_finish("w_in", red)
    poolw3 = lambda a: a.reshape(1, 4 * 32, GC)
    res = _adamw_landed(*done["pool_w"], ex.my_chip, poolw3(pool_w), poolw3(m_pool_w), poolw3(v_pool_w),
                        SHARD_ROWS["pool_w"], "adamw_pool_w")
    out["pool_w"] = [a.reshape(pool_w.shape) for a in res]
    tr = lambda a: a[0].T[None]
    res = _adamw_landed(*done["w_in"], ex.my_chip, tr(w_in), tr(m_w_in), tr(v_w_in), D_IN // NDEV, "adamw_w_in",
                        cols=256)
    out["w_in"] = [a[0].T[None] for a in res]

    def small_pack(P):
        return jnp.concatenate([_pack_rows([P[k] for k, _ in SMALL_REPL], 8),
                                _pack_rows([P["meta_tokens"], P["gate_w2"]], 8)], axis=0)

    g_small = red.at[7, 768].set(0.0)
    res_small = _adamw_small(g_small, small_pack(W), small_pack(Mo), small_pack(Vo))
    res_small = [g_small] + list(res_small)
    off = 0
    for k, nel in SMALL_REPL:
        out[k] = [a[0:8].reshape(-1)[off:off + nel].reshape(W[k].shape) for a in res_small]
        off += nel
    out["meta_tokens"] = [a[8:12].reshape(N_META, D // NDEV) for a in res_small]
    out["gate_w2"] = [a[12].reshape(1, RANK, KW // NDEV) for a in res_small]

    order = ["meta_tokens", "norm1_w", "w_in", "gate_w2", "gate_b", "gla_norm_w", "pool_w", "pool_scale", "w_out",
             "norm2_w", "mlp_w1", "mlp_w2", "final_norm_w"]
    return (loss, grad_x, *[out[k][0] for k in order], *[out[k][1] for k in order],
            *[out[k][2] for k in order], *[out[k][3] for k in order])


SHARD_ROWS = dict(w_out=256, mlp_w1=512, mlp_w2=256, pool_w=128)
C_GLR = 2 * KW + 2 * GW
GATHER_GROUPS = dict(small=("small",), w_in=("w_in",), mix=("w_out", "pool_w"), up=("mlp_w1",), down_a=("mlp_w2a",),
                     down_b=("mlp_w2b",))
GRAD_GROUPS = dict(down=("mlp_w2",), up=("mlp_w1",), mix=("w_out",), w_in=("pool_w", "w_in"))


class _Exchange:
    def __init__(self, first, rest):
        head, token = _gather_start(list(first.values()), "gather_start_first", first["small"])
        token, later = lax.optimization_barrier((token, list(rest.values())))
        tail, self.started = _gather_start([v.astype(BF16) for v in later], "gather_start_rest", token)
        self.state = dict(zip(list(first) + list(rest), head + tail))
        self.my_c = lax.axis_index("c").astype(jnp.int32).reshape(1)
        self.my_chip = (2 * lax.axis_index("x") + lax.axis_index("y")).astype(jnp.int32).reshape(1)
        self.sibling, self.chips = {}, {}

    def forward(self, group, after):
        ks = GATHER_GROUPS[group]
        fwd, token = _gather_forward([self.state[k] for k in ks], after, "gather_forward_" + group)
        self.state.update(zip(ks, fwd))
        return token

    def weights(self, group, after):
        ks = GATHER_GROUPS[group]
        g = dict(zip(ks, _gather_finish([self.state[k] for k in ks], after, "gather_finish_" + group)))
        if group == "w_in":
            nat = g["w_in"].reshape(D_IN, D)
            return jnp.concatenate([nat[:C_GLR], nat[C_GLR + RANK:], nat[C_GLR:C_GLR + RANK],
                                    jnp.zeros((D_INP - D_IN, D), BF16)], axis=0)
        if group == "small":
            return g["small"]
        if group == "mix":
            return (g["w_out"].reshape(D, D),
                    g["pool_w"].reshape(NDEV, 4, 32, GC).transpose(1, 0, 2, 3).reshape(4, GC, GC))
        return g["mlp_w1"] if group == "up" else g[ks[0]].reshape(DFF, D // 2)

    def grad(self, group, grads):
        parts = dict(grads)
        if group == "w_in":
            g = parts["w_in"]
            nat = jnp.concatenate([g[:C_GLR], g[C_GLR + PW:C_GLR + PW + RANK], g[C_GLR:C_GLR + PW]], axis=0)
            parts["w_in"] = nat.reshape(4, 2, D_IN // NDEV, D).transpose(1, 0, 2, 3)
            parts["pool_w"] = (parts["pool_w"].astype(BF16).reshape(4, 4, 2, 32, GC).transpose(2, 1, 0, 3, 4)
                               .reshape(2, 4, 4 * 32, GC))
        ks = GRAD_GROUPS[group]
        started, token = _to_sibling_start([parts[k] for k in ks], "grad_sibling_start_" + group)
        self.sibling[group] = started
        return token

    def grad_mid(self, group, after):
        ks = GRAD_GROUPS[group]
        both = _to_sibling_finish(self.sibling[group], after, "grad_sibling_finish_" + group)
        tile = lambda k, p: (p.shape[2], 512) if k == "w_in" else (SHARD_ROWS[k], p.shape[3])
        sums = [_chip_sum(p, s, self.my_c, tile(k, p), "chip_sum_" + k) for k, (p, s) in zip(ks, both)]
        self.chips[group], token = _to_chips_start(sums, "grad_chips_start_" + group)
        return token

    def grad_finish(self, group, after):
        done = _to_chips_finish(self.chips[group], after, "grad_chips_finish_" + group)
        return dict(zip(GRAD_GROUPS[group], done))


def _layer_step(x, target, ex, norm1_w, gate_b, gla_norm_w, pool_scale, norm2_w, final_norm_w):
    small = ex.weights("small", ex.forward("small", ex.started))
    meta_full = small[:, 0:4].reshape(NDEV, N_META, D // NDEV).transpose(1, 0, 2).reshape(N_META, D)
    gw2_full = small[:, 4].reshape(NDEV, RANK, KW // NDEV).transpose(1, 0, 2).reshape(RANK, KW)
    gw2p = jnp.pad(gw2_full, ((0, 128 - RANK), (0, 0))).astype(BF16)
    h0 = jnp.concatenate([jnp.zeros((PAD, D), F32), meta_full, x], axis=0)
    u1 = _rmsnorm_fwd(h0, norm1_w, "rmsnorm1")
    win_p = ex.weights("w_in", ex.forward("w_in", u1))
    proj = _matmul(u1, win_p, mode="nt", tm=1056, tn=1408, tk=2048, name="proj")
    tok = ex.forward("mix", proj)
    mixed, o_saved, st_saved = _gla_fwd(proj, gw2p, gate_b, gla_norm_w, deps=[tok])
    wout_f, poolw_f = ex.weights("mix", mixed)
    mixed = _pool_fwd(proj, poolw_f, pool_scale, mixed)
    h1 = _matmul(mixed, wout_f, mode="nn", tm=1056, tn=1024, tk=2048, name="mix_out", epi="add", extra=h0)
    tok = ex.forward("up", h1)
    u2 = _rmsnorm_fwd(h1, norm2_w, "rmsnorm2", deps=[tok])
    w1_g = ex.weights("up", u2)
    z, act = _matmul(u2, w1_g, mode="nn", tm=1056, tn=1024, tk=2048, name="mlp_up", epi="relu2", b_slots=True)
    w2a = ex.weights("down_a", ex.forward("down_a", act))
    h2 = _matmul(act, w2a, mode="nn", tm=1056, tn=1024, tk=2048, name="mlp_down_a", epi="add", extra=h1, n_total=D)
    w2b = ex.weights("down_b", ex.forward("down_b", h2))
    h2 = _matmul(act, w2b, mode="nn", tm=1056, tn=1024, tk=2048, name="mlp_down_b", epi="add", extra=h1, n_total=D,
                 col_block=1, into=h2)
    dh2, dh2b, sq, g_fnw = _loss_head(h2, final_norm_w, target)

    g_w2 = _matmul(act, dh2b, mode="tn", tm=512, tn=2048, tk=LP, name="d_mlp_w2", out_dtype=BF16, out_slots="rows")
    tok = ex.grad("down", dict(mlp_w2=g_w2))
    dz = _matmul(dh2b, (w2a, w2b), mode="nt", tm=1056, tn=1024, tk=2048, name="d_act", out_dtype=BF16, epi="dz",
                 extra=z, deps=[tok])
    tok = ex.grad_mid("down", dz)
    g_w1 = _matmul(u2, dz, mode="tn", tm=1024, tn=1024, tk=LP, name="d_mlp_w1", out_dtype=BF16, out_slots="cols",
                   deps=[tok])
    tok = ex.grad("up", dict(mlp_w1=g_w1))
    du2 = _matmul(dz, w1_g, mode="nt", tm=1056, tn=1024, tk=2048, name="d_u2", b_slots=True, deps=[tok])
    tok = ex.grad_mid("up", du2)
    dh1, dh1b, g_n2 = _rmsnorm_bwd(h1, norm2_w, du2, dh2, "rmsnorm2_bwd", deps=[tok])
    g_wout = _matmul(mixed, dh1b, mode="tn", tm=256, tn=2048, tk=LP, name="d_w_out", out_dtype=BF16, out_slots="rows")
    tok = ex.grad("mix", dict(w_out=g_wout))
    dmixed = _matmul(dh1b, wout_f, mode="nt", tm=1056, tn=1024, tk=2048, name="d_mixed", deps=[tok])
    tok = ex.grad_mid("mix", dmixed)
    dproj, dglr, g_gnw, g_gb, g_gw2 = _gla_bwd(proj, dmixed, o_saved, st_saved, gw2p, gate_b, gla_norm_w, deps=[tok])
    dproj, g_poolw, g_psc = _pool_bwd(proj, dmixed, poolw_f, pool_scale, dproj)
    dproj = lax.dynamic_update_slice(dproj, dglr, (0, GLR_BLK * 128))
    g_win_p = _matmul(dproj, u1, mode="tn", tm=384, tn=2048, tk=LP, name="d_w_in", out_dtype=BF16)
    tok = ex.grad("w_in", dict(pool_w=g_poolw, w_in=g_win_p))
    du1 = _matmul(dproj, win_p, mode="nn", tm=1056, tn=1024, tk=1408, name="d_u1", deps=[tok])
    tok = ex.grad_mid("w_in", du1)
    dx, dhead, g_n1 = _rmsnorm_bwd_input(h0, norm1_w, du1, dh1, "rmsnorm1_bwd", deps=[tok])
    return dict(dx=dx, dhead=dhead, sq=sq, gate_w2=g_gw2, norm1_w=g_n1, norm2_w=g_n2, final_norm_w=g_fnw, pool_scale=g_psc,
                gate_b=g_gb, gla_norm_w=g_gnw)
```

```python
import functools

import jax
import jax.numpy as jnp
from jax import lax
from jax.experimental import pallas as pl
from jax.experimental.pallas import tpu as pltpu

F32, BF16 = jnp.float32, jnp.bfloat16
MESH = pl.DeviceIdType.MESH

NDEV = 8
D = 2048
SEQ = 2048
N_META = 16
CHUNK = 64
PAD = (-N_META) % CHUNK
ROW_X = PAD + N_META
LP = ROW_X + SEQ
NCH = LP // CHUNK
H = 4
DK = 128
DV = 256
KW = H * DK
GW = H * DV
PW = 1024
RANK = 16
TAU = 16.0
WINDOWS = (2, 4, 8, 16)
GC = 256
DFF = 4 * D
EPS = 1e-6
D_IN = 2 * KW + 2 * GW + RANK + PW
D_INP = 4224
GLR_BLK = (2 * KW + 2 * GW + PW) // 128
POOL_BLK = (2 * KW + 2 * GW) // GC
LR, B1, B2, AEPS, WD, STEP = 0.001, 0.9, 0.999, 1e-08, 0.01, 10
VMEM_LIMIT = 48 * 1024 * 1024
CPS = 3


def _params(sem=None):
    return pltpu.CompilerParams(dimension_semantics=sem, vmem_limit_bytes=VMEM_LIMIT)


def _sds(shape, dtype):
    return jax.ShapeDtypeStruct(shape, dtype)


def _me():
    return lax.axis_index("x"), lax.axis_index("y"), lax.axis_index("c")


def _peer(j):
    x, y, c = _me()
    return (x ^ ((j >> 2) & 1), y ^ ((j >> 1) & 1), c ^ (j & 1))


def _slot(dev):
    return 4 * dev[0] + 2 * dev[1] + dev[2]


HBM_SPEC = pl.BlockSpec(memory_space=pltpu.HBM)
SEM_SPEC = pl.BlockSpec(memory_space=pltpu.SEMAPHORE)
ANY_SPEC = pl.BlockSpec(memory_space=pl.ANY)
EFFECT = pltpu.SideEffectType.DATAFLOW_SIDE_EFFECTING
SIBLING = 1
OTHER_CHIPS = (2, 4, 6)


def _in_hbm(a):
    return pltpu.with_memory_space_constraint(a, pltpu.HBM)


def _chip(dev):
    return 2 * dev[0] + dev[1]


def _rcopy(src, dst, send_sem, recv_sem, to):
    return pltpu.make_async_remote_copy(src_ref=src, dst_ref=dst, send_sem=send_sem, recv_sem=recv_sem,
                                        device_id=to, device_id_type=MESH)


def _split_call(body, name, ins, in_specs, out_shape, out_specs, aliases, scratch=()):
    n = len(ins) + len(out_shape)

    def with_token(*refs):
        body(*refs[:n], *refs[n + 1:])
        refs[n][...] = jnp.zeros_like(refs[n])

    return pl.pallas_call(
        with_token, name=name, in_specs=in_specs, out_shape=list(out_shape) + [_sds((8, 128), F32)],
        out_specs=list(out_specs) + [pl.BlockSpec(memory_space=pltpu.VMEM)],
        input_output_aliases=aliases, scratch_shapes=list(scratch),
        compiler_params=pltpu.CompilerParams(has_side_effects=EFFECT),
    )(*ins)


def _after(body, n_in, deps):
    deps = [d for d in deps if d is not None]
    if not deps:
        return body, [], []
    return (lambda *refs: body(*refs[:n_in], *refs[n_in + len(deps):])), deps, [ANY_SPEC] * len(deps)


def _gather_start(shards, name, after):
    n = len(shards)
    me = _slot(_me())
    lands = [lax.dynamic_update_slice(lax.empty((NDEV,) + s.shape, s.dtype), s[None], (me, 0, 0)) for s in shards]

    def body(*refs):
        src, land = refs[:n], refs[n:2 * n]
        outs = refs[2 * n + 1:]
        for i in range(n):
            send_sems, recv_sems = outs[4 * i], outs[4 * i + 1]
            for k, rel in enumerate((SIBLING,) + OTHER_CHIPS):
                _rcopy(src[i], land[i].at[_slot(_me())], send_sems.at[k], recv_sems.at[k], _peer(rel)).start()

    out_shape, out_specs, aliases = [], [], {}
    for i, s in enumerate(shards):
        out_shape += [pltpu.SemaphoreType.DMA((4,)), pltpu.SemaphoreType.DMA((4,)), pltpu.HBM(s.shape, s.dtype),
                      pltpu.HBM((NDEV,) + s.shape, s.dtype)]
        out_specs += [SEM_SPEC, SEM_SPEC, HBM_SPEC, HBM_SPEC]
        aliases[i] = 4 * i + 2
        aliases[n + i] = 4 * i + 3
    res = _split_call(body, name, [_in_hbm(s) for s in shards] + [_in_hbm(l) for l in lands] + [after],
                      [HBM_SPEC] * (2 * n) + [ANY_SPEC], out_shape, out_specs, aliases)
    return [tuple(res[4 * i:4 * i + 4]) for i in range(n)], res[-1]


def _gather_forward(started, after, name):
    n = len(started)

    def body(*refs):
        land, recv1 = refs[:n], refs[n:2 * n]
        outs = refs[2 * n + 1:]
        for i in range(n):
            send2, recv2 = outs[3 * i + 1], outs[3 * i + 2]
            for k, rel in enumerate(OTHER_CHIPS):
                blk = land[i].at[_slot(_peer(rel))]
                _rcopy(blk, blk, send2.at[k], recv1[i].at[1 + k], _peer(rel)).wait_recv()
                _rcopy(blk, blk, send2.at[k], recv2.at[k], _peer(SIBLING)).start()

    ins = [_in_hbm(st[3]) for st in started] + [st[1] for st in started] + [after]
    out_shape, out_specs, aliases = [], [], {}
    for i, st in enumerate(started):
        out_shape += [pltpu.HBM(st[3].shape, st[3].dtype), pltpu.SemaphoreType.DMA((3,)), pltpu.SemaphoreType.DMA((3,))]
        out_specs += [HBM_SPEC, SEM_SPEC, SEM_SPEC]
        aliases[i] = 3 * i
    res = _split_call(body, name, ins, [HBM_SPEC] * n + [SEM_SPEC] * n + [ANY_SPEC], out_shape, out_specs, aliases)
    return [(st[0], st[1], st[2], res[3 * i], res[3 * i + 1], res[3 * i + 2]) for i, st in enumerate(started)], res[-1]


def _gather_finish(forwarded, after, name):
    n = len(forwarded)

    def body(*refs):
        for i in range(n):
            send1, recv1, src, land, send2, recv2 = refs[6 * i:6 * i + 6]
            me = _slot(_me())
            sib = _slot(_peer(SIBLING))
            for k, rel in enumerate((SIBLING,) + OTHER_CHIPS):
                _rcopy(src, land.at[me], send1.at[k], recv1.at[k], _peer(rel)).wait_send()
            _rcopy(src, land.at[sib], send1.at[0], recv1.at[0], _peer(SIBLING)).wait_recv()
            for k, rel in enumerate(OTHER_CHIPS):
                mine, theirs = land.at[_slot(_peer(rel))], land.at[_slot(_peer(rel ^ SIBLING))]
                _rcopy(mine, mine, send2.at[k], recv2.at[k], _peer(SIBLING)).wait_send()
                _rcopy(theirs, theirs, send2.at[k], recv2.at[k], _peer(SIBLING)).wait_recv()

    ins, in_specs, out_shape, aliases = [], [], [], {}
    for i, f in enumerate(forwarded):
        ins += [f[0], f[1], _in_hbm(f[2]), _in_hbm(f[3]), f[4], f[5]]
        in_specs += [SEM_SPEC, SEM_SPEC, HBM_SPEC, HBM_SPEC, SEM_SPEC, SEM_SPEC]
        out_shape.append(pltpu.HBM(f[3].shape, f[3].dtype))
        aliases[6 * i + 3] = i
    res = _split_call(body, name, ins + [after], in_specs + [ANY_SPEC], out_shape, [HBM_SPEC] * n, aliases)
    return list(res[:-1])


def _to_sibling_start(parts, name):
    n = len(parts)
    lands = [lax.empty(p.shape[1:], p.dtype) for p in parts]

    def body(*refs):
        src, land = refs[:n], refs[n:2 * n]
        outs = refs[2 * n:]
        other = 1 - lax.axis_index("c")
        for i in range(n):
            _rcopy(src[i].at[other], land[i], outs[4 * i], outs[4 * i + 1], _peer(SIBLING)).start()

    out_shape, out_specs, aliases = [], [], {}
    for i, p in enumerate(parts):
        out_shape += [pltpu.SemaphoreType.DMA(()), pltpu.SemaphoreType.DMA(()), pltpu.HBM(p.shape, p.dtype),
                      pltpu.HBM(p.shape[1:], p.dtype)]
        out_specs += [SEM_SPEC, SEM_SPEC, HBM_SPEC, HBM_SPEC]
        aliases[i] = 4 * i + 2
        aliases[n + i] = 4 * i + 3
    res = _split_call(body, name, [_in_hbm(p) for p in parts] + [_in_hbm(l) for l in lands], [HBM_SPEC] * (2 * n),
                      out_shape, out_specs, aliases)
    return [tuple(res[4 * i:4 * i + 4]) for i in range(n)], res[-1]


def _to_sibling_finish(started, after, name):
    n = len(started)

    def body(*refs):
        for i in range(n):
            send, recv, src, land = refs[4 * i:4 * i + 4]
            cp = _rcopy(src.at[0], land, send, recv, _peer(SIBLING))
            cp.wait_send()
            cp.wait_recv()

    ins, in_specs, out_shape, aliases = [], [], [], {}
    for i, st in enumerate(started):
        ins += [st[0], st[1], _in_hbm(st[2]), _in_hbm(st[3])]
        in_specs += [SEM_SPEC, SEM_SPEC, HBM_SPEC, HBM_SPEC]
        out_shape += [pltpu.HBM(st[2].shape, st[2].dtype), pltpu.HBM(st[3].shape, st[3].dtype)]
        aliases[4 * i + 2] = 2 * i
        aliases[4 * i + 3] = 2 * i + 1
    res = _split_call(body, name, ins + [after], in_specs + [ANY_SPEC], out_shape, [HBM_SPEC] * (2 * n), aliases)
    return [(res[2 * i], res[2 * i + 1]) for i in range(n)]


def _chip_sum(parts, from_sibling, my_c, tile, name):
    _, _, r, c = parts.shape
    tr, tc = tile

    def body(c_ref, p_ref, s_ref, o_ref):
        o_ref[...] = (p_ref[...].astype(F32) + s_ref[...].astype(F32)).astype(o_ref.dtype)

    blk = pl.BlockSpec((4, tr, tc), lambda i, j, c_ref: (0, i, j))
    return pl.pallas_call(
        body, name=name, out_shape=_sds((4, r, c), parts.dtype),
        grid_spec=pltpu.PrefetchScalarGridSpec(
            num_scalar_prefetch=1, grid=(r // tr, c // tc),
            in_specs=[pl.BlockSpec((None, 4, tr, tc), lambda i, j, c_ref: (c_ref[0], 0, i, j)), blk], out_specs=blk),
        compiler_params=_params(("parallel", "parallel")),
    )(my_c, parts, from_sibling)


def _to_chips_start(sums, name):
    n = len(sums)
    lands = [lax.empty((3,) + s.shape[1:], s.dtype) for s in sums]

    def body(*refs):
        src, land = refs[:n], refs[n:2 * n]
        outs = refs[2 * n:]
        for i in range(n):
            for k, rel in enumerate(OTHER_CHIPS):
                to = _peer(rel)
                _rcopy(src[i].at[_chip(to)], land[i].at[k], outs[4 * i].at[k], outs[4 * i + 1].at[k], to).start()

    out_shape, out_specs, aliases = [], [], {}
    for i, s in enumerate(sums):
        out_shape += [pltpu.SemaphoreType.DMA((3,)), pltpu.SemaphoreType.DMA((3,)), pltpu.HBM(s.shape, s.dtype),
                      pltpu.HBM((3,) + s.shape[1:], s.dtype)]
        out_specs += [SEM_SPEC, SEM_SPEC, HBM_SPEC, HBM_SPEC]
        aliases[i] = 4 * i + 2
        aliases[n + i] = 4 * i + 3
    res = _split_call(body, name, [_in_hbm(s) for s in sums] + [_in_hbm(l) for l in lands], [HBM_SPEC] * (2 * n),
                      out_shape, out_specs, aliases)
    return [tuple(res[4 * i:4 * i + 4]) for i in range(n)], res[-1]


def _to_chips_finish(started, after, name):
    n = len(started)

    def body(*refs):
        for i in range(n):
            send, recv, src, land = refs[4 * i:4 * i + 4]
            for k, rel in enumerate(OTHER_CHIPS):
                cp = _rcopy(src.at[0], land.at[k], send.at[k], recv.at[k], _peer(rel))
                cp.wait_send()
                cp.wait_recv()

    ins, in_specs, out_shape, aliases = [], [], [], {}
    for i, st in enumerate(started):
        ins += [st[0], st[1], _in_hbm(st[2]), _in_hbm(st[3])]
        in_specs += [SEM_SPEC, SEM_SPEC, HBM_SPEC, HBM_SPEC]
        out_shape += [pltpu.HBM(st[2].shape, st[2].dtype), pltpu.HBM(st[3].shape, st[3].dtype)]
        aliases[4 * i + 2] = 2 * i
        aliases[4 * i + 3] = 2 * i + 1
    res = _split_call(body, name, ins + [after], in_specs + [ANY_SPEC], out_shape, [HBM_SPEC] * (2 * n), aliases)
    return [(res[2 * i], res[2 * i + 1]) for i in range(n)]


def _reduce_small(v, name, deps=()):
    _, r, c = v.shape

    def body(v_ref, o_ref, land, send_sems, recv_sems):
        me = _slot(_me())
        copies = []
        for j in range(1, NDEV):
            to = _peer(j)
            cp = _rcopy(v_ref.at[_slot(to)], land.at[me], send_sems.at[j - 1], recv_sems.at[j - 1], to)
            cp.start()
            copies.append(cp)
        land[me] = v_ref[me]
        for cp in copies:
            cp.wait()
        acc = land[0]
        for k in range(1, NDEV):
            acc = acc + land[k]
        o_ref[...] = acc

    vm = pl.BlockSpec(memory_space=pltpu.VMEM)
    body, dep_ins, dep_specs = _after(body, 1, deps)
    return pl.pallas_call(
        body, name=name, out_shape=_sds((r, c), F32), in_specs=[vm] + dep_specs, out_specs=vm,
        scratch_shapes=[pltpu.VMEM((NDEV, r, c), F32), pltpu.SemaphoreType.DMA((NDEV - 1,)),
                        pltpu.SemaphoreType.DMA((NDEV - 1,))],
        compiler_params=_params(),
    )(v, *dep_ins)


def _matmul(a, b, *, mode, tm, tn, tk, name, out_dtype=F32, epi=None, extra=None, b_slots=False, out_slots=False,
            deps=(), col_block=0, into=None, n_total=None):
    b_pair = b if isinstance(b, tuple) else None
    if b_pair:
        assert mode == "nt" and tk == 2 * b[0].shape[1] == a.shape[1] and not b_slots
        b = b[0]
    slot_w = b.shape[-1] if b_slots else None
    if mode == "nn":
        M, K = a.shape
        N = NDEV * slot_w if b_slots else b.shape[1]
    elif mode == "tn":
        K, M = a.shape
        N = b.shape[1]
    else:
        M, K = a.shape
        N = b.shape[-2]
        if b_slots:
            assert K == NDEV * slot_w and tk % slot_w == 0
    if mode == "nn" and b_slots:
        assert tn == slot_w
    if out_slots == "cols":
        assert tn * NDEV == N
    if out_slots == "rows":
        assert (M // NDEV) % tm == 0
    assert M % tm == 0 and N % tn == 0 and K % tk == 0, (name, M, N, K, tm, tn, tk)
    nk = K // tk
    dims = {"nn": ((1,), (0,)), "tn": ((0,), (0,)), "nt": ((1,), (1,))}[mode]

    if mode == "tn":
        a_spec = pl.BlockSpec((tk, tm), lambda i, j, k: (k, i))
    else:
        a_spec = pl.BlockSpec((tm, tk), lambda i, j, k: (i, k))
    if b_pair:
        b_spec = pl.BlockSpec((tn, tk // 2), lambda i, j, k: (j, 0))
    elif mode == "nt":
        b_spec = (pl.BlockSpec((tk // slot_w, tn, slot_w), lambda i, j, k: (k, j, 0)) if b_slots
                  else pl.BlockSpec((tn, tk), lambda i, j, k: (j, k)))
    else:
        b_spec = (pl.BlockSpec((None, tk, tn), lambda i, j, k: (j, k, 0)) if b_slots
                  else pl.BlockSpec((tk, tn), lambda i, j, k: (k, j)))
    tile = pl.BlockSpec((tm, tn), lambda i, j, k: (i, j + col_block))
    if out_slots == "cols":
        out_spec = pl.BlockSpec((None, None, tm, tn), lambda i, j, k: (j % 2, j // 2, i, 0))
        out_shape = _sds((2, 4, M, tn), out_dtype)
    elif out_slots == "rows":
        per = M // NDEV // tm
        out_spec = pl.BlockSpec((None, None, tm, tn), lambda i, j, k: ((i // per) % 2, (i // per) // 2, i % per, j))
        out_shape = _sds((2, 4, M // NDEV, N), out_dtype)
    else:
        out_spec, out_shape = tile, _sds((M, n_total or N), out_dtype)
    ins, in_specs = [a, b], [a_spec, b_spec]
    if b_pair:
        ins.append(b_pair[1])
        in_specs.append(b_spec)
    n_b = len(ins) - 1
    if epi in ("add", "dz"):
        ins.append(extra)
        in_specs.append(tile)
    aliases = {}
    if into is not None:
        aliases[len(ins)] = 0
        ins.append(into)
        in_specs.append(ANY_SPEC)
    if epi == "relu2":
        out_specs, out_shapes = [tile, tile], [_sds((M, N), F32), _sds((M, N), BF16)]
    else:
        out_specs, out_shapes = out_spec, out_shape
    n_in = len(ins)

    def body(*refs):
        outs = refs[n_in:-1] if nk > 1 else refs[n_in:]
        extra_ref = refs[1 + n_b]

        def finish(p):
            if epi is None:
                outs[0][...] = p.astype(out_dtype)
            elif epi == "add":
                outs[0][...] = (p + extra_ref[...]).astype(out_dtype)
            elif epi == "relu2":
                outs[0][...] = p
                rz = jnp.maximum(p, 0.0)
                outs[1][...] = (rz * rz).astype(BF16)
            else:
                outs[0][...] = (p * (2.0 * jnp.maximum(extra_ref[...], 0.0))).astype(out_dtype)

        def product():
            av = refs[0][...].astype(BF16)
            if b_pair:
                half = tk // 2
                return (lax.dot_general(av[:, :half], refs[1][...], (dims, ((), ())), preferred_element_type=F32)
                        + lax.dot_general(av[:, half:], refs[2][...], (dims, ((), ())), preferred_element_type=F32))
            if mode == "nt" and b_slots:
                return sum(lax.dot_general(av[:, s * slot_w:(s + 1) * slot_w], refs[1][s], (dims, ((), ())),
                                           preferred_element_type=F32) for s in range(tk // slot_w))
            return lax.dot_general(av, refs[1][...].astype(BF16), (dims, ((), ())), preferred_element_type=F32)

        if nk == 1:
            finish(product())
            return
        acc = refs[-1]
        k = pl.program_id(2)

        @pl.when(k == 0)
        def _():
            acc[...] = jnp.zeros_like(acc)

        acc[...] += product()

        @pl.when(k == nk - 1)
        def _():
            finish(acc[...])

    body, dep_ins, dep_specs = _after(body, n_in, deps)
    return pl.pallas_call(
        body, name=name, grid=(M // tm, N // tn, nk),
        in_specs=in_specs + dep_specs, out_specs=out_specs, out_shape=out_shapes, input_output_aliases=aliases,
        scratch_shapes=[pltpu.VMEM((tm, tn), F32)] if nk > 1 else [],
        compiler_params=_params(("parallel", "parallel", "arbitrary")),
    )(*ins, *dep_ins)


ROWS = 352


def _rmsnorm_fwd(h, w, name, deps=()):
    def body(h_ref, w_ref, u_ref):
        x = h_ref[...]
        rstd = lax.rsqrt(jnp.mean(x * x, axis=-1, keepdims=True) + EPS)
        u_ref[...] = (x * rstd * w_ref[...]).astype(BF16)

    row = pl.BlockSpec((ROWS, D), lambda i: (i, 0))
    body, dep_ins, dep_specs = _after(body, 2, deps)
    return pl.pallas_call(
        body, name=name, grid=(LP // ROWS,), in_specs=[row, pl.BlockSpec((1, D), lambda i: (0, 0))] + dep_specs,
        out_specs=row, out_shape=_sds((LP, D), BF16), compiler_params=_params(("parallel",)),
    )(h, w, *dep_ins)


TOKEN_ROWS = 512


def _rmsnorm_bwd_input(h, w, du, dres, name, deps=()):
    def math(h_ref, w_ref, du_ref, dres_ref):
        x = h_ref[...]
        rstd = lax.rsqrt(jnp.mean(x * x, axis=-1, keepdims=True) + EPS)
        xhat = x * rstd
        dy = du_ref[...]
        dxh = dy * w_ref[...]
        dh = dres_ref[...] + rstd * (dxh - xhat * jnp.mean(dxh * xhat, axis=-1, keepdims=True))
        return dh, jnp.sum(dy * xhat, axis=0, keepdims=True)

    def body(h_ref, w_ref, du_ref, dres_ref, hh_ref, duh_ref, dresh_ref, dx_ref, dhead_ref, gw_ref):
        dx_ref[...], part = math(h_ref, w_ref, du_ref, dres_ref)

        @pl.when(pl.program_id(0) == 0)
        def _():
            dhead_ref[...], head = math(hh_ref, w_ref, duh_ref, dresh_ref)
            gw_ref[...] = part + head

        @pl.when(pl.program_id(0) > 0)
        def _():
            gw_ref[...] += part

    rows = pl.BlockSpec((pl.Element(TOKEN_ROWS), pl.Element(D)),
                        lambda i: (pl.multiple_of(ROW_X + TOKEN_ROWS * i, 8), 0))
    head = pl.BlockSpec((ROW_X, D), lambda i: (0, 0))
    vec = pl.BlockSpec((1, D), lambda i: (0, 0))
    body, dep_ins, dep_specs = _after(body, 7, deps)
    return pl.pallas_call(
        body, name=name, grid=(SEQ // TOKEN_ROWS,),
        in_specs=[rows, vec, rows, rows, head, head, head] + dep_specs,
        out_specs=[pl.BlockSpec((TOKEN_ROWS, D), lambda i: (i, 0)), head, vec],
        out_shape=[_sds((SEQ, D), F32), _sds((ROW_X, D), F32), _sds((1, D), F32)],
        compiler_params=_params(("arbitrary",)),
    )(h, w, du, dres, h, du, dres, *dep_ins)


def _rmsnorm_bwd(h, w, du, dres, name, deps=()):
    def body(h_ref, w_ref, du_ref, dres_ref, dh_ref, dhb_ref, gw_ref):
        x = h_ref[...]
        rstd = lax.rsqrt(jnp.mean(x * x, axis=-1, keepdims=True) + EPS)
        xhat = x * rstd
        dy = du_ref[...]
        dxh = dy * w_ref[...]
        dh = dres_ref[...] + rstd * (dxh - xhat * jnp.mean(dxh * xhat, axis=-1, keepdims=True))
        dh_ref[...] = dh
        dhb_ref[...] = dh.astype(BF16)
        part = jnp.sum(dy * xhat, axis=0, keepdims=True)

        @pl.when(pl.program_id(0) == 0)
        def _():
            gw_ref[...] = part

        @pl.when(pl.program_id(0) > 0)
        def _():
            gw_ref[...] += part

    row = pl.BlockSpec((ROWS, D), lambda i: (i, 0))
    vec = pl.BlockSpec((1, D), lambda i: (0, 0))
    body, dep_ins, dep_specs = _after(body, 4, deps)
    return pl.pallas_call(
        body, name=name, grid=(LP // ROWS,), in_specs=[row, vec, row, row] + dep_specs, out_specs=[row, row, vec],
        out_shape=[_sds((LP, D), F32), _sds((LP, D), BF16), _sds((1, D), F32)],
        compiler_params=_params(("arbitrary",)),
    )(h, w, du, dres, *dep_ins)


def _loss_head(h2, wf, target):
    def body(h_ref, w_ref, t_ref, dh_ref, dhb_ref, sq_ref, gw_ref):
        i = pl.program_id(0)

        @pl.when(i == 0)
        def _():
            sq_ref[...] = jnp.zeros_like(sq_ref)
            gw_ref[...] = jnp.zeros_like(gw_ref)

        def rows(t, live):
            x = h_ref[...]
            rstd = lax.rsqrt(jnp.mean(x * x, axis=-1, keepdims=True) + EPS)
            xhat = x * rstd
            w = w_ref[...]
            err = xhat * w - t
            if live is not None:
                err = jnp.where(live, err, 0.0)
            sq_ref[...] += jnp.sum(err * err, axis=0, keepdims=True)
            dy = err * (1.0 / D)
            gw_ref[...] += jnp.sum(dy * xhat, axis=0, keepdims=True)
            dxh = dy * w
            dh = rstd * (dxh - xhat * jnp.mean(dxh * xhat, axis=-1, keepdims=True))
            dh_ref[...] = dh
            dhb_ref[...] = dh.astype(BF16)

        @pl.when(i == 0)
        def _():
            rid = lax.broadcasted_iota(jnp.int32, (ROWS, D), 0)
            rows(pltpu.roll(t_ref[...], ROW_X, 0), rid >= ROW_X)

        @pl.when(i > 0)
        def _():
            rows(t_ref[...], None)

    row = pl.BlockSpec((ROWS, D), lambda i: (i, 0))
    vec = pl.BlockSpec((1, D), lambda i: (0, 0))
    tgt = pl.BlockSpec((pl.Element(ROWS), pl.Element(D)),
                       lambda i: (pl.multiple_of(jnp.maximum(ROWS * i - ROW_X, 0), 8), 0))
    return pl.pallas_call(
        body, name="loss_head", grid=(LP // ROWS,),
        in_specs=[row, vec, tgt],
        out_specs=[row, row, vec, vec],
        out_shape=[_sds((LP, D), F32), _sds((LP, D), BF16), _sds((1, D), F32), _sds((1, D), F32)],
        compiler_params=_params(("arbitrary",)),
    )(h2, wf, target)


def _dot(a, b, dims):
    return lax.dot_general(a, b, (dims, ((), ())), preferred_element_type=F32)


NN, TN, NT = ((1,), (0,)), ((0,), (0,)), ((1,), (1,))


def _tri_sum(t, x):
    hi = x.astype(BF16)
    r1 = x - hi.astype(F32)
    mid = r1.astype(BF16)
    lo = (r1 - mid.astype(F32)).astype(BF16)
    return _dot(t, hi, NN) + _dot(t, mid, NN) + _dot(t, lo, NN)


def _gla_gates(glr_ref, gw2_ref, gb_ref, rows, row0):
    g_raw = _dot(glr_ref[rows, :].astype(BF16), gw2_ref[...], NN) + gb_ref[...]
    logsig = jnp.minimum(g_raw, 0.0) - jnp.log(1.0 + jnp.exp(-jnp.abs(g_raw)))
    rid = row0 + lax.broadcasted_iota(jnp.int32, g_raw.shape, 0)
    live = rid >= PAD
    return g_raw, jnp.where(live, logsig / TAU, 0.0), live


def _tri_masks():
    r = lax.broadcasted_iota(jnp.int32, (CHUNK, CHUNK), 0)
    c = lax.broadcasted_iota(jnp.int32, (CHUNK, CHUNK), 1)
    return r >= c


def _gla_specs(rev):
    n = NCH // CPS
    R = CPS * CHUNK
    st = (lambda s: n - 1 - s) if rev else (lambda s: s)
    return R, n, st, [
        pl.BlockSpec((R, KW), lambda s: (st(s), 0)),
        pl.BlockSpec((R, KW), lambda s: (st(s), 1)),
        pl.BlockSpec((R, GW), lambda s: (st(s), 1)),
        pl.BlockSpec((R, GW), lambda s: (st(s), 2)),
        pl.BlockSpec((R, 128), lambda s: (st(s), GLR_BLK)),
    ]


def _gla_fwd(proj, gw2p, gate_b, gnw, deps=()):
    R, n, st, pspecs = _gla_specs(False)

    def body(q_ref, k_ref, v_ref, r_ref, glr_ref, gw2_ref, gb_ref, gnw_ref, og_ref, o_ref, st_ref, state):
        s = pl.program_id(0)

        @pl.when(s == 0)
        def _():
            state[...] = jnp.zeros_like(state)

        causal = _tri_masks()
        tri = causal.astype(BF16)
        for c in range(CPS):
            rows = slice(c * CHUNK, (c + 1) * CHUNK)
            _, logg, _ = _gla_gates(glr_ref, gw2_ref, gb_ref, rows, s * R + c * CHUNK)
            G = _tri_sum(tri, logg)
            g_last = G[CHUNK - 1:CHUNK, :]
            q_dec = (q_ref[rows, :] * (DK ** -0.5) * jnp.exp(G)).astype(BF16)
            kk = k_ref[rows, :]
            k_inv = (kk * jnp.exp(-G)).astype(BF16)
            k_end = (kk * jnp.exp(g_last - G)).astype(BF16)
            decay = jnp.exp(g_last)
            for h in range(H):
                lk = slice(h * DK, (h + 1) * DK)
                lv = slice(h * DV, (h + 1) * DV)
                v = v_ref[rows, lv].astype(BF16)
                S = state[h]
                st_ref[c, h] = S
                A = jnp.where(causal, _dot(q_dec[:, lk], k_inv[:, lk], NT), 0.0).astype(BF16)
                o = _dot(A, v, NN) + _dot(q_dec[:, lk], S.astype(BF16), NT)
                state[h] = decay[:, lk] * S + _dot(v, k_end[:, lk], TN)
                o_ref[rows, lv] = o
                on = o * lax.rsqrt(jnp.mean(o * o, axis=-1, keepdims=True) + EPS) * gnw_ref[...]
                rr = r_ref[rows, lv]
                og_ref[rows, lv] = (on * (rr * jax.nn.sigmoid(rr))).astype(BF16)

    full = lambda shape: pl.BlockSpec(shape, lambda s: (0,) * len(shape))
    body, dep_ins, dep_specs = _after(body, 8, deps)
    return pl.pallas_call(
        body, name="gla_fwd", grid=(n,),
        in_specs=pspecs + [full((128, KW)), full((1, KW)), full((1, DV))] + dep_specs,
        out_specs=[pl.BlockSpec((R, GW), lambda s: (s, 0)), pl.BlockSpec((R, GW), lambda s: (s, 0)),
                   pl.BlockSpec((CPS, H, DV, DK), lambda s: (s, 0, 0, 0))],
        out_shape=[_sds((LP, GW + PW), BF16), _sds((LP, GW), F32), _sds((NCH, H, DV, DK), F32)],
        scratch_shapes=[pltpu.VMEM((H, DV, DK), F32)],
        compiler_params=_params(("arbitrary",)),
    )(proj, proj, proj, proj, proj, gw2p, gate_b, gnw, *dep_ins)


def _gla_bwd(proj, dmixed, o_saved, st_saved, gw2p, gate_b, gnw, deps=()):
    R, n, st, pspecs = _gla_specs(True)

    def body(q_ref, k_ref, v_ref, r_ref, glr_ref, dog_ref, o_ref, st_ref, gw2_ref, gb_ref, gnw_ref,
             dqkvr_ref, dglr_ref, ggn_ref, ggb_ref, ggw_ref, gstate):
        s = pl.program_id(0)

        @pl.when(s == 0)
        def _():
            gstate[...] = jnp.zeros_like(gstate)
            ggn_ref[...] = jnp.zeros_like(ggn_ref)
            ggb_ref[...] = jnp.zeros_like(ggb_ref)
            ggw_ref[...] = jnp.zeros_like(ggw_ref)

        causal = _tri_masks()
        tri = causal.astype(BF16)
        tri_up = (lax.broadcasted_iota(jnp.int32, (CHUNK, CHUNK), 0)
                  <= lax.broadcasted_iota(jnp.int32, (CHUNK, CHUNK), 1)).astype(BF16)
        gnw = gnw_ref[...]
        for c in reversed(range(CPS)):
            rows = slice(c * CHUNK, (c + 1) * CHUNK)
            g_raw, logg, live = _gla_gates(glr_ref, gw2_ref, gb_ref, rows, (n - 1 - s) * R + c * CHUNK)
            G = _tri_sum(tri, logg)
            g_last = G[CHUNK - 1:CHUNK, :]
            e_g, e_gi, e_end = jnp.exp(G), jnp.exp(-G), jnp.exp(g_last - G)
            q_dec = q_ref[rows, :] * (DK ** -0.5) * e_g
            kk = k_ref[rows, :]
            k_inv, k_end = kk * e_gi, kk * e_end
            q_dec_b, k_inv_b, k_end_b = q_dec.astype(BF16), k_inv.astype(BF16), k_end.astype(BF16)
            decay = jnp.exp(g_last)
            d_g, d_gl = [], []
            for h in range(H):
                lk = slice(h * DK, (h + 1) * DK)
                lv = slice(h * DV, (h + 1) * DV)
                o = o_ref[rows, lv]
                rr = r_ref[rows, lv]
                dog = dog_ref[rows, lv]
                rstd = lax.rsqrt(jnp.mean(o * o, axis=-1, keepdims=True) + EPS)
                ohat = o * rstd
                sr = jax.nn.sigmoid(rr)
                don = dog * (rr * sr)
                dqkvr_ref[rows, 2 * KW + GW + h * DV:2 * KW + GW + (h + 1) * DV] = (
                    dog * (ohat * gnw) * (sr * (1.0 + rr * (1.0 - sr)))).astype(BF16)
                ggn_ref[...] += jnp.sum(don * ohat, axis=0, keepdims=True)
                dohat = don * gnw
                do = (rstd * (dohat - ohat * jnp.mean(dohat * ohat, axis=-1, keepdims=True))).astype(BF16)
                v = v_ref[rows, lv].astype(BF16)
                S = st_ref[c, h]
                gS = gstate[h]
                S_b, gS_b = S.astype(BF16), gS.astype(BF16)
                qd, ki, ke = q_dec_b[:, lk], k_inv_b[:, lk], k_end_b[:, lk]
                A = jnp.where(causal, _dot(qd, ki, NT), 0.0).astype(BF16)
                dA = jnp.where(causal, _dot(do, v, NT), 0.0).astype(BF16)
                dv = _dot(A, do, TN) + _dot(ke, gS_b, NT)
                dq_dec = _dot(dA, ki, NN) + _dot(do, S_b, NN)
                dk_inv = _dot(dA, qd, TN)
                dk_end = _dot(v, gS_b, NN)
                d_decay = jnp.sum(gS * S, axis=0, keepdims=True)
                gstate[h] = decay[:, lk] * gS + _dot(do, qd, TN)
                dqkvr_ref[rows, lk] = (dq_dec * e_g[:, lk] * (DK ** -0.5)).astype(BF16)
                dqkvr_ref[rows, KW + h * DK:KW + (h + 1) * DK] = (
                    dk_inv * e_gi[:, lk] + dk_end * e_end[:, lk]).astype(BF16)
                dqkvr_ref[rows, 2 * KW + h * DV:2 * KW + (h + 1) * DV] = dv.astype(BF16)
                ke_prod = dk_end * k_end[:, lk]
                d_g.append(dq_dec * q_dec[:, lk] - dk_inv * k_inv[:, lk] - ke_prod)
                d_gl.append(jnp.sum(ke_prod, axis=0, keepdims=True) + d_decay * decay[:, lk])
            dlogg = _tri_sum(tri_up, jnp.concatenate(d_g, axis=1)) + jnp.concatenate(d_gl, axis=1)
            dg_raw = jnp.where(live, dlogg * (1.0 / TAU) * jax.nn.sigmoid(-g_raw), 0.0)
            ggb_ref[...] += jnp.sum(dg_raw, axis=0, keepdims=True)
            dg_b = dg_raw.astype(BF16)
            ggw_ref[...] += _dot(glr_ref[rows, :].astype(BF16), dg_b, TN)
            dglr_ref[rows, :] = _dot(dg_b, gw2_ref[...], NT).astype(BF16)

    full = lambda shape: pl.BlockSpec(shape, lambda s: (0,) * len(shape))
    body, dep_ins, dep_specs = _after(body, 11, deps)
    return pl.pallas_call(
        body, name="gla_bwd", grid=(n,),
        in_specs=pspecs + [pl.BlockSpec((R, GW), lambda s: (st(s), 0)), pl.BlockSpec((R, GW), lambda s: (st(s), 0)),
                           pl.BlockSpec((CPS, H, DV, DK), lambda s: (st(s), 0, 0, 0)),
                           full((128, KW)), full((1, KW)), full((1, DV))] + dep_specs,
        out_specs=[pl.BlockSpec((R, 2 * KW + 2 * GW), lambda s: (st(s), 0)), pl.BlockSpec((R, 128), lambda s: (st(s), 0)),
                   full((1, DV)), full((1, KW)), full((128, KW))],
        out_shape=[_sds((LP, D_INP), BF16), _sds((LP, 128), BF16),
                   _sds((1, DV), F32), _sds((1, KW), F32), _sds((128, KW), F32)],
        scratch_shapes=[pltpu.VMEM((H, DV, DK), F32)],
        compiler_params=_params(("arbitrary",)),
    )(proj, proj, proj, proj, proj, dmixed, o_saved, st_saved, gw2p, gate_b, gnw, *dep_ins)


def _pool_pre(x, win, rid):
    s, step = x, 1
    while step < win:
        s = s + pltpu.roll(s, step, 0)
        step *= 2
    cnt = jnp.clip(rid - (PAD - 1), 1, win).astype(F32)
    live = rid >= PAD
    return jnp.where(live, s / cnt - x, 0.0), cnt, live


def _pool_fwd(proj, pool_w, pool_scale, mixed):
    def body(pu_ref, w_ref, sc_ref, _, o_ref):
        rid = lax.broadcasted_iota(jnp.int32, (LP, GC), 0)
        for g, win in enumerate(WINDOWS):
            @pl.when(pl.program_id(0) == g)
            def _():
                y, _, _ = _pool_pre(pu_ref[...], win, rid)
                o_ref[...] = (_dot(y.astype(BF16), w_ref[...], NN) * sc_ref[...]).astype(BF16)

    col = lambda base: pl.BlockSpec((LP, GC), lambda g: (0, base + g))
    return pl.pallas_call(
        body, name="pool_fwd", grid=(len(WINDOWS),),
        in_specs=[col(POOL_BLK), pl.BlockSpec((None, GC, GC), lambda g: (g, 0, 0)),
                  pl.BlockSpec((1, GC), lambda g: (0, g)), ANY_SPEC],
        out_specs=col(GW // GC), out_shape=_sds(mixed.shape, BF16), input_output_aliases={3: 0},
        compiler_params=_params(("parallel",)),
    )(proj, pool_w, pool_scale, mixed)


def _pool_bwd(proj, dmixed, pool_w, pool_scale, dproj):
    def body(pu_ref, do_ref, w_ref, sc_ref, _, dpu_ref, dw_ref, dsc_ref):
        rid = lax.broadcasted_iota(jnp.int32, (LP, GC), 0)
        for g, win in enumerate(WINDOWS):
            @pl.when(pl.program_id(0) == g)
            def _():
                y, cnt, live = _pool_pre(pu_ref[...], win, rid)
                y_b = y.astype(BF16)
                w = w_ref[...]
                do = do_ref[...]
                dsc_ref[...] = jnp.sum(do * _dot(y_b, w, NN), axis=0, keepdims=True)
                dyw = (do * sc_ref[...]).astype(BF16)
                dw_ref[...] = _dot(y_b, dyw, TN)
                dy = jnp.where(live, _dot(dyw, w, NT), 0.0)
                s, step = dy / cnt, 1
                while step < win:
                    s = s + pltpu.roll(s, LP - step, 0)
                    step *= 2
                dpu_ref[...] = (s - dy).astype(BF16)

    col = lambda base: pl.BlockSpec((LP, GC), lambda g: (0, base + g))
    mat = pl.BlockSpec((None, GC, GC), lambda g: (g, 0, 0))
    vec = pl.BlockSpec((1, GC), lambda g: (0, g))
    return pl.pallas_call(
        body, name="pool_bwd", grid=(len(WINDOWS),),
        in_specs=[col(POOL_BLK), col(GW // GC), mat, vec, ANY_SPEC], out_specs=[col(POOL_BLK), mat, vec],
        out_shape=[_sds(dproj.shape, BF16), _sds((4, GC, GC), F32), _sds((1, PW), F32)],
        input_output_aliases={4: 0}, compiler_params=_params(("parallel",)),
    )(proj, dmixed, pool_w, pool_scale, dproj)


def _adamw_math(w, g, m, v):
    m = B1 * m + (1.0 - B1) * g
    v = B2 * v + (1.0 - B2) * (g * g)
    m_hat = m * (1.0 / (1.0 - B1 ** STEP))
    v_hat = v * (1.0 / (1.0 - B2 ** STEP))
    return -LR * (m_hat / (jnp.sqrt(v_hat) + AEPS) + WD * w), m, v


def _adamw_landed(sums, landed, my_chip, w, m, v, rows, name, cols=None):
    _, r, c = w.shape

    def body(chip_ref, s_ref, l_ref, w_ref, m_ref, v_ref, g_ref, d_ref, mo_ref, vo_ref):
        g = s_ref[...].astype(F32)
        for k in range(3):
            g = g + l_ref[k].astype(F32)
        g_ref[...] = g
        d_ref[...], mo_ref[...], vo_ref[...] = _adamw_math(w_ref[...], g, m_ref[...], v_ref[...])

    cols = cols or c
    blk = pl.BlockSpec((None, rows, cols), lambda i, j, chip_ref: (0, i, j))
    return pl.pallas_call(
        body, name=name, out_shape=[_sds((1, r, c), F32)] * 4,
        grid_spec=pltpu.PrefetchScalarGridSpec(
            num_scalar_prefetch=1, grid=(r // rows, c // cols),
            in_specs=[pl.BlockSpec((None, rows, cols), lambda i, j, chip_ref: (chip_ref[0], i, j)),
                      pl.BlockSpec((3, rows, cols), lambda i, j, chip_ref: (0, i, j)), blk, blk, blk],
            out_specs=[blk] * 4),
        compiler_params=_params(("parallel", "parallel")),
    )(my_chip, sums, landed, w, m, v)


def _adamw_small(g, w, m, v):
    def body(g_ref, w_ref, m_ref, v_ref, d_ref, mo_ref, vo_ref):
        d_ref[...], mo_ref[...], vo_ref[...] = _adamw_math(w_ref[...], g_ref[...], m_ref[...], v_ref[...])

    return pl.pallas_call(body, name="adamw_small", out_shape=[_sds(w.shape, F32)] * 3)(g, w, m, v)


SMALL_REPL = (("norm1_w", D), ("norm2_w", D), ("final_norm_w", D), ("pool_scale", PW), ("gate_b", KW),
              ("gla_norm_w", DV))


def _pack_rows(vecs, rows):
    flat = jnp.concatenate([jnp.ravel(v) for v in vecs])
    return jnp.pad(flat, (0, rows * 1024 - flat.shape[0])).reshape(rows, 1024)


def kernel(x, meta_tokens, norm1_w, w_in, gate_w2, gate_b, gla_norm_w, pool_w, pool_scale, w_out, norm2_w, mlp_w1, mlp_w2, final_norm_w, loss_target, m_meta_tokens, m_norm1_w, m_w_in, m_gate_w2, m_gate_b, m_gla_norm_w, m_pool_w, m_pool_scale, m_w_out, m_norm2_w, m_mlp_w1, m_mlp_w2, m_final_norm_w, v_meta_tokens, v_norm1_w, v_w_in, v_gate_w2, v_gate_b, v_gla_norm_w, v_pool_w, v_pool_scale, v_w_out, v_norm2_w, v_mlp_w1, v_mlp_w2, v_final_norm_w):
    W = dict(meta_tokens=meta_tokens, norm1_w=norm1_w, w_in=w_in, gate_w2=gate_w2, gate_b=gate_b,
             gla_norm_w=gla_norm_w, pool_w=pool_w, pool_scale=pool_scale, w_out=w_out, norm2_w=norm2_w,
             mlp_w1=mlp_w1, mlp_w2=mlp_w2, final_norm_w=final_norm_w)
    Mo = dict(meta_tokens=m_meta_tokens, norm1_w=m_norm1_w, w_in=m_w_in, gate_w2=m_gate_w2, gate_b=m_gate_b,
              gla_norm_w=m_gla_norm_w, pool_w=m_pool_w, pool_scale=m_pool_scale, w_out=m_w_out, norm2_w=m_norm2_w,
              mlp_w1=m_mlp_w1, mlp_w2=m_mlp_w2, final_norm_w=m_final_norm_w)
    Vo = dict(meta_tokens=v_meta_tokens, norm1_w=v_norm1_w, w_in=v_w_in, gate_w2=v_gate_w2, gate_b=v_gate_b,
              gla_norm_w=v_gla_norm_w, pool_w=v_pool_w, pool_scale=v_pool_scale, w_out=v_w_out, norm2_w=v_norm2_w,
              mlp_w1=v_mlp_w1, mlp_w2=v_mlp_w2, final_norm_w=v_final_norm_w)

    ex = _Exchange(dict(small=_pack_rows([meta_tokens, gate_w2[0]], 8), w_in=w_in[0].T.astype(BF16)),
                   dict(w_out=w_out[0], pool_w=pool_w[0].reshape(4 * 32, GC), mlp_w1=mlp_w1[0],
                        mlp_w2a=mlp_w2[0][:, :D // 2], mlp_w2b=mlp_w2[0][:, D // 2:]))
    tr = lambda a: a[0].T[None]
    win_t, m_win_t, v_win_t = tr(w_in), tr(m_w_in), tr(v_w_in)
    ex.early = [win_t, m_win_t, v_win_t]
    step = _layer_step(x[0], loss_target[0], ex, norm1_w, gate_b, gla_norm_w, pool_scale, norm2_w,
                       final_norm_w.reshape(1, D))
    grad_x = step["dx"][None]

    last = step["dx"]
    out = {}
    for group in ("down", "up", "mix"):
        for k, (sums, landed) in ex.grad_finish(group, last).items():
            out[k] = _adamw_landed(sums, landed, ex.my_chip, W[k], Mo[k], Vo[k], SHARD_ROWS[k], "adamw_" + k)
            last = out[k][1]

    loss_part = 0.5 * jnp.sum(step["sq"]) / D
    to_all = _pack_rows([step[k] for k, _ in SMALL_REPL] + [loss_part], 8)
    cols = lambda g: g.reshape(g.shape[0], NDEV, -1).transpose(1, 0, 2).reshape(NDEV, -1, 1024)
    packed = jnp.concatenate([jnp.broadcast_to(to_all, (NDEV, 8, 1024)), cols(step["dhead"][PAD:]),
                              cols(step["gate_w2"][:RANK]), jnp.zeros((NDEV, 3, 1024), F32)], axis=1)
    red = _reduce_small(packed, "reduce_small", deps=[last])
    loss = red[7, 768]

    done = ex.grad_finish("w_in", red)
    poolw3 = lambda a: a.reshape(1, 4 * 32, GC)
    res = _adamw_landed(*done["pool_w"], ex.my_chip, poolw3(pool_w), poolw3(m_pool_w), poolw3(v_pool_w),
                        SHARD_ROWS["pool_w"], "adamw_pool_w")
    out["pool_w"] = [a.reshape(pool_w.shape) for a in res]
    res = _adamw_landed(*done["w_in"], ex.my_chip, win_t, m_win_t, v_win_t, D_IN // NDEV, "adamw_w_in", cols=256)
    out["w_in"] = [a[0].T[None] for a in res]

    def small_pack(P):
        return jnp.concatenate([_pack_rows([P[k] for k, _ in SMALL_REPL], 8),
                                _pack_rows([P["meta_tokens"], P["gate_w2"]], 8)], axis=0)

    g_small = red.at[7, 768].set(0.0)
    res_small = _adamw_small(g_small, small_pack(W), small_pack(Mo), small_pack(Vo))
    res_small = [g_small] + list(res_small)
    off = 0
    for k, nel in SMALL_REPL:
        out[k] = [a[0:8].reshape(-1)[off:off + nel].reshape(W[k].shape) for a in res_small]
        off += nel
    out["meta_tokens"] = [a[8:12].reshape(N_META, D // NDEV) for a in res_small]
    out["gate_w2"] = [a[12].reshape(1, RANK, KW // NDEV) for a in res_small]

    order = ["meta_tokens", "norm1_w", "w_in", "gate_w2", "gate_b", "gla_norm_w", "pool_w", "pool_scale", "w_out",
             "norm2_w", "mlp_w1", "mlp_w2", "final_norm_w"]
    return (loss, grad_x, *[out[k][0] for k in order], *[out[k][1] for k in order],
            *[out[k][2] for k in order], *[out[k][3] for k in order])


SHARD_ROWS = dict(w_out=256, mlp_w1=512, mlp_w2=256, pool_w=128)
SLOT_ROWS = D_IN // NDEV


def _w_in_pieces(s):
    lo, hi, out = s * SLOT_ROWS, (s + 1) * SLOT_ROWS, []
    for a, b, shift in ((0, C_GLR, 0), (C_GLR, C_GLR + RANK, PW), (C_GLR + RANK, D_IN, -RANK)):
        a, b = max(a, lo), min(b, hi)
        if a < b:
            out.append((a - lo, a + shift, b - a))
    return out


def _w_in_to_layer_order(gathered):
    def body(g_ref, o_ref):
        for s in range(NDEV):
            @pl.when(pl.program_id(0) == s)
            def _():
                for src, dst, n in _w_in_pieces(s):
                    o_ref[pl.ds(dst, n), :] = g_ref[pl.ds(src, n), :]

        @pl.when(pl.program_id(0) == 0)
        def _():
            o_ref[pl.ds(D_IN, D_INP - D_IN), :] = jnp.zeros((D_INP - D_IN, D), BF16)

    return pl.pallas_call(
        body, name="w_in_rows", grid=(NDEV,), out_shape=_sds((D_INP, D), BF16),
        in_specs=[pl.BlockSpec((None, SLOT_ROWS, D), lambda s: (s, 0, 0))],
        out_specs=pl.BlockSpec((D_INP, D), lambda s: (0, 0)), compiler_params=_params(("arbitrary",)),
    )(gathered)


def _w_in_grad_to_parts(g):
    def body(g_ref, o_ref):
        for s in range(NDEV):
            @pl.when(pl.program_id(0) == s)
            def _():
                for dst, src, n in _w_in_pieces(s):
                    o_ref[pl.ds(dst, n), :] = g_ref[pl.ds(src, n), :]

    return pl.pallas_call(
        body, name="w_in_grad_rows", grid=(NDEV,), out_shape=_sds((2, 4, SLOT_ROWS, D), BF16),
        in_specs=[pl.BlockSpec((D_INP, D), lambda s: (0, 0))],
        out_specs=pl.BlockSpec((None, None, SLOT_ROWS, D), lambda s: (s % 2, s // 2, 0, 0)),
        compiler_params=_params(("arbitrary",)),
    )(g)
C_GLR = 2 * KW + 2 * GW
GATHER_GROUPS = dict(small=("small",), w_in=("w_in",), mix=("w_out", "pool_w"), up=("mlp_w1",), down_a=("mlp_w2a",),
                     down_b=("mlp_w2b",))
GRAD_GROUPS = dict(down=("mlp_w2",), up=("mlp_w1",), mix=("w_out",), w_in=("pool_w", "w_in"))


class _Exchange:
    def __init__(self, first, rest):
        head, token = _gather_start(list(first.values()), "gather_start_first", first["small"])
        token, later = lax.optimization_barrier((token, list(rest.values())))
        tail, self.started = _gather_start([v.astype(BF16) for v in later], "gather_start_rest", token)
        self.state = dict(zip(list(first) + list(rest), head + tail))
        self.my_c = lax.axis_index("c").astype(jnp.int32).reshape(1)
        self.my_chip = (2 * lax.axis_index("x") + lax.axis_index("y")).astype(jnp.int32).reshape(1)
        self.sibling, self.chips = {}, {}

    def forward(self, group, after):
        ks = GATHER_GROUPS[group]
        fwd, token = _gather_forward([self.state[k] for k in ks], after, "gather_forward_" + group)
        self.state.update(zip(ks, fwd))
        return token

    def weights(self, group, after):
        ks = GATHER_GROUPS[group]
        g = dict(zip(ks, _gather_finish([self.state[k] for k in ks], after, "gather_finish_" + group)))
        if group == "w_in":
            return _w_in_to_layer_order(g["w_in"])
        if group == "small":
            return g["small"]
        if group == "mix":
            return (g["w_out"].reshape(D, D),
                    g["pool_w"].reshape(NDEV, 4, 32, GC).transpose(1, 0, 2, 3).reshape(4, GC, GC))
        return g["mlp_w1"] if group == "up" else g[ks[0]].reshape(DFF, D // 2)

    def grad(self, group, grads):
        parts = dict(grads)
        if group == "w_in":
            parts["w_in"] = _w_in_grad_to_parts(parts["w_in"])
            parts["pool_w"] = (parts["pool_w"].astype(BF16).reshape(4, 4, 2, 32, GC).transpose(2, 1, 0, 3, 4)
                               .reshape(2, 4, 4 * 32, GC))
        ks = GRAD_GROUPS[group]
        started, token = _to_sibling_start([parts[k] for k in ks], "grad_sibling_start_" + group)
        self.sibling[group] = started
        return token

    def grad_mid(self, group, after):
        ks = GRAD_GROUPS[group]
        both = _to_sibling_finish(self.sibling[group], after, "grad_sibling_finish_" + group)
        tile = lambda k, p: (p.shape[2], 512) if k == "w_in" else (SHARD_ROWS[k], p.shape[3])
        sums = [_chip_sum(p, s, self.my_c, tile(k, p), "chip_sum_" + k) for k, (p, s) in zip(ks, both)]
        self.chips[group], token = _to_chips_start(sums, "grad_chips_start_" + group)
        return token

    def grad_finish(self, group, after):
        done = _to_chips_finish(self.chips[group], after, "grad_chips_finish_" + group)
        return dict(zip(GRAD_GROUPS[group], done))


def _layer_step(x, target, ex, norm1_w, gate_b, gla_norm_w, pool_scale, norm2_w, final_norm_w):
    small = ex.weights("small", ex.forward("small", ex.started))
    meta_full = small[:, 0:4].reshape(NDEV, N_META, D // NDEV).transpose(1, 0, 2).reshape(N_META, D)
    gw2_full = small[:, 4].reshape(NDEV, RANK, KW // NDEV).transpose(1, 0, 2).reshape(RANK, KW)
    gw2p = jnp.pad(gw2_full, ((0, 128 - RANK), (0, 0))).astype(BF16)
    h0 = jnp.concatenate([jnp.zeros((PAD, D), F32), meta_full, x], axis=0)
    u1 = _rmsnorm_fwd(h0, norm1_w, "rmsnorm1", deps=ex.early)
    win_p = ex.weights("w_in", ex.forward("w_in", u1))
    proj = _matmul(u1, win_p, mode="nt", tm=1056, tn=1408, tk=2048, name="proj")
    tok = ex.forward("mix", proj)
    mixed, o_saved, st_saved = _gla_fwd(proj, gw2p, gate_b, gla_norm_w, deps=[tok])
    wout_f, poolw_f = ex.weights("mix", mixed)
    mixed = _pool_fwd(proj, poolw_f, pool_scale, mixed)
    h1 = _matmul(mixed, wout_f, mode="nn", tm=1056, tn=1024, tk=2048, name="mix_out", epi="add", extra=h0)
    tok = ex.forward("up", h1)
    u2 = _rmsnorm_fwd(h1, norm2_w, "rmsnorm2", deps=[tok])
    w1_g = ex.weights("up", u2)
    z, act = _matmul(u2, w1_g, mode="nn", tm=1056, tn=1024, tk=2048, name="mlp_up", epi="relu2", b_slots=True)
    w2a = ex.weights("down_a", ex.forward("down_a", act))
    h2 = _matmul(act, w2a, mode="nn", tm=1056, tn=1024, tk=2048, name="mlp_down_a", epi="add", extra=h1, n_total=D)
    w2b = ex.weights("down_b", ex.forward("down_b", h2))
    h2 = _matmul(act, w2b, mode="nn", tm=1056, tn=1024, tk=2048, name="mlp_down_b", epi="add", extra=h1, n_total=D,
                 col_block=1, into=h2)
    dh2, dh2b, sq, g_fnw = _loss_head(h2, final_norm_w, target)

    g_w2 = _matmul(act, dh2b, mode="tn", tm=512, tn=2048, tk=LP, name="d_mlp_w2", out_dtype=BF16, out_slots="rows")
    tok = ex.grad("down", dict(mlp_w2=g_w2))
    dz = _matmul(dh2b, (w2a, w2b), mode="nt", tm=1056, tn=1024, tk=2048, name="d_act", out_dtype=BF16, epi="dz",
                 extra=z, deps=[tok])
    tok = ex.grad_mid("down", dz)
    g_w1 = _matmul(u2, dz, mode="tn", tm=1024, tn=1024, tk=LP, name="d_mlp_w1", out_dtype=BF16, out_slots="cols",
                   deps=[tok])
    tok = ex.grad("up", dict(mlp_w1=g_w1))
    du2 = _matmul(dz, w1_g, mode="nt", tm=1056, tn=1024, tk=2048, name="d_u2", b_slots=True, deps=[tok])
    tok = ex.grad_mid("up", du2)
    dh1, dh1b, g_n2 = _rmsnorm_bwd(h1, norm2_w, du2, dh2, "rmsnorm2_bwd", deps=[tok])
    g_wout = _matmul(mixed, dh1b, mode="tn", tm=256, tn=2048, tk=LP, name="d_w_out", out_dtype=BF16, out_slots="rows")
    tok = ex.grad("mix", dict(w_out=g_wout))
    dmixed = _matmul(dh1b, wout_f, mode="nt", tm=1056, tn=1024, tk=2048, name="d_mixed", deps=[tok])
    tok = ex.grad_mid("mix", dmixed)
    dproj, dglr, g_gnw, g_gb, g_gw2 = _gla_bwd(proj, dmixed, o_saved, st_saved, gw2p, gate_b, gla_norm_w, deps=[tok])
    dproj, g_poolw, g_psc = _pool_bwd(proj, dmixed, poolw_f, pool_scale, dproj)
    dproj = lax.dynamic_update_slice(dproj, dglr, (0, GLR_BLK * 128))
    g_win_p = _matmul(dproj, u1, mode="tn", tm=384, tn=2048, tk=LP, name="d_w_in", out_dtype=BF16)
    tok = ex.grad("w_in", dict(pool_w=g_poolw, w_in=g_win_p))
    du1 = _matmul(dproj, win_p, mode="nn", tm=1056, tn=1024, tk=1408, name="d_u1", deps=[tok])
    tok = ex.grad_mid("w_in", du1)
    dx, dhead, g_n1 = _rmsnorm_bwd_input(h0, norm1_w, du1, dh1, "rmsnorm1_bwd", deps=[tok])
    return dict(dx=dx, dhead=dhead, sq=sq, gate_w2=g_gw2, norm1_w=g_n1, norm2_w=g_n2, final_norm_w=g_fnw, pool_scale=g_psc,
                gate_b=g_gb, gla_norm_w=g_gnw)
```

```python
import functools

import jax
import jax.numpy as jnp
from jax import lax
from jax.experimental import pallas as pl
from jax.experimental.pallas import tpu as pltpu

F32, BF16 = jnp.float32, jnp.bfloat16
MESH = pl.DeviceIdType.MESH

NDEV = 8
D = 2048
SEQ = 2048
N_META = 16
CHUNK = 64
PAD = (-N_META) % CHUNK
ROW_X = PAD + N_META
LP = ROW_X + SEQ
NCH = LP // CHUNK
H = 4
DK = 128
DV = 256
KW = H * DK
GW = H * DV
PW = 1024
RANK = 16
TAU = 16.0
WINDOWS = (2, 4, 8, 16)
GC = 256
DFF = 4 * D
EPS = 1e-6
D_IN = 2 * KW + 2 * GW + RANK + PW
D_INP = 4224
GLR_BLK = (2 * KW + 2 * GW + PW) // 128
POOL_BLK = (2 * KW + 2 * GW) // GC
LR, B1, B2, AEPS, WD, STEP = 0.001, 0.9, 0.999, 1e-08, 0.01, 10
VMEM_LIMIT = 48 * 1024 * 1024
CPS = 3


def _params(sem=None):
    return pltpu.CompilerParams(dimension_semantics=sem, vmem_limit_bytes=VMEM_LIMIT)


def _sds(shape, dtype):
    return jax.ShapeDtypeStruct(shape, dtype)


def _me():
    return lax.axis_index("x"), lax.axis_index("y"), lax.axis_index("c")


def _peer(j):
    x, y, c = _me()
    return (x ^ ((j >> 2) & 1), y ^ ((j >> 1) & 1), c ^ (j & 1))


def _slot(dev):
    return 4 * dev[0] + 2 * dev[1] + dev[2]


HBM_SPEC = pl.BlockSpec(memory_space=pltpu.HBM)
SEM_SPEC = pl.BlockSpec(memory_space=pltpu.SEMAPHORE)
ANY_SPEC = pl.BlockSpec(memory_space=pl.ANY)
EFFECT = pltpu.SideEffectType.DATAFLOW_SIDE_EFFECTING
SIBLING = 1
OTHER_CHIPS = (2, 4, 6)


def _in_hbm(a):
    return pltpu.with_memory_space_constraint(a, pltpu.HBM)


def _chip(dev):
    return 2 * dev[0] + dev[1]


def _rcopy(src, dst, send_sem, recv_sem, to):
    return pltpu.make_async_remote_copy(src_ref=src, dst_ref=dst, send_sem=send_sem, recv_sem=recv_sem,
                                        device_id=to, device_id_type=MESH)


def _split_call(body, name, ins, in_specs, out_shape, out_specs, aliases, scratch=()):
    n = len(ins) + len(out_shape)

    def with_token(*refs):
        body(*refs[:n], *refs[n + 1:])
        refs[n][...] = jnp.zeros_like(refs[n])

    return pl.pallas_call(
        with_token, name=name, in_specs=in_specs, out_shape=list(out_shape) + [_sds((8, 128), F32)],
        out_specs=list(out_specs) + [pl.BlockSpec(memory_space=pltpu.VMEM)],
        input_output_aliases=aliases, scratch_shapes=list(scratch),
        compiler_params=pltpu.CompilerParams(has_side_effects=EFFECT),
    )(*ins)


def _after(body, n_in, deps):
    deps = [d for d in deps if d is not None]
    if not deps:
        return body, [], []
    return (lambda *refs: body(*refs[:n_in], *refs[n_in + len(deps):])), deps, [ANY_SPEC] * len(deps)


def _gather_start(shards, name, after):
    n = len(shards)
    me = _slot(_me())
    lands = [lax.dynamic_update_slice(lax.empty((NDEV,) + s.shape, s.dtype), s[None], (me, 0, 0)) for s in shards]

    def body(*refs):
        src, land = refs[:n], refs[n:2 * n]
        outs = refs[2 * n + 1:]
        for i in range(n):
            send_sems, recv_sems = outs[4 * i], outs[4 * i + 1]
            for k, rel in enumerate((SIBLING,) + OTHER_CHIPS):
                _rcopy(src[i], land[i].at[_slot(_me())], send_sems.at[k], recv_sems.at[k], _peer(rel)).start()

    out_shape, out_specs, aliases = [], [], {}
    for i, s in enumerate(shards):
        out_shape += [pltpu.SemaphoreType.DMA((4,)), pltpu.SemaphoreType.DMA((4,)), pltpu.HBM(s.shape, s.dtype),
                      pltpu.HBM((NDEV,) + s.shape, s.dtype)]
        out_specs += [SEM_SPEC, SEM_SPEC, HBM_SPEC, HBM_SPEC]
        aliases[i] = 4 * i + 2
        aliases[n + i] = 4 * i + 3
    res = _split_call(body, name, [_in_hbm(s) for s in shards] + [_in_hbm(l) for l in lands] + [after],
                      [HBM_SPEC] * (2 * n) + [ANY_SPEC], out_shape, out_specs, aliases)
    return [tuple(res[4 * i:4 * i + 4]) for i in range(n)], res[-1]


def _gather_forward(started, after, name):
    n = len(started)

    def body(*refs):
        land, recv1 = refs[:n], refs[n:2 * n]
        outs = refs[2 * n + 1:]
        for i in range(n):
            send2, recv2 = outs[3 * i + 1], outs[3 * i + 2]
            for k, rel in enumerate(OTHER_CHIPS):
                blk = land[i].at[_slot(_peer(rel))]
                _rcopy(blk, blk, send2.at[k], recv1[i].at[1 + k], _peer(rel)).wait_recv()
                _rcopy(blk, blk, send2.at[k], recv2.at[k], _peer(SIBLING)).start()

    ins = [_in_hbm(st[3]) for st in started] + [st[1] for st in started] + [after]
    out_shape, out_specs, aliases = [], [], {}
    for i, st in enumerate(started):
        out_shape += [pltpu.HBM(st[3].shape, st[3].dtype), pltpu.SemaphoreType.DMA((3,)), pltpu.SemaphoreType.DMA((3,))]
        out_specs += [HBM_SPEC, SEM_SPEC, SEM_SPEC]
        aliases[i] = 3 * i
    res = _split_call(body, name, ins, [HBM_SPEC] * n + [SEM_SPEC] * n + [ANY_SPEC], out_shape, out_specs, aliases)
    return [(st[0], st[1], st[2], res[3 * i], res[3 * i + 1], res[3 * i + 2]) for i, st in enumerate(started)], res[-1]


def _gather_finish(forwarded, after, name):
    n = len(forwarded)

    def body(*refs):
        for i in range(n):
            send1, recv1, src, land, send2, recv2 = refs[6 * i:6 * i + 6]
            me = _slot(_me())
            sib = _slot(_peer(SIBLING))
            for k, rel in enumerate((SIBLING,) + OTHER_CHIPS):
                _rcopy(src, land.at[me], send1.at[k], recv1.at[k], _peer(rel)).wait_send()
            _rcopy(src, land.at[sib], send1.at[0], recv1.at[0], _peer(SIBLING)).wait_recv()
            for k, rel in enumerate(OTHER_CHIPS):
                mine, theirs = land.at[_slot(_peer(rel))], land.at[_slot(_peer(rel ^ SIBLING))]
                _rcopy(mine, mine, send2.at[k], recv2.at[k], _peer(SIBLING)).wait_send()
                _rcopy(theirs, theirs, send2.at[k], recv2.at[k], _peer(SIBLING)).wait_recv()

    ins, in_specs, out_shape, aliases = [], [], [], {}
    for i, f in enumerate(forwarded):
        ins += [f[0], f[1], _in_hbm(f[2]), _in_hbm(f[3]), f[4], f[5]]
        in_specs += [SEM_SPEC, SEM_SPEC, HBM_SPEC, HBM_SPEC, SEM_SPEC, SEM_SPEC]
        out_shape.append(pltpu.HBM(f[3].shape, f[3].dtype))
        aliases[6 * i + 3] = i
    res = _split_call(body, name, ins + [after], in_specs + [ANY_SPEC], out_shape, [HBM_SPEC] * n, aliases)
    return list(res[:-1])


def _to_sibling_start(parts, name):
    n = len(parts)
    lands = [lax.empty(p.shape[1:], p.dtype) for p in parts]

    def body(*refs):
        src, land = refs[:n], refs[n:2 * n]
        outs = refs[2 * n:]
        other = 1 - lax.axis_index("c")
        for i in range(n):
            _rcopy(src[i].at[other], land[i], outs[4 * i], outs[4 * i + 1], _peer(SIBLING)).start()

    out_shape, out_specs, aliases = [], [], {}
    for i, p in enumerate(parts):
        out_shape += [pltpu.SemaphoreType.DMA(()), pltpu.SemaphoreType.DMA(()), pltpu.HBM(p.shape, p.dtype),
                      pltpu.HBM(p.shape[1:], p.dtype)]
        out_specs += [SEM_SPEC, SEM_SPEC, HBM_SPEC, HBM_SPEC]
        aliases[i] = 4 * i + 2
        aliases[n + i] = 4 * i + 3
    res = _split_call(body, name, [_in_hbm(p) for p in parts] + [_in_hbm(l) for l in lands], [HBM_SPEC] * (2 * n),
                      out_shape, out_specs, aliases)
    return [tuple(res[4 * i:4 * i + 4]) for i in range(n)], res[-1]


def _to_sibling_finish(started, after, name):
    n = len(started)

    def body(*refs):
        for i in range(n):
            send, recv, src, land = refs[4 * i:4 * i + 4]
            cp = _rcopy(src.at[0], land, send, recv, _peer(SIBLING))
            cp.wait_send()
            cp.wait_recv()

    ins, in_specs, out_shape, aliases = [], [], [], {}
    for i, st in enumerate(started):
        ins += [st[0], st[1], _in_hbm(st[2]), _in_hbm(st[3])]
        in_specs += [SEM_SPEC, SEM_SPEC, HBM_SPEC, HBM_SPEC]
        out_shape += [pltpu.HBM(st[2].shape, st[2].dtype), pltpu.HBM(st[3].shape, st[3].dtype)]
        aliases[4 * i + 2] = 2 * i
        aliases[4 * i + 3] = 2 * i + 1
    res = _split_call(body, name, ins + [after], in_specs + [ANY_SPEC], out_shape, [HBM_SPEC] * (2 * n), aliases)
    return [(res[2 * i], res[2 * i + 1]) for i in range(n)]


def _chip_sum(parts, from_sibling, my_c, tile, name):
    _, _, r, c = parts.shape
    tr, tc = tile

    def body(c_ref, p_ref, s_ref, o_ref):
        o_ref[...] = (p_ref[...].astype(F32) + s_ref[...].astype(F32)).astype(o_ref.dtype)

    blk = pl.BlockSpec((4, tr, tc), lambda i, j, c_ref: (0, i, j))
    return pl.pallas_call(
        body, name=name, out_shape=_sds((4, r, c), parts.dtype),
        grid_spec=pltpu.PrefetchScalarGridSpec(
            num_scalar_prefetch=1, grid=(r // tr, c // tc),
            in_specs=[pl.BlockSpec((None, 4, tr, tc), lambda i, j, c_ref: (c_ref[0], 0, i, j)), blk], out_specs=blk),
        compiler_params=_params(("parallel", "parallel")),
    )(my_c, parts, from_sibling)


class _Side:
    def __init__(self, scalar, ins, outs, fn):
        self.scalar, self.ins, self.outs, self.fn = scalar, ins, outs, fn


def _chip_sum_side(parts, from_sibling, my_c, steps):
    _, _, r, c = parts.shape
    rows = r // steps
    assert rows * steps == r and rows % 16 == 0
    return _Side(
        my_c,
        [(parts, (None, 4, rows, c), lambda t, s: (s[0], 0, t, 0)), (from_sibling, (4, rows, c), lambda t, s: (0, t, 0))],
        [((4, r, c), parts.dtype, (4, rows, c), lambda t, s: (0, t, 0))],
        lambda p, q: [(p.astype(F32) + q.astype(F32)).astype(parts.dtype)])


def _to_chips_start(sums, name):
    n = len(sums)
    lands = [lax.empty((3,) + s.shape[1:], s.dtype) for s in sums]

    def body(*refs):
        src, land = refs[:n], refs[n:2 * n]
        outs = refs[2 * n:]
        for i in range(n):
            for k, rel in enumerate(OTHER_CHIPS):
                to = _peer(rel)
                _rcopy(src[i].at[_chip(to)], land[i].at[k], outs[4 * i].at[k], outs[4 * i + 1].at[k], to).start()

    out_shape, out_specs, aliases = [], [], {}
    for i, s in enumerate(sums):
        out_shape += [pltpu.SemaphoreType.DMA((3,)), pltpu.SemaphoreType.DMA((3,)), pltpu.HBM(s.shape, s.dtype),
                      pltpu.HBM((3,) + s.shape[1:], s.dtype)]
        out_specs += [SEM_SPEC, SEM_SPEC, HBM_SPEC, HBM_SPEC]
        aliases[i] = 4 * i + 2
        aliases[n + i] = 4 * i + 3
    res = _split_call(body, name, [_in_hbm(s) for s in sums] + [_in_hbm(l) for l in lands], [HBM_SPEC] * (2 * n),
                      out_shape, out_specs, aliases)
    return [tuple(res[4 * i:4 * i + 4]) for i in range(n)], res[-1]


def _to_chips_finish(started, after, name):
    n = len(started)

    def body(*refs):
        for i in range(n):
            send, recv, src, land = refs[4 * i:4 * i + 4]
            for k, rel in enumerate(OTHER_CHIPS):
                cp = _rcopy(src.at[0], land.at[k], send.at[k], recv.at[k], _peer(rel))
                cp.wait_send()
                cp.wait_recv()

    ins, in_specs, out_shape, aliases = [], [], [], {}
    for i, st in enumerate(started):
        ins += [st[0], st[1], _in_hbm(st[2]), _in_hbm(st[3])]
        in_specs += [SEM_SPEC, SEM_SPEC, HBM_SPEC, HBM_SPEC]
        out_shape += [pltpu.HBM(st[2].shape, st[2].dtype), pltpu.HBM(st[3].shape, st[3].dtype)]
        aliases[4 * i + 2] = 2 * i
        aliases[4 * i + 3] = 2 * i + 1
    res = _split_call(body, name, ins + [after], in_specs + [ANY_SPEC], out_shape, [HBM_SPEC] * (2 * n), aliases)
    return [(res[2 * i], res[2 * i + 1]) for i in range(n)]


def _reduce_small(v, name, deps=()):
    _, r, c = v.shape

    def body(v_ref, o_ref, land, send_sems, recv_sems):
        me = _slot(_me())
        copies = []
        for j in range(1, NDEV):
            to = _peer(j)
            cp = _rcopy(v_ref.at[_slot(to)], land.at[me], send_sems.at[j - 1], recv_sems.at[j - 1], to)
            cp.start()
            copies.append(cp)
        land[me] = v_ref[me]
        for cp in copies:
            cp.wait()
        acc = land[0]
        for k in range(1, NDEV):
            acc = acc + land[k]
        o_ref[...] = acc

    vm = pl.BlockSpec(memory_space=pltpu.VMEM)
    body, dep_ins, dep_specs = _after(body, 1, deps)
    return pl.pallas_call(
        body, name=name, out_shape=_sds((r, c), F32), in_specs=[vm] + dep_specs, out_specs=vm,
        scratch_shapes=[pltpu.VMEM((NDEV, r, c), F32), pltpu.SemaphoreType.DMA((NDEV - 1,)),
                        pltpu.SemaphoreType.DMA((NDEV - 1,))],
        compiler_params=_params(),
    )(v, *dep_ins)


def _matmul(a, b, *, mode, tm, tn, tk, name, out_dtype=F32, epi=None, extra=None, b_slots=False, out_slots=False,
            deps=(), col_block=0, into=None, n_total=None, side=None):
    b_pair = b if isinstance(b, tuple) else None
    if b_pair:
        assert mode == "nt" and tk == 2 * b[0].shape[1] == a.shape[1] and not b_slots
        b = b[0]
    slot_w = b.shape[-1] if b_slots else None
    if mode == "nn":
        M, K = a.shape
        N = NDEV * slot_w if b_slots else b.shape[1]
    elif mode == "tn":
        K, M = a.shape
        N = b.shape[1]
    else:
        M, K = a.shape
        N = b.shape[-2]
        if b_slots:
            assert K == NDEV * slot_w and tk % slot_w == 0
    if mode == "nn" and b_slots:
        assert tn == slot_w
    if out_slots == "cols":
        assert tn * NDEV == N
    if out_slots == "rows":
        assert (M // NDEV) % tm == 0
    assert M % tm == 0 and N % tn == 0 and K % tk == 0, (name, M, N, K, tm, tn, tk)
    nk = K // tk
    dims = {"nn": ((1,), (0,)), "tn": ((0,), (0,)), "nt": ((1,), (1,))}[mode]

    if mode == "tn":
        a_spec = pl.BlockSpec((tk, tm), lambda i, j, k: (k, i))
    else:
        a_spec = pl.BlockSpec((tm, tk), lambda i, j, k: (i, k))
    if b_pair:
        b_spec = pl.BlockSpec((tn, tk // 2), lambda i, j, k: (j, 0))
    elif mode == "nt":
        b_spec = (pl.BlockSpec((tk // slot_w, tn, slot_w), lambda i, j, k: (k, j, 0)) if b_slots
                  else pl.BlockSpec((tn, tk), lambda i, j, k: (j, k)))
    else:
        b_spec = (pl.BlockSpec((None, tk, tn), lambda i, j, k: (j, k, 0)) if b_slots
                  else pl.BlockSpec((tk, tn), lambda i, j, k: (k, j)))
    tile = pl.BlockSpec((tm, tn), lambda i, j, k: (i, j + col_block))
    if out_slots == "cols":
        out_spec = pl.BlockSpec((None, None, tm, tn), lambda i, j, k: (j % 2, j // 2, i, 0))
        out_shape = _sds((2, 4, M, tn), out_dtype)
    elif out_slots == "rows":
        per = M // NDEV // tm
        out_spec = pl.BlockSpec((None, None, tm, tn), lambda i, j, k: ((i // per) % 2, (i // per) // 2, i % per, j))
        out_shape = _sds((2, 4, M // NDEV, N), out_dtype)
    else:
        out_spec, out_shape = tile, _sds((M, n_total or N), out_dtype)
    ins, in_specs = [a, b], [a_spec, b_spec]
    if b_pair:
        ins.append(b_pair[1])
        in_specs.append(b_spec)
    n_b = len(ins) - 1
    if epi in ("add", "dz"):
        ins.append(extra)
        in_specs.append(tile)
    aliases = {}
    if into is not None:
        aliases[len(ins)] = 0
        ins.append(into)
        in_specs.append(ANY_SPEC)
    if epi == "relu2":
        out_specs, out_shapes = [tile, tile], [_sds((M, N), F32), _sds((M, N), BF16)]
    else:
        out_specs, out_shapes = out_spec, out_shape
    n_in = len(ins)

    def body(*refs):
        outs = refs[n_in:-1] if nk > 1 else refs[n_in:]
        extra_ref = refs[1 + n_b]

        def finish(p):
            if epi is None:
                outs[0][...] = p.astype(out_dtype)
            elif epi == "add":
                outs[0][...] = (p + extra_ref[...]).astype(out_dtype)
            elif epi == "relu2":
                outs[0][...] = p
                rz = jnp.maximum(p, 0.0)
                outs[1][...] = (rz * rz).astype(BF16)
            else:
                outs[0][...] = (p * (2.0 * jnp.maximum(extra_ref[...], 0.0))).astype(out_dtype)

        def product():
            av = refs[0][...].astype(BF16)
            if b_pair:
                half = tk // 2
                return (lax.dot_general(av[:, :half], refs[1][...], (dims, ((), ())), preferred_element_type=F32)
                        + lax.dot_general(av[:, half:], refs[2][...], (dims, ((), ())), preferred_element_type=F32))
            if mode == "nt" and b_slots:
                return sum(lax.dot_general(av[:, s * slot_w:(s + 1) * slot_w], refs[1][s], (dims, ((), ())),
                                           preferred_element_type=F32) for s in range(tk // slot_w))
            return lax.dot_general(av, refs[1][...].astype(BF16), (dims, ((), ())), preferred_element_type=F32)

        if nk == 1:
            finish(product())
            return
        acc = refs[-1]
        k = pl.program_id(2)

        @pl.when(k == 0)
        def _():
            acc[...] = jnp.zeros_like(acc)

        acc[...] += product()

        @pl.when(k == nk - 1)
        def _():
            finish(acc[...])

    grid = (M // tm, N // tn, nk)
    scratch = [pltpu.VMEM((tm, tn), F32)] if nk > 1 else []
    if side is None:
        body, dep_ins, dep_specs = _after(body, n_in, deps)
        return pl.pallas_call(
            body, name=name, grid=grid,
            in_specs=in_specs + dep_specs, out_specs=out_specs, out_shape=out_shapes, input_output_aliases=aliases,
            scratch_shapes=scratch, compiler_params=_params(("parallel", "parallel", "arbitrary")),
        )(*ins, *dep_ins)

    deps = [d for d in deps if d is not None]
    step = lambda i, j, k: (i * grid[1] + j) * grid[2] + k
    host = lambda spec: (spec if spec.block_shape is None else
                         pl.BlockSpec(spec.block_shape, lambda i, j, k, s, f=spec.index_map: f(i, j, k)))
    cut = lambda blk, f: pl.BlockSpec(blk, lambda i, j, k, s: f(step(i, j, k), s))
    host_out_specs = list(out_specs) if isinstance(out_specs, list) else [out_specs]
    host_out_shapes = list(out_shapes) if isinstance(out_shapes, list) else [out_shapes]
    n_dep, n_si, n_ho, n_so = len(deps), len(side.ins), len(host_out_specs), len(side.outs)

    def with_side(*refs):
        rest = refs[1:]
        side_in = rest[n_in + n_dep:n_in + n_dep + n_si]
        outs_all = rest[n_in + n_dep + n_si:]
        body(*rest[:n_in], *outs_all[:n_ho], *outs_all[n_ho + n_so:])
        for o_ref, val in zip(outs_all[n_ho:n_ho + n_so], side.fn(*[r[...] for r in side_in])):
            o_ref[...] = val

    res = pl.pallas_call(
        with_side, name=name, input_output_aliases={k + 1: v for k, v in aliases.items()},
        out_shape=host_out_shapes + [_sds(shape, dt) for shape, dt, _, _ in side.outs],
        grid_spec=pltpu.PrefetchScalarGridSpec(
            num_scalar_prefetch=1, grid=grid,
            in_specs=[host(s) for s in in_specs] + [ANY_SPEC] * n_dep + [cut(blk, f) for _, blk, f in side.ins],
            out_specs=[host(s) for s in host_out_specs] + [cut(blk, f) for _, _, blk, f in side.outs],
            scratch_shapes=scratch),
        compiler_params=_params(("parallel", "parallel", "arbitrary")),
    )(side.scalar, *ins, *deps, *[arr for arr, _, _ in side.ins])
    host_res = res[0] if n_ho == 1 else tuple(res[:n_ho])
    return host_res, list(res[n_ho:])


ROWS = 352


def _rmsnorm_fwd(h, w, name, deps=()):
    def body(h_ref, w_ref, u_ref):
        x = h_ref[...]
        rstd = lax.rsqrt(jnp.mean(x * x, axis=-1, keepdims=True) + EPS)
        u_ref[...] = (x * rstd * w_ref[...]).astype(BF16)

    row = pl.BlockSpec((ROWS, D), lambda i: (i, 0))
    body, dep_ins, dep_specs = _after(body, 2, deps)
    return pl.pallas_call(
        body, name=name, grid=(LP // ROWS,), in_specs=[row, pl.BlockSpec((1, D), lambda i: (0, 0))] + dep_specs,
        out_specs=row, out_shape=_sds((LP, D), BF16), compiler_params=_params(("parallel",)),
    )(h, w, *dep_ins)


TOKEN_ROWS = 512


def _rmsnorm_bwd_input(h, w, du, dres, name, deps=()):
    def math(h_ref, w_ref, du_ref, dres_ref):
        x = h_ref[...]
        rstd = lax.rsqrt(jnp.mean(x * x, axis=-1, keepdims=True) + EPS)
        xhat = x * rstd
        dy = du_ref[...]
        dxh = dy * w_ref[...]
        dh = dres_ref[...] + rstd * (dxh - xhat * jnp.mean(dxh * xhat, axis=-1, keepdims=True))
        return dh, jnp.sum(dy * xhat, axis=0, keepdims=True)

    def body(h_ref, w_ref, du_ref, dres_ref, hh_ref, duh_ref, dresh_ref, dx_ref, dhead_ref, gw_ref):
        dx_ref[...], part = math(h_ref, w_ref, du_ref, dres_ref)

        @pl.when(pl.program_id(0) == 0)
        def _():
            dhead_ref[...], head = math(hh_ref, w_ref, duh_ref, dresh_ref)
            gw_ref[...] = part + head

        @pl.when(pl.program_id(0) > 0)
        def _():
            gw_ref[...] += part

    rows = pl.BlockSpec((pl.Element(TOKEN_ROWS), pl.Element(D)),
                        lambda i: (pl.multiple_of(ROW_X + TOKEN_ROWS * i, 8), 0))
    head = pl.BlockSpec((ROW_X, D), lambda i: (0, 0))
    vec = pl.BlockSpec((1, D), lambda i: (0, 0))
    body, dep_ins, dep_specs = _after(body, 7, deps)
    return pl.pallas_call(
        body, name=name, grid=(SEQ // TOKEN_ROWS,),
        in_specs=[rows, vec, rows, rows, head, head, head] + dep_specs,
        out_specs=[pl.BlockSpec((TOKEN_ROWS, D), lambda i: (i, 0)), head, vec],
        out_shape=[_sds((SEQ, D), F32), _sds((ROW_X, D), F32), _sds((1, D), F32)],
        compiler_params=_params(("arbitrary",)),
    )(h, w, du, dres, h, du, dres, *dep_ins)


def _rmsnorm_bwd(h, w, du, dres, name, deps=()):
    def body(h_ref, w_ref, du_ref, dres_ref, dh_ref, dhb_ref, gw_ref):
        x = h_ref[...]
        rstd = lax.rsqrt(jnp.mean(x * x, axis=-1, keepdims=True) + EPS)
        xhat = x * rstd
        dy = du_ref[...]
        dxh = dy * w_ref[...]
        dh = dres_ref[...] + rstd * (dxh - xhat * jnp.mean(dxh * xhat, axis=-1, keepdims=True))
        dh_ref[...] = dh
        dhb_ref[...] = dh.astype(BF16)
        part = jnp.sum(dy * xhat, axis=0, keepdims=True)

        @pl.when(pl.program_id(0) == 0)
        def _():
            gw_ref[...] = part

        @pl.when(pl.program_id(0) > 0)
        def _():
            gw_ref[...] += part

    row = pl.BlockSpec((ROWS, D), lambda i: (i, 0))
    vec = pl.BlockSpec((1, D), lambda i: (0, 0))
    body, dep_ins, dep_specs = _after(body, 4, deps)
    return pl.pallas_call(
        body, name=name, grid=(LP // ROWS,), in_specs=[row, vec, row, row] + dep_specs, out_specs=[row, row, vec],
        out_shape=[_sds((LP, D), F32), _sds((LP, D), BF16), _sds((1, D), F32)],
        compiler_params=_params(("arbitrary",)),
    )(h, w, du, dres, *dep_ins)


def _loss_head(h2, wf, target):
    def body(h_ref, w_ref, t_ref, dh_ref, dhb_ref, sq_ref, gw_ref):
        i = pl.program_id(0)

        @pl.when(i == 0)
        def _():
            sq_ref[...] = jnp.zeros_like(sq_ref)
            gw_ref[...] = jnp.zeros_like(gw_ref)

        def rows(t, live):
            x = h_ref[...]
            rstd = lax.rsqrt(jnp.mean(x * x, axis=-1, keepdims=True) + EPS)
            xhat = x * rstd
            w = w_ref[...]
            err = xhat * w - t
            if live is not None:
                err = jnp.where(live, err, 0.0)
            sq_ref[...] += jnp.sum(err * err, axis=0, keepdims=True)
            dy = err * (1.0 / D)
            gw_ref[...] += jnp.sum(dy * xhat, axis=0, keepdims=True)
            dxh = dy * w
            dh = rstd * (dxh - xhat * jnp.mean(dxh * xhat, axis=-1, keepdims=True))
            dh_ref[...] = dh
            dhb_ref[...] = dh.astype(BF16)

        @pl.when(i == 0)
        def _():
            rid = lax.broadcasted_iota(jnp.int32, (ROWS, D), 0)
            rows(pltpu.roll(t_ref[...], ROW_X, 0), rid >= ROW_X)

        @pl.when(i > 0)
        def _():
            rows(t_ref[...], None)

    row = pl.BlockSpec((ROWS, D), lambda i: (i, 0))
    vec = pl.BlockSpec((1, D), lambda i: (0, 0))
    tgt = pl.BlockSpec((pl.Element(ROWS), pl.Element(D)),
                       lambda i: (pl.multiple_of(jnp.maximum(ROWS * i - ROW_X, 0), 8), 0))
    return pl.pallas_call(
        body, name="loss_head", grid=(LP // ROWS,),
        in_specs=[row, vec, tgt],
        out_specs=[row, row, vec, vec],
        out_shape=[_sds((LP, D), F32), _sds((LP, D), BF16), _sds((1, D), F32), _sds((1, D), F32)],
        compiler_params=_params(("arbitrary",)),
    )(h2, wf, target)


def _dot(a, b, dims):
    return lax.dot_general(a, b, (dims, ((), ())), preferred_element_type=F32)


NN, TN, NT = ((1,), (0,)), ((0,), (0,)), ((1,), (1,))


def _tri_sum(t, x):
    hi = x.astype(BF16)
    r1 = x - hi.astype(F32)
    mid = r1.astype(BF16)
    lo = (r1 - mid.astype(F32)).astype(BF16)
    return _dot(t, hi, NN) + _dot(t, mid, NN) + _dot(t, lo, NN)


def _gla_gates(glr_ref, gw2_ref, gb_ref, rows, row0):
    g_raw = _dot(glr_ref[rows, :].astype(BF16), gw2_ref[...], NN) + gb_ref[...]
    logsig = jnp.minimum(g_raw, 0.0) - jnp.log(1.0 + jnp.exp(-jnp.abs(g_raw)))
    rid = row0 + lax.broadcasted_iota(jnp.int32, g_raw.shape, 0)
    live = rid >= PAD
    return g_raw, jnp.where(live, logsig / TAU, 0.0), live


def _tri_masks():
    r = lax.broadcasted_iota(jnp.int32, (CHUNK, CHUNK), 0)
    c = lax.broadcasted_iota(jnp.int32, (CHUNK, CHUNK), 1)
    return r >= c


def _gla_specs(rev):
    n = NCH // CPS
    R = CPS * CHUNK
    st = (lambda s: n - 1 - s) if rev else (lambda s: s)
    return R, n, st, [
        pl.BlockSpec((R, KW), lambda s: (st(s), 0)),
        pl.BlockSpec((R, KW), lambda s: (st(s), 1)),
        pl.BlockSpec((R, GW), lambda s: (st(s), 1)),
        pl.BlockSpec((R, GW), lambda s: (st(s), 2)),
        pl.BlockSpec((R, 128), lambda s: (st(s), GLR_BLK)),
    ]


def _gla_fwd(proj, gw2p, gate_b, gnw, deps=()):
    R, n, st, pspecs = _gla_specs(False)

    def body(q_ref, k_ref, v_ref, r_ref, glr_ref, gw2_ref, gb_ref, gnw_ref, og_ref, o_ref, st_ref, state):
        s = pl.program_id(0)

        @pl.when(s == 0)
        def _():
            state[...] = jnp.zeros_like(state)

        causal = _tri_masks()
        tri = causal.astype(BF16)
        for c in range(CPS):
            rows = slice(c * CHUNK, (c + 1) * CHUNK)
            _, logg, _ = _gla_gates(glr_ref, gw2_ref, gb_ref, rows, s * R + c * CHUNK)
            G = _tri_sum(tri, logg)
            g_last = G[CHUNK - 1:CHUNK, :]
            q_dec = (q_ref[rows, :] * (DK ** -0.5) * jnp.exp(G)).astype(BF16)
            kk = k_ref[rows, :]
            k_inv = (kk * jnp.exp(-G)).astype(BF16)
            k_end = (kk * jnp.exp(g_last - G)).astype(BF16)
            decay = jnp.exp(g_last)
            for h in range(H):
                lk = slice(h * DK, (h + 1) * DK)
                lv = slice(h * DV, (h + 1) * DV)
                v = v_ref[rows, lv].astype(BF16)
                S = state[h]
                st_ref[c, h] = S
                A = jnp.where(causal, _dot(q_dec[:, lk], k_inv[:, lk], NT), 0.0).astype(BF16)
                o = _dot(A, v, NN) + _dot(q_dec[:, lk], S.astype(BF16), NT)
                state[h] = decay[:, lk] * S + _dot(v, k_end[:, lk], TN)
                o_ref[rows, lv] = o
                on = o * lax.rsqrt(jnp.mean(o * o, axis=-1, keepdims=True) + EPS) * gnw_ref[...]
                rr = r_ref[rows, lv]
                og_ref[rows, lv] = (on * (rr * jax.nn.sigmoid(rr))).astype(BF16)

    full = lambda shape: pl.BlockSpec(shape, lambda s: (0,) * len(shape))
    body, dep_ins, dep_specs = _after(body, 8, deps)
    return pl.pallas_call(
        body, name="gla_fwd", grid=(n,),
        in_specs=pspecs + [full((128, KW)), full((1, KW)), full((1, DV))] + dep_specs,
        out_specs=[pl.BlockSpec((R, GW), lambda s: (s, 0)), pl.BlockSpec((R, GW), lambda s: (s, 0)),
                   pl.BlockSpec((CPS, H, DV, DK), lambda s: (s, 0, 0, 0))],
        out_shape=[_sds((LP, GW + PW), BF16), _sds((LP, GW), F32), _sds((NCH, H, DV, DK), F32)],
        scratch_shapes=[pltpu.VMEM((H, DV, DK), F32)],
        compiler_params=_params(("arbitrary",)),
    )(proj, proj, proj, proj, proj, gw2p, gate_b, gnw, *dep_ins)


def _gla_bwd(proj, dmixed, o_saved, st_saved, gw2p, gate_b, gnw, deps=()):
    R, n, st, pspecs = _gla_specs(True)

    def body(q_ref, k_ref, v_ref, r_ref, glr_ref, dog_ref, o_ref, st_ref, gw2_ref, gb_ref, gnw_ref,
             dqkvr_ref, dglr_ref, ggn_ref, ggb_ref, ggw_ref, gstate):
        s = pl.program_id(0)

        @pl.when(s == 0)
        def _():
            gstate[...] = jnp.zeros_like(gstate)
            ggn_ref[...] = jnp.zeros_like(ggn_ref)
            ggb_ref[...] = jnp.zeros_like(ggb_ref)
            ggw_ref[...] = jnp.zeros_like(ggw_ref)

        causal = _tri_masks()
        tri = causal.astype(BF16)
        tri_up = (lax.broadcasted_iota(jnp.int32, (CHUNK, CHUNK), 0)
                  <= lax.broadcasted_iota(jnp.int32, (CHUNK, CHUNK), 1)).astype(BF16)
        gnw = gnw_ref[...]
        for c in reversed(range(CPS)):
            rows = slice(c * CHUNK, (c + 1) * CHUNK)
            g_raw, logg, live = _gla_gates(glr_ref, gw2_ref, gb_ref, rows, (n - 1 - s) * R + c * CHUNK)
            G = _tri_sum(tri, logg)
            g_last = G[CHUNK - 1:CHUNK, :]
            e_g, e_gi, e_end = jnp.exp(G), jnp.exp(-G), jnp.exp(g_last - G)
            q_dec = q_ref[rows, :] * (DK ** -0.5) * e_g
            kk = k_ref[rows, :]
            k_inv, k_end = kk * e_gi, kk * e_end
            q_dec_b, k_inv_b, k_end_b = q_dec.astype(BF16), k_inv.astype(BF16), k_end.astype(BF16)
            decay = jnp.exp(g_last)
            d_g, d_gl = [], []
            for h in range(H):
                lk = slice(h * DK, (h + 1) * DK)
                lv = slice(h * DV, (h + 1) * DV)
                o = o_ref[rows, lv]
                rr = r_ref[rows, lv]
                dog = dog_ref[rows, lv]
                rstd = lax.rsqrt(jnp.mean(o * o, axis=-1, keepdims=True) + EPS)
                ohat = o * rstd
                sr = jax.nn.sigmoid(rr)
                don = dog * (rr * sr)
                dqkvr_ref[rows, 2 * KW + GW + h * DV:2 * KW + GW + (h + 1) * DV] = (
                    dog * (ohat * gnw) * (sr * (1.0 + rr * (1.0 - sr)))).astype(BF16)
                ggn_ref[...] += jnp.sum(don * ohat, axis=0, keepdims=True)
                dohat = don * gnw
                do = (rstd * (dohat - ohat * jnp.mean(dohat * ohat, axis=-1, keepdims=True))).astype(BF16)
                v = v_ref[rows, lv].astype(BF16)
                S = st_ref[c, h]
                gS = gstate[h]
                S_b, gS_b = S.astype(BF16), gS.astype(BF16)
                qd, ki, ke = q_dec_b[:, lk], k_inv_b[:, lk], k_end_b[:, lk]
                A = jnp.where(causal, _dot(qd, ki, NT), 0.0).astype(BF16)
                dA = jnp.where(causal, _dot(do, v, NT), 0.0).astype(BF16)
                dv = _dot(A, do, TN) + _dot(ke, gS_b, NT)
                dq_dec = _dot(dA, ki, NN) + _dot(do, S_b, NN)
                dk_inv = _dot(dA, qd, TN)
                dk_end = _dot(v, gS_b, NN)
                d_decay = jnp.sum(gS * S, axis=0, keepdims=True)
                gstate[h] = decay[:, lk] * gS + _dot(do, qd, TN)
                dqkvr_ref[rows, lk] = (dq_dec * e_g[:, lk] * (DK ** -0.5)).astype(BF16)
                dqkvr_ref[rows, KW + h * DK:KW + (h + 1) * DK] = (
                    dk_inv * e_gi[:, lk] + dk_end * e_end[:, lk]).astype(BF16)
                dqkvr_ref[rows, 2 * KW + h * DV:2 * KW + (h + 1) * DV] = dv.astype(BF16)
                ke_prod = dk_end * k_end[:, lk]
                d_g.append(dq_dec * q_dec[:, lk] - dk_inv * k_inv[:, lk] - ke_prod)
                d_gl.append(jnp.sum(ke_prod, axis=0, keepdims=True) + d_decay * decay[:, lk])
            dlogg = _tri_sum(tri_up, jnp.concatenate(d_g, axis=1)) + jnp.concatenate(d_gl, axis=1)
            dg_raw = jnp.where(live, dlogg * (1.0 / TAU) * jax.nn.sigmoid(-g_raw), 0.0)
            ggb_ref[...] += jnp.sum(dg_raw, axis=0, keepdims=True)
            dg_b = dg_raw.astype(BF16)
            ggw_ref[...] += _dot(glr_ref[rows, :].astype(BF16), dg_b, TN)
            dglr_ref[rows, :] = _dot(dg_b, gw2_ref[...], NT).astype(BF16)

    full = lambda shape: pl.BlockSpec(shape, lambda s: (0,) * len(shape))
    body, dep_ins, dep_specs = _after(body, 11, deps)
    return pl.pallas_call(
        body, name="gla_bwd", grid=(n,),
        in_specs=pspecs + [pl.BlockSpec((R, GW), lambda s: (st(s), 0)), pl.BlockSpec((R, GW), lambda s: (st(s), 0)),
                           pl.BlockSpec((CPS, H, DV, DK), lambda s: (st(s), 0, 0, 0)),
                           full((128, KW)), full((1, KW)), full((1, DV))] + dep_specs,
        out_specs=[pl.BlockSpec((R, 2 * KW + 2 * GW), lambda s: (st(s), 0)), pl.BlockSpec((R, 128), lambda s: (st(s), 0)),
                   full((1, DV)), full((1, KW)), full((128, KW))],
        out_shape=[_sds((LP, D_INP), BF16), _sds((LP, 128), BF16),
                   _sds((1, DV), F32), _sds((1, KW), F32), _sds((128, KW), F32)],
        scratch_shapes=[pltpu.VMEM((H, DV, DK), F32)],
        compiler_params=_params(("arbitrary",)),
    )(proj, proj, proj, proj, proj, dmixed, o_saved, st_saved, gw2p, gate_b, gnw, *dep_ins)


def _pool_pre(x, win, rid):
    s, step = x, 1
    while step < win:
        s = s + pltpu.roll(s, step, 0)
        step *= 2
    cnt = jnp.clip(rid - (PAD - 1), 1, win).astype(F32)
    live = rid >= PAD
    return jnp.where(live, s / cnt - x, 0.0), cnt, live


def _pool_fwd(proj, pool_w, pool_scale, mixed):
    def body(pu_ref, w_ref, sc_ref, _, o_ref):
        rid = lax.broadcasted_iota(jnp.int32, (LP, GC), 0)
        for g, win in enumerate(WINDOWS):
            @pl.when(pl.program_id(0) == g)
            def _():
                y, _, _ = _pool_pre(pu_ref[...], win, rid)
                o_ref[...] = (_dot(y.astype(BF16), w_ref[...], NN) * sc_ref[...]).astype(BF16)

    col = lambda base: pl.BlockSpec((LP, GC), lambda g: (0, base + g))
    return pl.pallas_call(
        body, name="pool_fwd", grid=(len(WINDOWS),),
        in_specs=[col(POOL_BLK), pl.BlockSpec((None, GC, GC), lambda g: (g, 0, 0)),
                  pl.BlockSpec((1, GC), lambda g: (0, g)), ANY_SPEC],
        out_specs=col(GW // GC), out_shape=_sds(mixed.shape, BF16), input_output_aliases={3: 0},
        compiler_params=_params(("parallel",)),
    )(proj, pool_w, pool_scale, mixed)


def _pool_bwd(proj, dmixed, pool_w, pool_scale, dproj):
    def body(pu_ref, do_ref, w_ref, sc_ref, _, dpu_ref, dw_ref, dsc_ref):
        rid = lax.broadcasted_iota(jnp.int32, (LP, GC), 0)
        for g, win in enumerate(WINDOWS):
            @pl.when(pl.program_id(0) == g)
            def _():
                y, cnt, live = _pool_pre(pu_ref[...], win, rid)
                y_b = y.astype(BF16)
                w = w_ref[...]
                do = do_ref[...]
                dsc_ref[...] = jnp.sum(do * _dot(y_b, w, NN), axis=0, keepdims=True)
                dyw = (do * sc_ref[...]).astype(BF16)
                dw_ref[...] = _dot(y_b, dyw, TN)
                dy = jnp.where(live, _dot(dyw, w, NT), 0.0)
                s, step = dy / cnt, 1
                while step < win:
                    s = s + pltpu.roll(s, LP - step, 0)
                    step *= 2
                dpu_ref[...] = (s - dy).astype(BF16)

    col = lambda base: pl.BlockSpec((LP, GC), lambda g: (0, base + g))
    mat = pl.BlockSpec((None, GC, GC), lambda g: (g, 0, 0))
    vec = pl.BlockSpec((1, GC), lambda g: (0, g))
    return pl.pallas_call(
        body, name="pool_bwd", grid=(len(WINDOWS),),
        in_specs=[col(POOL_BLK), col(GW // GC), mat, vec, ANY_SPEC], out_specs=[col(POOL_BLK), mat, vec],
        out_shape=[_sds(dproj.shape, BF16), _sds((4, GC, GC), F32), _sds((1, PW), F32)],
        input_output_aliases={4: 0}, compiler_params=_params(("parallel",)),
    )(proj, dmixed, pool_w, pool_scale, dproj)


def _adamw_math(w, g, m, v):
    m = B1 * m + (1.0 - B1) * g
    v = B2 * v + (1.0 - B2) * (g * g)
    m_hat = m * (1.0 / (1.0 - B1 ** STEP))
    v_hat = v * (1.0 / (1.0 - B2 ** STEP))
    return -LR * (m_hat / (jnp.sqrt(v_hat) + AEPS) + WD * w), m, v


def _adamw_landed(sums, landed, my_chip, w, m, v, rows, name, cols=None):
    _, r, c = w.shape

    def body(chip_ref, s_ref, l_ref, w_ref, m_ref, v_ref, g_ref, d_ref, mo_ref, vo_ref):
        g = s_ref[...].astype(F32)
        for k in range(3):
            g = g + l_ref[k].astype(F32)
        g_ref[...] = g
        d_ref[...], mo_ref[...], vo_ref[...] = _adamw_math(w_ref[...], g, m_ref[...], v_ref[...])

    cols = cols or c
    blk = pl.BlockSpec((None, rows, cols), lambda i, j, chip_ref: (0, i, j))
    return pl.pallas_call(
        body, name=name, out_shape=[_sds((1, r, c), F32)] * 4,
        grid_spec=pltpu.PrefetchScalarGridSpec(
            num_scalar_prefetch=1, grid=(r // rows, c // cols),
            in_specs=[pl.BlockSpec((None, rows, cols), lambda i, j, chip_ref: (chip_ref[0], i, j)),
                      pl.BlockSpec((3, rows, cols), lambda i, j, chip_ref: (0, i, j)), blk, blk, blk],
            out_specs=[blk] * 4),
        compiler_params=_params(("parallel", "parallel")),
    )(my_chip, sums, landed, w, m, v)


def _adamw_side(sums, landed, my_chip, w, m, v, steps):
    _, r, c = w.shape
    rows = r // steps
    assert rows * steps == r and rows % 16 == 0

    def fn(s, l, w_, m_, v_):
        g = s.astype(F32)
        for k in range(3):
            g = g + l[k].astype(F32)
        return [g, *_adamw_math(w_, g, m_, v_)]

    one = lambda t, s: (0, t, 0)
    return _Side(
        my_chip,
        [(sums, (None, rows, c), lambda t, s: (s[0], t, 0)), (landed, (3, rows, c), one)]
        + [(a, (None, rows, c), one) for a in (w, m, v)],
        [((1, r, c), F32, (None, rows, c), one)] * 4, fn)


def _adamw_small(g, w, m, v):
    def body(g_ref, w_ref, m_ref, v_ref, d_ref, mo_ref, vo_ref):
        d_ref[...], mo_ref[...], vo_ref[...] = _adamw_math(w_ref[...], g_ref[...], m_ref[...], v_ref[...])

    return pl.pallas_call(body, name="adamw_small", out_shape=[_sds(w.shape, F32)] * 3)(g, w, m, v)


SMALL_REPL = (("norm1_w", D), ("norm2_w", D), ("final_norm_w", D), ("pool_scale", PW), ("gate_b", KW),
              ("gla_norm_w", DV))


def _pack_rows(vecs, rows):
    flat = jnp.concatenate([jnp.ravel(v) for v in vecs])
    return jnp.pad(flat, (0, rows * 1024 - flat.shape[0])).reshape(rows, 1024)


def kernel(x, meta_tokens, norm1_w, w_in, gate_w2, gate_b, gla_norm_w, pool_w, pool_scale, w_out, norm2_w, mlp_w1, mlp_w2, final_norm_w, loss_target, m_meta_tokens, m_norm1_w, m_w_in, m_gate_w2, m_gate_b, m_gla_norm_w, m_pool_w, m_pool_scale, m_w_out, m_norm2_w, m_mlp_w1, m_mlp_w2, m_final_norm_w, v_meta_tokens, v_norm1_w, v_w_in, v_gate_w2, v_gate_b, v_gla_norm_w, v_pool_w, v_pool_scale, v_w_out, v_norm2_w, v_mlp_w1, v_mlp_w2, v_final_norm_w):
    W = dict(meta_tokens=meta_tokens, norm1_w=norm1_w, w_in=w_in, gate_w2=gate_w2, gate_b=gate_b,
             gla_norm_w=gla_norm_w, pool_w=pool_w, pool_scale=pool_scale, w_out=w_out, norm2_w=norm2_w,
             mlp_w1=mlp_w1, mlp_w2=mlp_w2, final_norm_w=final_norm_w)
    Mo = dict(meta_tokens=m_meta_tokens, norm1_w=m_norm1_w, w_in=m_w_in, gate_w2=m_gate_w2, gate_b=m_gate_b,
              gla_norm_w=m_gla_norm_w, pool_w=m_pool_w, pool_scale=m_pool_scale, w_out=m_w_out, norm2_w=m_norm2_w,
              mlp_w1=m_mlp_w1, mlp_w2=m_mlp_w2, final_norm_w=m_final_norm_w)
    Vo = dict(meta_tokens=v_meta_tokens, norm1_w=v_norm1_w, w_in=v_w_in, gate_w2=v_gate_w2, gate_b=v_gate_b,
              gla_norm_w=v_gla_norm_w, pool_w=v_pool_w, pool_scale=v_pool_scale, w_out=v_w_out, norm2_w=v_norm2_w,
              mlp_w1=v_mlp_w1, mlp_w2=v_mlp_w2, final_norm_w=v_final_norm_w)

    ex = _Exchange(dict(small=_pack_rows([meta_tokens, gate_w2[0]], 8), w_in=w_in[0].T.astype(BF16)),
                   dict(w_out=w_out[0], pool_w=pool_w[0].reshape(4 * 32, GC), mlp_w1=mlp_w1[0],
                        mlp_w2a=mlp_w2[0][:, :D // 2], mlp_w2b=mlp_w2[0][:, D // 2:]))
    tr = lambda a: a[0].T[None]
    win_t, m_win_t, v_win_t = tr(w_in), tr(m_w_in), tr(v_w_in)
    ex.early = [win_t, m_win_t, v_win_t]
    ex.shards = {k: (W[k], Mo[k], Vo[k]) for k in ("w_out", "mlp_w1", "mlp_w2")}
    step = _layer_step(x[0], loss_target[0], ex, norm1_w, gate_b, gla_norm_w, pool_scale, norm2_w,
                       final_norm_w.reshape(1, D))
    grad_x = step["dx"][None]

    last = ex.update("mix", step["dx"])
    out = dict(ex.done)

    loss_part = 0.5 * jnp.sum(step["sq"]) / D
    to_all = _pack_rows([step[k] for k, _ in SMALL_REPL] + [loss_part], 8)
    cols = lambda g: g.reshape(g.shape[0], NDEV, -1).transpose(1, 0, 2).reshape(NDEV, -1, 1024)
    packed = jnp.concatenate([jnp.broadcast_to(to_all, (NDEV, 8, 1024)), cols(step["dhead"][PAD:]),
                              cols(step["gate_w2"][:RANK]), jnp.zeros((NDEV, 3, 1024), F32)], axis=1)
    red = _reduce_small(packed, "reduce_small", deps=[last])
    loss = red[7, 768]

    done = ex.grad_finish("w_in", red)
    poolw3 = lambda a: a.reshape(1, 4 * 32, GC)
    res = _adamw_landed(*done["pool_w"], ex.my_chip, poolw3(pool_w), poolw3(m_pool_w), poolw3(v_pool_w),
                        SHARD_ROWS["pool_w"], "adamw_pool_w")
    out["pool_w"] = [a.reshape(pool_w.shape) for a in res]
    res = _adamw_landed(*done["w_in"], ex.my_chip, win_t, m_win_t, v_win_t, D_IN // NDEV, "adamw_w_in", cols=256)
    out["w_in"] = [a[0].T[None] for a in res]

    def small_pack(P):
        return jnp.concatenate([_pack_rows([P[k] for k, _ in SMALL_REPL], 8),
                                _pack_rows([P["meta_tokens"], P["gate_w2"]], 8)], axis=0)

    g_small = red.at[7, 768].set(0.0)
    res_small = _adamw_small(g_small, small_pack(W), small_pack(Mo), small_pack(Vo))
    res_small = [g_small] + list(res_small)
    off = 0
    for k, nel in SMALL_REPL:
        out[k] = [a[0:8].reshape(-1)[off:off + nel].reshape(W[k].shape) for a in res_small]
        off += nel
    out["meta_tokens"] = [a[8:12].reshape(N_META, D // NDEV) for a in res_small]
    out["gate_w2"] = [a[12].reshape(1, RANK, KW // NDEV) for a in res_small]

    order = ["meta_tokens", "norm1_w", "w_in", "gate_w2", "gate_b", "gla_norm_w", "pool_w", "pool_scale", "w_out",
             "norm2_w", "mlp_w1", "mlp_w2", "final_norm_w"]
    return (loss, grad_x, *[out[k][0] for k in order], *[out[k][1] for k in order],
            *[out[k][2] for k in order], *[out[k][3] for k in order])


SHARD_ROWS = dict(w_out=256, mlp_w1=512, mlp_w2=256, pool_w=128)
SLOT_ROWS = D_IN // NDEV


def _w_in_pieces(s):
    lo, hi, out = s * SLOT_ROWS, (s + 1) * SLOT_ROWS, []
    for a, b, shift in ((0, C_GLR, 0), (C_GLR, C_GLR + RANK, PW), (C_GLR + RANK, D_IN, -RANK)):
        a, b = max(a, lo), min(b, hi)
        if a < b:
            out.append((a - lo, a + shift, b - a))
    return out


def _w_in_to_layer_order(gathered):
    def body(g_ref, o_ref):
        for s in range(NDEV):
            @pl.when(pl.program_id(0) == s)
            def _():
                for src, dst, n in _w_in_pieces(s):
                    o_ref[pl.ds(dst, n), :] = g_ref[pl.ds(src, n), :]

        @pl.when(pl.program_id(0) == 0)
        def _():
            o_ref[pl.ds(D_IN, D_INP - D_IN), :] = jnp.zeros((D_INP - D_IN, D), BF16)

    return pl.pallas_call(
        body, name="w_in_rows", grid=(NDEV,), out_shape=_sds((D_INP, D), BF16),
        in_specs=[pl.BlockSpec((None, SLOT_ROWS, D), lambda s: (s, 0, 0))],
        out_specs=pl.BlockSpec((D_INP, D), lambda s: (0, 0)), compiler_params=_params(("arbitrary",)),
    )(gathered)


def _w_in_grad_to_parts(g):
    def body(g_ref, o_ref):
        for s in range(NDEV):
            @pl.when(pl.program_id(0) == s)
            def _():
                for dst, src, n in _w_in_pieces(s):
                    o_ref[pl.ds(dst, n), :] = g_ref[pl.ds(src, n), :]

    return pl.pallas_call(
        body, name="w_in_grad_rows", grid=(NDEV,), out_shape=_sds((2, 4, SLOT_ROWS, D), BF16),
        in_specs=[pl.BlockSpec((D_INP, D), lambda s: (0, 0))],
        out_specs=pl.BlockSpec((None, None, SLOT_ROWS, D), lambda s: (s % 2, s // 2, 0, 0)),
        compiler_params=_params(("arbitrary",)),
    )(g)
C_GLR = 2 * KW + 2 * GW
GATHER_GROUPS = dict(small=("small",), w_in=("w_in",), mix=("w_out", "pool_w"), up=("mlp_w1",), down_a=("mlp_w2a",),
                     down_b=("mlp_w2b",))
GRAD_GROUPS = dict(down=("mlp_w2",), up=("mlp_w1",), mix=("w_out",), w_in=("pool_w", "w_in"))


class _Exchange:
    def __init__(self, first, rest):
        head, token = _gather_start(list(first.values()), "gather_start_first", first["small"])
        token, later = lax.optimization_barrier((token, list(rest.values())))
        tail, self.started = _gather_start([v.astype(BF16) for v in later], "gather_start_rest", token)
        self.state = dict(zip(list(first) + list(rest), head + tail))
        self.my_c = lax.axis_index("c").astype(jnp.int32).reshape(1)
        self.my_chip = (2 * lax.axis_index("x") + lax.axis_index("y")).astype(jnp.int32).reshape(1)
        self.sibling, self.chips, self.done = {}, {}, {}

    def forward(self, group, after):
        ks = GATHER_GROUPS[group]
        fwd, token = _gather_forward([self.state[k] for k in ks], after, "gather_forward_" + group)
        self.state.update(zip(ks, fwd))
        return token

    def weights(self, group, after):
        ks = GATHER_GROUPS[group]
        g = dict(zip(ks, _gather_finish([self.state[k] for k in ks], after, "gather_finish_" + group)))
        if group == "w_in":
            return _w_in_to_layer_order(g["w_in"])
        if group == "small":
            return g["small"]
        if group == "mix":
            return (g["w_out"].reshape(D, D),
                    g["pool_w"].reshape(NDEV, 4, 32, GC).transpose(1, 0, 2, 3).reshape(4, GC, GC))
        return g["mlp_w1"] if group == "up" else g[ks[0]].reshape(DFF, D // 2)

    def grad(self, group, grads):
        parts = dict(grads)
        if group == "w_in":
            parts["w_in"] = _w_in_grad_to_parts(parts["w_in"])
            parts["pool_w"] = (parts["pool_w"].astype(BF16).reshape(4, 4, 2, 32, GC).transpose(2, 1, 0, 3, 4)
                               .reshape(2, 4, 4 * 32, GC))
        ks = GRAD_GROUPS[group]
        started, token = _to_sibling_start([parts[k] for k in ks], "grad_sibling_start_" + group)
        self.sibling[group] = started
        return token

    def grad_mid(self, group, after):
        ks = GRAD_GROUPS[group]
        both = _to_sibling_finish(self.sibling[group], after, "grad_sibling_finish_" + group)
        tile = lambda k, p: (p.shape[2], 512) if k == "w_in" else (SHARD_ROWS[k], p.shape[3])
        sums = [_chip_sum(p, s, self.my_c, tile(k, p), "chip_sum_" + k) for k, (p, s) in zip(ks, both)]
        self.chips[group], token = _to_chips_start(sums, "grad_chips_start_" + group)
        return token

    def chip_sum_side(self, group, after, steps):
        (parts, from_sibling), = _to_sibling_finish(self.sibling[group], after, "grad_sibling_finish_" + group)
        return _chip_sum_side(parts, from_sibling, self.my_c, steps)

    def adamw_side(self, group, after, steps):
        (k, (sums, landed)), = self.grad_finish(group, after).items()
        self.hosted = k
        return _adamw_side(sums, landed, self.my_chip, *self.shards[k], steps)

    def update(self, group, after):
        for k, (sums, landed) in self.grad_finish(group, after).items():
            self.done[k] = _adamw_landed(sums, landed, self.my_chip, *self.shards[k], SHARD_ROWS[k], "adamw_" + k)
            after = self.done[k][1]
        return after

    def grad_chips(self, group, sums):
        self.chips[group], token = _to_chips_start(sums, "grad_chips_start_" + group)
        return token

    def grad_finish(self, group, after):
        done = _to_chips_finish(self.chips[group], after, "grad_chips_finish_" + group)
        return dict(zip(GRAD_GROUPS[group], done))


def _layer_step(x, target, ex, norm1_w, gate_b, gla_norm_w, pool_scale, norm2_w, final_norm_w):
    small = ex.weights("small", ex.forward("small", ex.started))
    meta_full = small[:, 0:4].reshape(NDEV, N_META, D // NDEV).transpose(1, 0, 2).reshape(N_META, D)
    gw2_full = small[:, 4].reshape(NDEV, RANK, KW // NDEV).transpose(1, 0, 2).reshape(RANK, KW)
    gw2p = jnp.pad(gw2_full, ((0, 128 - RANK), (0, 0))).astype(BF16)
    h0 = jnp.concatenate([jnp.zeros((PAD, D), F32), meta_full, x], axis=0)
    u1 = _rmsnorm_fwd(h0, norm1_w, "rmsnorm1", deps=ex.early)
    win_p = ex.weights("w_in", ex.forward("w_in", u1))
    proj = _matmul(u1, win_p, mode="nt", tm=1056, tn=1408, tk=2048, name="proj")
    tok = ex.forward("mix", proj)
    mixed, o_saved, st_saved = _gla_fwd(proj, gw2p, gate_b, gla_norm_w, deps=[tok])
    wout_f, poolw_f = ex.weights("mix", mixed)
    mixed = _pool_fwd(proj, poolw_f, pool_scale, mixed)
    h1 = _matmul(mixed, wout_f, mode="nn", tm=1056, tn=1024, tk=2048, name="mix_out", epi="add", extra=h0)
    tok = ex.forward("up", h1)
    u2 = _rmsnorm_fwd(h1, norm2_w, "rmsnorm2", deps=[tok])
    w1_g = ex.weights("up", u2)
    z, act = _matmul(u2, w1_g, mode="nn", tm=1056, tn=1024, tk=2048, name="mlp_up", epi="relu2", b_slots=True)
    w2a = ex.weights("down_a", ex.forward("down_a", act))
    h2 = _matmul(act, w2a, mode="nn", tm=1056, tn=1024, tk=2048, name="mlp_down_a", epi="add", extra=h1, n_total=D)
    w2b = ex.weights("down_b", ex.forward("down_b", h2))
    h2 = _matmul(act, w2b, mode="nn", tm=1056, tn=1024, tk=2048, name="mlp_down_b", epi="add", extra=h1, n_total=D,
                 col_block=1, into=h2)
    dh2, dh2b, sq, g_fnw = _loss_head(h2, final_norm_w, target)

    g_w2 = _matmul(act, dh2b, mode="tn", tm=512, tn=2048, tk=LP, name="d_mlp_w2", out_dtype=BF16, out_slots="rows")
    tok = ex.grad("down", dict(mlp_w2=g_w2))
    dz = _matmul(dh2b, (w2a, w2b), mode="nt", tm=1056, tn=1024, tk=2048, name="d_act", out_dtype=BF16, epi="dz",
                 extra=z, deps=[tok])
    hosted = lambda res, side: res if side is not None else (res, None)
    side = ex.chip_sum_side("down", dz, 16)
    g_w1, sums = hosted(_matmul(u2, dz, mode="tn", tm=1024, tn=1024, tk=LP, name="d_mlp_w1", out_dtype=BF16,
                                out_slots="cols", side=side), side)
    toks = [ex.grad_chips("down", sums), ex.grad("up", dict(mlp_w1=g_w1))]
    du2 = _matmul(dz, w1_g, mode="nt", tm=1056, tn=1024, tk=2048, name="d_u2", b_slots=True, deps=toks)
    side = ex.chip_sum_side("up", du2, 8)
    dh1, dh1b, g_n2 = _rmsnorm_bwd(h1, norm2_w, du2, dh2, "rmsnorm2_bwd")
    g_wout, sums = hosted(_matmul(mixed, dh1b, mode="tn", tm=256, tn=2048, tk=LP, name="d_w_out", out_dtype=BF16,
                                  out_slots="rows", side=side), side)
    toks = [ex.grad_chips("up", sums), ex.grad("mix", dict(w_out=g_wout))]
    dmixed = _matmul(dh1b, wout_f, mode="nt", tm=1056, tn=1024, tk=2048, name="d_mixed", deps=toks)
    tok = ex.grad_mid("mix", dmixed)
    dproj, dglr, g_gnw, g_gb, g_gw2 = _gla_bwd(proj, dmixed, o_saved, st_saved, gw2p, gate_b, gla_norm_w, deps=[tok])
    dproj, g_poolw, g_psc = _pool_bwd(proj, dmixed, poolw_f, pool_scale, dproj)
    dproj = lax.dynamic_update_slice(dproj, dglr, (0, GLR_BLK * 128))
    g_win_p = _matmul(dproj, u1, mode="tn", tm=384, tn=2048, tk=LP, name="d_w_in", out_dtype=BF16)
    tok = ex.grad("w_in", dict(pool_w=g_poolw, w_in=g_win_p))
    tok = ex.grad_mid("w_in", ex.update("down", tok))
    side = ex.adamw_side("up", tok, 16)
    du1, ex.done[ex.hosted] = hosted(_matmul(dproj, win_p, mode="nn", tm=1056, tn=256, tk=D_INP, name="d_u1",
                                             deps=[tok], side=side), side)
    dx, dhead, g_n1 = _rmsnorm_bwd_input(h0, norm1_w, du1, dh1, "rmsnorm1_bwd")
    return dict(dx=dx, dhead=dhead, sq=sq, gate_w2=g_gw2, norm1_w=g_n1, norm2_w=g_n2, final_norm_w=g_fnw, pool_scale=g_psc,
                gate_b=g_gb, gla_norm_w=g_gnw)
```

```python
import functools

import jax
import jax.numpy as jnp
from jax import lax
from jax.experimental import pallas as pl
from jax.experimental.pallas import tpu as pltpu

F32, BF16 = jnp.float32, jnp.bfloat16
MESH = pl.DeviceIdType.MESH

NDEV = 8
D = 2048
SEQ = 2048
N_META = 16
CHUNK = 64
PAD = (-N_META) % CHUNK
ROW_X = PAD + N_META
LP = ROW_X + SEQ
NCH = LP // CHUNK
H = 4
DK = 128
DV = 256
KW = H * DK
GW = H * DV
PW = 1024
RANK = 16
TAU = 16.0
WINDOWS = (2, 4, 8, 16)
GC = 256
DFF = 4 * D
EPS = 1e-6
D_IN = 2 * KW + 2 * GW + RANK + PW
D_INP = 4224
GLR_BLK = (2 * KW + 2 * GW + PW) // 128
POOL_BLK = (2 * KW + 2 * GW) // GC
LR, B1, B2, AEPS, WD, STEP = 0.001, 0.9, 0.999, 1e-08, 0.01, 10
VMEM_LIMIT = 48 * 1024 * 1024
CPS = 3


def _params(sem=None):
    return pltpu.CompilerParams(dimension_semantics=sem, vmem_limit_bytes=VMEM_LIMIT)


def _sds(shape, dtype):
    return jax.ShapeDtypeStruct(shape, dtype)


def _me():
    return lax.axis_index("x"), lax.axis_index("y"), lax.axis_index("c")


def _peer(j):
    x, y, c = _me()
    return (x ^ ((j >> 2) & 1), y ^ ((j >> 1) & 1), c ^ (j & 1))


def _slot(dev):
    return 4 * dev[0] + 2 * dev[1] + dev[2]


HBM_SPEC = pl.BlockSpec(memory_space=pltpu.HBM)
SEM_SPEC = pl.BlockSpec(memory_space=pltpu.SEMAPHORE)
ANY_SPEC = pl.BlockSpec(memory_space=pl.ANY)
EFFECT = pltpu.SideEffectType.DATAFLOW_SIDE_EFFECTING
SIBLING = 1
OTHER_CHIPS = (2, 4, 6)


def _in_hbm(a):
    return pltpu.with_memory_space_constraint(a, pltpu.HBM)


def _chip(dev):
    return 2 * dev[0] + dev[1]


def _rcopy(src, dst, send_sem, recv_sem, to):
    return pltpu.make_async_remote_copy(src_ref=src, dst_ref=dst, send_sem=send_sem, recv_sem=recv_sem,
                                        device_id=to, device_id_type=MESH)


def _split_call(body, name, ins, in_specs, out_shape, out_specs, aliases, scratch=()):
    n = len(ins) + len(out_shape)

    def with_token(*refs):
        body(*refs[:n], *refs[n + 1:])
        refs[n][...] = jnp.zeros_like(refs[n])

    return pl.pallas_call(
        with_token, name=name, in_specs=in_specs, out_shape=list(out_shape) + [_sds((8, 128), F32)],
        out_specs=list(out_specs) + [pl.BlockSpec(memory_space=pltpu.VMEM)],
        input_output_aliases=aliases, scratch_shapes=list(scratch),
        compiler_params=pltpu.CompilerParams(has_side_effects=EFFECT),
    )(*ins)


def _after(body, n_in, deps):
    deps = [d for d in deps if d is not None]
    if not deps:
        return body, [], []
    return (lambda *refs: body(*refs[:n_in], *refs[n_in + len(deps):])), deps, [ANY_SPEC] * len(deps)


def _gather_start(shards, name, after):
    n = len(shards)
    me = _slot(_me())
    lands = [lax.dynamic_update_slice(lax.empty((NDEV,) + s.shape, s.dtype), s[None], (me, 0, 0)) for s in shards]

    def body(*refs):
        src, land = refs[:n], refs[n:2 * n]
        outs = refs[2 * n + 1:]
        for i in range(n):
            send_sems, recv_sems = outs[4 * i], outs[4 * i + 1]
            for k, rel in enumerate((SIBLING,) + OTHER_CHIPS):
                _rcopy(src[i], land[i].at[_slot(_me())], send_sems.at[k], recv_sems.at[k], _peer(rel)).start()

    out_shape, out_specs, aliases = [], [], {}
    for i, s in enumerate(shards):
        out_shape += [pltpu.SemaphoreType.DMA((4,)), pltpu.SemaphoreType.DMA((4,)), pltpu.HBM(s.shape, s.dtype),
                      pltpu.HBM((NDEV,) + s.shape, s.dtype)]
        out_specs += [SEM_SPEC, SEM_SPEC, HBM_SPEC, HBM_SPEC]
        aliases[i] = 4 * i + 2
        aliases[n + i] = 4 * i + 3
    res = _split_call(body, name, [_in_hbm(s) for s in shards] + [_in_hbm(l) for l in lands] + [after],
                      [HBM_SPEC] * (2 * n) + [ANY_SPEC], out_shape, out_specs, aliases)
    return [tuple(res[4 * i:4 * i + 4]) for i in range(n)], res[-1]


def _gather_forward(started, after, name):
    n = len(started)

    def body(*refs):
        land, recv1 = refs[:n], refs[n:2 * n]
        outs = refs[2 * n + 1:]
        for i in range(n):
            send2, recv2 = outs[3 * i + 1], outs[3 * i + 2]
            for k, rel in enumerate(OTHER_CHIPS):
                blk = land[i].at[_slot(_peer(rel))]
                _rcopy(blk, blk, send2.at[k], recv1[i].at[1 + k], _peer(rel)).wait_recv()
                _rcopy(blk, blk, send2.at[k], recv2.at[k], _peer(SIBLING)).start()

    ins = [_in_hbm(st[3]) for st in started] + [st[1] for st in started] + [after]
    out_shape, out_specs, aliases = [], [], {}
    for i, st in enumerate(started):
        out_shape += [pltpu.HBM(st[3].shape, st[3].dtype), pltpu.SemaphoreType.DMA((3,)), pltpu.SemaphoreType.DMA((3,))]
        out_specs += [HBM_SPEC, SEM_SPEC, SEM_SPEC]
        aliases[i] = 3 * i
    res = _split_call(body, name, ins, [HBM_SPEC] * n + [SEM_SPEC] * n + [ANY_SPEC], out_shape, out_specs, aliases)
    return [(st[0], st[1], st[2], res[3 * i], res[3 * i + 1], res[3 * i + 2]) for i, st in enumerate(started)], res[-1]


def _gather_finish(forwarded, after, name):
    n = len(forwarded)

    def body(*refs):
        for i in range(n):
            send1, recv1, src, land, send2, recv2 = refs[6 * i:6 * i + 6]
            me = _slot(_me())
            sib = _slot(_peer(SIBLING))
            for k, rel in enumerate((SIBLING,) + OTHER_CHIPS):
                _rcopy(src, land.at[me], send1.at[k], recv1.at[k], _peer(rel)).wait_send()
            _rcopy(src, land.at[sib], send1.at[0], recv1.at[0], _peer(SIBLING)).wait_recv()
            for k, rel in enumerate(OTHER_CHIPS):
                mine, theirs = land.at[_slot(_peer(rel))], land.at[_slot(_peer(rel ^ SIBLING))]
                _rcopy(mine, mine, send2.at[k], recv2.at[k], _peer(SIBLING)).wait_send()
                _rcopy(theirs, theirs, send2.at[k], recv2.at[k], _peer(SIBLING)).wait_recv()

    ins, in_specs, out_shape, aliases = [], [], [], {}
    for i, f in enumerate(forwarded):
        ins += [f[0], f[1], _in_hbm(f[2]), _in_hbm(f[3]), f[4], f[5]]
        in_specs += [SEM_SPEC, SEM_SPEC, HBM_SPEC, HBM_SPEC, SEM_SPEC, SEM_SPEC]
        out_shape.append(pltpu.HBM(f[3].shape, f[3].dtype))
        aliases[6 * i + 3] = i
    res = _split_call(body, name, ins + [after], in_specs + [ANY_SPEC], out_shape, [HBM_SPEC] * n, aliases)
    return list(res[:-1])


def _to_sibling_start(parts, name):
    n = len(parts)
    lands = [lax.empty(p.shape[1:], p.dtype) for p in parts]

    def body(*refs):
        src, land = refs[:n], refs[n:2 * n]
        outs = refs[2 * n:]
        other = 1 - lax.axis_index("c")
        for i in range(n):
            _rcopy(src[i].at[other], land[i], outs[4 * i], outs[4 * i + 1], _peer(SIBLING)).start()

    out_shape, out_specs, aliases = [], [], {}
    for i, p in enumerate(parts):
        out_shape += [pltpu.SemaphoreType.DMA(()), pltpu.SemaphoreType.DMA(()), pltpu.HBM(p.shape, p.dtype),
                      pltpu.HBM(p.shape[1:], p.dtype)]
        out_specs += [SEM_SPEC, SEM_SPEC, HBM_SPEC, HBM_SPEC]
        aliases[i] = 4 * i + 2
        aliases[n + i] = 4 * i + 3
    res = _split_call(body, name, [_in_hbm(p) for p in parts] + [_in_hbm(l) for l in lands], [HBM_SPEC] * (2 * n),
                      out_shape, out_specs, aliases)
    return [tuple(res[4 * i:4 * i + 4]) for i in range(n)], res[-1]


def _to_sibling_finish(started, after, name):
    n = len(started)

    def body(*refs):
        for i in range(n):
            send, recv, src, land = refs[4 * i:4 * i + 4]
            cp = _rcopy(src.at[0], land, send, recv, _peer(SIBLING))
            cp.wait_send()
            cp.wait_recv()

    ins, in_specs, out_shape, aliases = [], [], [], {}
    for i, st in enumerate(started):
        ins += [st[0], st[1], _in_hbm(st[2]), _in_hbm(st[3])]
        in_specs += [SEM_SPEC, SEM_SPEC, HBM_SPEC, HBM_SPEC]
        out_shape += [pltpu.HBM(st[2].shape, st[2].dtype), pltpu.HBM(st[3].shape, st[3].dtype)]
        aliases[4 * i + 2] = 2 * i
        aliases[4 * i + 3] = 2 * i + 1
    res = _split_call(body, name, ins + [after], in_specs + [ANY_SPEC], out_shape, [HBM_SPEC] * (2 * n), aliases)
    return [(res[2 * i], res[2 * i + 1]) for i in range(n)]


def _chip_sum(parts, from_sibling, my_c, tile, name):
    _, _, r, c = parts.shape
    tr, tc = tile

    def body(c_ref, p_ref, s_ref, o_ref):
        o_ref[...] = (p_ref[...].astype(F32) + s_ref[...].astype(F32)).astype(o_ref.dtype)

    blk = pl.BlockSpec((4, tr, tc), lambda i, j, c_ref: (0, i, j))
    return pl.pallas_call(
        body, name=name, out_shape=_sds((4, r, c), parts.dtype),
        grid_spec=pltpu.PrefetchScalarGridSpec(
            num_scalar_prefetch=1, grid=(r // tr, c // tc),
            in_specs=[pl.BlockSpec((None, 4, tr, tc), lambda i, j, c_ref: (c_ref[0], 0, i, j)), blk], out_specs=blk),
        compiler_params=_params(("parallel", "parallel")),
    )(my_c, parts, from_sibling)


class _Side:
    def __init__(self, scalar, ins, outs, fn):
        self.scalar, self.ins, self.outs, self.fn = scalar, ins, outs, fn


def _chip_sum_side(parts, from_sibling, my_c, steps):
    _, _, r, c = parts.shape
    rows = r // steps
    assert rows * steps == r and rows % 16 == 0
    return _Side(
        my_c,
        [(parts, (None, 4, rows, c), lambda t, s: (s[0], 0, t, 0)), (from_sibling, (4, rows, c), lambda t, s: (0, t, 0))],
        [((4, r, c), parts.dtype, (4, rows, c), lambda t, s: (0, t, 0))],
        lambda p, q: [(p.astype(F32) + q.astype(F32)).astype(parts.dtype)])


def _to_chips_start(sums, name):
    n = len(sums)
    lands = [lax.empty((3,) + s.shape[1:], s.dtype) for s in sums]

    def body(*refs):
        src, land = refs[:n], refs[n:2 * n]
        outs = refs[2 * n:]
        for i in range(n):
            for k, rel in enumerate(OTHER_CHIPS):
                to = _peer(rel)
                _rcopy(src[i].at[_chip(to)], land[i].at[k], outs[4 * i].at[k], outs[4 * i + 1].at[k], to).start()

    out_shape, out_specs, aliases = [], [], {}
    for i, s in enumerate(sums):
        out_shape += [pltpu.SemaphoreType.DMA((3,)), pltpu.SemaphoreType.DMA((3,)), pltpu.HBM(s.shape, s.dtype),
                      pltpu.HBM((3,) + s.shape[1:], s.dtype)]
        out_specs += [SEM_SPEC, SEM_SPEC, HBM_SPEC, HBM_SPEC]
        aliases[i] = 4 * i + 2
        aliases[n + i] = 4 * i + 3
    res = _split_call(body, name, [_in_hbm(s) for s in sums] + [_in_hbm(l) for l in lands], [HBM_SPEC] * (2 * n),
                      out_shape, out_specs, aliases)
    return [tuple(res[4 * i:4 * i + 4]) for i in range(n)], res[-1]


def _to_chips_finish(started, after, name):
    n = len(started)

    def body(*refs):
        for i in range(n):
            send, recv, src, land = refs[4 * i:4 * i + 4]
            for k, rel in enumerate(OTHER_CHIPS):
                cp = _rcopy(src.at[0], land.at[k], send.at[k], recv.at[k], _peer(rel))
                cp.wait_send()
                cp.wait_recv()

    ins, in_specs, out_shape, aliases = [], [], [], {}
    for i, st in enumerate(started):
        ins += [st[0], st[1], _in_hbm(st[2]), _in_hbm(st[3])]
        in_specs += [SEM_SPEC, SEM_SPEC, HBM_SPEC, HBM_SPEC]
        out_shape += [pltpu.HBM(st[2].shape, st[2].dtype), pltpu.HBM(st[3].shape, st[3].dtype)]
        aliases[4 * i + 2] = 2 * i
        aliases[4 * i + 3] = 2 * i + 1
    res = _split_call(body, name, ins + [after], in_specs + [ANY_SPEC], out_shape, [HBM_SPEC] * (2 * n), aliases)
    return [(res[2 * i], res[2 * i + 1]) for i in range(n)]


def _reduce_small(v, name, deps=()):
    _, r, c = v.shape

    def body(v_ref, o_ref, land, send_sems, recv_sems):
        me = _slot(_me())
        copies = []
        for j in range(1, NDEV):
            to = _peer(j)
            cp = _rcopy(v_ref.at[_slot(to)], land.at[me], send_sems.at[j - 1], recv_sems.at[j - 1], to)
            cp.start()
            copies.append(cp)
        land[me] = v_ref[me]
        for cp in copies:
            cp.wait()
        acc = land[0]
        for k in range(1, NDEV):
            acc = acc + land[k]
        o_ref[...] = acc

    vm = pl.BlockSpec(memory_space=pltpu.VMEM)
    body, dep_ins, dep_specs = _after(body, 1, deps)
    return pl.pallas_call(
        body, name=name, out_shape=_sds((r, c), F32), in_specs=[vm] + dep_specs, out_specs=vm,
        scratch_shapes=[pltpu.VMEM((NDEV, r, c), F32), pltpu.SemaphoreType.DMA((NDEV - 1,)),
                        pltpu.SemaphoreType.DMA((NDEV - 1,))],
        compiler_params=_params(),
    )(v, *dep_ins)


def _matmul(a, b, *, mode, tm, tn, tk, name, out_dtype=F32, epi=None, extra=None, b_slots=False, out_slots=False,
            deps=(), col_block=0, into=None, n_total=None, side=None):
    b_pair = b if isinstance(b, tuple) else None
    if b_pair:
        assert mode == "nt" and tk == 2 * b[0].shape[1] == a.shape[1] and not b_slots
        b = b[0]
    slot_w = b.shape[-1] if b_slots else None
    if mode == "nn":
        M, K = a.shape
        N = NDEV * slot_w if b_slots else b.shape[1]
    elif mode == "tn":
        K, M = a.shape
        N = b.shape[1]
    else:
        M, K = a.shape
        N = b.shape[-2]
        if b_slots:
            assert K == NDEV * slot_w and tk % slot_w == 0
    if mode == "nn" and b_slots:
        assert tn == slot_w
    if out_slots == "cols":
        assert tn * NDEV == N
    if out_slots == "rows":
        assert (M // NDEV) % tm == 0
    assert M % tm == 0 and N % tn == 0 and K % tk == 0, (name, M, N, K, tm, tn, tk)
    nk = K // tk
    dims = {"nn": ((1,), (0,)), "tn": ((0,), (0,)), "nt": ((1,), (1,))}[mode]

    if mode == "tn":
        a_spec = pl.BlockSpec((tk, tm), lambda i, j, k: (k, i))
    else:
        a_spec = pl.BlockSpec((tm, tk), lambda i, j, k: (i, k))
    if b_pair:
        b_spec = pl.BlockSpec((tn, tk // 2), lambda i, j, k: (j, 0))
    elif mode == "nt":
        b_spec = (pl.BlockSpec((tk // slot_w, tn, slot_w), lambda i, j, k: (k, j, 0)) if b_slots
                  else pl.BlockSpec((tn, tk), lambda i, j, k: (j, k)))
    else:
        b_spec = (pl.BlockSpec((None, tk, tn), lambda i, j, k: (j, k, 0)) if b_slots
                  else pl.BlockSpec((tk, tn), lambda i, j, k: (k, j)))
    tile = pl.BlockSpec((tm, tn), lambda i, j, k: (i, j + col_block))
    if out_slots == "cols":
        out_spec = pl.BlockSpec((None, None, tm, tn), lambda i, j, k: (j % 2, j // 2, i, 0))
        out_shape = _sds((2, 4, M, tn), out_dtype)
    elif out_slots == "rows":
        per = M // NDEV // tm
        out_spec = pl.BlockSpec((None, None, tm, tn), lambda i, j, k: ((i // per) % 2, (i // per) // 2, i % per, j))
        out_shape = _sds((2, 4, M // NDEV, N), out_dtype)
    else:
        out_spec, out_shape = tile, _sds((M, n_total or N), out_dtype)
    ins, in_specs = [a, b], [a_spec, b_spec]
    if b_pair:
        ins.append(b_pair[1])
        in_specs.append(b_spec)
    n_b = len(ins) - 1
    if epi in ("add", "dz"):
        ins.append(extra)
        in_specs.append(tile)
    aliases = {}
    if into is not None:
        aliases[len(ins)] = 0
        ins.append(into)
        in_specs.append(ANY_SPEC)
    if epi == "relu2":
        out_specs, out_shapes = [tile, tile], [_sds((M, N), F32), _sds((M, N), BF16)]
    else:
        out_specs, out_shapes = out_spec, out_shape
    n_in = len(ins)

    def body(*refs):
        outs = refs[n_in:-1] if nk > 1 else refs[n_in:]
        extra_ref = refs[1 + n_b]

        def finish(p):
            if epi is None:
                outs[0][...] = p.astype(out_dtype)
            elif epi == "add":
                outs[0][...] = (p + extra_ref[...]).astype(out_dtype)
            elif epi == "relu2":
                outs[0][...] = p
                rz = jnp.maximum(p, 0.0)
                outs[1][...] = (rz * rz).astype(BF16)
            else:
                outs[0][...] = (p * (2.0 * jnp.maximum(extra_ref[...], 0.0))).astype(out_dtype)

        def product():
            av = refs[0][...].astype(BF16)
            if b_pair:
                half = tk // 2
                return (lax.dot_general(av[:, :half], refs[1][...], (dims, ((), ())), preferred_element_type=F32)
                        + lax.dot_general(av[:, half:], refs[2][...], (dims, ((), ())), preferred_element_type=F32))
            if mode == "nt" and b_slots:
                return sum(lax.dot_general(av[:, s * slot_w:(s + 1) * slot_w], refs[1][s], (dims, ((), ())),
                                           preferred_element_type=F32) for s in range(tk // slot_w))
            return lax.dot_general(av, refs[1][...].astype(BF16), (dims, ((), ())), preferred_element_type=F32)

        if nk == 1:
            finish(product())
            return
        acc = refs[-1]
        k = pl.program_id(2)

        @pl.when(k == 0)
        def _():
            acc[...] = jnp.zeros_like(acc)

        acc[...] += product()

        @pl.when(k == nk - 1)
        def _():
            finish(acc[...])

    grid = (M // tm, N // tn, nk)
    scratch = [pltpu.VMEM((tm, tn), F32)] if nk > 1 else []
    if side is None:
        body, dep_ins, dep_specs = _after(body, n_in, deps)
        return pl.pallas_call(
            body, name=name, grid=grid,
            in_specs=in_specs + dep_specs, out_specs=out_specs, out_shape=out_shapes, input_output_aliases=aliases,
            scratch_shapes=scratch, compiler_params=_params(("parallel", "parallel", "arbitrary")),
        )(*ins, *dep_ins)

    deps = [d for d in deps if d is not None]
    step = lambda i, j, k: (i * grid[1] + j) * grid[2] + k
    host = lambda spec: (spec if spec.block_shape is None else
                         pl.BlockSpec(spec.block_shape, lambda i, j, k, s, f=spec.index_map: f(i, j, k)))
    cut = lambda blk, f: pl.BlockSpec(blk, lambda i, j, k, s: f(step(i, j, k), s))
    host_out_specs = list(out_specs) if isinstance(out_specs, list) else [out_specs]
    host_out_shapes = list(out_shapes) if isinstance(out_shapes, list) else [out_shapes]
    n_dep, n_si, n_ho, n_so = len(deps), len(side.ins), len(host_out_specs), len(side.outs)

    def with_side(*refs):
        rest = refs[1:]
        side_in = rest[n_in + n_dep:n_in + n_dep + n_si]
        outs_all = rest[n_in + n_dep + n_si:]
        body(*rest[:n_in], *outs_all[:n_ho], *outs_all[n_ho + n_so:])
        for o_ref, val in zip(outs_all[n_ho:n_ho + n_so], side.fn(*[r[...] for r in side_in])):
            o_ref[...] = val

    res = pl.pallas_call(
        with_side, name=name, input_output_aliases={k + 1: v for k, v in aliases.items()},
        out_shape=host_out_shapes + [_sds(shape, dt) for shape, dt, _, _ in side.outs],
        grid_spec=pltpu.PrefetchScalarGridSpec(
            num_scalar_prefetch=1, grid=grid,
            in_specs=[host(s) for s in in_specs] + [ANY_SPEC] * n_dep + [cut(blk, f) for _, blk, f in side.ins],
            out_specs=[host(s) for s in host_out_specs] + [cut(blk, f) for _, _, blk, f in side.outs],
            scratch_shapes=scratch),
        compiler_params=_params(("parallel", "parallel", "arbitrary")),
    )(side.scalar, *ins, *deps, *[arr for arr, _, _ in side.ins])
    host_res = res[0] if n_ho == 1 else tuple(res[:n_ho])
    return host_res, list(res[n_ho:])


ROWS = 352


def _rmsnorm_fwd(h, w, name, deps=()):
    def body(h_ref, w_ref, u_ref):
        x = h_ref[...]
        rstd = lax.rsqrt(jnp.mean(x * x, axis=-1, keepdims=True) + EPS)
        u_ref[...] = (x * rstd * w_ref[...]).astype(BF16)

    row = pl.BlockSpec((ROWS, D), lambda i: (i, 0))
    body, dep_ins, dep_specs = _after(body, 2, deps)
    return pl.pallas_call(
        body, name=name, grid=(LP // ROWS,), in_specs=[row, pl.BlockSpec((1, D), lambda i: (0, 0))] + dep_specs,
        out_specs=row, out_shape=_sds((LP, D), BF16), compiler_params=_params(("parallel",)),
    )(h, w, *dep_ins)


TOKEN_ROWS = 512


def _rmsnorm_bwd_input(h, w, du, dres, name, deps=()):
    def math(h_ref, w_ref, du_ref, dres_ref):
        x = h_ref[...]
        rstd = lax.rsqrt(jnp.mean(x * x, axis=-1, keepdims=True) + EPS)
        xhat = x * rstd
        dy = du_ref[...]
        dxh = dy * w_ref[...]
        dh = dres_ref[...] + rstd * (dxh - xhat * jnp.mean(dxh * xhat, axis=-1, keepdims=True))
        return dh, jnp.sum(dy * xhat, axis=0, keepdims=True)

    def body(h_ref, w_ref, du_ref, dres_ref, hh_ref, duh_ref, dresh_ref, dx_ref, dhead_ref, gw_ref):
        dx_ref[...], part = math(h_ref, w_ref, du_ref, dres_ref)

        @pl.when(pl.program_id(0) == 0)
        def _():
            dhead_ref[...], head = math(hh_ref, w_ref, duh_ref, dresh_ref)
            gw_ref[...] = part + head

        @pl.when(pl.program_id(0) > 0)
        def _():
            gw_ref[...] += part

    rows = pl.BlockSpec((pl.Element(TOKEN_ROWS), pl.Element(D)),
                        lambda i: (pl.multiple_of(ROW_X + TOKEN_ROWS * i, 8), 0))
    head = pl.BlockSpec((ROW_X, D), lambda i: (0, 0))
    vec = pl.BlockSpec((1, D), lambda i: (0, 0))
    body, dep_ins, dep_specs = _after(body, 7, deps)
    return pl.pallas_call(
        body, name=name, grid=(SEQ // TOKEN_ROWS,),
        in_specs=[rows, vec, rows, rows, head, head, head] + dep_specs,
        out_specs=[pl.BlockSpec((TOKEN_ROWS, D), lambda i: (i, 0)), head, vec],
        out_shape=[_sds((SEQ, D), F32), _sds((ROW_X, D), F32), _sds((1, D), F32)],
        compiler_params=_params(("arbitrary",)),
    )(h, w, du, dres, h, du, dres, *dep_ins)


def _rmsnorm_bwd(h, w, du, dres, name, deps=()):
    def body(h_ref, w_ref, du_ref, dres_ref, dh_ref, dhb_ref, gw_ref):
        x = h_ref[...]
        rstd = lax.rsqrt(jnp.mean(x * x, axis=-1, keepdims=True) + EPS)
        xhat = x * rstd
        dy = du_ref[...]
        dxh = dy * w_ref[...]
        dh = dres_ref[...] + rstd * (dxh - xhat * jnp.mean(dxh * xhat, axis=-1, keepdims=True))
        dh_ref[...] = dh
        dhb_ref[...] = dh.astype(BF16)
        part = jnp.sum(dy * xhat, axis=0, keepdims=True)

        @pl.when(pl.program_id(0) == 0)
        def _():
            gw_ref[...] = part

        @pl.when(pl.program_id(0) > 0)
        def _():
            gw_ref[...] += part

    row = pl.BlockSpec((ROWS, D), lambda i: (i, 0))
    vec = pl.BlockSpec((1, D), lambda i: (0, 0))
    body, dep_ins, dep_specs = _after(body, 4, deps)
    return pl.pallas_call(
        body, name=name, grid=(LP // ROWS,), in_specs=[row, vec, row, row] + dep_specs, out_specs=[row, row, vec],
        out_shape=[_sds((LP, D), F32), _sds((LP, D), BF16), _sds((1, D), F32)],
        compiler_params=_params(("arbitrary",)),
    )(h, w, du, dres, *dep_ins)


def _loss_head(h2, wf, target):
    def body(h_ref, w_ref, t_ref, dh_ref, dhb_ref, sq_ref, gw_ref):
        i = pl.program_id(0)

        @pl.when(i == 0)
        def _():
            sq_ref[...] = jnp.zeros_like(sq_ref)
            gw_ref[...] = jnp.zeros_like(gw_ref)

        def rows(t, live):
            x = h_ref[...]
            rstd = lax.rsqrt(jnp.mean(x * x, axis=-1, keepdims=True) + EPS)
            xhat = x * rstd
            w = w_ref[...]
            err = xhat * w - t
            if live is not None:
                err = jnp.where(live, err, 0.0)
            sq_ref[...] += jnp.sum(err * err, axis=0, keepdims=True)
            dy = err * (1.0 / D)
            gw_ref[...] += jnp.sum(dy * xhat, axis=0, keepdims=True)
            dxh = dy * w
            dh = rstd * (dxh - xhat * jnp.mean(dxh * xhat, axis=-1, keepdims=True))
            dh_ref[...] = dh
            dhb_ref[...] = dh.astype(BF16)

        @pl.when(i == 0)
        def _():
            rid = lax.broadcasted_iota(jnp.int32, (ROWS, D), 0)
            rows(pltpu.roll(t_ref[...], ROW_X, 0), rid >= ROW_X)

        @pl.when(i > 0)
        def _():
            rows(t_ref[...], None)

    row = pl.BlockSpec((ROWS, D), lambda i: (i, 0))
    vec = pl.BlockSpec((1, D), lambda i: (0, 0))
    tgt = pl.BlockSpec((pl.Element(ROWS), pl.Element(D)),
                       lambda i: (pl.multiple_of(jnp.maximum(ROWS * i - ROW_X, 0), 8), 0))
    return pl.pallas_call(
        body, name="loss_head", grid=(LP // ROWS,),
        in_specs=[row, vec, tgt],
        out_specs=[row, row, vec, vec],
        out_shape=[_sds((LP, D), F32), _sds((LP, D), BF16), _sds((1, D), F32), _sds((1, D), F32)],
        compiler_params=_params(("arbitrary",)),
    )(h2, wf, target)


def _dot(a, b, dims):
    return lax.dot_general(a, b, (dims, ((), ())), preferred_element_type=F32)


NN, TN, NT = ((1,), (0,)), ((0,), (0,)), ((1,), (1,))


def _tri_sum(t, x):
    hi = x.astype(BF16)
    r1 = x - hi.astype(F32)
    mid = r1.astype(BF16)
    lo = (r1 - mid.astype(F32)).astype(BF16)
    return _dot(t, hi, NN) + _dot(t, mid, NN) + _dot(t, lo, NN)


def _gla_gates(glr_ref, gw2_ref, gb_ref, rows, row0):
    g_raw = _dot(glr_ref[rows, :].astype(BF16), gw2_ref[...], NN) + gb_ref[...]
    logsig = jnp.minimum(g_raw, 0.0) - jnp.log(1.0 + jnp.exp(-jnp.abs(g_raw)))
    rid = row0 + lax.broadcasted_iota(jnp.int32, g_raw.shape, 0)
    live = rid >= PAD
    return g_raw, jnp.where(live, logsig / TAU, 0.0), live


def _tri_masks():
    r = lax.broadcasted_iota(jnp.int32, (CHUNK, CHUNK), 0)
    c = lax.broadcasted_iota(jnp.int32, (CHUNK, CHUNK), 1)
    return r >= c


def _gla_specs(rev):
    n = NCH // CPS
    R = CPS * CHUNK
    st = (lambda s: n - 1 - s) if rev else (lambda s: s)
    return R, n, st, [
        pl.BlockSpec((R, KW), lambda s: (st(s), 0)),
        pl.BlockSpec((R, KW), lambda s: (st(s), 1)),
        pl.BlockSpec((R, GW), lambda s: (st(s), 1)),
        pl.BlockSpec((R, GW), lambda s: (st(s), 2)),
        pl.BlockSpec((R, 128), lambda s: (st(s), GLR_BLK)),
    ]


def _gla_fwd(proj, gw2p, gate_b, gnw, deps=()):
    R, n, st, pspecs = _gla_specs(False)

    def body(q_ref, k_ref, v_ref, r_ref, glr_ref, gw2_ref, gb_ref, gnw_ref, og_ref, o_ref, st_ref, state):
        s = pl.program_id(0)

        @pl.when(s == 0)
        def _():
            state[...] = jnp.zeros_like(state)

        causal = _tri_masks()
        tri = causal.astype(BF16)
        for c in range(CPS):
            rows = slice(c * CHUNK, (c + 1) * CHUNK)
            _, logg, _ = _gla_gates(glr_ref, gw2_ref, gb_ref, rows, s * R + c * CHUNK)
            G = _tri_sum(tri, logg)
            g_last = G[CHUNK - 1:CHUNK, :]
            q_dec = (q_ref[rows, :] * (DK ** -0.5) * jnp.exp(G)).astype(BF16)
            kk = k_ref[rows, :]
            k_inv = (kk * jnp.exp(-G)).astype(BF16)
            k_end = (kk * jnp.exp(g_last - G)).astype(BF16)
            decay = jnp.exp(g_last)
            for h in range(H):
                lk = slice(h * DK, (h + 1) * DK)
                lv = slice(h * DV, (h + 1) * DV)
                v = v_ref[rows, lv].astype(BF16)
                S = state[h]
                st_ref[c, h] = S
                A = jnp.where(causal, _dot(q_dec[:, lk], k_inv[:, lk], NT), 0.0).astype(BF16)
                o = _dot(A, v, NN) + _dot(q_dec[:, lk], S.astype(BF16), NT)
                state[h] = decay[:, lk] * S + _dot(v, k_end[:, lk], TN)
                o_ref[rows, lv] = o
                on = o * lax.rsqrt(jnp.mean(o * o, axis=-1, keepdims=True) + EPS) * gnw_ref[...]
                rr = r_ref[rows, lv]
                og_ref[rows, lv] = (on * (rr * jax.nn.sigmoid(rr))).astype(BF16)

    full = lambda shape: pl.BlockSpec(shape, lambda s: (0,) * len(shape))
    body, dep_ins, dep_specs = _after(body, 8, deps)
    return pl.pallas_call(
        body, name="gla_fwd", grid=(n,),
        in_specs=pspecs + [full((128, KW)), full((1, KW)), full((1, DV))] + dep_specs,
        out_specs=[pl.BlockSpec((R, GW), lambda s: (s, 0)), pl.BlockSpec((R, GW), lambda s: (s, 0)),
                   pl.BlockSpec((CPS, H, DV, DK), lambda s: (s, 0, 0, 0))],
        out_shape=[_sds((LP, GW + PW), BF16), _sds((LP, GW), F32), _sds((NCH, H, DV, DK), F32)],
        scratch_shapes=[pltpu.VMEM((H, DV, DK), F32)],
        compiler_params=_params(("arbitrary",)),
    )(proj, proj, proj, proj, proj, gw2p, gate_b, gnw, *dep_ins)


def _gla_bwd(proj, dmixed, o_saved, st_saved, gw2p, gate_b, gnw, deps=()):
    R, n, st, pspecs = _gla_specs(True)

    def body(q_ref, k_ref, v_ref, r_ref, glr_ref, dog_ref, o_ref, st_ref, gw2_ref, gb_ref, gnw_ref,
             dqkvr_ref, dglr_ref, ggn_ref, ggb_ref, ggw_ref, gstate):
        s = pl.program_id(0)

        @pl.when(s == 0)
        def _():
            gstate[...] = jnp.zeros_like(gstate)
            ggn_ref[...] = jnp.zeros_like(ggn_ref)
            ggb_ref[...] = jnp.zeros_like(ggb_ref)
            ggw_ref[...] = jnp.zeros_like(ggw_ref)

        causal = _tri_masks()
        tri = causal.astype(BF16)
        tri_up = (lax.broadcasted_iota(jnp.int32, (CHUNK, CHUNK), 0)
                  <= lax.broadcasted_iota(jnp.int32, (CHUNK, CHUNK), 1)).astype(BF16)
        gnw = gnw_ref[...]
        for c in reversed(range(CPS)):
            rows = slice(c * CHUNK, (c + 1) * CHUNK)
            g_raw, logg, live = _gla_gates(glr_ref, gw2_ref, gb_ref, rows, (n - 1 - s) * R + c * CHUNK)
            G = _tri_sum(tri, logg)
            g_last = G[CHUNK - 1:CHUNK, :]
            e_g, e_gi, e_end = jnp.exp(G), jnp.exp(-G), jnp.exp(g_last - G)
            q_dec = q_ref[rows, :] * (DK ** -0.5) * e_g
            kk = k_ref[rows, :]
            k_inv, k_end = kk * e_gi, kk * e_end
            q_dec_b, k_inv_b, k_end_b = q_dec.astype(BF16), k_inv.astype(BF16), k_end.astype(BF16)
            decay = jnp.exp(g_last)
            d_g, d_gl = [], []
            for h in range(H):
                lk = slice(h * DK, (h + 1) * DK)
                lv = slice(h * DV, (h + 1) * DV)
                o = o_ref[rows, lv]
                rr = r_ref[rows, lv]
                dog = dog_ref[rows, lv]
                rstd = lax.rsqrt(jnp.mean(o * o, axis=-1, keepdims=True) + EPS)
                ohat = o * rstd
                sr = jax.nn.sigmoid(rr)
                don = dog * (rr * sr)
                dqkvr_ref[rows, 2 * KW + GW + h * DV:2 * KW + GW + (h + 1) * DV] = (
                    dog * (ohat * gnw) * (sr * (1.0 + rr * (1.0 - sr)))).astype(BF16)
                ggn_ref[...] += jnp.sum(don * ohat, axis=0, keepdims=True)
                dohat = don * gnw
                do = (rstd * (dohat - ohat * jnp.mean(dohat * ohat, axis=-1, keepdims=True))).astype(BF16)
                v = v_ref[rows, lv].astype(BF16)
                S = st_ref[c, h]
                gS = gstate[h]
                S_b, gS_b = S.astype(BF16), gS.astype(BF16)
                qd, ki, ke = q_dec_b[:, lk], k_inv_b[:, lk], k_end_b[:, lk]
                A = jnp.where(causal, _dot(qd, ki, NT), 0.0).astype(BF16)
                dA = jnp.where(causal, _dot(do, v, NT), 0.0).astype(BF16)
                dv = _dot(A, do, TN) + _dot(ke, gS_b, NT)
                dq_dec = _dot(dA, ki, NN) + _dot(do, S_b, NN)
                dk_inv = _dot(dA, qd, TN)
                dk_end = _dot(v, gS_b, NN)
                d_decay = jnp.sum(gS * S, axis=0, keepdims=True)
                gstate[h] = decay[:, lk] * gS + _dot(do, qd, TN)
                dqkvr_ref[rows, lk] = (dq_dec * e_g[:, lk] * (DK ** -0.5)).astype(BF16)
                dqkvr_ref[rows, KW + h * DK:KW + (h + 1) * DK] = (
                    dk_inv * e_gi[:, lk] + dk_end * e_end[:, lk]).astype(BF16)
                dqkvr_ref[rows, 2 * KW + h * DV:2 * KW + (h + 1) * DV] = dv.astype(BF16)
                ke_prod = dk_end * k_end[:, lk]
                d_g.append(dq_dec * q_dec[:, lk] - dk_inv * k_inv[:, lk] - ke_prod)
                d_gl.append(jnp.sum(ke_prod, axis=0, keepdims=True) + d_decay * decay[:, lk])
            dlogg = _tri_sum(tri_up, jnp.concatenate(d_g, axis=1)) + jnp.concatenate(d_gl, axis=1)
            dg_raw = jnp.where(live, dlogg * (1.0 / TAU) * jax.nn.sigmoid(-g_raw), 0.0)
            ggb_ref[...] += jnp.sum(dg_raw, axis=0, keepdims=True)
            dg_b = dg_raw.astype(BF16)
            ggw_ref[...] += _dot(glr_ref[rows, :].astype(BF16), dg_b, TN)
            dglr_ref[rows, :] = _dot(dg_b, gw2_ref[...], NT).astype(BF16)

    full = lambda shape: pl.BlockSpec(shape, lambda s: (0,) * len(shape))
    body, dep_ins, dep_specs = _after(body, 11, deps)
    return pl.pallas_call(
        body, name="gla_bwd", grid=(n,),
        in_specs=pspecs + [pl.BlockSpec((R, GW), lambda s: (st(s), 0)), pl.BlockSpec((R, GW), lambda s: (st(s), 0)),
                           pl.BlockSpec((CPS, H, DV, DK), lambda s: (st(s), 0, 0, 0)),
                           full((128, KW)), full((1, KW)), full((1, DV))] + dep_specs,
        out_specs=[pl.BlockSpec((R, 2 * KW + 2 * GW), lambda s: (st(s), 0)), pl.BlockSpec((R, 128), lambda s: (st(s), 0)),
                   full((1, DV)), full((1, KW)), full((128, KW))],
        out_shape=[_sds((LP, D_INP), BF16), _sds((LP, 128), BF16),
                   _sds((1, DV), F32), _sds((1, KW), F32), _sds((128, KW), F32)],
        scratch_shapes=[pltpu.VMEM((H, DV, DK), F32)],
        compiler_params=_params(("arbitrary",)),
    )(proj, proj, proj, proj, proj, dmixed, o_saved, st_saved, gw2p, gate_b, gnw, *dep_ins)


def _pool_pre(x, win, rid):
    s, step = x, 1
    while step < win:
        s = s + pltpu.roll(s, step, 0)
        step *= 2
    cnt = jnp.clip(rid - (PAD - 1), 1, win).astype(F32)
    live = rid >= PAD
    return jnp.where(live, s / cnt - x, 0.0), cnt, live


def _pool_fwd(proj, pool_w, pool_scale, mixed):
    def body(pu_ref, w_ref, sc_ref, _, o_ref):
        rid = lax.broadcasted_iota(jnp.int32, (LP, GC), 0)
        for g, win in enumerate(WINDOWS):
            @pl.when(pl.program_id(0) == g)
            def _():
                y, _, _ = _pool_pre(pu_ref[...], win, rid)
                o_ref[...] = (_dot(y.astype(BF16), w_ref[...], NN) * sc_ref[...]).astype(BF16)

    col = lambda base: pl.BlockSpec((LP, GC), lambda g: (0, base + g))
    return pl.pallas_call(
        body, name="pool_fwd", grid=(len(WINDOWS),),
        in_specs=[col(POOL_BLK), pl.BlockSpec((None, GC, GC), lambda g: (g, 0, 0)),
                  pl.BlockSpec((1, GC), lambda g: (0, g)), ANY_SPEC],
        out_specs=col(GW // GC), out_shape=_sds(mixed.shape, BF16), input_output_aliases={3: 0},
        compiler_params=_params(("parallel",)),
    )(proj, pool_w, pool_scale, mixed)


def _pool_bwd(proj, dmixed, pool_w, pool_scale, dproj):
    def body(pu_ref, do_ref, w_ref, sc_ref, _, dpu_ref, dw_ref, dsc_ref):
        rid = lax.broadcasted_iota(jnp.int32, (LP, GC), 0)
        for g, win in enumerate(WINDOWS):
            @pl.when(pl.program_id(0) == g)
            def _():
                y, cnt, live = _pool_pre(pu_ref[...], win, rid)
                y_b = y.astype(BF16)
                w = w_ref[...]
                do = do_ref[...]
                dsc_ref[...] = jnp.sum(do * _dot(y_b, w, NN), axis=0, keepdims=True)
                dyw = (do * sc_ref[...]).astype(BF16)
                dw_ref[...] = _dot(y_b, dyw, TN)
                dy = jnp.where(live, _dot(dyw, w, NT), 0.0)
                s, step = dy / cnt, 1
                while step < win:
                    s = s + pltpu.roll(s, LP - step, 0)
                    step *= 2
                dpu_ref[...] = (s - dy).astype(BF16)

    col = lambda base: pl.BlockSpec((LP, GC), lambda g: (0, base + g))
    mat = pl.BlockSpec((None, GC, GC), lambda g: (g, 0, 0))
    vec = pl.BlockSpec((1, GC), lambda g: (0, g))
    return pl.pallas_call(
        body, name="pool_bwd", grid=(len(WINDOWS),),
        in_specs=[col(POOL_BLK), col(GW // GC), mat, vec, ANY_SPEC], out_specs=[col(POOL_BLK), mat, vec],
        out_shape=[_sds(dproj.shape, BF16), _sds((4, GC, GC), F32), _sds((1, PW), F32)],
        input_output_aliases={4: 0}, compiler_params=_params(("parallel",)),
    )(proj, dmixed, pool_w, pool_scale, dproj)


def _adamw_math(w, g, m, v):
    m = B1 * m + (1.0 - B1) * g
    v = B2 * v + (1.0 - B2) * (g * g)
    m_hat = m * (1.0 / (1.0 - B1 ** STEP))
    v_hat = v * (1.0 / (1.0 - B2 ** STEP))
    return -LR * (m_hat / (jnp.sqrt(v_hat) + AEPS) + WD * w), m, v


def _adamw_landed(sums, landed, my_chip, w, m, v, rows, name, cols=None):
    _, r, c = w.shape

    def body(chip_ref, s_ref, l_ref, w_ref, m_ref, v_ref, g_ref, d_ref, mo_ref, vo_ref):
        g = s_ref[...].astype(F32)
        for k in range(3):
            g = g + l_ref[k].astype(F32)
        g_ref[...] = g
        d_ref[...], mo_ref[...], vo_ref[...] = _adamw_math(w_ref[...], g, m_ref[...], v_ref[...])

    cols = cols or c
    blk = pl.BlockSpec((None, rows, cols), lambda i, j, chip_ref: (0, i, j))
    return pl.pallas_call(
        body, name=name, out_shape=[_sds((1, r, c), F32)] * 4,
        grid_spec=pltpu.PrefetchScalarGridSpec(
            num_scalar_prefetch=1, grid=(r // rows, c // cols),
            in_specs=[pl.BlockSpec((None, rows, cols), lambda i, j, chip_ref: (chip_ref[0], i, j)),
                      pl.BlockSpec((3, rows, cols), lambda i, j, chip_ref: (0, i, j)), blk, blk, blk],
            out_specs=[blk] * 4),
        compiler_params=_params(("parallel", "parallel")),
    )(my_chip, sums, landed, w, m, v)


def _adamw_side(sums, landed, my_chip, w, m, v, steps):
    _, r, c = w.shape
    rows = r // steps
    assert rows * steps == r and rows % 16 == 0

    def fn(s, l, w_, m_, v_):
        g = s.astype(F32)
        for k in range(3):
            g = g + l[k].astype(F32)
        return [g, *_adamw_math(w_, g, m_, v_)]

    one = lambda t, s: (0, t, 0)
    return _Side(
        my_chip,
        [(sums, (None, rows, c), lambda t, s: (s[0], t, 0)), (landed, (3, rows, c), one)]
        + [(a, (None, rows, c), one) for a in (w, m, v)],
        [((1, r, c), F32, (None, rows, c), one)] * 4, fn)


SMALL_PLACES = (
    ("norm1_w", (1, D), ((0, 0, 1024), (1, 0, 1024))),
    ("norm2_w", (1, D), ((2, 0, 1024), (3, 0, 1024))),
    ("final_norm_w", (1, D), ((4, 0, 1024), (5, 0, 1024))),
    ("pool_scale", (1, PW), ((6, 0, 1024),)),
    ("gate_b", (1, KW), ((7, 0, KW),)),
    ("gla_norm_w", (1, DV), ((7, KW, DV),)),
    ("meta_tokens", (4, 1024), None),
    ("gate_w2", (1, 1024), ((12, 0, 1024),)),
)


def _adamw_small(g, w, m, v):
    n = len(SMALL_PLACES)

    def body(g_ref, w_ref, m_ref, v_ref, *refs):
        outs, buf = refs[:-1], refs[-1]
        gv = g_ref[...]
        for a, val in enumerate((gv, *_adamw_math(w_ref[...], gv, m_ref[...], v_ref[...]))):
            buf[a] = val
            for j, (_, _, pieces) in enumerate(SMALL_PLACES):
                o_ref = outs[a * n + j]
                if pieces is None:
                    o_ref[...] = buf[a, pl.ds(8, 4), :]
                else:
                    o_ref[...] = jnp.concatenate([buf[a, pl.ds(r, 1), pl.ds(l, k)] for r, l, k in pieces], axis=1)

    return pl.pallas_call(
        body, name="adamw_small", out_shape=[_sds(shape, F32) for _ in range(4) for _, shape, _ in SMALL_PLACES],
        scratch_shapes=[pltpu.VMEM((4,) + w.shape, F32)],
    )(g, w, m, v)


SMALL_REPL = (("norm1_w", D), ("norm2_w", D), ("final_norm_w", D), ("pool_scale", PW), ("gate_b", KW),
              ("gla_norm_w", DV))


def _pack_rows(vecs, rows):
    flat = jnp.concatenate([jnp.ravel(v) for v in vecs])
    return jnp.pad(flat, (0, rows * 1024 - flat.shape[0])).reshape(rows, 1024)


def kernel(x, meta_tokens, norm1_w, w_in, gate_w2, gate_b, gla_norm_w, pool_w, pool_scale, w_out, norm2_w, mlp_w1, mlp_w2, final_norm_w, loss_target, m_meta_tokens, m_norm1_w, m_w_in, m_gate_w2, m_gate_b, m_gla_norm_w, m_pool_w, m_pool_scale, m_w_out, m_norm2_w, m_mlp_w1, m_mlp_w2, m_final_norm_w, v_meta_tokens, v_norm1_w, v_w_in, v_gate_w2, v_gate_b, v_gla_norm_w, v_pool_w, v_pool_scale, v_w_out, v_norm2_w, v_mlp_w1, v_mlp_w2, v_final_norm_w):
    W = dict(meta_tokens=meta_tokens, norm1_w=norm1_w, w_in=w_in, gate_w2=gate_w2, gate_b=gate_b,
             gla_norm_w=gla_norm_w, pool_w=pool_w, pool_scale=pool_scale, w_out=w_out, norm2_w=norm2_w,
             mlp_w1=mlp_w1, mlp_w2=mlp_w2, final_norm_w=final_norm_w)
    Mo = dict(meta_tokens=m_meta_tokens, norm1_w=m_norm1_w, w_in=m_w_in, gate_w2=m_gate_w2, gate_b=m_gate_b,
              gla_norm_w=m_gla_norm_w, pool_w=m_pool_w, pool_scale=m_pool_scale, w_out=m_w_out, norm2_w=m_norm2_w,
              mlp_w1=m_mlp_w1, mlp_w2=m_mlp_w2, final_norm_w=m_final_norm_w)
    Vo = dict(meta_tokens=v_meta_tokens, norm1_w=v_norm1_w, w_in=v_w_in, gate_w2=v_gate_w2, gate_b=v_gate_b,
              gla_norm_w=v_gla_norm_w, pool_w=v_pool_w, pool_scale=v_pool_scale, w_out=v_w_out, norm2_w=v_norm2_w,
              mlp_w1=v_mlp_w1, mlp_w2=v_mlp_w2, final_norm_w=v_final_norm_w)

    ex = _Exchange(dict(small=_pack_rows([meta_tokens, gate_w2[0]], 8), w_in=w_in[0].T.astype(BF16)),
                   dict(w_out=w_out[0], pool_w=pool_w[0].reshape(4 * 32, GC), mlp_w1=mlp_w1[0],
                        mlp_w2a=mlp_w2[0][:, :D // 2], mlp_w2b=mlp_w2[0][:, D // 2:]))
    tr = lambda a: a[0].T[None]
    win_t, m_win_t, v_win_t = tr(w_in), tr(m_w_in), tr(v_w_in)
    ex.early = [win_t, m_win_t, v_win_t]
    ex.shards = {k: (W[k], Mo[k], Vo[k]) for k in ("w_out", "mlp_w1", "mlp_w2")}
    step = _layer_step(x[0], loss_target[0], ex, norm1_w, gate_b, gla_norm_w, pool_scale, norm2_w,
                       final_norm_w.reshape(1, D))
    grad_x = step["dx"][None]

    last = ex.update("mix", step["dx"])
    out = dict(ex.done)

    loss_part = 0.5 * jnp.sum(step["sq"]) / D
    to_all = _pack_rows([step[k] for k, _ in SMALL_REPL] + [loss_part], 8)
    cols = lambda g: g.reshape(g.shape[0], NDEV, -1).transpose(1, 0, 2).reshape(NDEV, -1, 1024)
    packed = jnp.concatenate([jnp.broadcast_to(to_all, (NDEV, 8, 1024)), cols(step["dhead"][PAD:]),
                              cols(step["gate_w2"][:RANK]), jnp.zeros((NDEV, 3, 1024), F32)], axis=1)
    red = _reduce_small(packed, "reduce_small", deps=[last])
    loss = red[7, 768]

    done = ex.grad_finish("w_in", red)
    poolw3 = lambda a: a.reshape(1, 4 * 32, GC)
    res = _adamw_landed(*done["pool_w"], ex.my_chip, poolw3(pool_w), poolw3(m_pool_w), poolw3(v_pool_w),
                        SHARD_ROWS["pool_w"], "adamw_pool_w")
    out["pool_w"] = [a.reshape(pool_w.shape) for a in res]
    res = _adamw_landed(*done["w_in"], ex.my_chip, win_t, m_win_t, v_win_t, D_IN // NDEV, "adamw_w_in", cols=256)
    out["w_in"] = [a[0].T[None] for a in res]

    def small_pack(P):
        return jnp.concatenate([_pack_rows([P[k] for k, _ in SMALL_REPL], 8),
                                _pack_rows([P["meta_tokens"], P["gate_w2"]], 8)], axis=0)

    res_small = _adamw_small(red.at[7, 768].set(0.0), small_pack(W), small_pack(Mo), small_pack(Vo))
    for j, (k, _, _) in enumerate(SMALL_PLACES):
        out[k] = [res_small[a * len(SMALL_PLACES) + j].reshape(W[k].shape) for a in range(4)]

    order = ["meta_tokens", "norm1_w", "w_in", "gate_w2", "gate_b", "gla_norm_w", "pool_w", "pool_scale", "w_out",
             "norm2_w", "mlp_w1", "mlp_w2", "final_norm_w"]
    return (loss, grad_x, *[out[k][0] for k in order], *[out[k][1] for k in order],
            *[out[k][2] for k in order], *[out[k][3] for k in order])


SHARD_ROWS = dict(w_out=256, mlp_w1=512, mlp_w2=256, pool_w=128)
SLOT_ROWS = D_IN // NDEV


def _w_in_pieces(s):
    lo, hi, out = s * SLOT_ROWS, (s + 1) * SLOT_ROWS, []
    for a, b, shift in ((0, C_GLR, 0), (C_GLR, C_GLR + RANK, PW), (C_GLR + RANK, D_IN, -RANK)):
        a, b = max(a, lo), min(b, hi)
        if a < b:
            out.append((a - lo, a + shift, b - a))
    return out


def _w_in_to_layer_order(gathered):
    def body(g_ref, o_ref):
        for s in range(NDEV):
            @pl.when(pl.program_id(0) == s)
            def _():
                for src, dst, n in _w_in_pieces(s):
                    o_ref[pl.ds(dst, n), :] = g_ref[pl.ds(src, n), :]

        @pl.when(pl.program_id(0) == 0)
        def _():
            o_ref[pl.ds(D_IN, D_INP - D_IN), :] = jnp.zeros((D_INP - D_IN, D), BF16)

    return pl.pallas_call(
        body, name="w_in_rows", grid=(NDEV,), out_shape=_sds((D_INP, D), BF16),
        in_specs=[pl.BlockSpec((None, SLOT_ROWS, D), lambda s: (s, 0, 0))],
        out_specs=pl.BlockSpec((D_INP, D), lambda s: (0, 0)), compiler_params=_params(("arbitrary",)),
    )(gathered)


def _w_in_grad_to_parts(g):
    def body(g_ref, o_ref):
        for s in range(NDEV):
            @pl.when(pl.program_id(0) == s)
            def _():
                for dst, src, n in _w_in_pieces(s):
                    o_ref[pl.ds(dst, n), :] = g_ref[pl.ds(src, n), :]

    return pl.pallas_call(
        body, name="w_in_grad_rows", grid=(NDEV,), out_shape=_sds((2, 4, SLOT_ROWS, D), BF16),
        in_specs=[pl.BlockSpec((D_INP, D), lambda s: (0, 0))],
        out_specs=pl.BlockSpec((None, None, SLOT_ROWS, D), lambda s: (s % 2, s // 2, 0, 0)),
        compiler_params=_params(("arbitrary",)),
    )(g)
C_GLR = 2 * KW + 2 * GW
GATHER_GROUPS = dict(small=("small",), w_in=("w_in",), mix=("w_out", "pool_w"), up=("mlp_w1",), down_a=("mlp_w2a",),
                     down_b=("mlp_w2b",))
GRAD_GROUPS = dict(down=("mlp_w2",), up=("mlp_w1",), mix=("w_out",), w_in=("pool_w", "w_in"))


class _Exchange:
    def __init__(self, first, rest):
        head, token = _gather_start(list(first.values()), "gather_start_first", first["small"])
        token, later = lax.optimization_barrier((token, list(rest.values())))
        tail, self.started = _gather_start([v.astype(BF16) for v in later], "gather_start_rest", token)
        self.state = dict(zip(list(first) + list(rest), head + tail))
        self.my_c = lax.axis_index("c").astype(jnp.int32).reshape(1)
        self.my_chip = (2 * lax.axis_index("x") + lax.axis_index("y")).astype(jnp.int32).reshape(1)
        self.sibling, self.chips, self.done = {}, {}, {}

    def forward(self, group, after):
        ks = GATHER_GROUPS[group]
        fwd, token = _gather_forward([self.state[k] for k in ks], after, "gather_forward_" + group)
        self.state.update(zip(ks, fwd))
        return token

    def weights(self, group, after):
        ks = GATHER_GROUPS[group]
        g = dict(zip(ks, _gather_finish([self.state[k] for k in ks], after, "gather_finish_" + group)))
        if group == "w_in":
            return _w_in_to_layer_order(g["w_in"])
        if group == "small":
            return g["small"]
        if group == "mix":
            return (g["w_out"].reshape(D, D),
                    g["pool_w"].reshape(NDEV, 4, 32, GC).transpose(1, 0, 2, 3).reshape(4, GC, GC))
        return g["mlp_w1"] if group == "up" else g[ks[0]].reshape(DFF, D // 2)

    def grad(self, group, grads):
        parts = dict(grads)
        if group == "w_in":
            parts["w_in"] = _w_in_grad_to_parts(parts["w_in"])
            parts["pool_w"] = (parts["pool_w"].astype(BF16).reshape(4, 4, 2, 32, GC).transpose(2, 1, 0, 3, 4)
                               .reshape(2, 4, 4 * 32, GC))
        ks = GRAD_GROUPS[group]
        started, token = _to_sibling_start([parts[k] for k in ks], "grad_sibling_start_" + group)
        self.sibling[group] = started
        return token

    def grad_mid(self, group, after):
        ks = GRAD_GROUPS[group]
        both = _to_sibling_finish(self.sibling[group], after, "grad_sibling_finish_" + group)
        tile = lambda k, p: (p.shape[2], 512) if k == "w_in" else (SHARD_ROWS[k], p.shape[3])
        sums = [_chip_sum(p, s, self.my_c, tile(k, p), "chip_sum_" + k) for k, (p, s) in zip(ks, both)]
        self.chips[group], token = _to_chips_start(sums, "grad_chips_start_" + group)
        return token

    def chip_sum_side(self, group, after, steps):
        (parts, from_sibling), = _to_sibling_finish(self.sibling[group], after, "grad_sibling_finish_" + group)
        return _chip_sum_side(parts, from_sibling, self.my_c, steps)

    def adamw_side(self, group, after, steps):
        (k, (sums, landed)), = self.grad_finish(group, after).items()
        self.hosted = k
        return _adamw_side(sums, landed, self.my_chip, *self.shards[k], steps)

    def update(self, group, after):
        for k, (sums, landed) in self.grad_finish(group, after).items():
            self.done[k] = _adamw_landed(sums, landed, self.my_chip, *self.shards[k], SHARD_ROWS[k], "adamw_" + k)
            after = self.done[k][1]
        return after

    def grad_chips(self, group, sums):
        self.chips[group], token = _to_chips_start(sums, "grad_chips_start_" + group)
        return token

    def grad_finish(self, group, after):
        done = _to_chips_finish(self.chips[group], after, "grad_chips_finish_" + group)
        return dict(zip(GRAD_GROUPS[group], done))


def _layer_step(x, target, ex, norm1_w, gate_b, gla_norm_w, pool_scale, norm2_w, final_norm_w):
    small = ex.weights("small", ex.forward("small", ex.started))
    meta_full = small[:, 0:4].reshape(NDEV, N_META, D // NDEV).transpose(1, 0, 2).reshape(N_META, D)
    gw2_full = small[:, 4].reshape(NDEV, RANK, KW // NDEV).transpose(1, 0, 2).reshape(RANK, KW)
    gw2p = jnp.pad(gw2_full, ((0, 128 - RANK), (0, 0))).astype(BF16)
    h0 = jnp.concatenate([jnp.zeros((PAD, D), F32), meta_full, x], axis=0)
    u1 = _rmsnorm_fwd(h0, norm1_w, "rmsnorm1", deps=ex.early)
    win_p = ex.weights("w_in", ex.forward("w_in", u1))
    proj = _matmul(u1, win_p, mode="nt", tm=1056, tn=1408, tk=2048, name="proj")
    tok = ex.forward("mix", proj)
    mixed, o_saved, st_saved = _gla_fwd(proj, gw2p, gate_b, gla_norm_w, deps=[tok])
    wout_f, poolw_f = ex.weights("mix", mixed)
    mixed = _pool_fwd(proj, poolw_f, pool_scale, mixed)
    h1 = _matmul(mixed, wout_f, mode="nn", tm=1056, tn=1024, tk=2048, name="mix_out", epi="add", extra=h0)
    tok = ex.forward("up", h1)
    u2 = _rmsnorm_fwd(h1, norm2_w, "rmsnorm2", deps=[tok])
    w1_g = ex.weights("up", u2)
    z, act = _matmul(u2, w1_g, mode="nn", tm=1056, tn=1024, tk=2048, name="mlp_up", epi="relu2", b_slots=True)
    w2a = ex.weights("down_a", ex.forward("down_a", act))
    h2 = _matmul(act, w2a, mode="nn", tm=1056, tn=1024, tk=2048, name="mlp_down_a", epi="add", extra=h1, n_total=D)
    w2b = ex.weights("down_b", ex.forward("down_b", h2))
    h2 = _matmul(act, w2b, mode="nn", tm=1056, tn=1024, tk=2048, name="mlp_down_b", epi="add", extra=h1, n_total=D,
                 col_block=1, into=h2)
    dh2, dh2b, sq, g_fnw = _loss_head(h2, final_norm_w, target)

    g_w2 = _matmul(act, dh2b, mode="tn", tm=512, tn=2048, tk=LP, name="d_mlp_w2", out_dtype=BF16, out_slots="rows")
    tok = ex.grad("down", dict(mlp_w2=g_w2))
    dz = _matmul(dh2b, (w2a, w2b), mode="nt", tm=1056, tn=1024, tk=2048, name="d_act", out_dtype=BF16, epi="dz",
                 extra=z, deps=[tok])
    hosted = lambda res, side: res if side is not None else (res, None)
    side = ex.chip_sum_side("down", dz, 16)
    g_w1, sums = hosted(_matmul(u2, dz, mode="tn", tm=1024, tn=1024, tk=LP, name="d_mlp_w1", out_dtype=BF16,
                                out_slots="cols", side=side), side)
    toks = [ex.grad_chips("down", sums), ex.grad("up", dict(mlp_w1=g_w1))]
    du2 = _matmul(dz, w1_g, mode="nt", tm=1056, tn=1024, tk=2048, name="d_u2", b_slots=True, deps=toks)
    side = ex.chip_sum_side("up", du2, 8)
    dh1, dh1b, g_n2 = _rmsnorm_bwd(h1, norm2_w, du2, dh2, "rmsnorm2_bwd")
    g_wout, sums = hosted(_matmul(mixed, dh1b, mode="tn", tm=256, tn=2048, tk=LP, name="d_w_out", out_dtype=BF16,
                                  out_slots="rows", side=side), side)
    toks = [ex.grad_chips("up", sums), ex.grad("mix", dict(w_out=g_wout))]
    dmixed = _matmul(dh1b, wout_f, mode="nt", tm=1056, tn=1024, tk=2048, name="d_mixed", deps=toks)
    tok = ex.grad_mid("mix", dmixed)
    dproj, dglr, g_gnw, g_gb, g_gw2 = _gla_bwd(proj, dmixed, o_saved, st_saved, gw2p, gate_b, gla_norm_w, deps=[tok])
    dproj, g_poolw, g_psc = _pool_bwd(proj, dmixed, poolw_f, pool_scale, dproj)
    dproj = lax.dynamic_update_slice(dproj, dglr, (0, GLR_BLK * 128))
    g_win_p = _matmul(dproj, u1, mode="tn", tm=384, tn=2048, tk=LP, name="d_w_in", out_dtype=BF16)
    tok = ex.grad("w_in", dict(pool_w=g_poolw, w_in=g_win_p))
    tok = ex.grad_mid("w_in", ex.update("down", tok))
    side = ex.adamw_side("up", tok, 16)
    du1, ex.done[ex.hosted] = hosted(_matmul(dproj, win_p, mode="nn", tm=1056, tn=256, tk=D_INP, name="d_u1",
                                             deps=[tok], side=side), side)
    dx, dhead, g_n1 = _rmsnorm_bwd_input(h0, norm1_w, du1, dh1, "rmsnorm1_bwd")
    return dict(dx=dx, dhead=dhead, sq=sq, gate_w2=g_gw2, norm1_w=g_n1, norm2_w=g_n2, final_norm_w=g_fnw, pool_scale=g_psc,
                gate_b=g_gb, gla_norm_w=g_gnw)
```

```python
import jax
import jax.numpy as jnp
from jax import lax
from jax.experimental import pallas as pl
from jax.experimental.pallas import tpu as pltpu

F32, BF16 = jnp.float32, jnp.bfloat16
MESH = pl.DeviceIdType.MESH

NDEV = 8
D = 2048
SEQ = 2048
N_META = 16
CHUNK = 64
PAD = (-N_META) % CHUNK
ROW_X = PAD + N_META
LP = ROW_X + SEQ
NCH = LP // CHUNK
H = 4
DK = 128
DV = 256
KW = H * DK
GW = H * DV
PW = 1024
RANK = 16
TAU = 16.0
WINDOWS = (2, 4, 8, 16)
GC = 256
DFF = 4 * D
EPS = 1e-6
D_IN = 2 * KW + 2 * GW + RANK + PW
D_INP = 4224
GLR_BLK = (2 * KW + 2 * GW + PW) // 128
POOL_BLK = (2 * KW + 2 * GW) // GC
LR, B1, B2, AEPS, WD, STEP = 0.001, 0.9, 0.999, 1e-08, 0.01, 10
VMEM_LIMIT = 48 * 1024 * 1024
CPS = 3


def _params(sem=None):
    return pltpu.CompilerParams(dimension_semantics=sem, vmem_limit_bytes=VMEM_LIMIT)


def _sds(shape, dtype):
    return jax.ShapeDtypeStruct(shape, dtype)


def _me():
    return lax.axis_index("x"), lax.axis_index("y"), lax.axis_index("c")


def _peer(j):
    x, y, c = _me()
    return (x ^ ((j >> 2) & 1), y ^ ((j >> 1) & 1), c ^ (j & 1))


def _slot(dev):
    return 4 * dev[0] + 2 * dev[1] + dev[2]


HBM_SPEC = pl.BlockSpec(memory_space=pltpu.HBM)
SEM_SPEC = pl.BlockSpec(memory_space=pltpu.SEMAPHORE)
ANY_SPEC = pl.BlockSpec(memory_space=pl.ANY)
EFFECT = pltpu.SideEffectType.DATAFLOW_SIDE_EFFECTING
SIBLING = 1
OTHER_CHIPS = (2, 4, 6)


def _in_hbm(a):
    return pltpu.with_memory_space_constraint(a, pltpu.HBM)


def _chip(dev):
    return 2 * dev[0] + dev[1]


def _rcopy(src, dst, send_sem, recv_sem, to):
    return pltpu.make_async_remote_copy(src_ref=src, dst_ref=dst, send_sem=send_sem, recv_sem=recv_sem,
                                        device_id=to, device_id_type=MESH)


def _split_call(body, name, ins, in_specs, out_shape, out_specs, aliases, scratch=()):
    n = len(ins) + len(out_shape)

    def with_token(*refs):
        body(*refs[:n], *refs[n + 1:])
        refs[n][...] = jnp.zeros_like(refs[n])

    return pl.pallas_call(
        with_token, name=name, in_specs=in_specs, out_shape=list(out_shape) + [_sds((8, 128), F32)],
        out_specs=list(out_specs) + [pl.BlockSpec(memory_space=pltpu.VMEM)],
        input_output_aliases=aliases, scratch_shapes=list(scratch),
        compiler_params=pltpu.CompilerParams(has_side_effects=EFFECT),
    )(*ins)


def _after(body, n_in, deps):
    deps = [d for d in deps if d is not None]
    if not deps:
        return body, [], []
    return (lambda *refs: body(*refs[:n_in], *refs[n_in + len(deps):])), deps, [ANY_SPEC] * len(deps)


def _gather_start(shards, name, after):
    n = len(shards)
    me = _slot(_me())
    lands = [lax.dynamic_update_slice(lax.empty((NDEV,) + s.shape, s.dtype), s[None], (me, 0, 0)) for s in shards]

    def body(*refs):
        src, land = refs[:n], refs[n:2 * n]
        outs = refs[2 * n + 1:]
        for i in range(n):
            send_sems, recv_sems = outs[4 * i], outs[4 * i + 1]
            for k, rel in enumerate((SIBLING,) + OTHER_CHIPS):
                _rcopy(src[i], land[i].at[_slot(_me())], send_sems.at[k], recv_sems.at[k], _peer(rel)).start()

    out_shape, out_specs, aliases = [], [], {}
    for i, s in enumerate(shards):
        out_shape += [pltpu.SemaphoreType.DMA((4,)), pltpu.SemaphoreType.DMA((4,)), pltpu.HBM(s.shape, s.dtype),
                      pltpu.HBM((NDEV,) + s.shape, s.dtype)]
        out_specs += [SEM_SPEC, SEM_SPEC, HBM_SPEC, HBM_SPEC]
        aliases[i] = 4 * i + 2
        aliases[n + i] = 4 * i + 3
    res = _split_call(body, name, [_in_hbm(s) for s in shards] + [_in_hbm(l) for l in lands] + [after],
                      [HBM_SPEC] * (2 * n) + [ANY_SPEC], out_shape, out_specs, aliases)
    return [tuple(res[4 * i:4 * i + 4]) for i in range(n)], res[-1]


def _gather_forward(started, after, name):
    n = len(started)

    def body(*refs):
        land, recv1 = refs[:n], refs[n:2 * n]
        outs = refs[2 * n + 1:]
        for i in range(n):
            send2, recv2 = outs[3 * i + 1], outs[3 * i + 2]
            for k, rel in enumerate(OTHER_CHIPS):
                blk = land[i].at[_slot(_peer(rel))]
                _rcopy(blk, blk, send2.at[k], recv1[i].at[1 + k], _peer(rel)).wait_recv()
                _rcopy(blk, blk, send2.at[k], recv2.at[k], _peer(SIBLING)).start()

    ins = [_in_hbm(st[3]) for st in started] + [st[1] for st in started] + [after]
    out_shape, out_specs, aliases = [], [], {}
    for i, st in enumerate(started):
        out_shape += [pltpu.HBM(st[3].shape, st[3].dtype), pltpu.SemaphoreType.DMA((3,)), pltpu.SemaphoreType.DMA((3,))]
        out_specs += [HBM_SPEC, SEM_SPEC, SEM_SPEC]
        aliases[i] = 3 * i
    res = _split_call(body, name, ins, [HBM_SPEC] * n + [SEM_SPEC] * n + [ANY_SPEC], out_shape, out_specs, aliases)
    return [(st[0], st[1], st[2], res[3 * i], res[3 * i + 1], res[3 * i + 2]) for i, st in enumerate(started)], res[-1]


def _gather_finish(forwarded, after, name):
    n = len(forwarded)

    def body(*refs):
        for i in range(n):
            send1, recv1, src, land, send2, recv2 = refs[6 * i:6 * i + 6]
            me = _slot(_me())
            sib = _slot(_peer(SIBLING))
            for k, rel in enumerate((SIBLING,) + OTHER_CHIPS):
                _rcopy(src, land.at[me], send1.at[k], recv1.at[k], _peer(rel)).wait_send()
            _rcopy(src, land.at[sib], send1.at[0], recv1.at[0], _peer(SIBLING)).wait_recv()
            for k, rel in enumerate(OTHER_CHIPS):
                mine, theirs = land.at[_slot(_peer(rel))], land.at[_slot(_peer(rel ^ SIBLING))]
                _rcopy(mine, mine, send2.at[k], recv2.at[k], _peer(SIBLING)).wait_send()
                _rcopy(theirs, theirs, send2.at[k], recv2.at[k], _peer(SIBLING)).wait_recv()

    ins, in_specs, out_shape, aliases = [], [], [], {}
    for i, f in enumerate(forwarded):
        ins += [f[0], f[1], _in_hbm(f[2]), _in_hbm(f[3]), f[4], f[5]]
        in_specs += [SEM_SPEC, SEM_SPEC, HBM_SPEC, HBM_SPEC, SEM_SPEC, SEM_SPEC]
        out_shape.append(pltpu.HBM(f[3].shape, f[3].dtype))
        aliases[6 * i + 3] = i
    res = _split_call(body, name, ins + [after], in_specs + [ANY_SPEC], out_shape, [HBM_SPEC] * n, aliases)
    return list(res[:-1])


def _to_sibling_start(parts, name):
    n = len(parts)
    lands = [lax.empty(p.shape[1:], p.dtype) for p in parts]

    def body(*refs):
        src, land = refs[:n], refs[n:2 * n]
        outs = refs[2 * n:]
        other = 1 - lax.axis_index("c")
        for i in range(n):
            _rcopy(src[i].at[other], land[i], outs[4 * i], outs[4 * i + 1], _peer(SIBLING)).start()

    out_shape, out_specs, aliases = [], [], {}
    for i, p in enumerate(parts):
        out_shape += [pltpu.SemaphoreType.DMA(()), pltpu.SemaphoreType.DMA(()), pltpu.HBM(p.shape, p.dtype),
                      pltpu.HBM(p.shape[1:], p.dtype)]
        out_specs += [SEM_SPEC, SEM_SPEC, HBM_SPEC, HBM_SPEC]
        aliases[i] = 4 * i + 2
        aliases[n + i] = 4 * i + 3
    res = _split_call(body, name, [_in_hbm(p) for p in parts] + [_in_hbm(l) for l in lands], [HBM_SPEC] * (2 * n),
                      out_shape, out_specs, aliases)
    return [tuple(res[4 * i:4 * i + 4]) for i in range(n)], res[-1]


def _to_sibling_finish(started, after, name):
    n = len(started)

    def body(*refs):
        for i in range(n):
            send, recv, src, land = refs[4 * i:4 * i + 4]
            cp = _rcopy(src.at[0], land, send, recv, _peer(SIBLING))
            cp.wait_send()
            cp.wait_recv()

    ins, in_specs, out_shape, aliases = [], [], [], {}
    for i, st in enumerate(started):
        ins += [st[0], st[1], _in_hbm(st[2]), _in_hbm(st[3])]
        in_specs += [SEM_SPEC, SEM_SPEC, HBM_SPEC, HBM_SPEC]
        out_shape += [pltpu.HBM(st[2].shape, st[2].dtype), pltpu.HBM(st[3].shape, st[3].dtype)]
        aliases[4 * i + 2] = 2 * i
        aliases[4 * i + 3] = 2 * i + 1
    res = _split_call(body, name, ins + [after], in_specs + [ANY_SPEC], out_shape, [HBM_SPEC] * (2 * n), aliases)
    return [(res[2 * i], res[2 * i + 1]) for i in range(n)]


def _chip_sum(parts, from_sibling, my_c, tile, name):
    _, _, r, c = parts.shape
    tr, tc = tile

    def body(c_ref, p_ref, s_ref, o_ref):
        o_ref[...] = (p_ref[...].astype(F32) + s_ref[...].astype(F32)).astype(o_ref.dtype)

    blk = pl.BlockSpec((4, tr, tc), lambda i, j, c_ref: (0, i, j))
    return pl.pallas_call(
        body, name=name, out_shape=_sds((4, r, c), parts.dtype),
        grid_spec=pltpu.PrefetchScalarGridSpec(
            num_scalar_prefetch=1, grid=(r // tr, c // tc),
            in_specs=[pl.BlockSpec((None, 4, tr, tc), lambda i, j, c_ref: (c_ref[0], 0, i, j)), blk], out_specs=blk),
        compiler_params=_params(("parallel", "parallel")),
    )(my_c, parts, from_sibling)


class _Side:
    def __init__(self, scalar, ins, outs, fn):
        self.scalar, self.ins, self.outs, self.fn = scalar, ins, outs, fn


def _chip_sum_side(parts, from_sibling, my_c, steps):
    _, _, r, c = parts.shape
    rows = r // steps
    assert rows * steps == r and rows % 16 == 0
    return _Side(
        my_c,
        [(parts, (None, 4, rows, c), lambda t, s: (s[0], 0, t, 0)), (from_sibling, (4, rows, c), lambda t, s: (0, t, 0))],
        [((4, r, c), parts.dtype, (4, rows, c), lambda t, s: (0, t, 0))],
        lambda p, q: [(p.astype(F32) + q.astype(F32)).astype(parts.dtype)])


def _to_chips_start(sums, name):
    n = len(sums)
    lands = [lax.empty((3,) + s.shape[1:], s.dtype) for s in sums]

    def body(*refs):
        src, land = refs[:n], refs[n:2 * n]
        outs = refs[2 * n:]
        for i in range(n):
            for k, rel in enumerate(OTHER_CHIPS):
                to = _peer(rel)
                _rcopy(src[i].at[_chip(to)], land[i].at[k], outs[4 * i].at[k], outs[4 * i + 1].at[k], to).start()

    out_shape, out_specs, aliases = [], [], {}
    for i, s in enumerate(sums):
        out_shape += [pltpu.SemaphoreType.DMA((3,)), pltpu.SemaphoreType.DMA((3,)), pltpu.HBM(s.shape, s.dtype),
                      pltpu.HBM((3,) + s.shape[1:], s.dtype)]
        out_specs += [SEM_SPEC, SEM_SPEC, HBM_SPEC, HBM_SPEC]
        aliases[i] = 4 * i + 2
        aliases[n + i] = 4 * i + 3
    res = _split_call(body, name, [_in_hbm(s) for s in sums] + [_in_hbm(l) for l in lands], [HBM_SPEC] * (2 * n),
                      out_shape, out_specs, aliases)
    return [tuple(res[4 * i:4 * i + 4]) for i in range(n)], res[-1]


def _to_chips_finish(started, after, name):
    n = len(started)

    def body(*refs):
        for i in range(n):
            send, recv, src, land = refs[4 * i:4 * i + 4]
            for k, rel in enumerate(OTHER_CHIPS):
                cp = _rcopy(src.at[0], land.at[k], send.at[k], recv.at[k], _peer(rel))
                cp.wait_send()
                cp.wait_recv()

    ins, in_specs, out_shape, aliases = [], [], [], {}
    for i, st in enumerate(started):
        ins += [st[0], st[1], _in_hbm(st[2]), _in_hbm(st[3])]
        in_specs += [SEM_SPEC, SEM_SPEC, HBM_SPEC, HBM_SPEC]
        out_shape += [pltpu.HBM(st[2].shape, st[2].dtype), pltpu.HBM(st[3].shape, st[3].dtype)]
        aliases[4 * i + 2] = 2 * i
        aliases[4 * i + 3] = 2 * i + 1
    res = _split_call(body, name, ins + [after], in_specs + [ANY_SPEC], out_shape, [HBM_SPEC] * (2 * n), aliases)
    return [(res[2 * i], res[2 * i + 1]) for i in range(n)]


def _reduce_small(v, name, deps=()):
    _, r, c = v.shape

    def body(v_ref, o_ref, land, send_sems, recv_sems):
        me = _slot(_me())
        copies = []
        for j in range(1, NDEV):
            to = _peer(j)
            cp = _rcopy(v_ref.at[_slot(to)], land.at[me], send_sems.at[j - 1], recv_sems.at[j - 1], to)
            cp.start()
            copies.append(cp)
        land[me] = v_ref[me]
        for cp in copies:
            cp.wait()
        acc = land[0]
        for k in range(1, NDEV):
            acc = acc + land[k]
        o_ref[...] = acc

    vm = pl.BlockSpec(memory_space=pltpu.VMEM)
    body, dep_ins, dep_specs = _after(body, 1, deps)
    return pl.pallas_call(
        body, name=name, out_shape=_sds((r, c), F32), in_specs=[vm] + dep_specs, out_specs=vm,
        scratch_shapes=[pltpu.VMEM((NDEV, r, c), F32), pltpu.SemaphoreType.DMA((NDEV - 1,)),
                        pltpu.SemaphoreType.DMA((NDEV - 1,))],
        compiler_params=_params(),
    )(v, *dep_ins)


def _matmul(a, b, *, mode, tm, tn, tk, name, out_dtype=F32, epi=None, extra=None, b_slots=False, out_slots=False,
            deps=(), col_block=0, into=None, n_total=None, side=None):
    b_pair = b if isinstance(b, tuple) else None
    if b_pair:
        assert mode == "nt" and tk == len(b) * b[0].shape[1] == a.shape[1] and not b_slots
        b = b[0]
    slot_w = b.shape[-1] if b_slots else None
    if mode == "nn":
        M, K = a.shape
        N = NDEV * slot_w if b_slots else b.shape[1]
    elif mode == "tn":
        K, M = a.shape
        N = b.shape[1]
    else:
        M, K = a.shape
        N = b.shape[-2]
        if b_slots:
            assert K == NDEV * slot_w and tk % slot_w == 0
    if mode == "nn" and b_slots:
        assert tn == slot_w
    if out_slots == "cols":
        assert tn * NDEV == N
    if out_slots == "rows":
        assert (M // NDEV) % tm == 0
    assert M % tm == 0 and N % tn == 0 and K % tk == 0, (name, M, N, K, tm, tn, tk)
    nk = K // tk
    dims = {"nn": ((1,), (0,)), "tn": ((0,), (0,)), "nt": ((1,), (1,))}[mode]

    if mode == "tn":
        a_spec = pl.BlockSpec((tk, tm), lambda i, j, k: (k, i))
    else:
        a_spec = pl.BlockSpec((tm, tk), lambda i, j, k: (i, k))
    if b_pair:
        b_spec = pl.BlockSpec((tn, tk // len(b_pair)), lambda i, j, k: (j, 0))
    elif mode == "nt":
        b_spec = (pl.BlockSpec((tk // slot_w, tn, slot_w), lambda i, j, k: (k, j, 0)) if b_slots
                  else pl.BlockSpec((tn, tk), lambda i, j, k: (j, k)))
    else:
        b_spec = (pl.BlockSpec((None, tk, tn), lambda i, j, k: (j, k, 0)) if b_slots
                  else pl.BlockSpec((tk, tn), lambda i, j, k: (k, j)))
    tile = pl.BlockSpec((tm, tn), lambda i, j, k: (i, j + col_block))
    if out_slots == "cols":
        out_spec = pl.BlockSpec((None, None, tm, tn), lambda i, j, k: (j % 2, j // 2, i, 0))
        out_shape = _sds((2, 4, M, tn), out_dtype)
    elif out_slots == "rows":
        per = M // NDEV // tm
        out_spec = pl.BlockSpec((None, None, tm, tn), lambda i, j, k: ((i // per) % 2, (i // per) // 2, i % per, j))
        out_shape = _sds((2, 4, M // NDEV, N), out_dtype)
    else:
        out_spec, out_shape = tile, _sds((M, n_total or N), out_dtype)
    ins, in_specs = [a, b], [a_spec, b_spec]
    if b_pair:
        ins += list(b_pair[1:])
        in_specs += [b_spec] * (len(b_pair) - 1)
    n_b = len(ins) - 1
    if epi in ("add", "dz"):
        ins.append(extra)
        in_specs.append(tile)
    aliases = {}
    if into is not None:
        aliases[len(ins)] = 0
        ins.append(into)
        in_specs.append(ANY_SPEC)
    if epi == "relu2":
        out_specs, out_shapes = [tile, tile], [_sds((M, N), F32), _sds((M, N), BF16)]
    else:
        out_specs, out_shapes = out_spec, out_shape
    n_in = len(ins)

    def body(*refs):
        outs = refs[n_in:-1] if nk > 1 else refs[n_in:]
        extra_ref = refs[1 + n_b]

        def finish(p):
            if epi is None:
                outs[0][...] = p.astype(out_dtype)
            elif epi == "add":
                outs[0][...] = (p + extra_ref[...]).astype(out_dtype)
            elif epi == "relu2":
                outs[0][...] = p
                rz = jnp.maximum(p, 0.0)
                outs[1][...] = (rz * rz).astype(BF16)
            else:
                outs[0][...] = (p * (2.0 * jnp.maximum(extra_ref[...], 0.0))).astype(out_dtype)

        def product():
            av = refs[0][...].astype(BF16)
            if b_pair:
                w = tk // len(b_pair)
                return sum(lax.dot_general(av[:, s * w:(s + 1) * w], refs[1 + s][...], (dims, ((), ())),
                                           preferred_element_type=F32) for s in range(len(b_pair)))
            if mode == "nt" and b_slots:
                return sum(lax.dot_general(av[:, s * slot_w:(s + 1) * slot_w], refs[1][s], (dims, ((), ())),
                                           preferred_element_type=F32) for s in range(tk // slot_w))
            return lax.dot_general(av, refs[1][...].astype(BF16), (dims, ((), ())), preferred_element_type=F32)

        if nk == 1:
            finish(product())
            return
        acc = refs[-1]
        k = pl.program_id(2)

        @pl.when(k == 0)
        def _():
            acc[...] = jnp.zeros_like(acc)

        acc[...] += product()

        @pl.when(k == nk - 1)
        def _():
            finish(acc[...])

    grid = (M // tm, N // tn, nk)
    scratch = [pltpu.VMEM((tm, tn), F32)] if nk > 1 else []
    if side is None:
        body, dep_ins, dep_specs = _after(body, n_in, deps)
        return pl.pallas_call(
            body, name=name, grid=grid,
            in_specs=in_specs + dep_specs, out_specs=out_specs, out_shape=out_shapes, input_output_aliases=aliases,
            scratch_shapes=scratch, compiler_params=_params(("parallel", "parallel", "arbitrary")),
        )(*ins, *dep_ins)

    deps = [d for d in deps if d is not None]
    step = lambda i, j, k: (i * grid[1] + j) * grid[2] + k
    host = lambda spec: (spec if spec.block_shape is None else
                         pl.BlockSpec(spec.block_shape, lambda i, j, k, s, f=spec.index_map: f(i, j, k)))
    cut = lambda blk, f: pl.BlockSpec(blk, lambda i, j, k, s: f(step(i, j, k), s))
    host_out_specs = list(out_specs) if isinstance(out_specs, list) else [out_specs]
    host_out_shapes = list(out_shapes) if isinstance(out_shapes, list) else [out_shapes]
    n_dep, n_si, n_ho, n_so = len(deps), len(side.ins), len(host_out_specs), len(side.outs)

    def with_side(*refs):
        rest = refs[1:]
        side_in = rest[n_in + n_dep:n_in + n_dep + n_si]
        outs_all = rest[n_in + n_dep + n_si:]
        body(*rest[:n_in], *outs_all[:n_ho], *outs_all[n_ho + n_so:])
        for o_ref, val in zip(outs_all[n_ho:n_ho + n_so], side.fn(*[r[...] for r in side_in])):
            o_ref[...] = val

    res = pl.pallas_call(
        with_side, name=name, input_output_aliases={k + 1: v for k, v in aliases.items()},
        out_shape=host_out_shapes + [_sds(shape, dt) for shape, dt, _, _ in side.outs],
        grid_spec=pltpu.PrefetchScalarGridSpec(
            num_scalar_prefetch=1, grid=grid,
            in_specs=[host(s) for s in in_specs] + [ANY_SPEC] * n_dep + [cut(blk, f) for _, blk, f in side.ins],
            out_specs=[host(s) for s in host_out_specs] + [cut(blk, f) for _, _, blk, f in side.outs],
            scratch_shapes=scratch),
        compiler_params=_params(("parallel", "parallel", "arbitrary")),
    )(side.scalar, *ins, *deps, *[arr for arr, _, _ in side.ins])
    host_res = res[0] if n_ho == 1 else tuple(res[:n_ho])
    return host_res, list(res[n_ho:])


ROWS = 352


def _rmsnorm_fwd(h, w, name, deps=()):
    def body(h_ref, w_ref, u_ref):
        x = h_ref[...]
        rstd = lax.rsqrt(jnp.mean(x * x, axis=-1, keepdims=True) + EPS)
        u_ref[...] = (x * rstd * w_ref[...]).astype(BF16)

    row = pl.BlockSpec((ROWS, D), lambda i: (i, 0))
    body, dep_ins, dep_specs = _after(body, 2, deps)
    return pl.pallas_call(
        body, name=name, grid=(LP // ROWS,), in_specs=[row, pl.BlockSpec((1, D), lambda i: (0, 0))] + dep_specs,
        out_specs=row, out_shape=_sds((LP, D), BF16), compiler_params=_params(("parallel",)),
    )(h, w, *dep_ins)


TOKEN_ROWS = 512


def _rmsnorm_bwd_input(h, w, du, dres, name, deps=()):
    def math(h_ref, w_ref, du_ref, dres_ref):
        x = h_ref[...]
        rstd = lax.rsqrt(jnp.mean(x * x, axis=-1, keepdims=True) + EPS)
        xhat = x * rstd
        dy = du_ref[...]
        dxh = dy * w_ref[...]
        dh = dres_ref[...] + rstd * (dxh - xhat * jnp.mean(dxh * xhat, axis=-1, keepdims=True))
        return dh, jnp.sum(dy * xhat, axis=0, keepdims=True)

    def body(h_ref, w_ref, du_ref, dres_ref, hh_ref, duh_ref, dresh_ref, dx_ref, dhead_ref, gw_ref):
        dx_ref[...], part = math(h_ref, w_ref, du_ref, dres_ref)

        @pl.when(pl.program_id(0) == 0)
        def _():
            dhead_ref[...], head = math(hh_ref, w_ref, duh_ref, dresh_ref)
            gw_ref[...] = part + head

        @pl.when(pl.program_id(0) > 0)
        def _():
            gw_ref[...] += part

    rows = pl.BlockSpec((pl.Element(TOKEN_ROWS), pl.Element(D)),
                        lambda i: (pl.multiple_of(ROW_X + TOKEN_ROWS * i, 8), 0))
    head = pl.BlockSpec((ROW_X, D), lambda i: (0, 0))
    vec = pl.BlockSpec((1, D), lambda i: (0, 0))
    body, dep_ins, dep_specs = _after(body, 7, deps)
    return pl.pallas_call(
        body, name=name, grid=(SEQ // TOKEN_ROWS,),
        in_specs=[rows, vec, rows, rows, head, head, head] + dep_specs,
        out_specs=[pl.BlockSpec((TOKEN_ROWS, D), lambda i: (i, 0)), head, vec],
        out_shape=[_sds((SEQ, D), F32), _sds((ROW_X, D), F32), _sds((1, D), F32)],
        compiler_params=_params(("arbitrary",)),
    )(h, w, du, dres, h, du, dres, *dep_ins)


def _rmsnorm_bwd(h, w, du, dres, name, deps=()):
    def body(h_ref, w_ref, du_ref, dres_ref, dh_ref, dhb_ref, gw_ref):
        x = h_ref[...]
        rstd = lax.rsqrt(jnp.mean(x * x, axis=-1, keepdims=True) + EPS)
        xhat = x * rstd
        dy = du_ref[...]
        dxh = dy * w_ref[...]
        dh = dres_ref[...] + rstd * (dxh - xhat * jnp.mean(dxh * xhat, axis=-1, keepdims=True))
        dh_ref[...] = dh
        dhb_ref[...] = dh.astype(BF16)
        part = jnp.sum(dy * xhat, axis=0, keepdims=True)

        @pl.when(pl.program_id(0) == 0)
        def _():
            gw_ref[...] = part

        @pl.when(pl.program_id(0) > 0)
        def _():
            gw_ref[...] += part

    row = pl.BlockSpec((ROWS, D), lambda i: (i, 0))
    vec = pl.BlockSpec((1, D), lambda i: (0, 0))
    body, dep_ins, dep_specs = _after(body, 4, deps)
    return pl.pallas_call(
        body, name=name, grid=(LP // ROWS,), in_specs=[row, vec, row, row] + dep_specs, out_specs=[row, row, vec],
        out_shape=[_sds((LP, D), F32), _sds((LP, D), BF16), _sds((1, D), F32)],
        compiler_params=_params(("arbitrary",)),
    )(h, w, du, dres, *dep_ins)


def _loss_head(h2, wf, target):
    def body(h_ref, w_ref, t_ref, dh_ref, dhb_ref, sq_ref, gw_ref):
        i = pl.program_id(0)

        @pl.when(i == 0)
        def _():
            sq_ref[...] = jnp.zeros_like(sq_ref)
            gw_ref[...] = jnp.zeros_like(gw_ref)

        def rows(t, live):
            x = h_ref[...]
            rstd = lax.rsqrt(jnp.mean(x * x, axis=-1, keepdims=True) + EPS)
            xhat = x * rstd
            w = w_ref[...]
            err = xhat * w - t
            if live is not None:
                err = jnp.where(live, err, 0.0)
            sq_ref[...] += jnp.sum(err * err, axis=0, keepdims=True)
            dy = err * (1.0 / D)
            gw_ref[...] += jnp.sum(dy * xhat, axis=0, keepdims=True)
            dxh = dy * w
            dh = rstd * (dxh - xhat * jnp.mean(dxh * xhat, axis=-1, keepdims=True))
            dh_ref[...] = dh
            dhb_ref[...] = dh.astype(BF16)

        @pl.when(i == 0)
        def _():
            rid = lax.broadcasted_iota(jnp.int32, (ROWS, D), 0)
            rows(pltpu.roll(t_ref[...], ROW_X, 0), rid >= ROW_X)

        @pl.when(i > 0)
        def _():
            rows(t_ref[...], None)

    row = pl.BlockSpec((ROWS, D), lambda i: (i, 0))
    vec = pl.BlockSpec((1, D), lambda i: (0, 0))
    tgt = pl.BlockSpec((pl.Element(ROWS), pl.Element(D)),
                       lambda i: (pl.multiple_of(jnp.maximum(ROWS * i - ROW_X, 0), 8), 0))
    return pl.pallas_call(
        body, name="loss_head", grid=(LP // ROWS,),
        in_specs=[row, vec, tgt],
        out_specs=[row, row, vec, vec],
        out_shape=[_sds((LP, D), F32), _sds((LP, D), BF16), _sds((1, D), F32), _sds((1, D), F32)],
        compiler_params=_params(("arbitrary",)),
    )(h2, wf, target)


def _dot(a, b, dims):
    return lax.dot_general(a, b, (dims, ((), ())), preferred_element_type=F32)


NN, TN, NT = ((1,), (0,)), ((0,), (0,)), ((1,), (1,))


def _tri_sum(t, x):
    hi = x.astype(BF16)
    r1 = x - hi.astype(F32)
    mid = r1.astype(BF16)
    lo = (r1 - mid.astype(F32)).astype(BF16)
    return _dot(t, hi, NN) + _dot(t, mid, NN) + _dot(t, lo, NN)


def _gla_gates(glr_ref, gw2_ref, gb_ref, rows, row0):
    g_raw = _dot(glr_ref[rows, :].astype(BF16), gw2_ref[...], NN) + gb_ref[...]
    logsig = jnp.minimum(g_raw, 0.0) - jnp.log(1.0 + jnp.exp(-jnp.abs(g_raw)))
    rid = row0 + lax.broadcasted_iota(jnp.int32, g_raw.shape, 0)
    live = rid >= PAD
    return g_raw, jnp.where(live, logsig / TAU, 0.0), live


def _tri_masks():
    r = lax.broadcasted_iota(jnp.int32, (CHUNK, CHUNK), 0)
    c = lax.broadcasted_iota(jnp.int32, (CHUNK, CHUNK), 1)
    return r >= c


def _gla_specs(rev):
    n = NCH // CPS
    R = CPS * CHUNK
    st = (lambda s: n - 1 - s) if rev else (lambda s: s)
    return R, n, st, [
        pl.BlockSpec((R, KW), lambda s: (st(s), 0)),
        pl.BlockSpec((R, KW), lambda s: (st(s), 1)),
        pl.BlockSpec((R, GW), lambda s: (st(s), 1)),
        pl.BlockSpec((R, GW), lambda s: (st(s), 2)),
        pl.BlockSpec((R, 128), lambda s: (st(s), GLR_BLK)),
    ]


def _gla_fwd(proj, gw2p, gate_b, gnw, deps=()):
    R, n, st, pspecs = _gla_specs(False)

    def body(q_ref, k_ref, v_ref, r_ref, glr_ref, gw2_ref, gb_ref, gnw_ref, og_ref, o_ref, st_ref, state):
        s = pl.program_id(0)

        @pl.when(s == 0)
        def _():
            state[...] = jnp.zeros_like(state)

        causal = _tri_masks()
        tri = causal.astype(BF16)
        for c in range(CPS):
            rows = slice(c * CHUNK, (c + 1) * CHUNK)
            _, logg, _ = _gla_gates(glr_ref, gw2_ref, gb_ref, rows, s * R + c * CHUNK)
            G = _tri_sum(tri, logg)
            g_last = G[CHUNK - 1:CHUNK, :]
            q_dec = (q_ref[rows, :] * (DK ** -0.5) * jnp.exp(G)).astype(BF16)
            kk = k_ref[rows, :]
            k_inv = (kk * jnp.exp(-G)).astype(BF16)
            k_end = (kk * jnp.exp(g_last - G)).astype(BF16)
            decay = jnp.exp(g_last)
            for h in range(H):
                lk = slice(h * DK, (h + 1) * DK)
                lv = slice(h * DV, (h + 1) * DV)
                v = v_ref[rows, lv].astype(BF16)
                S = state[h]
                st_ref[c, h] = S
                A = jnp.where(causal, _dot(q_dec[:, lk], k_inv[:, lk], NT), 0.0).astype(BF16)
                o = _dot(A, v, NN) + _dot(q_dec[:, lk], S.astype(BF16), NT)
                state[h] = decay[:, lk] * S + _dot(v, k_end[:, lk], TN)
                o_ref[rows, lv] = o
                on = o * lax.rsqrt(jnp.mean(o * o, axis=-1, keepdims=True) + EPS) * gnw_ref[...]
                rr = r_ref[rows, lv]
                og_ref[rows, lv] = (on * (rr * jax.nn.sigmoid(rr))).astype(BF16)

    full = lambda shape: pl.BlockSpec(shape, lambda s: (0,) * len(shape))
    body, dep_ins, dep_specs = _after(body, 8, deps)
    return pl.pallas_call(
        body, name="gla_fwd", grid=(n,),
        in_specs=pspecs + [full((128, KW)), full((1, KW)), full((1, DV))] + dep_specs,
        out_specs=[pl.BlockSpec((R, GW), lambda s: (s, 0)), pl.BlockSpec((R, GW), lambda s: (s, 0)),
                   pl.BlockSpec((CPS, H, DV, DK), lambda s: (s, 0, 0, 0))],
        out_shape=[_sds((LP, GW + PW), BF16), _sds((LP, GW), F32), _sds((NCH, H, DV, DK), F32)],
        scratch_shapes=[pltpu.VMEM((H, DV, DK), F32)],
        compiler_params=_params(("arbitrary",)),
    )(proj, proj, proj, proj, proj, gw2p, gate_b, gnw, *dep_ins)


def _gla_bwd(proj, dmixed, o_saved, st_saved, gw2p, gate_b, gnw, deps=()):
    R, n, st, pspecs = _gla_specs(True)

    def body(q_ref, k_ref, v_ref, r_ref, glr_ref, dog_ref, o_ref, st_ref, gw2_ref, gb_ref, gnw_ref,
             dqkvr_ref, dglr_ref, ggn_ref, ggb_ref, ggw_ref, gstate):
        s = pl.program_id(0)

        @pl.when(s == 0)
        def _():
            gstate[...] = jnp.zeros_like(gstate)
            ggn_ref[...] = jnp.zeros_like(ggn_ref)
            ggb_ref[...] = jnp.zeros_like(ggb_ref)
            ggw_ref[...] = jnp.zeros_like(ggw_ref)

        causal = _tri_masks()
        tri = causal.astype(BF16)
        tri_up = (lax.broadcasted_iota(jnp.int32, (CHUNK, CHUNK), 0)
                  <= lax.broadcasted_iota(jnp.int32, (CHUNK, CHUNK), 1)).astype(BF16)
        gnw = gnw_ref[...]
        for c in reversed(range(CPS)):
            rows = slice(c * CHUNK, (c + 1) * CHUNK)
            g_raw, logg, live = _gla_gates(glr_ref, gw2_ref, gb_ref, rows, (n - 1 - s) * R + c * CHUNK)
            G = _tri_sum(tri, logg)
            g_last = G[CHUNK - 1:CHUNK, :]
            e_g, e_gi, e_end = jnp.exp(G), jnp.exp(-G), jnp.exp(g_last - G)
            q_dec = q_ref[rows, :] * (DK ** -0.5) * e_g
            kk = k_ref[rows, :]
            k_inv, k_end = kk * e_gi, kk * e_end
            q_dec_b, k_inv_b, k_end_b = q_dec.astype(BF16), k_inv.astype(BF16), k_end.astype(BF16)
            decay = jnp.exp(g_last)
            d_g, d_gl = [], []
            for h in range(H):
                lk = slice(h * DK, (h + 1) * DK)
                lv = slice(h * DV, (h + 1) * DV)
                o = o_ref[rows, lv]
                rr = r_ref[rows, lv]
                dog = dog_ref[rows, lv]
                rstd = lax.rsqrt(jnp.mean(o * o, axis=-1, keepdims=True) + EPS)
                ohat = o * rstd
                sr = jax.nn.sigmoid(rr)
                don = dog * (rr * sr)
                dqkvr_ref[rows, 2 * KW + GW + h * DV:2 * KW + GW + (h + 1) * DV] = (
                    dog * (ohat * gnw) * (sr * (1.0 + rr * (1.0 - sr)))).astype(BF16)
                ggn_ref[...] += jnp.sum(don * ohat, axis=0, keepdims=True)
                dohat = don * gnw
                do = (rstd * (dohat - ohat * jnp.mean(dohat * ohat, axis=-1, keepdims=True))).astype(BF16)
                v = v_ref[rows, lv].astype(BF16)
                S = st_ref[c, h]
                gS = gstate[h]
                S_b, gS_b = S.astype(BF16), gS.astype(BF16)
                qd, ki, ke = q_dec_b[:, lk], k_inv_b[:, lk], k_end_b[:, lk]
                A = jnp.where(causal, _dot(qd, ki, NT), 0.0).astype(BF16)
                dA = jnp.where(causal, _dot(do, v, NT), 0.0).astype(BF16)
                dv = _dot(A, do, TN) + _dot(ke, gS_b, NT)
                dq_dec = _dot(dA, ki, NN) + _dot(do, S_b, NN)
                dk_inv = _dot(dA, qd, TN)
                dk_end = _dot(v, gS_b, NN)
                d_decay = jnp.sum(gS * S, axis=0, keepdims=True)
                gstate[h] = decay[:, lk] * gS + _dot(do, qd, TN)
                dqkvr_ref[rows, lk] = (dq_dec * e_g[:, lk] * (DK ** -0.5)).astype(BF16)
                dqkvr_ref[rows, KW + h * DK:KW + (h + 1) * DK] = (
                    dk_inv * e_gi[:, lk] + dk_end * e_end[:, lk]).astype(BF16)
                dqkvr_ref[rows, 2 * KW + h * DV:2 * KW + (h + 1) * DV] = dv.astype(BF16)
                ke_prod = dk_end * k_end[:, lk]
                d_g.append(dq_dec * q_dec[:, lk] - dk_inv * k_inv[:, lk] - ke_prod)
                d_gl.append(jnp.sum(ke_prod, axis=0, keepdims=True) + d_decay * decay[:, lk])
            dlogg = _tri_sum(tri_up, jnp.concatenate(d_g, axis=1)) + jnp.concatenate(d_gl, axis=1)
            dg_raw = jnp.where(live, dlogg * (1.0 / TAU) * jax.nn.sigmoid(-g_raw), 0.0)
            ggb_ref[...] += jnp.sum(dg_raw, axis=0, keepdims=True)
            dg_b = dg_raw.astype(BF16)
            ggw_ref[...] += _dot(glr_ref[rows, :].astype(BF16), dg_b, TN)
            dglr_ref[rows, :] = _dot(dg_b, gw2_ref[...], NT).astype(BF16)

    full = lambda shape: pl.BlockSpec(shape, lambda s: (0,) * len(shape))
    body, dep_ins, dep_specs = _after(body, 11, deps)
    return pl.pallas_call(
        body, name="gla_bwd", grid=(n,),
        in_specs=pspecs + [pl.BlockSpec((R, GW), lambda s: (st(s), 0)), pl.BlockSpec((R, GW), lambda s: (st(s), 0)),
                           pl.BlockSpec((CPS, H, DV, DK), lambda s: (st(s), 0, 0, 0)),
                           full((128, KW)), full((1, KW)), full((1, DV))] + dep_specs,
        out_specs=[pl.BlockSpec((R, 2 * KW + 2 * GW), lambda s: (st(s), 0)), pl.BlockSpec((R, 128), lambda s: (st(s), 0)),
                   full((1, DV)), full((1, KW)), full((128, KW))],
        out_shape=[_sds((LP, D_INP), BF16), _sds((LP, 128), BF16),
                   _sds((1, DV), F32), _sds((1, KW), F32), _sds((128, KW), F32)],
        scratch_shapes=[pltpu.VMEM((H, DV, DK), F32)],
        compiler_params=_params(("arbitrary",)),
    )(proj, proj, proj, proj, proj, dmixed, o_saved, st_saved, gw2p, gate_b, gnw, *dep_ins)


def _pool_pre(x, win, rid):
    s, step = x, 1
    while step < win:
        s = s + pltpu.roll(s, step, 0)
        step *= 2
    cnt = jnp.clip(rid - (PAD - 1), 1, win).astype(F32)
    live = rid >= PAD
    return jnp.where(live, s / cnt - x, 0.0), cnt, live


def _pool_fwd(proj, pool_w, pool_scale, mixed):
    def body(pu_ref, w_ref, sc_ref, _, o_ref):
        rid = lax.broadcasted_iota(jnp.int32, (LP, GC), 0)
        for g, win in enumerate(WINDOWS):
            @pl.when(pl.program_id(0) == g)
            def _():
                y, _, _ = _pool_pre(pu_ref[...], win, rid)
                o_ref[...] = (_dot(y.astype(BF16), w_ref[...], NN) * sc_ref[...]).astype(BF16)

    col = lambda base: pl.BlockSpec((LP, GC), lambda g: (0, base + g))
    return pl.pallas_call(
        body, name="pool_fwd", grid=(len(WINDOWS),),
        in_specs=[col(POOL_BLK), pl.BlockSpec((None, GC, GC), lambda g: (g, 0, 0)),
                  pl.BlockSpec((1, GC), lambda g: (0, g)), ANY_SPEC],
        out_specs=col(GW // GC), out_shape=_sds(mixed.shape, BF16), input_output_aliases={3: 0},
        compiler_params=_params(("parallel",)),
    )(proj, pool_w, pool_scale, mixed)


def _pool_bwd(proj, dmixed, pool_w, pool_scale, dproj):
    def body(pu_ref, do_ref, w_ref, sc_ref, _, dpu_ref, dw_ref, dsc_ref):
        rid = lax.broadcasted_iota(jnp.int32, (LP, GC), 0)
        for g, win in enumerate(WINDOWS):
            @pl.when(pl.program_id(0) == g)
            def _():
                y, cnt, live = _pool_pre(pu_ref[...], win, rid)
                y_b = y.astype(BF16)
                w = w_ref[...]
                do = do_ref[...]
                dsc_ref[...] = jnp.sum(do * _dot(y_b, w, NN), axis=0, keepdims=True)
                dyw = (do * sc_ref[...]).astype(BF16)
                dw_ref[...] = _dot(y_b, dyw, TN)
                dy = jnp.where(live, _dot(dyw, w, NT), 0.0)
                s, step = dy / cnt, 1
                while step < win:
                    s = s + pltpu.roll(s, LP - step, 0)
                    step *= 2
                dpu_ref[...] = (s - dy).astype(BF16)

    col = lambda base: pl.BlockSpec((LP, GC), lambda g: (0, base + g))
    mat = pl.BlockSpec((None, GC, GC), lambda g: (g, 0, 0))
    vec = pl.BlockSpec((1, GC), lambda g: (0, g))
    return pl.pallas_call(
        body, name="pool_bwd", grid=(len(WINDOWS),),
        in_specs=[col(POOL_BLK), col(GW // GC), mat, vec, ANY_SPEC], out_specs=[col(POOL_BLK), mat, vec],
        out_shape=[_sds(dproj.shape, BF16), _sds((4, GC, GC), F32), _sds((1, PW), F32)],
        input_output_aliases={4: 0}, compiler_params=_params(("parallel",)),
    )(proj, dmixed, pool_w, pool_scale, dproj)


def _adamw_math(w, g, m, v):
    m = B1 * m + (1.0 - B1) * g
    v = B2 * v + (1.0 - B2) * (g * g)
    m_hat = m * (1.0 / (1.0 - B1 ** STEP))
    v_hat = v * (1.0 / (1.0 - B2 ** STEP))
    return -LR * (m_hat / (jnp.sqrt(v_hat) + AEPS) + WD * w), m, v


def _adamw_landed(sums, landed, my_chip, w, m, v, rows, name, cols=None):
    _, r, c = w.shape

    def body(chip_ref, s_ref, l_ref, w_ref, m_ref, v_ref, g_ref, d_ref, mo_ref, vo_ref):
        g = s_ref[...].astype(F32)
        for k in range(3):
            g = g + l_ref[k].astype(F32)
        g_ref[...] = g
        d_ref[...], mo_ref[...], vo_ref[...] = _adamw_math(w_ref[...], g, m_ref[...], v_ref[...])

    cols = cols or c
    blk = pl.BlockSpec((None, rows, cols), lambda i, j, chip_ref: (0, i, j))
    return pl.pallas_call(
        body, name=name, out_shape=[_sds((1, r, c), F32)] * 4,
        grid_spec=pltpu.PrefetchScalarGridSpec(
            num_scalar_prefetch=1, grid=(r // rows, c // cols),
            in_specs=[pl.BlockSpec((None, rows, cols), lambda i, j, chip_ref: (chip_ref[0], i, j)),
                      pl.BlockSpec((3, rows, cols), lambda i, j, chip_ref: (0, i, j)), blk, blk, blk],
            out_specs=[blk] * 4),
        compiler_params=_params(("parallel", "parallel")),
    )(my_chip, sums, landed, w, m, v)


def _adamw_side(sums, landed, my_chip, w, m, v, steps):
    _, r, c = w.shape
    rows = r // steps
    assert rows * steps == r and rows % 16 == 0

    def fn(s, l, w_, m_, v_):
        g = s.astype(F32)
        for k in range(3):
            g = g + l[k].astype(F32)
        return [g, *_adamw_math(w_, g, m_, v_)]

    one = lambda t, s: (0, t, 0)
    return _Side(
        my_chip,
        [(sums, (None, rows, c), lambda t, s: (s[0], t, 0)), (landed, (3, rows, c), one)]
        + [(a, (None, rows, c), one) for a in (w, m, v)],
        [((1, r, c), F32, (None, rows, c), one)] * 4, fn)


SMALL_PLACES = (
    ("norm1_w", (1, D), ((0, 0, 1024), (1, 0, 1024))),
    ("norm2_w", (1, D), ((2, 0, 1024), (3, 0, 1024))),
    ("final_norm_w", (1, D), ((4, 0, 1024), (5, 0, 1024))),
    ("pool_scale", (1, PW), ((6, 0, 1024),)),
    ("gate_b", (1, KW), ((7, 0, KW),)),
    ("gla_norm_w", (1, DV), ((7, KW, DV),)),
    ("meta_tokens", (4, 1024), None),
    ("gate_w2", (1, 1024), ((12, 0, 1024),)),
)


def _adamw_small(g, w, m, v):
    n = len(SMALL_PLACES)

    def body(g_ref, w_ref, m_ref, v_ref, *refs):
        outs, buf = refs[:-1], refs[-1]
        gv = g_ref[...]
        for a, val in enumerate((gv, *_adamw_math(w_ref[...], gv, m_ref[...], v_ref[...]))):
            buf[a] = val
            for j, (_, _, pieces) in enumerate(SMALL_PLACES):
                o_ref = outs[a * n + j]
                if pieces is None:
                    o_ref[...] = buf[a, pl.ds(8, 4), :]
                else:
                    o_ref[...] = jnp.concatenate([buf[a, pl.ds(r, 1), pl.ds(l, k)] for r, l, k in pieces], axis=1)

    return pl.pallas_call(
        body, name="adamw_small", out_shape=[_sds(shape, F32) for _ in range(4) for _, shape, _ in SMALL_PLACES],
        scratch_shapes=[pltpu.VMEM((4,) + w.shape, F32)],
    )(g, w, m, v)


SMALL_REPL = (("norm1_w", D), ("norm2_w", D), ("final_norm_w", D), ("pool_scale", PW), ("gate_b", KW),
              ("gla_norm_w", DV))


def _pack_rows(vecs, rows):
    flat = jnp.concatenate([jnp.ravel(v) for v in vecs])
    return jnp.pad(flat, (0, rows * 1024 - flat.shape[0])).reshape(rows, 1024)


def kernel(x, meta_tokens, norm1_w, w_in, gate_w2, gate_b, gla_norm_w, pool_w, pool_scale, w_out, norm2_w, mlp_w1, mlp_w2, final_norm_w, loss_target, m_meta_tokens, m_norm1_w, m_w_in, m_gate_w2, m_gate_b, m_gla_norm_w, m_pool_w, m_pool_scale, m_w_out, m_norm2_w, m_mlp_w1, m_mlp_w2, m_final_norm_w, v_meta_tokens, v_norm1_w, v_w_in, v_gate_w2, v_gate_b, v_gla_norm_w, v_pool_w, v_pool_scale, v_w_out, v_norm2_w, v_mlp_w1, v_mlp_w2, v_final_norm_w):
    W = dict(meta_tokens=meta_tokens, norm1_w=norm1_w, w_in=w_in, gate_w2=gate_w2, gate_b=gate_b,
             gla_norm_w=gla_norm_w, pool_w=pool_w, pool_scale=pool_scale, w_out=w_out, norm2_w=norm2_w,
             mlp_w1=mlp_w1, mlp_w2=mlp_w2, final_norm_w=final_norm_w)
    Mo = dict(meta_tokens=m_meta_tokens, norm1_w=m_norm1_w, w_in=m_w_in, gate_w2=m_gate_w2, gate_b=m_gate_b,
              gla_norm_w=m_gla_norm_w, pool_w=m_pool_w, pool_scale=m_pool_scale, w_out=m_w_out, norm2_w=m_norm2_w,
              mlp_w1=m_mlp_w1, mlp_w2=m_mlp_w2, final_norm_w=m_final_norm_w)
    Vo = dict(meta_tokens=v_meta_tokens, norm1_w=v_norm1_w, w_in=v_w_in, gate_w2=v_gate_w2, gate_b=v_gate_b,
              gla_norm_w=v_gla_norm_w, pool_w=v_pool_w, pool_scale=v_pool_scale, w_out=v_w_out, norm2_w=v_norm2_w,
              mlp_w1=v_mlp_w1, mlp_w2=v_mlp_w2, final_norm_w=v_final_norm_w)

    ex = _Exchange(dict(small=_pack_rows([meta_tokens, gate_w2[0]], 8), w_in=w_in[0].T.astype(BF16)),
                   dict(w_out=w_out[0], pool_w=pool_w[0].reshape(4 * 32, GC), mlp_w1=mlp_w1[0],
                        **{"mlp_w2_%d" % q: mlp_w2[0][:, q * W2_COLS:(q + 1) * W2_COLS] for q in range(W2_PIECES)}))
    tr = lambda a: a[0].T[None]
    win_t, m_win_t, v_win_t = tr(w_in), tr(m_w_in), tr(v_w_in)
    ex.early = [win_t, m_win_t, v_win_t]
    ex.shards = {k: (W[k], Mo[k], Vo[k]) for k in ("w_out", "mlp_w1", "mlp_w2")}
    step = _layer_step(x[0], loss_target[0], ex, norm1_w, gate_b, gla_norm_w, pool_scale, norm2_w,
                       final_norm_w.reshape(1, D))
    grad_x = step["dx"][None]

    last = ex.update("mix", step["dx"])
    out = dict(ex.done)

    loss_part = 0.5 * jnp.sum(step["sq"]) / D
    to_all = _pack_rows([step[k] for k, _ in SMALL_REPL] + [loss_part], 8)
    cols = lambda g: g.reshape(g.shape[0], NDEV, -1).transpose(1, 0, 2).reshape(NDEV, -1, 1024)
    packed = jnp.concatenate([jnp.broadcast_to(to_all, (NDEV, 8, 1024)), cols(step["dhead"][PAD:]),
                              cols(step["gate_w2"][:RANK]), jnp.zeros((NDEV, 3, 1024), F32)], axis=1)
    red = _reduce_small(packed, "reduce_small", deps=[last])
    loss = red[7, 768]

    done = ex.grad_finish("w_in", red)
    poolw3 = lambda a: a.reshape(1, 4 * 32, GC)
    res = _adamw_landed(*done["pool_w"], ex.my_chip, poolw3(pool_w), poolw3(m_pool_w), poolw3(v_pool_w),
                        SHARD_ROWS["pool_w"], "adamw_pool_w")
    out["pool_w"] = [a.reshape(pool_w.shape) for a in res]
    res = _adamw_landed(*done["w_in"], ex.my_chip, win_t, m_win_t, v_win_t, D_IN // NDEV, "adamw_w_in", cols=256)
    out["w_in"] = [a[0].T[None] for a in res]

    def small_pack(P):
        return jnp.concatenate([_pack_rows([P[k] for k, _ in SMALL_REPL], 8),
                                _pack_rows([P["meta_tokens"], P["gate_w2"]], 8)], axis=0)

    res_small = _adamw_small(red.at[7, 768].set(0.0), small_pack(W), small_pack(Mo), small_pack(Vo))
    for j, (k, _, _) in enumerate(SMALL_PLACES):
        out[k] = [res_small[a * len(SMALL_PLACES) + j].reshape(W[k].shape) for a in range(4)]

    order = ["meta_tokens", "norm1_w", "w_in", "gate_w2", "gate_b", "gla_norm_w", "pool_w", "pool_scale", "w_out",
             "norm2_w", "mlp_w1", "mlp_w2", "final_norm_w"]
    return (loss, grad_x, *[out[k][0] for k in order], *[out[k][1] for k in order],
            *[out[k][2] for k in order], *[out[k][3] for k in order])


SHARD_ROWS = dict(w_out=256, mlp_w1=512, mlp_w2=256, pool_w=128)
SLOT_ROWS = D_IN // NDEV


def _w_in_pieces(s):
    lo, hi, out = s * SLOT_ROWS, (s + 1) * SLOT_ROWS, []
    for a, b, shift in ((0, C_GLR, 0), (C_GLR, C_GLR + RANK, PW), (C_GLR + RANK, D_IN, -RANK)):
        a, b = max(a, lo), min(b, hi)
        if a < b:
            out.append((a - lo, a + shift, b - a))
    return out


def _w_in_to_layer_order(gathered):
    def body(g_ref, o_ref):
        for s in range(NDEV):
            @pl.when(pl.program_id(0) == s)
            def _():
                for src, dst, n in _w_in_pieces(s):
                    o_ref[pl.ds(dst, n), :] = g_ref[pl.ds(src, n), :]

        @pl.when(pl.program_id(0) == 0)
        def _():
            o_ref[pl.ds(D_IN, D_INP - D_IN), :] = jnp.zeros((D_INP - D_IN, D), BF16)

    return pl.pallas_call(
        body, name="w_in_rows", grid=(NDEV,), out_shape=_sds((D_INP, D), BF16),
        in_specs=[pl.BlockSpec((None, SLOT_ROWS, D), lambda s: (s, 0, 0))],
        out_specs=pl.BlockSpec((D_INP, D), lambda s: (0, 0)), compiler_params=_params(("arbitrary",)),
    )(gathered)


def _w_in_grad_to_parts(g):
    def body(g_ref, o_ref):
        for s in range(NDEV):
            @pl.when(pl.program_id(0) == s)
            def _():
                for dst, src, n in _w_in_pieces(s):
                    o_ref[pl.ds(dst, n), :] = g_ref[pl.ds(src, n), :]

    return pl.pallas_call(
        body, name="w_in_grad_rows", grid=(NDEV,), out_shape=_sds((2, 4, SLOT_ROWS, D), BF16),
        in_specs=[pl.BlockSpec((D_INP, D), lambda s: (0, 0))],
        out_specs=pl.BlockSpec((None, None, SLOT_ROWS, D), lambda s: (s % 2, s // 2, 0, 0)),
        compiler_params=_params(("arbitrary",)),
    )(g)
C_GLR = 2 * KW + 2 * GW
W2_PIECES = 2
W2_COLS = D // W2_PIECES
GATHER_GROUPS = dict(small=("small",), w_in=("w_in",), mix=("w_out", "pool_w"), up=("mlp_w1",),
                     **{"down_%d" % q: ("mlp_w2_%d" % q,) for q in range(W2_PIECES)})
GRAD_GROUPS = dict(down=("mlp_w2",), up=("mlp_w1",), mix=("w_out",), w_in=("pool_w", "w_in"))


class _Exchange:
    def __init__(self, first, rest):
        head, token = _gather_start(list(first.values()), "gather_start_first", first["small"])
        token, later = lax.optimization_barrier((token, list(rest.values())))
        tail, self.started = _gather_start([v.astype(BF16) for v in later], "gather_start_rest", token)
        self.state = dict(zip(list(first) + list(rest), head + tail))
        self.my_c = lax.axis_index("c").astype(jnp.int32).reshape(1)
        self.my_chip = (2 * lax.axis_index("x") + lax.axis_index("y")).astype(jnp.int32).reshape(1)
        self.sibling, self.chips, self.done = {}, {}, {}

    def forward(self, group, after):
        ks = GATHER_GROUPS[group]
        fwd, token = _gather_forward([self.state[k] for k in ks], after, "gather_forward_" + group)
        self.state.update(zip(ks, fwd))
        return token

    def weights(self, group, after):
        ks = GATHER_GROUPS[group]
        g = dict(zip(ks, _gather_finish([self.state[k] for k in ks], after, "gather_finish_" + group)))
        if group == "w_in":
            return _w_in_to_layer_order(g["w_in"])
        if group == "small":
            return g["small"]
        if group == "mix":
            return (g["w_out"].reshape(D, D),
                    g["pool_w"].reshape(NDEV, 4, 32, GC).transpose(1, 0, 2, 3).reshape(4, GC, GC))
        return g["mlp_w1"] if group == "up" else g[ks[0]].reshape(DFF, W2_COLS)

    def grad(self, group, grads):
        parts = dict(grads)
        if group == "w_in":
            parts["w_in"] = _w_in_grad_to_parts(parts["w_in"])
            parts["pool_w"] = (parts["pool_w"].astype(BF16).reshape(4, 4, 2, 32, GC).transpose(2, 1, 0, 3, 4)
                               .reshape(2, 4, 4 * 32, GC))
        ks = GRAD_GROUPS[group]
        started, token = _to_sibling_start([parts[k] for k in ks], "grad_sibling_start_" + group)
        self.sibling[group] = started
        return token

    def grad_mid(self, group, after):
        ks = GRAD_GROUPS[group]
        both = _to_sibling_finish(self.sibling[group], after, "grad_sibling_finish_" + group)
        tile = lambda k, p: (p.shape[2], 512) if k == "w_in" else (SHARD_ROWS[k], p.shape[3])
        sums = [_chip_sum(p, s, self.my_c, tile(k, p), "chip_sum_" + k) for k, (p, s) in zip(ks, both)]
        self.chips[group], token = _to_chips_start(sums, "grad_chips_start_" + group)
        return token

    def chip_sum_side(self, group, after, steps):
        (parts, from_sibling), = _to_sibling_finish(self.sibling[group], after, "grad_sibling_finish_" + group)
        return _chip_sum_side(parts, from_sibling, self.my_c, steps)

    def adamw_side(self, group, after, steps):
        (k, (sums, landed)), = self.grad_finish(group, after).items()
        self.hosted = k
        return _adamw_side(sums, landed, self.my_chip, *self.shards[k], steps)

    def update(self, group, after):
        for k, (sums, landed) in self.grad_finish(group, after).items():
            self.done[k] = _adamw_landed(sums, landed, self.my_chip, *self.shards[k], SHARD_ROWS[k], "adamw_" + k)
            after = self.done[k][1]
        return after

    def grad_chips(self, group, sums):
        self.chips[group], token = _to_chips_start(sums, "grad_chips_start_" + group)
        return token

    def grad_finish(self, group, after):
        done = _to_chips_finish(self.chips[group], after, "grad_chips_finish_" + group)
        return dict(zip(GRAD_GROUPS[group], done))


def _layer_step(x, target, ex, norm1_w, gate_b, gla_norm_w, pool_scale, norm2_w, final_norm_w):
    small = ex.weights("small", ex.forward("small", ex.started))
    meta_full = small[:, 0:4].reshape(NDEV, N_META, D // NDEV).transpose(1, 0, 2).reshape(N_META, D)
    gw2_full = small[:, 4].reshape(NDEV, RANK, KW // NDEV).transpose(1, 0, 2).reshape(RANK, KW)
    gw2p = jnp.pad(gw2_full, ((0, 128 - RANK), (0, 0))).astype(BF16)
    h0 = jnp.concatenate([jnp.zeros((PAD, D), F32), meta_full, x], axis=0)
    u1 = _rmsnorm_fwd(h0, norm1_w, "rmsnorm1", deps=ex.early)
    win_p = ex.weights("w_in", ex.forward("w_in", u1))
    proj = _matmul(u1, win_p, mode="nt", tm=1056, tn=1408, tk=2048, name="proj")
    tok = ex.forward("mix", proj)
    mixed, o_saved, st_saved = _gla_fwd(proj, gw2p, gate_b, gla_norm_w, deps=[tok])
    wout_f, poolw_f = ex.weights("mix", mixed)
    mixed = _pool_fwd(proj, poolw_f, pool_scale, mixed)
    h1 = _matmul(mixed, wout_f, mode="nn", tm=1056, tn=1024, tk=2048, name="mix_out", epi="add", extra=h0)
    tok = ex.forward("up", h1)
    u2 = _rmsnorm_fwd(h1, norm2_w, "rmsnorm2", deps=[tok])
    w1_g = ex.weights("up", u2)
    z, act = _matmul(u2, w1_g, mode="nn", tm=1056, tn=1024, tk=2048, name="mlp_up", epi="relu2", b_slots=True)
    w2, h2 = [], None
    for q in range(W2_PIECES):
        w2.append(ex.weights("down_%d" % q, ex.forward("down_%d" % q, act if h2 is None else h2)))
        h2 = _matmul(act, w2[q], mode="nn", tm=1056, tn=W2_COLS, tk=2048, name="mlp_down_%d" % q, epi="add",
                     extra=h1, n_total=D, col_block=q, into=h2)
    dh2, dh2b, sq, g_fnw = _loss_head(h2, final_norm_w, target)

    g_w2 = _matmul(act, dh2b, mode="tn", tm=512, tn=2048, tk=LP, name="d_mlp_w2", out_dtype=BF16, out_slots="rows")
    tok = ex.grad("down", dict(mlp_w2=g_w2))
    dz = _matmul(dh2b, tuple(w2), mode="nt", tm=1056, tn=1024, tk=2048, name="d_act", out_dtype=BF16, epi="dz",
                 extra=z, deps=[tok])
    hosted = lambda res, side: res if side is not None else (res, None)
    side = ex.chip_sum_side("down", dz, 16)
    g_w1, sums = hosted(_matmul(u2, dz, mode="tn", tm=1024, tn=1024, tk=LP, name="d_mlp_w1", out_dtype=BF16,
                                out_slots="cols", side=side), side)
    toks = [ex.grad_chips("down", sums), ex.grad("up", dict(mlp_w1=g_w1))]
    du2 = _matmul(dz, w1_g, mode="nt", tm=1056, tn=1024, tk=2048, name="d_u2", b_slots=True, deps=toks)
    side = ex.chip_sum_side("up", du2, 8)
    dh1, dh1b, g_n2 = _rmsnorm_bwd(h1, norm2_w, du2, dh2, "rmsnorm2_bwd")
    g_wout, sums = hosted(_matmul(mixed, dh1b, mode="tn", tm=256, tn=2048, tk=LP, name="d_w_out", out_dtype=BF16,
                                  out_slots="rows", side=side), side)
    toks = [ex.grad_chips("up", sums), ex.grad("mix", dict(w_out=g_wout))]
    dmixed = _matmul(dh1b, wout_f, mode="nt", tm=1056, tn=1024, tk=2048, name="d_mixed", deps=toks)
    tok = ex.grad_mid("mix", dmixed)
    dproj, dglr, g_gnw, g_gb, g_gw2 = _gla_bwd(proj, dmixed, o_saved, st_saved, gw2p, gate_b, gla_norm_w, deps=[tok])
    dproj, g_poolw, g_psc = _pool_bwd(proj, dmixed, poolw_f, pool_scale, dproj)
    dproj = lax.dynamic_update_slice(dproj, dglr, (0, GLR_BLK * 128))
    g_win_p = _matmul(dproj, u1, mode="tn", tm=384, tn=2048, tk=LP, name="d_w_in", out_dtype=BF16)
    tok = ex.grad("w_in", dict(pool_w=g_poolw, w_in=g_win_p))
    tok = ex.grad_mid("w_in", ex.update("down", tok))
    side = ex.adamw_side("up", tok, 16)
    du1, ex.done[ex.hosted] = hosted(_matmul(dproj, win_p, mode="nn", tm=1056, tn=256, tk=D_INP, name="d_u1",
                                             deps=[tok], side=side), side)
    dx, dhead, g_n1 = _rmsnorm_bwd_input(h0, norm1_w, du1, dh1, "rmsnorm1_bwd")
    return dict(dx=dx, dhead=dhead, sq=sq, gate_w2=g_gw2, norm1_w=g_n1, norm2_w=g_n2, final_norm_w=g_fnw, pool_scale=g_psc,
                gate_b=g_gb, gla_norm_w=g_gnw)
```

```python
import jax
import jax.numpy as jnp
from jax import lax
from jax.experimental import pallas as pl
from jax.experimental.pallas import tpu as pltpu

F32, BF16 = jnp.float32, jnp.bfloat16
MESH = pl.DeviceIdType.MESH

NDEV = 8
D = 2048
SEQ = 2048
N_META = 16
CHUNK = 64
PAD = (-N_META) % CHUNK
ROW_X = PAD + N_META
LP = ROW_X + SEQ
NCH = LP // CHUNK
H = 4
DK = 128
DV = 256
KW = H * DK
GW = H * DV
PW = 1024
RANK = 16
TAU = 16.0
WINDOWS = (2, 4, 8, 16)
GC = 256
DFF = 4 * D
EPS = 1e-6
D_IN = 2 * KW + 2 * GW + RANK + PW
D_INP = 4224
GLR_BLK = (2 * KW + 2 * GW + PW) // 128
POOL_BLK = (2 * KW + 2 * GW) // GC
LR, B1, B2, AEPS, WD, STEP = 0.001, 0.9, 0.999, 1e-08, 0.01, 10
VMEM_LIMIT = 48 * 1024 * 1024
CPS = 3


def _params(sem=None):
    return pltpu.CompilerParams(dimension_semantics=sem, vmem_limit_bytes=VMEM_LIMIT)


def _sds(shape, dtype):
    return jax.ShapeDtypeStruct(shape, dtype)


def _me():
    return lax.axis_index("x"), lax.axis_index("y"), lax.axis_index("c")


def _peer(j):
    x, y, c = _me()
    return (x ^ ((j >> 2) & 1), y ^ ((j >> 1) & 1), c ^ (j & 1))


def _slot(dev):
    return 4 * dev[0] + 2 * dev[1] + dev[2]


HBM_SPEC = pl.BlockSpec(memory_space=pltpu.HBM)
SEM_SPEC = pl.BlockSpec(memory_space=pltpu.SEMAPHORE)
ANY_SPEC = pl.BlockSpec(memory_space=pl.ANY)
EFFECT = pltpu.SideEffectType.DATAFLOW_SIDE_EFFECTING
SIBLING = 1
OTHER_CHIPS = (2, 4, 6)


def _in_hbm(a):
    return pltpu.with_memory_space_constraint(a, pltpu.HBM)


def _chip(dev):
    return 2 * dev[0] + dev[1]


def _rcopy(src, dst, send_sem, recv_sem, to):
    return pltpu.make_async_remote_copy(src_ref=src, dst_ref=dst, send_sem=send_sem, recv_sem=recv_sem,
                                        device_id=to, device_id_type=MESH)


def _split_call(body, name, ins, in_specs, out_shape, out_specs, aliases, scratch=(), collective_id=None):
    n = len(ins) + len(out_shape)

    def with_token(*refs):
        body(*refs[:n], *refs[n + 1:])
        refs[n][...] = jnp.zeros_like(refs[n])

    return pl.pallas_call(
        with_token, name=name, in_specs=in_specs, out_shape=list(out_shape) + [_sds((8, 128), F32)],
        out_specs=list(out_specs) + [pl.BlockSpec(memory_space=pltpu.VMEM)],
        input_output_aliases=aliases, scratch_shapes=list(scratch),
        compiler_params=pltpu.CompilerParams(has_side_effects=EFFECT, collective_id=collective_id),
    )(*ins)


def _handshake(relations):
    barrier = pltpu.get_barrier_semaphore()
    for rel in relations:
        pl.semaphore_signal(barrier, inc=1, device_id=_peer(rel), device_id_type=MESH)
    pl.semaphore_wait(barrier, len(relations))


def _after(body, n_in, deps):
    deps = [d for d in deps if d is not None]
    if not deps:
        return body, [], []
    return (lambda *refs: body(*refs[:n_in], *refs[n_in + len(deps):])), deps, [ANY_SPEC] * len(deps)


def _gather_start(shards, name, after):
    n = len(shards)
    me = _slot(_me())
    lands = [lax.dynamic_update_slice(lax.empty((NDEV,) + s.shape, s.dtype), s[None], (me, 0, 0)) for s in shards]

    def body(*refs):
        src, land = refs[:n], refs[n:2 * n]
        outs = refs[2 * n + 1:]
        for i in range(n):
            send_sems, recv_sems = outs[4 * i], outs[4 * i + 1]
            for k, rel in enumerate((SIBLING,) + OTHER_CHIPS):
                _rcopy(src[i], land[i].at[_slot(_me())], send_sems.at[k], recv_sems.at[k], _peer(rel)).start()

    out_shape, out_specs, aliases = [], [], {}
    for i, s in enumerate(shards):
        out_shape += [pltpu.SemaphoreType.DMA((4,)), pltpu.SemaphoreType.DMA((4,)), pltpu.HBM(s.shape, s.dtype),
                      pltpu.HBM((NDEV,) + s.shape, s.dtype)]
        out_specs += [SEM_SPEC, SEM_SPEC, HBM_SPEC, HBM_SPEC]
        aliases[i] = 4 * i + 2
        aliases[n + i] = 4 * i + 3
    res = _split_call(body, name, [_in_hbm(s) for s in shards] + [_in_hbm(l) for l in lands] + [after],
                      [HBM_SPEC] * (2 * n) + [ANY_SPEC], out_shape, out_specs, aliases)
    return [tuple(res[4 * i:4 * i + 4]) for i in range(n)], res[-1]


def _gather_forward(started, after, name):
    n = len(started)

    def body(*refs):
        land, recv1 = refs[:n], refs[n:2 * n]
        outs = refs[2 * n + 1:]
        for i in range(n):
            send2, recv2 = outs[3 * i + 1], outs[3 * i + 2]
            for k, rel in enumerate(OTHER_CHIPS):
                blk = land[i].at[_slot(_peer(rel))]
                _rcopy(blk, blk, send2.at[k], recv1[i].at[1 + k], _peer(rel)).wait_recv()
                _rcopy(blk, blk, send2.at[k], recv2.at[k], _peer(SIBLING)).start()

    ins = [_in_hbm(st[3]) for st in started] + [st[1] for st in started] + [after]
    out_shape, out_specs, aliases = [], [], {}
    for i, st in enumerate(started):
        out_shape += [pltpu.HBM(st[3].shape, st[3].dtype), pltpu.SemaphoreType.DMA((3,)), pltpu.SemaphoreType.DMA((3,))]
        out_specs += [HBM_SPEC, SEM_SPEC, SEM_SPEC]
        aliases[i] = 3 * i
    res = _split_call(body, name, ins, [HBM_SPEC] * n + [SEM_SPEC] * n + [ANY_SPEC], out_shape, out_specs, aliases)
    return [(st[0], st[1], st[2], res[3 * i], res[3 * i + 1], res[3 * i + 2]) for i, st in enumerate(started)], res[-1]


def _gather_finish(forwarded, after, name):
    n = len(forwarded)

    def body(*refs):
        for i in range(n):
            send1, recv1, src, land, send2, recv2 = refs[6 * i:6 * i + 6]
            me = _slot(_me())
            sib = _slot(_peer(SIBLING))
            for k, rel in enumerate((SIBLING,) + OTHER_CHIPS):
                _rcopy(src, land.at[me], send1.at[k], recv1.at[k], _peer(rel)).wait_send()
            _rcopy(src, land.at[sib], send1.at[0], recv1.at[0], _peer(SIBLING)).wait_recv()
            for k, rel in enumerate(OTHER_CHIPS):
                mine, theirs = land.at[_slot(_peer(rel))], land.at[_slot(_peer(rel ^ SIBLING))]
                _rcopy(mine, mine, send2.at[k], recv2.at[k], _peer(SIBLING)).wait_send()
                _rcopy(theirs, theirs, send2.at[k], recv2.at[k], _peer(SIBLING)).wait_recv()

    ins, in_specs, out_shape, aliases = [], [], [], {}
    for i, f in enumerate(forwarded):
        ins += [f[0], f[1], _in_hbm(f[2]), _in_hbm(f[3]), f[4], f[5]]
        in_specs += [SEM_SPEC, SEM_SPEC, HBM_SPEC, HBM_SPEC, SEM_SPEC, SEM_SPEC]
        out_shape.append(pltpu.HBM(f[3].shape, f[3].dtype))
        aliases[6 * i + 3] = i
    res = _split_call(body, name, ins + [after], in_specs + [ANY_SPEC], out_shape, [HBM_SPEC] * n, aliases)
    return list(res[:-1])


def _to_sibling_start(parts, name, collective_id):
    n = len(parts)
    lands = [lax.empty(p.shape[1:], p.dtype) for p in parts]

    def body(*refs):
        _handshake((SIBLING,))
        src, land = refs[:n], refs[n:2 * n]
        outs = refs[2 * n:]
        other = 1 - lax.axis_index("c")
        for i in range(n):
            _rcopy(src[i].at[other], land[i], outs[4 * i], outs[4 * i + 1], _peer(SIBLING)).start()

    out_shape, out_specs, aliases = [], [], {}
    for i, p in enumerate(parts):
        out_shape += [pltpu.SemaphoreType.DMA(()), pltpu.SemaphoreType.DMA(()), pltpu.HBM(p.shape, p.dtype),
                      pltpu.HBM(p.shape[1:], p.dtype)]
        out_specs += [SEM_SPEC, SEM_SPEC, HBM_SPEC, HBM_SPEC]
        aliases[i] = 4 * i + 2
        aliases[n + i] = 4 * i + 3
    res = _split_call(body, name, [_in_hbm(p) for p in parts] + [_in_hbm(l) for l in lands], [HBM_SPEC] * (2 * n),
                      out_shape, out_specs, aliases, collective_id=collective_id)
    return [tuple(res[4 * i:4 * i + 4]) for i in range(n)], res[-1]


def _to_sibling_finish(started, after, name):
    n = len(started)

    def body(*refs):
        for i in range(n):
            send, recv, src, land = refs[4 * i:4 * i + 4]
            cp = _rcopy(src.at[0], land, send, recv, _peer(SIBLING))
            cp.wait_send()
            cp.wait_recv()

    ins, in_specs, out_shape, aliases = [], [], [], {}
    for i, st in enumerate(started):
        ins += [st[0], st[1], _in_hbm(st[2]), _in_hbm(st[3])]
        in_specs += [SEM_SPEC, SEM_SPEC, HBM_SPEC, HBM_SPEC]
        out_shape += [pltpu.HBM(st[2].shape, st[2].dtype), pltpu.HBM(st[3].shape, st[3].dtype)]
        aliases[4 * i + 2] = 2 * i
        aliases[4 * i + 3] = 2 * i + 1
    res = _split_call(body, name, ins + [after], in_specs + [ANY_SPEC], out_shape, [HBM_SPEC] * (2 * n), aliases)
    return [(res[2 * i], res[2 * i + 1]) for i in range(n)]


def _chip_sum(parts, from_sibling, my_c, tile, name):
    _, _, r, c = parts.shape
    tr, tc = tile

    def body(c_ref, p_ref, s_ref, o_ref):
        o_ref[...] = (p_ref[...].astype(F32) + s_ref[...].astype(F32)).astype(o_ref.dtype)

    blk = pl.BlockSpec((4, tr, tc), lambda i, j, c_ref: (0, i, j))
    return pl.pallas_call(
        body, name=name, out_shape=_sds((4, r, c), parts.dtype),
        grid_spec=pltpu.PrefetchScalarGridSpec(
            num_scalar_prefetch=1, grid=(r // tr, c // tc),
            in_specs=[pl.BlockSpec((None, 4, tr, tc), lambda i, j, c_ref: (c_ref[0], 0, i, j)), blk], out_specs=blk),
        compiler_params=_params(("parallel", "parallel")),
    )(my_c, parts, from_sibling)


class _Side:
    def __init__(self, scalar, ins, outs, fn):
        self.scalar, self.ins, self.outs, self.fn = scalar, ins, outs, fn


def _chip_sum_side(parts, from_sibling, my_c, steps):
    _, _, r, c = parts.shape
    rows = r // steps
    assert rows * steps == r and rows % 16 == 0
    return _Side(
        my_c,
        [(parts, (None, 4, rows, c), lambda t, s: (s[0], 0, t, 0)), (from_sibling, (4, rows, c), lambda t, s: (0, t, 0))],
        [((4, r, c), parts.dtype, (4, rows, c), lambda t, s: (0, t, 0))],
        lambda p, q: [(p.astype(F32) + q.astype(F32)).astype(parts.dtype)])


def _to_chips_start(sums, name, collective_id):
    n = len(sums)
    lands = [lax.empty((3,) + s.shape[1:], s.dtype) for s in sums]

    def body(*refs):
        _handshake(OTHER_CHIPS)
        src, land = refs[:n], refs[n:2 * n]
        outs = refs[2 * n:]
        for i in range(n):
            for k, rel in enumerate(OTHER_CHIPS):
                to = _peer(rel)
                _rcopy(src[i].at[_chip(to)], land[i].at[k], outs[4 * i].at[k], outs[4 * i + 1].at[k], to).start()

    out_shape, out_specs, aliases = [], [], {}
    for i, s in enumerate(sums):
        out_shape += [pltpu.SemaphoreType.DMA((3,)), pltpu.SemaphoreType.DMA((3,)), pltpu.HBM(s.shape, s.dtype),
                      pltpu.HBM((3,) + s.shape[1:], s.dtype)]
        out_specs += [SEM_SPEC, SEM_SPEC, HBM_SPEC, HBM_SPEC]
        aliases[i] = 4 * i + 2
        aliases[n + i] = 4 * i + 3
    res = _split_call(body, name, [_in_hbm(s) for s in sums] + [_in_hbm(l) for l in lands], [HBM_SPEC] * (2 * n),
                      out_shape, out_specs, aliases, collective_id=collective_id)
    return [tuple(res[4 * i:4 * i + 4]) for i in range(n)], res[-1]


def _to_chips_finish(started, after, name):
    n = len(started)

    def body(*refs):
        for i in range(n):
            send, recv, src, land = refs[4 * i:4 * i + 4]
            for k, rel in enumerate(OTHER_CHIPS):
                cp = _rcopy(src.at[0], land.at[k], send.at[k], recv.at[k], _peer(rel))
                cp.wait_send()
                cp.wait_recv()

    ins, in_specs, out_shape, aliases = [], [], [], {}
    for i, st in enumerate(started):
        ins += [st[0], st[1], _in_hbm(st[2]), _in_hbm(st[3])]
        in_specs += [SEM_SPEC, SEM_SPEC, HBM_SPEC, HBM_SPEC]
        out_shape += [pltpu.HBM(st[2].shape, st[2].dtype), pltpu.HBM(st[3].shape, st[3].dtype)]
        aliases[4 * i + 2] = 2 * i
        aliases[4 * i + 3] = 2 * i + 1
    res = _split_call(body, name, ins + [after], in_specs + [ANY_SPEC], out_shape, [HBM_SPEC] * (2 * n), aliases)
    return [(res[2 * i], res[2 * i + 1]) for i in range(n)]


def _reduce_small(v, name, deps=()):
    _, r, c = v.shape

    def body(v_ref, o_ref, land, send_sems, recv_sems):
        me = _slot(_me())
        copies = []
        for j in range(1, NDEV):
            to = _peer(j)
            cp = _rcopy(v_ref.at[_slot(to)], land.at[me], send_sems.at[j - 1], recv_sems.at[j - 1], to)
            cp.start()
            copies.append(cp)
        land[me] = v_ref[me]
        for cp in copies:
            cp.wait()
        acc = land[0]
        for k in range(1, NDEV):
            acc = acc + land[k]
        o_ref[...] = acc

    vm = pl.BlockSpec(memory_space=pltpu.VMEM)
    body, dep_ins, dep_specs = _after(body, 1, deps)
    return pl.pallas_call(
        body, name=name, out_shape=_sds((r, c), F32), in_specs=[vm] + dep_specs, out_specs=vm,
        scratch_shapes=[pltpu.VMEM((NDEV, r, c), F32), pltpu.SemaphoreType.DMA((NDEV - 1,)),
                        pltpu.SemaphoreType.DMA((NDEV - 1,))],
        compiler_params=_params(),
    )(v, *dep_ins)


def _matmul(a, b, *, mode, tm, tn, tk, name, out_dtype=F32, epi=None, extra=None, b_slots=False, out_slots=False,
            deps=(), col_block=0, into=None, n_total=None, side=None):
    b_pair = b if isinstance(b, tuple) else None
    if b_pair:
        assert mode == "nt" and tk == len(b) * b[0].shape[1] == a.shape[1] and not b_slots
        b = b[0]
    slot_w = b.shape[-1] if b_slots else None
    if mode == "nn":
        M, K = a.shape
        N = NDEV * slot_w if b_slots else b.shape[1]
    elif mode == "tn":
        K, M = a.shape
        N = b.shape[1]
    else:
        M, K = a.shape
        N = b.shape[-2]
        if b_slots:
            assert K == NDEV * slot_w and tk % slot_w == 0
    if mode == "nn" and b_slots:
        assert tn == slot_w
    if out_slots == "cols":
        assert tn * NDEV == N
    if out_slots == "rows":
        assert (M // NDEV) % tm == 0
    assert M % tm == 0 and N % tn == 0 and K % tk == 0, (name, M, N, K, tm, tn, tk)
    nk = K // tk
    dims = {"nn": ((1,), (0,)), "tn": ((0,), (0,)), "nt": ((1,), (1,))}[mode]

    if mode == "tn":
        a_spec = pl.BlockSpec((tk, tm), lambda i, j, k: (k, i))
    else:
        a_spec = pl.BlockSpec((tm, tk), lambda i, j, k: (i, k))
    if b_pair:
        b_spec = pl.BlockSpec((tn, tk // len(b_pair)), lambda i, j, k: (j, 0))
    elif mode == "nt":
        b_spec = (pl.BlockSpec((tk // slot_w, tn, slot_w), lambda i, j, k: (k, j, 0)) if b_slots
                  else pl.BlockSpec((tn, tk), lambda i, j, k: (j, k)))
    else:
        b_spec = (pl.BlockSpec((None, tk, tn), lambda i, j, k: (j, k, 0)) if b_slots
                  else pl.BlockSpec((tk, tn), lambda i, j, k: (k, j)))
    tile = pl.BlockSpec((tm, tn), lambda i, j, k: (i, j + col_block))
    if out_slots == "cols":
        out_spec = pl.BlockSpec((None, None, tm, tn), lambda i, j, k: (j % 2, j // 2, i, 0))
        out_shape = _sds((2, 4, M, tn), out_dtype)
    elif out_slots == "rows":
        per = M // NDEV // tm
        out_spec = pl.BlockSpec((None, None, tm, tn), lambda i, j, k: ((i // per) % 2, (i // per) // 2, i % per, j))
        out_shape = _sds((2, 4, M // NDEV, N), out_dtype)
    else:
        out_spec, out_shape = tile, _sds((M, n_total or N), out_dtype)
    ins, in_specs = [a, b], [a_spec, b_spec]
    if b_pair:
        ins += list(b_pair[1:])
        in_specs += [b_spec] * (len(b_pair) - 1)
    n_b = len(ins) - 1
    if epi in ("add", "dz"):
        ins.append(extra)
        in_specs.append(tile)
    aliases = {}
    if into is not None:
        aliases[len(ins)] = 0
        ins.append(into)
        in_specs.append(ANY_SPEC)
    if epi == "relu2":
        out_specs, out_shapes = [tile, tile], [_sds((M, N), F32), _sds((M, N), BF16)]
    else:
        out_specs, out_shapes = out_spec, out_shape
    n_in = len(ins)

    def body(*refs):
        outs = refs[n_in:-1] if nk > 1 else refs[n_in:]
        extra_ref = refs[1 + n_b]

        def finish(p):
            if epi is None:
                outs[0][...] = p.astype(out_dtype)
            elif epi == "add":
                outs[0][...] = (p + extra_ref[...]).astype(out_dtype)
            elif epi == "relu2":
                outs[0][...] = p
                rz = jnp.maximum(p, 0.0)
                outs[1][...] = (rz * rz).astype(BF16)
            else:
                outs[0][...] = (p * (2.0 * jnp.maximum(extra_ref[...], 0.0))).astype(out_dtype)

        def product():
            av = refs[0][...].astype(BF16)
            if b_pair:
                w = tk // len(b_pair)
                return sum(lax.dot_general(av[:, s * w:(s + 1) * w], refs[1 + s][...], (dims, ((), ())),
                                           preferred_element_type=F32) for s in range(len(b_pair)))
            if mode == "nt" and b_slots:
                return sum(lax.dot_general(av[:, s * slot_w:(s + 1) * slot_w], refs[1][s], (dims, ((), ())),
                                           preferred_element_type=F32) for s in range(tk // slot_w))
            return lax.dot_general(av, refs[1][...].astype(BF16), (dims, ((), ())), preferred_element_type=F32)

        if nk == 1:
            finish(product())
            return
        acc = refs[-1]
        k = pl.program_id(2)

        @pl.when(k == 0)
        def _():
            acc[...] = jnp.zeros_like(acc)

        acc[...] += product()

        @pl.when(k == nk - 1)
        def _():
            finish(acc[...])

    grid = (M // tm, N // tn, nk)
    scratch = [pltpu.VMEM((tm, tn), F32)] if nk > 1 else []
    if side is None:
        body, dep_ins, dep_specs = _after(body, n_in, deps)
        return pl.pallas_call(
            body, name=name, grid=grid,
            in_specs=in_specs + dep_specs, out_specs=out_specs, out_shape=out_shapes, input_output_aliases=aliases,
            scratch_shapes=scratch, compiler_params=_params(("parallel", "parallel", "arbitrary")),
        )(*ins, *dep_ins)

    deps = [d for d in deps if d is not None]
    step = lambda i, j, k: (i * grid[1] + j) * grid[2] + k
    host = lambda spec: (spec if spec.block_shape is None else
                         pl.BlockSpec(spec.block_shape, lambda i, j, k, s, f=spec.index_map: f(i, j, k)))
    cut = lambda blk, f: pl.BlockSpec(blk, lambda i, j, k, s: f(step(i, j, k), s))
    host_out_specs = list(out_specs) if isinstance(out_specs, list) else [out_specs]
    host_out_shapes = list(out_shapes) if isinstance(out_shapes, list) else [out_shapes]
    n_dep, n_si, n_ho, n_so = len(deps), len(side.ins), len(host_out_specs), len(side.outs)

    def with_side(*refs):
        rest = refs[1:]
        side_in = rest[n_in + n_dep:n_in + n_dep + n_si]
        outs_all = rest[n_in + n_dep + n_si:]
        body(*rest[:n_in], *outs_all[:n_ho], *outs_all[n_ho + n_so:])
        for o_ref, val in zip(outs_all[n_ho:n_ho + n_so], side.fn(*[r[...] for r in side_in])):
            o_ref[...] = val

    res = pl.pallas_call(
        with_side, name=name, input_output_aliases={k + 1: v for k, v in aliases.items()},
        out_shape=host_out_shapes + [_sds(shape, dt) for shape, dt, _, _ in side.outs],
        grid_spec=pltpu.PrefetchScalarGridSpec(
            num_scalar_prefetch=1, grid=grid,
            in_specs=[host(s) for s in in_specs] + [ANY_SPEC] * n_dep + [cut(blk, f) for _, blk, f in side.ins],
            out_specs=[host(s) for s in host_out_specs] + [cut(blk, f) for _, _, blk, f in side.outs],
            scratch_shapes=scratch),
        compiler_params=_params(("parallel", "parallel", "arbitrary")),
    )(side.scalar, *ins, *deps, *[arr for arr, _, _ in side.ins])
    host_res = res[0] if n_ho == 1 else tuple(res[:n_ho])
    return host_res, list(res[n_ho:])


ROWS = 352


def _rmsnorm_fwd(h, w, name, deps=()):
    def body(h_ref, w_ref, u_ref):
        x = h_ref[...]
        rstd = lax.rsqrt(jnp.mean(x * x, axis=-1, keepdims=True) + EPS)
        u_ref[...] = (x * rstd * w_ref[...]).astype(BF16)

    row = pl.BlockSpec((ROWS, D), lambda i: (i, 0))
    body, dep_ins, dep_specs = _after(body, 2, deps)
    return pl.pallas_call(
        body, name=name, grid=(LP // ROWS,), in_specs=[row, pl.BlockSpec((1, D), lambda i: (0, 0))] + dep_specs,
        out_specs=row, out_shape=_sds((LP, D), BF16), compiler_params=_params(("parallel",)),
    )(h, w, *dep_ins)


TOKEN_ROWS = 512


def _rmsnorm_bwd_input(h, w, du, dres, name, deps=()):
    def math(h_ref, w_ref, du_ref, dres_ref):
        x = h_ref[...]
        rstd = lax.rsqrt(jnp.mean(x * x, axis=-1, keepdims=True) + EPS)
        xhat = x * rstd
        dy = du_ref[...]
        dxh = dy * w_ref[...]
        dh = dres_ref[...] + rstd * (dxh - xhat * jnp.mean(dxh * xhat, axis=-1, keepdims=True))
        return dh, jnp.sum(dy * xhat, axis=0, keepdims=True)

    def body(h_ref, w_ref, du_ref, dres_ref, hh_ref, duh_ref, dresh_ref, dx_ref, dhead_ref, gw_ref):
        dx_ref[...], part = math(h_ref, w_ref, du_ref, dres_ref)

        @pl.when(pl.program_id(0) == 0)
        def _():
            dhead_ref[...], head = math(hh_ref, w_ref, duh_ref, dresh_ref)
            gw_ref[...] = part + head

        @pl.when(pl.program_id(0) > 0)
        def _():
            gw_ref[...] += part

    rows = pl.BlockSpec((pl.Element(TOKEN_ROWS), pl.Element(D)),
                        lambda i: (pl.multiple_of(ROW_X + TOKEN_ROWS * i, 8), 0))
    head = pl.BlockSpec((ROW_X, D), lambda i: (0, 0))
    vec = pl.BlockSpec((1, D), lambda i: (0, 0))
    body, dep_ins, dep_specs = _after(body, 7, deps)
    return pl.pallas_call(
        body, name=name, grid=(SEQ // TOKEN_ROWS,),
        in_specs=[rows, vec, rows, rows, head, head, head] + dep_specs,
        out_specs=[pl.BlockSpec((TOKEN_ROWS, D), lambda i: (i, 0)), head, vec],
        out_shape=[_sds((SEQ, D), F32), _sds((ROW_X, D), F32), _sds((1, D), F32)],
        compiler_params=_params(("arbitrary",)),
    )(h, w, du, dres, h, du, dres, *dep_ins)


def _rmsnorm_bwd(h, w, du, dres, name, deps=()):
    def body(h_ref, w_ref, du_ref, dres_ref, dh_ref, dhb_ref, gw_ref):
        x = h_ref[...]
        rstd = lax.rsqrt(jnp.mean(x * x, axis=-1, keepdims=True) + EPS)
        xhat = x * rstd
        dy = du_ref[...]
        dxh = dy * w_ref[...]
        dh = dres_ref[...] + rstd * (dxh - xhat * jnp.mean(dxh * xhat, axis=-1, keepdims=True))
        dh_ref[...] = dh
        dhb_ref[...] = dh.astype(BF16)
        part = jnp.sum(dy * xhat, axis=0, keepdims=True)

        @pl.when(pl.program_id(0) == 0)
        def _():
            gw_ref[...] = part

        @pl.when(pl.program_id(0) > 0)
        def _():
            gw_ref[...] += part

    row = pl.BlockSpec((ROWS, D), lambda i: (i, 0))
    vec = pl.BlockSpec((1, D), lambda i: (0, 0))
    body, dep_ins, dep_specs = _after(body, 4, deps)
    return pl.pallas_call(
        body, name=name, grid=(LP // ROWS,), in_specs=[row, vec, row, row] + dep_specs, out_specs=[row, row, vec],
        out_shape=[_sds((LP, D), F32), _sds((LP, D), BF16), _sds((1, D), F32)],
        compiler_params=_params(("arbitrary",)),
    )(h, w, du, dres, *dep_ins)


def _loss_head(h2, wf, target):
    def body(h_ref, w_ref, t_ref, dh_ref, dhb_ref, sq_ref, gw_ref):
        i = pl.program_id(0)

        @pl.when(i == 0)
        def _():
            sq_ref[...] = jnp.zeros_like(sq_ref)
            gw_ref[...] = jnp.zeros_like(gw_ref)

        def rows(t, live):
            x = h_ref[...]
            rstd = lax.rsqrt(jnp.mean(x * x, axis=-1, keepdims=True) + EPS)
            xhat = x * rstd
            w = w_ref[...]
            err = xhat * w - t
            if live is not None:
                err = jnp.where(live, err, 0.0)
            sq_ref[...] += jnp.sum(err * err, axis=0, keepdims=True)
            dy = err * (1.0 / D)
            gw_ref[...] += jnp.sum(dy * xhat, axis=0, keepdims=True)
            dxh = dy * w
            dh = rstd * (dxh - xhat * jnp.mean(dxh * xhat, axis=-1, keepdims=True))
            dh_ref[...] = dh
            dhb_ref[...] = dh.astype(BF16)

        @pl.when(i == 0)
        def _():
            rid = lax.broadcasted_iota(jnp.int32, (ROWS, D), 0)
            rows(pltpu.roll(t_ref[...], ROW_X, 0), rid >= ROW_X)

        @pl.when(i > 0)
        def _():
            rows(t_ref[...], None)

    row = pl.BlockSpec((ROWS, D), lambda i: (i, 0))
    vec = pl.BlockSpec((1, D), lambda i: (0, 0))
    tgt = pl.BlockSpec((pl.Element(ROWS), pl.Element(D)),
                       lambda i: (pl.multiple_of(jnp.maximum(ROWS * i - ROW_X, 0), 8), 0))
    return pl.pallas_call(
        body, name="loss_head", grid=(LP // ROWS,),
        in_specs=[row, vec, tgt],
        out_specs=[row, row, vec, vec],
        out_shape=[_sds((LP, D), F32), _sds((LP, D), BF16), _sds((1, D), F32), _sds((1, D), F32)],
        compiler_params=_params(("arbitrary",)),
    )(h2, wf, target)


def _dot(a, b, dims):
    return lax.dot_general(a, b, (dims, ((), ())), preferred_element_type=F32)


NN, TN, NT = ((1,), (0,)), ((0,), (0,)), ((1,), (1,))


def _tri_sum(t, x):
    hi = x.astype(BF16)
    r1 = x - hi.astype(F32)
    mid = r1.astype(BF16)
    lo = (r1 - mid.astype(F32)).astype(BF16)
    return _dot(t, hi, NN) + _dot(t, mid, NN) + _dot(t, lo, NN)


def _gla_gates(glr_ref, gw2_ref, gb_ref, rows, row0):
    g_raw = _dot(glr_ref[rows, :].astype(BF16), gw2_ref[...], NN) + gb_ref[...]
    logsig = jnp.minimum(g_raw, 0.0) - jnp.log(1.0 + jnp.exp(-jnp.abs(g_raw)))
    rid = row0 + lax.broadcasted_iota(jnp.int32, g_raw.shape, 0)
    live = rid >= PAD
    return g_raw, jnp.where(live, logsig / TAU, 0.0), live


def _tri_masks():
    r = lax.broadcasted_iota(jnp.int32, (CHUNK, CHUNK), 0)
    c = lax.broadcasted_iota(jnp.int32, (CHUNK, CHUNK), 1)
    return r >= c


def _gla_specs(rev):
    n = NCH // CPS
    R = CPS * CHUNK
    st = (lambda s: n - 1 - s) if rev else (lambda s: s)
    return R, n, st, [
        pl.BlockSpec((R, KW), lambda s: (st(s), 0)),
        pl.BlockSpec((R, KW), lambda s: (st(s), 1)),
        pl.BlockSpec((R, GW), lambda s: (st(s), 1)),
        pl.BlockSpec((R, GW), lambda s: (st(s), 2)),
        pl.BlockSpec((R, 128), lambda s: (st(s), GLR_BLK)),
    ]


def _gla_fwd(proj, gw2p, gate_b, gnw, deps=()):
    R, n, st, pspecs = _gla_specs(False)

    def body(q_ref, k_ref, v_ref, r_ref, glr_ref, gw2_ref, gb_ref, gnw_ref, og_ref, o_ref, st_ref, state):
        s = pl.program_id(0)

        @pl.when(s == 0)
        def _():
            state[...] = jnp.zeros_like(state)

        causal = _tri_masks()
        tri = causal.astype(BF16)
        for c in range(CPS):
            rows = slice(c * CHUNK, (c + 1) * CHUNK)
            _, logg, _ = _gla_gates(glr_ref, gw2_ref, gb_ref, rows, s * R + c * CHUNK)
            G = _tri_sum(tri, logg)
            g_last = G[CHUNK - 1:CHUNK, :]
            q_dec = (q_ref[rows, :] * (DK ** -0.5) * jnp.exp(G)).astype(BF16)
            kk = k_ref[rows, :]
            k_inv = (kk * jnp.exp(-G)).astype(BF16)
            k_end = (kk * jnp.exp(g_last - G)).astype(BF16)
            decay = jnp.exp(g_last)
            for h in range(H):
                lk = slice(h * DK, (h + 1) * DK)
                lv = slice(h * DV, (h + 1) * DV)
                v = v_ref[rows, lv].astype(BF16)
                S = state[h]
                st_ref[c, h] = S
                A = jnp.where(causal, _dot(q_dec[:, lk], k_inv[:, lk], NT), 0.0).astype(BF16)
                o = _dot(A, v, NN) + _dot(q_dec[:, lk], S.astype(BF16), NT)
                state[h] = decay[:, lk] * S + _dot(v, k_end[:, lk], TN)
                o_ref[rows, lv] = o
                on = o * lax.rsqrt(jnp.mean(o * o, axis=-1, keepdims=True) + EPS) * gnw_ref[...]
                rr = r_ref[rows, lv]
                og_ref[rows, lv] = (on * (rr * jax.nn.sigmoid(rr))).astype(BF16)

    full = lambda shape: pl.BlockSpec(shape, lambda s: (0,) * len(shape))
    body, dep_ins, dep_specs = _after(body, 8, deps)
    return pl.pallas_call(
        body, name="gla_fwd", grid=(n,),
        in_specs=pspecs + [full((128, KW)), full((1, KW)), full((1, DV))] + dep_specs,
        out_specs=[pl.BlockSpec((R, GW), lambda s: (s, 0)), pl.BlockSpec((R, GW), lambda s: (s, 0)),
                   pl.BlockSpec((CPS, H, DV, DK), lambda s: (s, 0, 0, 0))],
        out_shape=[_sds((LP, GW + PW), BF16), _sds((LP, GW), F32), _sds((NCH, H, DV, DK), F32)],
        scratch_shapes=[pltpu.VMEM((H, DV, DK), F32)],
        compiler_params=_params(("arbitrary",)),
    )(proj, proj, proj, proj, proj, gw2p, gate_b, gnw, *dep_ins)


def _gla_bwd(proj, dmixed, o_saved, st_saved, gw2p, gate_b, gnw, deps=()):
    R, n, st, pspecs = _gla_specs(True)

    def body(q_ref, k_ref, v_ref, r_ref, glr_ref, dog_ref, o_ref, st_ref, gw2_ref, gb_ref, gnw_ref,
             dqkvr_ref, dglr_ref, ggn_ref, ggb_ref, ggw_ref, gstate):
        s = pl.program_id(0)

        @pl.when(s == 0)
        def _():
            gstate[...] = jnp.zeros_like(gstate)
            ggn_ref[...] = jnp.zeros_like(ggn_ref)
            ggb_ref[...] = jnp.zeros_like(ggb_ref)
            ggw_ref[...] = jnp.zeros_like(ggw_ref)

        causal = _tri_masks()
        tri = causal.astype(BF16)
        tri_up = (lax.broadcasted_iota(jnp.int32, (CHUNK, CHUNK), 0)
                  <= lax.broadcasted_iota(jnp.int32, (CHUNK, CHUNK), 1)).astype(BF16)
        gnw = gnw_ref[...]
        for c in reversed(range(CPS)):
            rows = slice(c * CHUNK, (c + 1) * CHUNK)
            g_raw, logg, live = _gla_gates(glr_ref, gw2_ref, gb_ref, rows, (n - 1 - s) * R + c * CHUNK)
            G = _tri_sum(tri, logg)
            g_last = G[CHUNK - 1:CHUNK, :]
            e_g, e_gi, e_end = jnp.exp(G), jnp.exp(-G), jnp.exp(g_last - G)
            q_dec = q_ref[rows, :] * (DK ** -0.5) * e_g
            kk = k_ref[rows, :]
            k_inv, k_end = kk * e_gi, kk * e_end
            q_dec_b, k_inv_b, k_end_b = q_dec.astype(BF16), k_inv.astype(BF16), k_end.astype(BF16)
            decay = jnp.exp(g_last)
            d_g, d_gl = [], []
            for h in range(H):
                lk = slice(h * DK, (h + 1) * DK)
                lv = slice(h * DV, (h + 1) * DV)
                o = o_ref[rows, lv]
                rr = r_ref[rows, lv]
                dog = dog_ref[rows, lv]
                rstd = lax.rsqrt(jnp.mean(o * o, axis=-1, keepdims=True) + EPS)
                ohat = o * rstd
                sr = jax.nn.sigmoid(rr)
                don = dog * (rr * sr)
                dqkvr_ref[rows, 2 * KW + GW + h * DV:2 * KW + GW + (h + 1) * DV] = (
                    dog * (ohat * gnw) * (sr * (1.0 + rr * (1.0 - sr)))).astype(BF16)
                ggn_ref[...] += jnp.sum(don * ohat, axis=0, keepdims=True)
                dohat = don * gnw
                do = (rstd * (dohat - ohat * jnp.mean(dohat * ohat, axis=-1, keepdims=True))).astype(BF16)
                v = v_ref[rows, lv].astype(BF16)
                S = st_ref[c, h]
                gS = gstate[h]
                S_b, gS_b = S.astype(BF16), gS.astype(BF16)
                qd, ki, ke = q_dec_b[:, lk], k_inv_b[:, lk], k_end_b[:, lk]
                A = jnp.where(causal, _dot(qd, ki, NT), 0.0).astype(BF16)
                dA = jnp.where(causal, _dot(do, v, NT), 0.0).astype(BF16)
                dv = _dot(A, do, TN) + _dot(ke, gS_b, NT)
                dq_dec = _dot(dA, ki, NN) + _dot(do, S_b, NN)
                dk_inv = _dot(dA, qd, TN)
                dk_end = _dot(v, gS_b, NN)
                d_decay = jnp.sum(gS * S, axis=0, keepdims=True)
                gstate[h] = decay[:, lk] * gS + _dot(do, qd, TN)
                dqkvr_ref[rows, lk] = (dq_dec * e_g[:, lk] * (DK ** -0.5)).astype(BF16)
                dqkvr_ref[rows, KW + h * DK:KW + (h + 1) * DK] = (
                    dk_inv * e_gi[:, lk] + dk_end * e_end[:, lk]).astype(BF16)
                dqkvr_ref[rows, 2 * KW + h * DV:2 * KW + (h + 1) * DV] = dv.astype(BF16)
                ke_prod = dk_end * k_end[:, lk]
                d_g.append(dq_dec * q_dec[:, lk] - dk_inv * k_inv[:, lk] - ke_prod)
                d_gl.append(jnp.sum(ke_prod, axis=0, keepdims=True) + d_decay * decay[:, lk])
            dlogg = _tri_sum(tri_up, jnp.concatenate(d_g, axis=1)) + jnp.concatenate(d_gl, axis=1)
            dg_raw = jnp.where(live, dlogg * (1.0 / TAU) * jax.nn.sigmoid(-g_raw), 0.0)
            ggb_ref[...] += jnp.sum(dg_raw, axis=0, keepdims=True)
            dg_b = dg_raw.astype(BF16)
            ggw_ref[...] += _dot(glr_ref[rows, :].astype(BF16), dg_b, TN)
            dglr_ref[rows, :] = _dot(dg_b, gw2_ref[...], NT).astype(BF16)

    full = lambda shape: pl.BlockSpec(shape, lambda s: (0,) * len(shape))
    body, dep_ins, dep_specs = _after(body, 11, deps)
    return pl.pallas_call(
        body, name="gla_bwd", grid=(n,),
        in_specs=pspecs + [pl.BlockSpec((R, GW), lambda s: (st(s), 0)), pl.BlockSpec((R, GW), lambda s: (st(s), 0)),
                           pl.BlockSpec((CPS, H, DV, DK), lambda s: (st(s), 0, 0, 0)),
                           full((128, KW)), full((1, KW)), full((1, DV))] + dep_specs,
        out_specs=[pl.BlockSpec((R, 2 * KW + 2 * GW), lambda s: (st(s), 0)), pl.BlockSpec((R, 128), lambda s: (st(s), 0)),
                   full((1, DV)), full((1, KW)), full((128, KW))],
        out_shape=[_sds((LP, D_INP), BF16), _sds((LP, 128), BF16),
                   _sds((1, DV), F32), _sds((1, KW), F32), _sds((128, KW), F32)],
        scratch_shapes=[pltpu.VMEM((H, DV, DK), F32)],
        compiler_params=_params(("arbitrary",)),
    )(proj, proj, proj, proj, proj, dmixed, o_saved, st_saved, gw2p, gate_b, gnw, *dep_ins)


def _pool_pre(x, win, rid):
    s, step = x, 1
    while step < win:
        s = s + pltpu.roll(s, step, 0)
        step *= 2
    cnt = jnp.clip(rid - (PAD - 1), 1, win).astype(F32)
    live = rid >= PAD
    return jnp.where(live, s / cnt - x, 0.0), cnt, live


def _pool_fwd(proj, pool_w, pool_scale, mixed):
    def body(pu_ref, w_ref, sc_ref, _, o_ref):
        rid = lax.broadcasted_iota(jnp.int32, (LP, GC), 0)
        for g, win in enumerate(WINDOWS):
            @pl.when(pl.program_id(0) == g)
            def _():
                y, _, _ = _pool_pre(pu_ref[...], win, rid)
                o_ref[...] = (_dot(y.astype(BF16), w_ref[...], NN) * sc_ref[...]).astype(BF16)

    col = lambda base: pl.BlockSpec((LP, GC), lambda g: (0, base + g))
    return pl.pallas_call(
        body, name="pool_fwd", grid=(len(WINDOWS),),
        in_specs=[col(POOL_BLK), pl.BlockSpec((None, GC, GC), lambda g: (g, 0, 0)),
                  pl.BlockSpec((1, GC), lambda g: (0, g)), ANY_SPEC],
        out_specs=col(GW // GC), out_shape=_sds(mixed.shape, BF16), input_output_aliases={3: 0},
        compiler_params=_params(("parallel",)),
    )(proj, pool_w, pool_scale, mixed)


def _pool_bwd(proj, dmixed, pool_w, pool_scale, dproj):
    def body(pu_ref, do_ref, w_ref, sc_ref, _, dpu_ref, dw_ref, dsc_ref):
        rid = lax.broadcasted_iota(jnp.int32, (LP, GC), 0)
        for g, win in enumerate(WINDOWS):
            @pl.when(pl.program_id(0) == g)
            def _():
                y, cnt, live = _pool_pre(pu_ref[...], win, rid)
                y_b = y.astype(BF16)
                w = w_ref[...]
                do = do_ref[...]
                dsc_ref[...] = jnp.sum(do * _dot(y_b, w, NN), axis=0, keepdims=True)
                dyw = (do * sc_ref[...]).astype(BF16)
                dw_ref[...] = _dot(y_b, dyw, TN)
                dy = jnp.where(live, _dot(dyw, w, NT), 0.0)
                s, step = dy / cnt, 1
                while step < win:
                    s = s + pltpu.roll(s, LP - step, 0)
                    step *= 2
                dpu_ref[...] = (s - dy).astype(BF16)

    col = lambda base: pl.BlockSpec((LP, GC), lambda g: (0, base + g))
    mat = pl.BlockSpec((None, GC, GC), lambda g: (g, 0, 0))
    vec = pl.BlockSpec((1, GC), lambda g: (0, g))
    return pl.pallas_call(
        body, name="pool_bwd", grid=(len(WINDOWS),),
        in_specs=[col(POOL_BLK), col(GW // GC), mat, vec, ANY_SPEC], out_specs=[col(POOL_BLK), mat, vec],
        out_shape=[_sds(dproj.shape, BF16), _sds((4, GC, GC), F32), _sds((1, PW), F32)],
        input_output_aliases={4: 0}, compiler_params=_params(("parallel",)),
    )(proj, dmixed, pool_w, pool_scale, dproj)


def _adamw_math(w, g, m, v):
    m = B1 * m + (1.0 - B1) * g
    v = B2 * v + (1.0 - B2) * (g * g)
    m_hat = m * (1.0 / (1.0 - B1 ** STEP))
    v_hat = v * (1.0 / (1.0 - B2 ** STEP))
    return -LR * (m_hat / (jnp.sqrt(v_hat) + AEPS) + WD * w), m, v


def _adamw_landed(sums, landed, my_chip, w, m, v, rows, name, cols=None):
    _, r, c = w.shape

    def body(chip_ref, s_ref, l_ref, w_ref, m_ref, v_ref, g_ref, d_ref, mo_ref, vo_ref):
        g = s_ref[...].astype(F32)
        for k in range(3):
            g = g + l_ref[k].astype(F32)
        g_ref[...] = g
        d_ref[...], mo_ref[...], vo_ref[...] = _adamw_math(w_ref[...], g, m_ref[...], v_ref[...])

    cols = cols or c
    blk = pl.BlockSpec((None, rows, cols), lambda i, j, chip_ref: (0, i, j))
    return pl.pallas_call(
        body, name=name, out_shape=[_sds((1, r, c), F32)] * 4,
        grid_spec=pltpu.PrefetchScalarGridSpec(
            num_scalar_prefetch=1, grid=(r // rows, c // cols),
            in_specs=[pl.BlockSpec((None, rows, cols), lambda i, j, chip_ref: (chip_ref[0], i, j)),
                      pl.BlockSpec((3, rows, cols), lambda i, j, chip_ref: (0, i, j)), blk, blk, blk],
            out_specs=[blk] * 4),
        compiler_params=_params(("parallel", "parallel")),
    )(my_chip, sums, landed, w, m, v)


def _adamw_side(sums, landed, my_chip, w, m, v, steps):
    _, r, c = w.shape
    rows = r // steps
    assert rows * steps == r and rows % 16 == 0

    def fn(s, l, w_, m_, v_):
        g = s.astype(F32)
        for k in range(3):
            g = g + l[k].astype(F32)
        return [g, *_adamw_math(w_, g, m_, v_)]

    one = lambda t, s: (0, t, 0)
    return _Side(
        my_chip,
        [(sums, (None, rows, c), lambda t, s: (s[0], t, 0)), (landed, (3, rows, c), one)]
        + [(a, (None, rows, c), one) for a in (w, m, v)],
        [((1, r, c), F32, (None, rows, c), one)] * 4, fn)


SMALL_PLACES = (
    ("norm1_w", (1, D), ((0, 0, 1024), (1, 0, 1024))),
    ("norm2_w", (1, D), ((2, 0, 1024), (3, 0, 1024))),
    ("final_norm_w", (1, D), ((4, 0, 1024), (5, 0, 1024))),
    ("pool_scale", (1, PW), ((6, 0, 1024),)),
    ("gate_b", (1, KW), ((7, 0, KW),)),
    ("gla_norm_w", (1, DV), ((7, KW, DV),)),
    ("meta_tokens", (4, 1024), None),
    ("gate_w2", (1, 1024), ((12, 0, 1024),)),
)


def _adamw_small(g, w, m, v):
    n = len(SMALL_PLACES)

    def body(g_ref, w_ref, m_ref, v_ref, *refs):
        outs, buf = refs[:-1], refs[-1]
        gv = g_ref[...]
        for a, val in enumerate((gv, *_adamw_math(w_ref[...], gv, m_ref[...], v_ref[...]))):
            buf[a] = val
            for j, (_, _, pieces) in enumerate(SMALL_PLACES):
                o_ref = outs[a * n + j]
                if pieces is None:
                    o_ref[...] = buf[a, pl.ds(8, 4), :]
                else:
                    o_ref[...] = jnp.concatenate([buf[a, pl.ds(r, 1), pl.ds(l, k)] for r, l, k in pieces], axis=1)

    return pl.pallas_call(
        body, name="adamw_small", out_shape=[_sds(shape, F32) for _ in range(4) for _, shape, _ in SMALL_PLACES],
        scratch_shapes=[pltpu.VMEM((4,) + w.shape, F32)],
    )(g, w, m, v)


SMALL_REPL = (("norm1_w", D), ("norm2_w", D), ("final_norm_w", D), ("pool_scale", PW), ("gate_b", KW),
              ("gla_norm_w", DV))


def _pack_rows(vecs, rows):
    flat = jnp.concatenate([jnp.ravel(v) for v in vecs])
    return jnp.pad(flat, (0, rows * 1024 - flat.shape[0])).reshape(rows, 1024)


def kernel(x, meta_tokens, norm1_w, w_in, gate_w2, gate_b, gla_norm_w, pool_w, pool_scale, w_out, norm2_w, mlp_w1, mlp_w2, final_norm_w, loss_target, m_meta_tokens, m_norm1_w, m_w_in, m_gate_w2, m_gate_b, m_gla_norm_w, m_pool_w, m_pool_scale, m_w_out, m_norm2_w, m_mlp_w1, m_mlp_w2, m_final_norm_w, v_meta_tokens, v_norm1_w, v_w_in, v_gate_w2, v_gate_b, v_gla_norm_w, v_pool_w, v_pool_scale, v_w_out, v_norm2_w, v_mlp_w1, v_mlp_w2, v_final_norm_w):
    W = dict(meta_tokens=meta_tokens, norm1_w=norm1_w, w_in=w_in, gate_w2=gate_w2, gate_b=gate_b,
             gla_norm_w=gla_norm_w, pool_w=pool_w, pool_scale=pool_scale, w_out=w_out, norm2_w=norm2_w,
             mlp_w1=mlp_w1, mlp_w2=mlp_w2, final_norm_w=final_norm_w)
    Mo = dict(meta_tokens=m_meta_tokens, norm1_w=m_norm1_w, w_in=m_w_in, gate_w2=m_gate_w2, gate_b=m_gate_b,
              gla_norm_w=m_gla_norm_w, pool_w=m_pool_w, pool_scale=m_pool_scale, w_out=m_w_out, norm2_w=m_norm2_w,
              mlp_w1=m_mlp_w1, mlp_w2=m_mlp_w2, final_norm_w=m_final_norm_w)
    Vo = dict(meta_tokens=v_meta_tokens, norm1_w=v_norm1_w, w_in=v_w_in, gate_w2=v_gate_w2, gate_b=v_gate_b,
              gla_norm_w=v_gla_norm_w, pool_w=v_pool_w, pool_scale=v_pool_scale, w_out=v_w_out, norm2_w=v_norm2_w,
              mlp_w1=v_mlp_w1, mlp_w2=v_mlp_w2, final_norm_w=v_final_norm_w)

    ex = _Exchange(dict(small=_pack_rows([meta_tokens, gate_w2[0]], 8), w_in=w_in[0].T.astype(BF16)),
                   dict(w_out=w_out[0], pool_w=pool_w[0].reshape(4 * 32, GC), mlp_w1=mlp_w1[0],
                        **{"mlp_w2_%d" % q: mlp_w2[0][:, q * W2_COLS:(q + 1) * W2_COLS] for q in range(W2_PIECES)}))
    tr = lambda a: a[0].T[None]
    win_t, m_win_t, v_win_t = tr(w_in), tr(m_w_in), tr(v_w_in)
    ex.early = [win_t, m_win_t, v_win_t]
    ex.shards = {k: (W[k], Mo[k], Vo[k]) for k in ("w_out", "mlp_w1", "mlp_w2")}
    step = _layer_step(x[0], loss_target[0], ex, norm1_w, gate_b, gla_norm_w, pool_scale, norm2_w,
                       final_norm_w.reshape(1, D))
    grad_x = step["dx"][None]

    last = ex.update("mix", step["dx"])
    out = dict(ex.done)

    loss_part = 0.5 * jnp.sum(step["sq"]) / D
    to_all = _pack_rows([step[k] for k, _ in SMALL_REPL] + [loss_part], 8)
    cols = lambda g: g.reshape(g.shape[0], NDEV, -1).transpose(1, 0, 2).reshape(NDEV, -1, 1024)
    packed = jnp.concatenate([jnp.broadcast_to(to_all, (NDEV, 8, 1024)), cols(step["dhead"][PAD:]),
                              cols(step["gate_w2"][:RANK]), jnp.zeros((NDEV, 3, 1024), F32)], axis=1)
    red = _reduce_small(packed, "reduce_small", deps=[last])
    loss = red[7, 768]

    done = ex.grad_finish("w_in", red)
    poolw3 = lambda a: a.reshape(1, 4 * 32, GC)
    res = _adamw_landed(*done["pool_w"], ex.my_chip, poolw3(pool_w), poolw3(m_pool_w), poolw3(v_pool_w),
                        SHARD_ROWS["pool_w"], "adamw_pool_w")
    out["pool_w"] = [a.reshape(pool_w.shape) for a in res]
    res = _adamw_landed(*done["w_in"], ex.my_chip, win_t, m_win_t, v_win_t, D_IN // NDEV, "adamw_w_in", cols=256)
    out["w_in"] = [a[0].T[None] for a in res]

    def small_pack(P):
        return jnp.concatenate([_pack_rows([P[k] for k, _ in SMALL_REPL], 8),
                                _pack_rows([P["meta_tokens"], P["gate_w2"]], 8)], axis=0)

    res_small = _adamw_small(red.at[7, 768].set(0.0), small_pack(W), small_pack(Mo), small_pack(Vo))
    for j, (k, _, _) in enumerate(SMALL_PLACES):
        out[k] = [res_small[a * len(SMALL_PLACES) + j].reshape(W[k].shape) for a in range(4)]

    order = ["meta_tokens", "norm1_w", "w_in", "gate_w2", "gate_b", "gla_norm_w", "pool_w", "pool_scale", "w_out",
             "norm2_w", "mlp_w1", "mlp_w2", "final_norm_w"]
    return (loss, grad_x, *[out[k][0] for k in order], *[out[k][1] for k in order],
            *[out[k][2] for k in order], *[out[k][3] for k in order])


SHARD_ROWS = dict(w_out=256, mlp_w1=512, mlp_w2=256, pool_w=128)
SLOT_ROWS = D_IN // NDEV


def _w_in_pieces(s):
    lo, hi, out = s * SLOT_ROWS, (s + 1) * SLOT_ROWS, []
    for a, b, shift in ((0, C_GLR, 0), (C_GLR, C_GLR + RANK, PW), (C_GLR + RANK, D_IN, -RANK)):
        a, b = max(a, lo), min(b, hi)
        if a < b:
            out.append((a - lo, a + shift, b - a))
    return out


def _w_in_to_layer_order(gathered):
    def body(g_ref, o_ref):
        for s in range(NDEV):
            @pl.when(pl.program_id(0) == s)
            def _():
                for src, dst, n in _w_in_pieces(s):
                    o_ref[pl.ds(dst, n), :] = g_ref[pl.ds(src, n), :]

        @pl.when(pl.program_id(0) == 0)
        def _():
            o_ref[pl.ds(D_IN, D_INP - D_IN), :] = jnp.zeros((D_INP - D_IN, D), BF16)

    return pl.pallas_call(
        body, name="w_in_rows", grid=(NDEV,), out_shape=_sds((D_INP, D), BF16),
        in_specs=[pl.BlockSpec((None, SLOT_ROWS, D), lambda s: (s, 0, 0))],
        out_specs=pl.BlockSpec((D_INP, D), lambda s: (0, 0)), compiler_params=_params(("arbitrary",)),
    )(gathered)


def _w_in_grad_to_parts(g):
    def body(g_ref, o_ref):
        for s in range(NDEV):
            @pl.when(pl.program_id(0) == s)
            def _():
                for dst, src, n in _w_in_pieces(s):
                    o_ref[pl.ds(dst, n), :] = g_ref[pl.ds(src, n), :]

    return pl.pallas_call(
        body, name="w_in_grad_rows", grid=(NDEV,), out_shape=_sds((2, 4, SLOT_ROWS, D), BF16),
        in_specs=[pl.BlockSpec((D_INP, D), lambda s: (0, 0))],
        out_specs=pl.BlockSpec((None, None, SLOT_ROWS, D), lambda s: (s % 2, s // 2, 0, 0)),
        compiler_params=_params(("arbitrary",)),
    )(g)
C_GLR = 2 * KW + 2 * GW
W2_PIECES = 2
W2_COLS = D // W2_PIECES
GATHER_GROUPS = dict(small=("small",), w_in=("w_in",), mix=("w_out", "pool_w"), up=("mlp_w1",),
                     **{"down_%d" % q: ("mlp_w2_%d" % q,) for q in range(W2_PIECES)})
GRAD_GROUPS = dict(down=("mlp_w2",), up=("mlp_w1",), mix=("w_out",), w_in=("pool_w", "w_in"))
GRAD_IDS = dict(down=(1, 2), up=(3, 4), mix=(5, 6), w_in=(7, 8))


class _Exchange:
    def __init__(self, first, rest):
        head, token = _gather_start(list(first.values()), "gather_start_first", first["small"])
        token, later = lax.optimization_barrier((token, list(rest.values())))
        tail, self.started = _gather_start([v.astype(BF16) for v in later], "gather_start_rest", token)
        self.state = dict(zip(list(first) + list(rest), head + tail))
        self.my_c = lax.axis_index("c").astype(jnp.int32).reshape(1)
        self.my_chip = (2 * lax.axis_index("x") + lax.axis_index("y")).astype(jnp.int32).reshape(1)
        self.sibling, self.chips, self.done = {}, {}, {}

    def forward(self, group, after):
        ks = GATHER_GROUPS[group]
        fwd, token = _gather_forward([self.state[k] for k in ks], after, "gather_forward_" + group)
        self.state.update(zip(ks, fwd))
        return token

    def weights(self, group, after):
        ks = GATHER_GROUPS[group]
        g = dict(zip(ks, _gather_finish([self.state[k] for k in ks], after, "gather_finish_" + group)))
        if group == "w_in":
            return _w_in_to_layer_order(g["w_in"])
        if group == "small":
            return g["small"]
        if group == "mix":
            return (g["w_out"].reshape(D, D),
                    g["pool_w"].reshape(NDEV, 4, 32, GC).transpose(1, 0, 2, 3).reshape(4, GC, GC))
        return g["mlp_w1"] if group == "up" else g[ks[0]].reshape(DFF, W2_COLS)

    def grad(self, group, grads):
        parts = dict(grads)
        if group == "w_in":
            parts["w_in"] = _w_in_grad_to_parts(parts["w_in"])
            parts["pool_w"] = (parts["pool_w"].astype(BF16).reshape(4, 4, 2, 32, GC).transpose(2, 1, 0, 3, 4)
                               .reshape(2, 4, 4 * 32, GC))
        ks = GRAD_GROUPS[group]
        started, token = _to_sibling_start([parts[k] for k in ks], "grad_sibling_start_" + group,
                                           GRAD_IDS[group][0])
        self.sibling[group] = started
        return token

    def grad_mid(self, group, after):
        ks = GRAD_GROUPS[group]
        both = _to_sibling_finish(self.sibling[group], after, "grad_sibling_finish_" + group)
        tile = lambda k, p: (p.shape[2], 512) if k == "w_in" else (SHARD_ROWS[k], p.shape[3])
        sums = [_chip_sum(p, s, self.my_c, tile(k, p), "chip_sum_" + k) for k, (p, s) in zip(ks, both)]
        self.chips[group], token = _to_chips_start(sums, "grad_chips_start_" + group, GRAD_IDS[group][1])
        return token

    def chip_sum_side(self, group, after, steps):
        (parts, from_sibling), = _to_sibling_finish(self.sibling[group], after, "grad_sibling_finish_" + group)
        return _chip_sum_side(parts, from_sibling, self.my_c, steps)

    def adamw_side(self, group, after, steps):
        (k, (sums, landed)), = self.grad_finish(group, after).items()
        self.hosted = k
        return _adamw_side(sums, landed, self.my_chip, *self.shards[k], steps)

    def update(self, group, after):
        for k, (sums, landed) in self.grad_finish(group, after).items():
            self.done[k] = _adamw_landed(sums, landed, self.my_chip, *self.shards[k], SHARD_ROWS[k], "adamw_" + k)
            after = self.done[k][1]
        return after

    def grad_chips(self, group, sums):
        self.chips[group], token = _to_chips_start(sums, "grad_chips_start_" + group, GRAD_IDS[group][1])
        return token

    def grad_finish(self, group, after):
        done = _to_chips_finish(self.chips[group], after, "grad_chips_finish_" + group)
        return dict(zip(GRAD_GROUPS[group], done))


def _layer_step(x, target, ex, norm1_w, gate_b, gla_norm_w, pool_scale, norm2_w, final_norm_w):
    small = ex.weights("small", ex.forward("small", ex.started))
    meta_full = small[:, 0:4].reshape(NDEV, N_META, D // NDEV).transpose(1, 0, 2).reshape(N_META, D)
    gw2_full = small[:, 4].reshape(NDEV, RANK, KW // NDEV).transpose(1, 0, 2).reshape(RANK, KW)
    gw2p = jnp.pad(gw2_full, ((0, 128 - RANK), (0, 0))).astype(BF16)
    h0 = jnp.concatenate([jnp.zeros((PAD, D), F32), meta_full, x], axis=0)
    u1 = _rmsnorm_fwd(h0, norm1_w, "rmsnorm1", deps=ex.early)
    win_p = ex.weights("w_in", ex.forward("w_in", u1))
    proj = _matmul(u1, win_p, mode="nt", tm=1056, tn=1408, tk=2048, name="proj")
    tok = ex.forward("mix", proj)
    mixed, o_saved, st_saved = _gla_fwd(proj, gw2p, gate_b, gla_norm_w, deps=[tok])
    wout_f, poolw_f = ex.weights("mix", mixed)
    mixed = _pool_fwd(proj, poolw_f, pool_scale, mixed)
    h1 = _matmul(mixed, wout_f, mode="nn", tm=1056, tn=1024, tk=2048, name="mix_out", epi="add", extra=h0)
    tok = ex.forward("up", h1)
    u2 = _rmsnorm_fwd(h1, norm2_w, "rmsnorm2", deps=[tok])
    w1_g = ex.weights("up", u2)
    z, act = _matmul(u2, w1_g, mode="nn", tm=1056, tn=1024, tk=2048, name="mlp_up", epi="relu2", b_slots=True)
    w2, h2 = [], None
    for q in range(W2_PIECES):
        w2.append(ex.weights("down_%d" % q, ex.forward("down_%d" % q, act if h2 is None else h2)))
        h2 = _matmul(act, w2[q], mode="nn", tm=1056, tn=W2_COLS, tk=2048, name="mlp_down_%d" % q, epi="add",
                     extra=h1, n_total=D, col_block=q, into=h2)
    dh2, dh2b, sq, g_fnw = _loss_head(h2, final_norm_w, target)

    g_w2 = _matmul(act, dh2b, mode="tn", tm=512, tn=2048, tk=LP, name="d_mlp_w2", out_dtype=BF16, out_slots="rows")
    tok = ex.grad("down", dict(mlp_w2=g_w2))
    dz = _matmul(dh2b, tuple(w2), mode="nt", tm=1056, tn=1024, tk=2048, name="d_act", out_dtype=BF16, epi="dz",
                 extra=z, deps=[tok])
    hosted = lambda res, side: res if side is not None else (res, None)
    side = ex.chip_sum_side("down", dz, 16)
    g_w1, sums = hosted(_matmul(u2, dz, mode="tn", tm=1024, tn=1024, tk=LP, name="d_mlp_w1", out_dtype=BF16,
                                out_slots="cols", side=side), side)
    toks = [ex.grad_chips("down", sums), ex.grad("up", dict(mlp_w1=g_w1))]
    du2 = _matmul(dz, w1_g, mode="nt", tm=1056, tn=1024, tk=2048, name="d_u2", b_slots=True, deps=toks)
    side = ex.chip_sum_side("up", du2, 8)
    dh1, dh1b, g_n2 = _rmsnorm_bwd(h1, norm2_w, du2, dh2, "rmsnorm2_bwd")
    g_wout, sums = hosted(_matmul(mixed, dh1b, mode="tn", tm=256, tn=2048, tk=LP, name="d_w_out", out_dtype=BF16,
                                  out_slots="rows", side=side), side)
    toks = [ex.grad_chips("up", sums), ex.grad("mix", dict(w_out=g_wout))]
    dmixed = _matmul(dh1b, wout_f, mode="nt", tm=1056, tn=1024, tk=2048, name="d_mixed", deps=toks)
    tok = ex.grad_mid("mix", dmixed)
    dproj, dglr, g_gnw, g_gb, g_gw2 = _gla_bwd(proj, dmixed, o_saved, st_saved, gw2p, gate_b, gla_norm_w, deps=[tok])
    dproj, g_poolw, g_psc = _pool_bwd(proj, dmixed, poolw_f, pool_scale, dproj)
    dproj = lax.dynamic_update_slice(dproj, dglr, (0, GLR_BLK * 128))
    g_win_p = _matmul(dproj, u1, mode="tn", tm=384, tn=2048, tk=LP, name="d_w_in", out_dtype=BF16)
    tok = ex.grad("w_in", dict(pool_w=g_poolw, w_in=g_win_p))
    tok = ex.grad_mid("w_in", ex.update("down", tok))
    side = ex.adamw_side("up", tok, 16)
    du1, ex.done[ex.hosted] = hosted(_matmul(dproj, win_p, mode="nn", tm=1056, tn=256, tk=D_INP, name="d_u1",
                                             deps=[tok], side=side), side)
    dx, dhead, g_n1 = _rmsnorm_bwd_input(h0, norm1_w, du1, dh1, "rmsnorm1_bwd")
    return dict(dx=dx, dhead=dhead, sq=sq, gate_w2=g_gw2, norm1_w=g_n1, norm2_w=g_n2, final_norm_w=g_fnw, pool_scale=g_psc,
                gate_b=g_gb, gla_norm_w=g_gnw)
```

```python
import jax
import jax.numpy as jnp
from jax import lax
from jax.experimental import pallas as pl
from jax.experimental.pallas import tpu as pltpu

F32, BF16 = jnp.float32, jnp.bfloat16
MESH = pl.DeviceIdType.MESH

NDEV = 8
D = 2048
SEQ = 2048
N_META = 16
CHUNK = 64
PAD = (-N_META) % CHUNK
ROW_X = PAD + N_META
LP = ROW_X + SEQ
NCH = LP // CHUNK
H = 4
DK = 128
DV = 256
KW = H * DK
GW = H * DV
PW = 1024
RANK = 16
TAU = 16.0
WINDOWS = (2, 4, 8, 16)
GC = 256
DFF = 4 * D
EPS = 1e-6
D_IN = 2 * KW + 2 * GW + RANK + PW
D_INP = 4224
GLR_BLK = (2 * KW + 2 * GW + PW) // 128
POOL_BLK = (2 * KW + 2 * GW) // GC
LR, B1, B2, AEPS, WD, STEP = 0.001, 0.9, 0.999, 1e-08, 0.01, 10
VMEM_LIMIT = 48 * 1024 * 1024
CPS = 3


def _params(sem=None):
    return pltpu.CompilerParams(dimension_semantics=sem, vmem_limit_bytes=VMEM_LIMIT)


def _sds(shape, dtype):
    return jax.ShapeDtypeStruct(shape, dtype)


def _me():
    return lax.axis_index("x"), lax.axis_index("y"), lax.axis_index("c")


def _peer(j):
    x, y, c = _me()
    return (x ^ ((j >> 2) & 1), y ^ ((j >> 1) & 1), c ^ (j & 1))


def _slot(dev):
    return 4 * dev[0] + 2 * dev[1] + dev[2]


HBM_SPEC = pl.BlockSpec(memory_space=pltpu.HBM)
SEM_SPEC = pl.BlockSpec(memory_space=pltpu.SEMAPHORE)
ANY_SPEC = pl.BlockSpec(memory_space=pl.ANY)
EFFECT = pltpu.SideEffectType.DATAFLOW_SIDE_EFFECTING
SIBLING = 1
OTHER_CHIPS = (2, 4, 6)


def _in_hbm(a):
    return pltpu.with_memory_space_constraint(a, pltpu.HBM)


def _chip(dev):
    return 2 * dev[0] + dev[1]


def _rcopy(src, dst, send_sem, recv_sem, to):
    return pltpu.make_async_remote_copy(src_ref=src, dst_ref=dst, send_sem=send_sem, recv_sem=recv_sem,
                                        device_id=to, device_id_type=MESH)


def _split_call(body, name, ins, in_specs, out_shape, out_specs, aliases, scratch=(), collective_id=None):
    n = len(ins) + len(out_shape)

    def with_token(*refs):
        body(*refs[:n], *refs[n + 1:])
        refs[n][...] = jnp.zeros_like(refs[n])

    return pl.pallas_call(
        with_token, name=name, in_specs=in_specs, out_shape=list(out_shape) + [_sds((8, 128), F32)],
        out_specs=list(out_specs) + [pl.BlockSpec(memory_space=pltpu.VMEM)],
        input_output_aliases=aliases, scratch_shapes=list(scratch),
        compiler_params=pltpu.CompilerParams(has_side_effects=EFFECT, collective_id=collective_id),
    )(*ins)


def _handshake(relations):
    barrier = pltpu.get_barrier_semaphore()
    for rel in relations:
        pl.semaphore_signal(barrier, inc=1, device_id=_peer(rel), device_id_type=MESH)
    pl.semaphore_wait(barrier, len(relations))


def _after(body, n_in, deps):
    deps = [d for d in deps if d is not None]
    if not deps:
        return body, [], []
    return (lambda *refs: body(*refs[:n_in], *refs[n_in + len(deps):])), deps, [ANY_SPEC] * len(deps)


def _gather_start(shards, name, after):
    n = len(shards)
    me = _slot(_me())
    lands = [lax.dynamic_update_slice(lax.empty((NDEV,) + s.shape, s.dtype), s[None], (me, 0, 0)) for s in shards]

    def body(*refs):
        src, land = refs[:n], refs[n:2 * n]
        outs = refs[2 * n + 1:]
        for i in range(n):
            send_sems, recv_sems = outs[4 * i], outs[4 * i + 1]
            for k, rel in enumerate((SIBLING,) + OTHER_CHIPS):
                _rcopy(src[i], land[i].at[_slot(_me())], send_sems.at[k], recv_sems.at[k], _peer(rel)).start()

    out_shape, out_specs, aliases = [], [], {}
    for i, s in enumerate(shards):
        out_shape += [pltpu.SemaphoreType.DMA((4,)), pltpu.SemaphoreType.DMA((4,)), pltpu.HBM(s.shape, s.dtype),
                      pltpu.HBM((NDEV,) + s.shape, s.dtype)]
        out_specs += [SEM_SPEC, SEM_SPEC, HBM_SPEC, HBM_SPEC]
        aliases[i] = 4 * i + 2
        aliases[n + i] = 4 * i + 3
    res = _split_call(body, name, [_in_hbm(s) for s in shards] + [_in_hbm(l) for l in lands] + [after],
                      [HBM_SPEC] * (2 * n) + [ANY_SPEC], out_shape, out_specs, aliases)
    return [tuple(res[4 * i:4 * i + 4]) for i in range(n)], res[-1]


def _gather_forward(started, after, name):
    n = len(started)

    def body(*refs):
        land, recv1 = refs[:n], refs[n:2 * n]
        outs = refs[2 * n + 1:]
        for i in range(n):
            send2, recv2 = outs[3 * i + 1], outs[3 * i + 2]
            for k, rel in enumerate(OTHER_CHIPS):
                blk = land[i].at[_slot(_peer(rel))]
                _rcopy(blk, blk, send2.at[k], recv1[i].at[1 + k], _peer(rel)).wait_recv()
                _rcopy(blk, blk, send2.at[k], recv2.at[k], _peer(SIBLING)).start()

    ins = [_in_hbm(st[3]) for st in started] + [st[1] for st in started] + [after]
    out_shape, out_specs, aliases = [], [], {}
    for i, st in enumerate(started):
        out_shape += [pltpu.HBM(st[3].shape, st[3].dtype), pltpu.SemaphoreType.DMA((3,)), pltpu.SemaphoreType.DMA((3,))]
        out_specs += [HBM_SPEC, SEM_SPEC, SEM_SPEC]
        aliases[i] = 3 * i
    res = _split_call(body, name, ins, [HBM_SPEC] * n + [SEM_SPEC] * n + [ANY_SPEC], out_shape, out_specs, aliases)
    return [(st[0], st[1], st[2], res[3 * i], res[3 * i + 1], res[3 * i + 2]) for i, st in enumerate(started)], res[-1]


def _gather_finish(forwarded, after, name):
    n = len(forwarded)

    def body(*refs):
        for i in range(n):
            send1, recv1, src, land, send2, recv2 = refs[6 * i:6 * i + 6]
            me = _slot(_me())
            sib = _slot(_peer(SIBLING))
            for k, rel in enumerate((SIBLING,) + OTHER_CHIPS):
                _rcopy(src, land.at[me], send1.at[k], recv1.at[k], _peer(rel)).wait_send()
            _rcopy(src, land.at[sib], send1.at[0], recv1.at[0], _peer(SIBLING)).wait_recv()
            for k, rel in enumerate(OTHER_CHIPS):
                mine, theirs = land.at[_slot(_peer(rel))], land.at[_slot(_peer(rel ^ SIBLING))]
                _rcopy(mine, mine, send2.at[k], recv2.at[k], _peer(SIBLING)).wait_send()
                _rcopy(theirs, theirs, send2.at[k], recv2.at[k], _peer(SIBLING)).wait_recv()

    ins, in_specs, out_shape, aliases = [], [], [], {}
    for i, f in enumerate(forwarded):
        ins += [f[0], f[1], _in_hbm(f[2]), _in_hbm(f[3]), f[4], f[5]]
        in_specs += [SEM_SPEC, SEM_SPEC, HBM_SPEC, HBM_SPEC, SEM_SPEC, SEM_SPEC]
        out_shape.append(pltpu.HBM(f[3].shape, f[3].dtype))
        aliases[6 * i + 3] = i
    res = _split_call(body, name, ins + [after], in_specs + [ANY_SPEC], out_shape, [HBM_SPEC] * n, aliases)
    return list(res[:-1])


def _to_sibling_start(parts, name, collective_id):
    n = len(parts)
    lands = [lax.empty(p.shape[1:], p.dtype) for p in parts]

    def body(*refs):
        _handshake((SIBLING,))
        src, land = refs[:n], refs[n:2 * n]
        outs = refs[2 * n:]
        other = 1 - lax.axis_index("c")
        for i in range(n):
            _rcopy(src[i].at[other], land[i], outs[4 * i], outs[4 * i + 1], _peer(SIBLING)).start()

    out_shape, out_specs, aliases = [], [], {}
    for i, p in enumerate(parts):
        out_shape += [pltpu.SemaphoreType.DMA(()), pltpu.SemaphoreType.DMA(()), pltpu.HBM(p.shape, p.dtype),
                      pltpu.HBM(p.shape[1:], p.dtype)]
        out_specs += [SEM_SPEC, SEM_SPEC, HBM_SPEC, HBM_SPEC]
        aliases[i] = 4 * i + 2
        aliases[n + i] = 4 * i + 3
    res = _split_call(body, name, [_in_hbm(p) for p in parts] + [_in_hbm(l) for l in lands], [HBM_SPEC] * (2 * n),
                      out_shape, out_specs, aliases, collective_id=collective_id)
    return [tuple(res[4 * i:4 * i + 4]) for i in range(n)], res[-1]


def _to_sibling_finish(started, after, name):
    n = len(started)

    def body(*refs):
        for i in range(n):
            send, recv, src, land = refs[4 * i:4 * i + 4]
            cp = _rcopy(src.at[0], land, send, recv, _peer(SIBLING))
            cp.wait_send()
            cp.wait_recv()

    ins, in_specs, out_shape, aliases = [], [], [], {}
    for i, st in enumerate(started):
        ins += [st[0], st[1], _in_hbm(st[2]), _in_hbm(st[3])]
        in_specs += [SEM_SPEC, SEM_SPEC, HBM_SPEC, HBM_SPEC]
        out_shape += [pltpu.HBM(st[2].shape, st[2].dtype), pltpu.HBM(st[3].shape, st[3].dtype)]
        aliases[4 * i + 2] = 2 * i
        aliases[4 * i + 3] = 2 * i + 1
    res = _split_call(body, name, ins + [after], in_specs + [ANY_SPEC], out_shape, [HBM_SPEC] * (2 * n), aliases)
    return [(res[2 * i], res[2 * i + 1]) for i in range(n)]


def _chip_sum(parts, from_sibling, my_c, tile, name):
    _, _, r, c = parts.shape
    tr, tc = tile

    def body(c_ref, p_ref, s_ref, o_ref):
        o_ref[...] = (p_ref[...].astype(F32) + s_ref[...].astype(F32)).astype(o_ref.dtype)

    blk = pl.BlockSpec((4, tr, tc), lambda i, j, c_ref: (0, i, j))
    return pl.pallas_call(
        body, name=name, out_shape=_sds((4, r, c), parts.dtype),
        grid_spec=pltpu.PrefetchScalarGridSpec(
            num_scalar_prefetch=1, grid=(r // tr, c // tc),
            in_specs=[pl.BlockSpec((None, 4, tr, tc), lambda i, j, c_ref: (c_ref[0], 0, i, j)), blk], out_specs=blk),
        compiler_params=_params(("parallel", "parallel")),
    )(my_c, parts, from_sibling)


class _Side:
    def __init__(self, scalar, ins, outs, fn):
        self.scalar, self.ins, self.outs, self.fn = scalar, ins, outs, fn


def _chip_sum_side(parts, from_sibling, my_c, steps):
    _, _, r, c = parts.shape
    rows = r // steps
    assert rows * steps == r and rows % 16 == 0
    return _Side(
        my_c,
        [(parts, (None, 4, rows, c), lambda t, s: (s[0], 0, t, 0)), (from_sibling, (4, rows, c), lambda t, s: (0, t, 0))],
        [((4, r, c), parts.dtype, (4, rows, c), lambda t, s: (0, t, 0))],
        lambda p, q: [(p.astype(F32) + q.astype(F32)).astype(parts.dtype)])


def _to_chips_start(sums, name, collective_id):
    n = len(sums)
    lands = [lax.empty((3,) + s.shape[1:], s.dtype) for s in sums]

    def body(*refs):
        _handshake(OTHER_CHIPS)
        src, land = refs[:n], refs[n:2 * n]
        outs = refs[2 * n:]
        for i in range(n):
            for k, rel in enumerate(OTHER_CHIPS):
                to = _peer(rel)
                _rcopy(src[i].at[_chip(to)], land[i].at[k], outs[4 * i].at[k], outs[4 * i + 1].at[k], to).start()

    out_shape, out_specs, aliases = [], [], {}
    for i, s in enumerate(sums):
        out_shape += [pltpu.SemaphoreType.DMA((3,)), pltpu.SemaphoreType.DMA((3,)), pltpu.HBM(s.shape, s.dtype),
                      pltpu.HBM((3,) + s.shape[1:], s.dtype)]
        out_specs += [SEM_SPEC, SEM_SPEC, HBM_SPEC, HBM_SPEC]
        aliases[i] = 4 * i + 2
        aliases[n + i] = 4 * i + 3
    res = _split_call(body, name, [_in_hbm(s) for s in sums] + [_in_hbm(l) for l in lands], [HBM_SPEC] * (2 * n),
                      out_shape, out_specs, aliases, collective_id=collective_id)
    return [tuple(res[4 * i:4 * i + 4]) for i in range(n)], res[-1]


def _to_chips_finish(started, after, name):
    n = len(started)

    def body(*refs):
        for i in range(n):
            send, recv, src, land = refs[4 * i:4 * i + 4]
            for k, rel in enumerate(OTHER_CHIPS):
                cp = _rcopy(src.at[0], land.at[k], send.at[k], recv.at[k], _peer(rel))
                cp.wait_send()
                cp.wait_recv()

    ins, in_specs, out_shape, aliases = [], [], [], {}
    for i, st in enumerate(started):
        ins += [st[0], st[1], _in_hbm(st[2]), _in_hbm(st[3])]
        in_specs += [SEM_SPEC, SEM_SPEC, HBM_SPEC, HBM_SPEC]
        out_shape += [pltpu.HBM(st[2].shape, st[2].dtype), pltpu.HBM(st[3].shape, st[3].dtype)]
        aliases[4 * i + 2] = 2 * i
        aliases[4 * i + 3] = 2 * i + 1
    res = _split_call(body, name, ins + [after], in_specs + [ANY_SPEC], out_shape, [HBM_SPEC] * (2 * n), aliases)
    return [(res[2 * i], res[2 * i + 1]) for i in range(n)]


def _reduce_small(v, name, deps=()):
    _, r, c = v.shape

    def body(v_ref, o_ref, land, send_sems, recv_sems):
        me = _slot(_me())
        copies = []
        for j in range(1, NDEV):
            to = _peer(j)
            cp = _rcopy(v_ref.at[_slot(to)], land.at[me], send_sems.at[j - 1], recv_sems.at[j - 1], to)
            cp.start()
            copies.append(cp)
        land[me] = v_ref[me]
        for cp in copies:
            cp.wait()
        acc = land[0]
        for k in range(1, NDEV):
            acc = acc + land[k]
        o_ref[...] = acc

    vm = pl.BlockSpec(memory_space=pltpu.VMEM)
    body, dep_ins, dep_specs = _after(body, 1, deps)
    return pl.pallas_call(
        body, name=name, out_shape=_sds((r, c), F32), in_specs=[vm] + dep_specs, out_specs=vm,
        scratch_shapes=[pltpu.VMEM((NDEV, r, c), F32), pltpu.SemaphoreType.DMA((NDEV - 1,)),
                        pltpu.SemaphoreType.DMA((NDEV - 1,))],
        compiler_params=_params(),
    )(v, *dep_ins)


def _matmul(a, b, *, mode, tm, tn, tk, name, out_dtype=F32, epi=None, extra=None, b_slots=False, out_slots=False,
            deps=(), col_block=0, into=None, n_total=None, side=None):
    b_pair = b if isinstance(b, tuple) else None
    if b_pair:
        assert mode == "nt" and tk == len(b) * b[0].shape[1] == a.shape[1] and not b_slots
        b = b[0]
    slot_w = b.shape[-1] if b_slots else None
    if mode == "nn":
        M, K = a.shape
        N = NDEV * slot_w if b_slots else b.shape[1]
    elif mode == "tn":
        K, M = a.shape
        N = b.shape[1]
    else:
        M, K = a.shape
        N = b.shape[-2]
        if b_slots:
            assert K == NDEV * slot_w and tk % slot_w == 0
    if mode == "nn" and b_slots:
        assert tn == slot_w
    if out_slots == "cols":
        assert tn * NDEV == N
    if out_slots == "rows":
        assert (M // NDEV) % tm == 0
    assert M % tm == 0 and N % tn == 0 and K % tk == 0, (name, M, N, K, tm, tn, tk)
    nk = K // tk
    dims = {"nn": ((1,), (0,)), "tn": ((0,), (0,)), "nt": ((1,), (1,))}[mode]

    if mode == "tn":
        a_spec = pl.BlockSpec((tk, tm), lambda i, j, k: (k, i))
    else:
        a_spec = pl.BlockSpec((tm, tk), lambda i, j, k: (i, k))
    if b_pair:
        b_spec = pl.BlockSpec((tn, tk // len(b_pair)), lambda i, j, k: (j, 0))
    elif mode == "nt":
        b_spec = (pl.BlockSpec((tk // slot_w, tn, slot_w), lambda i, j, k: (k, j, 0)) if b_slots
                  else pl.BlockSpec((tn, tk), lambda i, j, k: (j, k)))
    else:
        b_spec = (pl.BlockSpec((None, tk, tn), lambda i, j, k: (j, k, 0)) if b_slots
                  else pl.BlockSpec((tk, tn), lambda i, j, k: (k, j)))
    tile = pl.BlockSpec((tm, tn), lambda i, j, k: (i, j + col_block))
    if out_slots == "cols":
        out_spec = pl.BlockSpec((None, None, tm, tn), lambda i, j, k: (j % 2, j // 2, i, 0))
        out_shape = _sds((2, 4, M, tn), out_dtype)
    elif out_slots == "rows":
        per = M // NDEV // tm
        out_spec = pl.BlockSpec((None, None, tm, tn), lambda i, j, k: ((i // per) % 2, (i // per) // 2, i % per, j))
        out_shape = _sds((2, 4, M // NDEV, N), out_dtype)
    else:
        out_spec, out_shape = tile, _sds((M, n_total or N), out_dtype)
    ins, in_specs = [a, b], [a_spec, b_spec]
    if b_pair:
        ins += list(b_pair[1:])
        in_specs += [b_spec] * (len(b_pair) - 1)
    n_b = len(ins) - 1
    if epi in ("add", "dz"):
        ins.append(extra)
        in_specs.append(tile)
    aliases = {}
    if into is not None:
        aliases[len(ins)] = 0
        ins.append(into)
        in_specs.append(ANY_SPEC)
    if epi == "relu2":
        out_specs, out_shapes = [tile, tile], [_sds((M, N), F32), _sds((M, N), BF16)]
    else:
        out_specs, out_shapes = out_spec, out_shape
    n_in = len(ins)

    def body(*refs):
        outs = refs[n_in:-1] if nk > 1 else refs[n_in:]
        extra_ref = refs[1 + n_b]

        def finish(p):
            if epi is None:
                outs[0][...] = p.astype(out_dtype)
            elif epi == "add":
                outs[0][...] = (p + extra_ref[...]).astype(out_dtype)
            elif epi == "relu2":
                outs[0][...] = p
                rz = jnp.maximum(p, 0.0)
                outs[1][...] = (rz * rz).astype(BF16)
            else:
                outs[0][...] = (p * (2.0 * jnp.maximum(extra_ref[...], 0.0))).astype(out_dtype)

        def product():
            av = refs[0][...].astype(BF16)
            if b_pair:
                w = tk // len(b_pair)
                return sum(lax.dot_general(av[:, s * w:(s + 1) * w], refs[1 + s][...], (dims, ((), ())),
                                           preferred_element_type=F32) for s in range(len(b_pair)))
            if mode == "nt" and b_slots:
                return sum(lax.dot_general(av[:, s * slot_w:(s + 1) * slot_w], refs[1][s], (dims, ((), ())),
                                           preferred_element_type=F32) for s in range(tk // slot_w))
            return lax.dot_general(av, refs[1][...].astype(BF16), (dims, ((), ())), preferred_element_type=F32)

        if nk == 1:
            finish(product())
            return
        acc = refs[-1]
        k = pl.program_id(2)

        @pl.when(k == 0)
        def _():
            acc[...] = jnp.zeros_like(acc)

        acc[...] += product()

        @pl.when(k == nk - 1)
        def _():
            finish(acc[...])

    grid = (M // tm, N // tn, nk)
    scratch = [pltpu.VMEM((tm, tn), F32)] if nk > 1 else []
    if side is None:
        body, dep_ins, dep_specs = _after(body, n_in, deps)
        return pl.pallas_call(
            body, name=name, grid=grid,
            in_specs=in_specs + dep_specs, out_specs=out_specs, out_shape=out_shapes, input_output_aliases=aliases,
            scratch_shapes=scratch, compiler_params=_params(("parallel", "parallel", "arbitrary")),
        )(*ins, *dep_ins)

    deps = [d for d in deps if d is not None]
    step = lambda i, j, k: (i * grid[1] + j) * grid[2] + k
    host = lambda spec: (spec if spec.block_shape is None else
                         pl.BlockSpec(spec.block_shape, lambda i, j, k, s, f=spec.index_map: f(i, j, k)))
    cut = lambda blk, f: pl.BlockSpec(blk, lambda i, j, k, s: f(step(i, j, k), s))
    host_out_specs = list(out_specs) if isinstance(out_specs, list) else [out_specs]
    host_out_shapes = list(out_shapes) if isinstance(out_shapes, list) else [out_shapes]
    n_dep, n_si, n_ho, n_so = len(deps), len(side.ins), len(host_out_specs), len(side.outs)

    def with_side(*refs):
        rest = refs[1:]
        side_in = rest[n_in + n_dep:n_in + n_dep + n_si]
        outs_all = rest[n_in + n_dep + n_si:]
        body(*rest[:n_in], *outs_all[:n_ho], *outs_all[n_ho + n_so:])
        for o_ref, val in zip(outs_all[n_ho:n_ho + n_so], side.fn(*[r[...] for r in side_in])):
            o_ref[...] = val

    res = pl.pallas_call(
        with_side, name=name, input_output_aliases={k + 1: v for k, v in aliases.items()},
        out_shape=host_out_shapes + [_sds(shape, dt) for shape, dt, _, _ in side.outs],
        grid_spec=pltpu.PrefetchScalarGridSpec(
            num_scalar_prefetch=1, grid=grid,
            in_specs=[host(s) for s in in_specs] + [ANY_SPEC] * n_dep + [cut(blk, f) for _, blk, f in side.ins],
            out_specs=[host(s) for s in host_out_specs] + [cut(blk, f) for _, _, blk, f in side.outs],
            scratch_shapes=scratch),
        compiler_params=_params(("parallel", "parallel", "arbitrary")),
    )(side.scalar, *ins, *deps, *[arr for arr, _, _ in side.ins])
    host_res = res[0] if n_ho == 1 else tuple(res[:n_ho])
    return host_res, list(res[n_ho:])


ROWS = 352


def _rmsnorm_fwd(h, w, name, deps=()):
    def body(h_ref, w_ref, u_ref):
        x = h_ref[...]
        rstd = lax.rsqrt(jnp.mean(x * x, axis=-1, keepdims=True) + EPS)
        u_ref[...] = (x * rstd * w_ref[...]).astype(BF16)

    row = pl.BlockSpec((ROWS, D), lambda i: (i, 0))
    body, dep_ins, dep_specs = _after(body, 2, deps)
    return pl.pallas_call(
        body, name=name, grid=(LP // ROWS,), in_specs=[row, pl.BlockSpec((1, D), lambda i: (0, 0))] + dep_specs,
        out_specs=row, out_shape=_sds((LP, D), BF16), compiler_params=_params(("parallel",)),
    )(h, w, *dep_ins)


TOKEN_ROWS = 512


def _rmsnorm_bwd_input(h, w, du, dres, name, deps=()):
    def math(h_ref, w_ref, du_ref, dres_ref):
        x = h_ref[...]
        rstd = lax.rsqrt(jnp.mean(x * x, axis=-1, keepdims=True) + EPS)
        xhat = x * rstd
        dy = du_ref[...]
        dxh = dy * w_ref[...]
        dh = dres_ref[...] + rstd * (dxh - xhat * jnp.mean(dxh * xhat, axis=-1, keepdims=True))
        return dh, jnp.sum(dy * xhat, axis=0, keepdims=True)

    def body(h_ref, w_ref, du_ref, dres_ref, hh_ref, duh_ref, dresh_ref, dx_ref, dhead_ref, gw_ref):
        dx_ref[...], part = math(h_ref, w_ref, du_ref, dres_ref)

        @pl.when(pl.program_id(0) == 0)
        def _():
            dhead_ref[...], head = math(hh_ref, w_ref, duh_ref, dresh_ref)
            gw_ref[...] = part + head

        @pl.when(pl.program_id(0) > 0)
        def _():
            gw_ref[...] += part

    rows = pl.BlockSpec((pl.Element(TOKEN_ROWS), pl.Element(D)),
                        lambda i: (pl.multiple_of(ROW_X + TOKEN_ROWS * i, 8), 0))
    head = pl.BlockSpec((ROW_X, D), lambda i: (0, 0))
    vec = pl.BlockSpec((1, D), lambda i: (0, 0))
    body, dep_ins, dep_specs = _after(body, 7, deps)
    return pl.pallas_call(
        body, name=name, grid=(SEQ // TOKEN_ROWS,),
        in_specs=[rows, vec, rows, rows, head, head, head] + dep_specs,
        out_specs=[pl.BlockSpec((TOKEN_ROWS, D), lambda i: (i, 0)), head, vec],
        out_shape=[_sds((SEQ, D), F32), _sds((ROW_X, D), F32), _sds((1, D), F32)],
        compiler_params=_params(("arbitrary",)),
    )(h, w, du, dres, h, du, dres, *dep_ins)


def _rmsnorm_bwd(h, w, du, dres, name, deps=()):
    def body(h_ref, w_ref, du_ref, dres_ref, dh_ref, dhb_ref, gw_ref):
        x = h_ref[...]
        rstd = lax.rsqrt(jnp.mean(x * x, axis=-1, keepdims=True) + EPS)
        xhat = x * rstd
        dy = du_ref[...]
        dxh = dy * w_ref[...]
        dh = dres_ref[...] + rstd * (dxh - xhat * jnp.mean(dxh * xhat, axis=-1, keepdims=True))
        dh_ref[...] = dh
        dhb_ref[...] = dh.astype(BF16)
        part = jnp.sum(dy * xhat, axis=0, keepdims=True)

        @pl.when(pl.program_id(0) == 0)
        def _():
            gw_ref[...] = part

        @pl.when(pl.program_id(0) > 0)
        def _():
            gw_ref[...] += part

    row = pl.BlockSpec((ROWS, D), lambda i: (i, 0))
    vec = pl.BlockSpec((1, D), lambda i: (0, 0))
    body, dep_ins, dep_specs = _after(body, 4, deps)
    return pl.pallas_call(
        body, name=name, grid=(LP // ROWS,), in_specs=[row, vec, row, row] + dep_specs, out_specs=[row, row, vec],
        out_shape=[_sds((LP, D), F32), _sds((LP, D), BF16), _sds((1, D), F32)],
        compiler_params=_params(("arbitrary",)),
    )(h, w, du, dres, *dep_ins)


def _loss_head(h2, wf, target):
    def body(h_ref, w_ref, t_ref, dh_ref, dhb_ref, sq_ref, gw_ref):
        i = pl.program_id(0)

        @pl.when(i == 0)
        def _():
            sq_ref[...] = jnp.zeros_like(sq_ref)
            gw_ref[...] = jnp.zeros_like(gw_ref)

        def rows(t, live):
            x = h_ref[...]
            rstd = lax.rsqrt(jnp.mean(x * x, axis=-1, keepdims=True) + EPS)
            xhat = x * rstd
            w = w_ref[...]
            err = xhat * w - t
            if live is not None:
                err = jnp.where(live, err, 0.0)
            sq_ref[...] += jnp.sum(err * err, axis=0, keepdims=True)
            dy = err * (1.0 / D)
            gw_ref[...] += jnp.sum(dy * xhat, axis=0, keepdims=True)
            dxh = dy * w
            dh = rstd * (dxh - xhat * jnp.mean(dxh * xhat, axis=-1, keepdims=True))
            dh_ref[...] = dh
            dhb_ref[...] = dh.astype(BF16)

        @pl.when(i == 0)
        def _():
            rid = lax.broadcasted_iota(jnp.int32, (ROWS, D), 0)
            rows(pltpu.roll(t_ref[...], ROW_X, 0), rid >= ROW_X)

        @pl.when(i > 0)
        def _():
            rows(t_ref[...], None)

    row = pl.BlockSpec((ROWS, D), lambda i: (i, 0))
    vec = pl.BlockSpec((1, D), lambda i: (0, 0))
    tgt = pl.BlockSpec((pl.Element(ROWS), pl.Element(D)),
                       lambda i: (pl.multiple_of(jnp.maximum(ROWS * i - ROW_X, 0), 8), 0))
    return pl.pallas_call(
        body, name="loss_head", grid=(LP // ROWS,),
        in_specs=[row, vec, tgt],
        out_specs=[row, row, vec, vec],
        out_shape=[_sds((LP, D), F32), _sds((LP, D), BF16), _sds((1, D), F32), _sds((1, D), F32)],
        compiler_params=_params(("arbitrary",)),
    )(h2, wf, target)


def _dot(a, b, dims):
    return lax.dot_general(a, b, (dims, ((), ())), preferred_element_type=F32)


NN, TN, NT = ((1,), (0,)), ((0,), (0,)), ((1,), (1,))


def _tri_sum(t, x):
    hi = x.astype(BF16)
    r1 = x - hi.astype(F32)
    mid = r1.astype(BF16)
    lo = (r1 - mid.astype(F32)).astype(BF16)
    return _dot(t, hi, NN) + _dot(t, mid, NN) + _dot(t, lo, NN)


def _gla_gates(glr_ref, gw2_ref, gb_ref, rows, row0):
    g_raw = _dot(glr_ref[rows, :].astype(BF16), gw2_ref[...], NN) + gb_ref[...]
    logsig = jnp.minimum(g_raw, 0.0) - jnp.log(1.0 + jnp.exp(-jnp.abs(g_raw)))
    rid = row0 + lax.broadcasted_iota(jnp.int32, g_raw.shape, 0)
    live = rid >= PAD
    return g_raw, jnp.where(live, logsig / TAU, 0.0), live


def _tri_masks():
    r = lax.broadcasted_iota(jnp.int32, (CHUNK, CHUNK), 0)
    c = lax.broadcasted_iota(jnp.int32, (CHUNK, CHUNK), 1)
    return r >= c


def _gla_specs(rev):
    n = NCH // CPS
    R = CPS * CHUNK
    st = (lambda s: n - 1 - s) if rev else (lambda s: s)
    return R, n, st, [
        pl.BlockSpec((R, KW), lambda s: (st(s), 0)),
        pl.BlockSpec((R, KW), lambda s: (st(s), 1)),
        pl.BlockSpec((R, GW), lambda s: (st(s), 1)),
        pl.BlockSpec((R, GW), lambda s: (st(s), 2)),
        pl.BlockSpec((R, 128), lambda s: (st(s), GLR_BLK)),
    ]


def _gla_fwd(proj, gw2p, gate_b, gnw, deps=()):
    R, n, st, pspecs = _gla_specs(False)

    def body(q_ref, k_ref, v_ref, r_ref, glr_ref, gw2_ref, gb_ref, gnw_ref, og_ref, o_ref, st_ref, g_ref, gf_ref, state):
        s = pl.program_id(0)

        @pl.when(s == 0)
        def _():
            state[...] = jnp.zeros_like(state)

        causal = _tri_masks()
        tri = causal.astype(BF16)
        for c in range(CPS):
            rows = slice(c * CHUNK, (c + 1) * CHUNK)
            g_raw, logg, live = _gla_gates(glr_ref, gw2_ref, gb_ref, rows, s * R + c * CHUNK)
            G = _tri_sum(tri, logg)
            g_ref[rows, :] = G
            gf_ref[rows, :] = jnp.where(live, (1.0 / TAU) * jax.nn.sigmoid(-g_raw), 0.0)
            g_last = G[CHUNK - 1:CHUNK, :]
            q_dec = (q_ref[rows, :] * (DK ** -0.5) * jnp.exp(G)).astype(BF16)
            kk = k_ref[rows, :]
            k_inv = (kk * jnp.exp(-G)).astype(BF16)
            k_end = (kk * jnp.exp(g_last - G)).astype(BF16)
            decay = jnp.exp(g_last)
            for h in range(H):
                lk = slice(h * DK, (h + 1) * DK)
                lv = slice(h * DV, (h + 1) * DV)
                v = v_ref[rows, lv].astype(BF16)
                S = state[h]
                st_ref[c, h] = S
                A = jnp.where(causal, _dot(q_dec[:, lk], k_inv[:, lk], NT), 0.0).astype(BF16)
                o = _dot(A, v, NN) + _dot(q_dec[:, lk], S.astype(BF16), NT)
                state[h] = decay[:, lk] * S + _dot(v, k_end[:, lk], TN)
                o_ref[rows, lv] = o
                on = o * lax.rsqrt(jnp.mean(o * o, axis=-1, keepdims=True) + EPS) * gnw_ref[...]
                rr = r_ref[rows, lv]
                og_ref[rows, lv] = (on * (rr * jax.nn.sigmoid(rr))).astype(BF16)

    full = lambda shape: pl.BlockSpec(shape, lambda s: (0,) * len(shape))
    body, dep_ins, dep_specs = _after(body, 8, deps)
    return pl.pallas_call(
        body, name="gla_fwd", grid=(n,),
        in_specs=pspecs + [full((128, KW)), full((1, KW)), full((1, DV))] + dep_specs,
        out_specs=[pl.BlockSpec((R, GW), lambda s: (s, 0)), pl.BlockSpec((R, GW), lambda s: (s, 0)),
                   pl.BlockSpec((CPS, H, DV, DK), lambda s: (s, 0, 0, 0)),
                   pl.BlockSpec((R, KW), lambda s: (s, 0)), pl.BlockSpec((R, KW), lambda s: (s, 0))],
        out_shape=[_sds((LP, GW + PW), BF16), _sds((LP, GW), F32), _sds((NCH, H, DV, DK), F32),
                   _sds((LP, KW), F32), _sds((LP, KW), F32)],
        scratch_shapes=[pltpu.VMEM((H, DV, DK), F32)],
        compiler_params=_params(("arbitrary",)),
    )(proj, proj, proj, proj, proj, gw2p, gate_b, gnw, *dep_ins)


def _gla_bwd(proj, dmixed, o_saved, st_saved, g_saved, gfac, gw2p, gnw, deps=()):
    R, n, st, pspecs = _gla_specs(True)

    def body(q_ref, k_ref, v_ref, r_ref, glr_ref, dog_ref, o_ref, st_ref, g_ref, gf_ref, gw2_ref, gnw_ref,
             dqkvr_ref, dglr_ref, ggn_ref, ggb_ref, ggw_ref, gstate):
        s = pl.program_id(0)

        @pl.when(s == 0)
        def _():
            gstate[...] = jnp.zeros_like(gstate)
            ggn_ref[...] = jnp.zeros_like(ggn_ref)
            ggb_ref[...] = jnp.zeros_like(ggb_ref)
            ggw_ref[...] = jnp.zeros_like(ggw_ref)

        causal = _tri_masks()
        tri_up = (lax.broadcasted_iota(jnp.int32, (CHUNK, CHUNK), 0)
                  <= lax.broadcasted_iota(jnp.int32, (CHUNK, CHUNK), 1)).astype(BF16)
        gnw = gnw_ref[...]
        for c in reversed(range(CPS)):
            rows = slice(c * CHUNK, (c + 1) * CHUNK)
            G = g_ref[rows, :]
            g_last = G[CHUNK - 1:CHUNK, :]
            e_g, e_gi, e_end = jnp.exp(G), jnp.exp(-G), jnp.exp(g_last - G)
            q_dec = q_ref[rows, :] * (DK ** -0.5) * e_g
            kk = k_ref[rows, :]
            k_inv, k_end = kk * e_gi, kk * e_end
            q_dec_b, k_inv_b, k_end_b = q_dec.astype(BF16), k_inv.astype(BF16), k_end.astype(BF16)
            decay = jnp.exp(g_last)
            d_g, d_gl = [], []
            for h in range(H):
                lk = slice(h * DK, (h + 1) * DK)
                lv = slice(h * DV, (h + 1) * DV)
                o = o_ref[rows, lv]
                rr = r_ref[rows, lv]
                dog = dog_ref[rows, lv]
                rstd = lax.rsqrt(jnp.mean(o * o, axis=-1, keepdims=True) + EPS)
                ohat = o * rstd
                sr = jax.nn.sigmoid(rr)
                don = dog * (rr * sr)
                dqkvr_ref[rows, 2 * KW + GW + h * DV:2 * KW + GW + (h + 1) * DV] = (
                    dog * (ohat * gnw) * (sr * (1.0 + rr * (1.0 - sr)))).astype(BF16)
                ggn_ref[...] += jnp.sum(don * ohat, axis=0, keepdims=True)
                dohat = don * gnw
                do = (rstd * (dohat - ohat * jnp.mean(dohat * ohat, axis=-1, keepdims=True))).astype(BF16)
                v = v_ref[rows, lv].astype(BF16)
                S = st_ref[c, h]
                gS = gstate[h]
                S_b, gS_b = S.astype(BF16), gS.astype(BF16)
                qd, ki, ke = q_dec_b[:, lk], k_inv_b[:, lk], k_end_b[:, lk]
                A = jnp.where(causal, _dot(qd, ki, NT), 0.0).astype(BF16)
                dA = jnp.where(causal, _dot(do, v, NT), 0.0).astype(BF16)
                dv = _dot(A, do, TN) + _dot(ke, gS_b, NT)
                dq_dec = _dot(dA, ki, NN) + _dot(do, S_b, NN)
                dk_inv = _dot(dA, qd, TN)
                dk_end = _dot(v, gS_b, NN)
                d_decay = jnp.sum(gS * S, axis=0, keepdims=True)
                gstate[h] = decay[:, lk] * gS + _dot(do, qd, TN)
                dqkvr_ref[rows, lk] = (dq_dec * e_g[:, lk] * (DK ** -0.5)).astype(BF16)
                dqkvr_ref[rows, KW + h * DK:KW + (h + 1) * DK] = (
                    dk_inv * e_gi[:, lk] + dk_end * e_end[:, lk]).astype(BF16)
                dqkvr_ref[rows, 2 * KW + h * DV:2 * KW + (h + 1) * DV] = dv.astype(BF16)
                ke_prod = dk_end * k_end[:, lk]
                d_g.append(dq_dec * q_dec[:, lk] - dk_inv * k_inv[:, lk] - ke_prod)
                d_gl.append(jnp.sum(ke_prod, axis=0, keepdims=True) + d_decay * decay[:, lk])
            dlogg = _tri_sum(tri_up, jnp.concatenate(d_g, axis=1)) + jnp.concatenate(d_gl, axis=1)
            dg_raw = dlogg * gf_ref[rows, :]
            ggb_ref[...] += jnp.sum(dg_raw, axis=0, keepdims=True)
            dg_b = dg_raw.astype(BF16)
            ggw_ref[...] += _dot(glr_ref[rows, :].astype(BF16), dg_b, TN)
            dglr_ref[rows, :] = _dot(dg_b, gw2_ref[...], NT).astype(BF16)

    full = lambda shape: pl.BlockSpec(shape, lambda s: (0,) * len(shape))
    body, dep_ins, dep_specs = _after(body, 12, deps)
    return pl.pallas_call(
        body, name="gla_bwd", grid=(n,),
        in_specs=pspecs + [pl.BlockSpec((R, GW), lambda s: (st(s), 0)), pl.BlockSpec((R, GW), lambda s: (st(s), 0)),
                           pl.BlockSpec((CPS, H, DV, DK), lambda s: (st(s), 0, 0, 0)),
                           pl.BlockSpec((R, KW), lambda s: (st(s), 0)), pl.BlockSpec((R, KW), lambda s: (st(s), 0)),
                           full((128, KW)), full((1, DV))] + dep_specs,
        out_specs=[pl.BlockSpec((R, 2 * KW + 2 * GW), lambda s: (st(s), 0)), pl.BlockSpec((R, 128), lambda s: (st(s), 0)),
                   full((1, DV)), full((1, KW)), full((128, KW))],
        out_shape=[_sds((LP, D_INP), BF16), _sds((LP, 128), BF16),
                   _sds((1, DV), F32), _sds((1, KW), F32), _sds((128, KW), F32)],
        scratch_shapes=[pltpu.VMEM((H, DV, DK), F32)],
        compiler_params=_params(("arbitrary",)),
    )(proj, proj, proj, proj, proj, dmixed, o_saved, st_saved, g_saved, gfac, gw2p, gnw, *dep_ins)


def _pool_pre(x, win, rid):
    s, step = x, 1
    while step < win:
        s = s + pltpu.roll(s, step, 0)
        step *= 2
    cnt = jnp.clip(rid - (PAD - 1), 1, win).astype(F32)
    live = rid >= PAD
    return jnp.where(live, s / cnt - x, 0.0), cnt, live


def _pool_fwd(proj, pool_w, pool_scale, mixed):
    def body(pu_ref, w_ref, sc_ref, _, o_ref):
        rid = lax.broadcasted_iota(jnp.int32, (LP, GC), 0)
        for g, win in enumerate(WINDOWS):
            @pl.when(pl.program_id(0) == g)
            def _():
                y, _, _ = _pool_pre(pu_ref[...], win, rid)
                o_ref[...] = (_dot(y.astype(BF16), w_ref[...], NN) * sc_ref[...]).astype(BF16)

    col = lambda base: pl.BlockSpec((LP, GC), lambda g: (0, base + g))
    return pl.pallas_call(
        body, name="pool_fwd", grid=(len(WINDOWS),),
        in_specs=[col(POOL_BLK), pl.BlockSpec((None, GC, GC), lambda g: (g, 0, 0)),
                  pl.BlockSpec((1, GC), lambda g: (0, g)), ANY_SPEC],
        out_specs=col(GW // GC), out_shape=_sds(mixed.shape, BF16), input_output_aliases={3: 0},
        compiler_params=_params(("parallel",)),
    )(proj, pool_w, pool_scale, mixed)


def _pool_bwd(proj, dmixed, pool_w, pool_scale, dproj):
    def body(pu_ref, do_ref, w_ref, sc_ref, _, dpu_ref, dw_ref, dsc_ref):
        rid = lax.broadcasted_iota(jnp.int32, (LP, GC), 0)
        for g, win in enumerate(WINDOWS):
            @pl.when(pl.program_id(0) == g)
            def _():
                y, cnt, live = _pool_pre(pu_ref[...], win, rid)
                y_b = y.astype(BF16)
                w = w_ref[...]
                do = do_ref[...]
                dsc_ref[...] = jnp.sum(do * _dot(y_b, w, NN), axis=0, keepdims=True)
                dyw = (do * sc_ref[...]).astype(BF16)
                dw_ref[...] = _dot(y_b, dyw, TN)
                dy = jnp.where(live, _dot(dyw, w, NT), 0.0)
                s, step = dy / cnt, 1
                while step < win:
                    s = s + pltpu.roll(s, LP - step, 0)
                    step *= 2
                dpu_ref[...] = (s - dy).astype(BF16)

    col = lambda base: pl.BlockSpec((LP, GC), lambda g: (0, base + g))
    mat = pl.BlockSpec((None, GC, GC), lambda g: (g, 0, 0))
    vec = pl.BlockSpec((1, GC), lambda g: (0, g))
    return pl.pallas_call(
        body, name="pool_bwd", grid=(len(WINDOWS),),
        in_specs=[col(POOL_BLK), col(GW // GC), mat, vec, ANY_SPEC], out_specs=[col(POOL_BLK), mat, vec],
        out_shape=[_sds(dproj.shape, BF16), _sds((4, GC, GC), F32), _sds((1, PW), F32)],
        input_output_aliases={4: 0}, compiler_params=_params(("parallel",)),
    )(proj, dmixed, pool_w, pool_scale, dproj)


def _adamw_math(w, g, m, v):
    m = B1 * m + (1.0 - B1) * g
    v = B2 * v + (1.0 - B2) * (g * g)
    m_hat = m * (1.0 / (1.0 - B1 ** STEP))
    v_hat = v * (1.0 / (1.0 - B2 ** STEP))
    return -LR * (m_hat / (jnp.sqrt(v_hat) + AEPS) + WD * w), m, v


def _adamw_landed(sums, landed, my_chip, w, m, v, rows, name, cols=None):
    _, r, c = w.shape

    def body(chip_ref, s_ref, l_ref, w_ref, m_ref, v_ref, g_ref, d_ref, mo_ref, vo_ref):
        g = s_ref[...].astype(F32)
        for k in range(3):
            g = g + l_ref[k].astype(F32)
        g_ref[...] = g
        d_ref[...], mo_ref[...], vo_ref[...] = _adamw_math(w_ref[...], g, m_ref[...], v_ref[...])

    cols = cols or c
    blk = pl.BlockSpec((None, rows, cols), lambda i, j, chip_ref: (0, i, j))
    return pl.pallas_call(
        body, name=name, out_shape=[_sds((1, r, c), F32)] * 4,
        grid_spec=pltpu.PrefetchScalarGridSpec(
            num_scalar_prefetch=1, grid=(r // rows, c // cols),
            in_specs=[pl.BlockSpec((None, rows, cols), lambda i, j, chip_ref: (chip_ref[0], i, j)),
                      pl.BlockSpec((3, rows, cols), lambda i, j, chip_ref: (0, i, j)), blk, blk, blk],
            out_specs=[blk] * 4),
        compiler_params=_params(("parallel", "parallel")),
    )(my_chip, sums, landed, w, m, v)


def _adamw_side(sums, landed, my_chip, w, m, v, steps):
    _, r, c = w.shape
    rows = r // steps
    assert rows * steps == r and rows % 16 == 0

    def fn(s, l, w_, m_, v_):
        g = s.astype(F32)
        for k in range(3):
            g = g + l[k].astype(F32)
        return [g, *_adamw_math(w_, g, m_, v_)]

    one = lambda t, s: (0, t, 0)
    return _Side(
        my_chip,
        [(sums, (None, rows, c), lambda t, s: (s[0], t, 0)), (landed, (3, rows, c), one)]
        + [(a, (None, rows, c), one) for a in (w, m, v)],
        [((1, r, c), F32, (None, rows, c), one)] * 4, fn)


SMALL_PLACES = (
    ("norm1_w", (1, D), ((0, 0, 1024), (1, 0, 1024))),
    ("norm2_w", (1, D), ((2, 0, 1024), (3, 0, 1024))),
    ("final_norm_w", (1, D), ((4, 0, 1024), (5, 0, 1024))),
    ("pool_scale", (1, PW), ((6, 0, 1024),)),
    ("gate_b", (1, KW), ((7, 0, KW),)),
    ("gla_norm_w", (1, DV), ((7, KW, DV),)),
    ("meta_tokens", (4, 1024), None),
    ("gate_w2", (1, 1024), ((12, 0, 1024),)),
)


def _adamw_small(g, w, m, v):
    n = len(SMALL_PLACES)

    def body(g_ref, w_ref, m_ref, v_ref, *refs):
        outs, buf = refs[:-1], refs[-1]
        gv = g_ref[...]
        for a, val in enumerate((gv, *_adamw_math(w_ref[...], gv, m_ref[...], v_ref[...]))):
            buf[a] = val
            for j, (_, _, pieces) in enumerate(SMALL_PLACES):
                o_ref = outs[a * n + j]
                if pieces is None:
                    o_ref[...] = buf[a, pl.ds(8, 4), :]
                else:
                    o_ref[...] = jnp.concatenate([buf[a, pl.ds(r, 1), pl.ds(l, k)] for r, l, k in pieces], axis=1)

    return pl.pallas_call(
        body, name="adamw_small", out_shape=[_sds(shape, F32) for _ in range(4) for _, shape, _ in SMALL_PLACES],
        scratch_shapes=[pltpu.VMEM((4,) + w.shape, F32)],
    )(g, w, m, v)


SMALL_REPL = (("norm1_w", D), ("norm2_w", D), ("final_norm_w", D), ("pool_scale", PW), ("gate_b", KW),
              ("gla_norm_w", DV))


def _pack_rows(vecs, rows):
    flat = jnp.concatenate([jnp.ravel(v) for v in vecs])
    return jnp.pad(flat, (0, rows * 1024 - flat.shape[0])).reshape(rows, 1024)


def kernel(x, meta_tokens, norm1_w, w_in, gate_w2, gate_b, gla_norm_w, pool_w, pool_scale, w_out, norm2_w, mlp_w1, mlp_w2, final_norm_w, loss_target, m_meta_tokens, m_norm1_w, m_w_in, m_gate_w2, m_gate_b, m_gla_norm_w, m_pool_w, m_pool_scale, m_w_out, m_norm2_w, m_mlp_w1, m_mlp_w2, m_final_norm_w, v_meta_tokens, v_norm1_w, v_w_in, v_gate_w2, v_gate_b, v_gla_norm_w, v_pool_w, v_pool_scale, v_w_out, v_norm2_w, v_mlp_w1, v_mlp_w2, v_final_norm_w):
    W = dict(meta_tokens=meta_tokens, norm1_w=norm1_w, w_in=w_in, gate_w2=gate_w2, gate_b=gate_b,
             gla_norm_w=gla_norm_w, pool_w=pool_w, pool_scale=pool_scale, w_out=w_out, norm2_w=norm2_w,
             mlp_w1=mlp_w1, mlp_w2=mlp_w2, final_norm_w=final_norm_w)
    Mo = dict(meta_tokens=m_meta_tokens, norm1_w=m_norm1_w, w_in=m_w_in, gate_w2=m_gate_w2, gate_b=m_gate_b,
              gla_norm_w=m_gla_norm_w, pool_w=m_pool_w, pool_scale=m_pool_scale, w_out=m_w_out, norm2_w=m_norm2_w,
              mlp_w1=m_mlp_w1, mlp_w2=m_mlp_w2, final_norm_w=m_final_norm_w)
    Vo = dict(meta_tokens=v_meta_tokens, norm1_w=v_norm1_w, w_in=v_w_in, gate_w2=v_gate_w2, gate_b=v_gate_b,
              gla_norm_w=v_gla_norm_w, pool_w=v_pool_w, pool_scale=v_pool_scale, w_out=v_w_out, norm2_w=v_norm2_w,
              mlp_w1=v_mlp_w1, mlp_w2=v_mlp_w2, final_norm_w=v_final_norm_w)

    ex = _Exchange(dict(small=_pack_rows([meta_tokens, gate_w2[0]], 8), w_in=w_in[0].T.astype(BF16)),
                   dict(w_out=w_out[0], pool_w=pool_w[0].reshape(4 * 32, GC), mlp_w1=mlp_w1[0],
                        **{"mlp_w2_%d" % q: mlp_w2[0][:, q * W2_COLS:(q + 1) * W2_COLS] for q in range(W2_PIECES)}))
    tr = lambda a: a[0].T[None]
    win_t, m_win_t, v_win_t = tr(w_in), tr(m_w_in), tr(v_w_in)
    ex.early = [win_t, m_win_t, v_win_t]
    ex.shards = {k: (W[k], Mo[k], Vo[k]) for k in ("w_out", "mlp_w1", "mlp_w2")}
    step = _layer_step(x[0], loss_target[0], ex, norm1_w, gate_b, gla_norm_w, pool_scale, norm2_w,
                       final_norm_w.reshape(1, D))
    grad_x = step["dx"][None]

    last = ex.update("mix", step["dx"])
    out = dict(ex.done)

    loss_part = 0.5 * jnp.sum(step["sq"]) / D
    to_all = _pack_rows([step[k] for k, _ in SMALL_REPL] + [loss_part], 8)
    cols = lambda g: g.reshape(g.shape[0], NDEV, -1).transpose(1, 0, 2).reshape(NDEV, -1, 1024)
    packed = jnp.concatenate([jnp.broadcast_to(to_all, (NDEV, 8, 1024)), cols(step["dhead"][PAD:]),
                              cols(step["gate_w2"][:RANK]), jnp.zeros((NDEV, 3, 1024), F32)], axis=1)
    red = _reduce_small(packed, "reduce_small", deps=[last])
    loss = red[7, 768]

    done = ex.grad_finish("w_in", red)
    poolw3 = lambda a: a.reshape(1, 4 * 32, GC)
    res = _adamw_landed(*done["pool_w"], ex.my_chip, poolw3(pool_w), poolw3(m_pool_w), poolw3(v_pool_w),
                        SHARD_ROWS["pool_w"], "adamw_pool_w")
    out["pool_w"] = [a.reshape(pool_w.shape) for a in res]
    res = _adamw_landed(*done["w_in"], ex.my_chip, win_t, m_win_t, v_win_t, D_IN // NDEV, "adamw_w_in", cols=256)
    out["w_in"] = [a[0].T[None] for a in res]

    def small_pack(P):
        return jnp.concatenate([_pack_rows([P[k] for k, _ in SMALL_REPL], 8),
                                _pack_rows([P["meta_tokens"], P["gate_w2"]], 8)], axis=0)

    res_small = _adamw_small(red.at[7, 768].set(0.0), small_pack(W), small_pack(Mo), small_pack(Vo))
    for j, (k, _, _) in enumerate(SMALL_PLACES):
        out[k] = [res_small[a * len(SMALL_PLACES) + j].reshape(W[k].shape) for a in range(4)]

    order = ["meta_tokens", "norm1_w", "w_in", "gate_w2", "gate_b", "gla_norm_w", "pool_w", "pool_scale", "w_out",
             "norm2_w", "mlp_w1", "mlp_w2", "final_norm_w"]
    return (loss, grad_x, *[out[k][0] for k in order], *[out[k][1] for k in order],
            *[out[k][2] for k in order], *[out[k][3] for k in order])


SHARD_ROWS = dict(w_out=256, mlp_w1=512, mlp_w2=256, pool_w=128)
SLOT_ROWS = D_IN // NDEV


def _w_in_pieces(s):
    lo, hi, out = s * SLOT_ROWS, (s + 1) * SLOT_ROWS, []
    for a, b, shift in ((0, C_GLR, 0), (C_GLR, C_GLR + RANK, PW), (C_GLR + RANK, D_IN, -RANK)):
        a, b = max(a, lo), min(b, hi)
        if a < b:
            out.append((a - lo, a + shift, b - a))
    return out


def _w_in_to_layer_order(gathered):
    def body(g_ref, o_ref):
        for s in range(NDEV):
            @pl.when(pl.program_id(0) == s)
            def _():
                for src, dst, n in _w_in_pieces(s):
                    o_ref[pl.ds(dst, n), :] = g_ref[pl.ds(src, n), :]

        @pl.when(pl.program_id(0) == 0)
        def _():
            o_ref[pl.ds(D_IN, D_INP - D_IN), :] = jnp.zeros((D_INP - D_IN, D), BF16)

    return pl.pallas_call(
        body, name="w_in_rows", grid=(NDEV,), out_shape=_sds((D_INP, D), BF16),
        in_specs=[pl.BlockSpec((None, SLOT_ROWS, D), lambda s: (s, 0, 0))],
        out_specs=pl.BlockSpec((D_INP, D), lambda s: (0, 0)), compiler_params=_params(("arbitrary",)),
    )(gathered)


def _w_in_grad_to_parts(g):
    def body(g_ref, o_ref):
        for s in range(NDEV):
            @pl.when(pl.program_id(0) == s)
            def _():
                for dst, src, n in _w_in_pieces(s):
                    o_ref[pl.ds(dst, n), :] = g_ref[pl.ds(src, n), :]

    return pl.pallas_call(
        body, name="w_in_grad_rows", grid=(NDEV,), out_shape=_sds((2, 4, SLOT_ROWS, D), BF16),
        in_specs=[pl.BlockSpec((D_INP, D), lambda s: (0, 0))],
        out_specs=pl.BlockSpec((None, None, SLOT_ROWS, D), lambda s: (s % 2, s // 2, 0, 0)),
        compiler_params=_params(("arbitrary",)),
    )(g)
C_GLR = 2 * KW + 2 * GW
W2_PIECES = 2
W2_COLS = D // W2_PIECES
GATHER_GROUPS = dict(small=("small",), w_in=("w_in",), mix=("w_out", "pool_w"), up=("mlp_w1",),
                     **{"down_%d" % q: ("mlp_w2_%d" % q,) for q in range(W2_PIECES)})
GRAD_GROUPS = dict(down=("mlp_w2",), up=("mlp_w1",), mix=("w_out",), w_in=("pool_w", "w_in"))
GRAD_IDS = dict(down=(1, 2), up=(3, 4), mix=(5, 6), w_in=(7, 8))


class _Exchange:
    def __init__(self, first, rest):
        head, token = _gather_start(list(first.values()), "gather_start_first", first["small"])
        token, later = lax.optimization_barrier((token, list(rest.values())))
        tail, self.started = _gather_start([v.astype(BF16) for v in later], "gather_start_rest", token)
        self.state = dict(zip(list(first) + list(rest), head + tail))
        self.my_c = lax.axis_index("c").astype(jnp.int32).reshape(1)
        self.my_chip = (2 * lax.axis_index("x") + lax.axis_index("y")).astype(jnp.int32).reshape(1)
        self.sibling, self.chips, self.done = {}, {}, {}

    def forward(self, group, after):
        ks = GATHER_GROUPS[group]
        fwd, token = _gather_forward([self.state[k] for k in ks], after, "gather_forward_" + group)
        self.state.update(zip(ks, fwd))
        return token

    def weights(self, group, after):
        ks = GATHER_GROUPS[group]
        g = dict(zip(ks, _gather_finish([self.state[k] for k in ks], after, "gather_finish_" + group)))
        if group == "w_in":
            return _w_in_to_layer_order(g["w_in"])
        if group == "small":
            return g["small"]
        if group == "mix":
            return (g["w_out"].reshape(D, D),
                    g["pool_w"].reshape(NDEV, 4, 32, GC).transpose(1, 0, 2, 3).reshape(4, GC, GC))
        return g["mlp_w1"] if group == "up" else g[ks[0]].reshape(DFF, W2_COLS)

    def grad(self, group, grads):
        parts = dict(grads)
        if group == "w_in":
            parts["w_in"] = _w_in_grad_to_parts(parts["w_in"])
            parts["pool_w"] = (parts["pool_w"].astype(BF16).reshape(4, 4, 2, 32, GC).transpose(2, 1, 0, 3, 4)
                               .reshape(2, 4, 4 * 32, GC))
        ks = GRAD_GROUPS[group]
        started, token = _to_sibling_start([parts[k] for k in ks], "grad_sibling_start_" + group,
                                           GRAD_IDS[group][0])
        self.sibling[group] = started
        return token

    def grad_mid(self, group, after):
        ks = GRAD_GROUPS[group]
        both = _to_sibling_finish(self.sibling[group], after, "grad_sibling_finish_" + group)
        tile = lambda k, p: (p.shape[2], 512) if k == "w_in" else (SHARD_ROWS[k], p.shape[3])
        sums = [_chip_sum(p, s, self.my_c, tile(k, p), "chip_sum_" + k) for k, (p, s) in zip(ks, both)]
        self.chips[group], token = _to_chips_start(sums, "grad_chips_start_" + group, GRAD_IDS[group][1])
        return token

    def chip_sum_side(self, group, after, steps):
        (parts, from_sibling), = _to_sibling_finish(self.sibling[group], after, "grad_sibling_finish_" + group)
        return _chip_sum_side(parts, from_sibling, self.my_c, steps)

    def adamw_side(self, group, after, steps):
        (k, (sums, landed)), = self.grad_finish(group, after).items()
        self.hosted = k
        return _adamw_side(sums, landed, self.my_chip, *self.shards[k], steps)

    def update(self, group, after):
        for k, (sums, landed) in self.grad_finish(group, after).items():
            self.done[k] = _adamw_landed(sums, landed, self.my_chip, *self.shards[k], SHARD_ROWS[k], "adamw_" + k)
            after = self.done[k][1]
        return after

    def grad_chips(self, group, sums):
        self.chips[group], token = _to_chips_start(sums, "grad_chips_start_" + group, GRAD_IDS[group][1])
        return token

    def grad_finish(self, group, after):
        done = _to_chips_finish(self.chips[group], after, "grad_chips_finish_" + group)
        return dict(zip(GRAD_GROUPS[group], done))


def _layer_step(x, target, ex, norm1_w, gate_b, gla_norm_w, pool_scale, norm2_w, final_norm_w):
    small = ex.weights("small", ex.forward("small", ex.started))
    meta_full = small[:, 0:4].reshape(NDEV, N_META, D // NDEV).transpose(1, 0, 2).reshape(N_META, D)
    gw2_full = small[:, 4].reshape(NDEV, RANK, KW // NDEV).transpose(1, 0, 2).reshape(RANK, KW)
    gw2p = jnp.pad(gw2_full, ((0, 128 - RANK), (0, 0))).astype(BF16)
    h0 = jnp.concatenate([jnp.zeros((PAD, D), F32), meta_full, x], axis=0)
    u1 = _rmsnorm_fwd(h0, norm1_w, "rmsnorm1", deps=ex.early)
    win_p = ex.weights("w_in", ex.forward("w_in", u1))
    proj = _matmul(u1, win_p, mode="nt", tm=1056, tn=1408, tk=2048, name="proj")
    tok = ex.forward("mix", proj)
    mixed, o_saved, st_saved, g_saved, gfac = _gla_fwd(proj, gw2p, gate_b, gla_norm_w, deps=[tok])
    wout_f, poolw_f = ex.weights("mix", mixed)
    mixed = _pool_fwd(proj, poolw_f, pool_scale, mixed)
    h1 = _matmul(mixed, wout_f, mode="nn", tm=1056, tn=1024, tk=2048, name="mix_out", epi="add", extra=h0)
    tok = ex.forward("up", h1)
    u2 = _rmsnorm_fwd(h1, norm2_w, "rmsnorm2", deps=[tok])
    w1_g = ex.weights("up", u2)
    z, act = _matmul(u2, w1_g, mode="nn", tm=1056, tn=1024, tk=2048, name="mlp_up", epi="relu2", b_slots=True)
    w2, h2 = [], None
    for q in range(W2_PIECES):
        w2.append(ex.weights("down_%d" % q, ex.forward("down_%d" % q, act if h2 is None else h2)))
        h2 = _matmul(act, w2[q], mode="nn", tm=1056, tn=W2_COLS, tk=2048, name="mlp_down_%d" % q, epi="add",
                     extra=h1, n_total=D, col_block=q, into=h2)
    dh2, dh2b, sq, g_fnw = _loss_head(h2, final_norm_w, target)

    g_w2 = _matmul(act, dh2b, mode="tn", tm=512, tn=2048, tk=LP, name="d_mlp_w2", out_dtype=BF16, out_slots="rows")
    tok = ex.grad("down", dict(mlp_w2=g_w2))
    dz = _matmul(dh2b, tuple(w2), mode="nt", tm=1056, tn=1024, tk=2048, name="d_act", out_dtype=BF16, epi="dz",
                 extra=z, deps=[tok])
    hosted = lambda res, side: res if side is not None else (res, None)
    side = ex.chip_sum_side("down", dz, 16)
    g_w1, sums = hosted(_matmul(u2, dz, mode="tn", tm=1024, tn=1024, tk=LP, name="d_mlp_w1", out_dtype=BF16,
                                out_slots="cols", side=side), side)
    toks = [ex.grad_chips("down", sums), ex.grad("up", dict(mlp_w1=g_w1))]
    du2 = _matmul(dz, w1_g, mode="nt", tm=1056, tn=1024, tk=2048, name="d_u2", b_slots=True, deps=toks)
    side = ex.chip_sum_side("up", du2, 8)
    dh1, dh1b, g_n2 = _rmsnorm_bwd(h1, norm2_w, du2, dh2, "rmsnorm2_bwd")
    g_wout, sums = hosted(_matmul(mixed, dh1b, mode="tn", tm=256, tn=2048, tk=LP, name="d_w_out", out_dtype=BF16,
                                  out_slots="rows", side=side), side)
    toks = [ex.grad_chips("up", sums), ex.grad("mix", dict(w_out=g_wout))]
    dmixed = _matmul(dh1b, wout_f, mode="nt", tm=1056, tn=1024, tk=2048, name="d_mixed", deps=toks)
    tok = ex.grad_mid("mix", dmixed)
    dproj, dglr, g_gnw, g_gb, g_gw2 = _gla_bwd(proj, dmixed, o_saved, st_saved, g_saved, gfac, gw2p, gla_norm_w,
                                               deps=[tok])
    dproj, g_poolw, g_psc = _pool_bwd(proj, dmixed, poolw_f, pool_scale, dproj)
    dproj = lax.dynamic_update_slice(dproj, dglr, (0, GLR_BLK * 128))
    g_win_p = _matmul(dproj, u1, mode="tn", tm=384, tn=2048, tk=LP, name="d_w_in", out_dtype=BF16)
    tok = ex.grad("w_in", dict(pool_w=g_poolw, w_in=g_win_p))
    tok = ex.grad_mid("w_in", ex.update("down", tok))
    side = ex.adamw_side("up", tok, 16)
    du1, ex.done[ex.hosted] = hosted(_matmul(dproj, win_p, mode="nn", tm=1056, tn=256, tk=D_INP, name="d_u1",
                                             deps=[tok], side=side), side)
    dx, dhead, g_n1 = _rmsnorm_bwd_input(h0, norm1_w, du1, dh1, "rmsnorm1_bwd")
    return dict(dx=dx, dhead=dhead, sq=sq, gate_w2=g_gw2, norm1_w=g_n1, norm2_w=g_n2, final_norm_w=g_fnw, pool_scale=g_psc,
                gate_b=g_gb, gla_norm_w=g_gnw)
```

```python
import jax
import jax.numpy as jnp
from jax import lax
from jax.experimental import pallas as pl
from jax.experimental.pallas import tpu as pltpu

F32, BF16 = jnp.float32, jnp.bfloat16
MESH = pl.DeviceIdType.MESH

NDEV = 8
D = 2048
SEQ = 2048
N_META = 16
CHUNK = 64
PAD = (-N_META) % CHUNK
ROW_X = PAD + N_META
LP = ROW_X + SEQ
NCH = LP // CHUNK
H = 4
DK = 128
DV = 256
KW = H * DK
GW = H * DV
PW = 1024
RANK = 16
TAU = 16.0
WINDOWS = (2, 4, 8, 16)
GC = 256
DFF = 4 * D
EPS = 1e-6
D_IN = 2 * KW + 2 * GW + RANK + PW
D_INP = 4224
GLR_BLK = (2 * KW + 2 * GW + PW) // 128
POOL_BLK = (2 * KW + 2 * GW) // GC
LR, B1, B2, AEPS, WD, STEP = 0.001, 0.9, 0.999, 1e-08, 0.01, 10
VMEM_LIMIT = 48 * 1024 * 1024
CPS = 3


def _params(sem=None):
    return pltpu.CompilerParams(dimension_semantics=sem, vmem_limit_bytes=VMEM_LIMIT)


def _sds(shape, dtype):
    return jax.ShapeDtypeStruct(shape, dtype)


def _me():
    return lax.axis_index("x"), lax.axis_index("y"), lax.axis_index("c")


def _peer(j):
    x, y, c = _me()
    return (x ^ ((j >> 2) & 1), y ^ ((j >> 1) & 1), c ^ (j & 1))


def _slot(dev):
    return 4 * dev[0] + 2 * dev[1] + dev[2]


HBM_SPEC = pl.BlockSpec(memory_space=pltpu.HBM)
SEM_SPEC = pl.BlockSpec(memory_space=pltpu.SEMAPHORE)
ANY_SPEC = pl.BlockSpec(memory_space=pl.ANY)
EFFECT = pltpu.SideEffectType.DATAFLOW_SIDE_EFFECTING
SIBLING = 1
OTHER_CHIPS = (2, 4, 6)


def _in_hbm(a):
    return pltpu.with_memory_space_constraint(a, pltpu.HBM)


def _chip(dev):
    return 2 * dev[0] + dev[1]


def _rcopy(src, dst, send_sem, recv_sem, to):
    return pltpu.make_async_remote_copy(src_ref=src, dst_ref=dst, send_sem=send_sem, recv_sem=recv_sem,
                                        device_id=to, device_id_type=MESH)


def _split_call(body, name, ins, in_specs, out_shape, out_specs, aliases, scratch=(), collective_id=None):
    n = len(ins) + len(out_shape)

    def with_token(*refs):
        body(*refs[:n], *refs[n + 1:])
        refs[n][...] = jnp.zeros_like(refs[n])

    return pl.pallas_call(
        with_token, name=name, in_specs=in_specs, out_shape=list(out_shape) + [_sds((8, 128), F32)],
        out_specs=list(out_specs) + [pl.BlockSpec(memory_space=pltpu.VMEM)],
        input_output_aliases=aliases, scratch_shapes=list(scratch),
        compiler_params=pltpu.CompilerParams(has_side_effects=EFFECT, collective_id=collective_id),
    )(*ins)


def _handshake(relations):
    barrier = pltpu.get_barrier_semaphore()
    for rel in relations:
        pl.semaphore_signal(barrier, inc=1, device_id=_peer(rel), device_id_type=MESH)
    pl.semaphore_wait(barrier, len(relations))


def _after(body, n_in, deps):
    deps = [d for d in deps if d is not None]
    if not deps:
        return body, [], []
    return (lambda *refs: body(*refs[:n_in], *refs[n_in + len(deps):])), deps, [ANY_SPEC] * len(deps)


def _gather_start(shards, name, after):
    n = len(shards)
    me = _slot(_me())
    lands = [lax.dynamic_update_slice(lax.empty((NDEV,) + s.shape, s.dtype), s[None], (me, 0, 0)) for s in shards]

    def body(*refs):
        src, land = refs[:n], refs[n:2 * n]
        outs = refs[2 * n + 1:]
        for i in range(n):
            send_sems, recv_sems = outs[4 * i], outs[4 * i + 1]
            for k, rel in enumerate((SIBLING,) + OTHER_CHIPS):
                _rcopy(src[i], land[i].at[_slot(_me())], send_sems.at[k], recv_sems.at[k], _peer(rel)).start()

    out_shape, out_specs, aliases = [], [], {}
    for i, s in enumerate(shards):
        out_shape += [pltpu.SemaphoreType.DMA((4,)), pltpu.SemaphoreType.DMA((4,)), pltpu.HBM(s.shape, s.dtype),
                      pltpu.HBM((NDEV,) + s.shape, s.dtype)]
        out_specs += [SEM_SPEC, SEM_SPEC, HBM_SPEC, HBM_SPEC]
        aliases[i] = 4 * i + 2
        aliases[n + i] = 4 * i + 3
    res = _split_call(body, name, [_in_hbm(s) for s in shards] + [_in_hbm(l) for l in lands] + [after],
                      [HBM_SPEC] * (2 * n) + [ANY_SPEC], out_shape, out_specs, aliases)
    return [tuple(res[4 * i:4 * i + 4]) for i in range(n)], res[-1]


def _gather_forward(started, after, name):
    n = len(started)

    def body(*refs):
        land, recv1 = refs[:n], refs[n:2 * n]
        outs = refs[2 * n + 1:]
        for i in range(n):
            send2, recv2 = outs[3 * i + 1], outs[3 * i + 2]
            for k, rel in enumerate(OTHER_CHIPS):
                blk = land[i].at[_slot(_peer(rel))]
                _rcopy(blk, blk, send2.at[k], recv1[i].at[1 + k], _peer(rel)).wait_recv()
                _rcopy(blk, blk, send2.at[k], recv2.at[k], _peer(SIBLING)).start()

    ins = [_in_hbm(st[3]) for st in started] + [st[1] for st in started] + [after]
    out_shape, out_specs, aliases = [], [], {}
    for i, st in enumerate(started):
        out_shape += [pltpu.HBM(st[3].shape, st[3].dtype), pltpu.SemaphoreType.DMA((3,)), pltpu.SemaphoreType.DMA((3,))]
        out_specs += [HBM_SPEC, SEM_SPEC, SEM_SPEC]
        aliases[i] = 3 * i
    res = _split_call(body, name, ins, [HBM_SPEC] * n + [SEM_SPEC] * n + [ANY_SPEC], out_shape, out_specs, aliases)
    return [(st[0], st[1], st[2], res[3 * i], res[3 * i + 1], res[3 * i + 2]) for i, st in enumerate(started)], res[-1]


def _gather_finish(forwarded, after, name):
    n = len(forwarded)

    def body(*refs):
        for i in range(n):
            send1, recv1, src, land, send2, recv2 = refs[6 * i:6 * i + 6]
            me = _slot(_me())
            sib = _slot(_peer(SIBLING))
            for k, rel in enumerate((SIBLING,) + OTHER_CHIPS):
                _rcopy(src, land.at[me], send1.at[k], recv1.at[k], _peer(rel)).wait_send()
            _rcopy(src, land.at[sib], send1.at[0], recv1.at[0], _peer(SIBLING)).wait_recv()
            for k, rel in enumerate(OTHER_CHIPS):
                mine, theirs = land.at[_slot(_peer(rel))], land.at[_slot(_peer(rel ^ SIBLING))]
                _rcopy(mine, mine, send2.at[k], recv2.at[k], _peer(SIBLING)).wait_send()
                _rcopy(theirs, theirs, send2.at[k], recv2.at[k], _peer(SIBLING)).wait_recv()

    ins, in_specs, out_shape, aliases = [], [], [], {}
    for i, f in enumerate(forwarded):
        ins += [f[0], f[1], _in_hbm(f[2]), _in_hbm(f[3]), f[4], f[5]]
        in_specs += [SEM_SPEC, SEM_SPEC, HBM_SPEC, HBM_SPEC, SEM_SPEC, SEM_SPEC]
        out_shape.append(pltpu.HBM(f[3].shape, f[3].dtype))
        aliases[6 * i + 3] = i
    res = _split_call(body, name, ins + [after], in_specs + [ANY_SPEC], out_shape, [HBM_SPEC] * n, aliases)
    return list(res[:-1])


def _to_sibling_start(parts, name, collective_id):
    n = len(parts)
    lands = [lax.empty(p.shape[1:], p.dtype) for p in parts]

    def body(*refs):
        _handshake((SIBLING,))
        src, land = refs[:n], refs[n:2 * n]
        outs = refs[2 * n:]
        other = 1 - lax.axis_index("c")
        for i in range(n):
            _rcopy(src[i].at[other], land[i], outs[4 * i], outs[4 * i + 1], _peer(SIBLING)).start()

    out_shape, out_specs, aliases = [], [], {}
    for i, p in enumerate(parts):
        out_shape += [pltpu.SemaphoreType.DMA(()), pltpu.SemaphoreType.DMA(()), pltpu.HBM(p.shape, p.dtype),
                      pltpu.HBM(p.shape[1:], p.dtype)]
        out_specs += [SEM_SPEC, SEM_SPEC, HBM_SPEC, HBM_SPEC]
        aliases[i] = 4 * i + 2
        aliases[n + i] = 4 * i + 3
    res = _split_call(body, name, [_in_hbm(p) for p in parts] + [_in_hbm(l) for l in lands], [HBM_SPEC] * (2 * n),
                      out_shape, out_specs, aliases, collective_id=collective_id)
    return [tuple(res[4 * i:4 * i + 4]) for i in range(n)], res[-1]


def _to_sibling_finish(started, after, name):
    n = len(started)

    def body(*refs):
        for i in range(n):
            send, recv, src, land = refs[4 * i:4 * i + 4]
            cp = _rcopy(src.at[0], land, send, recv, _peer(SIBLING))
            cp.wait_send()
            cp.wait_recv()

    ins, in_specs, out_shape, aliases = [], [], [], {}
    for i, st in enumerate(started):
        ins += [st[0], st[1], _in_hbm(st[2]), _in_hbm(st[3])]
        in_specs += [SEM_SPEC, SEM_SPEC, HBM_SPEC, HBM_SPEC]
        out_shape += [pltpu.HBM(st[2].shape, st[2].dtype), pltpu.HBM(st[3].shape, st[3].dtype)]
        aliases[4 * i + 2] = 2 * i
        aliases[4 * i + 3] = 2 * i + 1
    res = _split_call(body, name, ins + [after], in_specs + [ANY_SPEC], out_shape, [HBM_SPEC] * (2 * n), aliases)
    return [(res[2 * i], res[2 * i + 1]) for i in range(n)]


def _chip_sum(parts, from_sibling, my_c, tile, name):
    _, _, r, c = parts.shape
    tr, tc = tile

    def body(c_ref, p_ref, s_ref, o_ref):
        o_ref[...] = (p_ref[...].astype(F32) + s_ref[...].astype(F32)).astype(o_ref.dtype)

    blk = pl.BlockSpec((4, tr, tc), lambda i, j, c_ref: (0, i, j))
    return pl.pallas_call(
        body, name=name, out_shape=_sds((4, r, c), parts.dtype),
        grid_spec=pltpu.PrefetchScalarGridSpec(
            num_scalar_prefetch=1, grid=(r // tr, c // tc),
            in_specs=[pl.BlockSpec((None, 4, tr, tc), lambda i, j, c_ref: (c_ref[0], 0, i, j)), blk], out_specs=blk),
        compiler_params=_params(("parallel", "parallel")),
    )(my_c, parts, from_sibling)


class _Side:
    def __init__(self, scalar, ins, outs, fn):
        self.scalar, self.ins, self.outs, self.fn = scalar, ins, outs, fn


def _chip_sum_side(parts, from_sibling, my_c, steps):
    _, _, r, c = parts.shape
    rows = r // steps
    assert rows * steps == r and rows % 16 == 0
    return _Side(
        my_c,
        [(parts, (None, 4, rows, c), lambda t, s: (s[0], 0, t, 0)), (from_sibling, (4, rows, c), lambda t, s: (0, t, 0))],
        [((4, r, c), parts.dtype, (4, rows, c), lambda t, s: (0, t, 0))],
        lambda p, q: [(p.astype(F32) + q.astype(F32)).astype(parts.dtype)])


def _to_chips_start(sums, name, collective_id):
    n = len(sums)
    lands = [lax.empty((3,) + s.shape[1:], s.dtype) for s in sums]

    def body(*refs):
        _handshake(OTHER_CHIPS)
        src, land = refs[:n], refs[n:2 * n]
        outs = refs[2 * n:]
        for i in range(n):
            for k, rel in enumerate(OTHER_CHIPS):
                to = _peer(rel)
                _rcopy(src[i].at[_chip(to)], land[i].at[k], outs[4 * i].at[k], outs[4 * i + 1].at[k], to).start()

    out_shape, out_specs, aliases = [], [], {}
    for i, s in enumerate(sums):
        out_shape += [pltpu.SemaphoreType.DMA((3,)), pltpu.SemaphoreType.DMA((3,)), pltpu.HBM(s.shape, s.dtype),
                      pltpu.HBM((3,) + s.shape[1:], s.dtype)]
        out_specs += [SEM_SPEC, SEM_SPEC, HBM_SPEC, HBM_SPEC]
        aliases[i] = 4 * i + 2
        aliases[n + i] = 4 * i + 3
    res = _split_call(body, name, [_in_hbm(s) for s in sums] + [_in_hbm(l) for l in lands], [HBM_SPEC] * (2 * n),
                      out_shape, out_specs, aliases, collective_id=collective_id)
    return [tuple(res[4 * i:4 * i + 4]) for i in range(n)], res[-1]


def _to_chips_finish(started, after, name):
    n = len(started)

    def body(*refs):
        for i in range(n):
            send, recv, src, land = refs[4 * i:4 * i + 4]
            for k, rel in enumerate(OTHER_CHIPS):
                cp = _rcopy(src.at[0], land.at[k], send.at[k], recv.at[k], _peer(rel))
                cp.wait_send()
                cp.wait_recv()

    ins, in_specs, out_shape, aliases = [], [], [], {}
    for i, st in enumerate(started):
        ins += [st[0], st[1], _in_hbm(st[2]), _in_hbm(st[3])]
        in_specs += [SEM_SPEC, SEM_SPEC, HBM_SPEC, HBM_SPEC]
        out_shape += [pltpu.HBM(st[2].shape, st[2].dtype), pltpu.HBM(st[3].shape, st[3].dtype)]
        aliases[4 * i + 2] = 2 * i
        aliases[4 * i + 3] = 2 * i + 1
    res = _split_call(body, name, ins + [after], in_specs + [ANY_SPEC], out_shape, [HBM_SPEC] * (2 * n), aliases)
    return [(res[2 * i], res[2 * i + 1]) for i in range(n)]


def _reduce_small(v, name, deps=()):
    _, r, c = v.shape

    def body(v_ref, o_ref, land, send_sems, recv_sems):
        me = _slot(_me())
        copies = []
        for j in range(1, NDEV):
            to = _peer(j)
            cp = _rcopy(v_ref.at[_slot(to)], land.at[me], send_sems.at[j - 1], recv_sems.at[j - 1], to)
            cp.start()
            copies.append(cp)
        land[me] = v_ref[me]
        for cp in copies:
            cp.wait()
        acc = land[0]
        for k in range(1, NDEV):
            acc = acc + land[k]
        o_ref[...] = acc

    vm = pl.BlockSpec(memory_space=pltpu.VMEM)
    body, dep_ins, dep_specs = _after(body, 1, deps)
    return pl.pallas_call(
        body, name=name, out_shape=_sds((r, c), F32), in_specs=[vm] + dep_specs, out_specs=vm,
        scratch_shapes=[pltpu.VMEM((NDEV, r, c), F32), pltpu.SemaphoreType.DMA((NDEV - 1,)),
                        pltpu.SemaphoreType.DMA((NDEV - 1,))],
        compiler_params=_params(),
    )(v, *dep_ins)


def _matmul(a, b, *, mode, tm, tn, tk, name, out_dtype=F32, epi=None, extra=None, b_slots=False, out_slots=False,
            deps=(), col_block=0, into=None, n_total=None, side=None):
    b_pair = b if isinstance(b, tuple) else None
    if b_pair:
        assert mode == "nt" and tk == len(b) * b[0].shape[1] == a.shape[1] and not b_slots
        b = b[0]
    slot_w = b.shape[-1] if b_slots else None
    if mode == "nn":
        M, K = a.shape
        N = NDEV * slot_w if b_slots else b.shape[1]
    elif mode == "tn":
        K, M = a.shape
        N = b.shape[1]
    else:
        M, K = a.shape
        N = b.shape[-2]
        if b_slots:
            assert K == NDEV * slot_w and tk % slot_w == 0
    if mode == "nn" and b_slots:
        assert tn == slot_w
    if out_slots == "cols":
        assert tn * NDEV == N
    if out_slots == "rows":
        assert (M // NDEV) % tm == 0
    assert M % tm == 0 and N % tn == 0 and K % tk == 0, (name, M, N, K, tm, tn, tk)
    nk = K // tk
    dims = {"nn": ((1,), (0,)), "tn": ((0,), (0,)), "nt": ((1,), (1,))}[mode]

    if mode == "tn":
        a_spec = pl.BlockSpec((tk, tm), lambda i, j, k: (k, i))
    else:
        a_spec = pl.BlockSpec((tm, tk), lambda i, j, k: (i, k))
    if b_pair:
        b_spec = pl.BlockSpec((tn, tk // len(b_pair)), lambda i, j, k: (j, 0))
    elif mode == "nt":
        b_spec = (pl.BlockSpec((tk // slot_w, tn, slot_w), lambda i, j, k: (k, j, 0)) if b_slots
                  else pl.BlockSpec((tn, tk), lambda i, j, k: (j, k)))
    else:
        b_spec = (pl.BlockSpec((None, tk, tn), lambda i, j, k: (j, k, 0)) if b_slots
                  else pl.BlockSpec((tk, tn), lambda i, j, k: (k, j)))
    tile = pl.BlockSpec((tm, tn), lambda i, j, k: (i, j + col_block))
    if out_slots == "cols":
        out_spec = pl.BlockSpec((None, None, tm, tn), lambda i, j, k: (j % 2, j // 2, i, 0))
        out_shape = _sds((2, 4, M, tn), out_dtype)
    elif out_slots == "rows":
        per = M // NDEV // tm
        out_spec = pl.BlockSpec((None, None, tm, tn), lambda i, j, k: ((i // per) % 2, (i // per) // 2, i % per, j))
        out_shape = _sds((2, 4, M // NDEV, N), out_dtype)
    else:
        out_spec, out_shape = tile, _sds((M, n_total or N), out_dtype)
    ins, in_specs = [a, b], [a_spec, b_spec]
    if b_pair:
        ins += list(b_pair[1:])
        in_specs += [b_spec] * (len(b_pair) - 1)
    n_b = len(ins) - 1
    if epi in ("add", "dz"):
        ins.append(extra)
        in_specs.append(tile)
    aliases = {}
    if into is not None:
        aliases[len(ins)] = 0
        ins.append(into)
        in_specs.append(ANY_SPEC)
    if epi == "relu2":
        out_specs, out_shapes = [tile, tile], [_sds((M, N), F32), _sds((M, N), BF16)]
    else:
        out_specs, out_shapes = out_spec, out_shape
    n_in = len(ins)

    def body(*refs):
        outs = refs[n_in:-1] if nk > 1 else refs[n_in:]
        extra_ref = refs[1 + n_b]

        def finish(p):
            if epi is None:
                outs[0][...] = p.astype(out_dtype)
            elif epi == "add":
                outs[0][...] = (p + extra_ref[...]).astype(out_dtype)
            elif epi == "relu2":
                outs[0][...] = p
                rz = jnp.maximum(p, 0.0)
                outs[1][...] = (rz * rz).astype(BF16)
            else:
                outs[0][...] = (p * (2.0 * jnp.maximum(extra_ref[...], 0.0))).astype(out_dtype)

        def product():
            av = refs[0][...].astype(BF16)
            if b_pair:
                w = tk // len(b_pair)
                return sum(lax.dot_general(av[:, s * w:(s + 1) * w], refs[1 + s][...], (dims, ((), ())),
                                           preferred_element_type=F32) for s in range(len(b_pair)))
            if mode == "nt" and b_slots:
                return sum(lax.dot_general(av[:, s * slot_w:(s + 1) * slot_w], refs[1][s], (dims, ((), ())),
                                           preferred_element_type=F32) for s in range(tk // slot_w))
            return lax.dot_general(av, refs[1][...].astype(BF16), (dims, ((), ())), preferred_element_type=F32)

        if nk == 1:
            finish(product())
            return
        acc = refs[-1]
        k = pl.program_id(2)

        @pl.when(k == 0)
        def _():
            acc[...] = jnp.zeros_like(acc)

        acc[...] += product()

        @pl.when(k == nk - 1)
        def _():
            finish(acc[...])

    grid = (M // tm, N // tn, nk)
    scratch = [pltpu.VMEM((tm, tn), F32)] if nk > 1 else []
    if side is None:
        body, dep_ins, dep_specs = _after(body, n_in, deps)
        return pl.pallas_call(
            body, name=name, grid=grid,
            in_specs=in_specs + dep_specs, out_specs=out_specs, out_shape=out_shapes, input_output_aliases=aliases,
            scratch_shapes=scratch, compiler_params=_params(("parallel", "parallel", "arbitrary")),
        )(*ins, *dep_ins)

    deps = [d for d in deps if d is not None]
    step = lambda i, j, k: (i * grid[1] + j) * grid[2] + k
    host = lambda spec: (spec if spec.block_shape is None else
                         pl.BlockSpec(spec.block_shape, lambda i, j, k, s, f=spec.index_map: f(i, j, k)))
    cut = lambda blk, f: pl.BlockSpec(blk, lambda i, j, k, s: f(step(i, j, k), s))
    host_out_specs = list(out_specs) if isinstance(out_specs, list) else [out_specs]
    host_out_shapes = list(out_shapes) if isinstance(out_shapes, list) else [out_shapes]
    n_dep, n_si, n_ho, n_so = len(deps), len(side.ins), len(host_out_specs), len(side.outs)

    def with_side(*refs):
        rest = refs[1:]
        side_in = rest[n_in + n_dep:n_in + n_dep + n_si]
        outs_all = rest[n_in + n_dep + n_si:]
        body(*rest[:n_in], *outs_all[:n_ho], *outs_all[n_ho + n_so:])
        for o_ref, val in zip(outs_all[n_ho:n_ho + n_so], side.fn(*[r[...] for r in side_in])):
            o_ref[...] = val

    res = pl.pallas_call(
        with_side, name=name, input_output_aliases={k + 1: v for k, v in aliases.items()},
        out_shape=host_out_shapes + [_sds(shape, dt) for shape, dt, _, _ in side.outs],
        grid_spec=pltpu.PrefetchScalarGridSpec(
            num_scalar_prefetch=1, grid=grid,
            in_specs=[host(s) for s in in_specs] + [ANY_SPEC] * n_dep + [cut(blk, f) for _, blk, f in side.ins],
            out_specs=[host(s) for s in host_out_specs] + [cut(blk, f) for _, _, blk, f in side.outs],
            scratch_shapes=scratch),
        compiler_params=_params(("parallel", "parallel", "arbitrary")),
    )(side.scalar, *ins, *deps, *[arr for arr, _, _ in side.ins])
    host_res = res[0] if n_ho == 1 else tuple(res[:n_ho])
    return host_res, list(res[n_ho:])


ROWS = 352


def _rmsnorm_fwd(h, w, name, deps=()):
    def body(h_ref, w_ref, u_ref):
        x = h_ref[...]
        rstd = lax.rsqrt(jnp.mean(x * x, axis=-1, keepdims=True) + EPS)
        u_ref[...] = (x * rstd * w_ref[...]).astype(BF16)

    row = pl.BlockSpec((ROWS, D), lambda i: (i, 0))
    body, dep_ins, dep_specs = _after(body, 2, deps)
    return pl.pallas_call(
        body, name=name, grid=(LP // ROWS,), in_specs=[row, pl.BlockSpec((1, D), lambda i: (0, 0))] + dep_specs,
        out_specs=row, out_shape=_sds((LP, D), BF16), compiler_params=_params(("parallel",)),
    )(h, w, *dep_ins)


TOKEN_ROWS = 512


def _rmsnorm_bwd_input(h, w, du, dres, name, deps=()):
    def math(h_ref, w_ref, du_ref, dres_ref):
        x = h_ref[...]
        rstd = lax.rsqrt(jnp.mean(x * x, axis=-1, keepdims=True) + EPS)
        xhat = x * rstd
        dy = du_ref[...]
        dxh = dy * w_ref[...]
        dh = dres_ref[...] + rstd * (dxh - xhat * jnp.mean(dxh * xhat, axis=-1, keepdims=True))
        return dh, jnp.sum(dy * xhat, axis=0, keepdims=True)

    def body(h_ref, w_ref, du_ref, dres_ref, hh_ref, duh_ref, dresh_ref, dx_ref, dhead_ref, gw_ref):
        dx_ref[...], part = math(h_ref, w_ref, du_ref, dres_ref)

        @pl.when(pl.program_id(0) == 0)
        def _():
            dhead_ref[...], head = math(hh_ref, w_ref, duh_ref, dresh_ref)
            gw_ref[...] = part + head

        @pl.when(pl.program_id(0) > 0)
        def _():
            gw_ref[...] += part

    rows = pl.BlockSpec((pl.Element(TOKEN_ROWS), pl.Element(D)),
                        lambda i: (pl.multiple_of(ROW_X + TOKEN_ROWS * i, 8), 0))
    head = pl.BlockSpec((ROW_X, D), lambda i: (0, 0))
    vec = pl.BlockSpec((1, D), lambda i: (0, 0))
    body, dep_ins, dep_specs = _after(body, 7, deps)
    return pl.pallas_call(
        body, name=name, grid=(SEQ // TOKEN_ROWS,),
        in_specs=[rows, vec, rows, rows, head, head, head] + dep_specs,
        out_specs=[pl.BlockSpec((TOKEN_ROWS, D), lambda i: (i, 0)), head, vec],
        out_shape=[_sds((SEQ, D), F32), _sds((ROW_X, D), F32), _sds((1, D), F32)],
        compiler_params=_params(("arbitrary",)),
    )(h, w, du, dres, h, du, dres, *dep_ins)


def _rmsnorm_bwd(h, w, du, dres, name, deps=()):
    def body(h_ref, w_ref, du_ref, dres_ref, dh_ref, dhb_ref, gw_ref):
        x = h_ref[...]
        rstd = lax.rsqrt(jnp.mean(x * x, axis=-1, keepdims=True) + EPS)
        xhat = x * rstd
        dy = du_ref[...]
        dxh = dy * w_ref[...]
        dh = dres_ref[...] + rstd * (dxh - xhat * jnp.mean(dxh * xhat, axis=-1, keepdims=True))
        dh_ref[...] = dh
        dhb_ref[...] = dh.astype(BF16)
        part = jnp.sum(dy * xhat, axis=0, keepdims=True)

        @pl.when(pl.program_id(0) == 0)
        def _():
            gw_ref[...] = part

        @pl.when(pl.program_id(0) > 0)
        def _():
            gw_ref[...] += part

    row = pl.BlockSpec((ROWS, D), lambda i: (i, 0))
    vec = pl.BlockSpec((1, D), lambda i: (0, 0))
    body, dep_ins, dep_specs = _after(body, 4, deps)
    return pl.pallas_call(
        body, name=name, grid=(LP // ROWS,), in_specs=[row, vec, row, row] + dep_specs, out_specs=[row, row, vec],
        out_shape=[_sds((LP, D), F32), _sds((LP, D), BF16), _sds((1, D), F32)],
        compiler_params=_params(("arbitrary",)),
    )(h, w, du, dres, *dep_ins)


def _loss_head(h2, wf, target):
    def body(h_ref, w_ref, t_ref, dh_ref, dhb_ref, sq_ref, gw_ref):
        i = pl.program_id(0)

        @pl.when(i == 0)
        def _():
            sq_ref[...] = jnp.zeros_like(sq_ref)
            gw_ref[...] = jnp.zeros_like(gw_ref)

        def rows(t, live):
            x = h_ref[...]
            rstd = lax.rsqrt(jnp.mean(x * x, axis=-1, keepdims=True) + EPS)
            xhat = x * rstd
            w = w_ref[...]
            err = xhat * w - t
            if live is not None:
                err = jnp.where(live, err, 0.0)
            sq_ref[...] += jnp.sum(err * err, axis=0, keepdims=True)
            dy = err * (1.0 / D)
            gw_ref[...] += jnp.sum(dy * xhat, axis=0, keepdims=True)
            dxh = dy * w
            dh = rstd * (dxh - xhat * jnp.mean(dxh * xhat, axis=-1, keepdims=True))
            dh_ref[...] = dh
            dhb_ref[...] = dh.astype(BF16)

        @pl.when(i == 0)
        def _():
            rid = lax.broadcasted_iota(jnp.int32, (ROWS, D), 0)
            rows(pltpu.roll(t_ref[...], ROW_X, 0), rid >= ROW_X)

        @pl.when(i > 0)
        def _():
            rows(t_ref[...], None)

    row = pl.BlockSpec((ROWS, D), lambda i: (i, 0))
    vec = pl.BlockSpec((1, D), lambda i: (0, 0))
    tgt = pl.BlockSpec((pl.Element(ROWS), pl.Element(D)),
                       lambda i: (pl.multiple_of(jnp.maximum(ROWS * i - ROW_X, 0), 8), 0))
    return pl.pallas_call(
        body, name="loss_head", grid=(LP // ROWS,),
        in_specs=[row, vec, tgt],
        out_specs=[row, row, vec, vec],
        out_shape=[_sds((LP, D), F32), _sds((LP, D), BF16), _sds((1, D), F32), _sds((1, D), F32)],
        compiler_params=_params(("arbitrary",)),
    )(h2, wf, target)


def _dot(a, b, dims):
    return lax.dot_general(a, b, (dims, ((), ())), preferred_element_type=F32)


NN, TN, NT = ((1,), (0,)), ((0,), (0,)), ((1,), (1,))


def _tri_sum(t, x):
    hi = x.astype(BF16)
    r1 = x - hi.astype(F32)
    mid = r1.astype(BF16)
    lo = (r1 - mid.astype(F32)).astype(BF16)
    return _dot(t, hi, NN) + _dot(t, mid, NN) + _dot(t, lo, NN)


def _gla_gates(glr_ref, gw2_ref, gb_ref, rows, row0):
    g_raw = _dot(glr_ref[rows, :].astype(BF16), gw2_ref[...], NN) + gb_ref[...]
    logsig = jnp.minimum(g_raw, 0.0) - jnp.log(1.0 + jnp.exp(-jnp.abs(g_raw)))
    rid = row0 + lax.broadcasted_iota(jnp.int32, g_raw.shape, 0)
    live = rid >= PAD
    return g_raw, jnp.where(live, logsig / TAU, 0.0), live


def _tri_masks():
    r = lax.broadcasted_iota(jnp.int32, (CHUNK, CHUNK), 0)
    c = lax.broadcasted_iota(jnp.int32, (CHUNK, CHUNK), 1)
    return r >= c


def _gla_specs(rev):
    n = NCH // CPS
    R = CPS * CHUNK
    st = (lambda s: n - 1 - s) if rev else (lambda s: s)
    return R, n, st, [
        pl.BlockSpec((R, KW), lambda s: (st(s), 0)),
        pl.BlockSpec((R, KW), lambda s: (st(s), 1)),
        pl.BlockSpec((R, GW), lambda s: (st(s), 1)),
        pl.BlockSpec((R, GW), lambda s: (st(s), 2)),
        pl.BlockSpec((R, 128), lambda s: (st(s), GLR_BLK)),
    ]


def _gla_fwd(proj, gw2p, gate_b, gnw, deps=()):
    R, n, st, pspecs = _gla_specs(False)

    def body(q_ref, k_ref, v_ref, r_ref, glr_ref, gw2_ref, gb_ref, gnw_ref, og_ref, o_ref, st_ref, g_ref, gf_ref, state):
        s = pl.program_id(0)

        @pl.when(s == 0)
        def _():
            state[...] = jnp.zeros_like(state)

        causal = _tri_masks()
        tri = causal.astype(BF16)
        for c in range(CPS):
            rows = slice(c * CHUNK, (c + 1) * CHUNK)
            g_raw, logg, live = _gla_gates(glr_ref, gw2_ref, gb_ref, rows, s * R + c * CHUNK)
            G = _tri_sum(tri, logg)
            g_ref[rows, :] = G
            gf_ref[rows, :] = jnp.where(live, (1.0 / TAU) * jax.nn.sigmoid(-g_raw), 0.0)
            g_last = G[CHUNK - 1:CHUNK, :]
            q_dec = (q_ref[rows, :] * (DK ** -0.5) * jnp.exp(G)).astype(BF16)
            kk = k_ref[rows, :]
            k_inv = (kk * jnp.exp(-G)).astype(BF16)
            k_end = (kk * jnp.exp(g_last - G)).astype(BF16)
            decay = jnp.exp(g_last)
            for h in range(H):
                lk = slice(h * DK, (h + 1) * DK)
                lv = slice(h * DV, (h + 1) * DV)
                v = v_ref[rows, lv].astype(BF16)
                S = state[h]
                st_ref[c, h] = S
                A = jnp.where(causal, _dot(q_dec[:, lk], k_inv[:, lk], NT), 0.0).astype(BF16)
                o = _dot(A, v, NN) + _dot(q_dec[:, lk], S.astype(BF16), NT)
                state[h] = decay[:, lk] * S + _dot(v, k_end[:, lk], TN)
                o_ref[rows, lv] = o
                on = o * lax.rsqrt(jnp.mean(o * o, axis=-1, keepdims=True) + EPS) * gnw_ref[...]
                rr = r_ref[rows, lv]
                og_ref[rows, lv] = (on * (rr * jax.nn.sigmoid(rr))).astype(BF16)

    full = lambda shape: pl.BlockSpec(shape, lambda s: (0,) * len(shape))
    body, dep_ins, dep_specs = _after(body, 8, deps)
    return pl.pallas_call(
        body, name="gla_fwd", grid=(n,),
        in_specs=pspecs + [full((128, KW)), full((1, KW)), full((1, DV))] + dep_specs,
        out_specs=[pl.BlockSpec((R, GW), lambda s: (s, 0)), pl.BlockSpec((R, GW), lambda s: (s, 0)),
                   pl.BlockSpec((CPS, H, DV, DK), lambda s: (s, 0, 0, 0)),
                   pl.BlockSpec((R, KW), lambda s: (s, 0)), pl.BlockSpec((R, KW), lambda s: (s, 0))],
        out_shape=[_sds((LP, GW + PW), BF16), _sds((LP, GW), F32), _sds((NCH, H, DV, DK), F32),
                   _sds((LP, KW), F32), _sds((LP, KW), F32)],
        scratch_shapes=[pltpu.VMEM((H, DV, DK), F32)],
        compiler_params=_params(("arbitrary",)),
    )(proj, proj, proj, proj, proj, gw2p, gate_b, gnw, *dep_ins)


def _gla_bwd(proj, dmixed, o_saved, st_saved, g_saved, gfac, gw2p, gnw, deps=()):
    R, n, st, pspecs = _gla_specs(True)

    def body(q_ref, k_ref, v_ref, r_ref, glr_ref, dog_ref, o_ref, st_ref, g_ref, gf_ref, gw2_ref, gnw_ref,
             dqkvr_ref, dglr_ref, ggn_ref, ggb_ref, ggw_ref, gstate, dg_buf):
        s = pl.program_id(0)

        @pl.when(s == 0)
        def _():
            gstate[...] = jnp.zeros_like(gstate)
            ggn_ref[...] = jnp.zeros_like(ggn_ref)
            ggb_ref[...] = jnp.zeros_like(ggb_ref)
            ggw_ref[...] = jnp.zeros_like(ggw_ref)

        causal = _tri_masks()
        tri_up = (lax.broadcasted_iota(jnp.int32, (CHUNK, CHUNK), 0)
                  <= lax.broadcasted_iota(jnp.int32, (CHUNK, CHUNK), 1)).astype(BF16)
        gnw = gnw_ref[...]
        for c in reversed(range(CPS)):
            rows = slice(c * CHUNK, (c + 1) * CHUNK)
            for h in range(H):
                lk = slice(h * DK, (h + 1) * DK)
                lv = slice(h * DV, (h + 1) * DV)
                G = g_ref[rows, lk]
                g_last = G[CHUNK - 1:CHUNK, :]
                e_g, e_gi, e_end = jnp.exp(G), jnp.exp(-G), jnp.exp(g_last - G)
                q_dec = q_ref[rows, lk] * (DK ** -0.5) * e_g
                kk = k_ref[rows, lk]
                k_inv, k_end = kk * e_gi, kk * e_end
                qd, ki, ke = q_dec.astype(BF16), k_inv.astype(BF16), k_end.astype(BF16)
                decay = jnp.exp(g_last)
                o = o_ref[rows, lv]
                rr = r_ref[rows, lv]
                dog = dog_ref[rows, lv]
                rstd = lax.rsqrt(jnp.mean(o * o, axis=-1, keepdims=True) + EPS)
                ohat = o * rstd
                sr = jax.nn.sigmoid(rr)
                don = dog * (rr * sr)
                dqkvr_ref[rows, 2 * KW + GW + h * DV:2 * KW + GW + (h + 1) * DV] = (
                    dog * (ohat * gnw) * (sr * (1.0 + rr * (1.0 - sr)))).astype(BF16)
                ggn_ref[...] += jnp.sum(don * ohat, axis=0, keepdims=True)
                dohat = don * gnw
                do = (rstd * (dohat - ohat * jnp.mean(dohat * ohat, axis=-1, keepdims=True))).astype(BF16)
                v = v_ref[rows, lv].astype(BF16)
                S = st_ref[c, h]
                gS = gstate[h]
                S_b, gS_b = S.astype(BF16), gS.astype(BF16)
                A =jnp.where(causal, _dot(qd, ki, NT), 0.0).astype(BF16)
                dA = jnp.where(causal, _dot(do, v, NT), 0.0).astype(BF16)
                dv = _dot(A, do, TN) + _dot(ke, gS_b, NT)
                dq_dec = _dot(dA, ki, NN) + _dot(do, S_b, NN)
                dk_inv = _dot(dA, qd, TN)
                dk_end = _dot(v, gS_b, NN)
                d_decay = jnp.sum(gS * S, axis=0, keepdims=True)
                gstate[h] = decay * gS + _dot(do, qd, TN)
                dqkvr_ref[rows, lk] = (dq_dec * e_g * (DK ** -0.5)).astype(BF16)
                dqkvr_ref[rows, KW + h * DK:KW + (h + 1) * DK] = (dk_inv * e_gi + dk_end * e_end).astype(BF16)
                dqkvr_ref[rows, 2 * KW + h * DV:2 * KW + (h + 1) * DV] = dv.astype(BF16)
                ke_prod = dk_end * k_end
                d_g = dq_dec * q_dec - dk_inv * k_inv - ke_prod
                d_gl = jnp.sum(ke_prod, axis=0, keepdims=True) + d_decay * decay
                dg_buf[:, lk] = (_tri_sum(tri_up, d_g) + d_gl) * gf_ref[rows, lk]
            dg_raw = dg_buf[...]
            ggb_ref[...] += jnp.sum(dg_raw, axis=0, keepdims=True)
            dg_b = dg_raw.astype(BF16)
            ggw_ref[...] += _dot(glr_ref[rows, :].astype(BF16), dg_b, TN)
            dglr_ref[rows, :] = _dot(dg_b, gw2_ref[...], NT).astype(BF16)

    full = lambda shape: pl.BlockSpec(shape, lambda s: (0,) * len(shape))
    body, dep_ins, dep_specs = _after(body, 12, deps)
    return pl.pallas_call(
        body, name="gla_bwd", grid=(n,),
        in_specs=pspecs + [pl.BlockSpec((R, GW), lambda s: (st(s), 0)), pl.BlockSpec((R, GW), lambda s: (st(s), 0)),
                           pl.BlockSpec((CPS, H, DV, DK), lambda s: (st(s), 0, 0, 0)),
                           pl.BlockSpec((R, KW), lambda s: (st(s), 0)), pl.BlockSpec((R, KW), lambda s: (st(s), 0)),
                           full((128, KW)), full((1, DV))] + dep_specs,
        out_specs=[pl.BlockSpec((R, 2 * KW + 2 * GW), lambda s: (st(s), 0)), pl.BlockSpec((R, 128), lambda s: (st(s), 0)),
                   full((1, DV)), full((1, KW)), full((128, KW))],
        out_shape=[_sds((LP, D_INP), BF16), _sds((LP, 128), BF16),
                   _sds((1, DV), F32), _sds((1, KW), F32), _sds((128, KW), F32)],
        scratch_shapes=[pltpu.VMEM((H, DV, DK), F32), pltpu.VMEM((CHUNK, KW), F32)],
        compiler_params=_params(("arbitrary",)),
    )(proj, proj, proj, proj, proj, dmixed, o_saved, st_saved, g_saved, gfac, gw2p, gnw, *dep_ins)


def _pool_pre(x, win, rid):
    s, step = x, 1
    while step < win:
        s = s + pltpu.roll(s, step, 0)
        step *= 2
    cnt = jnp.clip(rid - (PAD - 1), 1, win).astype(F32)
    live = rid >= PAD
    return jnp.where(live, s / cnt - x, 0.0), cnt, live


def _pool_fwd(proj, pool_w, pool_scale, mixed):
    def body(pu_ref, w_ref, sc_ref, _, o_ref):
        rid = lax.broadcasted_iota(jnp.int32, (LP, GC), 0)
        for g, win in enumerate(WINDOWS):
            @pl.when(pl.program_id(0) == g)
            def _():
                y, _, _ = _pool_pre(pu_ref[...], win, rid)
                o_ref[...] = (_dot(y.astype(BF16), w_ref[...], NN) * sc_ref[...]).astype(BF16)

    col = lambda base: pl.BlockSpec((LP, GC), lambda g: (0, base + g))
    return pl.pallas_call(
        body, name="pool_fwd", grid=(len(WINDOWS),),
        in_specs=[col(POOL_BLK), pl.BlockSpec((None, GC, GC), lambda g: (g, 0, 0)),
                  pl.BlockSpec((1, GC), lambda g: (0, g)), ANY_SPEC],
        out_specs=col(GW // GC), out_shape=_sds(mixed.shape, BF16), input_output_aliases={3: 0},
        compiler_params=_params(("parallel",)),
    )(proj, pool_w, pool_scale, mixed)


def _pool_bwd(proj, dmixed, pool_w, pool_scale, dproj):
    def body(pu_ref, do_ref, w_ref, sc_ref, _, dpu_ref, dw_ref, dsc_ref):
        rid = lax.broadcasted_iota(jnp.int32, (LP, GC), 0)
        for g, win in enumerate(WINDOWS):
            @pl.when(pl.program_id(0) == g)
            def _():
                y, cnt, live = _pool_pre(pu_ref[...], win, rid)
                y_b = y.astype(BF16)
                w = w_ref[...]
                do = do_ref[...]
                dsc_ref[...] = jnp.sum(do * _dot(y_b, w, NN), axis=0, keepdims=True)
                dyw = (do * sc_ref[...]).astype(BF16)
                dw_ref[...] = _dot(y_b, dyw, TN)
                dy = jnp.where(live, _dot(dyw, w, NT), 0.0)
                s, step = dy / cnt, 1
                while step < win:
                    s = s + pltpu.roll(s, LP - step, 0)
                    step *= 2
                dpu_ref[...] = (s - dy).astype(BF16)

    col = lambda base: pl.BlockSpec((LP, GC), lambda g: (0, base + g))
    mat = pl.BlockSpec((None, GC, GC), lambda g: (g, 0, 0))
    vec = pl.BlockSpec((1, GC), lambda g: (0, g))
    return pl.pallas_call(
        body, name="pool_bwd", grid=(len(WINDOWS),),
        in_specs=[col(POOL_BLK), col(GW // GC), mat, vec, ANY_SPEC], out_specs=[col(POOL_BLK), mat, vec],
        out_shape=[_sds(dproj.shape, BF16), _sds((4, GC, GC), F32), _sds((1, PW), F32)],
        input_output_aliases={4: 0}, compiler_params=_params(("parallel",)),
    )(proj, dmixed, pool_w, pool_scale, dproj)


def _adamw_math(w, g, m, v):
    m = B1 * m + (1.0 - B1) * g
    v = B2 * v + (1.0 - B2) * (g * g)
    m_hat = m * (1.0 / (1.0 - B1 ** STEP))
    v_hat = v * (1.0 / (1.0 - B2 ** STEP))
    return -LR * (m_hat / (jnp.sqrt(v_hat) + AEPS) + WD * w), m, v


def _adamw_landed(sums, landed, my_chip, w, m, v, rows, name, cols=None):
    _, r, c = w.shape

    def body(chip_ref, s_ref, l_ref, w_ref, m_ref, v_ref, g_ref, d_ref, mo_ref, vo_ref):
        g = s_ref[...].astype(F32)
        for k in range(3):
            g = g + l_ref[k].astype(F32)
        g_ref[...] = g
        d_ref[...], mo_ref[...], vo_ref[...] = _adamw_math(w_ref[...], g, m_ref[...], v_ref[...])

    cols = cols or c
    blk = pl.BlockSpec((None, rows, cols), lambda i, j, chip_ref: (0, i, j))
    return pl.pallas_call(
        body, name=name, out_shape=[_sds((1, r, c), F32)] * 4,
        grid_spec=pltpu.PrefetchScalarGridSpec(
            num_scalar_prefetch=1, grid=(r // rows, c // cols),
            in_specs=[pl.BlockSpec((None, rows, cols), lambda i, j, chip_ref: (chip_ref[0], i, j)),
                      pl.BlockSpec((3, rows, cols), lambda i, j, chip_ref: (0, i, j)), blk, blk, blk],
            out_specs=[blk] * 4),
        compiler_params=_params(("parallel", "parallel")),
    )(my_chip, sums, landed, w, m, v)


def _adamw_side(sums, landed, my_chip, w, m, v, steps):
    _, r, c = w.shape
    rows = r // steps
    assert rows * steps == r and rows % 16 == 0

    def fn(s, l, w_, m_, v_):
        g = s.astype(F32)
        for k in range(3):
            g = g + l[k].astype(F32)
        return [g, *_adamw_math(w_, g, m_, v_)]

    one = lambda t, s: (0, t, 0)
    return _Side(
        my_chip,
        [(sums, (None, rows, c), lambda t, s: (s[0], t, 0)), (landed, (3, rows, c), one)]
        + [(a, (None, rows, c), one) for a in (w, m, v)],
        [((1, r, c), F32, (None, rows, c), one)] * 4, fn)


SMALL_PLACES = (
    ("norm1_w", (1, D), ((0, 0, 1024), (1, 0, 1024))),
    ("norm2_w", (1, D), ((2, 0, 1024), (3, 0, 1024))),
    ("final_norm_w", (1, D), ((4, 0, 1024), (5, 0, 1024))),
    ("pool_scale", (1, PW), ((6, 0, 1024),)),
    ("gate_b", (1, KW), ((7, 0, KW),)),
    ("gla_norm_w", (1, DV), ((7, KW, DV),)),
    ("meta_tokens", (4, 1024), None),
    ("gate_w2", (1, 1024), ((12, 0, 1024),)),
)


def _adamw_small(g, w, m, v):
    n = len(SMALL_PLACES)

    def body(g_ref, w_ref, m_ref, v_ref, *refs):
        outs, buf = refs[:-1], refs[-1]
        gv = g_ref[...]
        for a, val in enumerate((gv, *_adamw_math(w_ref[...], gv, m_ref[...], v_ref[...]))):
            buf[a] = val
            for j, (_, _, pieces) in enumerate(SMALL_PLACES):
                o_ref = outs[a * n + j]
                if pieces is None:
                    o_ref[...] = buf[a, pl.ds(8, 4), :]
                else:
                    o_ref[...] = jnp.concatenate([buf[a, pl.ds(r, 1), pl.ds(l, k)] for r, l, k in pieces], axis=1)

    return pl.pallas_call(
        body, name="adamw_small", out_shape=[_sds(shape, F32) for _ in range(4) for _, shape, _ in SMALL_PLACES],
        scratch_shapes=[pltpu.VMEM((4,) + w.shape, F32)],
    )(g, w, m, v)


SMALL_REPL = (("norm1_w", D), ("norm2_w", D), ("final_norm_w", D), ("pool_scale", PW), ("gate_b", KW),
              ("gla_norm_w", DV))


def _pack_rows(vecs, rows):
    flat = jnp.concatenate([jnp.ravel(v) for v in vecs])
    return jnp.pad(flat, (0, rows * 1024 - flat.shape[0])).reshape(rows, 1024)


def kernel(x, meta_tokens, norm1_w, w_in, gate_w2, gate_b, gla_norm_w, pool_w, pool_scale, w_out, norm2_w, mlp_w1, mlp_w2, final_norm_w, loss_target, m_meta_tokens, m_norm1_w, m_w_in, m_gate_w2, m_gate_b, m_gla_norm_w, m_pool_w, m_pool_scale, m_w_out, m_norm2_w, m_mlp_w1, m_mlp_w2, m_final_norm_w, v_meta_tokens, v_norm1_w, v_w_in, v_gate_w2, v_gate_b, v_gla_norm_w, v_pool_w, v_pool_scale, v_w_out, v_norm2_w, v_mlp_w1, v_mlp_w2, v_final_norm_w):
    W = dict(meta_tokens=meta_tokens, norm1_w=norm1_w, w_in=w_in, gate_w2=gate_w2, gate_b=gate_b,
             gla_norm_w=gla_norm_w, pool_w=pool_w, pool_scale=pool_scale, w_out=w_out, norm2_w=norm2_w,
             mlp_w1=mlp_w1, mlp_w2=mlp_w2, final_norm_w=final_norm_w)
    Mo = dict(meta_tokens=m_meta_tokens, norm1_w=m_norm1_w, w_in=m_w_in, gate_w2=m_gate_w2, gate_b=m_gate_b,
              gla_norm_w=m_gla_norm_w, pool_w=m_pool_w, pool_scale=m_pool_scale, w_out=m_w_out, norm2_w=m_norm2_w,
              mlp_w1=m_mlp_w1, mlp_w2=m_mlp_w2, final_norm_w=m_final_norm_w)
    Vo = dict(meta_tokens=v_meta_tokens, norm1_w=v_norm1_w, w_in=v_w_in, gate_w2=v_gate_w2, gate_b=v_gate_b,
              gla_norm_w=v_gla_norm_w, pool_w=v_pool_w, pool_scale=v_pool_scale, w_out=v_w_out, norm2_w=v_norm2_w,
              mlp_w1=v_mlp_w1, mlp_w2=v_mlp_w2, final_norm_w=v_final_norm_w)

    ex = _Exchange(dict(small=_pack_rows([meta_tokens, gate_w2[0]], 8), w_in=w_in[0].T.astype(BF16)),
                   dict(w_out=w_out[0], pool_w=pool_w[0].reshape(4 * 32, GC), mlp_w1=mlp_w1[0],
                        **{"mlp_w2_%d" % q: mlp_w2[0][:, q * W2_COLS:(q + 1) * W2_COLS] for q in range(W2_PIECES)}))
    tr = lambda a: a[0].T[None]
    win_t, m_win_t, v_win_t = tr(w_in), tr(m_w_in), tr(v_w_in)
    ex.early = [win_t, m_win_t, v_win_t]
    ex.shards = {k: (W[k], Mo[k], Vo[k]) for k in ("w_out", "mlp_w1", "mlp_w2")}
    step = _layer_step(x[0], loss_target[0], ex, norm1_w, gate_b, gla_norm_w, pool_scale, norm2_w,
                       final_norm_w.reshape(1, D))
    grad_x = step["dx"][None]

    last = ex.update("mix", step["dx"])
    out = dict(ex.done)

    loss_part = 0.5 * jnp.sum(step["sq"]) / D
    to_all = _pack_rows([step[k] for k, _ in SMALL_REPL] + [loss_part], 8)
    cols = lambda g: g.reshape(g.shape[0], NDEV, -1).transpose(1, 0, 2).reshape(NDEV, -1, 1024)
    packed = jnp.concatenate([jnp.broadcast_to(to_all, (NDEV, 8, 1024)), cols(step["dhead"][PAD:]),
                              cols(step["gate_w2"][:RANK]), jnp.zeros((NDEV, 3, 1024), F32)], axis=1)
    red = _reduce_small(packed, "reduce_small", deps=[last])
    loss = red[7, 768]

    done = ex.grad_finish("w_in", red)
    poolw3 = lambda a: a.reshape(1, 4 * 32, GC)
    res = _adamw_landed(*done["pool_w"], ex.my_chip, poolw3(pool_w), poolw3(m_pool_w), poolw3(v_pool_w),
                        SHARD_ROWS["pool_w"], "adamw_pool_w")
    out["pool_w"] = [a.reshape(pool_w.shape) for a in res]
    res = _adamw_landed(*done["w_in"], ex.my_chip, win_t, m_win_t, v_win_t, D_IN // NDEV, "adamw_w_in", cols=256)
    out["w_in"] = [a[0].T[None] for a in res]

    def small_pack(P):
        return jnp.concatenate([_pack_rows([P[k] for k, _ in SMALL_REPL], 8),
                                _pack_rows([P["meta_tokens"], P["gate_w2"]], 8)], axis=0)

    res_small = _adamw_small(red.at[7, 768].set(0.0), small_pack(W), small_pack(Mo), small_pack(Vo))
    for j, (k, _, _) in enumerate(SMALL_PLACES):
        out[k] = [res_small[a * len(SMALL_PLACES) + j].reshape(W[k].shape) for a in range(4)]

    order = ["meta_tokens", "norm1_w", "w_in", "gate_w2", "gate_b", "gla_norm_w", "pool_w", "pool_scale", "w_out",
             "norm2_w", "mlp_w1", "mlp_w2", "final_norm_w"]
    return (loss, grad_x, *[out[k][0] for k in order], *[out[k][1] for k in order],
            *[out[k][2] for k in order], *[out[k][3] for k in order])


SHARD_ROWS = dict(w_out=256, mlp_w1=512, mlp_w2=256, pool_w=128)
SLOT_ROWS = D_IN // NDEV


def _w_in_pieces(s):
    lo, hi, out = s * SLOT_ROWS, (s + 1) * SLOT_ROWS, []
    for a, b, shift in ((0, C_GLR, 0), (C_GLR, C_GLR + RANK, PW), (C_GLR + RANK, D_IN, -RANK)):
        a, b = max(a, lo), min(b, hi)
        if a < b:
            out.append((a - lo, a + shift, b - a))
    return out


def _w_in_to_layer_order(gathered):
    def body(g_ref, o_ref):
        for s in range(NDEV):
            @pl.when(pl.program_id(0) == s)
            def _():
                for src, dst, n in _w_in_pieces(s):
                    o_ref[pl.ds(dst, n), :] = g_ref[pl.ds(src, n), :]

        @pl.when(pl.program_id(0) == 0)
        def _():
            o_ref[pl.ds(D_IN, D_INP - D_IN), :] = jnp.zeros((D_INP - D_IN, D), BF16)

    return pl.pallas_call(
        body, name="w_in_rows", grid=(NDEV,), out_shape=_sds((D_INP, D), BF16),
        in_specs=[pl.BlockSpec((None, SLOT_ROWS, D), lambda s: (s, 0, 0))],
        out_specs=pl.BlockSpec((D_INP, D), lambda s: (0, 0)), compiler_params=_params(("arbitrary",)),
    )(gathered)


def _w_in_grad_to_parts(g):
    def body(g_ref, o_ref):
        for s in range(NDEV):
            @pl.when(pl.program_id(0) == s)
            def _():
                for dst, src, n in _w_in_pieces(s):
                    o_ref[pl.ds(dst, n), :] = g_ref[pl.ds(src, n), :]

    return pl.pallas_call(
        body, name="w_in_grad_rows", grid=(NDEV,), out_shape=_sds((2, 4, SLOT_ROWS, D), BF16),
        in_specs=[pl.BlockSpec((D_INP, D), lambda s: (0, 0))],
        out_specs=pl.BlockSpec((None, None, SLOT_ROWS, D), lambda s: (s % 2, s // 2, 0, 0)),
        compiler_params=_params(("arbitrary",)),
    )(g)
C_GLR = 2 * KW + 2 * GW
W2_PIECES = 2
W2_COLS = D // W2_PIECES
GATHER_GROUPS = dict(small=("small",), w_in=("w_in",), mix=("w_out", "pool_w"), up=("mlp_w1",),
                     **{"down_%d" % q: ("mlp_w2_%d" % q,) for q in range(W2_PIECES)})
GRAD_GROUPS = dict(down=("mlp_w2",), up=("mlp_w1",), mix=("w_out",), w_in=("pool_w", "w_in"))
GRAD_IDS = dict(down=(1, 2), up=(3, 4), mix=(5, 6), w_in=(7, 8))


class _Exchange:
    def __init__(self, first, rest):
        head, token = _gather_start(list(first.values()), "gather_start_first", first["small"])
        token, later = lax.optimization_barrier((token, list(rest.values())))
        tail, self.started = _gather_start([v.astype(BF16) for v in later], "gather_start_rest", token)
        self.state = dict(zip(list(first) + list(rest), head + tail))
        self.my_c = lax.axis_index("c").astype(jnp.int32).reshape(1)
        self.my_chip = (2 * lax.axis_index("x") + lax.axis_index("y")).astype(jnp.int32).reshape(1)
        self.sibling, self.chips, self.done = {}, {}, {}

    def forward(self, group, after):
        ks = GATHER_GROUPS[group]
        fwd, token = _gather_forward([self.state[k] for k in ks], after, "gather_forward_" + group)
        self.state.update(zip(ks, fwd))
        return token

    def weights(self, group, after):
        ks = GATHER_GROUPS[group]
        g = dict(zip(ks, _gather_finish([self.state[k] for k in ks], after, "gather_finish_" + group)))
        if group == "w_in":
            return _w_in_to_layer_order(g["w_in"])
        if group == "small":
            return g["small"]
        if group == "mix":
            return (g["w_out"].reshape(D, D),
                    g["pool_w"].reshape(NDEV, 4, 32, GC).transpose(1, 0, 2, 3).reshape(4, GC, GC))
        return g["mlp_w1"] if group == "up" else g[ks[0]].reshape(DFF, W2_COLS)

    def grad(self, group, grads):
        parts = dict(grads)
        if group == "w_in":
            parts["w_in"] = _w_in_grad_to_parts(parts["w_in"])
            parts["pool_w"] = (parts["pool_w"].astype(BF16).reshape(4, 4, 2, 32, GC).transpose(2, 1, 0, 3, 4)
                               .reshape(2, 4, 4 * 32, GC))
        ks = GRAD_GROUPS[group]
        started, token = _to_sibling_start([parts[k] for k in ks], "grad_sibling_start_" + group,
                                           GRAD_IDS[group][0])
        self.sibling[group] = started
        return token

    def grad_mid(self, group, after):
        ks = GRAD_GROUPS[group]
        both = _to_sibling_finish(self.sibling[group], after, "grad_sibling_finish_" + group)
        tile = lambda k, p: (p.shape[2], 512) if k == "w_in" else (SHARD_ROWS[k], p.shape[3])
        sums = [_chip_sum(p, s, self.my_c, tile(k, p), "chip_sum_" + k) for k, (p, s) in zip(ks, both)]
        self.chips[group], token = _to_chips_start(sums, "grad_chips_start_" + group, GRAD_IDS[group][1])
        return token

    def chip_sum_side(self, group, after, steps):
        (parts, from_sibling), = _to_sibling_finish(self.sibling[group], after, "grad_sibling_finish_" + group)
        return _chip_sum_side(parts, from_sibling, self.my_c, steps)

    def adamw_side(self, group, after, steps):
        (k, (sums, landed)), = self.grad_finish(group, after).items()
        self.hosted = k
        return _adamw_side(sums, landed, self.my_chip, *self.shards[k], steps)

    def update(self, group, after):
        for k, (sums, landed) in self.grad_finish(group, after).items():
            self.done[k] = _adamw_landed(sums, landed, self.my_chip, *self.shards[k], SHARD_ROWS[k], "adamw_" + k)
            after = self.done[k][1]
        return after

    def grad_chips(self, group, sums):
        self.chips[group], token = _to_chips_start(sums, "grad_chips_start_" + group, GRAD_IDS[group][1])
        return token

    def grad_finish(self, group, after):
        done = _to_chips_finish(self.chips[group], after, "grad_chips_finish_" + group)
        return dict(zip(GRAD_GROUPS[group], done))


def _layer_step(x, target, ex, norm1_w, gate_b, gla_norm_w, pool_scale, norm2_w, final_norm_w):
    small = ex.weights("small", ex.forward("small", ex.started))
    meta_full = small[:, 0:4].reshape(NDEV, N_META, D // NDEV).transpose(1, 0, 2).reshape(N_META, D)
    gw2_full = small[:, 4].reshape(NDEV, RANK, KW // NDEV).transpose(1, 0, 2).reshape(RANK, KW)
    gw2p = jnp.pad(gw2_full, ((0, 128 - RANK), (0, 0))).astype(BF16)
    h0 = jnp.concatenate([jnp.zeros((PAD, D), F32), meta_full, x], axis=0)
    u1 = _rmsnorm_fwd(h0, norm1_w, "rmsnorm1", deps=ex.early)
    win_p = ex.weights("w_in", ex.forward("w_in", u1))
    proj = _matmul(u1, win_p, mode="nt", tm=1056, tn=1408, tk=2048, name="proj")
    tok = ex.forward("mix", proj)
    mixed, o_saved, st_saved, g_saved, gfac = _gla_fwd(proj, gw2p, gate_b, gla_norm_w, deps=[tok])
    wout_f, poolw_f = ex.weights("mix", mixed)
    mixed = _pool_fwd(proj, poolw_f, pool_scale, mixed)
    h1 = _matmul(mixed, wout_f, mode="nn", tm=1056, tn=1024, tk=2048, name="mix_out", epi="add", extra=h0)
    tok = ex.forward("up", h1)
    u2 = _rmsnorm_fwd(h1, norm2_w, "rmsnorm2", deps=[tok])
    w1_g = ex.weights("up", u2)
    z, act = _matmul(u2, w1_g, mode="nn", tm=1056, tn=1024, tk=2048, name="mlp_up", epi="relu2", b_slots=True)
    w2, h2 = [], None
    for q in range(W2_PIECES):
        w2.append(ex.weights("down_%d" % q, ex.forward("down_%d" % q, act if h2 is None else h2)))
        h2 = _matmul(act, w2[q], mode="nn", tm=1056, tn=W2_COLS, tk=2048, name="mlp_down_%d" % q, epi="add",
                     extra=h1, n_total=D, col_block=q, into=h2)
    dh2, dh2b, sq, g_fnw = _loss_head(h2, final_norm_w, target)

    g_w2 = _matmul(act, dh2b, mode="tn", tm=512, tn=2048, tk=LP, name="d_mlp_w2", out_dtype=BF16, out_slots="rows")
    tok = ex.grad("down", dict(mlp_w2=g_w2))
    dz = _matmul(dh2b, tuple(w2), mode="nt", tm=1056, tn=1024, tk=2048, name="d_act", out_dtype=BF16, epi="dz",
                 extra=z, deps=[tok])
    hosted = lambda res, side: res if side is not None else (res, None)
    side = ex.chip_sum_side("down", dz, 16)
    g_w1, sums = hosted(_matmul(u2, dz, mode="tn", tm=1024, tn=1024, tk=LP, name="d_mlp_w1", out_dtype=BF16,
                                out_slots="cols", side=side), side)
    toks = [ex.grad_chips("down", sums), ex.grad("up", dict(mlp_w1=g_w1))]
    du2 = _matmul(dz, w1_g, mode="nt", tm=1056, tn=1024, tk=2048, name="d_u2", b_slots=True, deps=toks)
    side = ex.chip_sum_side("up", du2, 8)
    dh1, dh1b, g_n2 = _rmsnorm_bwd(h1, norm2_w, du2, dh2, "rmsnorm2_bwd")
    g_wout, sums = hosted(_matmul(mixed, dh1b, mode="tn", tm=256, tn=2048, tk=LP, name="d_w_out", out_dtype=BF16,
                                  out_slots="rows", side=side), side)
    toks = [ex.grad_chips("up", sums), ex.grad("mix", dict(w_out=g_wout))]
    dmixed = _matmul(dh1b, wout_f, mode="nt", tm=1056, tn=1024, tk=2048, name="d_mixed", deps=toks)
    tok = ex.grad_mid("mix", dmixed)
    dproj, dglr, g_gnw, g_gb, g_gw2 = _gla_bwd(proj, dmixed, o_saved, st_saved, g_saved, gfac, gw2p, gla_norm_w,
                                               deps=[tok])
    dproj, g_poolw, g_psc = _pool_bwd(proj, dmixed, poolw_f, pool_scale, dproj)
    dproj = lax.dynamic_update_slice(dproj, dglr, (0, GLR_BLK * 128))
    g_win_p = _matmul(dproj, u1, mode="tn", tm=384, tn=2048, tk=LP, name="d_w_in", out_dtype=BF16)
    tok = ex.grad("w_in", dict(pool_w=g_poolw, w_in=g_win_p))
    tok = ex.grad_mid("w_in", ex.update("down", tok))
    side = ex.adamw_side("up", tok, 16)
    du1, ex.done[ex.hosted] = hosted(_matmul(dproj, win_p, mode="nn", tm=1056, tn=256, tk=D_INP, name="d_u1",
                                             deps=[tok], side=side), side)
    dx, dhead, g_n1 = _rmsnorm_bwd_input(h0, norm1_w, du1, dh1, "rmsnorm1_bwd")
    return dict(dx=dx, dhead=dhead, sq=sq, gate_w2=g_gw2, norm1_w=g_n1, norm2_w=g_n2, final_norm_w=g_fnw, pool_scale=g_psc,
                gate_b=g_gb, gla_norm_w=g_gnw)
```
